```python
import jax, jax.numpy as jnp
from jax import lax
import numpy as np

D_MODEL = 1024
BATCH = 8
SEQ = 16384
DEPTH = 4

N_A_LAYERS = DEPTH // 2
N_B_LAYERS = DEPTH - N_A_LAYERS
POOL_WINDOWS = (2, 4, 8, 16)
N_POOL_GROUPS = len(POOL_WINDOWS)
POOL_GROUP = D_MODEL // N_POOL_GROUPS
HEAD_DIM = 64
N_HEADS = D_MODEL // HEAD_DIM
N_KV_HEADS = 4
GQA_GROUP = N_HEADS // N_KV_HEADS
WINDOW = 128
BLOCK = 128
ROPE_THETA = 10000.0
D_FF = ((8 * D_MODEL + 3 * 256 - 1) // (3 * 256)) * 256
PLE_DIM = 256
RMS_EPS = 1e-6
NEG_INF = -1e30

kernel_name = "yoco_pool_swa_sink_hybrid"


def rms_norm(x, g):
    xf = x.astype(jnp.float32)
    y = xf * lax.rsqrt(jnp.mean(xf * xf, axis=-1, keepdims=True) + RMS_EPS)
    return (y * g.astype(jnp.float32)).astype(x.dtype)


def rope_tables(seq):
    inv = 1.0 / (ROPE_THETA ** (jnp.arange(0, HEAD_DIM, 2, dtype=jnp.float32) / HEAD_DIM))
    ang = jnp.arange(seq, dtype=jnp.float32)[:, None] * inv[None, :]
    return jnp.cos(ang), jnp.sin(ang)


def apply_rope(x, cos, sin):
    xf = x.astype(jnp.float32)
    half = HEAD_DIM // 2
    x1, x2 = xf[..., :half], xf[..., half:]
    c = cos[None, :, None, :]
    s = sin[None, :, None, :]
    return jnp.concatenate([x1 * c - x2 * s, x2 * c + x1 * s], axis=-1).astype(x.dtype)


def multiscale_pool(x, w_pool, pool_scale):
    b, s, _ = x.shape
    xf = x.astype(jnp.float32).reshape(b, s, N_POOL_GROUPS, POOL_GROUP)
    cs = jnp.cumsum(xf, axis=1)
    t = jnp.arange(s)
    pooled = []
    for g, w in enumerate(POOL_WINDOWS):
        csg = cs[:, :, g]
        lower = jnp.concatenate([jnp.zeros((b, w, POOL_GROUP), jnp.float32), csg[:, :s - w]], axis=1)
        cnt = jnp.minimum(t + 1, w).astype(jnp.float32)[None, :, None]
        pooled.append((csg - lower) / cnt - xf[:, :, g])
    pooled = jnp.stack(pooled, axis=2).astype(x.dtype)
    y = jnp.einsum('bsgc,gcd->bsgd', pooled, w_pool).reshape(b, s, D_MODEL)
    return y * pool_scale


def sliding_window_sink_attention(q, k, v, sinks):
    b, s = q.shape[0], q.shape[1]
    nb = s // BLOCK
    qb = q.reshape(b, nb, BLOCK, N_KV_HEADS, GQA_GROUP, HEAD_DIM)

    def band(t):
        tb = t.reshape(b, nb, BLOCK, N_KV_HEADS, HEAD_DIM)
        prev = jnp.concatenate([jnp.zeros_like(tb[:, :1]), tb[:, :-1]], axis=1)
        return jnp.concatenate([prev, tb], axis=2)

    kb, vb = band(k), band(v)
    scores = jnp.einsum('bnqkgd,bnskd->bnkgqs', qb, kb).astype(jnp.float32) * (HEAD_DIM ** -0.5)
    blk = jnp.arange(nb)[:, None, None] * BLOCK
    qpos = blk + jnp.arange(BLOCK)[None, :, None]
    kpos = blk - BLOCK + jnp.arange(2 * BLOCK)[None, None, :]
    diff = qpos - kpos
    valid = (diff >= 0) & (diff < WINDOW) & (kpos >= 0)
    scores = jnp.where(valid[None, :, None, None], scores, NEG_INF)
    sink = sinks.astype(jnp.float32).reshape(N_KV_HEADS, GQA_GROUP)[None, None, :, :, None, None]
    sink = jnp.broadcast_to(sink, scores.shape[:-1] + (1,))
    probs = jax.nn.softmax(jnp.concatenate([scores, sink], axis=-1), axis=-1)[..., :-1]
    out = jnp.einsum('bnkgqs,bnskd->bnqkgd', probs.astype(v.dtype), vb)
    return out.reshape(b, s, N_HEADS * HEAD_DIM)


def swiglu(x, w_gate, w_up, w_down):
    return (jax.nn.silu(x @ w_gate) * (x @ w_up)) @ w_down


def _fwd_setup_inputs(seed: int = 0) -> dict:
    key = jax.random.key(seed)
    ks = jax.random.split(key, 20)
    f32 = jnp.float32

    def nrm(k, shape, fan_in):
        return jax.random.normal(k, shape, f32) * (fan_in ** -0.5)

    def gain(k, shape):
        return 1.0 + 0.05 * jax.random.normal(k, shape, f32)

    return {
        "x": jax.random.normal(ks[0], (BATCH, SEQ, D_MODEL), f32),
        "p": jax.random.normal(ks[1], (DEPTH, BATCH, SEQ, PLE_DIM), f32),
        "mix_pre_g": gain(ks[2], (DEPTH, D_MODEL)),
        "mix_post_g": gain(ks[3], (DEPTH, D_MODEL)),
        "ffn_pre_g": gain(ks[4], (DEPTH, D_MODEL)),
        "ffn_post_g": gain(ks[5], (DEPTH, D_MODEL)),
        "pool_w": nrm(ks[6], (N_A_LAYERS, N_POOL_GROUPS, POOL_GROUP, POOL_GROUP), POOL_GROUP),
        "pool_scale": 1.0 + 0.1 * jax.random.normal(ks[7], (N_A_LAYERS, D_MODEL), f32),
        "kv_norm_g": gain(ks[8], (D_MODEL,)),
        "w_k": nrm(ks[9], (D_MODEL, N_KV_HEADS * HEAD_DIM), D_MODEL),
        "w_v": nrm(ks[10], (D_MODEL, N_KV_HEADS * HEAD_DIM), D_MODEL),
        "w_q": nrm(ks[11], (N_B_LAYERS, D_MODEL, N_HEADS * HEAD_DIM), D_MODEL),
        "w_o": nrm(ks[12], (N_B_LAYERS, N_HEADS * HEAD_DIM, D_MODEL), N_HEADS * HEAD_DIM),
        "sinks": 0.5 * jax.random.normal(ks[13], (N_B_LAYERS, N_HEADS), f32),
        "w_ff_gate": nrm(ks[14], (DEPTH, D_MODEL, D_FF), D_MODEL),
        "w_ff_up": nrm(ks[15], (DEPTH, D_MODEL, D_FF), D_MODEL),
        "w_ff_down": nrm(ks[16], (DEPTH, D_FF, D_MODEL), D_FF),
        "ple_norm_g": gain(ks[17], (DEPTH, D_MODEL)),
        "w_ple_gate": nrm(ks[18], (DEPTH, D_MODEL, D_MODEL), D_MODEL),
        "w_ple_proj": nrm(ks[19], (DEPTH, PLE_DIM, D_MODEL), PLE_DIM),
    }


def _fwd_reference(x, p, mix_pre_g, mix_post_g, ffn_pre_g, ffn_post_g, pool_w, pool_scale,
              kv_norm_g, w_k, w_v, w_q, w_o, sinks, w_ff_gate, w_ff_up, w_ff_down,
              ple_norm_g, w_ple_gate, w_ple_proj):
    b, s, _ = x.shape
    cos, sin = rope_tables(s)
    h = x
    k_shared = None
    v_shared = None
    for i in range(DEPTH):
        hn = rms_norm(h, mix_pre_g[i])
        if i < N_A_LAYERS:
            m = multiscale_pool(hn, pool_w[i], pool_scale[i])
        else:
            j = i - N_A_LAYERS
            q = apply_rope((hn @ w_q[j]).reshape(b, s, N_HEADS, HEAD_DIM), cos, sin)
            m = sliding_window_sink_attention(q, k_shared, v_shared, sinks[j]) @ w_o[j]
        h = h + rms_norm(m, mix_post_g[i])
        f = swiglu(rms_norm(h, ffn_pre_g[i]), w_ff_gate[i], w_ff_up[i], w_ff_down[i])
        h = h + rms_norm(f, ffn_post_g[i])
        gate = jax.nn.sigmoid(rms_norm(h, ple_norm_g[i]) @ w_ple_gate[i])
        h = h + (p[i] @ w_ple_proj[i]) * gate
        if i == N_A_LAYERS - 1:
            hk = rms_norm(h, kv_norm_g)
            k_shared = apply_rope((hk @ w_k).reshape(b, s, N_KV_HEADS, HEAD_DIM), cos, sin)
            v_shared = (hk @ w_v).reshape(b, s, N_KV_HEADS, HEAD_DIM)
    return h


import jax as _jax
import jax.numpy as _jnp

TWIN_FORMAT = 'train_step'
FWD_PARAMS = ['x', 'p', 'mix_pre_g', 'mix_post_g', 'ffn_pre_g', 'ffn_post_g', 'pool_w', 'pool_scale', 'kv_norm_g', 'w_k', 'w_v', 'w_q', 'w_o', 'sinks', 'w_ff_gate', 'w_ff_up', 'w_ff_down', 'ple_norm_g', 'w_ple_gate', 'w_ple_proj']
TWIN_WEIGHTS = ['mix_pre_g', 'mix_post_g', 'ffn_pre_g', 'ffn_post_g', 'pool_w', 'pool_scale', 'kv_norm_g', 'w_k', 'w_v', 'w_q', 'w_o', 'sinks', 'w_ff_gate', 'w_ff_up', 'w_ff_down', 'ple_norm_g', 'w_ple_gate', 'w_ple_proj']
TWIN_DIFF_INPUT = 'x'
TWIN_INPUTS = ['x', 'p', 'mix_pre_g', 'mix_post_g', 'ffn_pre_g', 'ffn_post_g', 'pool_w', 'pool_scale', 'kv_norm_g', 'w_k', 'w_v', 'w_q', 'w_o', 'sinks', 'w_ff_gate', 'w_ff_up', 'w_ff_down', 'ple_norm_g', 'w_ple_gate', 'w_ple_proj', 'loss_target', 'm_mix_pre_g', 'm_mix_post_g', 'm_ffn_pre_g', 'm_ffn_post_g', 'm_pool_w', 'm_pool_scale', 'm_kv_norm_g', 'm_w_k', 'm_w_v', 'm_w_q', 'm_w_o', 'm_sinks', 'm_w_ff_gate', 'm_w_ff_up', 'm_w_ff_down', 'm_ple_norm_g', 'm_w_ple_gate', 'm_w_ple_proj', 'v_mix_pre_g', 'v_mix_post_g', 'v_ffn_pre_g', 'v_ffn_post_g', 'v_pool_w', 'v_pool_scale', 'v_kv_norm_g', 'v_w_k', 'v_w_v', 'v_w_q', 'v_w_o', 'v_sinks', 'v_w_ff_gate', 'v_w_ff_up', 'v_w_ff_down', 'v_ple_norm_g', 'v_w_ple_gate', 'v_w_ple_proj']
TWIN_OUTPUTS = ['loss', 'grad_x', 'grad_mix_pre_g', 'grad_mix_post_g', 'grad_ffn_pre_g', 'grad_ffn_post_g', 'grad_pool_w', 'grad_pool_scale', 'grad_kv_norm_g', 'grad_w_k', 'grad_w_v', 'grad_w_q', 'grad_w_o', 'grad_sinks', 'grad_w_ff_gate', 'grad_w_ff_up', 'grad_w_ff_down', 'grad_ple_norm_g', 'grad_w_ple_gate', 'grad_w_ple_proj', 'delta_mix_pre_g', 'delta_mix_post_g', 'delta_ffn_pre_g', 'delta_ffn_post_g', 'delta_pool_w', 'delta_pool_scale', 'delta_kv_norm_g', 'delta_w_k', 'delta_w_v', 'delta_w_q', 'delta_w_o', 'delta_sinks', 'delta_w_ff_gate', 'delta_w_ff_up', 'delta_w_ff_down', 'delta_ple_norm_g', 'delta_w_ple_gate', 'delta_w_ple_proj', 'new_m_mix_pre_g', 'new_m_mix_post_g', 'new_m_ffn_pre_g', 'new_m_ffn_post_g', 'new_m_pool_w', 'new_m_pool_scale', 'new_m_kv_norm_g', 'new_m_w_k', 'new_m_w_v', 'new_m_w_q', 'new_m_w_o', 'new_m_sinks', 'new_m_w_ff_gate', 'new_m_w_ff_up', 'new_m_w_ff_down', 'new_m_ple_norm_g', 'new_m_w_ple_gate', 'new_m_w_ple_proj', 'new_v_mix_pre_g', 'new_v_mix_post_g', 'new_v_ffn_pre_g', 'new_v_ffn_post_g', 'new_v_pool_w', 'new_v_pool_scale', 'new_v_kv_norm_g', 'new_v_w_k', 'new_v_w_v', 'new_v_w_q', 'new_v_w_o', 'new_v_sinks', 'new_v_w_ff_gate', 'new_v_w_ff_up', 'new_v_w_ff_down', 'new_v_ple_norm_g', 'new_v_w_ple_gate', 'new_v_w_ple_proj']
TWIN_LEAF_KINDS = {'loss': 'loss', 'grad_x': 'grad_x', 'grad_mix_pre_g': 'grad_w', 'grad_mix_post_g': 'grad_w', 'grad_ffn_pre_g': 'grad_w', 'grad_ffn_post_g': 'grad_w', 'grad_pool_w': 'grad_w', 'grad_pool_scale': 'grad_w', 'grad_kv_norm_g': 'grad_w', 'grad_w_k': 'grad_w', 'grad_w_v': 'grad_w', 'grad_w_q': 'grad_w', 'grad_w_o': 'grad_w', 'grad_sinks': 'grad_w', 'grad_w_ff_gate': 'grad_w', 'grad_w_ff_up': 'grad_w', 'grad_w_ff_down': 'grad_w', 'grad_ple_norm_g': 'grad_w', 'grad_w_ple_gate': 'grad_w', 'grad_w_ple_proj': 'grad_w', 'delta_mix_pre_g': 'delta_w', 'delta_mix_post_g': 'delta_w', 'delta_ffn_pre_g': 'delta_w', 'delta_ffn_post_g': 'delta_w', 'delta_pool_w': 'delta_w', 'delta_pool_scale': 'delta_w', 'delta_kv_norm_g': 'delta_w', 'delta_w_k': 'delta_w', 'delta_w_v': 'delta_w', 'delta_w_q': 'delta_w', 'delta_w_o': 'delta_w', 'delta_sinks': 'delta_w', 'delta_w_ff_gate': 'delta_w', 'delta_w_ff_up': 'delta_w', 'delta_w_ff_down': 'delta_w', 'delta_ple_norm_g': 'delta_w', 'delta_w_ple_gate': 'delta_w', 'delta_w_ple_proj': 'delta_w', 'new_m_mix_pre_g': 'new_m', 'new_m_mix_post_g': 'new_m', 'new_m_ffn_pre_g': 'new_m', 'new_m_ffn_post_g': 'new_m', 'new_m_pool_w': 'new_m', 'new_m_pool_scale': 'new_m', 'new_m_kv_norm_g': 'new_m', 'new_m_w_k': 'new_m', 'new_m_w_v': 'new_m', 'new_m_w_q': 'new_m', 'new_m_w_o': 'new_m', 'new_m_sinks': 'new_m', 'new_m_w_ff_gate': 'new_m', 'new_m_w_ff_up': 'new_m', 'new_m_w_ff_down': 'new_m', 'new_m_ple_norm_g': 'new_m', 'new_m_w_ple_gate': 'new_m', 'new_m_w_ple_proj': 'new_m', 'new_v_mix_pre_g': 'new_v', 'new_v_mix_post_g': 'new_v', 'new_v_ffn_pre_g': 'new_v', 'new_v_ffn_post_g': 'new_v', 'new_v_pool_w': 'new_v', 'new_v_pool_scale': 'new_v', 'new_v_kv_norm_g': 'new_v', 'new_v_w_k': 'new_v', 'new_v_w_v': 'new_v', 'new_v_w_q': 'new_v', 'new_v_w_o': 'new_v', 'new_v_sinks': 'new_v', 'new_v_w_ff_gate': 'new_v', 'new_v_w_ff_up': 'new_v', 'new_v_w_ff_down': 'new_v', 'new_v_ple_norm_g': 'new_v', 'new_v_w_ple_gate': 'new_v', 'new_v_w_ple_proj': 'new_v'}


def _forward(args):
    return _fwd_reference(*[args[k] for k in FWD_PARAMS])


def _output_shape():
    def fwd():
        inp = _fwd_setup_inputs(0)
        return _fwd_reference(*[inp[k] for k in FWD_PARAMS])
    out = _jax.eval_shape(fwd)
    return out.shape, out.dtype

N_MICROBATCH = 1
ADAM_LR = 0.001
ADAM_B1 = 0.9
ADAM_B2 = 0.999
ADAM_EPS = 1e-08
ADAM_WD = 0.01
ADAM_STEP = 10
PER_EXAMPLE_BATCH_AXIS = {'x': 0, 'p': 1, 'loss_target': 0}
SHARED_INPUTS = []
_WEIGHT_DTYPES = {'mix_pre_g': _jnp.float32, 'mix_post_g': _jnp.float32, 'ffn_pre_g': _jnp.float32, 'ffn_post_g': _jnp.float32, 'pool_w': _jnp.float32, 'pool_scale': _jnp.float32, 'kv_norm_g': _jnp.float32, 'w_k': _jnp.float32, 'w_v': _jnp.float32, 'w_q': _jnp.float32, 'w_o': _jnp.float32, 'sinks': _jnp.float32, 'w_ff_gate': _jnp.float32, 'w_ff_up': _jnp.float32, 'w_ff_down': _jnp.float32, 'ple_norm_g': _jnp.float32, 'w_ple_gate': _jnp.float32, 'w_ple_proj': _jnp.float32}
MOMENT_SCALE = {'mix_pre_g': 6.565965e+00, 'mix_post_g': 1.305026e+02, 'ffn_pre_g': 4.830092e+00, 'ffn_post_g': 1.271836e+02, 'pool_w': 8.331521e+00, 'pool_scale': 1.335702e+01, 'kv_norm_g': 5.163485e+00, 'w_k': 5.417277e+00, 'w_v': 8.166167e+00, 'w_q': 1.919992e+00, 'w_o': 3.397055e+00, 'sinks': 4.778217e-01, 'w_ff_gate': 1.526113e+00, 'w_ff_up': 1.912183e+00, 'w_ff_down': 3.254737e+00, 'ple_norm_g': 3.020119e+00, 'w_ple_gate': 5.547140e-01, 'w_ple_proj': 1.612344e+00}


def _to_microbatches(a, axis):
    t = _jnp.moveaxis(a, axis, 0)
    t = t.reshape((N_MICROBATCH, t.shape[0] // N_MICROBATCH) + t.shape[1:])
    return _jnp.moveaxis(t, 1, axis + 1)


def setup_inputs(seed: int = 0) -> dict:
    inp = _fwd_setup_inputs(seed)
    key = _jax.random.fold_in(_jax.random.key(seed), 7919)
    shape, _ = _output_shape()
    out = dict(inp)
    out["loss_target"] = _jax.random.normal(_jax.random.fold_in(key, 0), shape, _jnp.float32)
    for i, name in enumerate(TWIN_WEIGHTS):
        w = inp[name].astype(_jnp.float32)
        if MOMENT_SCALE is None:
            s = _jnp.sqrt(_jnp.mean(_jnp.square(w)) + 1e-30)
        else:
            s = MOMENT_SCALE[name]
        km, kv = _jax.random.split(_jax.random.fold_in(key, i + 1))
        out[name] = w
        out["m_" + name] = s * _jax.random.normal(km, w.shape, _jnp.float32)
        out["v_" + name] = (s * s) * _jax.random.uniform(kv, w.shape, _jnp.float32, 0.5, 1.5)
    if N_MICROBATCH > 1:
        for name, axis in PER_EXAMPLE_BATCH_AXIS.items():
            out[name] = _to_microbatches(out[name], axis)
    return {'x': out['x'], 'p': out['p'], 'mix_pre_g': out['mix_pre_g'], 'mix_post_g': out['mix_post_g'], 'ffn_pre_g': out['ffn_pre_g'], 'ffn_post_g': out['ffn_post_g'], 'pool_w': out['pool_w'], 'pool_scale': out['pool_scale'], 'kv_norm_g': out['kv_norm_g'], 'w_k': out['w_k'], 'w_v': out['w_v'], 'w_q': out['w_q'], 'w_o': out['w_o'], 'sinks': out['sinks'], 'w_ff_gate': out['w_ff_gate'], 'w_ff_up': out['w_ff_up'], 'w_ff_down': out['w_ff_down'], 'ple_norm_g': out['ple_norm_g'], 'w_ple_gate': out['w_ple_gate'], 'w_ple_proj': out['w_ple_proj'], 'loss_target': out['loss_target'], 'm_mix_pre_g': out['m_mix_pre_g'], 'm_mix_post_g': out['m_mix_post_g'], 'm_ffn_pre_g': out['m_ffn_pre_g'], 'm_ffn_post_g': out['m_ffn_post_g'], 'm_pool_w': out['m_pool_w'], 'm_pool_scale': out['m_pool_scale'], 'm_kv_norm_g': out['m_kv_norm_g'], 'm_w_k': out['m_w_k'], 'm_w_v': out['m_w_v'], 'm_w_q': out['m_w_q'], 'm_w_o': out['m_w_o'], 'm_sinks': out['m_sinks'], 'm_w_ff_gate': out['m_w_ff_gate'], 'm_w_ff_up': out['m_w_ff_up'], 'm_w_ff_down': out['m_w_ff_down'], 'm_ple_norm_g': out['m_ple_norm_g'], 'm_w_ple_gate': out['m_w_ple_gate'], 'm_w_ple_proj': out['m_w_ple_proj'], 'v_mix_pre_g': out['v_mix_pre_g'], 'v_mix_post_g': out['v_mix_post_g'], 'v_ffn_pre_g': out['v_ffn_pre_g'], 'v_ffn_post_g': out['v_ffn_post_g'], 'v_pool_w': out['v_pool_w'], 'v_pool_scale': out['v_pool_scale'], 'v_kv_norm_g': out['v_kv_norm_g'], 'v_w_k': out['v_w_k'], 'v_w_v': out['v_w_v'], 'v_w_q': out['v_w_q'], 'v_w_o': out['v_w_o'], 'v_sinks': out['v_sinks'], 'v_w_ff_gate': out['v_w_ff_gate'], 'v_w_ff_up': out['v_w_ff_up'], 'v_w_ff_down': out['v_w_ff_down'], 'v_ple_norm_g': out['v_ple_norm_g'], 'v_w_ple_gate': out['v_w_ple_gate'], 'v_w_ple_proj': out['v_w_ple_proj']}


def _loss(weights, diff, rest, loss_target):
    with _jax.named_scope("forward"):
        args = {**rest, TWIN_DIFF_INPUT: diff, **{k: w.astype(_WEIGHT_DTYPES[k]) for k, w in weights.items()}}
        y = _forward(args)
    with _jax.named_scope("loss_head"):
        err = _jnp.square(y.astype(_jnp.float32) - loss_target)
        return 0.5 * _jnp.sum(_jnp.mean(err, axis=-1)) if err.ndim else 0.5 * err


def _adamw(w, g, m, v):
    m = ADAM_B1 * m + (1.0 - ADAM_B1) * g
    v = ADAM_B2 * v + (1.0 - ADAM_B2) * _jnp.square(g)
    m_hat = m / (1.0 - ADAM_B1 ** ADAM_STEP)
    v_hat = v / (1.0 - ADAM_B2 ** ADAM_STEP)
    delta = -ADAM_LR * (m_hat / (_jnp.sqrt(v_hat) + ADAM_EPS) + ADAM_WD * w)
    return delta, m, v


def reference(x, p, mix_pre_g, mix_post_g, ffn_pre_g, ffn_post_g, pool_w, pool_scale, kv_norm_g, w_k, w_v, w_q, w_o, sinks, w_ff_gate, w_ff_up, w_ff_down, ple_norm_g, w_ple_gate, w_ple_proj, loss_target, m_mix_pre_g, m_mix_post_g, m_ffn_pre_g, m_ffn_post_g, m_pool_w, m_pool_scale, m_kv_norm_g, m_w_k, m_w_v, m_w_q, m_w_o, m_sinks, m_w_ff_gate, m_w_ff_up, m_w_ff_down, m_ple_norm_g, m_w_ple_gate, m_w_ple_proj, v_mix_pre_g, v_mix_post_g, v_ffn_pre_g, v_ffn_post_g, v_pool_w, v_pool_scale, v_kv_norm_g, v_w_k, v_w_v, v_w_q, v_w_o, v_sinks, v_w_ff_gate, v_w_ff_up, v_w_ff_down, v_ple_norm_g, v_w_ple_gate, v_w_ple_proj):
    given = dict(x=x, p=p, mix_pre_g=mix_pre_g, mix_post_g=mix_post_g, ffn_pre_g=ffn_pre_g, ffn_post_g=ffn_post_g, pool_w=pool_w, pool_scale=pool_scale, kv_norm_g=kv_norm_g, w_k=w_k, w_v=w_v, w_q=w_q, w_o=w_o, sinks=sinks, w_ff_gate=w_ff_gate, w_ff_up=w_ff_up, w_ff_down=w_ff_down, ple_norm_g=ple_norm_g, w_ple_gate=w_ple_gate, w_ple_proj=w_ple_proj, loss_target=loss_target, m_mix_pre_g=m_mix_pre_g, m_mix_post_g=m_mix_post_g, m_ffn_pre_g=m_ffn_pre_g, m_ffn_post_g=m_ffn_post_g, m_pool_w=m_pool_w, m_pool_scale=m_pool_scale, m_kv_norm_g=m_kv_norm_g, m_w_k=m_w_k, m_w_v=m_w_v, m_w_q=m_w_q, m_w_o=m_w_o, m_sinks=m_sinks, m_w_ff_gate=m_w_ff_gate, m_w_ff_up=m_w_ff_up, m_w_ff_down=m_w_ff_down, m_ple_norm_g=m_ple_norm_g, m_w_ple_gate=m_w_ple_gate, m_w_ple_proj=m_w_ple_proj, v_mix_pre_g=v_mix_pre_g, v_mix_post_g=v_mix_post_g, v_ffn_pre_g=v_ffn_pre_g, v_ffn_post_g=v_ffn_post_g, v_pool_w=v_pool_w, v_pool_scale=v_pool_scale, v_kv_norm_g=v_kv_norm_g, v_w_k=v_w_k, v_w_v=v_w_v, v_w_q=v_w_q, v_w_o=v_w_o, v_sinks=v_sinks, v_w_ff_gate=v_w_ff_gate, v_w_ff_up=v_w_ff_up, v_w_ff_down=v_w_ff_down, v_ple_norm_g=v_ple_norm_g, v_w_ple_gate=v_w_ple_gate, v_w_ple_proj=v_w_ple_proj)
    weights = {n: given[n] for n in TWIN_WEIGHTS}
    shared = {n: given[n] for n in SHARED_INPUTS}
    per_example = {n: given[n] for n in ['x', 'p']}
    grad_fn = _jax.value_and_grad(_loss, argnums=(0, 1))

    def one_microbatch(ex, loss_target):
        ex = dict(ex)
        diff = ex.pop(TWIN_DIFF_INPUT)
        return grad_fn(weights, diff, {**shared, **ex}, loss_target)

    if N_MICROBATCH == 1:
        loss, (grad_w, grad_x) = one_microbatch(per_example, given["loss_target"])
    else:
        def body(carry, xs):
            loss_sum, grad_sum = carry
            l_k, (gw_k, gx_k) = one_microbatch(xs[0], xs[1])
            with _jax.named_scope("update"):
                return (loss_sum + l_k, _jax.tree.map(_jnp.add, grad_sum, gw_k)), gx_k

        init = (_jnp.zeros((), _jnp.float32), _jax.tree.map(_jnp.zeros_like, weights))
        (loss, grad_w), grad_x = _jax.lax.scan(body, init, (per_example, given["loss_target"]))
    with _jax.named_scope("update"):
        delta_w, new_m, new_v = {}, {}, {}
        for n in TWIN_WEIGHTS:
            delta_w[n], new_m[n], new_v[n] = _adamw(weights[n], grad_w[n], given["m_" + n], given["v_" + n])
    return (loss, grad_x, *[grad_w[n] for n in TWIN_WEIGHTS], *[delta_w[n] for n in TWIN_WEIGHTS],
            *[new_m[n] for n in TWIN_WEIGHTS], *[new_v[n] for n in TWIN_WEIGHTS])
```

```python
import functools

import jax
import jax.numpy as jnp
from jax import lax
from jax.experimental import pallas as pl
from jax.experimental.pallas import tpu as pltpu

F32 = jnp.float32
BF16 = jnp.bfloat16

N_DEV = 8
HEAD_DIM = 64
N_HEADS = 16
N_KV_HEADS = 4
GQA = N_HEADS // N_KV_HEADS
BLOCK = 128
POOL_WINDOWS = (2, 4, 8, 16)
POOL_GROUP = 256
HALO = 16
ROPE_THETA = 10000.0
RMS_EPS = 1e-6
NEG_INF = -1e30
LANES = 128
FFN_CHUNK = 768
VMEM_LIMIT = 56 * 1024 * 1024

ADAM_LR = 0.001
ADAM_B1 = 0.9
ADAM_B2 = 0.999
ADAM_EPS = 1e-08
ADAM_WD = 0.01
ADAM_STEP = 10

MESH = pl.DeviceIdType.MESH
ANY = pl.BlockSpec(memory_space=pl.ANY)

NT_DIMS = (((1,), (1,)), ((), ()))
TN_DIMS = (((0,), (0,)), ((), ()))


def _cparams(sem=None, vmem=None):
    kw = {}
    if sem is not None:
        kw["dimension_semantics"] = sem
    if vmem is not None:
        kw["vmem_limit_bytes"] = vmem
    return pltpu.CompilerParams(**kw)


def _rows(tm, n):
    return pl.BlockSpec((tm, n), lambda i: (i, 0))


def _rows_rev(tm, n, nt):
    return pl.BlockSpec((tm, n), lambda i: (nt - 1 - i, 0))


def _const(shape):
    nd = len(shape)
    return pl.BlockSpec(shape, lambda *_: (0,) * nd, pipeline_mode=pl.Buffered(1))


def _resident(shape):
    nd = len(shape)
    return pl.BlockSpec(shape, lambda *_: (0,) * nd)


def _tile_rows(t):
    return 512 if t % 512 == 0 else 128


def _dot(a, b):
    return jnp.dot(a, b, preferred_element_type=F32)


def _dot_nt(a, b):
    return lax.dot_general(a, b, NT_DIMS, preferred_element_type=F32)


def _dot_tn(a, b):
    return lax.dot_general(a, b, TN_DIMS, preferred_element_type=F32)


def _rms_r(x):
    return lax.rsqrt(jnp.mean(x * x, axis=-1, keepdims=True) + RMS_EPS)


def _rms_bwd(x, r, g, dy):
    gy = dy * g
    dx = r * gy - x * (r * r * r * jnp.mean(gy * x, axis=-1, keepdims=True))
    dg = jnp.sum(dy * (x * r), axis=0, keepdims=True)
    return dx, dg


def _sigmoid(x):
    return jax.nn.sigmoid(x)


def _rope_tables(t):
    inv = 1.0 / (ROPE_THETA ** (jnp.arange(0, HEAD_DIM, 2, dtype=F32) / HEAD_DIM))
    ang = jnp.arange(t, dtype=F32)[:, None] * inv[None, :]
    c, s = jnp.cos(ang), jnp.sin(ang)
    cos = jnp.concatenate([c, c, c, c], axis=1)
    sin = jnp.concatenate([-s, s, -s, s], axis=1)
    return cos, sin


def _swap_halves(x):
    n = x.shape[1]
    lane = lax.broadcasted_iota(jnp.int32, x.shape, 1)
    first = (lane % HEAD_DIM) < (HEAD_DIM // 2)
    return jnp.where(first, pltpu.roll(x, n - HEAD_DIM // 2, 1), pltpu.roll(x, HEAD_DIM // 2, 1))


def _rope(x, cos, sin):
    reps = x.shape[1] // LANES
    return x * jnp.tile(cos, (1, reps)) + _swap_halves(x) * jnp.tile(sin, (1, reps))


def _unrope(dy, cos, sin):
    reps = dy.shape[1] // LANES
    return dy * jnp.tile(cos, (1, reps)) + _swap_halves(dy * jnp.tile(sin, (1, reps)))


def _acc_init(acc_ref):
    @pl.when(pl.program_id(0) == 0)
    def _():
        acc_ref[...] = jnp.zeros_like(acc_ref)


def _window_sums(ext, tm, forward):
    n = tm + HALO
    out = []
    for g, w in enumerate(POOL_WINDOWS):
        s = ext[:, g * POOL_GROUP:(g + 1) * POOL_GROUP]
        k = 1
        while k < w:
            s = s + pltpu.roll(s, k if forward else n - k, 0)
            k *= 2
        out.append(s[HALO:, :] if forward else s[:tm, :])
    return out


def _pool_counts(tile, tm):
    t = tile * tm + lax.broadcasted_iota(jnp.int32, (tm, 1), 0)
    return [jnp.minimum(t + 1, w).astype(F32) for w in POOL_WINDOWS]


def _pool_mix(hn, ext, cnts, pw_ref, scale, tm):
    sums = _window_sums(ext, tm, True)
    pooled, ys = [], []
    for g in range(len(POOL_WINDOWS)):
        pg = (sums[g] / cnts[g] - hn[:, g * POOL_GROUP:(g + 1) * POOL_GROUP]).astype(BF16)
        pooled.append(pg)
        ys.append(_dot(pg, pw_ref[g]))
    y = jnp.concatenate(ys, axis=1)
    return pooled, y, y * scale


def pool_mix_fwd(h0, gpre, pool_w, scale, gpost, gffn):
    t, d = h0.shape
    tm = _tile_rows(t)

    def body(h_ref, gpre_ref, pw_ref, scale_ref, gpost_ref, gffn_ref, h1_ref, a_ref, carry):
        i = pl.program_id(0)

        @pl.when(i == 0)
        def _():
            carry[...] = jnp.zeros_like(carry)

        x = h_ref[...]
        hn = x * _rms_r(x) * gpre_ref[...]
        ext = jnp.concatenate([carry[...], hn], axis=0)
        carry[...] = hn[tm - HALO:, :]
        _, _, m = _pool_mix(hn, ext, _pool_counts(i, tm), pw_ref, scale_ref[...], tm)
        h1 = x + m * _rms_r(m) * gpost_ref[...]
        h1_ref[...] = h1
        a_ref[...] = (h1 * _rms_r(h1) * gffn_ref[...]).astype(BF16)

    return pl.pallas_call(
        functools.partial(body), name="pool_mix_fwd", grid=(t // tm,),
        in_specs=[_rows(tm, d), _const((1, d)), _const(pool_w.shape), _const((1, d)), _const((1, d)), _const((1, d))],
        out_specs=[_rows(tm, d), _rows(tm, d)],
        out_shape=[jax.ShapeDtypeStruct((t, d), F32), jax.ShapeDtypeStruct((t, d), BF16)],
        scratch_shapes=[pltpu.VMEM((HALO, d), F32)],
        compiler_params=_cparams(("arbitrary",), VMEM_LIMIT),
    )(h0, gpre, pool_w, scale, gpost, gffn)


def pool_mix_bwd(h0, dh2, da, gpre, pool_w, scale, gpost, gffn):
    t, d = h0.shape
    tm = _tile_rows(t)
    nt = t // tm
    hb = tm // HALO

    def body(h_ref, halo_ref, dh2_ref, da_ref, gpre_ref, pw_ref, scale_ref, gpost_ref, gffn_ref,
             dh0_ref, dpw_ref, gacc_ref, carry):
        i = pl.program_id(0)
        tile = nt - 1 - i
        _acc_init(gacc_ref)
        _acc_init(dpw_ref)

        @pl.when(i == 0)
        def _():
            carry[...] = jnp.zeros_like(carry)

        x = h_ref[...]
        gpre_v, scale_v, gpost_v, gffn_v = gpre_ref[...], scale_ref[...], gpost_ref[...], gffn_ref[...]
        r0 = _rms_r(x)
        hn = x * r0 * gpre_v
        xh = halo_ref[...]
        hn_halo = jnp.where(tile > 0, xh * _rms_r(xh) * gpre_v, 0.0)
        ext = jnp.concatenate([hn_halo, hn], axis=0)
        cnts = _pool_counts(tile, tm)
        pooled, y, m = _pool_mix(hn, ext, cnts, pw_ref, scale_v, tm)
        rm = _rms_r(m)
        h1 = x + m * rm * gpost_v
        dh1_n, dgffn = _rms_bwd(h1, _rms_r(h1), gffn_v, da_ref[...])
        dh1 = dh2_ref[...] + dh1_n
        dm, dgpost = _rms_bwd(m, rm, gpost_v, dh1)
        dscale = jnp.sum(dm * y, axis=0, keepdims=True)
        dy = (dm * scale_v).astype(BF16)
        dpn = []
        for g in range(len(POOL_WINDOWS)):
            dyg = dy[:, g * POOL_GROUP:(g + 1) * POOL_GROUP]
            dpw_ref[g] += _dot_tn(pooled[g], dyg)
            dpn.append(_dot_nt(dyg, pw_ref[g]))
        dpooled = jnp.concatenate(dpn, axis=1)
        dpc = jnp.concatenate([dpn[g] / cnts[g] for g in range(len(POOL_WINDOWS))], axis=1)
        ext2 = jnp.concatenate([dpc, carry[...]], axis=0)
        carry[...] = dpc[:HALO, :]
        dhn = jnp.concatenate(_window_sums(ext2, tm, False), axis=1) - dpooled
        dh0_n, dgpre = _rms_bwd(x, r0, gpre_v, dhn)
        dh0_ref[...] = dh1 + dh0_n
        gacc_ref[0:1, :] += dgpre
        gacc_ref[1:2, :] += dgpost
        gacc_ref[2:3, :] += dgffn
        gacc_ref[3:4, :] += dscale

    return pl.pallas_call(
        functools.partial(body), name="pool_mix_bwd", grid=(nt,),
        in_specs=[_rows_rev(tm, d, nt),
                  pl.BlockSpec((HALO, d), lambda i: (jnp.maximum((nt - 1 - i) * hb - 1, 0), 0)),
                  _rows_rev(tm, d, nt), _rows_rev(tm, d, nt),
                  _const((1, d)), _const(pool_w.shape), _const((1, d)), _const((1, d)), _const((1, d))],
        out_specs=[_rows_rev(tm, d, nt), _resident(pool_w.shape), _resident((8, d))],
        out_shape=[jax.ShapeDtypeStruct((t, d), F32), jax.ShapeDtypeStruct(pool_w.shape, F32),
                   jax.ShapeDtypeStruct((8, d), F32)],
        scratch_shapes=[pltpu.VMEM((HALO, d), F32)],
        compiler_params=_cparams(("arbitrary",), VMEM_LIMIT),
    )(h0, h0, dh2, da, gpre, pool_w, scale, gpost, gffn)


def _ffn_chunks(f):
    return [(c, min(c + FFN_CHUNK, f)) for c in range(0, f, FFN_CHUNK)]


def ffn_fwd(a, wg_t, wu_t, wd):
    t, d = a.shape
    f = wd.shape[0]
    tm = _tile_rows(t)

    def body(a_ref, wg_ref, wu_ref, wd_ref, f_ref, gte_ref, up_ref):
        av = a_ref[...]
        acc = jnp.zeros((tm, d), F32)
        for c0, c1 in _ffn_chunks(f):
            gte = _dot_nt(av, wg_ref[c0:c1, :])
            up = _dot_nt(av, wu_ref[c0:c1, :])
            gte_ref[:, c0:c1] = gte.astype(BF16)
            up_ref[:, c0:c1] = up.astype(BF16)
            hdn = (gte * _sigmoid(gte) * up).astype(BF16)
            acc = acc + _dot(hdn, wd_ref[c0:c1, :])
        f_ref[...] = acc

    return pl.pallas_call(
        functools.partial(body), name="ffn_fwd", grid=(t // tm,),
        in_specs=[_rows(tm, d), _const((f, d)), _const((f, d)), _const((f, d))],
        out_specs=[_rows(tm, d), _rows(tm, f), _rows(tm, f)],
        out_shape=[jax.ShapeDtypeStruct((t, d), F32), jax.ShapeDtypeStruct((t, f), BF16),
                   jax.ShapeDtypeStruct((t, f), BF16)],
        compiler_params=_cparams(("parallel",), VMEM_LIMIT),
    )(a, wg_t, wu_t, wd)


def ffn_bwd_act(df, gte, up, wg_t, wu_t, wd):
    t, d = df.shape
    f = wd.shape[0]
    tm = min(_tile_rows(t), 256)

    def body(df_ref, gte_ref, up_ref, wg_ref, wu_ref, wd_ref, da_ref, dgte_ref, dup_ref, hdn_ref):
        dfv = df_ref[...]
        acc = jnp.zeros((tm, d), F32)
        for c0, c1 in _ffn_chunks(f):
            g = gte_ref[:, c0:c1].astype(F32)
            u = up_ref[:, c0:c1].astype(F32)
            sg = _sigmoid(g)
            sl = g * sg
            hdn_ref[:, c0:c1] = (sl * u).astype(BF16)
            dh = _dot_nt(dfv, wd_ref[c0:c1, :])
            dup = (dh * sl).astype(BF16)
            dgte = (dh * u * (sg * (1.0 + g * (1.0 - sg)))).astype(BF16)
            dup_ref[:, c0:c1] = dup
            dgte_ref[:, c0:c1] = dgte
            acc = acc + _dot(dgte, wg_ref[c0:c1, :]) + _dot(dup, wu_ref[c0:c1, :])
        da_ref[...] = acc

    return pl.pallas_call(
        functools.partial(body), name="ffn_bwd_act", grid=(t // tm,),
        in_specs=[_rows(tm, d), _rows(tm, f), _rows(tm, f), _const((f, d)), _const((f, d)), _const((f, d))],
        out_specs=[_rows(tm, d), _rows(tm, f), _rows(tm, f), _rows(tm, f)],
        out_shape=[jax.ShapeDtypeStruct((t, d), F32)] + [jax.ShapeDtypeStruct((t, f), BF16)] * 3,
        compiler_params=_cparams(("parallel",), VMEM_LIMIT),
    )(df, gte, up, wg_t, wu_t, wd)


def xty(x, y):
    t, nx = x.shape
    ny = y.shape[1]
    tk = _tile_rows(t)
    bn = nx // 2 if nx > 1024 else nx
    nk = t // tk

    def body(x_ref, y_ref, o_ref, acc):
        k = pl.program_id(1)

        @pl.when(k == 0)
        def _():
            acc[...] = jnp.zeros_like(acc)

        acc[...] += _dot_tn(x_ref[...].astype(BF16), y_ref[...].astype(BF16))

        @pl.when(k == nk - 1)
        def _():
            o_ref[...] = acc[...].astype(BF16)

    return pl.pallas_call(
        functools.partial(body), name="xty", grid=(nx // bn, nk),
        in_specs=[pl.BlockSpec((tk, bn), lambda j, k: (k, j)), pl.BlockSpec((tk, ny), lambda j, k: (k, 0))],
        out_specs=pl.BlockSpec((bn, ny), lambda j, k: (j, 0)),
        out_shape=jax.ShapeDtypeStruct((nx, ny), BF16),
        scratch_shapes=[pltpu.VMEM((bn, ny), F32)],
        compiler_params=_cparams(("parallel", "arbitrary"), VMEM_LIMIT),
    )(x, y)


def _ple_fwd_tile(h1, f, p, gpost, gple, wpg_ref, wpp_ref):
    rf = _rms_r(f)
    h2 = h1 + f * rf * gpost
    r2 = _rms_r(h2)
    ub = (h2 * r2 * gple).astype(BF16)
    gate = _sigmoid(_dot(ub, wpg_ref[...]))
    pp = _dot_nt(p.astype(BF16), wpp_ref[...])
    return rf, h2, r2, ub, gate, pp


def post_ple_fwd(h1, f, p, gpost, gple, wpg, wpp_t, target=None):
    t, d = h1.shape
    pd = p.shape[1]
    tm = _tile_rows(t)
    with_loss = target is not None

    def body(*refs):
        if with_loss:
            h1_ref, f_ref, p_ref, gpost_ref, gple_ref, wpg_ref, wpp_ref, tgt_ref, out_ref, loss_ref = refs
        else:
            h1_ref, f_ref, p_ref, gpost_ref, gple_ref, wpg_ref, wpp_ref, out_ref = refs
        _, h2, _, _, gate, pp = _ple_fwd_tile(h1_ref[...], f_ref[...], p_ref[...], gpost_ref[...], gple_ref[...],
                                              wpg_ref, wpp_ref)
        h3 = h2 + pp * gate
        if with_loss:
            err = h3 - tgt_ref[...]
            out_ref[...] = err * (1.0 / d)
            colsum = jnp.sum(err * err, axis=0, keepdims=True) * (0.5 / d)
            loss_ref[...] = jnp.broadcast_to(colsum, (8, d)) * (lax.broadcasted_iota(jnp.int32, (8, d), 0) == 0)
        else:
            out_ref[...] = h3

    in_specs = [_rows(tm, d), _rows(tm, d), _rows(tm, pd), _const((1, d)), _const((1, d)), _const(wpg.shape),
                _const(wpp_t.shape)]
    out_specs = [_rows(tm, d)]
    out_shape = [jax.ShapeDtypeStruct((t, d), F32)]
    args = [h1, f, p, gpost, gple, wpg, wpp_t]
    if with_loss:
        in_specs.append(_rows(tm, d))
        out_specs.append(_rows(8, d))
        out_shape.append(jax.ShapeDtypeStruct((t // tm * 8, d), F32))
        args.append(target)
    return pl.pallas_call(
        functools.partial(body), name="post_ple_loss" if with_loss else "post_ple_fwd", grid=(t // tm,),
        in_specs=in_specs, out_specs=out_specs, out_shape=out_shape,
        compiler_params=_cparams(("parallel",), VMEM_LIMIT),
    )(*args)


def post_ple_bwd(dh3, h1, f, p, gpost, gple, wpg, wpp_t):
    t, d = h1.shape
    pd = p.shape[1]
    tm = _tile_rows(t)

    def body(dh3_ref, h1_ref, f_ref, p_ref, gpost_ref, gple_ref, wpg_ref, wpp_ref,
             dh2_ref, df_ref, u_ref, dz_ref, dpp_ref, gacc_ref):
        _acc_init(gacc_ref)
        fv = f_ref[...]
        gpost_v, gple_v = gpost_ref[...], gple_ref[...]
        rf, h2, r2, ub, gate, pp = _ple_fwd_tile(h1_ref[...], fv, p_ref[...], gpost_v, gple_v, wpg_ref, wpp_ref)
        dh3v = dh3_ref[...]
        dpp_ref[...] = (dh3v * gate).astype(BF16)
        dz = (dh3v * pp * gate * (1.0 - gate)).astype(BF16)
        dz_ref[...] = dz
        u_ref[...] = ub
        du = _dot_nt(dz, wpg_ref[...])
        dh2_n, dgple = _rms_bwd(h2, r2, gple_v, du)
        dh2 = dh3v + dh2_n
        df, dgpost = _rms_bwd(fv, rf, gpost_v, dh2)
        dh2_ref[...] = dh2
        df_ref[...] = df.astype(BF16)
        gacc_ref[0:1, :] += dgple
        gacc_ref[1:2, :] += dgpost

    return pl.pallas_call(
        functools.partial(body), name="post_ple_bwd", grid=(t // tm,),
        in_specs=[_rows(tm, d), _rows(tm, d), _rows(tm, d), _rows(tm, pd), _const((1, d)), _const((1, d)),
                  _const(wpg.shape), _const(wpp_t.shape)],
        out_specs=[_rows(tm, d)] * 5 + [_resident((8, d))],
        out_shape=[jax.ShapeDtypeStruct((t, d), F32)] + [jax.ShapeDtypeStruct((t, d), BF16)] * 4
        + [jax.ShapeDtypeStruct((8, d), F32)],
        compiler_params=_cparams(("arbitrary",), VMEM_LIMIT),
    )(dh3, h1, f, p, gpost, gple, wpg, wpp_t)


def proj_rope_fwd(h, gain, w, cos, sin, n_rope, name):
    t, d = h.shape
    n = w.shape[1]
    tm = _tile_rows(t)

    def body(h_ref, g_ref, w_ref, cos_ref, sin_ref, hn_ref, y_ref):
        x = h_ref[...]
        hn = (x * _rms_r(x) * g_ref[...]).astype(BF16)
        hn_ref[...] = hn
        y = _dot(hn, w_ref[...])
        y_ref[:, :n_rope] = _rope(y[:, :n_rope], cos_ref[...], sin_ref[...]).astype(BF16)
        if n_rope < n:
            y_ref[:, n_rope:] = y[:, n_rope:].astype(BF16)

    return pl.pallas_call(
        functools.partial(body), name=name, grid=(t // tm,),
        in_specs=[_rows(tm, d), _const((1, d)), _const(w.shape), _rows(tm, LANES), _rows(tm, LANES)],
        out_specs=[_rows(tm, d), _rows(tm, n)],
        out_shape=[jax.ShapeDtypeStruct((t, d), BF16), jax.ShapeDtypeStruct((t, n), BF16)],
        compiler_params=_cparams(("parallel",), VMEM_LIMIT),
    )(h, gain, w, cos, sin)


def proj_rope_bwd(dh1, h0, cos, sin, branches, name):
    t, d = h0.shape
    tm = _tile_rows(t)
    nb = len(branches)
    n_cot = [len(b[3]) for b in branches]

    def body(*refs):
        dh1_ref, h0_ref, cos_ref, sin_ref = refs[:4]
        pos = 4
        br_refs = []
        for b in range(nb):
            br_refs.append((refs[pos], refs[pos + 1], refs[pos + 2:pos + 2 + n_cot[b]]))
            pos += 2 + n_cot[b]
        dh0_ref = refs[pos]
        dpre_refs = refs[pos + 1:pos + 1 + nb]
        gacc_ref = refs[pos + 1 + nb]
        _acc_init(gacc_ref)
        x = h0_ref[...]
        r0 = _rms_r(x)
        dh = dh1_ref[...]
        for b in range(nb):
            g_ref, w_ref, cot_refs = br_refs[b]
            n_rope = branches[b][2]
            dy = cot_refs[0][...].astype(F32)
            for c_ref in cot_refs[1:]:
                dy = dy + c_ref[...].astype(F32)
            n = dy.shape[1]
            dpre_refs[b][:, :n_rope] = _unrope(dy[:, :n_rope], cos_ref[...], sin_ref[...]).astype(BF16)
            if n_rope < n:
                dpre_refs[b][:, n_rope:] = dy[:, n_rope:].astype(BF16)
            dhn = _dot_nt(dpre_refs[b][...], w_ref[...])
            dx, dg = _rms_bwd(x, r0, g_ref[...], dhn)
            dh = dh + dx
            gacc_ref[b:b + 1, :] += dg
        dh0_ref[...] = dh

    in_specs = [_rows(tm, d), _rows(tm, d), _rows(tm, LANES), _rows(tm, LANES)]
    args = [dh1, h0, cos, sin]
    out_specs = [_rows(tm, d)]
    out_shape = [jax.ShapeDtypeStruct((t, d), F32)]
    for gain, w, _, cots in branches:
        n = w.shape[1]
        in_specs += [_const((1, d)), _const(w.shape)] + [_rows(tm, n)] * len(cots)
        args += [gain, w] + list(cots)
        out_specs.append(_rows(tm, n))
        out_shape.append(jax.ShapeDtypeStruct((t, n), BF16))
    out_specs.append(_resident((8, d)))
    out_shape.append(jax.ShapeDtypeStruct((8, d), F32))
    return pl.pallas_call(
        functools.partial(body), name=name, grid=(t // tm,),
        in_specs=in_specs, out_specs=out_specs, out_shape=out_shape,
        compiler_params=_cparams(("arbitrary",), VMEM_LIMIT),
    )(*args)


def _band_mask(n):
    row = lax.broadcasted_iota(jnp.int32, (GQA * BLOCK, 2 * BLOCK), 0) % BLOCK
    col = lax.broadcasted_iota(jnp.int32, (GQA * BLOCK, 2 * BLOCK), 1)
    return ((col < BLOCK) & (col > row) & (n > 0)) | ((col >= BLOCK) & ((col - BLOCK) <= row))


def _upper_half():
    return lax.broadcasted_iota(jnp.int32, (BLOCK, LANES), 1) >= HEAD_DIM


def _group_rows(x_ref, kh, upper):
    e_kv = kh % 2
    parts = []
    for r in range(GQA):
        h = GQA * kh + r
        tile = x_ref[:, (h // 2) * LANES:(h // 2 + 1) * LANES].astype(F32)
        tile = jnp.where(upper if h % 2 else ~upper, tile, 0.0)
        if h % 2 != e_kv:
            tile = pltpu.roll(tile, HEAD_DIM, 1)
        parts.append(tile)
    return jnp.concatenate(parts, axis=0).astype(BF16)


def _scatter_group(res, kh, upper, tiles):
    e_kv = kh % 2
    for r in range(GQA):
        h = GQA * kh + r
        piece = res[r * BLOCK:(r + 1) * BLOCK, :]
        piece = jnp.where(upper if e_kv else ~upper, piece, 0.0)
        if h % 2 != e_kv:
            piece = pltpu.roll(piece, HEAD_DIM, 1)
        tiles[h // 2] = tiles[h // 2] + piece


def _sink_col(sink_ref, kh):
    return jnp.concatenate(
        [jnp.broadcast_to(sink_ref[GQA * kh + r:GQA * kh + r + 1, 0:1], (BLOCK, 1)) for r in range(GQA)], axis=0)


def _probs(q4, kt, valid, sink):
    s = _dot_nt(q4, kt) * (HEAD_DIM ** -0.5)
    s = jnp.where(valid, s, NEG_INF)
    mx = jnp.maximum(jnp.max(s, axis=1, keepdims=True), sink)
    p = jnp.exp(s - mx)
    ps = jnp.exp(sink - mx)
    den = jnp.sum(p, axis=1, keepdims=True) + ps
    return p / den, ps / den


def swa_fwd(q, kv, sink_b):
    t, d = q.shape
    nb = t // BLOCK
    kvw = N_KV_HEADS * HEAD_DIM

    def body(q_ref, kvc_ref, kvp_ref, sink_ref, o_ref):
        n = pl.program_id(0)
        valid = _band_mask(n)
        upper = _upper_half()
        band = jnp.concatenate([kvp_ref[...], kvc_ref[...]], axis=0)
        tiles = [jnp.zeros((BLOCK, LANES), F32) for _ in range(d // LANES)]
        for kh in range(N_KV_HEADS):
            lt = kh // 2
            kt = band[:, lt * LANES:(lt + 1) * LANES]
            vt = band[:, kvw + lt * LANES:kvw + (lt + 1) * LANES]
            pr, _ = _probs(_group_rows(q_ref, kh, upper), kt, valid, _sink_col(sink_ref, kh))
            _scatter_group(_dot(pr.astype(BF16), vt), kh, upper, tiles)
        for j, tl in enumerate(tiles):
            o_ref[:, j * LANES:(j + 1) * LANES] = tl.astype(BF16)

    return pl.pallas_call(
        functools.partial(body), name="swa_fwd", grid=(nb,),
        in_specs=[_rows(BLOCK, d), _rows(BLOCK, 2 * kvw),
                  pl.BlockSpec((BLOCK, 2 * kvw), lambda n: (jnp.maximum(n - 1, 0), 0)), _const(sink_b.shape)],
        out_specs=_rows(BLOCK, d),
        out_shape=jax.ShapeDtypeStruct((t, d), BF16),
        compiler_params=_cparams(("parallel",), VMEM_LIMIT),
    )(q, kv, kv, sink_b)


def swa_bwd(q, kv, do, sink_b):
    t, d = q.shape
    nb = t // BLOCK
    kvw = N_KV_HEADS * HEAD_DIM

    def body(q_ref, do_ref, kvc_ref, kvp_ref, sink_ref, dq_ref, dkv_ref, dsink_ref, carry):
        i = pl.program_id(0)
        n = nb - 1 - i
        _acc_init(dsink_ref)

        @pl.when(i == 0)
        def _():
            carry[...] = jnp.zeros_like(carry)

        valid = _band_mask(n)
        upper = _upper_half()
        band = jnp.concatenate([kvp_ref[...], kvc_ref[...]], axis=0)
        dq_tiles = [jnp.zeros((BLOCK, LANES), F32) for _ in range(d // LANES)]
        dband = [jnp.zeros((2 * BLOCK, LANES), F32) for _ in range(2 * kvw // LANES)]
        for kh in range(N_KV_HEADS):
            lt = kh // 2
            kt = band[:, lt * LANES:(lt + 1) * LANES]
            vt = band[:, kvw + lt * LANES:kvw + (lt + 1) * LANES]
            q4 = _group_rows(q_ref, kh, upper)
            do4 = _group_rows(do_ref, kh, upper)
            p, psink = _probs(q4, kt, valid, _sink_col(sink_ref, kh))
            dp = _dot_nt(do4, vt)
            delta = jnp.sum(p * dp, axis=1, keepdims=True)
            ds = (p * (dp - delta) * (HEAD_DIM ** -0.5)).astype(BF16)
            dsk = psink * delta
            for r in range(GQA):
                h = GQA * kh + r
                dsink_ref[h:h + 1, :] -= jnp.sum(dsk[r * BLOCK:(r + 1) * BLOCK, :], axis=0, keepdims=True)
            _scatter_group(_dot(ds, kt), kh, upper, dq_tiles)
            dband[lt] = dband[lt] + _dot_tn(ds, q4)
            dband[kvw // LANES + lt] = dband[kvw // LANES + lt] + _dot_tn(p.astype(BF16), do4)
        for j, tl in enumerate(dq_tiles):
            dq_ref[:, j * LANES:(j + 1) * LANES] = tl.astype(BF16)
        dall = jnp.concatenate(dband, axis=1)
        dkv_ref[...] = dall[BLOCK:, :] + carry[...]
        carry[...] = dall[:BLOCK, :]

    rev = lambda i: (nb - 1 - i, 0)
    return pl.pallas_call(
        functools.partial(body), name="swa_bwd", grid=(nb,),
        in_specs=[pl.BlockSpec((BLOCK, d), rev), pl.BlockSpec((BLOCK, d), rev), pl.BlockSpec((BLOCK, 2 * kvw), rev),
                  pl.BlockSpec((BLOCK, 2 * kvw), lambda i: (jnp.maximum(nb - 2 - i, 0), 0)), _const(sink_b.shape)],
        out_specs=[pl.BlockSpec((BLOCK, d), rev), pl.BlockSpec((BLOCK, 2 * kvw), rev), _resident(sink_b.shape)],
        out_shape=[jax.ShapeDtypeStruct((t, d), BF16), jax.ShapeDtypeStruct((t, 2 * kvw), F32),
                   jax.ShapeDtypeStruct(sink_b.shape, F32)],
        scratch_shapes=[pltpu.VMEM((BLOCK, 2 * kvw), F32)],
        compiler_params=_cparams(("arbitrary",), VMEM_LIMIT),
    )(q, do, kv, kv, sink_b)


def oproj_post_fwd(attn, w_o, h0, gpost, gffn):
    t, d = h0.shape
    tm = _tile_rows(t)

    def body(at_ref, w_ref, h0_ref, gpost_ref, gffn_ref, m_ref, h1_ref, a_ref):
        m = _dot(at_ref[...], w_ref[...])
        m_ref[...] = m
        h1 = h0_ref[...] + m * _rms_r(m) * gpost_ref[...]
        h1_ref[...] = h1
        a_ref[...] = (h1 * _rms_r(h1) * gffn_ref[...]).astype(BF16)

    return pl.pallas_call(
        functools.partial(body), name="oproj_post_fwd", grid=(t // tm,),
        in_specs=[_rows(tm, d), _const(w_o.shape), _rows(tm, d), _const((1, d)), _const((1, d))],
        out_specs=[_rows(tm, d)] * 3,
        out_shape=[jax.ShapeDtypeStruct((t, d), F32), jax.ShapeDtypeStruct((t, d), F32),
                   jax.ShapeDtypeStruct((t, d), BF16)],
        compiler_params=_cparams(("parallel",), VMEM_LIMIT),
    )(attn, w_o, h0, gpost, gffn)


def oproj_post_bwd(dh2, da, h1, m, w_o, gpost, gffn):
    t, d = h1.shape
    tm = _tile_rows(t)

    def body(dh2_ref, da_ref, h1_ref, m_ref, w_ref, gpost_ref, gffn_ref, dh1_ref, dm_ref, dat_ref, gacc_ref):
        _acc_init(gacc_ref)
        h1v, mv = h1_ref[...], m_ref[...]
        dh1_n, dgffn = _rms_bwd(h1v, _rms_r(h1v), gffn_ref[...], da_ref[...])
        dh1 = dh2_ref[...] + dh1_n
        dm, dgpost = _rms_bwd(mv, _rms_r(mv), gpost_ref[...], dh1)
        dmb = dm.astype(BF16)
        dh1_ref[...] = dh1
        dm_ref[...] = dmb
        dat_ref[...] = _dot_nt(dmb, w_ref[...]).astype(BF16)
        gacc_ref[0:1, :] += dgpost
        gacc_ref[1:2, :] += dgffn

    return pl.pallas_call(
        functools.partial(body), name="oproj_post_bwd", grid=(t // tm,),
        in_specs=[_rows(tm, d)] * 4 + [_const(w_o.shape), _const((1, d)), _const((1, d))],
        out_specs=[_rows(tm, d)] * 3 + [_resident((8, d))],
        out_shape=[jax.ShapeDtypeStruct((t, d), F32), jax.ShapeDtypeStruct((t, d), BF16),
                   jax.ShapeDtypeStruct((t, d), BF16), jax.ShapeDtypeStruct((8, d), F32)],
        compiler_params=_cparams(("arbitrary",), VMEM_LIMIT),
    )(dh2, da, h1, m, w_o, gpost, gffn)


def _my_place():
    return lax.axis_index("x"), lax.axis_index("y"), lax.axis_index("c")


def _block_index(px, py, pc):
    return 4 * px + 2 * py + pc


def allgather_pieces(shards, name):
    np_ = len(shards)

    def body(*refs):
        in_refs, out_refs = refs[:np_], refs[np_:2 * np_]
        send_sems, recv_sems, local_sems = refs[2 * np_:]
        x, y, c = _my_place()
        me, sibling = (x, y, c), (x, y, 1 - c)
        chips = [(1 - x, y), (x, 1 - y), (1 - x, 1 - y)]

        def rows(p, place):
            r = in_refs[p].shape[0]
            return out_refs[p].at[pl.ds(_block_index(*place) * r, r), :]

        def copy(p, k, block, to, src=None):
            return pltpu.make_async_remote_copy(
                src_ref=rows(p, block) if src is None else src, dst_ref=rows(p, block),
                send_sem=send_sems.at[p, k], recv_sem=recv_sems.at[p, k], device_id=to, device_id_type=MESH)

        mine = [pltpu.make_async_copy(in_refs[p], rows(p, me), local_sems.at[p]) for p in range(np_)]
        first, passed = [], []
        for p in range(np_):
            mine[p].start()
            first.append(copy(p, 0, me, sibling, src=in_refs[p]))
            first += [copy(p, 1 + j, me, (*chip, c), src=in_refs[p]) for j, chip in enumerate(chips)]
        for cp in first:
            cp.start()
        for p in range(np_):
            for j, chip in enumerate(chips):
                copy(p, 1 + j, (*chip, c), me).wait_recv()
                fwd = copy(p, 4 + j, (*chip, c), sibling)
                fwd.start()
                passed.append(fwd)
        for p in range(np_):
            copy(p, 0, sibling, me).wait_recv()
            for j, chip in enumerate(chips):
                copy(p, 4 + j, (*chip, 1 - c), me).wait_recv()
        for cp in first + passed:
            cp.wait_send()
        for cp in mine:
            cp.wait()

    return pl.pallas_call(
        functools.partial(body), name=name,
        in_specs=[ANY] * np_, out_specs=[ANY] * np_,
        out_shape=[jax.ShapeDtypeStruct((N_DEV * s.shape[0], s.shape[1]), s.dtype) for s in shards],
        scratch_shapes=[pltpu.SemaphoreType.DMA((np_, 7)), pltpu.SemaphoreType.DMA((np_, 7)),
                        pltpu.SemaphoreType.DMA((np_,))],
    )(*shards)


def _peers():
    x, y, c = _my_place()
    flips = [(fx, fy, fc) for fx in (0, 1) for fy in (0, 1) for fc in (0, 1)][1:]
    return [(1 - x if fx else x, 1 - y if fy else y, 1 - c if fc else c) for fx, fy, fc in flips]


def reduce_scatter_pieces(fulls, name):
    np_ = len(fulls)

    def body(*refs):
        in_refs, out_refs = refs[:np_], refs[np_:2 * np_]
        send_sems, recv_sems, local_sems = refs[2 * np_:]
        me = _my_place()
        my_block = _block_index(*me)
        peers = _peers()

        def copy(p, k, to):
            r = out_refs[p].shape[1]
            return pltpu.make_async_remote_copy(
                src_ref=in_refs[p].at[pl.ds(_block_index(*to) * r, r), :], dst_ref=out_refs[p].at[my_block],
                send_sem=send_sems.at[p, k], recv_sem=recv_sems.at[p, k], device_id=to, device_id_type=MESH)

        def landed(p, k, frm):
            r = out_refs[p].shape[1]
            return pltpu.make_async_remote_copy(
                src_ref=in_refs[p].at[pl.ds(0, r), :], dst_ref=out_refs[p].at[_block_index(*frm)],
                send_sem=send_sems.at[p, k], recv_sem=recv_sems.at[p, k], device_id=frm, device_id_type=MESH)

        mine, sends = [], []
        for p in range(np_):
            r = out_refs[p].shape[1]
            own = pltpu.make_async_copy(in_refs[p].at[pl.ds(my_block * r, r), :], out_refs[p].at[my_block],
                                        local_sems.at[p])
            own.start()
            mine.append(own)
            for k, peer in enumerate(peers):
                cp = copy(p, k, peer)
                cp.start()
                sends.append(cp)
        for p in range(np_):
            for k, peer in enumerate(peers):
                landed(p, k, peer).wait_recv()
        for cp in sends:
            cp.wait_send()
        for cp in mine:
            cp.wait()

    return pl.pallas_call(
        functools.partial(body), name=name,
        in_specs=[ANY] * np_, out_specs=[ANY] * np_,
        out_shape=[jax.ShapeDtypeStruct((N_DEV, g.shape[0] // N_DEV, g.shape[1]), g.dtype) for g in fulls],
        scratch_shapes=[pltpu.SemaphoreType.DMA((np_, 7)), pltpu.SemaphoreType.DMA((np_, 7)),
                        pltpu.SemaphoreType.DMA((np_,))],
    )(*fulls)


def allreduce_small(pack):
    r, c = pack.shape

    def body(pack_ref, out_ref, gathered, send_sems, recv_sems):
        me = _my_place()
        my_block = _block_index(*me)
        peers = _peers()

        def copy(k, slot, to):
            return pltpu.make_async_remote_copy(
                src_ref=pack_ref, dst_ref=gathered.at[slot], send_sem=send_sems.at[k], recv_sem=recv_sems.at[k],
                device_id=to, device_id_type=MESH)

        sends = [copy(k, my_block, peer) for k, peer in enumerate(peers)]
        for cp in sends:
            cp.start()
        gathered[my_block] = pack_ref[...]
        for k, peer in enumerate(peers):
            copy(k, _block_index(*peer), peer).wait_recv()
        for cp in sends:
            cp.wait_send()
        total = gathered[0]
        for j in range(1, N_DEV):
            total = total + gathered[j]
        out_ref[...] = total

    return pl.pallas_call(
        functools.partial(body), name="allreduce_small",
        in_specs=[pl.BlockSpec(memory_space=pltpu.VMEM)], out_specs=pl.BlockSpec(memory_space=pltpu.VMEM),
        out_shape=jax.ShapeDtypeStruct((r, c), F32),
        scratch_shapes=[pltpu.VMEM((N_DEV, r, c), F32), pltpu.SemaphoreType.DMA((7,)), pltpu.SemaphoreType.DMA((7,))],
    )(pack)


def sum_parts(parts):
    n, r, c = parts.shape
    br = 256 if r % 256 == 0 else r

    def body(p_ref, g_ref):
        g = p_ref[0].astype(F32)
        for j in range(1, n):
            g = g + p_ref[j].astype(F32)
        g_ref[...] = g

    return pl.pallas_call(
        functools.partial(body), name="sum_parts", grid=(r // br,),
        in_specs=[pl.BlockSpec((n, br, c), lambda i: (0, i, 0))], out_specs=_rows(br, c),
        out_shape=jax.ShapeDtypeStruct((r, c), F32),
        compiler_params=_cparams(("parallel",)),
    )(parts)


def adamw(w, m, v, parts):
    r, c = w.shape
    n = parts.shape[0]
    br = 256 if r % 256 == 0 else r

    def body(w_ref, m_ref, v_ref, p_ref, g_ref, d_ref, nm_ref, nv_ref):
        g = p_ref[0].astype(F32)
        for j in range(1, n):
            g = g + p_ref[j].astype(F32)
        nm = ADAM_B1 * m_ref[...] + (1.0 - ADAM_B1) * g
        nv = ADAM_B2 * v_ref[...] + (1.0 - ADAM_B2) * (g * g)
        m_hat = nm / (1.0 - ADAM_B1 ** ADAM_STEP)
        v_hat = nv / (1.0 - ADAM_B2 ** ADAM_STEP)
        g_ref[...] = g
        d_ref[...] = -ADAM_LR * (m_hat / (jnp.sqrt(v_hat) + ADAM_EPS) + ADAM_WD * w_ref[...])
        nm_ref[...] = nm
        nv_ref[...] = nv

    return pl.pallas_call(
        functools.partial(body), name="adamw", grid=(r // br,),
        in_specs=[_rows(br, c)] * 3 + [pl.BlockSpec((n, br, c), lambda i: (0, i, 0))],
        out_specs=[_rows(br, c)] * 4, out_shape=[jax.ShapeDtypeStruct((r, c), F32)] * 4,
        compiler_params=_cparams(("parallel",)),
    )(w, m, v, parts)


def _adamw_nd(w, m, v, g):
    shp = w.shape
    c = shp[-1]
    flat = lambda a: a.reshape(-1, c)
    outs = adamw(flat(w), flat(m), flat(v), flat(g)[None])
    return [o.reshape(shp) for o in outs]


def _pad_rows(a, rows=8):
    return jnp.pad(a, ((0, rows - a.shape[0]), (0, 0)))


def kernel(x, p, mix_pre_g, mix_post_g, ffn_pre_g, ffn_post_g, pool_w, pool_scale, kv_norm_g, w_k, w_v, w_q, w_o, sinks, w_ff_gate, w_ff_up, w_ff_down, ple_norm_g, w_ple_gate, w_ple_proj, loss_target, m_mix_pre_g, m_mix_post_g, m_ffn_pre_g, m_ffn_post_g, m_pool_w, m_pool_scale, m_kv_norm_g, m_w_k, m_w_v, m_w_q, m_w_o, m_sinks, m_w_ff_gate, m_w_ff_up, m_w_ff_down, m_ple_norm_g, m_w_ple_gate, m_w_ple_proj, v_mix_pre_g, v_mix_post_g, v_ffn_pre_g, v_ffn_post_g, v_pool_w, v_pool_scale, v_kv_norm_g, v_w_k, v_w_v, v_w_q, v_w_o, v_sinks, v_w_ff_gate, v_w_ff_up, v_w_ff_down, v_ple_norm_g, v_w_ple_gate, v_w_ple_proj):
    depth = w_ff_gate.shape[0]
    n_a = pool_w.shape[0]
    t, d = x.shape[1], x.shape[2]
    h = x[0]
    tgt = loss_target[0]
    my_block = _block_index(*_my_place())
    row = lambda g, i: g[i][None, :]
    bf = lambda a: a.astype(BF16)

    full = []
    for i in range(depth):
        shards = [bf(w_ff_gate[i].T), bf(w_ff_up[i].T), bf(w_ff_down[i]), bf(w_ple_gate[i]), bf(w_ple_proj[i].T)]
        if i < n_a:
            shards.append(bf(pool_w[i].reshape(-1, POOL_GROUP)))
        else:
            shards += [bf(w_q[i - n_a]), bf(w_o[i - n_a])]
            if i == n_a:
                shards.append(bf(jnp.concatenate([w_k, w_v], axis=1)))
        full.append(allgather_pieces(shards, f"allgather_l{i}"))
    scale_full = allgather_pieces([_pad_rows(pool_scale)], "allgather_scale")[0]
    scale_full = scale_full.reshape(N_DEV, 8, -1)[:, :n_a].transpose(1, 0, 2).reshape(n_a, 1, d)
    pool_full = [full[i][5].reshape(N_DEV, len(POOL_WINDOWS), -1, POOL_GROUP).transpose(1, 0, 2, 3)
                 .reshape(len(POOL_WINDOWS), POOL_GROUP, POOL_GROUP) for i in range(n_a)]

    cos, sin = _rope_tables(t)
    sink_b = [jnp.broadcast_to(sinks[j][:, None], (N_HEADS, LANES)) for j in range(depth - n_a)]

    saved = []
    kv = hk = None
    for i in range(depth):
        wg_t, wu_t, wd, wpg, wpp_t = full[i][:5]
        s = {"h0": h}
        if i < n_a:
            h1, a = pool_mix_fwd(h, row(mix_pre_g, i), pool_full[i], scale_full[i], row(mix_post_g, i),
                                 row(ffn_pre_g, i))
        else:
            j = i - n_a
            if i == n_a:
                hk, kv = proj_rope_fwd(h, kv_norm_g[None, :], full[i][7], cos, sin, N_KV_HEADS * HEAD_DIM, "kv_proj_fwd")
            hn, q = proj_rope_fwd(h, row(mix_pre_g, i), full[i][5], cos, sin, d, "q_proj_fwd")
            attn = swa_fwd(q, kv, sink_b[j])
            m, h1, a = oproj_post_fwd(attn, full[i][6], h, row(mix_post_g, i), row(ffn_pre_g, i))
            s.update(hn=hn, q=q, attn=attn, m=m)
        f, gte, up = ffn_fwd(a, wg_t, wu_t, wd)
        s.update(h1=h1, a=a, f=f, gte=gte, up=up)
        if i < depth - 1:
            h = post_ple_fwd(h1, f, p[i, 0], row(ffn_post_g, i), row(ple_norm_g, i), wpg, wpp_t)[0]
        else:
            dh, loss_rows = post_ple_fwd(h1, f, p[i, 0], row(ffn_post_g, i), row(ple_norm_g, i), wpg, wpp_t, target=tgt)
        saved.append(s)

    zero_row = jnp.zeros((1, d), F32)
    g_mix_pre, g_mix_post, g_ffn_pre, g_ffn_post, g_ple = ([None] * depth for _ in range(5))
    g_kv = g_sinks = None
    g_scale = [None] * n_a
    landing = [None] * depth
    dkv_sum = []
    for i in reversed(range(depth)):
        s = saved[i]
        wg_t, wu_t, wd, wpg, wpp_t = full[i][:5]
        dh2, df, ub, dzb, dppb, gacc = post_ple_bwd(dh, s["h1"], s["f"], p[i, 0], row(ffn_post_g, i),
                                                    row(ple_norm_g, i), wpg, wpp_t)
        g_ple[i], g_ffn_post[i] = gacc[0], gacc[1]
        da, dgte, dup, hdn = ffn_bwd_act(df, s["gte"], s["up"], wg_t, wu_t, wd)
        grads = [xty(dgte, s["a"]), xty(dup, s["a"]), xty(hdn, df), xty(ub, dzb), xty(dppb, p[i, 0])]
        if i < n_a:
            dh, dpw, gacc = pool_mix_bwd(s["h0"], dh2, da, row(mix_pre_g, i), pool_full[i], scale_full[i],
                                         row(mix_post_g, i), row(ffn_pre_g, i))
            g_mix_pre[i], g_mix_post[i], g_ffn_pre[i], g_scale[i] = gacc[0], gacc[1], gacc[2], gacc[3]
            dpw = dpw.reshape(len(POOL_WINDOWS), N_DEV, -1, POOL_GROUP).transpose(1, 0, 2, 3)
            grads.append(bf(dpw.reshape(-1, POOL_GROUP)))
        else:
            j = i - n_a
            dh1, dmb, dattn, gacc = oproj_post_bwd(dh2, da, s["h1"], s["m"], full[i][6], row(mix_post_g, i),
                                                   row(ffn_pre_g, i))
            g_mix_post[i], g_ffn_pre[i] = gacc[0], gacc[1]
            dq, dkv, dsink = swa_bwd(s["q"], kv, dattn, sink_b[j])
            dkv_sum.append(dkv)
            g_sinks = [dsink[:, 0]] + (g_sinks or [])
            branches = [(row(mix_pre_g, i), full[i][5], d, [dq])]
            if i == n_a:
                branches.append((kv_norm_g[None, :], full[i][7], N_KV_HEADS * HEAD_DIM, dkv_sum))
            outs = proj_rope_bwd(dh1, s["h0"], cos, sin, branches, f"proj_bwd_l{i}")
            dh, gacc = outs[0], outs[-1]
            g_mix_pre[i] = gacc[0]
            grads += [xty(s["hn"], outs[1]), xty(s["attn"], dmb)]
            if i == n_a:
                g_kv = gacc[1]
                grads.append(xty(hk, outs[2]))
        landing[i] = reduce_scatter_pieces(grads, f"reduce_scatter_l{i}")
    grad_x = dh[None]

    loss_row = jnp.sum(loss_rows, axis=0, keepdims=True)
    sink_row = jnp.pad(jnp.concatenate(g_sinks)[None, :], ((0, 0), (0, d - sinks.size)))
    stack = lambda rows_: _pad_rows(jnp.stack(rows_))
    pack = jnp.concatenate([stack(g_mix_pre), stack(g_mix_post), stack(g_ffn_pre), stack(g_ffn_post), stack(g_ple),
                            _pad_rows(g_kv[None]), stack(g_scale), _pad_rows(sink_row), _pad_rows(loss_row)], axis=0)
    tot = allreduce_small(pack)
    sec = lambda k, n: tot[8 * k:8 * k + n]
    loss = jnp.sum(tot[64])
    small = {
        "mix_pre_g": sec(0, depth), "mix_post_g": sec(1, depth), "ffn_pre_g": sec(2, depth),
        "ffn_post_g": sec(3, depth), "ple_norm_g": sec(4, depth), "kv_norm_g": tot[40],
        "pool_scale": lax.dynamic_slice_in_dim(sec(6, n_a), my_block * pool_scale.shape[1], pool_scale.shape[1], axis=1),
        "sinks": tot[56, :sinks.size].reshape(sinks.shape),
    }

    weights = dict(mix_pre_g=mix_pre_g, mix_post_g=mix_post_g, ffn_pre_g=ffn_pre_g, ffn_post_g=ffn_post_g, pool_w=pool_w, pool_scale=pool_scale, kv_norm_g=kv_norm_g, w_k=w_k, w_v=w_v, w_q=w_q, w_o=w_o, sinks=sinks, w_ff_gate=w_ff_gate, w_ff_up=w_ff_up, w_ff_down=w_ff_down, ple_norm_g=ple_norm_g, w_ple_gate=w_ple_gate, w_ple_proj=w_ple_proj)
    mom1 = dict(mix_pre_g=m_mix_pre_g, mix_post_g=m_mix_post_g, ffn_pre_g=m_ffn_pre_g, ffn_post_g=m_ffn_post_g, pool_w=m_pool_w, pool_scale=m_pool_scale, kv_norm_g=m_kv_norm_g, w_k=m_w_k, w_v=m_w_v, w_q=m_w_q, w_o=m_w_o, sinks=m_sinks, w_ff_gate=m_w_ff_gate, w_ff_up=m_w_ff_up, w_ff_down=m_w_ff_down, ple_norm_g=m_ple_norm_g, w_ple_gate=m_w_ple_gate, w_ple_proj=m_w_ple_proj)
    mom2 = dict(mix_pre_g=v_mix_pre_g, mix_post_g=v_mix_post_g, ffn_pre_g=v_ffn_pre_g, ffn_post_g=v_ffn_post_g, pool_w=v_pool_w, pool_scale=v_pool_scale, kv_norm_g=v_kv_norm_g, w_k=v_w_k, w_v=v_w_v, w_q=v_w_q, w_o=v_w_o, sinks=v_sinks, w_ff_gate=v_w_ff_gate, w_ff_up=v_w_ff_up, w_ff_down=v_w_ff_down, ple_norm_g=v_ple_norm_g, w_ple_gate=v_w_ple_gate, w_ple_proj=v_w_ple_proj)

    def land(i, k):
        return sum_parts(landing[i][k])

    gw = dict(small)
    gw["kv_norm_g"] = small["kv_norm_g"]
    gw["w_ff_gate"] = jnp.stack([land(i, 0).T for i in range(depth)])
    gw["w_ff_up"] = jnp.stack([land(i, 1).T for i in range(depth)])
    gw["w_ff_down"] = jnp.stack([land(i, 2) for i in range(depth)])
    gw["w_ple_gate"] = jnp.stack([land(i, 3) for i in range(depth)])
    gw["w_ple_proj"] = jnp.stack([land(i, 4).T for i in range(depth)])
    gw["pool_w"] = jnp.stack([land(i, 5).reshape(pool_w.shape[1:]) for i in range(n_a)])
    gw["w_q"] = jnp.stack([land(i, 5) for i in range(n_a, depth)])
    gw["w_o"] = jnp.stack([land(i, 6) for i in range(n_a, depth)])
    gkv = land(n_a, 7)
    gw["w_k"], gw["w_v"] = gkv[:, :w_k.shape[1]], gkv[:, w_k.shape[1]:]

    order = ["mix_pre_g", "mix_post_g", "ffn_pre_g", "ffn_post_g", "pool_w", "pool_scale", "kv_norm_g", "w_k", "w_v",
             "w_q", "w_o", "sinks", "w_ff_gate", "w_ff_up", "w_ff_down", "ple_norm_g", "w_ple_gate", "w_ple_proj"]
    g_out, d_out, m_out, v_out = [], [], [], []
    for nme in order:
        w = weights[nme]
        as2d = (lambda a: a[None, :]) if w.ndim == 1 else (lambda a: a)
        g, dl, nm, nv = _adamw_nd(as2d(w), as2d(mom1[nme]), as2d(mom2[nme]), as2d(gw[nme]))
        for lst, val in ((g_out, g), (d_out, dl), (m_out, nm), (v_out, nv)):
            lst.append(val.reshape(w.shape))
    return (loss, grad_x, *g_out, *d_out, *m_out, *v_out)
```

```python
import functools

import jax
import jax.numpy as jnp
from jax import lax
from jax.experimental import pallas as pl
from jax.experimental.pallas import tpu as pltpu

F32 = jnp.float32
BF16 = jnp.bfloat16

N_DEV = 8
HEAD_DIM = 64
N_HEADS = 16
N_KV_HEADS = 4
GQA = N_HEADS // N_KV_HEADS
BLOCK = 128
POOL_WINDOWS = (2, 4, 8, 16)
POOL_GROUP = 256
HALO = 16
ROPE_THETA = 10000.0
RMS_EPS = 1e-6
NEG_INF = -1e30
LANES = 128
FFN_CHUNK = 768
VMEM_LIMIT = 56 * 1024 * 1024

ADAM_LR = 0.001
ADAM_B1 = 0.9
ADAM_B2 = 0.999
ADAM_EPS = 1e-08
ADAM_WD = 0.01
ADAM_STEP = 10

MESH = pl.DeviceIdType.MESH
ANY = pl.BlockSpec(memory_space=pl.ANY)

NT_DIMS = (((1,), (1,)), ((), ()))
TN_DIMS = (((0,), (0,)), ((), ()))


def _cparams(sem=None, vmem=None):
    kw = {}
    if sem is not None:
        kw["dimension_semantics"] = sem
    if vmem is not None:
        kw["vmem_limit_bytes"] = vmem
    return pltpu.CompilerParams(**kw)


def _rows(tm, n):
    return pl.BlockSpec((tm, n), lambda i: (i, 0))


def _rows_rev(tm, n, nt):
    return pl.BlockSpec((tm, n), lambda i: (nt - 1 - i, 0))


def _const(shape):
    nd = len(shape)
    return pl.BlockSpec(shape, lambda *_: (0,) * nd, pipeline_mode=pl.Buffered(1))


def _resident(shape):
    nd = len(shape)
    return pl.BlockSpec(shape, lambda *_: (0,) * nd)


def _tile_rows(t):
    return 512 if t % 512 == 0 else 128


def _dot(a, b):
    return jnp.dot(a, b, preferred_element_type=F32)


def _dot_nt(a, b):
    return lax.dot_general(a, b, NT_DIMS, preferred_element_type=F32)


def _dot_tn(a, b):
    return lax.dot_general(a, b, TN_DIMS, preferred_element_type=F32)


def _rms_r(x):
    return lax.rsqrt(jnp.mean(x * x, axis=-1, keepdims=True) + RMS_EPS)


def _rms_bwd(x, r, g, dy):
    gy = dy * g
    dx = r * gy - x * (r * r * r * jnp.mean(gy * x, axis=-1, keepdims=True))
    dg = jnp.sum(dy * (x * r), axis=0, keepdims=True)
    return dx, dg


def _sigmoid(x):
    return jax.nn.sigmoid(x)


def _rope_tables(t):
    inv = 1.0 / (ROPE_THETA ** (jnp.arange(0, HEAD_DIM, 2, dtype=F32) / HEAD_DIM))
    ang = jnp.arange(t, dtype=F32)[:, None] * inv[None, :]
    c, s = jnp.cos(ang), jnp.sin(ang)
    cos = jnp.concatenate([c, c, c, c], axis=1)
    sin = jnp.concatenate([-s, s, -s, s], axis=1)
    return cos, sin


def _swap_halves(x):
    n = x.shape[1]
    lane = lax.broadcasted_iota(jnp.int32, x.shape, 1)
    first = (lane % HEAD_DIM) < (HEAD_DIM // 2)
    return jnp.where(first, pltpu.roll(x, n - HEAD_DIM // 2, 1), pltpu.roll(x, HEAD_DIM // 2, 1))


def _rope(x, cos, sin):
    reps = x.shape[1] // LANES
    return x * jnp.tile(cos, (1, reps)) + _swap_halves(x) * jnp.tile(sin, (1, reps))


def _unrope(dy, cos, sin):
    reps = dy.shape[1] // LANES
    return dy * jnp.tile(cos, (1, reps)) + _swap_halves(dy * jnp.tile(sin, (1, reps)))


def _acc_init(acc_ref):
    @pl.when(pl.program_id(0) == 0)
    def _():
        acc_ref[...] = jnp.zeros_like(acc_ref)


def _window_sums(ext, tm, forward):
    n = tm + HALO
    out = []
    for g, w in enumerate(POOL_WINDOWS):
        s = ext[:, g * POOL_GROUP:(g + 1) * POOL_GROUP]
        k = 1
        while k < w:
            s = s + pltpu.roll(s, k if forward else n - k, 0)
            k *= 2
        out.append(s[HALO:, :] if forward else s[:tm, :])
    return out


def _pool_counts(tile, tm):
    t = tile * tm + lax.broadcasted_iota(jnp.int32, (tm, 1), 0)
    return [jnp.minimum(t + 1, w).astype(F32) for w in POOL_WINDOWS]


def _pool_mix(hn, ext, cnts, pw_ref, scale, tm):
    sums = _window_sums(ext, tm, True)
    pooled, ys = [], []
    for g in range(len(POOL_WINDOWS)):
        pg = (sums[g] / cnts[g] - hn[:, g * POOL_GROUP:(g + 1) * POOL_GROUP]).astype(BF16)
        pooled.append(pg)
        ys.append(_dot(pg, pw_ref[g]))
    y = jnp.concatenate(ys, axis=1)
    return pooled, y, y * scale


def pool_mix_fwd(h0, gpre, pool_w, scale, gpost, gffn):
    t, d = h0.shape
    tm = _tile_rows(t)

    def body(h_ref, gpre_ref, pw_ref, scale_ref, gpost_ref, gffn_ref, h1_ref, a_ref, carry):
        i = pl.program_id(0)

        @pl.when(i == 0)
        def _():
            carry[...] = jnp.zeros_like(carry)

        x = h_ref[...]
        hn = x * _rms_r(x) * gpre_ref[...]
        ext = jnp.concatenate([carry[...], hn], axis=0)
        carry[...] = hn[tm - HALO:, :]
        _, _, m = _pool_mix(hn, ext, _pool_counts(i, tm), pw_ref, scale_ref[...], tm)
        h1 = x + m * _rms_r(m) * gpost_ref[...]
        h1_ref[...] = h1
        a_ref[...] = (h1 * _rms_r(h1) * gffn_ref[...]).astype(BF16)

    return pl.pallas_call(
        functools.partial(body), name="pool_mix_fwd", grid=(t // tm,),
        in_specs=[_rows(tm, d), _const((1, d)), _const(pool_w.shape), _const((1, d)), _const((1, d)), _const((1, d))],
        out_specs=[_rows(tm, d), _rows(tm, d)],
        out_shape=[jax.ShapeDtypeStruct((t, d), F32), jax.ShapeDtypeStruct((t, d), BF16)],
        scratch_shapes=[pltpu.VMEM((HALO, d), F32)],
        compiler_params=_cparams(("arbitrary",), VMEM_LIMIT),
    )(h0, gpre, pool_w, scale, gpost, gffn)


def pool_mix_bwd(h0, dh2, da, gpre, pool_w, scale, gpost, gffn):
    t, d = h0.shape
    tm = _tile_rows(t)
    nt = t // tm
    hb = tm // HALO

    def body(h_ref, halo_ref, dh2_ref, da_ref, gpre_ref, pw_ref, scale_ref, gpost_ref, gffn_ref,
             dh0_ref, dpw_ref, gacc_ref, carry):
        i = pl.program_id(0)
        tile = nt - 1 - i
        _acc_init(gacc_ref)
        _acc_init(dpw_ref)

        @pl.when(i == 0)
        def _():
            carry[...] = jnp.zeros_like(carry)

        x = h_ref[...]
        gpre_v, scale_v, gpost_v, gffn_v = gpre_ref[...], scale_ref[...], gpost_ref[...], gffn_ref[...]
        r0 = _rms_r(x)
        hn = x * r0 * gpre_v
        xh = halo_ref[...]
        hn_halo = jnp.where(tile > 0, xh * _rms_r(xh) * gpre_v, 0.0)
        ext = jnp.concatenate([hn_halo, hn], axis=0)
        cnts = _pool_counts(tile, tm)
        pooled, y, m = _pool_mix(hn, ext, cnts, pw_ref, scale_v, tm)
        rm = _rms_r(m)
        h1 = x + m * rm * gpost_v
        dh1_n, dgffn = _rms_bwd(h1, _rms_r(h1), gffn_v, da_ref[...])
        dh1 = dh2_ref[...] + dh1_n
        dm, dgpost = _rms_bwd(m, rm, gpost_v, dh1)
        dscale = jnp.sum(dm * y, axis=0, keepdims=True)
        dy = (dm * scale_v).astype(BF16)
        dpn = []
        for g in range(len(POOL_WINDOWS)):
            dyg = dy[:, g * POOL_GROUP:(g + 1) * POOL_GROUP]
            dpw_ref[g] += _dot_tn(pooled[g], dyg)
            dpn.append(_dot_nt(dyg, pw_ref[g]))
        dpooled = jnp.concatenate(dpn, axis=1)
        dpc = jnp.concatenate([dpn[g] / cnts[g] for g in range(len(POOL_WINDOWS))], axis=1)
        ext2 = jnp.concatenate([dpc, carry[...]], axis=0)
        carry[...] = dpc[:HALO, :]
        dhn = jnp.concatenate(_window_sums(ext2, tm, False), axis=1) - dpooled
        dh0_n, dgpre = _rms_bwd(x, r0, gpre_v, dhn)
        dh0_ref[...] = dh1 + dh0_n
        gacc_ref[0:1, :] += dgpre
        gacc_ref[1:2, :] += dgpost
        gacc_ref[2:3, :] += dgffn
        gacc_ref[3:4, :] += dscale

    return pl.pallas_call(
        functools.partial(body), name="pool_mix_bwd", grid=(nt,),
        in_specs=[_rows_rev(tm, d, nt),
                  pl.BlockSpec((HALO, d), lambda i: (jnp.maximum((nt - 1 - i) * hb - 1, 0), 0)),
                  _rows_rev(tm, d, nt), _rows_rev(tm, d, nt),
                  _const((1, d)), _const(pool_w.shape), _const((1, d)), _const((1, d)), _const((1, d))],
        out_specs=[_rows_rev(tm, d, nt), _resident(pool_w.shape), _resident((8, d))],
        out_shape=[jax.ShapeDtypeStruct((t, d), F32), jax.ShapeDtypeStruct(pool_w.shape, F32),
                   jax.ShapeDtypeStruct((8, d), F32)],
        scratch_shapes=[pltpu.VMEM((HALO, d), F32)],
        compiler_params=_cparams(("arbitrary",), VMEM_LIMIT),
    )(h0, h0, dh2, da, gpre, pool_w, scale, gpost, gffn)


def _ffn_chunks(f):
    return [(c, min(c + FFN_CHUNK, f)) for c in range(0, f, FFN_CHUNK)]


def ffn_fwd(a, wg_t, wu_t, wd):
    t, d = a.shape
    f = wd.shape[0]
    tm = _tile_rows(t)

    def body(a_ref, wg_ref, wu_ref, wd_ref, f_ref, gte_ref, up_ref):
        av = a_ref[...]
        acc = jnp.zeros((tm, d), F32)
        for c0, c1 in _ffn_chunks(f):
            gte = _dot_nt(av, wg_ref[c0:c1, :])
            up = _dot_nt(av, wu_ref[c0:c1, :])
            gte_ref[:, c0:c1] = gte.astype(BF16)
            up_ref[:, c0:c1] = up.astype(BF16)
            hdn = (gte * _sigmoid(gte) * up).astype(BF16)
            acc = acc + _dot(hdn, wd_ref[c0:c1, :])
        f_ref[...] = acc

    return pl.pallas_call(
        functools.partial(body), name="ffn_fwd", grid=(t // tm,),
        in_specs=[_rows(tm, d), _const((f, d)), _const((f, d)), _const((f, d))],
        out_specs=[_rows(tm, d), _rows(tm, f), _rows(tm, f)],
        out_shape=[jax.ShapeDtypeStruct((t, d), F32), jax.ShapeDtypeStruct((t, f), BF16),
                   jax.ShapeDtypeStruct((t, f), BF16)],
        compiler_params=_cparams(("parallel",), VMEM_LIMIT),
    )(a, wg_t, wu_t, wd)


def ffn_bwd_act(df, gte, up, wg_t, wu_t, wd):
    t, d = df.shape
    f = wd.shape[0]
    tm = min(_tile_rows(t), 256)

    def body(df_ref, gte_ref, up_ref, wg_ref, wu_ref, wd_ref, da_ref, dgte_ref, dup_ref, hdn_ref):
        dfv = df_ref[...]
        acc = jnp.zeros((tm, d), F32)
        for c0, c1 in _ffn_chunks(f):
            g = gte_ref[:, c0:c1].astype(F32)
            u = up_ref[:, c0:c1].astype(F32)
            sg = _sigmoid(g)
            sl = g * sg
            hdn_ref[:, c0:c1] = (sl * u).astype(BF16)
            dh = _dot_nt(dfv, wd_ref[c0:c1, :])
            dup = (dh * sl).astype(BF16)
            dgte = (dh * u * (sg * (1.0 + g * (1.0 - sg)))).astype(BF16)
            dup_ref[:, c0:c1] = dup
            dgte_ref[:, c0:c1] = dgte
            acc = acc + _dot(dgte, wg_ref[c0:c1, :]) + _dot(dup, wu_ref[c0:c1, :])
        da_ref[...] = acc

    return pl.pallas_call(
        functools.partial(body), name="ffn_bwd_act", grid=(t // tm,),
        in_specs=[_rows(tm, d), _rows(tm, f), _rows(tm, f), _const((f, d)), _const((f, d)), _const((f, d))],
        out_specs=[_rows(tm, d), _rows(tm, f), _rows(tm, f), _rows(tm, f)],
        out_shape=[jax.ShapeDtypeStruct((t, d), F32)] + [jax.ShapeDtypeStruct((t, f), BF16)] * 3,
        compiler_params=_cparams(("parallel",), VMEM_LIMIT),
    )(df, gte, up, wg_t, wu_t, wd)


def xty(x, y):
    t, nx = x.shape
    ny = y.shape[1]
    tk = _tile_rows(t)
    bn = nx // 2 if nx > 1024 else nx
    nk = t // tk

    def body(x_ref, y_ref, o_ref, acc):
        k = pl.program_id(1)

        @pl.when(k == 0)
        def _():
            acc[...] = jnp.zeros_like(acc)

        acc[...] += _dot_tn(x_ref[...].astype(BF16), y_ref[...].astype(BF16))

        @pl.when(k == nk - 1)
        def _():
            o_ref[...] = acc[...].astype(BF16)

    return pl.pallas_call(
        functools.partial(body), name="xty", grid=(nx // bn, nk),
        in_specs=[pl.BlockSpec((tk, bn), lambda j, k: (k, j)), pl.BlockSpec((tk, ny), lambda j, k: (k, 0))],
        out_specs=pl.BlockSpec((bn, ny), lambda j, k: (j, 0)),
        out_shape=jax.ShapeDtypeStruct((nx, ny), BF16),
        scratch_shapes=[pltpu.VMEM((bn, ny), F32)],
        compiler_params=_cparams(("parallel", "arbitrary"), VMEM_LIMIT),
    )(x, y)


def _ple_fwd_tile(h1, f, p, gpost, gple, wpg_ref, wpp_ref):
    rf = _rms_r(f)
    h2 = h1 + f * rf * gpost
    r2 = _rms_r(h2)
    ub = (h2 * r2 * gple).astype(BF16)
    gate = _sigmoid(_dot(ub, wpg_ref[...]))
    pp = _dot_nt(p.astype(BF16), wpp_ref[...])
    return rf, h2, r2, ub, gate, pp


def post_ple_fwd(h1, f, p, gpost, gple, wpg, wpp_t, target=None):
    t, d = h1.shape
    pd = p.shape[1]
    tm = _tile_rows(t)
    with_loss = target is not None

    def body(*refs):
        if with_loss:
            h1_ref, f_ref, p_ref, gpost_ref, gple_ref, wpg_ref, wpp_ref, tgt_ref, out_ref, loss_ref = refs
        else:
            h1_ref, f_ref, p_ref, gpost_ref, gple_ref, wpg_ref, wpp_ref, out_ref = refs
        _, h2, _, _, gate, pp = _ple_fwd_tile(h1_ref[...], f_ref[...], p_ref[...], gpost_ref[...], gple_ref[...],
                                              wpg_ref, wpp_ref)
        h3 = h2 + pp * gate
        if with_loss:
            err = h3 - tgt_ref[...]
            out_ref[...] = err * (1.0 / d)
            colsum = jnp.sum(err * err, axis=0, keepdims=True) * (0.5 / d)
            loss_ref[...] = jnp.broadcast_to(colsum, (8, d)) * (lax.broadcasted_iota(jnp.int32, (8, d), 0) == 0)
        else:
            out_ref[...] = h3

    in_specs = [_rows(tm, d), _rows(tm, d), _rows(tm, pd), _const((1, d)), _const((1, d)), _const(wpg.shape),
                _const(wpp_t.shape)]
    out_specs = [_rows(tm, d)]
    out_shape = [jax.ShapeDtypeStruct((t, d), F32)]
    args = [h1, f, p, gpost, gple, wpg, wpp_t]
    if with_loss:
        in_specs.append(_rows(tm, d))
        out_specs.append(_rows(8, d))
        out_shape.append(jax.ShapeDtypeStruct((t // tm * 8, d), F32))
        args.append(target)
    return pl.pallas_call(
        functools.partial(body), name="post_ple_loss" if with_loss else "post_ple_fwd", grid=(t // tm,),
        in_specs=in_specs, out_specs=out_specs, out_shape=out_shape,
        compiler_params=_cparams(("parallel",), VMEM_LIMIT),
    )(*args)


def post_ple_bwd(dh3, h1, f, p, gpost, gple, wpg, wpp_t):
    t, d = h1.shape
    pd = p.shape[1]
    tm = _tile_rows(t)

    def body(dh3_ref, h1_ref, f_ref, p_ref, gpost_ref, gple_ref, wpg_ref, wpp_ref,
             dh2_ref, df_ref, u_ref, dz_ref, dpp_ref, gacc_ref):
        _acc_init(gacc_ref)
        fv = f_ref[...]
        gpost_v, gple_v = gpost_ref[...], gple_ref[...]
        rf, h2, r2, ub, gate, pp = _ple_fwd_tile(h1_ref[...], fv, p_ref[...], gpost_v, gple_v, wpg_ref, wpp_ref)
        dh3v = dh3_ref[...]
        dpp_ref[...] = (dh3v * gate).astype(BF16)
        dz = (dh3v * pp * gate * (1.0 - gate)).astype(BF16)
        dz_ref[...] = dz
        u_ref[...] = ub
        du = _dot_nt(dz, wpg_ref[...])
        dh2_n, dgple = _rms_bwd(h2, r2, gple_v, du)
        dh2 = dh3v + dh2_n
        df, dgpost = _rms_bwd(fv, rf, gpost_v, dh2)
        dh2_ref[...] = dh2
        df_ref[...] = df.astype(BF16)
        gacc_ref[0:1, :] += dgple
        gacc_ref[1:2, :] += dgpost

    return pl.pallas_call(
        functools.partial(body), name="post_ple_bwd", grid=(t // tm,),
        in_specs=[_rows(tm, d), _rows(tm, d), _rows(tm, d), _rows(tm, pd), _const((1, d)), _const((1, d)),
                  _const(wpg.shape), _const(wpp_t.shape)],
        out_specs=[_rows(tm, d)] * 5 + [_resident((8, d))],
        out_shape=[jax.ShapeDtypeStruct((t, d), F32)] + [jax.ShapeDtypeStruct((t, d), BF16)] * 4
        + [jax.ShapeDtypeStruct((8, d), F32)],
        compiler_params=_cparams(("arbitrary",), VMEM_LIMIT),
    )(dh3, h1, f, p, gpost, gple, wpg, wpp_t)


def proj_rope_fwd(h, gain, w, cos, sin, n_rope, name):
    t, d = h.shape
    n = w.shape[1]
    tm = _tile_rows(t)

    def body(h_ref, g_ref, w_ref, cos_ref, sin_ref, hn_ref, y_ref):
        x = h_ref[...]
        hn = (x * _rms_r(x) * g_ref[...]).astype(BF16)
        hn_ref[...] = hn
        y = _dot(hn, w_ref[...])
        y_ref[:, :n_rope] = _rope(y[:, :n_rope], cos_ref[...], sin_ref[...]).astype(BF16)
        if n_rope < n:
            y_ref[:, n_rope:] = y[:, n_rope:].astype(BF16)

    return pl.pallas_call(
        functools.partial(body), name=name, grid=(t // tm,),
        in_specs=[_rows(tm, d), _const((1, d)), _const(w.shape), _rows(tm, LANES), _rows(tm, LANES)],
        out_specs=[_rows(tm, d), _rows(tm, n)],
        out_shape=[jax.ShapeDtypeStruct((t, d), BF16), jax.ShapeDtypeStruct((t, n), BF16)],
        compiler_params=_cparams(("parallel",), VMEM_LIMIT),
    )(h, gain, w, cos, sin)


def proj_rope_bwd(dh1, h0, cos, sin, branches, name):
    t, d = h0.shape
    tm = _tile_rows(t)
    nb = len(branches)
    n_cot = [len(b[3]) for b in branches]

    def body(*refs):
        dh1_ref, h0_ref, cos_ref, sin_ref = refs[:4]
        pos = 4
        br_refs = []
        for b in range(nb):
            br_refs.append((refs[pos], refs[pos + 1], refs[pos + 2:pos + 2 + n_cot[b]]))
            pos += 2 + n_cot[b]
        dh0_ref = refs[pos]
        dpre_refs = refs[pos + 1:pos + 1 + nb]
        gacc_ref = refs[pos + 1 + nb]
        _acc_init(gacc_ref)
        x = h0_ref[...]
        r0 = _rms_r(x)
        dh = dh1_ref[...]
        for b in range(nb):
            g_ref, w_ref, cot_refs = br_refs[b]
            n_rope = branches[b][2]
            dy = cot_refs[0][...].astype(F32)
            for c_ref in cot_refs[1:]:
                dy = dy + c_ref[...].astype(F32)
            n = dy.shape[1]
            dpre_refs[b][:, :n_rope] = _unrope(dy[:, :n_rope], cos_ref[...], sin_ref[...]).astype(BF16)
            if n_rope < n:
                dpre_refs[b][:, n_rope:] = dy[:, n_rope:].astype(BF16)
            dhn = _dot_nt(dpre_refs[b][...], w_ref[...])
            dx, dg = _rms_bwd(x, r0, g_ref[...], dhn)
            dh = dh + dx
            gacc_ref[b:b + 1, :] += dg
        dh0_ref[...] = dh

    in_specs = [_rows(tm, d), _rows(tm, d), _rows(tm, LANES), _rows(tm, LANES)]
    args = [dh1, h0, cos, sin]
    out_specs = [_rows(tm, d)]
    out_shape = [jax.ShapeDtypeStruct((t, d), F32)]
    for gain, w, _, cots in branches:
        n = w.shape[1]
        in_specs += [_const((1, d)), _const(w.shape)] + [_rows(tm, n)] * len(cots)
        args += [gain, w] + list(cots)
        out_specs.append(_rows(tm, n))
        out_shape.append(jax.ShapeDtypeStruct((t, n), BF16))
    out_specs.append(_resident((8, d)))
    out_shape.append(jax.ShapeDtypeStruct((8, d), F32))
    return pl.pallas_call(
        functools.partial(body), name=name, grid=(t // tm,),
        in_specs=in_specs, out_specs=out_specs, out_shape=out_shape,
        compiler_params=_cparams(("arbitrary",), VMEM_LIMIT),
    )(*args)


def _tri():
    row = lax.broadcasted_iota(jnp.int32, (BLOCK, BLOCK), 0)
    col = lax.broadcasted_iota(jnp.int32, (BLOCK, BLOCK), 1)
    return col <= row


def _block_diag(x):
    lo = lax.broadcasted_iota(jnp.int32, x.shape, 1) < HEAD_DIM
    zero = jnp.zeros_like(x)
    return jnp.concatenate([jnp.where(lo, x, zero), jnp.where(lo, zero, x)], axis=0)


def _dense(x, tri):
    return (jnp.where(tri, x[:, BLOCK:2 * BLOCK], x[:, :BLOCK]),
            jnp.where(tri, x[:, 3 * BLOCK:], x[:, 2 * BLOCK:3 * BLOCK]))


def _banded(xa, xb, tri):
    zero = jnp.zeros_like(xa)
    return jnp.concatenate([jnp.where(tri, zero, xa), jnp.where(tri, xa, zero),
                            jnp.where(tri, zero, xb), jnp.where(tri, xb, zero)], axis=1).astype(BF16)


def _softmax_sink(s, sink):
    mx = jnp.maximum(jnp.max(s, axis=1, keepdims=True), sink)
    e = jnp.exp(s - mx)
    es = jnp.exp(sink - mx)
    inv = 1.0 / (jnp.sum(e, axis=1, keepdims=True) + es)
    return e * inv, es * inv


def _sink_column(sink_ref):
    return jnp.concatenate([jnp.broadcast_to(sink_ref[h:h + 1, 0:1], (BLOCK, 1)) for h in range(N_HEADS)], axis=0)


def _kv_block_diag(band, kvw):
    n_lt = kvw // LANES
    return ([_block_diag(band[:, lt * LANES:(lt + 1) * LANES]) for lt in range(n_lt)],
            [_block_diag(band[:, kvw + lt * LANES:kvw + (lt + 1) * LANES]) for lt in range(n_lt)])


def _all_probs(q_ref, kbd, tri, n, sink_ref):
    dense = []
    for tq in range(N_HEADS // 2):
        s = _dot_nt(q_ref[:, tq * LANES:(tq + 1) * LANES], kbd[tq // GQA])
        dense += list(_dense(s, tri))
    bias = jnp.where(jnp.logical_not(tri) & (n == 0), NEG_INF, 0.0)
    s_all = jnp.concatenate(dense, axis=0) * (HEAD_DIM ** -0.5) + jnp.concatenate([bias] * N_HEADS, axis=0)
    return _softmax_sink(s_all, _sink_column(sink_ref))


def _head_rows(x, tq):
    return x[2 * tq * BLOCK:(2 * tq + 1) * BLOCK], x[(2 * tq + 1) * BLOCK:(2 * tq + 2) * BLOCK]


def swa_fwd(q, kv, sink_b):
    t, d = q.shape
    nb = t // BLOCK
    kvw = N_KV_HEADS * HEAD_DIM

    def body(q_ref, kvc_ref, kvp_ref, sink_ref, o_ref):
        n = pl.program_id(0)
        tri = _tri()
        kbd, vbd = _kv_block_diag(jnp.concatenate([kvp_ref[...], kvc_ref[...]], axis=0), kvw)
        p, _ = _all_probs(q_ref, kbd, tri, n, sink_ref)
        for tq in range(N_HEADS // 2):
            pa, pb = _head_rows(p, tq)
            o_ref[:, tq * LANES:(tq + 1) * LANES] = _dot(_banded(pa, pb, tri), vbd[tq // GQA]).astype(BF16)

    return pl.pallas_call(
        functools.partial(body), name="swa_fwd", grid=(nb,),
        in_specs=[_rows(BLOCK, d), _rows(BLOCK, 2 * kvw),
                  pl.BlockSpec((BLOCK, 2 * kvw), lambda n: (jnp.maximum(n - 1, 0), 0)), _const(sink_b.shape)],
        out_specs=_rows(BLOCK, d),
        out_shape=jax.ShapeDtypeStruct((t, d), BF16),
        compiler_params=_cparams(("parallel",), VMEM_LIMIT),
    )(q, kv, kv, sink_b)


def swa_bwd(q, kv, do, sink_b):
    t, d = q.shape
    nb = t // BLOCK
    kvw = N_KV_HEADS * HEAD_DIM

    def body(q_ref, do_ref, kvc_ref, kvp_ref, sink_ref, dq_ref, dkv_ref, dsink_ref, carry):
        i = pl.program_id(0)
        n = nb - 1 - i
        _acc_init(dsink_ref)

        @pl.when(i == 0)
        def _():
            carry[...] = jnp.zeros_like(carry)

        tri = _tri()
        lo = lax.broadcasted_iota(jnp.int32, (2 * BLOCK, LANES), 1) < HEAD_DIM
        kbd, vbd = _kv_block_diag(jnp.concatenate([kvp_ref[...], kvc_ref[...]], axis=0), kvw)
        p, ps = _all_probs(q_ref, kbd, tri, n, sink_ref)
        dp = []
        for tq in range(N_HEADS // 2):
            dp += list(_dense(_dot_nt(do_ref[:, tq * LANES:(tq + 1) * LANES], vbd[tq // GQA]), tri))
        dp = jnp.concatenate(dp, axis=0)
        delta = jnp.sum(p * dp, axis=1, keepdims=True)
        ds = p * (dp - delta) * (HEAD_DIM ** -0.5)
        dsk = ps * delta
        for h in range(N_HEADS):
            dsink_ref[h:h + 1, :] -= jnp.sum(dsk[h * BLOCK:(h + 1) * BLOCK], axis=0, keepdims=True)
        dkb = [jnp.zeros((4 * BLOCK, LANES), F32) for _ in kbd]
        dvb = [jnp.zeros((4 * BLOCK, LANES), F32) for _ in kbd]
        for tq in range(N_HEADS // 2):
            lt = tq // GQA
            cols = slice(tq * LANES, (tq + 1) * LANES)
            dsb = _banded(*_head_rows(ds, tq), tri)
            dq_ref[:, cols] = _dot(dsb, kbd[lt]).astype(BF16)
            dkb[lt] = dkb[lt] + _dot_tn(dsb, q_ref[:, cols])
            dvb[lt] = dvb[lt] + _dot_tn(_banded(*_head_rows(p, tq), tri), do_ref[:, cols])
        dall = jnp.concatenate([jnp.where(lo, x[:2 * BLOCK], x[2 * BLOCK:]) for x in dkb + dvb], axis=1)
        dkv_ref[...] = dall[BLOCK:, :] + carry[...]
        carry[...] = dall[:BLOCK, :]

    rev = lambda i: (nb - 1 - i, 0)
    return pl.pallas_call(
        functools.partial(body), name="swa_bwd", grid=(nb,),
        in_specs=[pl.BlockSpec((BLOCK, d), rev), pl.BlockSpec((BLOCK, d), rev), pl.BlockSpec((BLOCK, 2 * kvw), rev),
                  pl.BlockSpec((BLOCK, 2 * kvw), lambda i: (jnp.maximum(nb - 2 - i, 0), 0)), _const(sink_b.shape)],
        out_specs=[pl.BlockSpec((BLOCK, d), rev), pl.BlockSpec((BLOCK, 2 * kvw), rev), _resident(sink_b.shape)],
        out_shape=[jax.ShapeDtypeStruct((t, d), BF16), jax.ShapeDtypeStruct((t, 2 * kvw), F32),
                   jax.ShapeDtypeStruct(sink_b.shape, F32)],
        scratch_shapes=[pltpu.VMEM((BLOCK, 2 * kvw), F32)],
        compiler_params=_cparams(("arbitrary",), VMEM_LIMIT),
    )(q, do, kv, kv, sink_b)


def oproj_post_fwd(attn, w_o, h0, gpost, gffn):
    t, d = h0.shape
    tm = _tile_rows(t)

    def body(at_ref, w_ref, h0_ref, gpost_ref, gffn_ref, m_ref, h1_ref, a_ref):
        m = _dot(at_ref[...], w_ref[...])
        m_ref[...] = m
        h1 = h0_ref[...] + m * _rms_r(m) * gpost_ref[...]
        h1_ref[...] = h1
        a_ref[...] = (h1 * _rms_r(h1) * gffn_ref[...]).astype(BF16)

    return pl.pallas_call(
        functools.partial(body), name="oproj_post_fwd", grid=(t // tm,),
        in_specs=[_rows(tm, d), _const(w_o.shape), _rows(tm, d), _const((1, d)), _const((1, d))],
        out_specs=[_rows(tm, d)] * 3,
        out_shape=[jax.ShapeDtypeStruct((t, d), F32), jax.ShapeDtypeStruct((t, d), F32),
                   jax.ShapeDtypeStruct((t, d), BF16)],
        compiler_params=_cparams(("parallel",), VMEM_LIMIT),
    )(attn, w_o, h0, gpost, gffn)


def oproj_post_bwd(dh2, da, h1, m, w_o, gpost, gffn):
    t, d = h1.shape
    tm = _tile_rows(t)

    def body(dh2_ref, da_ref, h1_ref, m_ref, w_ref, gpost_ref, gffn_ref, dh1_ref, dm_ref, dat_ref, gacc_ref):
        _acc_init(gacc_ref)
        h1v, mv = h1_ref[...], m_ref[...]
        dh1_n, dgffn = _rms_bwd(h1v, _rms_r(h1v), gffn_ref[...], da_ref[...])
        dh1 = dh2_ref[...] + dh1_n
        dm, dgpost = _rms_bwd(mv, _rms_r(mv), gpost_ref[...], dh1)
        dmb = dm.astype(BF16)
        dh1_ref[...] = dh1
        dm_ref[...] = dmb
        dat_ref[...] = _dot_nt(dmb, w_ref[...]).astype(BF16)
        gacc_ref[0:1, :] += dgpost
        gacc_ref[1:2, :] += dgffn

    return pl.pallas_call(
        functools.partial(body), name="oproj_post_bwd", grid=(t // tm,),
        in_specs=[_rows(tm, d)] * 4 + [_const(w_o.shape), _const((1, d)), _const((1, d))],
        out_specs=[_rows(tm, d)] * 3 + [_resident((8, d))],
        out_shape=[jax.ShapeDtypeStruct((t, d), F32), jax.ShapeDtypeStruct((t, d), BF16),
                   jax.ShapeDtypeStruct((t, d), BF16), jax.ShapeDtypeStruct((8, d), F32)],
        compiler_params=_cparams(("arbitrary",), VMEM_LIMIT),
    )(dh2, da, h1, m, w_o, gpost, gffn)


def _my_place():
    return lax.axis_index("x"), lax.axis_index("y"), lax.axis_index("c")


def _block_index(px, py, pc):
    return 4 * px + 2 * py + pc


def allgather_pieces(shards, name):
    np_ = len(shards)

    def body(*refs):
        in_refs, out_refs = refs[:np_], refs[np_:2 * np_]
        send_sems, recv_sems, local_sems = refs[2 * np_:]
        x, y, c = _my_place()
        me, sibling = (x, y, c), (x, y, 1 - c)
        chips = [(1 - x, y), (x, 1 - y), (1 - x, 1 - y)]

        def rows(p, place):
            r = in_refs[p].shape[0]
            return out_refs[p].at[pl.ds(_block_index(*place) * r, r), :]

        def copy(p, k, block, to, src=None):
            return pltpu.make_async_remote_copy(
                src_ref=rows(p, block) if src is None else src, dst_ref=rows(p, block),
                send_sem=send_sems.at[p, k], recv_sem=recv_sems.at[p, k], device_id=to, device_id_type=MESH)

        mine = [pltpu.make_async_copy(in_refs[p], rows(p, me), local_sems.at[p]) for p in range(np_)]
        first, passed = [], []
        for p in range(np_):
            mine[p].start()
            first.append(copy(p, 0, me, sibling, src=in_refs[p]))
            first += [copy(p, 1 + j, me, (*chip, c), src=in_refs[p]) for j, chip in enumerate(chips)]
        for cp in first:
            cp.start()
        for p in range(np_):
            for j, chip in enumerate(chips):
                copy(p, 1 + j, (*chip, c), me).wait_recv()
                fwd = copy(p, 4 + j, (*chip, c), sibling)
                fwd.start()
                passed.append(fwd)
        for p in range(np_):
            copy(p, 0, sibling, me).wait_recv()
            for j, chip in enumerate(chips):
                copy(p, 4 + j, (*chip, 1 - c), me).wait_recv()
        for cp in first + passed:
            cp.wait_send()
        for cp in mine:
            cp.wait()

    return pl.pallas_call(
        functools.partial(body), name=name,
        in_specs=[ANY] * np_, out_specs=[ANY] * np_,
        out_shape=[jax.ShapeDtypeStruct((N_DEV * s.shape[0], s.shape[1]), s.dtype) for s in shards],
        scratch_shapes=[pltpu.SemaphoreType.DMA((np_, 7)), pltpu.SemaphoreType.DMA((np_, 7)),
                        pltpu.SemaphoreType.DMA((np_,))],
    )(*shards)


def _peers():
    x, y, c = _my_place()
    flips = [(fx, fy, fc) for fx in (0, 1) for fy in (0, 1) for fc in (0, 1)][1:]
    return [(1 - x if fx else x, 1 - y if fy else y, 1 - c if fc else c) for fx, fy, fc in flips]


def reduce_scatter_pieces(fulls, name):
    np_ = len(fulls)

    def body(*refs):
        in_refs, out_refs = refs[:np_], refs[np_:2 * np_]
        send_sems, recv_sems, local_sems = refs[2 * np_:]
        me = _my_place()
        my_block = _block_index(*me)
        peers = _peers()

        def copy(p, k, to):
            r = out_refs[p].shape[1]
            return pltpu.make_async_remote_copy(
                src_ref=in_refs[p].at[pl.ds(_block_index(*to) * r, r), :], dst_ref=out_refs[p].at[my_block],
                send_sem=send_sems.at[p, k], recv_sem=recv_sems.at[p, k], device_id=to, device_id_type=MESH)

        def landed(p, k, frm):
            r = out_refs[p].shape[1]
            return pltpu.make_async_remote_copy(
                src_ref=in_refs[p].at[pl.ds(0, r), :], dst_ref=out_refs[p].at[_block_index(*frm)],
                send_sem=send_sems.at[p, k], recv_sem=recv_sems.at[p, k], device_id=frm, device_id_type=MESH)

        mine, sends = [], []
        for p in range(np_):
            r = out_refs[p].shape[1]
            own = pltpu.make_async_copy(in_refs[p].at[pl.ds(my_block * r, r), :], out_refs[p].at[my_block],
                                        local_sems.at[p])
            own.start()
            mine.append(own)
            for k, peer in enumerate(peers):
                cp = copy(p, k, peer)
                cp.start()
                sends.append(cp)
        for p in range(np_):
            for k, peer in enumerate(peers):
                landed(p, k, peer).wait_recv()
        for cp in sends:
            cp.wait_send()
        for cp in mine:
            cp.wait()

    return pl.pallas_call(
        functools.partial(body), name=name,
        in_specs=[ANY] * np_, out_specs=[ANY] * np_,
        out_shape=[jax.ShapeDtypeStruct((N_DEV, g.shape[0] // N_DEV, g.shape[1]), g.dtype) for g in fulls],
        scratch_shapes=[pltpu.SemaphoreType.DMA((np_, 7)), pltpu.SemaphoreType.DMA((np_, 7)),
                        pltpu.SemaphoreType.DMA((np_,))],
    )(*fulls)


def allreduce_small(pack):
    r, c = pack.shape

    def body(pack_ref, out_ref, gathered, send_sems, recv_sems):
        me = _my_place()
        my_block = _block_index(*me)
        peers = _peers()

        def copy(k, slot, to):
            return pltpu.make_async_remote_copy(
                src_ref=pack_ref, dst_ref=gathered.at[slot], send_sem=send_sems.at[k], recv_sem=recv_sems.at[k],
                device_id=to, device_id_type=MESH)

        sends = [copy(k, my_block, peer) for k, peer in enumerate(peers)]
        for cp in sends:
            cp.start()
        gathered[my_block] = pack_ref[...]
        for k, peer in enumerate(peers):
            copy(k, _block_index(*peer), peer).wait_recv()
        for cp in sends:
            cp.wait_send()
        total = gathered[0]
        for j in range(1, N_DEV):
            total = total + gathered[j]
        out_ref[...] = total

    return pl.pallas_call(
        functools.partial(body), name="allreduce_small",
        in_specs=[pl.BlockSpec(memory_space=pltpu.VMEM)], out_specs=pl.BlockSpec(memory_space=pltpu.VMEM),
        out_shape=jax.ShapeDtypeStruct((r, c), F32),
        scratch_shapes=[pltpu.VMEM((N_DEV, r, c), F32), pltpu.SemaphoreType.DMA((7,)), pltpu.SemaphoreType.DMA((7,))],
    )(pack)


def sum_parts(parts):
    n, r, c = parts.shape
    br = 256 if r % 256 == 0 else r

    def body(p_ref, g_ref):
        g = p_ref[0].astype(F32)
        for j in range(1, n):
            g = g + p_ref[j].astype(F32)
        g_ref[...] = g

    return pl.pallas_call(
        functools.partial(body), name="sum_parts", grid=(r // br,),
        in_specs=[pl.BlockSpec((n, br, c), lambda i: (0, i, 0))], out_specs=_rows(br, c),
        out_shape=jax.ShapeDtypeStruct((r, c), F32),
        compiler_params=_cparams(("parallel",)),
    )(parts)


def adamw(w, m, v, parts):
    r, c = w.shape
    n = parts.shape[0]
    br = 256 if r % 256 == 0 else r

    def body(w_ref, m_ref, v_ref, p_ref, g_ref, d_ref, nm_ref, nv_ref):
        g = p_ref[0].astype(F32)
        for j in range(1, n):
            g = g + p_ref[j].astype(F32)
        nm = ADAM_B1 * m_ref[...] + (1.0 - ADAM_B1) * g
        nv = ADAM_B2 * v_ref[...] + (1.0 - ADAM_B2) * (g * g)
        m_hat = nm / (1.0 - ADAM_B1 ** ADAM_STEP)
        v_hat = nv / (1.0 - ADAM_B2 ** ADAM_STEP)
        g_ref[...] = g
        d_ref[...] = -ADAM_LR * (m_hat / (jnp.sqrt(v_hat) + ADAM_EPS) + ADAM_WD * w_ref[...])
        nm_ref[...] = nm
        nv_ref[...] = nv

    return pl.pallas_call(
        functools.partial(body), name="adamw", grid=(r // br,),
        in_specs=[_rows(br, c)] * 3 + [pl.BlockSpec((n, br, c), lambda i: (0, i, 0))],
        out_specs=[_rows(br, c)] * 4, out_shape=[jax.ShapeDtypeStruct((r, c), F32)] * 4,
        compiler_params=_cparams(("parallel",)),
    )(w, m, v, parts)


def _adamw_nd(w, m, v, g):
    shp = w.shape
    c = shp[-1]
    flat = lambda a: a.reshape(-1, c)
    outs = adamw(flat(w), flat(m), flat(v), flat(g)[None])
    return [o.reshape(shp) for o in outs]


def _pair_heads(a, axis, width=HEAD_DIM):
    shp = a.shape
    a = a.reshape(shp[:axis] + (2, 2, GQA, width) + shp[axis + 1:])
    return jnp.swapaxes(a, axis + 1, axis + 2).reshape(shp)


def _unpair_heads(a, axis, width=HEAD_DIM):
    shp = a.shape
    a = a.reshape(shp[:axis] + (2, GQA, 2, width) + shp[axis + 1:])
    return jnp.swapaxes(a, axis + 1, axis + 2).reshape(shp)


def _pad_rows(a, rows=8):
    return jnp.pad(a, ((0, rows - a.shape[0]), (0, 0)))


def kernel(x, p, mix_pre_g, mix_post_g, ffn_pre_g, ffn_post_g, pool_w, pool_scale, kv_norm_g, w_k, w_v, w_q, w_o, sinks, w_ff_gate, w_ff_up, w_ff_down, ple_norm_g, w_ple_gate, w_ple_proj, loss_target, m_mix_pre_g, m_mix_post_g, m_ffn_pre_g, m_ffn_post_g, m_pool_w, m_pool_scale, m_kv_norm_g, m_w_k, m_w_v, m_w_q, m_w_o, m_sinks, m_w_ff_gate, m_w_ff_up, m_w_ff_down, m_ple_norm_g, m_w_ple_gate, m_w_ple_proj, v_mix_pre_g, v_mix_post_g, v_ffn_pre_g, v_ffn_post_g, v_pool_w, v_pool_scale, v_kv_norm_g, v_w_k, v_w_v, v_w_q, v_w_o, v_sinks, v_w_ff_gate, v_w_ff_up, v_w_ff_down, v_ple_norm_g, v_w_ple_gate, v_w_ple_proj):
    depth = w_ff_gate.shape[0]
    n_a = pool_w.shape[0]
    t, d = x.shape[1], x.shape[2]
    h = x[0]
    tgt = loss_target[0]
    my_block = _block_index(*_my_place())
    row = lambda g, i: g[i][None, :]
    bf = lambda a: a.astype(BF16)

    full = []
    for i in range(depth):
        shards = [bf(w_ff_gate[i].T), bf(w_ff_up[i].T), bf(w_ff_down[i]), bf(w_ple_gate[i]), bf(w_ple_proj[i].T)]
        if i < n_a:
            shards.append(bf(pool_w[i].reshape(-1, POOL_GROUP)))
        else:
            shards += [bf(_pair_heads(w_q[i - n_a], 1)), bf(w_o[i - n_a])]
            if i == n_a:
                shards.append(bf(jnp.concatenate([w_k, w_v], axis=1)))
        full.append(allgather_pieces(shards, f"allgather_l{i}"))
    scale_full = allgather_pieces([_pad_rows(pool_scale)], "allgather_scale")[0]
    scale_full = scale_full.reshape(N_DEV, 8, -1)[:, :n_a].transpose(1, 0, 2).reshape(n_a, 1, d)
    pool_full = [full[i][5].reshape(N_DEV, len(POOL_WINDOWS), -1, POOL_GROUP).transpose(1, 0, 2, 3)
                 .reshape(len(POOL_WINDOWS), POOL_GROUP, POOL_GROUP) for i in range(n_a)]

    cos, sin = _rope_tables(t)
    sink_b = [jnp.broadcast_to(_pair_heads(sinks[j][:, None], 0, 1), (N_HEADS, LANES)) for j in range(depth - n_a)]
    wo_full = {i: _pair_heads(full[i][6], 0) for i in range(n_a, depth)}

    saved = []
    kv = hk = None
    for i in range(depth):
        wg_t, wu_t, wd, wpg, wpp_t = full[i][:5]
        s = {"h0": h}
        if i < n_a:
            h1, a = pool_mix_fwd(h, row(mix_pre_g, i), pool_full[i], scale_full[i], row(mix_post_g, i),
                                 row(ffn_pre_g, i))
        else:
            j = i - n_a
            if i == n_a:
                hk, kv = proj_rope_fwd(h, kv_norm_g[None, :], full[i][7], cos, sin, N_KV_HEADS * HEAD_DIM, "kv_proj_fwd")
            hn, q = proj_rope_fwd(h, row(mix_pre_g, i), full[i][5], cos, sin, d, "q_proj_fwd")
            attn = swa_fwd(q, kv, sink_b[j])
            m, h1, a = oproj_post_fwd(attn, wo_full[i], h, row(mix_post_g, i), row(ffn_pre_g, i))
            s.update(hn=hn, q=q, attn=attn, m=m)
        f, gte, up = ffn_fwd(a, wg_t, wu_t, wd)
        s.update(h1=h1, a=a, f=f, gte=gte, up=up)
        if i < depth - 1:
            h = post_ple_fwd(h1, f, p[i, 0], row(ffn_post_g, i), row(ple_norm_g, i), wpg, wpp_t)[0]
        else:
            dh, loss_rows = post_ple_fwd(h1, f, p[i, 0], row(ffn_post_g, i), row(ple_norm_g, i), wpg, wpp_t, target=tgt)
        saved.append(s)

    zero_row = jnp.zeros((1, d), F32)
    g_mix_pre, g_mix_post, g_ffn_pre, g_ffn_post, g_ple = ([None] * depth for _ in range(5))
    g_kv = g_sinks = None
    g_scale = [None] * n_a
    landing = [None] * depth
    dkv_sum = []
    for i in reversed(range(depth)):
        s = saved[i]
        wg_t, wu_t, wd, wpg, wpp_t = full[i][:5]
        dh2, df, ub, dzb, dppb, gacc = post_ple_bwd(dh, s["h1"], s["f"], p[i, 0], row(ffn_post_g, i),
                                                    row(ple_norm_g, i), wpg, wpp_t)
        g_ple[i], g_ffn_post[i] = gacc[0], gacc[1]
        da, dgte, dup, hdn = ffn_bwd_act(df, s["gte"], s["up"], wg_t, wu_t, wd)
        grads = [xty(dgte, s["a"]), xty(dup, s["a"]), xty(hdn, df), xty(ub, dzb), xty(dppb, p[i, 0])]
        if i < n_a:
            dh, dpw, gacc = pool_mix_bwd(s["h0"], dh2, da, row(mix_pre_g, i), pool_full[i], scale_full[i],
                                         row(mix_post_g, i), row(ffn_pre_g, i))
            g_mix_pre[i], g_mix_post[i], g_ffn_pre[i], g_scale[i] = gacc[0], gacc[1], gacc[2], gacc[3]
            dpw = dpw.reshape(len(POOL_WINDOWS), N_DEV, -1, POOL_GROUP).transpose(1, 0, 2, 3)
            grads.append(bf(dpw.reshape(-1, POOL_GROUP)))
        else:
            j = i - n_a
            dh1, dmb, dattn, gacc = oproj_post_bwd(dh2, da, s["h1"], s["m"], wo_full[i], row(mix_post_g, i),
                                                   row(ffn_pre_g, i))
            g_mix_post[i], g_ffn_pre[i] = gacc[0], gacc[1]
            dq, dkv, dsink = swa_bwd(s["q"], kv, dattn, sink_b[j])
            dkv_sum.append(dkv)
            g_sinks = [_unpair_heads(dsink[:, 0:1], 0, 1)[:, 0]] + (g_sinks or [])
            branches = [(row(mix_pre_g, i), full[i][5], d, [dq])]
            if i == n_a:
                branches.append((kv_norm_g[None, :], full[i][7], N_KV_HEADS * HEAD_DIM, dkv_sum))
            outs = proj_rope_bwd(dh1, s["h0"], cos, sin, branches, f"proj_bwd_l{i}")
            dh, gacc = outs[0], outs[-1]
            g_mix_pre[i] = gacc[0]
            grads += [xty(s["hn"], outs[1]), _unpair_heads(xty(s["attn"], dmb), 0)]
            if i == n_a:
                g_kv = gacc[1]
                grads.append(xty(hk, outs[2]))
        landing[i] = reduce_scatter_pieces(grads, f"reduce_scatter_l{i}")
    grad_x = dh[None]

    loss_row = jnp.sum(loss_rows, axis=0, keepdims=True)
    sink_row = jnp.pad(jnp.concatenate(g_sinks)[None, :], ((0, 0), (0, d - sinks.size)))
    stack = lambda rows_: _pad_rows(jnp.stack(rows_))
    pack = jnp.concatenate([stack(g_mix_pre), stack(g_mix_post), stack(g_ffn_pre), stack(g_ffn_post), stack(g_ple),
                            _pad_rows(g_kv[None]), stack(g_scale), _pad_rows(sink_row), _pad_rows(loss_row)], axis=0)
    tot = allreduce_small(pack)
    sec = lambda k, n: tot[8 * k:8 * k + n]
    loss = jnp.sum(tot[64])
    small = {
        "mix_pre_g": sec(0, depth), "mix_post_g": sec(1, depth), "ffn_pre_g": sec(2, depth),
        "ffn_post_g": sec(3, depth), "ple_norm_g": sec(4, depth), "kv_norm_g": tot[40],
        "pool_scale": lax.dynamic_slice_in_dim(sec(6, n_a), my_block * pool_scale.shape[1], pool_scale.shape[1], axis=1),
        "sinks": tot[56, :sinks.size].reshape(sinks.shape),
    }

    weights = dict(mix_pre_g=mix_pre_g, mix_post_g=mix_post_g, ffn_pre_g=ffn_pre_g, ffn_post_g=ffn_post_g, pool_w=pool_w, pool_scale=pool_scale, kv_norm_g=kv_norm_g, w_k=w_k, w_v=w_v, w_q=w_q, w_o=w_o, sinks=sinks, w_ff_gate=w_ff_gate, w_ff_up=w_ff_up, w_ff_down=w_ff_down, ple_norm_g=ple_norm_g, w_ple_gate=w_ple_gate, w_ple_proj=w_ple_proj)
    mom1 = dict(mix_pre_g=m_mix_pre_g, mix_post_g=m_mix_post_g, ffn_pre_g=m_ffn_pre_g, ffn_post_g=m_ffn_post_g, pool_w=m_pool_w, pool_scale=m_pool_scale, kv_norm_g=m_kv_norm_g, w_k=m_w_k, w_v=m_w_v, w_q=m_w_q, w_o=m_w_o, sinks=m_sinks, w_ff_gate=m_w_ff_gate, w_ff_up=m_w_ff_up, w_ff_down=m_w_ff_down, ple_norm_g=m_ple_norm_g, w_ple_gate=m_w_ple_gate, w_ple_proj=m_w_ple_proj)
    mom2 = dict(mix_pre_g=v_mix_pre_g, mix_post_g=v_mix_post_g, ffn_pre_g=v_ffn_pre_g, ffn_post_g=v_ffn_post_g, pool_w=v_pool_w, pool_scale=v_pool_scale, kv_norm_g=v_kv_norm_g, w_k=v_w_k, w_v=v_w_v, w_q=v_w_q, w_o=v_w_o, sinks=v_sinks, w_ff_gate=v_w_ff_gate, w_ff_up=v_w_ff_up, w_ff_down=v_w_ff_down, ple_norm_g=v_ple_norm_g, w_ple_gate=v_w_ple_gate, w_ple_proj=v_w_ple_proj)

    def land(i, k):
        return sum_parts(landing[i][k])

    gw = dict(small)
    gw["kv_norm_g"] = small["kv_norm_g"]
    gw["w_ff_gate"] = jnp.stack([land(i, 0).T for i in range(depth)])
    gw["w_ff_up"] = jnp.stack([land(i, 1).T for i in range(depth)])
    gw["w_ff_down"] = jnp.stack([land(i, 2) for i in range(depth)])
    gw["w_ple_gate"] = jnp.stack([land(i, 3) for i in range(depth)])
    gw["w_ple_proj"] = jnp.stack([land(i, 4).T for i in range(depth)])
    gw["pool_w"] = jnp.stack([land(i, 5).reshape(pool_w.shape[1:]) for i in range(n_a)])
    gw["w_q"] = jnp.stack([_unpair_heads(land(i, 5), 1) for i in range(n_a, depth)])
    gw["w_o"] = jnp.stack([land(i, 6) for i in range(n_a, depth)])
    gkv = land(n_a, 7)
    gw["w_k"], gw["w_v"] = gkv[:, :w_k.shape[1]], gkv[:, w_k.shape[1]:]

    order = ["mix_pre_g", "mix_post_g", "ffn_pre_g", "ffn_post_g", "pool_w", "pool_scale", "kv_norm_g", "w_k", "w_v",
             "w_q", "w_o", "sinks", "w_ff_gate", "w_ff_up", "w_ff_down", "ple_norm_g", "w_ple_gate", "w_ple_proj"]
    g_out, d_out, m_out, v_out = [], [], [], []
    for nme in order:
        w = weights[nme]
        as2d = (lambda a: a[None, :]) if w.ndim == 1 else (lambda a: a)
        g, dl, nm, nv = _adamw_nd(as2d(w), as2d(mom1[nme]), as2d(mom2[nme]), as2d(gw[nme]))
        for lst, val in ((g_out, g), (d_out, dl), (m_out, nm), (v_out, nv)):
            lst.append(val.reshape(w.shape))
    return (loss, grad_x, *g_out, *d_out, *m_out, *v_out)
```

```python
import functools

import jax
import jax.numpy as jnp
from jax import lax
from jax.experimental import pallas as pl
from jax.experimental.pallas import tpu as pltpu

F32 = jnp.float32
BF16 = jnp.bfloat16

N_DEV = 8
HEAD_DIM = 64
N_HEADS = 16
N_KV_HEADS = 4
GQA = N_HEADS // N_KV_HEADS
BLOCK = 128
POOL_WINDOWS = (2, 4, 8, 16)
POOL_GROUP = 256
HALO = 16
ROPE_THETA = 10000.0
RMS_EPS = 1e-6
NEG_INF = -1e30
LANES = 128
FFN_CHUNK = 768
VMEM_LIMIT = 56 * 1024 * 1024

ADAM_LR = 0.001
ADAM_B1 = 0.9
ADAM_B2 = 0.999
ADAM_EPS = 1e-08
ADAM_WD = 0.01
ADAM_STEP = 10

MESH = pl.DeviceIdType.MESH
ANY = pl.BlockSpec(memory_space=pl.ANY)

NT_DIMS = (((1,), (1,)), ((), ()))
TN_DIMS = (((0,), (0,)), ((), ()))


def _cparams(sem=None, vmem=None):
    kw = {}
    if sem is not None:
        kw["dimension_semantics"] = sem
    if vmem is not None:
        kw["vmem_limit_bytes"] = vmem
    return pltpu.CompilerParams(**kw)


def _rows(tm, n):
    return pl.BlockSpec((tm, n), lambda i: (i, 0))


def _rows_rev(tm, n, nt):
    return pl.BlockSpec((tm, n), lambda i: (nt - 1 - i, 0))


def _const(shape):
    nd = len(shape)
    return pl.BlockSpec(shape, lambda *_: (0,) * nd, pipeline_mode=pl.Buffered(1))


def _resident(shape):
    nd = len(shape)
    return pl.BlockSpec(shape, lambda *_: (0,) * nd)


def _tile_rows(t):
    return 512 if t % 512 == 0 else 128


def _dot(a, b):
    return jnp.dot(a, b, preferred_element_type=F32)


def _dot_nt(a, b):
    return lax.dot_general(a, b, NT_DIMS, preferred_element_type=F32)


def _dot_tn(a, b):
    return lax.dot_general(a, b, TN_DIMS, preferred_element_type=F32)


def _rms_r(x):
    return lax.rsqrt(jnp.mean(x * x, axis=-1, keepdims=True) + RMS_EPS)


def _rms_bwd(x, r, g, dy):
    gy = dy * g
    dx = r * gy - x * (r * r * r * jnp.mean(gy * x, axis=-1, keepdims=True))
    dg = jnp.sum(dy * (x * r), axis=0, keepdims=True)
    return dx, dg


def _sigmoid(x):
    return jax.nn.sigmoid(x)


def _rope_tables(t):
    inv = 1.0 / (ROPE_THETA ** (jnp.arange(0, HEAD_DIM, 2, dtype=F32) / HEAD_DIM))
    ang = jnp.arange(t, dtype=F32)[:, None] * inv[None, :]
    c, s = jnp.cos(ang), jnp.sin(ang)
    cos = jnp.concatenate([c, c, c, c], axis=1)
    sin = jnp.concatenate([-s, s, -s, s], axis=1)
    return cos, sin


def _swap_halves(x):
    n = x.shape[1]
    lane = lax.broadcasted_iota(jnp.int32, x.shape, 1)
    first = (lane % HEAD_DIM) < (HEAD_DIM // 2)
    return jnp.where(first, pltpu.roll(x, n - HEAD_DIM // 2, 1), pltpu.roll(x, HEAD_DIM // 2, 1))


def _rope(x, cos, sin):
    reps = x.shape[1] // LANES
    return x * jnp.tile(cos, (1, reps)) + _swap_halves(x) * jnp.tile(sin, (1, reps))


def _unrope(dy, cos, sin):
    reps = dy.shape[1] // LANES
    return dy * jnp.tile(cos, (1, reps)) + _swap_halves(dy * jnp.tile(sin, (1, reps)))


def _acc_init(acc_ref):
    @pl.when(pl.program_id(0) == 0)
    def _():
        acc_ref[...] = jnp.zeros_like(acc_ref)


def _window_sums(ext, tm, forward):
    n = tm + HALO
    out = []
    for g, w in enumerate(POOL_WINDOWS):
        s = ext[:, g * POOL_GROUP:(g + 1) * POOL_GROUP]
        k = 1
        while k < w:
            s = s + pltpu.roll(s, k if forward else n - k, 0)
            k *= 2
        out.append(s[HALO:, :] if forward else s[:tm, :])
    return out


def _pool_counts(tile, tm):
    t = tile * tm + lax.broadcasted_iota(jnp.int32, (tm, 1), 0)
    return [jnp.minimum(t + 1, w).astype(F32) for w in POOL_WINDOWS]


def _pool_mix(hn, ext, cnts, pw_ref, scale, tm):
    sums = _window_sums(ext, tm, True)
    pooled, ys = [], []
    for g in range(len(POOL_WINDOWS)):
        pg = (sums[g] / cnts[g] - hn[:, g * POOL_GROUP:(g + 1) * POOL_GROUP]).astype(BF16)
        pooled.append(pg)
        ys.append(_dot(pg, pw_ref[g]))
    y = jnp.concatenate(ys, axis=1)
    return pooled, y, y * scale


def pool_mix_fwd(h0, gpre, pool_w, scale, gpost, gffn):
    t, d = h0.shape
    tm = _tile_rows(t)

    def body(h_ref, gpre_ref, pw_ref, scale_ref, gpost_ref, gffn_ref, h1_ref, a_ref, carry):
        i = pl.program_id(0)

        @pl.when(i == 0)
        def _():
            carry[...] = jnp.zeros_like(carry)

        x = h_ref[...]
        hn = x * _rms_r(x) * gpre_ref[...]
        ext = jnp.concatenate([carry[...], hn], axis=0)
        carry[...] = hn[tm - HALO:, :]
        _, _, m = _pool_mix(hn, ext, _pool_counts(i, tm), pw_ref, scale_ref[...], tm)
        h1 = x + m * _rms_r(m) * gpost_ref[...]
        h1_ref[...] = h1
        a_ref[...] = (h1 * _rms_r(h1) * gffn_ref[...]).astype(BF16)

    return pl.pallas_call(
        functools.partial(body), name="pool_mix_fwd", grid=(t // tm,),
        in_specs=[_rows(tm, d), _const((1, d)), _const(pool_w.shape), _const((1, d)), _const((1, d)), _const((1, d))],
        out_specs=[_rows(tm, d), _rows(tm, d)],
        out_shape=[jax.ShapeDtypeStruct((t, d), F32), jax.ShapeDtypeStruct((t, d), BF16)],
        scratch_shapes=[pltpu.VMEM((HALO, d), F32)],
        compiler_params=_cparams(("arbitrary",), VMEM_LIMIT),
    )(h0, gpre, pool_w, scale, gpost, gffn)


def pool_mix_bwd(h0, dh2, da, gpre, pool_w, scale, gpost, gffn):
    t, d = h0.shape
    tm = _tile_rows(t)
    nt = t // tm
    hb = tm // HALO

    def body(h_ref, halo_ref, dh2_ref, da_ref, gpre_ref, pw_ref, scale_ref, gpost_ref, gffn_ref,
             dh0_ref, dpw_ref, gacc_ref, carry):
        i = pl.program_id(0)
        tile = nt - 1 - i
        _acc_init(gacc_ref)
        _acc_init(dpw_ref)

        @pl.when(i == 0)
        def _():
            carry[...] = jnp.zeros_like(carry)

        x = h_ref[...]
        gpre_v, scale_v, gpost_v, gffn_v = gpre_ref[...], scale_ref[...], gpost_ref[...], gffn_ref[...]
        r0 = _rms_r(x)
        hn = x * r0 * gpre_v
        xh = halo_ref[...]
        hn_halo = jnp.where(tile > 0, xh * _rms_r(xh) * gpre_v, 0.0)
        ext = jnp.concatenate([hn_halo, hn], axis=0)
        cnts = _pool_counts(tile, tm)
        pooled, y, m = _pool_mix(hn, ext, cnts, pw_ref, scale_v, tm)
        rm = _rms_r(m)
        h1 = x + m * rm * gpost_v
        dh1_n, dgffn = _rms_bwd(h1, _rms_r(h1), gffn_v, da_ref[...])
        dh1 = dh2_ref[...] + dh1_n
        dm, dgpost = _rms_bwd(m, rm, gpost_v, dh1)
        dscale = jnp.sum(dm * y, axis=0, keepdims=True)
        dy = (dm * scale_v).astype(BF16)
        dpn = []
        for g in range(len(POOL_WINDOWS)):
            dyg = dy[:, g * POOL_GROUP:(g + 1) * POOL_GROUP]
            dpw_ref[g] += _dot_tn(pooled[g], dyg)
            dpn.append(_dot_nt(dyg, pw_ref[g]))
        dpooled = jnp.concatenate(dpn, axis=1)
        dpc = jnp.concatenate([dpn[g] / cnts[g] for g in range(len(POOL_WINDOWS))], axis=1)
        ext2 = jnp.concatenate([dpc, carry[...]], axis=0)
        carry[...] = dpc[:HALO, :]
        dhn = jnp.concatenate(_window_sums(ext2, tm, False), axis=1) - dpooled
        dh0_n, dgpre = _rms_bwd(x, r0, gpre_v, dhn)
        dh0_ref[...] = dh1 + dh0_n
        gacc_ref[0:1, :] += dgpre
        gacc_ref[1:2, :] += dgpost
        gacc_ref[2:3, :] += dgffn
        gacc_ref[3:4, :] += dscale

    return pl.pallas_call(
        functools.partial(body), name="pool_mix_bwd", grid=(nt,),
        in_specs=[_rows_rev(tm, d, nt),
                  pl.BlockSpec((HALO, d), lambda i: (jnp.maximum((nt - 1 - i) * hb - 1, 0), 0)),
                  _rows_rev(tm, d, nt), _rows_rev(tm, d, nt),
                  _const((1, d)), _const(pool_w.shape), _const((1, d)), _const((1, d)), _const((1, d))],
        out_specs=[_rows_rev(tm, d, nt), _resident(pool_w.shape), _resident((8, d))],
        out_shape=[jax.ShapeDtypeStruct((t, d), F32), jax.ShapeDtypeStruct(pool_w.shape, F32),
                   jax.ShapeDtypeStruct((8, d), F32)],
        scratch_shapes=[pltpu.VMEM((HALO, d), F32)],
        compiler_params=_cparams(("arbitrary",), VMEM_LIMIT),
    )(h0, h0, dh2, da, gpre, pool_w, scale, gpost, gffn)


def _ffn_chunks(f):
    return [(c, min(c + FFN_CHUNK, f)) for c in range(0, f, FFN_CHUNK)]


def ffn_fwd(a, wg_t, wu_t, wd):
    t, d = a.shape
    f = wd.shape[0]
    tm = _tile_rows(t)

    def body(a_ref, wg_ref, wu_ref, wd_ref, f_ref, gte_ref, up_ref):
        av = a_ref[...]
        acc = jnp.zeros((tm, d), F32)
        for c0, c1 in _ffn_chunks(f):
            gte = _dot_nt(av, wg_ref[c0:c1, :])
            up = _dot_nt(av, wu_ref[c0:c1, :])
            gte_ref[:, c0:c1] = gte.astype(BF16)
            up_ref[:, c0:c1] = up.astype(BF16)
            hdn = (gte * _sigmoid(gte) * up).astype(BF16)
            acc = acc + _dot(hdn, wd_ref[c0:c1, :])
        f_ref[...] = acc

    return pl.pallas_call(
        functools.partial(body), name="ffn_fwd", grid=(t // tm,),
        in_specs=[_rows(tm, d), _const((f, d)), _const((f, d)), _const((f, d))],
        out_specs=[_rows(tm, d), _rows(tm, f), _rows(tm, f)],
        out_shape=[jax.ShapeDtypeStruct((t, d), F32), jax.ShapeDtypeStruct((t, f), BF16),
                   jax.ShapeDtypeStruct((t, f), BF16)],
        compiler_params=_cparams(("parallel",), VMEM_LIMIT),
    )(a, wg_t, wu_t, wd)


def ffn_bwd_act(df, gte, up, wg_t, wu_t, wd):
    t, d = df.shape
    f = wd.shape[0]
    tm = min(_tile_rows(t), 256)

    def body(df_ref, gte_ref, up_ref, wg_ref, wu_ref, wd_ref, da_ref, dgte_ref, dup_ref, hdn_ref):
        dfv = df_ref[...]
        acc = jnp.zeros((tm, d), F32)
        for c0, c1 in _ffn_chunks(f):
            g = gte_ref[:, c0:c1].astype(F32)
            u = up_ref[:, c0:c1].astype(F32)
            sg = _sigmoid(g)
            sl = g * sg
            hdn_ref[:, c0:c1] = (sl * u).astype(BF16)
            dh = _dot_nt(dfv, wd_ref[c0:c1, :])
            dup = (dh * sl).astype(BF16)
            dgte = (dh * u * (sg * (1.0 + g * (1.0 - sg)))).astype(BF16)
            dup_ref[:, c0:c1] = dup
            dgte_ref[:, c0:c1] = dgte
            acc = acc + _dot(dgte, wg_ref[c0:c1, :]) + _dot(dup, wu_ref[c0:c1, :])
        da_ref[...] = acc

    return pl.pallas_call(
        functools.partial(body), name="ffn_bwd_act", grid=(t // tm,),
        in_specs=[_rows(tm, d), _rows(tm, f), _rows(tm, f), _const((f, d)), _const((f, d)), _const((f, d))],
        out_specs=[_rows(tm, d), _rows(tm, f), _rows(tm, f), _rows(tm, f)],
        out_shape=[jax.ShapeDtypeStruct((t, d), F32)] + [jax.ShapeDtypeStruct((t, f), BF16)] * 3,
        compiler_params=_cparams(("parallel",), VMEM_LIMIT),
    )(df, gte, up, wg_t, wu_t, wd)


def xty(x, y):
    t, nx = x.shape
    ny = y.shape[1]
    tk = _tile_rows(t)
    bn = nx // 2 if nx > 1024 else nx
    nk = t // tk

    def body(x_ref, y_ref, o_ref, acc):
        k = pl.program_id(1)

        @pl.when(k == 0)
        def _():
            acc[...] = jnp.zeros_like(acc)

        acc[...] += _dot_tn(x_ref[...].astype(BF16), y_ref[...].astype(BF16))

        @pl.when(k == nk - 1)
        def _():
            o_ref[...] = acc[...].astype(BF16)

    return pl.pallas_call(
        functools.partial(body), name="xty", grid=(nx // bn, nk),
        in_specs=[pl.BlockSpec((tk, bn), lambda j, k: (k, j)), pl.BlockSpec((tk, ny), lambda j, k: (k, 0))],
        out_specs=pl.BlockSpec((bn, ny), lambda j, k: (j, 0)),
        out_shape=jax.ShapeDtypeStruct((nx, ny), BF16),
        scratch_shapes=[pltpu.VMEM((bn, ny), F32)],
        compiler_params=_cparams(("parallel", "arbitrary"), VMEM_LIMIT),
    )(x, y)


def _ple_fwd_tile(h1, f, p, gpost, gple, wpg_ref, wpp_ref):
    rf = _rms_r(f)
    h2 = h1 + f * rf * gpost
    r2 = _rms_r(h2)
    ub = (h2 * r2 * gple).astype(BF16)
    gate = _sigmoid(_dot(ub, wpg_ref[...]))
    pp = _dot_nt(p.astype(BF16), wpp_ref[...])
    return rf, h2, r2, ub, gate, pp


def post_ple_fwd(h1, f, p, gpost, gple, wpg, wpp_t, target=None):
    t, d = h1.shape
    pd = p.shape[1]
    tm = _tile_rows(t)
    with_loss = target is not None

    def body(*refs):
        if with_loss:
            h1_ref, f_ref, p_ref, gpost_ref, gple_ref, wpg_ref, wpp_ref, tgt_ref, out_ref, loss_ref = refs
        else:
            h1_ref, f_ref, p_ref, gpost_ref, gple_ref, wpg_ref, wpp_ref, out_ref = refs
        _, h2, _, _, gate, pp = _ple_fwd_tile(h1_ref[...], f_ref[...], p_ref[...], gpost_ref[...], gple_ref[...],
                                              wpg_ref, wpp_ref)
        h3 = h2 + pp * gate
        if with_loss:
            err = h3 - tgt_ref[...]
            out_ref[...] = err * (1.0 / d)
            colsum = jnp.sum(err * err, axis=0, keepdims=True) * (0.5 / d)
            loss_ref[...] = jnp.broadcast_to(colsum, (8, d)) * (lax.broadcasted_iota(jnp.int32, (8, d), 0) == 0)
        else:
            out_ref[...] = h3

    in_specs = [_rows(tm, d), _rows(tm, d), _rows(tm, pd), _const((1, d)), _const((1, d)), _const(wpg.shape),
                _const(wpp_t.shape)]
    out_specs = [_rows(tm, d)]
    out_shape = [jax.ShapeDtypeStruct((t, d), F32)]
    args = [h1, f, p, gpost, gple, wpg, wpp_t]
    if with_loss:
        in_specs.append(_rows(tm, d))
        out_specs.append(_rows(8, d))
        out_shape.append(jax.ShapeDtypeStruct((t // tm * 8, d), F32))
        args.append(target)
    return pl.pallas_call(
        functools.partial(body), name="post_ple_loss" if with_loss else "post_ple_fwd", grid=(t // tm,),
        in_specs=in_specs, out_specs=out_specs, out_shape=out_shape,
        compiler_params=_cparams(("parallel",), VMEM_LIMIT),
    )(*args)


def post_ple_bwd(dh3, h1, f, p, gpost, gple, wpg, wpp_t):
    t, d = h1.shape
    pd = p.shape[1]
    tm = _tile_rows(t)

    def body(dh3_ref, h1_ref, f_ref, p_ref, gpost_ref, gple_ref, wpg_ref, wpp_ref,
             dh2_ref, df_ref, u_ref, dz_ref, dpp_ref, gacc_ref):
        _acc_init(gacc_ref)
        fv = f_ref[...]
        gpost_v, gple_v = gpost_ref[...], gple_ref[...]
        rf, h2, r2, ub, gate, pp = _ple_fwd_tile(h1_ref[...], fv, p_ref[...], gpost_v, gple_v, wpg_ref, wpp_ref)
        dh3v = dh3_ref[...]
        dpp_ref[...] = (dh3v * gate).astype(BF16)
        dz = (dh3v * pp * gate * (1.0 - gate)).astype(BF16)
        dz_ref[...] = dz
        u_ref[...] = ub
        du = _dot_nt(dz, wpg_ref[...])
        dh2_n, dgple = _rms_bwd(h2, r2, gple_v, du)
        dh2 = dh3v + dh2_n
        df, dgpost = _rms_bwd(fv, rf, gpost_v, dh2)
        dh2_ref[...] = dh2
        df_ref[...] = df.astype(BF16)
        gacc_ref[0:1, :] += dgple
        gacc_ref[1:2, :] += dgpost

    return pl.pallas_call(
        functools.partial(body), name="post_ple_bwd", grid=(t // tm,),
        in_specs=[_rows(tm, d), _rows(tm, d), _rows(tm, d), _rows(tm, pd), _const((1, d)), _const((1, d)),
                  _const(wpg.shape), _const(wpp_t.shape)],
        out_specs=[_rows(tm, d)] * 5 + [_resident((8, d))],
        out_shape=[jax.ShapeDtypeStruct((t, d), F32)] + [jax.ShapeDtypeStruct((t, d), BF16)] * 4
        + [jax.ShapeDtypeStruct((8, d), F32)],
        compiler_params=_cparams(("arbitrary",), VMEM_LIMIT),
    )(dh3, h1, f, p, gpost, gple, wpg, wpp_t)


def proj_rope_fwd(h, gain, w, cos, sin, n_rope, name):
    t, d = h.shape
    n = w.shape[1]
    tm = _tile_rows(t)

    def body(h_ref, g_ref, w_ref, cos_ref, sin_ref, hn_ref, y_ref):
        x = h_ref[...]
        hn = (x * _rms_r(x) * g_ref[...]).astype(BF16)
        hn_ref[...] = hn
        y = _dot(hn, w_ref[...])
        y_ref[:, :n_rope] = _rope(y[:, :n_rope], cos_ref[...], sin_ref[...]).astype(BF16)
        if n_rope < n:
            y_ref[:, n_rope:] = y[:, n_rope:].astype(BF16)

    return pl.pallas_call(
        functools.partial(body), name=name, grid=(t // tm,),
        in_specs=[_rows(tm, d), _const((1, d)), _const(w.shape), _rows(tm, LANES), _rows(tm, LANES)],
        out_specs=[_rows(tm, d), _rows(tm, n)],
        out_shape=[jax.ShapeDtypeStruct((t, d), BF16), jax.ShapeDtypeStruct((t, n), BF16)],
        compiler_params=_cparams(("parallel",), VMEM_LIMIT),
    )(h, gain, w, cos, sin)


def proj_rope_bwd(dh1, h0, cos, sin, branches, name):
    t, d = h0.shape
    tm = _tile_rows(t)
    nb = len(branches)
    n_cot = [len(b[3]) for b in branches]

    def body(*refs):
        dh1_ref, h0_ref, cos_ref, sin_ref = refs[:4]
        pos = 4
        br_refs = []
        for b in range(nb):
            br_refs.append((refs[pos], refs[pos + 1], refs[pos + 2:pos + 2 + n_cot[b]]))
            pos += 2 + n_cot[b]
        dh0_ref = refs[pos]
        dpre_refs = refs[pos + 1:pos + 1 + nb]
        gacc_ref = refs[pos + 1 + nb]
        _acc_init(gacc_ref)
        x = h0_ref[...]
        r0 = _rms_r(x)
        dh = dh1_ref[...]
        for b in range(nb):
            g_ref, w_ref, cot_refs = br_refs[b]
            n_rope = branches[b][2]
            dy = cot_refs[0][...].astype(F32)
            for c_ref in cot_refs[1:]:
                dy = dy + c_ref[...].astype(F32)
            n = dy.shape[1]
            dpre_refs[b][:, :n_rope] = _unrope(dy[:, :n_rope], cos_ref[...], sin_ref[...]).astype(BF16)
            if n_rope < n:
                dpre_refs[b][:, n_rope:] = dy[:, n_rope:].astype(BF16)
            dhn = _dot_nt(dpre_refs[b][...], w_ref[...])
            dx, dg = _rms_bwd(x, r0, g_ref[...], dhn)
            dh = dh + dx
            gacc_ref[b:b + 1, :] += dg
        dh0_ref[...] = dh

    in_specs = [_rows(tm, d), _rows(tm, d), _rows(tm, LANES), _rows(tm, LANES)]
    args = [dh1, h0, cos, sin]
    out_specs = [_rows(tm, d)]
    out_shape = [jax.ShapeDtypeStruct((t, d), F32)]
    for gain, w, _, cots in branches:
        n = w.shape[1]
        in_specs += [_const((1, d)), _const(w.shape)] + [_rows(tm, n)] * len(cots)
        args += [gain, w] + list(cots)
        out_specs.append(_rows(tm, n))
        out_shape.append(jax.ShapeDtypeStruct((t, n), BF16))
    out_specs.append(_resident((8, d)))
    out_shape.append(jax.ShapeDtypeStruct((8, d), F32))
    return pl.pallas_call(
        functools.partial(body), name=name, grid=(t // tm,),
        in_specs=in_specs, out_specs=out_specs, out_shape=out_shape,
        compiler_params=_cparams(("arbitrary",), VMEM_LIMIT),
    )(*args)


def _tri():
    row = lax.broadcasted_iota(jnp.int32, (BLOCK, BLOCK), 0)
    col = lax.broadcasted_iota(jnp.int32, (BLOCK, BLOCK), 1)
    return col <= row


def _block_diag(x):
    lo = lax.broadcasted_iota(jnp.int32, x.shape, 1) < HEAD_DIM
    zero = jnp.zeros_like(x)
    return jnp.concatenate([jnp.where(lo, x, zero), jnp.where(lo, zero, x)], axis=0)


def _dense(x, tri):
    return (jnp.where(tri, x[:, BLOCK:2 * BLOCK], x[:, :BLOCK]),
            jnp.where(tri, x[:, 3 * BLOCK:], x[:, 2 * BLOCK:3 * BLOCK]))


def _banded(xa, xb, tri):
    zero = jnp.zeros_like(xa)
    return jnp.concatenate([jnp.where(tri, zero, xa), jnp.where(tri, xa, zero),
                            jnp.where(tri, zero, xb), jnp.where(tri, xb, zero)], axis=1).astype(BF16)


def _softmax_sink(s, sink):
    mx = jnp.maximum(jnp.max(s, axis=1, keepdims=True), sink)
    e = jnp.exp(s - mx)
    es = jnp.exp(sink - mx)
    inv = 1.0 / (jnp.sum(e, axis=1, keepdims=True) + es)
    return e * inv, es * inv


def _sink_column(sink_ref):
    return jnp.concatenate([jnp.broadcast_to(sink_ref[h:h + 1, 0:1], (BLOCK, 1)) for h in range(N_HEADS)], axis=0)


def _kv_block_diag(band, kvw):
    n_lt = kvw // LANES
    return ([_block_diag(band[:, lt * LANES:(lt + 1) * LANES]) for lt in range(n_lt)],
            [_block_diag(band[:, kvw + lt * LANES:kvw + (lt + 1) * LANES]) for lt in range(n_lt)])


def _all_probs(q_ref, kbd, tri, n, sink_ref):
    dense = []
    for tq in range(N_HEADS // 2):
        s = _dot_nt(q_ref[:, tq * LANES:(tq + 1) * LANES], kbd[tq // GQA])
        dense += list(_dense(s, tri))
    bias = jnp.where(jnp.logical_not(tri) & (n == 0), NEG_INF, 0.0)
    s_all = jnp.concatenate(dense, axis=0) * (HEAD_DIM ** -0.5) + jnp.concatenate([bias] * N_HEADS, axis=0)
    return _softmax_sink(s_all, _sink_column(sink_ref))


def _head_rows(x, tq):
    return x[2 * tq * BLOCK:(2 * tq + 1) * BLOCK], x[(2 * tq + 1) * BLOCK:(2 * tq + 2) * BLOCK]


def swa_fwd(q, kv, sink_b):
    t, d = q.shape
    nb = t // BLOCK
    kvw = N_KV_HEADS * HEAD_DIM

    def body(q_ref, kvc_ref, kvp_ref, sink_ref, o_ref):
        n = pl.program_id(0)
        tri = _tri()
        kbd, vbd = _kv_block_diag(jnp.concatenate([kvp_ref[...], kvc_ref[...]], axis=0), kvw)
        p, _ = _all_probs(q_ref, kbd, tri, n, sink_ref)
        for tq in range(N_HEADS // 2):
            pa, pb = _head_rows(p, tq)
            o_ref[:, tq * LANES:(tq + 1) * LANES] = _dot(_banded(pa, pb, tri), vbd[tq // GQA]).astype(BF16)

    return pl.pallas_call(
        functools.partial(body), name="swa_fwd", grid=(nb,),
        in_specs=[_rows(BLOCK, d), _rows(BLOCK, 2 * kvw),
                  pl.BlockSpec((BLOCK, 2 * kvw), lambda n: (jnp.maximum(n - 1, 0), 0)), _const(sink_b.shape)],
        out_specs=_rows(BLOCK, d),
        out_shape=jax.ShapeDtypeStruct((t, d), BF16),
        compiler_params=_cparams(("parallel",), VMEM_LIMIT),
    )(q, kv, kv, sink_b)


def swa_bwd(q, kv, do, sink_b):
    t, d = q.shape
    nb = t // BLOCK
    kvw = N_KV_HEADS * HEAD_DIM

    def body(q_ref, do_ref, kvc_ref, kvp_ref, sink_ref, dq_ref, dkv_ref, dsink_ref, carry):
        i = pl.program_id(0)
        n = nb - 1 - i
        _acc_init(dsink_ref)

        @pl.when(i == 0)
        def _():
            carry[...] = jnp.zeros_like(carry)

        tri = _tri()
        lo = lax.broadcasted_iota(jnp.int32, (2 * BLOCK, LANES), 1) < HEAD_DIM
        kbd, vbd = _kv_block_diag(jnp.concatenate([kvp_ref[...], kvc_ref[...]], axis=0), kvw)
        p, ps = _all_probs(q_ref, kbd, tri, n, sink_ref)
        dp = []
        for tq in range(N_HEADS // 2):
            dp += list(_dense(_dot_nt(do_ref[:, tq * LANES:(tq + 1) * LANES], vbd[tq // GQA]), tri))
        dp = jnp.concatenate(dp, axis=0)
        delta = jnp.sum(p * dp, axis=1, keepdims=True)
        ds = p * (dp - delta) * (HEAD_DIM ** -0.5)
        dsk = ps * delta
        for h in range(N_HEADS):
            dsink_ref[h:h + 1, :] -= jnp.sum(dsk[h * BLOCK:(h + 1) * BLOCK], axis=0, keepdims=True)
        dkb = [jnp.zeros((4 * BLOCK, LANES), F32) for _ in kbd]
        dvb = [jnp.zeros((4 * BLOCK, LANES), F32) for _ in kbd]
        for tq in range(N_HEADS // 2):
            lt = tq // GQA
            cols = slice(tq * LANES, (tq + 1) * LANES)
            dsb = _banded(*_head_rows(ds, tq), tri)
            dq_ref[:, cols] = _dot(dsb, kbd[lt]).astype(BF16)
            dkb[lt] = dkb[lt] + _dot_tn(dsb, q_ref[:, cols])
            dvb[lt] = dvb[lt] + _dot_tn(_banded(*_head_rows(p, tq), tri), do_ref[:, cols])
        dall = jnp.concatenate([jnp.where(lo, x[:2 * BLOCK], x[2 * BLOCK:]) for x in dkb + dvb], axis=1)
        dkv_ref[...] = dall[BLOCK:, :] + carry[...]
        carry[...] = dall[:BLOCK, :]

    rev = lambda i: (nb - 1 - i, 0)
    return pl.pallas_call(
        functools.partial(body), name="swa_bwd", grid=(nb,),
        in_specs=[pl.BlockSpec((BLOCK, d), rev), pl.BlockSpec((BLOCK, d), rev), pl.BlockSpec((BLOCK, 2 * kvw), rev),
                  pl.BlockSpec((BLOCK, 2 * kvw), lambda i: (jnp.maximum(nb - 2 - i, 0), 0)), _const(sink_b.shape)],
        out_specs=[pl.BlockSpec((BLOCK, d), rev), pl.BlockSpec((BLOCK, 2 * kvw), rev), _resident(sink_b.shape)],
        out_shape=[jax.ShapeDtypeStruct((t, d), BF16), jax.ShapeDtypeStruct((t, 2 * kvw), F32),
                   jax.ShapeDtypeStruct(sink_b.shape, F32)],
        scratch_shapes=[pltpu.VMEM((BLOCK, 2 * kvw), F32)],
        compiler_params=_cparams(("arbitrary",), VMEM_LIMIT),
    )(q, do, kv, kv, sink_b)


def oproj_post_fwd(attn, w_o, h0, gpost, gffn):
    t, d = h0.shape
    tm = _tile_rows(t)

    def body(at_ref, w_ref, h0_ref, gpost_ref, gffn_ref, m_ref, h1_ref, a_ref):
        m = _dot(at_ref[...], w_ref[...])
        m_ref[...] = m
        h1 = h0_ref[...] + m * _rms_r(m) * gpost_ref[...]
        h1_ref[...] = h1
        a_ref[...] = (h1 * _rms_r(h1) * gffn_ref[...]).astype(BF16)

    return pl.pallas_call(
        functools.partial(body), name="oproj_post_fwd", grid=(t // tm,),
        in_specs=[_rows(tm, d), _const(w_o.shape), _rows(tm, d), _const((1, d)), _const((1, d))],
        out_specs=[_rows(tm, d)] * 3,
        out_shape=[jax.ShapeDtypeStruct((t, d), F32), jax.ShapeDtypeStruct((t, d), F32),
                   jax.ShapeDtypeStruct((t, d), BF16)],
        compiler_params=_cparams(("parallel",), VMEM_LIMIT),
    )(attn, w_o, h0, gpost, gffn)


def oproj_post_bwd(dh2, da, h1, m, w_o, gpost, gffn):
    t, d = h1.shape
    tm = _tile_rows(t)

    def body(dh2_ref, da_ref, h1_ref, m_ref, w_ref, gpost_ref, gffn_ref, dh1_ref, dm_ref, dat_ref, gacc_ref):
        _acc_init(gacc_ref)
        h1v, mv = h1_ref[...], m_ref[...]
        dh1_n, dgffn = _rms_bwd(h1v, _rms_r(h1v), gffn_ref[...], da_ref[...])
        dh1 = dh2_ref[...] + dh1_n
        dm, dgpost = _rms_bwd(mv, _rms_r(mv), gpost_ref[...], dh1)
        dmb = dm.astype(BF16)
        dh1_ref[...] = dh1
        dm_ref[...] = dmb
        dat_ref[...] = _dot_nt(dmb, w_ref[...]).astype(BF16)
        gacc_ref[0:1, :] += dgpost
        gacc_ref[1:2, :] += dgffn

    return pl.pallas_call(
        functools.partial(body), name="oproj_post_bwd", grid=(t // tm,),
        in_specs=[_rows(tm, d)] * 4 + [_const(w_o.shape), _const((1, d)), _const((1, d))],
        out_specs=[_rows(tm, d)] * 3 + [_resident((8, d))],
        out_shape=[jax.ShapeDtypeStruct((t, d), F32), jax.ShapeDtypeStruct((t, d), BF16),
                   jax.ShapeDtypeStruct((t, d), BF16), jax.ShapeDtypeStruct((8, d), F32)],
        compiler_params=_cparams(("arbitrary",), VMEM_LIMIT),
    )(dh2, da, h1, m, w_o, gpost, gffn)


def _my_place():
    return lax.axis_index("x"), lax.axis_index("y"), lax.axis_index("c")


def _block_index(px, py, pc):
    return 4 * px + 2 * py + pc


def allgather_pieces(shards, name):
    np_ = len(shards)

    def body(*refs):
        in_refs, out_refs = refs[:np_], refs[np_:2 * np_]
        send_sems, recv_sems, local_sems = refs[2 * np_:]
        x, y, c = _my_place()
        me, sibling = (x, y, c), (x, y, 1 - c)
        chips = [(1 - x, y), (x, 1 - y), (1 - x, 1 - y)]

        def rows(p, place):
            r = in_refs[p].shape[0]
            return out_refs[p].at[pl.ds(_block_index(*place) * r, r), :]

        def copy(p, k, block, to, src=None):
            return pltpu.make_async_remote_copy(
                src_ref=rows(p, block) if src is None else src, dst_ref=rows(p, block),
                send_sem=send_sems.at[p, k], recv_sem=recv_sems.at[p, k], device_id=to, device_id_type=MESH)

        mine = [pltpu.make_async_copy(in_refs[p], rows(p, me), local_sems.at[p]) for p in range(np_)]
        first, passed = [], []
        for p in range(np_):
            mine[p].start()
            first.append(copy(p, 0, me, sibling, src=in_refs[p]))
            first += [copy(p, 1 + j, me, (*chip, c), src=in_refs[p]) for j, chip in enumerate(chips)]
        for cp in first:
            cp.start()
        for p in range(np_):
            for j, chip in enumerate(chips):
                copy(p, 1 + j, (*chip, c), me).wait_recv()
                fwd = copy(p, 4 + j, (*chip, c), sibling)
                fwd.start()
                passed.append(fwd)
        for p in range(np_):
            copy(p, 0, sibling, me).wait_recv()
            for j, chip in enumerate(chips):
                copy(p, 4 + j, (*chip, 1 - c), me).wait_recv()
        for cp in first + passed:
            cp.wait_send()
        for cp in mine:
            cp.wait()

    return pl.pallas_call(
        functools.partial(body), name=name,
        in_specs=[ANY] * np_, out_specs=[ANY] * np_,
        out_shape=[jax.ShapeDtypeStruct((N_DEV * s.shape[0], s.shape[1]), s.dtype) for s in shards],
        scratch_shapes=[pltpu.SemaphoreType.DMA((np_, 7)), pltpu.SemaphoreType.DMA((np_, 7)),
                        pltpu.SemaphoreType.DMA((np_,))],
    )(*shards)


def _peers():
    x, y, c = _my_place()
    flips = [(fx, fy, fc) for fx in (0, 1) for fy in (0, 1) for fc in (0, 1)][1:]
    return [(1 - x if fx else x, 1 - y if fy else y, 1 - c if fc else c) for fx, fy, fc in flips]


HBM = pl.BlockSpec(memory_space=pltpu.HBM)
SEM = pl.BlockSpec(memory_space=pltpu.SEMAPHORE)


def _exchange_windows(scatter, src_ref, land_ref, my_block, peer_block):
    if scatter:
        r = land_ref.shape[1]
        return src_ref.at[pl.ds(peer_block * r, r), :], land_ref.at[my_block], land_ref.at[peer_block]
    r = src_ref.shape[0]
    return src_ref, land_ref.at[pl.ds(my_block * r, r), :], land_ref.at[pl.ds(peer_block * r, r), :]


def exchange_start(srcs, lands, after, scatter, name):
    np_ = len(srcs)

    def body(*refs):
        src_refs, land_refs = refs[:np_], refs[np_:2 * np_]
        send_sems, recv_sems = refs[2 * np_ + 1:2 * np_ + 3]
        token = refs[-1]
        my_block = _block_index(*_my_place())
        for p in range(np_):
            for k, peer in enumerate(_peers()):
                src, dst, _ = _exchange_windows(scatter, src_refs[p], land_refs[p], my_block, _block_index(*peer))
                pltpu.make_async_remote_copy(src_ref=src, dst_ref=dst, send_sem=send_sems.at[7 * p + k],
                                             recv_sem=recv_sems.at[7 * p + k], device_id=peer, device_id_type=MESH).start()
        token[...] = jnp.zeros_like(token)

    hbm = lambda a: pltpu.with_memory_space_constraint(a, pltpu.HBM)
    outs = pl.pallas_call(
        functools.partial(body), name=name,
        in_specs=[HBM] * (2 * np_) + [ANY],
        out_specs=[SEM, SEM] + [HBM] * (2 * np_) + [pl.BlockSpec(memory_space=pltpu.VMEM)],
        out_shape=[pltpu.SemaphoreType.DMA((7 * np_,)), pltpu.SemaphoreType.DMA((7 * np_,))]
        + [pltpu.HBM(a.shape, a.dtype) for a in list(srcs) + list(lands)] + [jax.ShapeDtypeStruct((8, LANES), F32)],
        input_output_aliases={i: 2 + i for i in range(2 * np_)},
        compiler_params=pltpu.CompilerParams(has_side_effects=pltpu.SideEffectType.DATAFLOW_SIDE_EFFECTING),
    )(*[hbm(a) for a in srcs], *[hbm(a) for a in lands], after)
    return dict(sems=outs[:2], srcs=outs[2:2 + np_], lands=outs[2 + np_:2 + 2 * np_], token=outs[-1], scatter=scatter)


def exchange_wait(started, after, name):
    srcs, lands = started["srcs"], started["lands"]
    scatter = started["scatter"]
    np_ = len(srcs)

    def body(*refs):
        src_refs, land_refs = refs[:np_], refs[np_:2 * np_]
        send_sems, recv_sems = refs[2 * np_:2 * np_ + 2]
        my_block = _block_index(*_my_place())
        for p in range(np_):
            for k, peer in enumerate(_peers()):
                src, dst, arrival = _exchange_windows(scatter, src_refs[p], land_refs[p], my_block, _block_index(*peer))
                pltpu.make_async_remote_copy(src_ref=src, dst_ref=dst, send_sem=send_sems.at[7 * p + k],
                                             recv_sem=recv_sems.at[7 * p + k], device_id=peer, device_id_type=MESH).wait_send()
                pltpu.make_async_remote_copy(src_ref=src, dst_ref=arrival, send_sem=send_sems.at[7 * p + k],
                                             recv_sem=recv_sems.at[7 * p + k], device_id=peer, device_id_type=MESH).wait_recv()

    outs = pl.pallas_call(
        functools.partial(body), name=name,
        in_specs=[HBM] * (2 * np_) + [SEM, SEM, ANY],
        out_specs=[HBM] * (2 * np_),
        out_shape=[pltpu.HBM(a.shape, a.dtype) for a in list(srcs) + list(lands)],
        input_output_aliases={i: i for i in range(2 * np_)},
        compiler_params=pltpu.CompilerParams(has_side_effects=pltpu.SideEffectType.DATAFLOW_SIDE_EFFECTING),
    )(*srcs, *lands, *started["sems"], after)
    return list(outs[np_:])


def _gather_zone(shard, my_block):
    r, c = shard.shape
    return lax.dynamic_update_slice(lax.empty((N_DEV * r, c), shard.dtype), shard, (my_block * r, 0))


def _scatter_zone(full, my_block):
    r = full.shape[0] // N_DEV
    own = lax.dynamic_slice_in_dim(full, my_block * r, r, axis=0)
    return lax.dynamic_update_slice(lax.empty((N_DEV, r, full.shape[1]), full.dtype), own[None], (my_block, 0, 0))


def allreduce_small(pack):
    r, c = pack.shape

    def body(pack_ref, out_ref, gathered, send_sems, recv_sems):
        me = _my_place()
        my_block = _block_index(*me)
        peers = _peers()

        def copy(k, slot, to):
            return pltpu.make_async_remote_copy(
                src_ref=pack_ref, dst_ref=gathered.at[slot], send_sem=send_sems.at[k], recv_sem=recv_sems.at[k],
                device_id=to, device_id_type=MESH)

        sends = [copy(k, my_block, peer) for k, peer in enumerate(peers)]
        for cp in sends:
            cp.start()
        gathered[my_block] = pack_ref[...]
        for k, peer in enumerate(peers):
            copy(k, _block_index(*peer), peer).wait_recv()
        for cp in sends:
            cp.wait_send()
        total = gathered[0]
        for j in range(1, N_DEV):
            total = total + gathered[j]
        out_ref[...] = total

    return pl.pallas_call(
        functools.partial(body), name="allreduce_small",
        in_specs=[pl.BlockSpec(memory_space=pltpu.VMEM)], out_specs=pl.BlockSpec(memory_space=pltpu.VMEM),
        out_shape=jax.ShapeDtypeStruct((r, c), F32),
        scratch_shapes=[pltpu.VMEM((N_DEV, r, c), F32), pltpu.SemaphoreType.DMA((7,)), pltpu.SemaphoreType.DMA((7,))],
    )(pack)


def sum_parts(parts):
    n, r, c = parts.shape
    br = 256 if r % 256 == 0 else r

    def body(p_ref, g_ref):
        g = p_ref[0].astype(F32)
        for j in range(1, n):
            g = g + p_ref[j].astype(F32)
        g_ref[...] = g

    return pl.pallas_call(
        functools.partial(body), name="sum_parts", grid=(r // br,),
        in_specs=[pl.BlockSpec((n, br, c), lambda i: (0, i, 0))], out_specs=_rows(br, c),
        out_shape=jax.ShapeDtypeStruct((r, c), F32),
        compiler_params=_cparams(("parallel",)),
    )(parts)


def adamw(w, m, v, parts):
    r, c = w.shape
    n = parts.shape[0]
    br = 256 if r % 256 == 0 else r

    def body(w_ref, m_ref, v_ref, p_ref, g_ref, d_ref, nm_ref, nv_ref):
        g = p_ref[0].astype(F32)
        for j in range(1, n):
            g = g + p_ref[j].astype(F32)
        nm = ADAM_B1 * m_ref[...] + (1.0 - ADAM_B1) * g
        nv = ADAM_B2 * v_ref[...] + (1.0 - ADAM_B2) * (g * g)
        m_hat = nm / (1.0 - ADAM_B1 ** ADAM_STEP)
        v_hat = nv / (1.0 - ADAM_B2 ** ADAM_STEP)
        g_ref[...] = g
        d_ref[...] = -ADAM_LR * (m_hat / (jnp.sqrt(v_hat) + ADAM_EPS) + ADAM_WD * w_ref[...])
        nm_ref[...] = nm
        nv_ref[...] = nv

    return pl.pallas_call(
        functools.partial(body), name="adamw", grid=(r // br,),
        in_specs=[_rows(br, c)] * 3 + [pl.BlockSpec((n, br, c), lambda i: (0, i, 0))],
        out_specs=[_rows(br, c)] * 4, out_shape=[jax.ShapeDtypeStruct((r, c), F32)] * 4,
        compiler_params=_cparams(("parallel",)),
    )(w, m, v, parts)


def _adamw_nd(w, m, v, g):
    shp = w.shape
    c = shp[-1]
    flat = lambda a: a.reshape(-1, c)
    outs = adamw(flat(w), flat(m), flat(v), flat(g)[None])
    return [o.reshape(shp) for o in outs]


def _pair_heads(a, axis, width=HEAD_DIM):
    shp = a.shape
    a = a.reshape(shp[:axis] + (2, 2, GQA, width) + shp[axis + 1:])
    return jnp.swapaxes(a, axis + 1, axis + 2).reshape(shp)


def _unpair_heads(a, axis, width=HEAD_DIM):
    shp = a.shape
    a = a.reshape(shp[:axis] + (2, GQA, 2, width) + shp[axis + 1:])
    return jnp.swapaxes(a, axis + 1, axis + 2).reshape(shp)


def _pad_rows(a, rows=8):
    return jnp.pad(a, ((0, rows - a.shape[0]), (0, 0)))


def kernel(x, p, mix_pre_g, mix_post_g, ffn_pre_g, ffn_post_g, pool_w, pool_scale, kv_norm_g, w_k, w_v, w_q, w_o, sinks, w_ff_gate, w_ff_up, w_ff_down, ple_norm_g, w_ple_gate, w_ple_proj, loss_target, m_mix_pre_g, m_mix_post_g, m_ffn_pre_g, m_ffn_post_g, m_pool_w, m_pool_scale, m_kv_norm_g, m_w_k, m_w_v, m_w_q, m_w_o, m_sinks, m_w_ff_gate, m_w_ff_up, m_w_ff_down, m_ple_norm_g, m_w_ple_gate, m_w_ple_proj, v_mix_pre_g, v_mix_post_g, v_ffn_pre_g, v_ffn_post_g, v_pool_w, v_pool_scale, v_kv_norm_g, v_w_k, v_w_v, v_w_q, v_w_o, v_sinks, v_w_ff_gate, v_w_ff_up, v_w_ff_down, v_ple_norm_g, v_w_ple_gate, v_w_ple_proj):
    depth = w_ff_gate.shape[0]
    n_a = pool_w.shape[0]
    t, d = x.shape[1], x.shape[2]
    h = x[0]
    tgt = loss_target[0]
    my_block = _block_index(*_my_place())
    row = lambda g, i: g[i][None, :]
    bf = lambda a: a.astype(BF16)

    full, gathers = [None] * depth, {}
    start_tokens = jnp.zeros((), F32)
    for i in range(depth):
        shards = [bf(w_ff_gate[i].T), bf(w_ff_up[i].T), bf(w_ff_down[i]), bf(w_ple_gate[i]), bf(w_ple_proj[i].T)]
        if i < n_a:
            shards.append(bf(pool_w[i].reshape(-1, POOL_GROUP)))
        else:
            shards += [bf(_pair_heads(w_q[i - n_a], 1)), bf(w_o[i - n_a])]
            if i == n_a:
                shards.append(bf(jnp.concatenate([w_k, w_v], axis=1)))
        if i == 0:
            full[0] = allgather_pieces(shards, "allgather_l0")
        else:
            gathers[i] = exchange_start(shards, [_gather_zone(s, my_block) for s in shards], full[0][0], False,
                                        f"allgather_start_l{i}")
            start_tokens = start_tokens + gathers[i]["token"][0, 0]
    scale_full = allgather_pieces([_pad_rows(pool_scale)], "allgather_scale")[0]
    scale_full = scale_full.reshape(N_DEV, 8, -1)[:, :n_a].transpose(1, 0, 2).reshape(n_a, 1, d)

    cos, sin = _rope_tables(t)
    sink_b = [jnp.broadcast_to(_pair_heads(sinks[j][:, None], 0, 1), (N_HEADS, LANES)) for j in range(depth - n_a)]
    pool_full, wo_full = {}, {}

    saved = []
    kv = hk = None
    for i in range(depth):
        if i > 0:
            full[i] = exchange_wait(gathers[i], h, f"allgather_wait_l{i}")
        wg_t, wu_t, wd, wpg, wpp_t = full[i][:5]
        s = {"h0": h}
        if i < n_a:
            pool_full[i] = (full[i][5].reshape(N_DEV, len(POOL_WINDOWS), -1, POOL_GROUP).transpose(1, 0, 2, 3)
                            .reshape(len(POOL_WINDOWS), POOL_GROUP, POOL_GROUP))
            gpre = row(mix_pre_g, i) + start_tokens if i == 0 else row(mix_pre_g, i)
            h1, a = pool_mix_fwd(h, gpre, pool_full[i], scale_full[i], row(mix_post_g, i), row(ffn_pre_g, i))
        else:
            j = i - n_a
            wo_full[i] = _pair_heads(full[i][6], 0)
            if i == n_a:
                hk, kv = proj_rope_fwd(h, kv_norm_g[None, :], full[i][7], cos, sin, N_KV_HEADS * HEAD_DIM, "kv_proj_fwd")
            hn, q = proj_rope_fwd(h, row(mix_pre_g, i), full[i][5], cos, sin, d, "q_proj_fwd")
            attn = swa_fwd(q, kv, sink_b[j])
            m, h1, a = oproj_post_fwd(attn, wo_full[i], h, row(mix_post_g, i), row(ffn_pre_g, i))
            s.update(hn=hn, q=q, attn=attn, m=m)
        f, gte, up = ffn_fwd(a, wg_t, wu_t, wd)
        s.update(h1=h1, a=a, f=f, gte=gte, up=up)
        if i < depth - 1:
            h = post_ple_fwd(h1, f, p[i, 0], row(ffn_post_g, i), row(ple_norm_g, i), wpg, wpp_t)[0]
        else:
            dh, loss_rows = post_ple_fwd(h1, f, p[i, 0], row(ffn_post_g, i), row(ple_norm_g, i), wpg, wpp_t, target=tgt)
        saved.append(s)

    zero_row = jnp.zeros((1, d), F32)
    g_mix_pre, g_mix_post, g_ffn_pre, g_ffn_post, g_ple = ([None] * depth for _ in range(5))
    g_kv = g_sinks = None
    g_scale = [None] * n_a
    landing, scatters = [None] * depth, {}
    dkv_sum = []
    scatter_token = jnp.zeros((), F32)
    for i in reversed(range(depth)):
        s = saved[i]
        wg_t, wu_t, wd, wpg, wpp_t = full[i][:5]
        dh2, df, ub, dzb, dppb, gacc = post_ple_bwd(dh, s["h1"], s["f"], p[i, 0], row(ffn_post_g, i) + scatter_token,
                                                    row(ple_norm_g, i), wpg, wpp_t)
        g_ple[i], g_ffn_post[i] = gacc[0], gacc[1]
        da, dgte, dup, hdn = ffn_bwd_act(df, s["gte"], s["up"], wg_t, wu_t, wd)
        grads = [xty(dgte, s["a"]), xty(dup, s["a"]), xty(hdn, df), xty(ub, dzb), xty(dppb, p[i, 0])]
        if i < n_a:
            dh, dpw, gacc = pool_mix_bwd(s["h0"], dh2, da, row(mix_pre_g, i), pool_full[i], scale_full[i],
                                         row(mix_post_g, i), row(ffn_pre_g, i))
            g_mix_pre[i], g_mix_post[i], g_ffn_pre[i], g_scale[i] = gacc[0], gacc[1], gacc[2], gacc[3]
            dpw = dpw.reshape(len(POOL_WINDOWS), N_DEV, -1, POOL_GROUP).transpose(1, 0, 2, 3)
            grads.append(bf(dpw.reshape(-1, POOL_GROUP)))
        else:
            j = i - n_a
            dh1, dmb, dattn, gacc = oproj_post_bwd(dh2, da, s["h1"], s["m"], wo_full[i], row(mix_post_g, i),
                                                   row(ffn_pre_g, i))
            g_mix_post[i], g_ffn_pre[i] = gacc[0], gacc[1]
            dq, dkv, dsink = swa_bwd(s["q"], kv, dattn, sink_b[j])
            dkv_sum.append(dkv)
            g_sinks = [_unpair_heads(dsink[:, 0:1], 0, 1)[:, 0]] + (g_sinks or [])
            branches = [(row(mix_pre_g, i), full[i][5], d, [dq])]
            if i == n_a:
                branches.append((kv_norm_g[None, :], full[i][7], N_KV_HEADS * HEAD_DIM, dkv_sum))
            outs = proj_rope_bwd(dh1, s["h0"], cos, sin, branches, f"proj_bwd_l{i}")
            dh, gacc = outs[0], outs[-1]
            g_mix_pre[i] = gacc[0]
            grads += [xty(s["hn"], outs[1]), _unpair_heads(xty(s["attn"], dmb), 0)]
            if i == n_a:
                g_kv = gacc[1]
                grads.append(xty(hk, outs[2]))
        scatters[i] = exchange_start(grads, [_scatter_zone(g, my_block) for g in grads], dh, True,
                                     f"reduce_scatter_start_l{i}")
        scatter_token = scatters[i]["token"][0, 0]
    grad_x = dh[None]
    after = dh
    for i in reversed(range(depth)):
        landing[i] = exchange_wait(scatters[i], after, f"reduce_scatter_wait_l{i}")
        after = landing[i][0]

    loss_row = jnp.sum(loss_rows, axis=0, keepdims=True)
    sink_row = jnp.pad(jnp.concatenate(g_sinks)[None, :], ((0, 0), (0, d - sinks.size)))
    stack = lambda rows_: _pad_rows(jnp.stack(rows_))
    pack = jnp.concatenate([stack(g_mix_pre), stack(g_mix_post), stack(g_ffn_pre), stack(g_ffn_post), stack(g_ple),
                            _pad_rows(g_kv[None]), stack(g_scale), _pad_rows(sink_row), _pad_rows(loss_row)], axis=0)
    tot = allreduce_small(pack)
    sec = lambda k, n: tot[8 * k:8 * k + n]
    loss = jnp.sum(tot[64])
    small = {
        "mix_pre_g": sec(0, depth), "mix_post_g": sec(1, depth), "ffn_pre_g": sec(2, depth),
        "ffn_post_g": sec(3, depth), "ple_norm_g": sec(4, depth), "kv_norm_g": tot[40],
        "pool_scale": lax.dynamic_slice_in_dim(sec(6, n_a), my_block * pool_scale.shape[1], pool_scale.shape[1], axis=1),
        "sinks": tot[56, :sinks.size].reshape(sinks.shape),
    }

    weights = dict(mix_pre_g=mix_pre_g, mix_post_g=mix_post_g, ffn_pre_g=ffn_pre_g, ffn_post_g=ffn_post_g, pool_w=pool_w, pool_scale=pool_scale, kv_norm_g=kv_norm_g, w_k=w_k, w_v=w_v, w_q=w_q, w_o=w_o, sinks=sinks, w_ff_gate=w_ff_gate, w_ff_up=w_ff_up, w_ff_down=w_ff_down, ple_norm_g=ple_norm_g, w_ple_gate=w_ple_gate, w_ple_proj=w_ple_proj)
    mom1 = dict(mix_pre_g=m_mix_pre_g, mix_post_g=m_mix_post_g, ffn_pre_g=m_ffn_pre_g, ffn_post_g=m_ffn_post_g, pool_w=m_pool_w, pool_scale=m_pool_scale, kv_norm_g=m_kv_norm_g, w_k=m_w_k, w_v=m_w_v, w_q=m_w_q, w_o=m_w_o, sinks=m_sinks, w_ff_gate=m_w_ff_gate, w_ff_up=m_w_ff_up, w_ff_down=m_w_ff_down, ple_norm_g=m_ple_norm_g, w_ple_gate=m_w_ple_gate, w_ple_proj=m_w_ple_proj)
    mom2 = dict(mix_pre_g=v_mix_pre_g, mix_post_g=v_mix_post_g, ffn_pre_g=v_ffn_pre_g, ffn_post_g=v_ffn_post_g, pool_w=v_pool_w, pool_scale=v_pool_scale, kv_norm_g=v_kv_norm_g, w_k=v_w_k, w_v=v_w_v, w_q=v_w_q, w_o=v_w_o, sinks=v_sinks, w_ff_gate=v_w_ff_gate, w_ff_up=v_w_ff_up, w_ff_down=v_w_ff_down, ple_norm_g=v_ple_norm_g, w_ple_gate=v_w_ple_gate, w_ple_proj=v_w_ple_proj)

    def land(i, k):
        return sum_parts(landing[i][k])

    gw = dict(small)
    gw["kv_norm_g"] = small["kv_norm_g"]
    gw["w_ff_gate"] = jnp.stack([land(i, 0).T for i in range(depth)])
    gw["w_ff_up"] = jnp.stack([land(i, 1).T for i in range(depth)])
    gw["w_ff_down"] = jnp.stack([land(i, 2) for i in range(depth)])
    gw["w_ple_gate"] = jnp.stack([land(i, 3) for i in range(depth)])
    gw["w_ple_proj"] = jnp.stack([land(i, 4).T for i in range(depth)])
    gw["pool_w"] = jnp.stack([land(i, 5).reshape(pool_w.shape[1:]) for i in range(n_a)])
    gw["w_q"] = jnp.stack([_unpair_heads(land(i, 5), 1) for i in range(n_a, depth)])
    gw["w_o"] = jnp.stack([land(i, 6) for i in range(n_a, depth)])
    gkv = land(n_a, 7)
    gw["w_k"], gw["w_v"] = gkv[:, :w_k.shape[1]], gkv[:, w_k.shape[1]:]

    order = ["mix_pre_g", "mix_post_g", "ffn_pre_g", "ffn_post_g", "pool_w", "pool_scale", "kv_norm_g", "w_k", "w_v",
             "w_q", "w_o", "sinks", "w_ff_gate", "w_ff_up", "w_ff_down", "ple_norm_g", "w_ple_gate", "w_ple_proj"]
    g_out, d_out, m_out, v_out = [], [], [], []
    for nme in order:
        w = weights[nme]
        as2d = (lambda a: a[None, :]) if w.ndim == 1 else (lambda a: a)
        g, dl, nm, nv = _adamw_nd(as2d(w), as2d(mom1[nme]), as2d(mom2[nme]), as2d(gw[nme]))
        for lst, val in ((g_out, g), (d_out, dl), (m_out, nm), (v_out, nv)):
            lst.append(val.reshape(w.shape))
    return (loss, grad_x, *g_out, *d_out, *m_out, *v_out)
```

```python
import functools

import jax
import jax.numpy as jnp
from jax import lax
from jax.experimental import pallas as pl
from jax.experimental.pallas import tpu as pltpu

F32 = jnp.float32
BF16 = jnp.bfloat16

N_DEV = 8
HEAD_DIM = 64
N_HEADS = 16
N_KV_HEADS = 4
GQA = N_HEADS // N_KV_HEADS
BLOCK = 128
POOL_WINDOWS = (2, 4, 8, 16)
POOL_GROUP = 256
HALO = 16
ROPE_THETA = 10000.0
RMS_EPS = 1e-6
NEG_INF = -1e30
LANES = 128
XTY_ROWS = 2048
FFN_CHUNK = 768
VMEM_LIMIT = 56 * 1024 * 1024

ADAM_LR = 0.001
ADAM_B1 = 0.9
ADAM_B2 = 0.999
ADAM_EPS = 1e-08
ADAM_WD = 0.01
ADAM_STEP = 10

MESH = pl.DeviceIdType.MESH
ANY = pl.BlockSpec(memory_space=pl.ANY)

NT_DIMS = (((1,), (1,)), ((), ()))
TN_DIMS = (((0,), (0,)), ((), ()))


def _cparams(sem=None, vmem=None):
    kw = {}
    if sem is not None:
        kw["dimension_semantics"] = sem
    if vmem is not None:
        kw["vmem_limit_bytes"] = vmem
    return pltpu.CompilerParams(**kw)


def _rows(tm, n, first=0):
    return pl.BlockSpec((tm, n), lambda i: (i + first, 0))


def _rows_rev(tm, n, nt):
    return pl.BlockSpec((tm, n), lambda i: (nt - 1 - i, 0))


def _const(shape):
    nd = len(shape)
    return pl.BlockSpec(shape, lambda *_: (0,) * nd, pipeline_mode=pl.Buffered(1))


def _resident(shape):
    nd = len(shape)
    return pl.BlockSpec(shape, lambda *_: (0,) * nd)


def _tile_rows(t):
    return 512 if t % 512 == 0 else 128


def _dot(a, b):
    return jnp.dot(a, b, preferred_element_type=F32)


def _dot_nt(a, b):
    return lax.dot_general(a, b, NT_DIMS, preferred_element_type=F32)


def _dot_tn(a, b):
    return lax.dot_general(a, b, TN_DIMS, preferred_element_type=F32)


def _rms_r(x):
    return lax.rsqrt(jnp.mean(x * x, axis=-1, keepdims=True) + RMS_EPS)


def _rms_bwd(x, r, g, dy):
    gy = dy * g
    dx = r * gy - x * (r * r * r * jnp.mean(gy * x, axis=-1, keepdims=True))
    dg = jnp.sum(dy * (x * r), axis=0, keepdims=True)
    return dx, dg


def _sigmoid(x):
    return jax.nn.sigmoid(x)


def _rope_tables(t):
    inv = 1.0 / (ROPE_THETA ** (jnp.arange(0, HEAD_DIM, 2, dtype=F32) / HEAD_DIM))
    ang = jnp.arange(t, dtype=F32)[:, None] * inv[None, :]
    c, s = jnp.cos(ang), jnp.sin(ang)
    cos = jnp.concatenate([c, c, c, c], axis=1)
    sin = jnp.concatenate([-s, s, -s, s], axis=1)
    return cos, sin


def _swap_halves(x):
    n = x.shape[1]
    lane = lax.broadcasted_iota(jnp.int32, x.shape, 1)
    first = (lane % HEAD_DIM) < (HEAD_DIM // 2)
    return jnp.where(first, pltpu.roll(x, n - HEAD_DIM // 2, 1), pltpu.roll(x, HEAD_DIM // 2, 1))


def _rope(x, cos, sin):
    reps = x.shape[1] // LANES
    return x * jnp.tile(cos, (1, reps)) + _swap_halves(x) * jnp.tile(sin, (1, reps))


def _unrope(dy, cos, sin):
    reps = dy.shape[1] // LANES
    return dy * jnp.tile(cos, (1, reps)) + _swap_halves(dy * jnp.tile(sin, (1, reps)))


def _acc_init(acc_ref):
    @pl.when(pl.program_id(0) == 0)
    def _():
        acc_ref[...] = jnp.zeros_like(acc_ref)


def _window_sums(ext, tm, forward):
    n = tm + HALO
    out = []
    for g, w in enumerate(POOL_WINDOWS):
        s = ext[:, g * POOL_GROUP:(g + 1) * POOL_GROUP]
        k = 1
        while k < w:
            s = s + pltpu.roll(s, k if forward else n - k, 0)
            k *= 2
        out.append(s[HALO:, :] if forward else s[:tm, :])
    return out


def _pool_counts(tile, tm):
    t = tile * tm + lax.broadcasted_iota(jnp.int32, (tm, 1), 0)
    return [jnp.minimum(t + 1, w).astype(F32) for w in POOL_WINDOWS]


def _pool_mix(hn, ext, cnts, pw_ref, scale, tm):
    sums = _window_sums(ext, tm, True)
    pooled, ys = [], []
    for g in range(len(POOL_WINDOWS)):
        pg = (sums[g] / cnts[g] - hn[:, g * POOL_GROUP:(g + 1) * POOL_GROUP]).astype(BF16)
        pooled.append(pg)
        ys.append(_dot(pg, pw_ref[g]))
    y = jnp.concatenate(ys, axis=1)
    return pooled, y, y * scale


def pool_mix_fwd(h0, gpre, pool_w, scale, gpost, gffn):
    t, d = h0.shape
    tm = _tile_rows(t)

    def body(h_ref, gpre_ref, pw_ref, scale_ref, gpost_ref, gffn_ref, h1_ref, a_ref, carry):
        i = pl.program_id(0)

        @pl.when(i == 0)
        def _():
            carry[...] = jnp.zeros_like(carry)

        x = h_ref[...]
        hn = x * _rms_r(x) * gpre_ref[...]
        ext = jnp.concatenate([carry[...], hn], axis=0)
        carry[...] = hn[tm - HALO:, :]
        _, _, m = _pool_mix(hn, ext, _pool_counts(i, tm), pw_ref, scale_ref[...], tm)
        h1 = x + m * _rms_r(m) * gpost_ref[...]
        h1_ref[...] = h1
        a_ref[...] = (h1 * _rms_r(h1) * gffn_ref[...]).astype(BF16)

    return pl.pallas_call(
        functools.partial(body), name="pool_mix_fwd", grid=(t // tm,),
        in_specs=[_rows(tm, d), _const((1, d)), _const(pool_w.shape), _const((1, d)), _const((1, d)), _const((1, d))],
        out_specs=[_rows(tm, d), _rows(tm, d)],
        out_shape=[jax.ShapeDtypeStruct((t, d), F32), jax.ShapeDtypeStruct((t, d), BF16)],
        scratch_shapes=[pltpu.VMEM((HALO, d), F32)],
        compiler_params=_cparams(("arbitrary",), VMEM_LIMIT),
    )(h0, gpre, pool_w, scale, gpost, gffn)


def pool_mix_bwd(h0, dh2, da, gpre, pool_w, scale, gpost, gffn):
    t, d = h0.shape
    tm = _tile_rows(t)
    nt = t // tm
    hb = tm // HALO

    def body(h_ref, halo_ref, dh2_ref, da_ref, gpre_ref, pw_ref, scale_ref, gpost_ref, gffn_ref,
             dh0_ref, dpw_ref, gacc_ref, carry):
        i = pl.program_id(0)
        tile = nt - 1 - i
        _acc_init(gacc_ref)
        _acc_init(dpw_ref)

        @pl.when(i == 0)
        def _():
            carry[...] = jnp.zeros_like(carry)

        x = h_ref[...]
        gpre_v, scale_v, gpost_v, gffn_v = gpre_ref[...], scale_ref[...], gpost_ref[...], gffn_ref[...]
        r0 = _rms_r(x)
        hn = x * r0 * gpre_v
        xh = halo_ref[...]
        hn_halo = jnp.where(tile > 0, xh * _rms_r(xh) * gpre_v, 0.0)
        ext = jnp.concatenate([hn_halo, hn], axis=0)
        cnts = _pool_counts(tile, tm)
        pooled, y, m = _pool_mix(hn, ext, cnts, pw_ref, scale_v, tm)
        rm = _rms_r(m)
        h1 = x + m * rm * gpost_v
        dh1_n, dgffn = _rms_bwd(h1, _rms_r(h1), gffn_v, da_ref[...])
        dh1 = dh2_ref[...] + dh1_n
        dm, dgpost = _rms_bwd(m, rm, gpost_v, dh1)
        dscale = jnp.sum(dm * y, axis=0, keepdims=True)
        dy = (dm * scale_v).astype(BF16)
        dpn = []
        for g in range(len(POOL_WINDOWS)):
            dyg = dy[:, g * POOL_GROUP:(g + 1) * POOL_GROUP]
            dpw_ref[g] += _dot_tn(pooled[g], dyg)
            dpn.append(_dot_nt(dyg, pw_ref[g]))
        dpooled = jnp.concatenate(dpn, axis=1)
        dpc = jnp.concatenate([dpn[g] / cnts[g] for g in range(len(POOL_WINDOWS))], axis=1)
        ext2 = jnp.concatenate([dpc, carry[...]], axis=0)
        carry[...] = dpc[:HALO, :]
        dhn = jnp.concatenate(_window_sums(ext2, tm, False), axis=1) - dpooled
        dh0_n, dgpre = _rms_bwd(x, r0, gpre_v, dhn)
        dh0_ref[...] = dh1 + dh0_n
        gacc_ref[0:1, :] += dgpre
        gacc_ref[1:2, :] += dgpost
        gacc_ref[2:3, :] += dgffn
        gacc_ref[3:4, :] += dscale

    return pl.pallas_call(
        functools.partial(body), name="pool_mix_bwd", grid=(nt,),
        in_specs=[_rows_rev(tm, d, nt),
                  pl.BlockSpec((HALO, d), lambda i: (jnp.maximum((nt - 1 - i) * hb - 1, 0), 0)),
                  _rows_rev(tm, d, nt), _rows_rev(tm, d, nt),
                  _const((1, d)), _const(pool_w.shape), _const((1, d)), _const((1, d)), _const((1, d))],
        out_specs=[_rows_rev(tm, d, nt), _resident(pool_w.shape), _resident((8, d))],
        out_shape=[jax.ShapeDtypeStruct((t, d), F32), jax.ShapeDtypeStruct(pool_w.shape, F32),
                   jax.ShapeDtypeStruct((8, d), F32)],
        scratch_shapes=[pltpu.VMEM((HALO, d), F32)],
        compiler_params=_cparams(("arbitrary",), VMEM_LIMIT),
    )(h0, h0, dh2, da, gpre, pool_w, scale, gpost, gffn)


def _ffn_chunks(f):
    return [(c, min(c + FFN_CHUNK, f)) for c in range(0, f, FFN_CHUNK)]


def ffn_fwd(a, wg_t, wu_t, wd):
    t, d = a.shape
    f = wd.shape[0]
    tm = _tile_rows(t)

    def body(a_ref, wg_ref, wu_ref, wd_ref, f_ref, gte_ref, up_ref, hdn_ref):
        av = a_ref[...]
        acc = jnp.zeros((tm, d), F32)
        for c0, c1 in _ffn_chunks(f):
            gte = _dot_nt(av, wg_ref[c0:c1, :])
            up = _dot_nt(av, wu_ref[c0:c1, :])
            gte_ref[:, c0:c1] = gte.astype(BF16)
            up_ref[:, c0:c1] = up.astype(BF16)
            hdn = (gte * _sigmoid(gte) * up).astype(BF16)
            hdn_ref[:, c0:c1] = hdn
            acc = acc + _dot(hdn, wd_ref[c0:c1, :])
        f_ref[...] = acc

    return pl.pallas_call(
        functools.partial(body), name="ffn_fwd", grid=(t // tm,),
        in_specs=[_rows(tm, d), _const((f, d)), _const((f, d)), _const((f, d))],
        out_specs=[_rows(tm, d), _rows(tm, f), _rows(tm, f), _rows(tm, f)],
        out_shape=[jax.ShapeDtypeStruct((t, d), F32)] + [jax.ShapeDtypeStruct((t, f), BF16)] * 3,
        compiler_params=_cparams(("parallel",), VMEM_LIMIT),
    )(a, wg_t, wu_t, wd)


def ffn_bwd_act(df, gte, up, wg_t, wu_t, wd):
    t, d = df.shape
    f = wd.shape[0]
    tm = _tile_rows(t)

    def body(df_ref, gte_ref, up_ref, wg_ref, wu_ref, wd_ref, da_ref, dgte_ref, dup_ref):
        dfv = df_ref[...]
        for c0, c1 in _ffn_chunks(f):
            g = gte_ref[:, c0:c1].astype(F32)
            u = up_ref[:, c0:c1].astype(F32)
            sg = _sigmoid(g)
            sl = g * sg
            dh = _dot_nt(dfv, wd_ref[c0:c1, :])
            dup_ref[:, c0:c1] = (dh * sl).astype(BF16)
            dgte_ref[:, c0:c1] = (dh * u * (sg * (1.0 + g * (1.0 - sg)))).astype(BF16)
        da_ref[...] = _dot(dgte_ref[...], wg_ref[...]) + _dot(dup_ref[...], wu_ref[...])

    return pl.pallas_call(
        functools.partial(body), name="ffn_bwd_act", grid=(t // tm,),
        in_specs=[_rows(tm, d), _rows(tm, f), _rows(tm, f), _const((f, d)), _const((f, d)), _const((f, d))],
        out_specs=[_rows(tm, d), _rows(tm, f), _rows(tm, f)],
        out_shape=[jax.ShapeDtypeStruct((t, d), F32)] + [jax.ShapeDtypeStruct((t, f), BF16)] * 2,
        compiler_params=_cparams(("parallel",), VMEM_LIMIT),
    )(df, gte, up, wg_t, wu_t, wd)


def xty(x, y, y_part=0):
    t, nx = x.shape
    ny = y.shape[1]
    tk = XTY_ROWS if t % XTY_ROWS == 0 else _tile_rows(t)
    bn = nx // 2 if nx > 1024 else nx
    nk = t // tk

    def body(x_ref, y_ref, o_ref, acc):
        k = pl.program_id(1)

        @pl.when(k == 0)
        def _():
            acc[...] = jnp.zeros_like(acc)

        acc[...] += _dot_tn(x_ref[...].astype(BF16), y_ref[...].astype(BF16))

        @pl.when(k == nk - 1)
        def _():
            o_ref[...] = acc[...].astype(BF16)

    return pl.pallas_call(
        functools.partial(body), name="xty", grid=(nx // bn, nk),
        in_specs=[pl.BlockSpec((tk, bn), lambda j, k: (k, j)),
                  pl.BlockSpec((tk, ny), lambda j, k: (k + y_part * nk, 0))],
        out_specs=pl.BlockSpec((bn, ny), lambda j, k: (j, 0)),
        out_shape=jax.ShapeDtypeStruct((nx, ny), BF16),
        scratch_shapes=[pltpu.VMEM((bn, ny), F32)],
        compiler_params=_cparams(("parallel", "arbitrary"), VMEM_LIMIT),
    )(x, y)


def _ple_fwd_tile(h1, f, p, gpost, gple, wpg_ref, wpp_ref):
    rf = _rms_r(f)
    h2 = h1 + f * rf * gpost
    r2 = _rms_r(h2)
    ub = (h2 * r2 * gple).astype(BF16)
    gate = _sigmoid(_dot(ub, wpg_ref[...]))
    pp = _dot_nt(p.astype(BF16), wpp_ref[...])
    return rf, h2, r2, ub, gate, pp


def post_ple_fwd(h1, f, p, layer, gpost, gple, wpg, wpp_t, target=None):
    t, d = h1.shape
    pd = p.shape[1]
    tm = _tile_rows(t)
    with_loss = target is not None

    def body(*refs):
        if with_loss:
            h1_ref, f_ref, p_ref, gpost_ref, gple_ref, wpg_ref, wpp_ref, tgt_ref, out_ref, loss_ref = refs
        else:
            h1_ref, f_ref, p_ref, gpost_ref, gple_ref, wpg_ref, wpp_ref, out_ref = refs
        _, h2, _, _, gate, pp = _ple_fwd_tile(h1_ref[...], f_ref[...], p_ref[...], gpost_ref[...], gple_ref[...],
                                              wpg_ref, wpp_ref)
        h3 = h2 + pp * gate
        if with_loss:
            err = h3 - tgt_ref[...]
            out_ref[...] = err * (1.0 / d)
            colsum = jnp.sum(err * err, axis=0, keepdims=True) * (0.5 / d)
            loss_ref[...] = jnp.broadcast_to(colsum, (8, d)) * (lax.broadcasted_iota(jnp.int32, (8, d), 0) == 0)
        else:
            out_ref[...] = h3

    in_specs = [_rows(tm, d), _rows(tm, d), _rows(tm, pd, layer * (t // tm)), _const((1, d)), _const((1, d)),
                _const(wpg.shape), _const(wpp_t.shape)]
    out_specs = [_rows(tm, d)]
    out_shape = [jax.ShapeDtypeStruct((t, d), F32)]
    args = [h1, f, p, gpost, gple, wpg, wpp_t]
    if with_loss:
        in_specs.append(_rows(tm, d))
        out_specs.append(_rows(8, d))
        out_shape.append(jax.ShapeDtypeStruct((t // tm * 8, d), F32))
        args.append(target)
    return pl.pallas_call(
        functools.partial(body), name="post_ple_loss" if with_loss else "post_ple_fwd", grid=(t // tm,),
        in_specs=in_specs, out_specs=out_specs, out_shape=out_shape,
        compiler_params=_cparams(("parallel",), VMEM_LIMIT),
    )(*args)


def post_ple_bwd(dh3, h1, f, p, layer, gpost, gple, wpg, wpp_t):
    t, d = h1.shape
    pd = p.shape[1]
    tm = _tile_rows(t)

    def body(dh3_ref, h1_ref, f_ref, p_ref, gpost_ref, gple_ref, wpg_ref, wpp_ref,
             dh2_ref, df_ref, u_ref, dz_ref, dpp_ref, gacc_ref):
        _acc_init(gacc_ref)
        fv = f_ref[...]
        gpost_v, gple_v = gpost_ref[...], gple_ref[...]
        rf, h2, r2, ub, gate, pp = _ple_fwd_tile(h1_ref[...], fv, p_ref[...], gpost_v, gple_v, wpg_ref, wpp_ref)
        dh3v = dh3_ref[...]
        dpp_ref[...] = (dh3v * gate).astype(BF16)
        dz = (dh3v * pp * gate * (1.0 - gate)).astype(BF16)
        dz_ref[...] = dz
        u_ref[...] = ub
        du = _dot_nt(dz, wpg_ref[...])
        dh2_n, dgple = _rms_bwd(h2, r2, gple_v, du)
        dh2 = dh3v + dh2_n
        df, dgpost = _rms_bwd(fv, rf, gpost_v, dh2)
        dh2_ref[...] = dh2
        df_ref[...] = df.astype(BF16)
        gacc_ref[0:1, :] += dgple
        gacc_ref[1:2, :] += dgpost

    return pl.pallas_call(
        functools.partial(body), name="post_ple_bwd", grid=(t // tm,),
        in_specs=[_rows(tm, d), _rows(tm, d), _rows(tm, d), _rows(tm, pd, layer * (t // tm)), _const((1, d)),
                  _const((1, d)), _const(wpg.shape), _const(wpp_t.shape)],
        out_specs=[_rows(tm, d)] * 5 + [_resident((8, d))],
        out_shape=[jax.ShapeDtypeStruct((t, d), F32)] + [jax.ShapeDtypeStruct((t, d), BF16)] * 4
        + [jax.ShapeDtypeStruct((8, d), F32)],
        compiler_params=_cparams(("arbitrary",), VMEM_LIMIT),
    )(dh3, h1, f, p, gpost, gple, wpg, wpp_t)


def proj_rope_fwd(h, gain, w, cos, sin, n_rope, name):
    t, d = h.shape
    n = w.shape[1]
    tm = _tile_rows(t)

    def body(h_ref, g_ref, w_ref, cos_ref, sin_ref, hn_ref, y_ref):
        x = h_ref[...]
        hn = (x * _rms_r(x) * g_ref[...]).astype(BF16)
        hn_ref[...] = hn
        y = _dot(hn, w_ref[...])
        y_ref[:, :n_rope] = _rope(y[:, :n_rope], cos_ref[...], sin_ref[...]).astype(BF16)
        if n_rope < n:
            y_ref[:, n_rope:] = y[:, n_rope:].astype(BF16)

    return pl.pallas_call(
        functools.partial(body), name=name, grid=(t // tm,),
        in_specs=[_rows(tm, d), _const((1, d)), _const(w.shape), _rows(tm, LANES), _rows(tm, LANES)],
        out_specs=[_rows(tm, d), _rows(tm, n)],
        out_shape=[jax.ShapeDtypeStruct((t, d), BF16), jax.ShapeDtypeStruct((t, n), BF16)],
        compiler_params=_cparams(("parallel",), VMEM_LIMIT),
    )(h, gain, w, cos, sin)


def proj_rope_bwd(dh1, h0, cos, sin, branches, name):
    t, d = h0.shape
    tm = _tile_rows(t)
    nb = len(branches)
    n_cot = [len(b[3]) for b in branches]

    def body(*refs):
        dh1_ref, h0_ref, cos_ref, sin_ref = refs[:4]
        pos = 4
        br_refs = []
        for b in range(nb):
            br_refs.append((refs[pos], refs[pos + 1], refs[pos + 2:pos + 2 + n_cot[b]]))
            pos += 2 + n_cot[b]
        dh0_ref = refs[pos]
        dpre_refs = refs[pos + 1:pos + 1 + nb]
        gacc_ref = refs[pos + 1 + nb]
        _acc_init(gacc_ref)
        x = h0_ref[...]
        r0 = _rms_r(x)
        dh = dh1_ref[...]
        for b in range(nb):
            g_ref, w_ref, cot_refs = br_refs[b]
            n_rope = branches[b][2]
            dy = cot_refs[0][...].astype(F32)
            for c_ref in cot_refs[1:]:
                dy = dy + c_ref[...].astype(F32)
            n = dy.shape[1]
            dpre_refs[b][:, :n_rope] = _unrope(dy[:, :n_rope], cos_ref[...], sin_ref[...]).astype(BF16)
            if n_rope < n:
                dpre_refs[b][:, n_rope:] = dy[:, n_rope:].astype(BF16)
            dhn = _dot_nt(dpre_refs[b][...], w_ref[...])
            dx, dg = _rms_bwd(x, r0, g_ref[...], dhn)
            dh = dh + dx
            gacc_ref[b:b + 1, :] += dg
        dh0_ref[...] = dh

    in_specs = [_rows(tm, d), _rows(tm, d), _rows(tm, LANES), _rows(tm, LANES)]
    args = [dh1, h0, cos, sin]
    out_specs = [_rows(tm, d)]
    out_shape = [jax.ShapeDtypeStruct((t, d), F32)]
    for gain, w, _, cots in branches:
        n = w.shape[1]
        in_specs += [_const((1, d)), _const(w.shape)] + [_rows(tm, n)] * len(cots)
        args += [gain, w] + list(cots)
        out_specs.append(_rows(tm, n))
        out_shape.append(jax.ShapeDtypeStruct((t, n), BF16))
    out_specs.append(_resident((8, d)))
    out_shape.append(jax.ShapeDtypeStruct((8, d), F32))
    return pl.pallas_call(
        functools.partial(body), name=name, grid=(t // tm,),
        in_specs=in_specs, out_specs=out_specs, out_shape=out_shape,
        compiler_params=_cparams(("arbitrary",), VMEM_LIMIT),
    )(*args)


def _tri():
    row = lax.broadcasted_iota(jnp.int32, (BLOCK, BLOCK), 0)
    col = lax.broadcasted_iota(jnp.int32, (BLOCK, BLOCK), 1)
    return col <= row


def _block_diag(x):
    lo = lax.broadcasted_iota(jnp.int32, x.shape, 1) < HEAD_DIM
    zero = jnp.zeros_like(x)
    return jnp.concatenate([jnp.where(lo, x, zero), jnp.where(lo, zero, x)], axis=0)


def _dense(x, tri):
    return (jnp.where(tri, x[:, BLOCK:2 * BLOCK], x[:, :BLOCK]),
            jnp.where(tri, x[:, 3 * BLOCK:], x[:, 2 * BLOCK:3 * BLOCK]))


def _banded(xa, xb, tri):
    zero = jnp.zeros_like(xa)
    return jnp.concatenate([jnp.where(tri, zero, xa), jnp.where(tri, xa, zero),
                            jnp.where(tri, zero, xb), jnp.where(tri, xb, zero)], axis=1).astype(BF16)


def _softmax_sink(s, sink):
    mx = jnp.maximum(jnp.max(s, axis=1, keepdims=True), sink)
    e = jnp.exp(s - mx)
    es = jnp.exp(sink - mx)
    inv = 1.0 / (jnp.sum(e, axis=1, keepdims=True) + es)
    return e * inv, es * inv


def _sink_column(sink_ref):
    return jnp.concatenate([jnp.broadcast_to(sink_ref[h:h + 1, 0:1], (BLOCK, 1)) for h in range(N_HEADS)], axis=0)


def _kv_block_diag(band, kvw):
    n_lt = kvw // LANES
    return ([_block_diag(band[:, lt * LANES:(lt + 1) * LANES]) for lt in range(n_lt)],
            [_block_diag(band[:, kvw + lt * LANES:kvw + (lt + 1) * LANES]) for lt in range(n_lt)])


def _all_probs(q_ref, kbd, tri, n, sink_ref):
    dense = []
    for tq in range(N_HEADS // 2):
        s = _dot_nt(q_ref[:, tq * LANES:(tq + 1) * LANES], kbd[tq // GQA])
        dense += list(_dense(s, tri))
    bias = jnp.where(jnp.logical_not(tri) & (n == 0), NEG_INF, 0.0)
    s_all = jnp.concatenate(dense, axis=0) * (HEAD_DIM ** -0.5) + jnp.concatenate([bias] * N_HEADS, axis=0)
    return _softmax_sink(s_all, _sink_column(sink_ref))


def _head_rows(x, tq):
    return x[2 * tq * BLOCK:(2 * tq + 1) * BLOCK], x[(2 * tq + 1) * BLOCK:(2 * tq + 2) * BLOCK]


def swa_fwd(q, kv, sink_b):
    t, d = q.shape
    nb = t // BLOCK
    kvw = N_KV_HEADS * HEAD_DIM

    def body(q_ref, kvc_ref, kvp_ref, sink_ref, o_ref):
        n = pl.program_id(0)
        tri = _tri()
        kbd, vbd = _kv_block_diag(jnp.concatenate([kvp_ref[...], kvc_ref[...]], axis=0), kvw)
        p, _ = _all_probs(q_ref, kbd, tri, n, sink_ref)
        for tq in range(N_HEADS // 2):
            pa, pb = _head_rows(p, tq)
            o_ref[:, tq * LANES:(tq + 1) * LANES] = _dot(_banded(pa, pb, tri), vbd[tq // GQA]).astype(BF16)

    return pl.pallas_call(
        functools.partial(body), name="swa_fwd", grid=(nb,),
        in_specs=[_rows(BLOCK, d), _rows(BLOCK, 2 * kvw),
                  pl.BlockSpec((BLOCK, 2 * kvw), lambda n: (jnp.maximum(n - 1, 0), 0)), _const(sink_b.shape)],
        out_specs=_rows(BLOCK, d),
        out_shape=jax.ShapeDtypeStruct((t, d), BF16),
        compiler_params=_cparams(("parallel",), VMEM_LIMIT),
    )(q, kv, kv, sink_b)


def swa_bwd(q, kv, do, sink_b):
    t, d = q.shape
    nb = t // BLOCK
    kvw = N_KV_HEADS * HEAD_DIM

    def body(q_ref, do_ref, kvc_ref, kvp_ref, sink_ref, dq_ref, dkv_ref, dsink_ref, carry):
        i = pl.program_id(0)
        n = nb - 1 - i
        _acc_init(dsink_ref)

        @pl.when(i == 0)
        def _():
            carry[...] = jnp.zeros_like(carry)

        tri = _tri()
        lo = lax.broadcasted_iota(jnp.int32, (2 * BLOCK, LANES), 1) < HEAD_DIM
        kbd, vbd = _kv_block_diag(jnp.concatenate([kvp_ref[...], kvc_ref[...]], axis=0), kvw)
        p, ps = _all_probs(q_ref, kbd, tri, n, sink_ref)
        dp = []
        for tq in range(N_HEADS // 2):
            dp += list(_dense(_dot_nt(do_ref[:, tq * LANES:(tq + 1) * LANES], vbd[tq // GQA]), tri))
        dp = jnp.concatenate(dp, axis=0)
        delta = jnp.sum(p * dp, axis=1, keepdims=True)
        ds = p * (dp - delta) * (HEAD_DIM ** -0.5)
        dsk = ps * delta
        for h in range(N_HEADS):
            dsink_ref[h:h + 1, :] -= jnp.sum(dsk[h * BLOCK:(h + 1) * BLOCK], axis=0, keepdims=True)
        dkb = [jnp.zeros((4 * BLOCK, LANES), F32) for _ in kbd]
        dvb = [jnp.zeros((4 * BLOCK, LANES), F32) for _ in kbd]
        for tq in range(N_HEADS // 2):
            lt = tq // GQA
            cols = slice(tq * LANES, (tq + 1) * LANES)
            dsb = _banded(*_head_rows(ds, tq), tri)
            dq_ref[:, cols] = _dot(dsb, kbd[lt]).astype(BF16)
            dkb[lt] = dkb[lt] + _dot_tn(dsb, q_ref[:, cols])
            dvb[lt] = dvb[lt] + _dot_tn(_banded(*_head_rows(p, tq), tri), do_ref[:, cols])
        dall = jnp.concatenate([jnp.where(lo, x[:2 * BLOCK], x[2 * BLOCK:]) for x in dkb + dvb], axis=1)
        dkv_ref[...] = dall[BLOCK:, :] + carry[...]
        carry[...] = dall[:BLOCK, :]

    rev = lambda i: (nb - 1 - i, 0)
    return pl.pallas_call(
        functools.partial(body), name="swa_bwd", grid=(nb,),
        in_specs=[pl.BlockSpec((BLOCK, d), rev), pl.BlockSpec((BLOCK, d), rev), pl.BlockSpec((BLOCK, 2 * kvw), rev),
                  pl.BlockSpec((BLOCK, 2 * kvw), lambda i: (jnp.maximum(nb - 2 - i, 0), 0)), _const(sink_b.shape)],
        out_specs=[pl.BlockSpec((BLOCK, d), rev), pl.BlockSpec((BLOCK, 2 * kvw), rev), _resident(sink_b.shape)],
        out_shape=[jax.ShapeDtypeStruct((t, d), BF16), jax.ShapeDtypeStruct((t, 2 * kvw), F32),
                   jax.ShapeDtypeStruct(sink_b.shape, F32)],
        scratch_shapes=[pltpu.VMEM((BLOCK, 2 * kvw), F32)],
        compiler_params=_cparams(("arbitrary",), VMEM_LIMIT),
    )(q, do, kv, kv, sink_b)


def oproj_post_fwd(attn, w_o, h0, gpost, gffn):
    t, d = h0.shape
    tm = _tile_rows(t)

    def body(at_ref, w_ref, h0_ref, gpost_ref, gffn_ref, m_ref, h1_ref, a_ref):
        m = _dot(at_ref[...], w_ref[...])
        m_ref[...] = m
        h1 = h0_ref[...] + m * _rms_r(m) * gpost_ref[...]
        h1_ref[...] = h1
        a_ref[...] = (h1 * _rms_r(h1) * gffn_ref[...]).astype(BF16)

    return pl.pallas_call(
        functools.partial(body), name="oproj_post_fwd", grid=(t // tm,),
        in_specs=[_rows(tm, d), _const(w_o.shape), _rows(tm, d), _const((1, d)), _const((1, d))],
        out_specs=[_rows(tm, d)] * 3,
        out_shape=[jax.ShapeDtypeStruct((t, d), F32), jax.ShapeDtypeStruct((t, d), F32),
                   jax.ShapeDtypeStruct((t, d), BF16)],
        compiler_params=_cparams(("parallel",), VMEM_LIMIT),
    )(attn, w_o, h0, gpost, gffn)


def oproj_post_bwd(dh2, da, h1, m, w_o, gpost, gffn):
    t, d = h1.shape
    tm = _tile_rows(t)

    def body(dh2_ref, da_ref, h1_ref, m_ref, w_ref, gpost_ref, gffn_ref, dh1_ref, dm_ref, dat_ref, gacc_ref):
        _acc_init(gacc_ref)
        h1v, mv = h1_ref[...], m_ref[...]
        dh1_n, dgffn = _rms_bwd(h1v, _rms_r(h1v), gffn_ref[...], da_ref[...])
        dh1 = dh2_ref[...] + dh1_n
        dm, dgpost = _rms_bwd(mv, _rms_r(mv), gpost_ref[...], dh1)
        dmb = dm.astype(BF16)
        dh1_ref[...] = dh1
        dm_ref[...] = dmb
        dat_ref[...] = _dot_nt(dmb, w_ref[...]).astype(BF16)
        gacc_ref[0:1, :] += dgpost
        gacc_ref[1:2, :] += dgffn

    return pl.pallas_call(
        functools.partial(body), name="oproj_post_bwd", grid=(t // tm,),
        in_specs=[_rows(tm, d)] * 4 + [_const(w_o.shape), _const((1, d)), _const((1, d))],
        out_specs=[_rows(tm, d)] * 3 + [_resident((8, d))],
        out_shape=[jax.ShapeDtypeStruct((t, d), F32), jax.ShapeDtypeStruct((t, d), BF16),
                   jax.ShapeDtypeStruct((t, d), BF16), jax.ShapeDtypeStruct((8, d), F32)],
        compiler_params=_cparams(("arbitrary",), VMEM_LIMIT),
    )(dh2, da, h1, m, w_o, gpost, gffn)


def _my_place():
    return lax.axis_index("x"), lax.axis_index("y"), lax.axis_index("c")


def _block_index(px, py, pc):
    return 4 * px + 2 * py + pc


def allgather_pieces(shards, name):
    np_ = len(shards)

    def body(*refs):
        in_refs, out_refs = refs[:np_], refs[np_:2 * np_]
        send_sems, recv_sems, local_sems = refs[2 * np_:]
        x, y, c = _my_place()
        me, sibling = (x, y, c), (x, y, 1 - c)
        chips = [(1 - x, y), (x, 1 - y), (1 - x, 1 - y)]

        def rows(p, place):
            r = in_refs[p].shape[0]
            return out_refs[p].at[pl.ds(_block_index(*place) * r, r), :]

        def copy(p, k, block, to, src=None):
            return pltpu.make_async_remote_copy(
                src_ref=rows(p, block) if src is None else src, dst_ref=rows(p, block),
                send_sem=send_sems.at[p, k], recv_sem=recv_sems.at[p, k], device_id=to, device_id_type=MESH)

        mine = [pltpu.make_async_copy(in_refs[p], rows(p, me), local_sems.at[p]) for p in range(np_)]
        first, passed = [], []
        for p in range(np_):
            mine[p].start()
            first.append(copy(p, 0, me, sibling, src=in_refs[p]))
            first += [copy(p, 1 + j, me, (*chip, c), src=in_refs[p]) for j, chip in enumerate(chips)]
        for cp in first:
            cp.start()
        for p in range(np_):
            for j, chip in enumerate(chips):
                copy(p, 1 + j, (*chip, c), me).wait_recv()
                fwd = copy(p, 4 + j, (*chip, c), sibling)
                fwd.start()
                passed.append(fwd)
        for p in range(np_):
            copy(p, 0, sibling, me).wait_recv()
            for j, chip in enumerate(chips):
                copy(p, 4 + j, (*chip, 1 - c), me).wait_recv()
        for cp in first + passed:
            cp.wait_send()
        for cp in mine:
            cp.wait()

    return pl.pallas_call(
        functools.partial(body), name=name,
        in_specs=[ANY] * np_, out_specs=[ANY] * np_,
        out_shape=[jax.ShapeDtypeStruct((N_DEV * s.shape[0], s.shape[1]), s.dtype) for s in shards],
        scratch_shapes=[pltpu.SemaphoreType.DMA((np_, 7)), pltpu.SemaphoreType.DMA((np_, 7)),
                        pltpu.SemaphoreType.DMA((np_,))],
    )(*shards)


def _peers():
    x, y, c = _my_place()
    flips = [(fx, fy, fc) for fx in (0, 1) for fy in (0, 1) for fc in (0, 1)][1:]
    return [(1 - x if fx else x, 1 - y if fy else y, 1 - c if fc else c) for fx, fy, fc in flips]


HBM = pl.BlockSpec(memory_space=pltpu.HBM)
SEM = pl.BlockSpec(memory_space=pltpu.SEMAPHORE)


def _exchange_windows(scatter, src_ref, land_ref, my_block, peer_block):
    if scatter:
        r = land_ref.shape[1]
        return src_ref.at[pl.ds(peer_block * r, r), :], land_ref.at[my_block], land_ref.at[peer_block]
    r = src_ref.shape[0]
    return src_ref, land_ref.at[pl.ds(my_block * r, r), :], land_ref.at[pl.ds(peer_block * r, r), :]


def _own_copy(scatter, src_ref, land_ref, my_block, sem):
    if scatter:
        r = land_ref.shape[1]
        return pltpu.make_async_copy(src_ref.at[pl.ds(my_block * r, r), :], land_ref.at[my_block], sem)
    r = src_ref.shape[0]
    return pltpu.make_async_copy(src_ref, land_ref.at[pl.ds(my_block * r, r), :], sem)


def exchange_start(srcs, lands, after, scatter, name):
    np_ = len(srcs)

    def body(*refs):
        src_refs, land_refs = refs[:np_], refs[np_:2 * np_]
        send_sems, recv_sems, own_sems = refs[2 * np_ + 1:2 * np_ + 4]
        token = refs[-1]
        my_block = _block_index(*_my_place())
        for p in range(np_):
            _own_copy(scatter, src_refs[p], land_refs[p], my_block, own_sems.at[p]).start()
            for k, peer in enumerate(_peers()):
                src, dst, _ = _exchange_windows(scatter, src_refs[p], land_refs[p], my_block, _block_index(*peer))
                pltpu.make_async_remote_copy(src_ref=src, dst_ref=dst, send_sem=send_sems.at[7 * p + k],
                                             recv_sem=recv_sems.at[7 * p + k], device_id=peer, device_id_type=MESH).start()
        token[...] = jnp.zeros_like(token)

    hbm = lambda a: pltpu.with_memory_space_constraint(a, pltpu.HBM)
    outs = pl.pallas_call(
        functools.partial(body), name=name,
        in_specs=[HBM] * (2 * np_) + [ANY],
        out_specs=[SEM, SEM, SEM] + [HBM] * (2 * np_) + [pl.BlockSpec(memory_space=pltpu.VMEM)],
        out_shape=[pltpu.SemaphoreType.DMA((7 * np_,)), pltpu.SemaphoreType.DMA((7 * np_,)), pltpu.SemaphoreType.DMA((np_,))]
        + [pltpu.HBM(a.shape, a.dtype) for a in list(srcs) + list(lands)] + [jax.ShapeDtypeStruct((8, LANES), F32)],
        input_output_aliases={i: 3 + i for i in range(2 * np_)},
        compiler_params=pltpu.CompilerParams(has_side_effects=pltpu.SideEffectType.DATAFLOW_SIDE_EFFECTING),
    )(*[hbm(a) for a in srcs], *[hbm(a) for a in lands], after)
    return dict(sems=outs[:3], srcs=outs[3:3 + np_], lands=outs[3 + np_:3 + 2 * np_], token=outs[-1], scatter=scatter)


def exchange_wait(started, after, name):
    srcs, lands = started["srcs"], started["lands"]
    scatter = started["scatter"]
    np_ = len(srcs)

    def body(*refs):
        src_refs, land_refs = refs[:np_], refs[np_:2 * np_]
        send_sems, recv_sems, own_sems = refs[2 * np_:2 * np_ + 3]
        my_block = _block_index(*_my_place())
        for p in range(np_):
            _own_copy(scatter, src_refs[p], land_refs[p], my_block, own_sems.at[p]).wait()
            for k, peer in enumerate(_peers()):
                src, dst, arrival = _exchange_windows(scatter, src_refs[p], land_refs[p], my_block, _block_index(*peer))
                pltpu.make_async_remote_copy(src_ref=src, dst_ref=dst, send_sem=send_sems.at[7 * p + k],
                                             recv_sem=recv_sems.at[7 * p + k], device_id=peer, device_id_type=MESH).wait_send()
                pltpu.make_async_remote_copy(src_ref=src, dst_ref=arrival, send_sem=send_sems.at[7 * p + k],
                                             recv_sem=recv_sems.at[7 * p + k], device_id=peer, device_id_type=MESH).wait_recv()

    outs = pl.pallas_call(
        functools.partial(body), name=name,
        in_specs=[HBM] * (2 * np_) + [SEM, SEM, SEM, ANY],
        out_specs=[HBM] * (2 * np_),
        out_shape=[pltpu.HBM(a.shape, a.dtype) for a in list(srcs) + list(lands)],
        input_output_aliases={i: i for i in range(2 * np_)},
        compiler_params=pltpu.CompilerParams(has_side_effects=pltpu.SideEffectType.DATAFLOW_SIDE_EFFECTING),
    )(*srcs, *lands, *started["sems"], after)
    return list(outs[np_:])


def _gather_zone(shard):
    return lax.empty((N_DEV * shard.shape[0], shard.shape[1]), shard.dtype)


def _scatter_zone(full):
    return lax.empty((N_DEV, full.shape[0] // N_DEV, full.shape[1]), full.dtype)


def allreduce_small(pack):
    r, c = pack.shape

    def body(pack_ref, out_ref, gathered, send_sems, recv_sems):
        me = _my_place()
        my_block = _block_index(*me)
        peers = _peers()

        def copy(k, slot, to):
            return pltpu.make_async_remote_copy(
                src_ref=pack_ref, dst_ref=gathered.at[slot], send_sem=send_sems.at[k], recv_sem=recv_sems.at[k],
                device_id=to, device_id_type=MESH)

        sends = [copy(k, my_block, peer) for k, peer in enumerate(peers)]
        for cp in sends:
            cp.start()
        gathered[my_block] = pack_ref[...]
        for k, peer in enumerate(peers):
            copy(k, _block_index(*peer), peer).wait_recv()
        for cp in sends:
            cp.wait_send()
        total = gathered[0]
        for j in range(1, N_DEV):
            total = total + gathered[j]
        out_ref[...] = total

    return pl.pallas_call(
        functools.partial(body), name="allreduce_small",
        in_specs=[pl.BlockSpec(memory_space=pltpu.VMEM)], out_specs=pl.BlockSpec(memory_space=pltpu.VMEM),
        out_shape=jax.ShapeDtypeStruct((r, c), F32),
        scratch_shapes=[pltpu.VMEM((N_DEV, r, c), F32), pltpu.SemaphoreType.DMA((7,)), pltpu.SemaphoreType.DMA((7,))],
    )(pack)


def sum_parts(parts):
    n, r, c = parts.shape
    br = 256 if r % 256 == 0 else r

    def body(p_ref, g_ref):
        g = p_ref[0].astype(F32)
        for j in range(1, n):
            g = g + p_ref[j].astype(F32)
        g_ref[...] = g

    return pl.pallas_call(
        functools.partial(body), name="sum_parts", grid=(r // br,),
        in_specs=[pl.BlockSpec((n, br, c), lambda i: (0, i, 0))], out_specs=_rows(br, c),
        out_shape=jax.ShapeDtypeStruct((r, c), F32),
        compiler_params=_cparams(("parallel",)),
    )(parts)


def adamw(w, m, v, parts):
    r, c = w.shape
    n = parts.shape[0]
    br = 256 if r % 256 == 0 else r

    def body(w_ref, m_ref, v_ref, p_ref, g_ref, d_ref, nm_ref, nv_ref):
        g = p_ref[0].astype(F32)
        for j in range(1, n):
            g = g + p_ref[j].astype(F32)
        nm = ADAM_B1 * m_ref[...] + (1.0 - ADAM_B1) * g
        nv = ADAM_B2 * v_ref[...] + (1.0 - ADAM_B2) * (g * g)
        m_hat = nm / (1.0 - ADAM_B1 ** ADAM_STEP)
        v_hat = nv / (1.0 - ADAM_B2 ** ADAM_STEP)
        g_ref[...] = g
        d_ref[...] = -ADAM_LR * (m_hat / (jnp.sqrt(v_hat) + ADAM_EPS) + ADAM_WD * w_ref[...])
        nm_ref[...] = nm
        nv_ref[...] = nv

    return pl.pallas_call(
        functools.partial(body), name="adamw", grid=(r // br,),
        in_specs=[_rows(br, c)] * 3 + [pl.BlockSpec((n, br, c), lambda i: (0, i, 0))],
        out_specs=[_rows(br, c)] * 4, out_shape=[jax.ShapeDtypeStruct((r, c), F32)] * 4,
        compiler_params=_cparams(("parallel",)),
    )(w, m, v, parts)


def _adamw_nd(w, m, v, g):
    shp = w.shape
    c = shp[-1]
    flat = lambda a: a.reshape(-1, c)
    outs = adamw(flat(w), flat(m), flat(v), flat(g)[None])
    return [o.reshape(shp) for o in outs]


def _pair_heads(a, axis, width=HEAD_DIM):
    shp = a.shape
    a = a.reshape(shp[:axis] + (2, 2, GQA, width) + shp[axis + 1:])
    return jnp.swapaxes(a, axis + 1, axis + 2).reshape(shp)


def _unpair_heads(a, axis, width=HEAD_DIM):
    shp = a.shape
    a = a.reshape(shp[:axis] + (2, GQA, 2, width) + shp[axis + 1:])
    return jnp.swapaxes(a, axis + 1, axis + 2).reshape(shp)


def _pad_rows(a, rows=8):
    return jnp.pad(a, ((0, rows - a.shape[0]), (0, 0)))


def kernel(x, p, mix_pre_g, mix_post_g, ffn_pre_g, ffn_post_g, pool_w, pool_scale, kv_norm_g, w_k, w_v, w_q, w_o, sinks, w_ff_gate, w_ff_up, w_ff_down, ple_norm_g, w_ple_gate, w_ple_proj, loss_target, m_mix_pre_g, m_mix_post_g, m_ffn_pre_g, m_ffn_post_g, m_pool_w, m_pool_scale, m_kv_norm_g, m_w_k, m_w_v, m_w_q, m_w_o, m_sinks, m_w_ff_gate, m_w_ff_up, m_w_ff_down, m_ple_norm_g, m_w_ple_gate, m_w_ple_proj, v_mix_pre_g, v_mix_post_g, v_ffn_pre_g, v_ffn_post_g, v_pool_w, v_pool_scale, v_kv_norm_g, v_w_k, v_w_v, v_w_q, v_w_o, v_sinks, v_w_ff_gate, v_w_ff_up, v_w_ff_down, v_ple_norm_g, v_w_ple_gate, v_w_ple_proj):
    depth = w_ff_gate.shape[0]
    n_a = pool_w.shape[0]
    t, d = x.shape[1], x.shape[2]
    h = x[0]
    tgt = loss_target[0]
    p_all = p.reshape(depth * t, p.shape[-1])
    my_block = _block_index(*_my_place())
    row = lambda g, i: g[i][None, :]
    bf = lambda a: a.astype(BF16)

    full, gathers = [None] * depth, {}
    start_tokens = jnp.zeros((), F32)
    for i in range(depth):
        shards = [bf(w_ff_gate[i].T), bf(w_ff_up[i].T), bf(w_ff_down[i]), bf(w_ple_gate[i]), bf(w_ple_proj[i].T)]
        if i < n_a:
            shards.append(bf(pool_w[i].reshape(-1, POOL_GROUP)))
        else:
            shards += [bf(_pair_heads(w_q[i - n_a], 1)), bf(w_o[i - n_a])]
            if i == n_a:
                shards.append(bf(jnp.concatenate([w_k, w_v], axis=1)))
        if i == 0:
            full[0] = allgather_pieces(shards, "allgather_l0")
        else:
            gathers[i] = exchange_start(shards, [_gather_zone(s) for s in shards], full[0][0], False,
                                        f"allgather_start_l{i}")
            start_tokens = start_tokens + gathers[i]["token"][0, 0]
    scale_full = allgather_pieces([_pad_rows(pool_scale)], "allgather_scale")[0]
    scale_full = scale_full.reshape(N_DEV, 8, -1)[:, :n_a].transpose(1, 0, 2).reshape(n_a, 1, d)

    cos, sin = _rope_tables(t)
    sink_b = [jnp.broadcast_to(_pair_heads(sinks[j][:, None], 0, 1), (N_HEADS, LANES)) for j in range(depth - n_a)]
    pool_full, wo_full = {}, {}

    saved = []
    kv = hk = None
    for i in range(depth):
        if i > 0:
            full[i] = exchange_wait(gathers[i], h, f"allgather_wait_l{i}")
        wg_t, wu_t, wd, wpg, wpp_t = full[i][:5]
        s = {"h0": h}
        if i < n_a:
            pool_full[i] = (full[i][5].reshape(N_DEV, len(POOL_WINDOWS), -1, POOL_GROUP).transpose(1, 0, 2, 3)
                            .reshape(len(POOL_WINDOWS), POOL_GROUP, POOL_GROUP))
            gpre = row(mix_pre_g, i) + start_tokens if i == 0 else row(mix_pre_g, i)
            h1, a = pool_mix_fwd(h, gpre, pool_full[i], scale_full[i], row(mix_post_g, i), row(ffn_pre_g, i))
        else:
            j = i - n_a
            wo_full[i] = _pair_heads(full[i][6], 0)
            if i == n_a:
                hk, kv = proj_rope_fwd(h, kv_norm_g[None, :], full[i][7], cos, sin, N_KV_HEADS * HEAD_DIM, "kv_proj_fwd")
            hn, q = proj_rope_fwd(h, row(mix_pre_g, i), full[i][5], cos, sin, d, "q_proj_fwd")
            attn = swa_fwd(q, kv, sink_b[j])
            m, h1, a = oproj_post_fwd(attn, wo_full[i], h, row(mix_post_g, i), row(ffn_pre_g, i))
            s.update(hn=hn, q=q, attn=attn, m=m)
        f, gte, up, hdn = ffn_fwd(a, wg_t, wu_t, wd)
        s.update(h1=h1, a=a, f=f, gte=gte, up=up, hdn=hdn)
        if i < depth - 1:
            h = post_ple_fwd(h1, f, p_all, i, row(ffn_post_g, i), row(ple_norm_g, i), wpg, wpp_t)[0]
        else:
            dh, loss_rows = post_ple_fwd(h1, f, p_all, i, row(ffn_post_g, i), row(ple_norm_g, i), wpg, wpp_t, target=tgt)
        saved.append(s)

    zero_row = jnp.zeros((1, d), F32)
    g_mix_pre, g_mix_post, g_ffn_pre, g_ffn_post, g_ple = ([None] * depth for _ in range(5))
    g_kv = g_sinks = None
    g_scale = [None] * n_a
    landing, scatters = [None] * depth, {}
    dkv_sum = []
    scatter_token = jnp.zeros((), F32)
    for i in reversed(range(depth)):
        s = saved[i]
        wg_t, wu_t, wd, wpg, wpp_t = full[i][:5]
        dh2, df, ub, dzb, dppb, gacc = post_ple_bwd(dh, s["h1"], s["f"], p_all, i, row(ffn_post_g, i) + scatter_token,
                                                    row(ple_norm_g, i), wpg, wpp_t)
        g_ple[i], g_ffn_post[i] = gacc[0], gacc[1]
        da, dgte, dup = ffn_bwd_act(df, s["gte"], s["up"], wg_t, wu_t, wd)
        grads = [xty(dgte, s["a"]), xty(dup, s["a"]), xty(s["hdn"], df), xty(ub, dzb), xty(dppb, p_all, i)]
        if i < n_a:
            dh, dpw, gacc = pool_mix_bwd(s["h0"], dh2, da, row(mix_pre_g, i), pool_full[i], scale_full[i],
                                         row(mix_post_g, i), row(ffn_pre_g, i))
            g_mix_pre[i], g_mix_post[i], g_ffn_pre[i], g_scale[i] = gacc[0], gacc[1], gacc[2], gacc[3]
            dpw = dpw.reshape(len(POOL_WINDOWS), N_DEV, -1, POOL_GROUP).transpose(1, 0, 2, 3)
            grads.append(bf(dpw.reshape(-1, POOL_GROUP)))
        else:
            j = i - n_a
            dh1, dmb, dattn, gacc = oproj_post_bwd(dh2, da, s["h1"], s["m"], wo_full[i], row(mix_post_g, i),
                                                   row(ffn_pre_g, i))
            g_mix_post[i], g_ffn_pre[i] = gacc[0], gacc[1]
            dq, dkv, dsink = swa_bwd(s["q"], kv, dattn, sink_b[j])
            dkv_sum.append(dkv)
            g_sinks = [_unpair_heads(dsink[:, 0:1], 0, 1)[:, 0]] + (g_sinks or [])
            branches = [(row(mix_pre_g, i), full[i][5], d, [dq])]
            if i == n_a:
                branches.append((kv_norm_g[None, :], full[i][7], N_KV_HEADS * HEAD_DIM, dkv_sum))
            outs = proj_rope_bwd(dh1, s["h0"], cos, sin, branches, f"proj_bwd_l{i}")
            dh, gacc = outs[0], outs[-1]
            g_mix_pre[i] = gacc[0]
            grads += [xty(s["hn"], outs[1]), _unpair_heads(xty(s["attn"], dmb), 0)]
            if i == n_a:
                g_kv = gacc[1]
                grads.append(xty(hk, outs[2]))
        scatters[i] = exchange_start(grads, [_scatter_zone(g) for g in grads], dh, True,
                                     f"reduce_scatter_start_l{i}")
        scatter_token = scatters[i]["token"][0, 0]
    grad_x = dh[None]
    after = dh
    for i in reversed(range(depth)):
        landing[i] = exchange_wait(scatters[i], after, f"reduce_scatter_wait_l{i}")
        after = landing[i][0]

    loss_row = jnp.sum(loss_rows, axis=0, keepdims=True)
    sink_row = jnp.pad(jnp.concatenate(g_sinks)[None, :], ((0, 0), (0, d - sinks.size)))
    stack = lambda rows_: _pad_rows(jnp.stack(rows_))
    pack = jnp.concatenate([stack(g_mix_pre), stack(g_mix_post), stack(g_ffn_pre), stack(g_ffn_post), stack(g_ple),
                            _pad_rows(g_kv[None]), stack(g_scale), _pad_rows(sink_row), _pad_rows(loss_row)], axis=0)
    tot = allreduce_small(pack)
    sec = lambda k, n: tot[8 * k:8 * k + n]
    loss = jnp.sum(tot[64])
    small = {
        "mix_pre_g": sec(0, depth), "mix_post_g": sec(1, depth), "ffn_pre_g": sec(2, depth),
        "ffn_post_g": sec(3, depth), "ple_norm_g": sec(4, depth), "kv_norm_g": tot[40],
        "pool_scale": lax.dynamic_slice_in_dim(sec(6, n_a), my_block * pool_scale.shape[1], pool_scale.shape[1], axis=1),
        "sinks": tot[56, :sinks.size].reshape(sinks.shape),
    }

    weights = dict(mix_pre_g=mix_pre_g, mix_post_g=mix_post_g, ffn_pre_g=ffn_pre_g, ffn_post_g=ffn_post_g, pool_w=pool_w, pool_scale=pool_scale, kv_norm_g=kv_norm_g, w_k=w_k, w_v=w_v, w_q=w_q, w_o=w_o, sinks=sinks, w_ff_gate=w_ff_gate, w_ff_up=w_ff_up, w_ff_down=w_ff_down, ple_norm_g=ple_norm_g, w_ple_gate=w_ple_gate, w_ple_proj=w_ple_proj)
    mom1 = dict(mix_pre_g=m_mix_pre_g, mix_post_g=m_mix_post_g, ffn_pre_g=m_ffn_pre_g, ffn_post_g=m_ffn_post_g, pool_w=m_pool_w, pool_scale=m_pool_scale, kv_norm_g=m_kv_norm_g, w_k=m_w_k, w_v=m_w_v, w_q=m_w_q, w_o=m_w_o, sinks=m_sinks, w_ff_gate=m_w_ff_gate, w_ff_up=m_w_ff_up, w_ff_down=m_w_ff_down, ple_norm_g=m_ple_norm_g, w_ple_gate=m_w_ple_gate, w_ple_proj=m_w_ple_proj)
    mom2 = dict(mix_pre_g=v_mix_pre_g, mix_post_g=v_mix_post_g, ffn_pre_g=v_ffn_pre_g, ffn_post_g=v_ffn_post_g, pool_w=v_pool_w, pool_scale=v_pool_scale, kv_norm_g=v_kv_norm_g, w_k=v_w_k, w_v=v_w_v, w_q=v_w_q, w_o=v_w_o, sinks=v_sinks, w_ff_gate=v_w_ff_gate, w_ff_up=v_w_ff_up, w_ff_down=v_w_ff_down, ple_norm_g=v_ple_norm_g, w_ple_gate=v_w_ple_gate, w_ple_proj=v_w_ple_proj)

    def land(i, k):
        return sum_parts(landing[i][k])

    gw = dict(small)
    gw["kv_norm_g"] = small["kv_norm_g"]
    gw["w_ff_gate"] = jnp.stack([land(i, 0).T for i in range(depth)])
    gw["w_ff_up"] = jnp.stack([land(i, 1).T for i in range(depth)])
    gw["w_ff_down"] = jnp.stack([land(i, 2) for i in range(depth)])
    gw["w_ple_gate"] = jnp.stack([land(i, 3) for i in range(depth)])
    gw["w_ple_proj"] = jnp.stack([land(i, 4).T for i in range(depth)])
    gw["pool_w"] = jnp.stack([land(i, 5).reshape(pool_w.shape[1:]) for i in range(n_a)])
    gw["w_q"] = jnp.stack([_unpair_heads(land(i, 5), 1) for i in range(n_a, depth)])
    gw["w_o"] = jnp.stack([land(i, 6) for i in range(n_a, depth)])
    gkv = land(n_a, 7)
    gw["w_k"], gw["w_v"] = gkv[:, :w_k.shape[1]], gkv[:, w_k.shape[1]:]

    order = ["mix_pre_g", "mix_post_g", "ffn_pre_g", "ffn_post_g", "pool_w", "pool_scale", "kv_norm_g", "w_k", "w_v",
             "w_q", "w_o", "sinks", "w_ff_gate", "w_ff_up", "w_ff_down", "ple_norm_g", "w_ple_gate", "w_ple_proj"]
    g_out, d_out, m_out, v_out = [], [], [], []
    for nme in order:
        w = weights[nme]
        as2d = (lambda a: a[None, :]) if w.ndim == 1 else (lambda a: a)
        g, dl, nm, nv = _adamw_nd(as2d(w), as2d(mom1[nme]), as2d(mom2[nme]), as2d(gw[nme]))
        for lst, val in ((g_out, g), (d_out, dl), (m_out, nm), (v_out, nv)):
            lst.append(val.reshape(w.shape))
    return (loss, grad_x, *g_out, *d_out, *m_out, *v_out)
```

```python
import functools

import jax
import jax.numpy as jnp
from jax import lax
from jax.experimental import pallas as pl
from jax.experimental.pallas import tpu as pltpu

F32 = jnp.float32
BF16 = jnp.bfloat16

N_DEV = 8
HEAD_DIM = 64
N_HEADS = 16
N_KV_HEADS = 4
GQA = N_HEADS // N_KV_HEADS
BLOCK = 128
POOL_WINDOWS = (2, 4, 8, 16)
POOL_GROUP = 256
HALO = 16
ROPE_THETA = 10000.0
RMS_EPS = 1e-6
NEG_INF = -1e30
LANES = 128
XTY_ROWS = 2048
FFN_CHUNK = 768
VMEM_LIMIT = 56 * 1024 * 1024

ADAM_LR = 0.001
ADAM_B1 = 0.9
ADAM_B2 = 0.999
ADAM_EPS = 1e-08
ADAM_WD = 0.01
ADAM_STEP = 10

MESH = pl.DeviceIdType.MESH
ANY = pl.BlockSpec(memory_space=pl.ANY)

NT_DIMS = (((1,), (1,)), ((), ()))
TN_DIMS = (((0,), (0,)), ((), ()))


def _cparams(sem=None, vmem=None):
    kw = {}
    if sem is not None:
        kw["dimension_semantics"] = sem
    if vmem is not None:
        kw["vmem_limit_bytes"] = vmem
    return pltpu.CompilerParams(**kw)


def _rows(tm, n, first=0):
    return pl.BlockSpec((tm, n), lambda i: (i + first, 0))


def _rows_rev(tm, n, nt):
    return pl.BlockSpec((tm, n), lambda i: (nt - 1 - i, 0))


def _const(shape):
    nd = len(shape)
    return pl.BlockSpec(shape, lambda *_: (0,) * nd, pipeline_mode=pl.Buffered(1))


def _resident(shape):
    nd = len(shape)
    return pl.BlockSpec(shape, lambda *_: (0,) * nd)


def _tile_rows(t):
    return 512 if t % 512 == 0 else 128


def _dot(a, b):
    return jnp.dot(a, b, preferred_element_type=F32)


def _dot_nt(a, b):
    return lax.dot_general(a, b, NT_DIMS, preferred_element_type=F32)


def _dot_tn(a, b):
    return lax.dot_general(a, b, TN_DIMS, preferred_element_type=F32)


def _rms_r(x):
    return lax.rsqrt(jnp.mean(x * x, axis=-1, keepdims=True) + RMS_EPS)


def _rms_bwd(x, r, g, dy):
    gy = dy * g
    dx = r * gy - x * (r * r * r * jnp.mean(gy * x, axis=-1, keepdims=True))
    dg = jnp.sum(dy * (x * r), axis=0, keepdims=True)
    return dx, dg


def _sigmoid(x):
    return jax.nn.sigmoid(x)


def _rope_tables(t):
    inv = 1.0 / (ROPE_THETA ** (jnp.arange(0, HEAD_DIM, 2, dtype=F32) / HEAD_DIM))
    ang = jnp.arange(t, dtype=F32)[:, None] * inv[None, :]
    c, s = jnp.cos(ang), jnp.sin(ang)
    cos = jnp.concatenate([c, c, c, c], axis=1)
    sin = jnp.concatenate([-s, s, -s, s], axis=1)
    return cos, sin


def _swap_halves(x):
    n = x.shape[1]
    lane = lax.broadcasted_iota(jnp.int32, x.shape, 1)
    first = (lane % HEAD_DIM) < (HEAD_DIM // 2)
    return jnp.where(first, pltpu.roll(x, n - HEAD_DIM // 2, 1), pltpu.roll(x, HEAD_DIM // 2, 1))


def _rope(x, cos, sin):
    reps = x.shape[1] // LANES
    return x * jnp.tile(cos, (1, reps)) + _swap_halves(x) * jnp.tile(sin, (1, reps))


def _unrope(dy, cos, sin):
    reps = dy.shape[1] // LANES
    return dy * jnp.tile(cos, (1, reps)) + _swap_halves(dy * jnp.tile(sin, (1, reps)))


def _acc_init(acc_ref):
    @pl.when(pl.program_id(0) == 0)
    def _():
        acc_ref[...] = jnp.zeros_like(acc_ref)


def _window_sums(ext, tm, forward):
    n = tm + HALO
    out = []
    for g, w in enumerate(POOL_WINDOWS):
        s = ext[:, g * POOL_GROUP:(g + 1) * POOL_GROUP]
        k = 1
        while k < w:
            s = s + pltpu.roll(s, k if forward else n - k, 0)
            k *= 2
        out.append(s[HALO:, :] if forward else s[:tm, :])
    return out


def _pool_counts(tile, tm):
    t = tile * tm + lax.broadcasted_iota(jnp.int32, (tm, 1), 0)
    return [jnp.minimum(t + 1, w).astype(F32) for w in POOL_WINDOWS]


def _pool_mix(hn, ext, cnts, pw_ref, scale, tm):
    sums = _window_sums(ext, tm, True)
    pooled, ys = [], []
    for g in range(len(POOL_WINDOWS)):
        pg = (sums[g] / cnts[g] - hn[:, g * POOL_GROUP:(g + 1) * POOL_GROUP]).astype(BF16)
        pooled.append(pg)
        ys.append(_dot(pg, pw_ref[g]))
    y = jnp.concatenate(ys, axis=1)
    return pooled, y, y * scale


def pool_mix_fwd(h0, gpre, pool_w, scale, gpost, gffn):
    t, d = h0.shape
    tm = _tile_rows(t)

    def body(h_ref, gpre_ref, pw_ref, scale_ref, gpost_ref, gffn_ref, h1_ref, a_ref, carry):
        i = pl.program_id(0)

        @pl.when(i == 0)
        def _():
            carry[...] = jnp.zeros_like(carry)

        x = h_ref[...]
        hn = x * _rms_r(x) * gpre_ref[...]
        ext = jnp.concatenate([carry[...], hn], axis=0)
        carry[...] = hn[tm - HALO:, :]
        _, _, m = _pool_mix(hn, ext, _pool_counts(i, tm), pw_ref, scale_ref[...], tm)
        h1 = x + m * _rms_r(m) * gpost_ref[...]
        h1_ref[...] = h1
        a_ref[...] = (h1 * _rms_r(h1) * gffn_ref[...]).astype(BF16)

    return pl.pallas_call(
        functools.partial(body), name="pool_mix_fwd", grid=(t // tm,),
        in_specs=[_rows(tm, d), _const((1, d)), _const(pool_w.shape), _const((1, d)), _const((1, d)), _const((1, d))],
        out_specs=[_rows(tm, d), _rows(tm, d)],
        out_shape=[jax.ShapeDtypeStruct((t, d), F32), jax.ShapeDtypeStruct((t, d), BF16)],
        scratch_shapes=[pltpu.VMEM((HALO, d), F32)],
        compiler_params=_cparams(("arbitrary",), VMEM_LIMIT),
    )(h0, gpre, pool_w, scale, gpost, gffn)


def pool_mix_bwd(h0, dh2, da, gpre, pool_w, scale, gpost, gffn):
    t, d = h0.shape
    tm = _tile_rows(t)
    nt = t // tm
    hb = tm // HALO

    def body(h_ref, halo_ref, dh2_ref, da_ref, gpre_ref, pw_ref, scale_ref, gpost_ref, gffn_ref,
             dh0_ref, dpw_ref, gacc_ref, carry):
        i = pl.program_id(0)
        tile = nt - 1 - i
        _acc_init(gacc_ref)
        _acc_init(dpw_ref)

        @pl.when(i == 0)
        def _():
            carry[...] = jnp.zeros_like(carry)

        x = h_ref[...]
        gpre_v, scale_v, gpost_v, gffn_v = gpre_ref[...], scale_ref[...], gpost_ref[...], gffn_ref[...]
        r0 = _rms_r(x)
        hn = x * r0 * gpre_v
        xh = halo_ref[...]
        hn_halo = jnp.where(tile > 0, xh * _rms_r(xh) * gpre_v, 0.0)
        ext = jnp.concatenate([hn_halo, hn], axis=0)
        cnts = _pool_counts(tile, tm)
        pooled, y, m = _pool_mix(hn, ext, cnts, pw_ref, scale_v, tm)
        rm = _rms_r(m)
        h1 = x + m * rm * gpost_v
        dh1_n, dgffn = _rms_bwd(h1, _rms_r(h1), gffn_v, da_ref[...])
        dh1 = dh2_ref[...] + dh1_n
        dm, dgpost = _rms_bwd(m, rm, gpost_v, dh1)
        dscale = jnp.sum(dm * y, axis=0, keepdims=True)
        dy = (dm * scale_v).astype(BF16)
        dpn = []
        for g in range(len(POOL_WINDOWS)):
            dyg = dy[:, g * POOL_GROUP:(g + 1) * POOL_GROUP]
            dpw_ref[g] += _dot_tn(pooled[g], dyg)
            dpn.append(_dot_nt(dyg, pw_ref[g]))
        dpooled = jnp.concatenate(dpn, axis=1)
        dpc = jnp.concatenate([dpn[g] / cnts[g] for g in range(len(POOL_WINDOWS))], axis=1)
        ext2 = jnp.concatenate([dpc, carry[...]], axis=0)
        carry[...] = dpc[:HALO, :]
        dhn = jnp.concatenate(_window_sums(ext2, tm, False), axis=1) - dpooled
        dh0_n, dgpre = _rms_bwd(x, r0, gpre_v, dhn)
        dh0_ref[...] = dh1 + dh0_n
        gacc_ref[0:1, :] += dgpre
        gacc_ref[1:2, :] += dgpost
        gacc_ref[2:3, :] += dgffn
        gacc_ref[3:4, :] += dscale

    return pl.pallas_call(
        functools.partial(body), name="pool_mix_bwd", grid=(nt,),
        in_specs=[_rows_rev(tm, d, nt),
                  pl.BlockSpec((HALO, d), lambda i: (jnp.maximum((nt - 1 - i) * hb - 1, 0), 0)),
                  _rows_rev(tm, d, nt), _rows_rev(tm, d, nt),
                  _const((1, d)), _const(pool_w.shape), _const((1, d)), _const((1, d)), _const((1, d))],
        out_specs=[_rows_rev(tm, d, nt), _resident(pool_w.shape), _resident((8, d))],
        out_shape=[jax.ShapeDtypeStruct((t, d), F32), jax.ShapeDtypeStruct(pool_w.shape, F32),
                   jax.ShapeDtypeStruct((8, d), F32)],
        scratch_shapes=[pltpu.VMEM((HALO, d), F32)],
        compiler_params=_cparams(("arbitrary",), VMEM_LIMIT),
    )(h0, h0, dh2, da, gpre, pool_w, scale, gpost, gffn)


def _ffn_chunks(f):
    return [(c, min(c + FFN_CHUNK, f)) for c in range(0, f, FFN_CHUNK)]


def ffn_fwd(a, wg_t, wu_t, wd):
    t, d = a.shape
    f = wd.shape[0]
    tm = _tile_rows(t)

    def body(a_ref, wg_ref, wu_ref, wd_ref, f_ref, gte_ref, up_ref, hdn_ref):
        av = a_ref[...]
        acc = jnp.zeros((tm, d), F32)
        for c0, c1 in _ffn_chunks(f):
            gte = _dot_nt(av, wg_ref[c0:c1, :])
            up = _dot_nt(av, wu_ref[c0:c1, :])
            gte_ref[:, c0:c1] = gte.astype(BF16)
            up_ref[:, c0:c1] = up.astype(BF16)
            hdn = (gte * _sigmoid(gte) * up).astype(BF16)
            hdn_ref[:, c0:c1] = hdn
            acc = acc + _dot(hdn, wd_ref[c0:c1, :])
        f_ref[...] = acc

    return pl.pallas_call(
        functools.partial(body), name="ffn_fwd", grid=(t // tm,),
        in_specs=[_rows(tm, d), _const((f, d)), _const((f, d)), _const((f, d))],
        out_specs=[_rows(tm, d), _rows(tm, f), _rows(tm, f), _rows(tm, f)],
        out_shape=[jax.ShapeDtypeStruct((t, d), F32)] + [jax.ShapeDtypeStruct((t, f), BF16)] * 3,
        compiler_params=_cparams(("parallel",), VMEM_LIMIT),
    )(a, wg_t, wu_t, wd)


def ffn_bwd_act(df, gte, up, wg_t, wu_t, wd):
    t, d = df.shape
    f = wd.shape[0]
    tm = _tile_rows(t)

    def body(df_ref, gte_ref, up_ref, wg_ref, wu_ref, wd_ref, da_ref, dgte_ref, dup_ref):
        dfv = df_ref[...]
        for c0, c1 in _ffn_chunks(f):
            g = gte_ref[:, c0:c1].astype(F32)
            u = up_ref[:, c0:c1].astype(F32)
            sg = _sigmoid(g)
            sl = g * sg
            dh = _dot_nt(dfv, wd_ref[c0:c1, :])
            dup_ref[:, c0:c1] = (dh * sl).astype(BF16)
            dgte_ref[:, c0:c1] = (dh * u * (sg * (1.0 + g * (1.0 - sg)))).astype(BF16)
        da_ref[...] = _dot(dgte_ref[...], wg_ref[...]) + _dot(dup_ref[...], wu_ref[...])

    return pl.pallas_call(
        functools.partial(body), name="ffn_bwd_act", grid=(t // tm,),
        in_specs=[_rows(tm, d), _rows(tm, f), _rows(tm, f), _const((f, d)), _const((f, d)), _const((f, d))],
        out_specs=[_rows(tm, d), _rows(tm, f), _rows(tm, f)],
        out_shape=[jax.ShapeDtypeStruct((t, d), F32)] + [jax.ShapeDtypeStruct((t, f), BF16)] * 2,
        compiler_params=_cparams(("parallel",), VMEM_LIMIT),
    )(df, gte, up, wg_t, wu_t, wd)


def xty(x, y, y_part=0):
    t, nx = x.shape
    ny = y.shape[1]
    tk = XTY_ROWS if t % XTY_ROWS == 0 else _tile_rows(t)
    bn = nx // 2 if nx > 1024 else nx
    nk = t // tk

    def body(x_ref, y_ref, o_ref, acc):
        k = pl.program_id(1)

        @pl.when(k == 0)
        def _():
            acc[...] = jnp.zeros_like(acc)

        acc[...] += _dot_tn(x_ref[...].astype(BF16), y_ref[...].astype(BF16))

        @pl.when(k == nk - 1)
        def _():
            o_ref[...] = acc[...].astype(BF16)

    return pl.pallas_call(
        functools.partial(body), name="xty", grid=(nx // bn, nk),
        in_specs=[pl.BlockSpec((tk, bn), lambda j, k: (k, j)),
                  pl.BlockSpec((tk, ny), lambda j, k: (k + y_part * nk, 0))],
        out_specs=pl.BlockSpec((bn, ny), lambda j, k: (j, 0)),
        out_shape=jax.ShapeDtypeStruct((nx, ny), BF16),
        scratch_shapes=[pltpu.VMEM((bn, ny), F32)],
        compiler_params=_cparams(("parallel", "arbitrary"), VMEM_LIMIT),
    )(x, y)


def _ple_fwd_tile(h1, f, p, gpost, gple, wpg_ref, wpp_ref):
    rf = _rms_r(f)
    h2 = h1 + f * rf * gpost
    r2 = _rms_r(h2)
    ub = (h2 * r2 * gple).astype(BF16)
    gate = _sigmoid(_dot(ub, wpg_ref[...]))
    pp = _dot_nt(p.astype(BF16), wpp_ref[...])
    return rf, h2, r2, ub, gate, pp


def post_ple_fwd(h1, f, p, layer, gpost, gple, wpg, wpp_t, target=None):
    t, d = h1.shape
    pd = p.shape[1]
    tm = _tile_rows(t)
    with_loss = target is not None

    def body(*refs):
        if with_loss:
            h1_ref, f_ref, p_ref, gpost_ref, gple_ref, wpg_ref, wpp_ref, tgt_ref, out_ref, loss_ref = refs
        else:
            h1_ref, f_ref, p_ref, gpost_ref, gple_ref, wpg_ref, wpp_ref, out_ref = refs
        _, h2, _, _, gate, pp = _ple_fwd_tile(h1_ref[...], f_ref[...], p_ref[...], gpost_ref[...], gple_ref[...],
                                              wpg_ref, wpp_ref)
        h3 = h2 + pp * gate
        if with_loss:
            err = h3 - tgt_ref[...]
            out_ref[...] = err * (1.0 / d)
            colsum = jnp.sum(err * err, axis=0, keepdims=True) * (0.5 / d)
            loss_ref[...] = jnp.broadcast_to(colsum, (8, d)) * (lax.broadcasted_iota(jnp.int32, (8, d), 0) == 0)
        else:
            out_ref[...] = h3

    in_specs = [_rows(tm, d), _rows(tm, d), _rows(tm, pd, layer * (t // tm)), _const((1, d)), _const((1, d)),
                _const(wpg.shape), _const(wpp_t.shape)]
    out_specs = [_rows(tm, d)]
    out_shape = [jax.ShapeDtypeStruct((t, d), F32)]
    args = [h1, f, p, gpost, gple, wpg, wpp_t]
    if with_loss:
        in_specs.append(_rows(tm, d))
        out_specs.append(_rows(8, d))
        out_shape.append(jax.ShapeDtypeStruct((t // tm * 8, d), F32))
        args.append(target)
    return pl.pallas_call(
        functools.partial(body), name="post_ple_loss" if with_loss else "post_ple_fwd", grid=(t // tm,),
        in_specs=in_specs, out_specs=out_specs, out_shape=out_shape,
        compiler_params=_cparams(("parallel",), VMEM_LIMIT),
    )(*args)


def post_ple_bwd(dh3, h1, f, p, layer, gpost, gple, wpg, wpp_t):
    t, d = h1.shape
    pd = p.shape[1]
    tm = _tile_rows(t)

    def body(dh3_ref, h1_ref, f_ref, p_ref, gpost_ref, gple_ref, wpg_ref, wpp_ref,
             dh2_ref, df_ref, u_ref, dz_ref, dpp_ref, gacc_ref):
        _acc_init(gacc_ref)
        fv = f_ref[...]
        gpost_v, gple_v = gpost_ref[...], gple_ref[...]
        rf, h2, r2, ub, gate, pp = _ple_fwd_tile(h1_ref[...], fv, p_ref[...], gpost_v, gple_v, wpg_ref, wpp_ref)
        dh3v = dh3_ref[...]
        dpp_ref[...] = (dh3v * gate).astype(BF16)
        dz = (dh3v * pp * gate * (1.0 - gate)).astype(BF16)
        dz_ref[...] = dz
        u_ref[...] = ub
        du = _dot_nt(dz, wpg_ref[...])
        dh2_n, dgple = _rms_bwd(h2, r2, gple_v, du)
        dh2 = dh3v + dh2_n
        df, dgpost = _rms_bwd(fv, rf, gpost_v, dh2)
        dh2_ref[...] = dh2
        df_ref[...] = df.astype(BF16)
        gacc_ref[0:1, :] += dgple
        gacc_ref[1:2, :] += dgpost

    return pl.pallas_call(
        functools.partial(body), name="post_ple_bwd", grid=(t // tm,),
        in_specs=[_rows(tm, d), _rows(tm, d), _rows(tm, d), _rows(tm, pd, layer * (t // tm)), _const((1, d)),
                  _const((1, d)), _const(wpg.shape), _const(wpp_t.shape)],
        out_specs=[_rows(tm, d)] * 5 + [_resident((8, d))],
        out_shape=[jax.ShapeDtypeStruct((t, d), F32)] + [jax.ShapeDtypeStruct((t, d), BF16)] * 4
        + [jax.ShapeDtypeStruct((8, d), F32)],
        compiler_params=_cparams(("arbitrary",), VMEM_LIMIT),
    )(dh3, h1, f, p, gpost, gple, wpg, wpp_t)


def proj_rope_fwd(h, gain, w, cos, sin, n_rope, name):
    t, d = h.shape
    n = w.shape[1]
    tm = _tile_rows(t)

    def body(h_ref, g_ref, w_ref, cos_ref, sin_ref, hn_ref, y_ref):
        x = h_ref[...]
        hn = (x * _rms_r(x) * g_ref[...]).astype(BF16)
        hn_ref[...] = hn
        y = _dot(hn, w_ref[...])
        y_ref[:, :n_rope] = _rope(y[:, :n_rope], cos_ref[...], sin_ref[...]).astype(BF16)
        if n_rope < n:
            y_ref[:, n_rope:] = y[:, n_rope:].astype(BF16)

    return pl.pallas_call(
        functools.partial(body), name=name, grid=(t // tm,),
        in_specs=[_rows(tm, d), _const((1, d)), _const(w.shape), _rows(tm, LANES), _rows(tm, LANES)],
        out_specs=[_rows(tm, d), _rows(tm, n)],
        out_shape=[jax.ShapeDtypeStruct((t, d), BF16), jax.ShapeDtypeStruct((t, n), BF16)],
        compiler_params=_cparams(("parallel",), VMEM_LIMIT),
    )(h, gain, w, cos, sin)


def proj_rope_bwd(dh1, h0, cos, sin, branches, name):
    t, d = h0.shape
    tm = _tile_rows(t)
    nb = len(branches)
    n_cot = [len(b[3]) for b in branches]

    def body(*refs):
        dh1_ref, h0_ref, cos_ref, sin_ref = refs[:4]
        pos = 4
        br_refs = []
        for b in range(nb):
            br_refs.append((refs[pos], refs[pos + 1], refs[pos + 2:pos + 2 + n_cot[b]]))
            pos += 2 + n_cot[b]
        dh0_ref = refs[pos]
        dpre_refs = refs[pos + 1:pos + 1 + nb]
        gacc_ref = refs[pos + 1 + nb]
        _acc_init(gacc_ref)
        x = h0_ref[...]
        r0 = _rms_r(x)
        dh = dh1_ref[...]
        for b in range(nb):
            g_ref, w_ref, cot_refs = br_refs[b]
            n_rope = branches[b][2]
            dy = cot_refs[0][...].astype(F32)
            for c_ref in cot_refs[1:]:
                dy = dy + c_ref[...].astype(F32)
            n = dy.shape[1]
            dpre_refs[b][:, :n_rope] = _unrope(dy[:, :n_rope], cos_ref[...], sin_ref[...]).astype(BF16)
            if n_rope < n:
                dpre_refs[b][:, n_rope:] = dy[:, n_rope:].astype(BF16)
            dhn = _dot_nt(dpre_refs[b][...], w_ref[...])
            dx, dg = _rms_bwd(x, r0, g_ref[...], dhn)
            dh = dh + dx
            gacc_ref[b:b + 1, :] += dg
        dh0_ref[...] = dh

    in_specs = [_rows(tm, d), _rows(tm, d), _rows(tm, LANES), _rows(tm, LANES)]
    args = [dh1, h0, cos, sin]
    out_specs = [_rows(tm, d)]
    out_shape = [jax.ShapeDtypeStruct((t, d), F32)]
    for gain, w, _, cots in branches:
        n = w.shape[1]
        in_specs += [_const((1, d)), _const(w.shape)] + [_rows(tm, n)] * len(cots)
        args += [gain, w] + list(cots)
        out_specs.append(_rows(tm, n))
        out_shape.append(jax.ShapeDtypeStruct((t, n), BF16))
    out_specs.append(_resident((8, d)))
    out_shape.append(jax.ShapeDtypeStruct((8, d), F32))
    return pl.pallas_call(
        functools.partial(body), name=name, grid=(t // tm,),
        in_specs=in_specs, out_specs=out_specs, out_shape=out_shape,
        compiler_params=_cparams(("arbitrary",), VMEM_LIMIT),
    )(*args)


def _tri():
    row = lax.broadcasted_iota(jnp.int32, (BLOCK, BLOCK), 0)
    col = lax.broadcasted_iota(jnp.int32, (BLOCK, BLOCK), 1)
    return col <= row


def _block_diag(x):
    lo = lax.broadcasted_iota(jnp.int32, x.shape, 1) < HEAD_DIM
    zero = jnp.zeros_like(x)
    return jnp.concatenate([jnp.where(lo, x, zero), jnp.where(lo, zero, x)], axis=0)


def _dense(x, tri):
    return (jnp.where(tri, x[:, BLOCK:2 * BLOCK], x[:, :BLOCK]),
            jnp.where(tri, x[:, 3 * BLOCK:], x[:, 2 * BLOCK:3 * BLOCK]))


def _banded(xa, xb, tri):
    zero = jnp.zeros_like(xa)
    return jnp.concatenate([jnp.where(tri, zero, xa), jnp.where(tri, xa, zero),
                            jnp.where(tri, zero, xb), jnp.where(tri, xb, zero)], axis=1).astype(BF16)


def _softmax_sink(s, sink):
    mx = jnp.maximum(jnp.max(s, axis=1, keepdims=True), sink)
    e = jnp.exp(s - mx)
    es = jnp.exp(sink - mx)
    inv = 1.0 / (jnp.sum(e, axis=1, keepdims=True) + es)
    return e * inv, es * inv


def _sink_column(sink_ref):
    return jnp.concatenate([jnp.broadcast_to(sink_ref[h:h + 1, 0:1], (BLOCK, 1)) for h in range(N_HEADS)], axis=0)


def _kv_block_diag(band, kvw):
    n_lt = kvw // LANES
    return ([_block_diag(band[:, lt * LANES:(lt + 1) * LANES]) for lt in range(n_lt)],
            [_block_diag(band[:, kvw + lt * LANES:kvw + (lt + 1) * LANES]) for lt in range(n_lt)])


def _all_probs(q_ref, kbd, tri, n, sink_ref):
    dense = []
    for tq in range(N_HEADS // 2):
        s = _dot_nt(q_ref[:, tq * LANES:(tq + 1) * LANES], kbd[tq // GQA])
        dense += list(_dense(s, tri))
    bias = jnp.where(jnp.logical_not(tri) & (n == 0), NEG_INF, 0.0)
    s_all = jnp.concatenate(dense, axis=0) * (HEAD_DIM ** -0.5) + jnp.concatenate([bias] * N_HEADS, axis=0)
    return _softmax_sink(s_all, _sink_column(sink_ref))


def _head_rows(x, tq):
    return x[2 * tq * BLOCK:(2 * tq + 1) * BLOCK], x[(2 * tq + 1) * BLOCK:(2 * tq + 2) * BLOCK]


def swa_fwd(q, kv, sink_b):
    t, d = q.shape
    nb = t // BLOCK
    kvw = N_KV_HEADS * HEAD_DIM

    def body(q_ref, kvc_ref, kvp_ref, sink_ref, o_ref):
        n = pl.program_id(0)
        tri = _tri()
        kbd, vbd = _kv_block_diag(jnp.concatenate([kvp_ref[...], kvc_ref[...]], axis=0), kvw)
        p, _ = _all_probs(q_ref, kbd, tri, n, sink_ref)
        for tq in range(N_HEADS // 2):
            pa, pb = _head_rows(p, tq)
            o_ref[:, tq * LANES:(tq + 1) * LANES] = _dot(_banded(pa, pb, tri), vbd[tq // GQA]).astype(BF16)

    return pl.pallas_call(
        functools.partial(body), name="swa_fwd", grid=(nb,),
        in_specs=[_rows(BLOCK, d), _rows(BLOCK, 2 * kvw),
                  pl.BlockSpec((BLOCK, 2 * kvw), lambda n: (jnp.maximum(n - 1, 0), 0)), _const(sink_b.shape)],
        out_specs=_rows(BLOCK, d),
        out_shape=jax.ShapeDtypeStruct((t, d), BF16),
        compiler_params=_cparams(("parallel",), VMEM_LIMIT),
    )(q, kv, kv, sink_b)


def swa_bwd(q, kv, do, sink_b):
    t, d = q.shape
    nb = t // BLOCK
    kvw = N_KV_HEADS * HEAD_DIM

    def body(q_ref, do_ref, kvc_ref, kvp_ref, sink_ref, dq_ref, dkv_ref, dsink_ref, carry):
        i = pl.program_id(0)
        n = nb - 1 - i
        _acc_init(dsink_ref)

        @pl.when(i == 0)
        def _():
            carry[...] = jnp.zeros_like(carry)

        tri = _tri()
        lo = lax.broadcasted_iota(jnp.int32, (2 * BLOCK, LANES), 1) < HEAD_DIM
        kbd, vbd = _kv_block_diag(jnp.concatenate([kvp_ref[...], kvc_ref[...]], axis=0), kvw)
        p, ps = _all_probs(q_ref, kbd, tri, n, sink_ref)
        dp = []
        for tq in range(N_HEADS // 2):
            dp += list(_dense(_dot_nt(do_ref[:, tq * LANES:(tq + 1) * LANES], vbd[tq // GQA]), tri))
        dp = jnp.concatenate(dp, axis=0)
        delta = jnp.sum(p * dp, axis=1, keepdims=True)
        ds = p * (dp - delta) * (HEAD_DIM ** -0.5)
        dsk = ps * delta
        for h in range(N_HEADS):
            dsink_ref[h:h + 1, :] -= jnp.sum(dsk[h * BLOCK:(h + 1) * BLOCK], axis=0, keepdims=True)
        dkb = [jnp.zeros((4 * BLOCK, LANES), F32) for _ in kbd]
        dvb = [jnp.zeros((4 * BLOCK, LANES), F32) for _ in kbd]
        for tq in range(N_HEADS // 2):
            lt = tq // GQA
            cols = slice(tq * LANES, (tq + 1) * LANES)
            dsb = _banded(*_head_rows(ds, tq), tri)
            dq_ref[:, cols] = _dot(dsb, kbd[lt]).astype(BF16)
            dkb[lt] = dkb[lt] + _dot_tn(dsb, q_ref[:, cols])
            dvb[lt] = dvb[lt] + _dot_tn(_banded(*_head_rows(p, tq), tri), do_ref[:, cols])
        dall = jnp.concatenate([jnp.where(lo, x[:2 * BLOCK], x[2 * BLOCK:]) for x in dkb + dvb], axis=1)
        dkv_ref[...] = dall[BLOCK:, :] + carry[...]
        carry[...] = dall[:BLOCK, :]

    rev = lambda i: (nb - 1 - i, 0)
    return pl.pallas_call(
        functools.partial(body), name="swa_bwd", grid=(nb,),
        in_specs=[pl.BlockSpec((BLOCK, d), rev), pl.BlockSpec((BLOCK, d), rev), pl.BlockSpec((BLOCK, 2 * kvw), rev),
                  pl.BlockSpec((BLOCK, 2 * kvw), lambda i: (jnp.maximum(nb - 2 - i, 0), 0)), _const(sink_b.shape)],
        out_specs=[pl.BlockSpec((BLOCK, d), rev), pl.BlockSpec((BLOCK, 2 * kvw), rev), _resident(sink_b.shape)],
        out_shape=[jax.ShapeDtypeStruct((t, d), BF16), jax.ShapeDtypeStruct((t, 2 * kvw), F32),
                   jax.ShapeDtypeStruct(sink_b.shape, F32)],
        scratch_shapes=[pltpu.VMEM((BLOCK, 2 * kvw), F32)],
        compiler_params=_cparams(("arbitrary",), VMEM_LIMIT),
    )(q, do, kv, kv, sink_b)


def oproj_post_fwd(attn, w_o, h0, gpost, gffn):
    t, d = h0.shape
    tm = _tile_rows(t)

    def body(at_ref, w_ref, h0_ref, gpost_ref, gffn_ref, m_ref, h1_ref, a_ref):
        m = _dot(at_ref[...], w_ref[...])
        m_ref[...] = m
        h1 = h0_ref[...] + m * _rms_r(m) * gpost_ref[...]
        h1_ref[...] = h1
        a_ref[...] = (h1 * _rms_r(h1) * gffn_ref[...]).astype(BF16)

    return pl.pallas_call(
        functools.partial(body), name="oproj_post_fwd", grid=(t // tm,),
        in_specs=[_rows(tm, d), _const(w_o.shape), _rows(tm, d), _const((1, d)), _const((1, d))],
        out_specs=[_rows(tm, d)] * 3,
        out_shape=[jax.ShapeDtypeStruct((t, d), F32), jax.ShapeDtypeStruct((t, d), F32),
                   jax.ShapeDtypeStruct((t, d), BF16)],
        compiler_params=_cparams(("parallel",), VMEM_LIMIT),
    )(attn, w_o, h0, gpost, gffn)


def oproj_post_bwd(dh2, da, h1, m, w_o, gpost, gffn):
    t, d = h1.shape
    tm = _tile_rows(t)

    def body(dh2_ref, da_ref, h1_ref, m_ref, w_ref, gpost_ref, gffn_ref, dh1_ref, dm_ref, dat_ref, gacc_ref):
        _acc_init(gacc_ref)
        h1v, mv = h1_ref[...], m_ref[...]
        dh1_n, dgffn = _rms_bwd(h1v, _rms_r(h1v), gffn_ref[...], da_ref[...])
        dh1 = dh2_ref[...] + dh1_n
        dm, dgpost = _rms_bwd(mv, _rms_r(mv), gpost_ref[...], dh1)
        dmb = dm.astype(BF16)
        dh1_ref[...] = dh1
        dm_ref[...] = dmb
        dat_ref[...] = _dot_nt(dmb, w_ref[...]).astype(BF16)
        gacc_ref[0:1, :] += dgpost
        gacc_ref[1:2, :] += dgffn

    return pl.pallas_call(
        functools.partial(body), name="oproj_post_bwd", grid=(t // tm,),
        in_specs=[_rows(tm, d)] * 4 + [_const(w_o.shape), _const((1, d)), _const((1, d))],
        out_specs=[_rows(tm, d)] * 3 + [_resident((8, d))],
        out_shape=[jax.ShapeDtypeStruct((t, d), F32), jax.ShapeDtypeStruct((t, d), BF16),
                   jax.ShapeDtypeStruct((t, d), BF16), jax.ShapeDtypeStruct((8, d), F32)],
        compiler_params=_cparams(("arbitrary",), VMEM_LIMIT),
    )(dh2, da, h1, m, w_o, gpost, gffn)


def _my_place():
    return lax.axis_index("x"), lax.axis_index("y"), lax.axis_index("c")


def _block_index(px, py, pc):
    return 4 * px + 2 * py + pc


def allgather_pieces(shards, name):
    np_ = len(shards)

    def body(*refs):
        in_refs, out_refs = refs[:np_], refs[np_:2 * np_]
        send_sems, recv_sems, local_sems = refs[2 * np_:]
        x, y, c = _my_place()
        me, sibling = (x, y, c), (x, y, 1 - c)
        chips = [(1 - x, y), (x, 1 - y), (1 - x, 1 - y)]

        def rows(p, place):
            r = in_refs[p].shape[0]
            return out_refs[p].at[pl.ds(_block_index(*place) * r, r), :]

        def copy(p, k, block, to, src=None):
            return pltpu.make_async_remote_copy(
                src_ref=rows(p, block) if src is None else src, dst_ref=rows(p, block),
                send_sem=send_sems.at[p, k], recv_sem=recv_sems.at[p, k], device_id=to, device_id_type=MESH)

        mine = [pltpu.make_async_copy(in_refs[p], rows(p, me), local_sems.at[p]) for p in range(np_)]
        first, passed = [], []
        for p in range(np_):
            mine[p].start()
            first.append(copy(p, 0, me, sibling, src=in_refs[p]))
            first += [copy(p, 1 + j, me, (*chip, c), src=in_refs[p]) for j, chip in enumerate(chips)]
        for cp in first:
            cp.start()
        for p in range(np_):
            for j, chip in enumerate(chips):
                copy(p, 1 + j, (*chip, c), me).wait_recv()
                fwd = copy(p, 4 + j, (*chip, c), sibling)
                fwd.start()
                passed.append(fwd)
        for p in range(np_):
            copy(p, 0, sibling, me).wait_recv()
            for j, chip in enumerate(chips):
                copy(p, 4 + j, (*chip, 1 - c), me).wait_recv()
        for cp in first + passed:
            cp.wait_send()
        for cp in mine:
            cp.wait()

    return pl.pallas_call(
        functools.partial(body), name=name,
        in_specs=[ANY] * np_, out_specs=[ANY] * np_,
        out_shape=[jax.ShapeDtypeStruct((N_DEV * s.shape[0], s.shape[1]), s.dtype) for s in shards],
        scratch_shapes=[pltpu.SemaphoreType.DMA((np_, 7)), pltpu.SemaphoreType.DMA((np_, 7)),
                        pltpu.SemaphoreType.DMA((np_,))],
    )(*shards)


def _peers():
    x, y, c = _my_place()
    flips = [(fx, fy, fc) for fx in (0, 1) for fy in (0, 1) for fc in (0, 1)][1:]
    return [(1 - x if fx else x, 1 - y if fy else y, 1 - c if fc else c) for fx, fy, fc in flips]


HBM = pl.BlockSpec(memory_space=pltpu.HBM)
SEM = pl.BlockSpec(memory_space=pltpu.SEMAPHORE)


def _exchange_windows(scatter, src_ref, land_ref, my_block, peer_block):
    if scatter:
        r = land_ref.shape[1]
        return src_ref.at[pl.ds(peer_block * r, r), :], land_ref.at[my_block], land_ref.at[peer_block]
    r = src_ref.shape[0]
    return src_ref, land_ref.at[pl.ds(my_block * r, r), :], land_ref.at[pl.ds(peer_block * r, r), :]


def _own_copy(scatter, src_ref, land_ref, my_block, sem):
    if scatter:
        r = land_ref.shape[1]
        return pltpu.make_async_copy(src_ref.at[pl.ds(my_block * r, r), :], land_ref.at[my_block], sem)
    r = src_ref.shape[0]
    return pltpu.make_async_copy(src_ref, land_ref.at[pl.ds(my_block * r, r), :], sem)


def exchange_start(srcs, lands, after, scatter, name):
    np_ = len(srcs)

    def body(*refs):
        src_refs, land_refs = refs[:np_], refs[np_:2 * np_]
        send_sems, recv_sems, own_sems = refs[2 * np_ + 1:2 * np_ + 4]
        token = refs[-1]
        my_block = _block_index(*_my_place())
        for p in range(np_):
            _own_copy(scatter, src_refs[p], land_refs[p], my_block, own_sems.at[p]).start()
            for k, peer in enumerate(_peers()):
                src, dst, _ = _exchange_windows(scatter, src_refs[p], land_refs[p], my_block, _block_index(*peer))
                pltpu.make_async_remote_copy(src_ref=src, dst_ref=dst, send_sem=send_sems.at[7 * p + k],
                                             recv_sem=recv_sems.at[7 * p + k], device_id=peer, device_id_type=MESH).start()
        token[...] = jnp.zeros_like(token)

    hbm = lambda a: pltpu.with_memory_space_constraint(a, pltpu.HBM)
    outs = pl.pallas_call(
        functools.partial(body), name=name,
        in_specs=[HBM] * (2 * np_) + [ANY],
        out_specs=[SEM, SEM, SEM] + [HBM] * (2 * np_) + [pl.BlockSpec(memory_space=pltpu.VMEM)],
        out_shape=[pltpu.SemaphoreType.DMA((7 * np_,)), pltpu.SemaphoreType.DMA((7 * np_,)), pltpu.SemaphoreType.DMA((np_,))]
        + [pltpu.HBM(a.shape, a.dtype) for a in list(srcs) + list(lands)] + [jax.ShapeDtypeStruct((8, LANES), F32)],
        input_output_aliases={i: 3 + i for i in range(2 * np_)},
        compiler_params=pltpu.CompilerParams(has_side_effects=pltpu.SideEffectType.DATAFLOW_SIDE_EFFECTING),
    )(*[hbm(a) for a in srcs], *[hbm(a) for a in lands], after)
    return dict(sems=outs[:3], srcs=outs[3:3 + np_], lands=outs[3 + np_:3 + 2 * np_], token=outs[-1], scatter=scatter)


def exchange_wait(started, after, name):
    srcs, lands = started["srcs"], started["lands"]
    scatter = started["scatter"]
    np_ = len(srcs)

    def body(*refs):
        src_refs, land_refs = refs[:np_], refs[np_:2 * np_]
        send_sems, recv_sems, own_sems = refs[2 * np_:2 * np_ + 3]
        my_block = _block_index(*_my_place())
        for p in range(np_):
            _own_copy(scatter, src_refs[p], land_refs[p], my_block, own_sems.at[p]).wait()
            for k, peer in enumerate(_peers()):
                src, dst, arrival = _exchange_windows(scatter, src_refs[p], land_refs[p], my_block, _block_index(*peer))
                pltpu.make_async_remote_copy(src_ref=src, dst_ref=dst, send_sem=send_sems.at[7 * p + k],
                                             recv_sem=recv_sems.at[7 * p + k], device_id=peer, device_id_type=MESH).wait_send()
                pltpu.make_async_remote_copy(src_ref=src, dst_ref=arrival, send_sem=send_sems.at[7 * p + k],
                                             recv_sem=recv_sems.at[7 * p + k], device_id=peer, device_id_type=MESH).wait_recv()

    outs = pl.pallas_call(
        functools.partial(body), name=name,
        in_specs=[HBM] * (2 * np_) + [SEM, SEM, SEM, ANY],
        out_specs=[HBM] * (2 * np_),
        out_shape=[pltpu.HBM(a.shape, a.dtype) for a in list(srcs) + list(lands)],
        input_output_aliases={i: i for i in range(2 * np_)},
        compiler_params=pltpu.CompilerParams(has_side_effects=pltpu.SideEffectType.DATAFLOW_SIDE_EFFECTING),
    )(*srcs, *lands, *started["sems"], after)
    return list(outs[np_:])


def _gather_zone(shard):
    return lax.empty((N_DEV * shard.shape[0], shard.shape[1]), shard.dtype)


def _scatter_zone(full):
    return lax.empty((N_DEV, full.shape[0] // N_DEV, full.shape[1]), full.dtype)


def allreduce_small(pack):
    r, c = pack.shape

    def body(pack_ref, out_ref, gathered, send_sems, recv_sems):
        me = _my_place()
        my_block = _block_index(*me)
        peers = _peers()

        def copy(k, slot, to):
            return pltpu.make_async_remote_copy(
                src_ref=pack_ref, dst_ref=gathered.at[slot], send_sem=send_sems.at[k], recv_sem=recv_sems.at[k],
                device_id=to, device_id_type=MESH)

        sends = [copy(k, my_block, peer) for k, peer in enumerate(peers)]
        for cp in sends:
            cp.start()
        gathered[my_block] = pack_ref[...]
        for k, peer in enumerate(peers):
            copy(k, _block_index(*peer), peer).wait_recv()
        for cp in sends:
            cp.wait_send()
        total = gathered[0]
        for j in range(1, N_DEV):
            total = total + gathered[j]
        out_ref[...] = total

    return pl.pallas_call(
        functools.partial(body), name="allreduce_small",
        in_specs=[pl.BlockSpec(memory_space=pltpu.VMEM)], out_specs=pl.BlockSpec(memory_space=pltpu.VMEM),
        out_shape=jax.ShapeDtypeStruct((r, c), F32),
        scratch_shapes=[pltpu.VMEM((N_DEV, r, c), F32), pltpu.SemaphoreType.DMA((7,)), pltpu.SemaphoreType.DMA((7,))],
    )(pack)


def sum_parts(parts):
    n, r, c = parts.shape
    br = 256 if r % 256 == 0 else r

    def body(p_ref, g_ref):
        g = p_ref[0].astype(F32)
        for j in range(1, n):
            g = g + p_ref[j].astype(F32)
        g_ref[...] = g

    return pl.pallas_call(
        functools.partial(body), name="sum_parts", grid=(r // br,),
        in_specs=[pl.BlockSpec((n, br, c), lambda i: (0, i, 0))], out_specs=_rows(br, c),
        out_shape=jax.ShapeDtypeStruct((r, c), F32),
        compiler_params=_cparams(("parallel",)),
    )(parts)


def adamw(w, m, v, parts):
    r, c = w.shape
    n = parts.shape[0]
    br = 256 if r % 256 == 0 else r

    def body(w_ref, m_ref, v_ref, p_ref, g_ref, d_ref, nm_ref, nv_ref):
        g = p_ref[0].astype(F32)
        for j in range(1, n):
            g = g + p_ref[j].astype(F32)
        nm = ADAM_B1 * m_ref[...] + (1.0 - ADAM_B1) * g
        nv = ADAM_B2 * v_ref[...] + (1.0 - ADAM_B2) * (g * g)
        m_hat = nm / (1.0 - ADAM_B1 ** ADAM_STEP)
        v_hat = nv / (1.0 - ADAM_B2 ** ADAM_STEP)
        g_ref[...] = g
        d_ref[...] = -ADAM_LR * (m_hat / (jnp.sqrt(v_hat) + ADAM_EPS) + ADAM_WD * w_ref[...])
        nm_ref[...] = nm
        nv_ref[...] = nv

    return pl.pallas_call(
        functools.partial(body), name="adamw", grid=(r // br,),
        in_specs=[_rows(br, c)] * 3 + [pl.BlockSpec((n, br, c), lambda i: (0, i, 0))],
        out_specs=[_rows(br, c)] * 4, out_shape=[jax.ShapeDtypeStruct((r, c), F32)] * 4,
        compiler_params=_cparams(("parallel",)),
    )(w, m, v, parts)


def _adamw_nd(w, m, v, g):
    shp = w.shape
    c = shp[-1]
    flat = lambda a: a.reshape(-1, c)
    outs = adamw(flat(w), flat(m), flat(v), flat(g)[None])
    return [o.reshape(shp) for o in outs]


def _pair_heads(a, axis, width=HEAD_DIM):
    shp = a.shape
    a = a.reshape(shp[:axis] + (2, 2, GQA, width) + shp[axis + 1:])
    return jnp.swapaxes(a, axis + 1, axis + 2).reshape(shp)


def _unpair_heads(a, axis, width=HEAD_DIM):
    shp = a.shape
    a = a.reshape(shp[:axis] + (2, GQA, 2, width) + shp[axis + 1:])
    return jnp.swapaxes(a, axis + 1, axis + 2).reshape(shp)


def _pad_rows(a, rows=8):
    return jnp.pad(a, ((0, rows - a.shape[0]), (0, 0)))


def kernel(x, p, mix_pre_g, mix_post_g, ffn_pre_g, ffn_post_g, pool_w, pool_scale, kv_norm_g, w_k, w_v, w_q, w_o, sinks, w_ff_gate, w_ff_up, w_ff_down, ple_norm_g, w_ple_gate, w_ple_proj, loss_target, m_mix_pre_g, m_mix_post_g, m_ffn_pre_g, m_ffn_post_g, m_pool_w, m_pool_scale, m_kv_norm_g, m_w_k, m_w_v, m_w_q, m_w_o, m_sinks, m_w_ff_gate, m_w_ff_up, m_w_ff_down, m_ple_norm_g, m_w_ple_gate, m_w_ple_proj, v_mix_pre_g, v_mix_post_g, v_ffn_pre_g, v_ffn_post_g, v_pool_w, v_pool_scale, v_kv_norm_g, v_w_k, v_w_v, v_w_q, v_w_o, v_sinks, v_w_ff_gate, v_w_ff_up, v_w_ff_down, v_ple_norm_g, v_w_ple_gate, v_w_ple_proj):
    depth = w_ff_gate.shape[0]
    n_a = pool_w.shape[0]
    t, d = x.shape[1], x.shape[2]
    h = x[0]
    tgt = loss_target[0]
    p_all = p.reshape(depth * t, p.shape[-1])
    my_block = _block_index(*_my_place())
    row = lambda g, i: g[i][None, :]
    bf = lambda a: a.astype(BF16)

    full, gathers = [None] * depth, {}
    start_tokens = jnp.zeros((), F32)
    for i in range(depth):
        shards = [bf(w_ff_gate[i].T), bf(w_ff_up[i].T), bf(w_ff_down[i]), bf(w_ple_gate[i]), bf(w_ple_proj[i].T)]
        if i == 0:
            pool0, scale_full = allgather_pieces([bf(pool_w[0].reshape(-1, POOL_GROUP)), _pad_rows(pool_scale)],
                                                 "allgather_pool0")
            order = pool0
        elif i < n_a:
            shards.append(bf(pool_w[i].reshape(-1, POOL_GROUP)))
        else:
            shards += [bf(_pair_heads(w_q[i - n_a], 1)), bf(w_o[i - n_a])]
            if i == n_a:
                shards.append(bf(jnp.concatenate([w_k, w_v], axis=1)))
        gathers[i] = exchange_start(shards, [_gather_zone(s) for s in shards], order, False, f"allgather_start_l{i}")
        order = gathers[i]["token"]
        start_tokens = start_tokens + order[0, 0]
    scale_full = scale_full.reshape(N_DEV, 8, -1)[:, :n_a].transpose(1, 0, 2).reshape(n_a, 1, d)

    cos, sin = _rope_tables(t)
    sink_b = [jnp.broadcast_to(_pair_heads(sinks[j][:, None], 0, 1), (N_HEADS, LANES)) for j in range(depth - n_a)]
    pool_full, wo_full = {}, {}

    saved = []
    kv = hk = None
    for i in range(depth):
        if i > 0:
            full[i] = exchange_wait(gathers[i], h, f"allgather_wait_l{i}")
        s = {"h0": h}
        if i < n_a:
            pool_full[i] = ((pool0 if i == 0 else full[i][5]).reshape(N_DEV, len(POOL_WINDOWS), -1, POOL_GROUP)
                            .transpose(1, 0, 2, 3).reshape(len(POOL_WINDOWS), POOL_GROUP, POOL_GROUP))
            gpre = row(mix_pre_g, i) + start_tokens if i == 0 else row(mix_pre_g, i)
            h1, a = pool_mix_fwd(h, gpre, pool_full[i], scale_full[i], row(mix_post_g, i), row(ffn_pre_g, i))
            if i == 0:
                full[0] = exchange_wait(gathers[0], h1, "allgather_wait_l0")
        else:
            j = i - n_a
            wo_full[i] = _pair_heads(full[i][6], 0)
            if i == n_a:
                hk, kv = proj_rope_fwd(h, kv_norm_g[None, :], full[i][7], cos, sin, N_KV_HEADS * HEAD_DIM, "kv_proj_fwd")
            hn, q = proj_rope_fwd(h, row(mix_pre_g, i), full[i][5], cos, sin, d, "q_proj_fwd")
            attn = swa_fwd(q, kv, sink_b[j])
            m, h1, a = oproj_post_fwd(attn, wo_full[i], h, row(mix_post_g, i), row(ffn_pre_g, i))
            s.update(hn=hn, q=q, attn=attn, m=m)
        wg_t, wu_t, wd, wpg, wpp_t = full[i][:5]
        f, gte, up, hdn = ffn_fwd(a, wg_t, wu_t, wd)
        s.update(h1=h1, a=a, f=f, gte=gte, up=up, hdn=hdn)
        if i < depth - 1:
            h = post_ple_fwd(h1, f, p_all, i, row(ffn_post_g, i), row(ple_norm_g, i), wpg, wpp_t)[0]
        else:
            dh, loss_rows = post_ple_fwd(h1, f, p_all, i, row(ffn_post_g, i), row(ple_norm_g, i), wpg, wpp_t, target=tgt)
        saved.append(s)

    g_mix_pre, g_mix_post, g_ffn_pre, g_ffn_post, g_ple = ([None] * depth for _ in range(5))
    g_kv = g_sinks = None
    g_scale = [None] * n_a
    landing, scatters = [None] * depth, {}
    dkv_sum = []
    scatter_token = jnp.zeros((), F32)
    for i in reversed(range(depth)):
        s = saved[i]
        wg_t, wu_t, wd, wpg, wpp_t = full[i][:5]
        dh2, df, ub, dzb, dppb, gacc = post_ple_bwd(dh, s["h1"], s["f"], p_all, i, row(ffn_post_g, i) + scatter_token,
                                                    row(ple_norm_g, i), wpg, wpp_t)
        g_ple[i], g_ffn_post[i] = gacc[0], gacc[1]
        da, dgte, dup = ffn_bwd_act(df, s["gte"], s["up"], wg_t, wu_t, wd)
        grads = [xty(dgte, s["a"]), xty(dup, s["a"]), xty(s["hdn"], df), xty(ub, dzb), xty(dppb, p_all, i)]
        early = exchange_start(grads, [_scatter_zone(g) for g in grads], dh2, True, f"reduce_scatter_start_l{i}a")
        early_token = early["token"][0, 0]
        if i < n_a:
            dh, dpw, gacc = pool_mix_bwd(s["h0"], dh2, da, row(mix_pre_g, i) + early_token, pool_full[i], scale_full[i],
                                         row(mix_post_g, i), row(ffn_pre_g, i))
            g_mix_pre[i], g_mix_post[i], g_ffn_pre[i], g_scale[i] = gacc[0], gacc[1], gacc[2], gacc[3]
            dpw = dpw.reshape(len(POOL_WINDOWS), N_DEV, -1, POOL_GROUP).transpose(1, 0, 2, 3)
            grads = [bf(dpw.reshape(-1, POOL_GROUP))]
        else:
            j = i - n_a
            dh1, dmb, dattn, gacc = oproj_post_bwd(dh2, da, s["h1"], s["m"], wo_full[i], row(mix_post_g, i) + early_token,
                                                   row(ffn_pre_g, i))
            g_mix_post[i], g_ffn_pre[i] = gacc[0], gacc[1]
            dq, dkv, dsink = swa_bwd(s["q"], kv, dattn, sink_b[j])
            dkv_sum.append(dkv)
            g_sinks = [_unpair_heads(dsink[:, 0:1], 0, 1)[:, 0]] + (g_sinks or [])
            branches = [(row(mix_pre_g, i), full[i][5], d, [dq])]
            if i == n_a:
                branches.append((kv_norm_g[None, :], full[i][7], N_KV_HEADS * HEAD_DIM, dkv_sum))
            outs = proj_rope_bwd(dh1, s["h0"], cos, sin, branches, f"proj_bwd_l{i}")
            dh, gacc = outs[0], outs[-1]
            g_mix_pre[i] = gacc[0]
            grads = [xty(s["hn"], outs[1]), _unpair_heads(xty(s["attn"], dmb), 0)]
            if i == n_a:
                g_kv = gacc[1]
                grads.append(xty(hk, outs[2]))
        late = exchange_start(grads, [_scatter_zone(g) for g in grads], dh, True, f"reduce_scatter_start_l{i}b")
        scatter_token = late["token"][0, 0]
        scatters[i] = (early, late)
    grad_x = dh[None]
    after = dh
    for i in reversed(range(depth)):
        landing[i] = (exchange_wait(scatters[i][0], after, f"reduce_scatter_wait_l{i}a")
                      + exchange_wait(scatters[i][1], after, f"reduce_scatter_wait_l{i}b"))
        after = landing[i][0]

    loss_row = jnp.sum(loss_rows, axis=0, keepdims=True)
    sink_row = jnp.pad(jnp.concatenate(g_sinks)[None, :], ((0, 0), (0, d - sinks.size)))
    stack = lambda rows_: _pad_rows(jnp.stack(rows_))
    pack = jnp.concatenate([stack(g_mix_pre), stack(g_mix_post), stack(g_ffn_pre), stack(g_ffn_post), stack(g_ple),
                            _pad_rows(g_kv[None]), stack(g_scale), _pad_rows(sink_row), _pad_rows(loss_row)], axis=0)
    tot = allreduce_small(pack)
    sec = lambda k, n: tot[8 * k:8 * k + n]
    loss = jnp.sum(tot[64])
    small = {
        "mix_pre_g": sec(0, depth), "mix_post_g": sec(1, depth), "ffn_pre_g": sec(2, depth),
        "ffn_post_g": sec(3, depth), "ple_norm_g": sec(4, depth), "kv_norm_g": tot[40],
        "pool_scale": lax.dynamic_slice_in_dim(sec(6, n_a), my_block * pool_scale.shape[1], pool_scale.shape[1], axis=1),
        "sinks": tot[56, :sinks.size].reshape(sinks.shape),
    }

    weights = dict(mix_pre_g=mix_pre_g, mix_post_g=mix_post_g, ffn_pre_g=ffn_pre_g, ffn_post_g=ffn_post_g, pool_w=pool_w, pool_scale=pool_scale, kv_norm_g=kv_norm_g, w_k=w_k, w_v=w_v, w_q=w_q, w_o=w_o, sinks=sinks, w_ff_gate=w_ff_gate, w_ff_up=w_ff_up, w_ff_down=w_ff_down, ple_norm_g=ple_norm_g, w_ple_gate=w_ple_gate, w_ple_proj=w_ple_proj)
    mom1 = dict(mix_pre_g=m_mix_pre_g, mix_post_g=m_mix_post_g, ffn_pre_g=m_ffn_pre_g, ffn_post_g=m_ffn_post_g, pool_w=m_pool_w, pool_scale=m_pool_scale, kv_norm_g=m_kv_norm_g, w_k=m_w_k, w_v=m_w_v, w_q=m_w_q, w_o=m_w_o, sinks=m_sinks, w_ff_gate=m_w_ff_gate, w_ff_up=m_w_ff_up, w_ff_down=m_w_ff_down, ple_norm_g=m_ple_norm_g, w_ple_gate=m_w_ple_gate, w_ple_proj=m_w_ple_proj)
    mom2 = dict(mix_pre_g=v_mix_pre_g, mix_post_g=v_mix_post_g, ffn_pre_g=v_ffn_pre_g, ffn_post_g=v_ffn_post_g, pool_w=v_pool_w, pool_scale=v_pool_scale, kv_norm_g=v_kv_norm_g, w_k=v_w_k, w_v=v_w_v, w_q=v_w_q, w_o=v_w_o, sinks=v_sinks, w_ff_gate=v_w_ff_gate, w_ff_up=v_w_ff_up, w_ff_down=v_w_ff_down, ple_norm_g=v_ple_norm_g, w_ple_gate=v_w_ple_gate, w_ple_proj=v_w_ple_proj)

    def land(i, k):
        return sum_parts(landing[i][k])

    gw = dict(small)
    gw["kv_norm_g"] = small["kv_norm_g"]
    gw["w_ff_gate"] = jnp.stack([land(i, 0).T for i in range(depth)])
    gw["w_ff_up"] = jnp.stack([land(i, 1).T for i in range(depth)])
    gw["w_ff_down"] = jnp.stack([land(i, 2) for i in range(depth)])
    gw["w_ple_gate"] = jnp.stack([land(i, 3) for i in range(depth)])
    gw["w_ple_proj"] = jnp.stack([land(i, 4).T for i in range(depth)])
    gw["pool_w"] = jnp.stack([land(i, 5).reshape(pool_w.shape[1:]) for i in range(n_a)])
    gw["w_q"] = jnp.stack([_unpair_heads(land(i, 5), 1) for i in range(n_a, depth)])
    gw["w_o"] = jnp.stack([land(i, 6) for i in range(n_a, depth)])
    gkv = land(n_a, 7)
    gw["w_k"], gw["w_v"] = gkv[:, :w_k.shape[1]], gkv[:, w_k.shape[1]:]

    order = ["mix_pre_g", "mix_post_g", "ffn_pre_g", "ffn_post_g", "pool_w", "pool_scale", "kv_norm_g", "w_k", "w_v",
             "w_q", "w_o", "sinks", "w_ff_gate", "w_ff_up", "w_ff_down", "ple_norm_g", "w_ple_gate", "w_ple_proj"]
    g_out, d_out, m_out, v_out = [], [], [], []
    for nme in order:
        w = weights[nme]
        as2d = (lambda a: a[None, :]) if w.ndim == 1 else (lambda a: a)
        g, dl, nm, nv = _adamw_nd(as2d(w), as2d(mom1[nme]), as2d(mom2[nme]), as2d(gw[nme]))
        for lst, val in ((g_out, g), (d_out, dl), (m_out, nm), (v_out, nv)):
            lst.append(val.reshape(w.shape))
    return (loss, grad_x, *g_out, *d_out, *m_out, *v_out)
```

```python
import functools

import jax
import jax.numpy as jnp
from jax import lax
from jax.experimental import pallas as pl
from jax.experimental.pallas import tpu as pltpu

F32 = jnp.float32
BF16 = jnp.bfloat16

N_DEV = 8
HEAD_DIM = 64
N_HEADS = 16
N_KV_HEADS = 4
GQA = N_HEADS // N_KV_HEADS
BLOCK = 128
POOL_WINDOWS = (2, 4, 8, 16)
POOL_GROUP = 256
HALO = 16
ROPE_THETA = 10000.0
RMS_EPS = 1e-6
NEG_INF = -1e30
LANES = 128
ATTN_SUB = 4
XTY_ROWS = 2048
FFN_CHUNK = 768
VMEM_LIMIT = 56 * 1024 * 1024

ADAM_LR = 0.001
ADAM_B1 = 0.9
ADAM_B2 = 0.999
ADAM_EPS = 1e-08
ADAM_WD = 0.01
ADAM_STEP = 10

MESH = pl.DeviceIdType.MESH
ANY = pl.BlockSpec(memory_space=pl.ANY)

NT_DIMS = (((1,), (1,)), ((), ()))
TN_DIMS = (((0,), (0,)), ((), ()))


def _cparams(sem=None, vmem=None):
    kw = {}
    if sem is not None:
        kw["dimension_semantics"] = sem
    if vmem is not None:
        kw["vmem_limit_bytes"] = vmem
    return pltpu.CompilerParams(**kw)


def _rows(tm, n, first=0):
    return pl.BlockSpec((tm, n), lambda i: (i + first, 0))


def _rows_rev(tm, n, nt):
    return pl.BlockSpec((tm, n), lambda i: (nt - 1 - i, 0))


def _const(shape):
    nd = len(shape)
    return pl.BlockSpec(shape, lambda *_: (0,) * nd, pipeline_mode=pl.Buffered(1))


def _resident(shape):
    nd = len(shape)
    return pl.BlockSpec(shape, lambda *_: (0,) * nd)


def _tile_rows(t):
    return 512 if t % 512 == 0 else 128


def _dot(a, b):
    return jnp.dot(a, b, preferred_element_type=F32)


def _dot_nt(a, b):
    return lax.dot_general(a, b, NT_DIMS, preferred_element_type=F32)


def _dot_tn(a, b):
    return lax.dot_general(a, b, TN_DIMS, preferred_element_type=F32)


def _rms_r(x):
    return lax.rsqrt(jnp.mean(x * x, axis=-1, keepdims=True) + RMS_EPS)


def _rms_bwd(x, r, g, dy):
    gy = dy * g
    dx = r * gy - x * (r * r * r * jnp.mean(gy * x, axis=-1, keepdims=True))
    dg = jnp.sum(dy * (x * r), axis=0, keepdims=True)
    return dx, dg


def _sigmoid(x):
    return jax.nn.sigmoid(x)


def _rope_tables(t):
    inv = 1.0 / (ROPE_THETA ** (jnp.arange(0, HEAD_DIM, 2, dtype=F32) / HEAD_DIM))
    ang = jnp.arange(t, dtype=F32)[:, None] * inv[None, :]
    c, s = jnp.cos(ang), jnp.sin(ang)
    cos = jnp.concatenate([c, c, c, c], axis=1)
    sin = jnp.concatenate([-s, s, -s, s], axis=1)
    return cos, sin


def _swap_halves(x):
    n = x.shape[1]
    lane = lax.broadcasted_iota(jnp.int32, x.shape, 1)
    first = (lane % HEAD_DIM) < (HEAD_DIM // 2)
    return jnp.where(first, pltpu.roll(x, n - HEAD_DIM // 2, 1), pltpu.roll(x, HEAD_DIM // 2, 1))


def _rope(x, cos, sin):
    reps = x.shape[1] // LANES
    return x * jnp.tile(cos, (1, reps)) + _swap_halves(x) * jnp.tile(sin, (1, reps))


def _unrope(dy, cos, sin):
    reps = dy.shape[1] // LANES
    return dy * jnp.tile(cos, (1, reps)) + _swap_halves(dy * jnp.tile(sin, (1, reps)))


def _acc_init(acc_ref):
    @pl.when(pl.program_id(0) == 0)
    def _():
        acc_ref[...] = jnp.zeros_like(acc_ref)


def _window_sums(ext, tm, forward):
    n = tm + HALO
    out = []
    for g, w in enumerate(POOL_WINDOWS):
        s = ext[:, g * POOL_GROUP:(g + 1) * POOL_GROUP]
        k = 1
        while k < w:
            s = s + pltpu.roll(s, k if forward else n - k, 0)
            k *= 2
        out.append(s[HALO:, :] if forward else s[:tm, :])
    return out


def _pool_inv_counts(tile, tm):
    t = tile * tm + lax.broadcasted_iota(jnp.int32, (tm, 1), 0)
    return [1.0 / jnp.minimum(t + 1, w).astype(F32) for w in POOL_WINDOWS]


def _pool_mix(hn, ext, inv_cnts, pw_ref, scale, tm):
    sums = _window_sums(ext, tm, True)
    pooled, ys = [], []
    for g in range(len(POOL_WINDOWS)):
        pg = (sums[g] * inv_cnts[g] - hn[:, g * POOL_GROUP:(g + 1) * POOL_GROUP]).astype(BF16)
        pooled.append(pg)
        ys.append(_dot(pg, pw_ref[g]))
    y = jnp.concatenate(ys, axis=1)
    return pooled, y, y * scale


def pool_mix_fwd(h0, gpre, pool_w, scale, gpost, gffn):
    t, d = h0.shape
    tm = _tile_rows(t)

    def body(h_ref, gpre_ref, pw_ref, scale_ref, gpost_ref, gffn_ref, h1_ref, a_ref, carry):
        i = pl.program_id(0)

        @pl.when(i == 0)
        def _():
            carry[...] = jnp.zeros_like(carry)

        x = h_ref[...]
        hn = x * _rms_r(x) * gpre_ref[...]
        ext = jnp.concatenate([carry[...], hn], axis=0)
        carry[...] = hn[tm - HALO:, :]
        _, _, m = _pool_mix(hn, ext, _pool_inv_counts(i, tm), pw_ref, scale_ref[...], tm)
        h1 = x + m * _rms_r(m) * gpost_ref[...]
        h1_ref[...] = h1
        a_ref[...] = (h1 * _rms_r(h1) * gffn_ref[...]).astype(BF16)

    return pl.pallas_call(
        functools.partial(body), name="pool_mix_fwd", grid=(t // tm,),
        in_specs=[_rows(tm, d), _const((1, d)), _const(pool_w.shape), _const((1, d)), _const((1, d)), _const((1, d))],
        out_specs=[_rows(tm, d), _rows(tm, d)],
        out_shape=[jax.ShapeDtypeStruct((t, d), F32), jax.ShapeDtypeStruct((t, d), BF16)],
        scratch_shapes=[pltpu.VMEM((HALO, d), F32)],
        compiler_params=_cparams(("arbitrary",), VMEM_LIMIT),
    )(h0, gpre, pool_w, scale, gpost, gffn)


def pool_mix_bwd(h0, dh2, da, gpre, pool_w, scale, gpost, gffn):
    t, d = h0.shape
    tm = _tile_rows(t)
    nt = t // tm
    hb = tm // HALO

    def body(h_ref, halo_ref, dh2_ref, da_ref, gpre_ref, pw_ref, scale_ref, gpost_ref, gffn_ref,
             dh0_ref, dpw_ref, gacc_ref, carry):
        i = pl.program_id(0)
        tile = nt - 1 - i
        _acc_init(gacc_ref)
        _acc_init(dpw_ref)

        @pl.when(i == 0)
        def _():
            carry[...] = jnp.zeros_like(carry)

        x = h_ref[...]
        gpre_v, scale_v, gpost_v, gffn_v = gpre_ref[...], scale_ref[...], gpost_ref[...], gffn_ref[...]
        r0 = _rms_r(x)
        hn = x * r0 * gpre_v
        xh = halo_ref[...]
        hn_halo = jnp.where(tile > 0, xh * _rms_r(xh) * gpre_v, 0.0)
        ext = jnp.concatenate([hn_halo, hn], axis=0)
        inv_cnts = _pool_inv_counts(tile, tm)
        pooled, y, m = _pool_mix(hn, ext, inv_cnts, pw_ref, scale_v, tm)
        rm = _rms_r(m)
        h1 = x + m * rm * gpost_v
        dh1_n, dgffn = _rms_bwd(h1, _rms_r(h1), gffn_v, da_ref[...])
        dh1 = dh2_ref[...] + dh1_n
        dm, dgpost = _rms_bwd(m, rm, gpost_v, dh1)
        dscale = jnp.sum(dm * y, axis=0, keepdims=True)
        dy = (dm * scale_v).astype(BF16)
        dpn = []
        for g in range(len(POOL_WINDOWS)):
            dyg = dy[:, g * POOL_GROUP:(g + 1) * POOL_GROUP]
            dpw_ref[g] += _dot_tn(pooled[g], dyg)
            dpn.append(_dot_nt(dyg, pw_ref[g]))
        dpooled = jnp.concatenate(dpn, axis=1)
        dpc = jnp.concatenate([dpn[g] * inv_cnts[g] for g in range(len(POOL_WINDOWS))], axis=1)
        ext2 = jnp.concatenate([dpc, carry[...]], axis=0)
        carry[...] = dpc[:HALO, :]
        dhn = jnp.concatenate(_window_sums(ext2, tm, False), axis=1) - dpooled
        dh0_n, dgpre = _rms_bwd(x, r0, gpre_v, dhn)
        dh0_ref[...] = dh1 + dh0_n
        gacc_ref[0:1, :] += dgpre
        gacc_ref[1:2, :] += dgpost
        gacc_ref[2:3, :] += dgffn
        gacc_ref[3:4, :] += dscale

    return pl.pallas_call(
        functools.partial(body), name="pool_mix_bwd", grid=(nt,),
        in_specs=[_rows_rev(tm, d, nt),
                  pl.BlockSpec((HALO, d), lambda i: (jnp.maximum((nt - 1 - i) * hb - 1, 0), 0)),
                  _rows_rev(tm, d, nt), _rows_rev(tm, d, nt),
                  _const((1, d)), _const(pool_w.shape), _const((1, d)), _const((1, d)), _const((1, d))],
        out_specs=[_rows_rev(tm, d, nt), _resident(pool_w.shape), _resident((8, d))],
        out_shape=[jax.ShapeDtypeStruct((t, d), F32), jax.ShapeDtypeStruct(pool_w.shape, F32),
                   jax.ShapeDtypeStruct((8, d), F32)],
        scratch_shapes=[pltpu.VMEM((HALO, d), F32)],
        compiler_params=_cparams(("arbitrary",), VMEM_LIMIT),
    )(h0, h0, dh2, da, gpre, pool_w, scale, gpost, gffn)


def _ffn_chunks(f):
    return [(c, min(c + FFN_CHUNK, f)) for c in range(0, f, FFN_CHUNK)]


def ffn_fwd(a, wg_t, wu_t, wd):
    t, d = a.shape
    f = wd.shape[0]
    tm = _tile_rows(t)

    def body(a_ref, wg_ref, wu_ref, wd_ref, f_ref, gte_ref, up_ref, hdn_ref):
        av = a_ref[...]
        acc = jnp.zeros((tm, d), F32)
        for c0, c1 in _ffn_chunks(f):
            gte = _dot_nt(av, wg_ref[c0:c1, :])
            up = _dot_nt(av, wu_ref[c0:c1, :])
            gte_ref[:, c0:c1] = gte.astype(BF16)
            up_ref[:, c0:c1] = up.astype(BF16)
            hdn = (gte * _sigmoid(gte) * up).astype(BF16)
            hdn_ref[:, c0:c1] = hdn
            acc = acc + _dot(hdn, wd_ref[c0:c1, :])
        f_ref[...] = acc.astype(BF16)

    return pl.pallas_call(
        functools.partial(body), name="ffn_fwd", grid=(t // tm,),
        in_specs=[_rows(tm, d), _const((f, d)), _const((f, d)), _const((f, d))],
        out_specs=[_rows(tm, d), _rows(tm, f), _rows(tm, f), _rows(tm, f)],
        out_shape=[jax.ShapeDtypeStruct((t, d), BF16)] + [jax.ShapeDtypeStruct((t, f), BF16)] * 3,
        compiler_params=_cparams(("parallel",), VMEM_LIMIT),
    )(a, wg_t, wu_t, wd)


def ffn_bwd_act(df, gte, up, wg_t, wu_t, wd):
    t, d = df.shape
    f = wd.shape[0]
    tm = _tile_rows(t)

    def body(df_ref, gte_ref, up_ref, wg_ref, wu_ref, wd_ref, da_ref, dgte_ref, dup_ref):
        dfv = df_ref[...]
        chunks = _ffn_chunks(f)
        half = chunks[len(chunks) // 2][0]
        acc = None
        for c0, c1 in chunks:
            g = gte_ref[:, c0:c1].astype(F32)
            u = up_ref[:, c0:c1].astype(F32)
            sg = _sigmoid(g)
            sl = g * sg
            dh = _dot_nt(dfv, wd_ref[c0:c1, :])
            dup_ref[:, c0:c1] = (dh * sl).astype(BF16)
            dgte_ref[:, c0:c1] = (dh * u * (sg * (1.0 + g * (1.0 - sg)))).astype(BF16)
            if c1 == half:
                acc = _dot(dgte_ref[:, :half], wg_ref[:half, :]) + _dot(dup_ref[:, :half], wu_ref[:half, :])
        da_ref[...] = acc + _dot(dgte_ref[:, half:], wg_ref[half:, :]) + _dot(dup_ref[:, half:], wu_ref[half:, :])

    return pl.pallas_call(
        functools.partial(body), name="ffn_bwd_act", grid=(t // tm,),
        in_specs=[_rows(tm, d), _rows(tm, f), _rows(tm, f), _const((f, d)), _const((f, d)), _const((f, d))],
        out_specs=[_rows(tm, d), _rows(tm, f), _rows(tm, f)],
        out_shape=[jax.ShapeDtypeStruct((t, d), F32)] + [jax.ShapeDtypeStruct((t, f), BF16)] * 2,
        compiler_params=_cparams(("parallel",), VMEM_LIMIT),
    )(df, gte, up, wg_t, wu_t, wd)


def xty(x, y, y_part=0):
    t, nx = x.shape
    ny = y.shape[1]
    tk = XTY_ROWS if t % XTY_ROWS == 0 else _tile_rows(t)
    bn = nx // 2 if nx > 1024 else nx
    nk = t // tk

    def body(x_ref, y_ref, o_ref, acc):
        k = pl.program_id(1)

        @pl.when(k == 0)
        def _():
            acc[...] = jnp.zeros_like(acc)

        acc[...] += _dot_tn(x_ref[...].astype(BF16), y_ref[...].astype(BF16))

        @pl.when(k == nk - 1)
        def _():
            o_ref[...] = acc[...].astype(BF16)

    return pl.pallas_call(
        functools.partial(body), name="xty", grid=(nx // bn, nk),
        in_specs=[pl.BlockSpec((tk, bn), lambda j, k: (k, j)),
                  pl.BlockSpec((tk, ny), lambda j, k: (k + y_part * nk, 0))],
        out_specs=pl.BlockSpec((bn, ny), lambda j, k: (j, 0)),
        out_shape=jax.ShapeDtypeStruct((nx, ny), BF16),
        scratch_shapes=[pltpu.VMEM((bn, ny), F32)],
        compiler_params=_cparams(("parallel", "arbitrary"), VMEM_LIMIT),
    )(x, y)


def _ple_fwd_tile(h1, f, p, gpost, gple, wpg_ref, wpp_ref):
    rf = _rms_r(f)
    h2 = h1 + f * rf * gpost
    r2 = _rms_r(h2)
    ub = (h2 * r2 * gple).astype(BF16)
    gate = _sigmoid(_dot(ub, wpg_ref[...]))
    pp = _dot_nt(p.astype(BF16), wpp_ref[...])
    return rf, h2, r2, ub, gate, pp


def post_ple_fwd(h1, f, p, layer, gpost, gple, wpg, wpp_t, target=None):
    t, d = h1.shape
    pd = p.shape[1]
    tm = _tile_rows(t)
    with_loss = target is not None

    def body(*refs):
        if with_loss:
            h1_ref, f_ref, p_ref, gpost_ref, gple_ref, wpg_ref, wpp_ref, tgt_ref, out_ref, loss_ref = refs
        else:
            h1_ref, f_ref, p_ref, gpost_ref, gple_ref, wpg_ref, wpp_ref, out_ref = refs
        _, h2, _, _, gate, pp = _ple_fwd_tile(h1_ref[...], f_ref[...].astype(F32), p_ref[...], gpost_ref[...],
                                              gple_ref[...], wpg_ref, wpp_ref)
        h3 = h2 + pp * gate
        if with_loss:
            err = h3 - tgt_ref[...]
            out_ref[...] = err * (1.0 / d)
            colsum = jnp.sum(err * err, axis=0, keepdims=True) * (0.5 / d)
            loss_ref[...] = jnp.broadcast_to(colsum, (8, d)) * (lax.broadcasted_iota(jnp.int32, (8, d), 0) == 0)
        else:
            out_ref[...] = h3

    in_specs = [_rows(tm, d), _rows(tm, d), _rows(tm, pd, layer * (t // tm)), _const((1, d)), _const((1, d)),
                _const(wpg.shape), _const(wpp_t.shape)]
    out_specs = [_rows(tm, d)]
    out_shape = [jax.ShapeDtypeStruct((t, d), F32)]
    args = [h1, f, p, gpost, gple, wpg, wpp_t]
    if with_loss:
        in_specs.append(_rows(tm, d))
        out_specs.append(_rows(8, d))
        out_shape.append(jax.ShapeDtypeStruct((t // tm * 8, d), F32))
        args.append(target)
    return pl.pallas_call(
        functools.partial(body), name="post_ple_loss" if with_loss else "post_ple_fwd", grid=(t // tm,),
        in_specs=in_specs, out_specs=out_specs, out_shape=out_shape,
        compiler_params=_cparams(("parallel",), VMEM_LIMIT),
    )(*args)


def post_ple_bwd(dh3, h1, f, p, layer, gpost, gple, wpg, wpp_t):
    t, d = h1.shape
    pd = p.shape[1]
    tm = _tile_rows(t)

    def body(dh3_ref, h1_ref, f_ref, p_ref, gpost_ref, gple_ref, wpg_ref, wpp_ref,
             dh2_ref, df_ref, u_ref, dz_ref, dpp_ref, gacc_ref):
        _acc_init(gacc_ref)
        fv = f_ref[...].astype(F32)
        gpost_v, gple_v = gpost_ref[...], gple_ref[...]
        rf, h2, r2, ub, gate, pp = _ple_fwd_tile(h1_ref[...], fv, p_ref[...], gpost_v, gple_v, wpg_ref, wpp_ref)
        dh3v = dh3_ref[...]
        dpp_ref[...] = (dh3v * gate).astype(BF16)
        dz = (dh3v * pp * gate * (1.0 - gate)).astype(BF16)
        dz_ref[...] = dz
        u_ref[...] = ub
        du = _dot_nt(dz, wpg_ref[...])
        dh2_n, dgple = _rms_bwd(h2, r2, gple_v, du)
        dh2 = dh3v + dh2_n
        df, dgpost = _rms_bwd(fv, rf, gpost_v, dh2)
        dh2_ref[...] = dh2
        df_ref[...] = df.astype(BF16)
        gacc_ref[0:1, :] += dgple
        gacc_ref[1:2, :] += dgpost

    return pl.pallas_call(
        functools.partial(body), name="post_ple_bwd", grid=(t // tm,),
        in_specs=[_rows(tm, d), _rows(tm, d), _rows(tm, d), _rows(tm, pd, layer * (t // tm)), _const((1, d)),
                  _const((1, d)), _const(wpg.shape), _const(wpp_t.shape)],
        out_specs=[_rows(tm, d)] * 5 + [_resident((8, d))],
        out_shape=[jax.ShapeDtypeStruct((t, d), F32)] + [jax.ShapeDtypeStruct((t, d), BF16)] * 4
        + [jax.ShapeDtypeStruct((8, d), F32)],
        compiler_params=_cparams(("arbitrary",), VMEM_LIMIT),
    )(dh3, h1, f, p, gpost, gple, wpg, wpp_t)


def proj_rope_fwd(h, gain, w, cos, sin, n_rope, name):
    t, d = h.shape
    n = w.shape[1]
    tm = _tile_rows(t)

    def body(h_ref, g_ref, w_ref, cos_ref, sin_ref, hn_ref, y_ref):
        x = h_ref[...]
        hn = (x * _rms_r(x) * g_ref[...]).astype(BF16)
        hn_ref[...] = hn
        y = _dot(hn, w_ref[...])
        y_ref[:, :n_rope] = _rope(y[:, :n_rope], cos_ref[...], sin_ref[...]).astype(BF16)
        if n_rope < n:
            y_ref[:, n_rope:] = y[:, n_rope:].astype(BF16)

    return pl.pallas_call(
        functools.partial(body), name=name, grid=(t // tm,),
        in_specs=[_rows(tm, d), _const((1, d)), _const(w.shape), _rows(tm, LANES), _rows(tm, LANES)],
        out_specs=[_rows(tm, d), _rows(tm, n)],
        out_shape=[jax.ShapeDtypeStruct((t, d), BF16), jax.ShapeDtypeStruct((t, n), BF16)],
        compiler_params=_cparams(("parallel",), VMEM_LIMIT),
    )(h, gain, w, cos, sin)


def proj_rope_bwd(dh1, h0, cos, sin, branches, name):
    t, d = h0.shape
    tm = _tile_rows(t)
    nb = len(branches)
    n_cot = [len(b[3]) for b in branches]

    def body(*refs):
        dh1_ref, h0_ref, cos_ref, sin_ref = refs[:4]
        pos = 4
        br_refs = []
        for b in range(nb):
            br_refs.append((refs[pos], refs[pos + 1], refs[pos + 2:pos + 2 + n_cot[b]]))
            pos += 2 + n_cot[b]
        dh0_ref = refs[pos]
        dpre_refs = refs[pos + 1:pos + 1 + nb]
        gacc_ref = refs[pos + 1 + nb]
        _acc_init(gacc_ref)
        x = h0_ref[...]
        r0 = _rms_r(x)
        dh = dh1_ref[...]
        for b in range(nb):
            g_ref, w_ref, cot_refs = br_refs[b]
            n_rope = branches[b][2]
            dy = cot_refs[0][...].astype(F32)
            for c_ref in cot_refs[1:]:
                dy = dy + c_ref[...].astype(F32)
            n = dy.shape[1]
            dpre_refs[b][:, :n_rope] = _unrope(dy[:, :n_rope], cos_ref[...], sin_ref[...]).astype(BF16)
            if n_rope < n:
                dpre_refs[b][:, n_rope:] = dy[:, n_rope:].astype(BF16)
            dhn = _dot_nt(dpre_refs[b][...], w_ref[...])
            dx, dg = _rms_bwd(x, r0, g_ref[...], dhn)
            dh = dh + dx
            gacc_ref[b:b + 1, :] += dg
        dh0_ref[...] = dh

    in_specs = [_rows(tm, d), _rows(tm, d), _rows(tm, LANES), _rows(tm, LANES)]
    args = [dh1, h0, cos, sin]
    out_specs = [_rows(tm, d)]
    out_shape = [jax.ShapeDtypeStruct((t, d), F32)]
    for gain, w, _, cots in branches:
        n = w.shape[1]
        in_specs += [_const((1, d)), _const(w.shape)] + [_rows(tm, n)] * len(cots)
        args += [gain, w] + list(cots)
        out_specs.append(_rows(tm, n))
        out_shape.append(jax.ShapeDtypeStruct((t, n), BF16))
    out_specs.append(_resident((8, d)))
    out_shape.append(jax.ShapeDtypeStruct((8, d), F32))
    return pl.pallas_call(
        functools.partial(body), name=name, grid=(t // tm,),
        in_specs=in_specs, out_specs=out_specs, out_shape=out_shape,
        compiler_params=_cparams(("arbitrary",), VMEM_LIMIT),
    )(*args)


def _tri():
    row = lax.broadcasted_iota(jnp.int32, (BLOCK, BLOCK), 0)
    col = lax.broadcasted_iota(jnp.int32, (BLOCK, BLOCK), 1)
    return col <= row


def _block_diag(x):
    lo = lax.broadcasted_iota(jnp.int32, x.shape, 1) < HEAD_DIM
    zero = jnp.zeros_like(x)
    return jnp.concatenate([jnp.where(lo, x, zero), jnp.where(lo, zero, x)], axis=0)


def _dense(x, tri):
    return (jnp.where(tri, x[:, BLOCK:2 * BLOCK], x[:, :BLOCK]),
            jnp.where(tri, x[:, 3 * BLOCK:], x[:, 2 * BLOCK:3 * BLOCK]))


def _banded(xa, xb, tri):
    zero = jnp.zeros_like(xa)
    return jnp.concatenate([jnp.where(tri, zero, xa), jnp.where(tri, xa, zero),
                            jnp.where(tri, zero, xb), jnp.where(tri, xb, zero)], axis=1).astype(BF16)


def _softmax_sink(s, sink):
    mx = jnp.maximum(jnp.max(s, axis=1, keepdims=True), sink)
    e = jnp.exp(s - mx)
    es = jnp.exp(sink - mx)
    inv = 1.0 / (jnp.sum(e, axis=1, keepdims=True) + es)
    return e * inv, es * inv


def _sink_column(sink_ref):
    return jnp.concatenate([jnp.broadcast_to(sink_ref[h:h + 1, 0:1], (BLOCK, 1)) for h in range(N_HEADS)], axis=0)


def _kv_block_diag(band, kvw):
    n_lt = kvw // LANES
    return ([_block_diag(band[:, lt * LANES:(lt + 1) * LANES]) for lt in range(n_lt)],
            [_block_diag(band[:, kvw + lt * LANES:kvw + (lt + 1) * LANES]) for lt in range(n_lt)])


def _all_probs(q_ref, r0, kbd, tri, n, sink_ref):
    dense = []
    for tq in range(N_HEADS // 2):
        s = _dot_nt(q_ref[r0:r0 + BLOCK, tq * LANES:(tq + 1) * LANES], kbd[tq // GQA])
        dense += list(_dense(s, tri))
    bias = jnp.where(jnp.logical_not(tri) & (n == 0), NEG_INF, 0.0)
    s_all = jnp.concatenate(dense, axis=0) * (HEAD_DIM ** -0.5) + jnp.concatenate([bias] * N_HEADS, axis=0)
    return _softmax_sink(s_all, _sink_column(sink_ref))


def _head_rows(x, tq):
    return x[2 * tq * BLOCK:(2 * tq + 1) * BLOCK], x[(2 * tq + 1) * BLOCK:(2 * tq + 2) * BLOCK]


def _attn_sub(t):
    return ATTN_SUB if t % (ATTN_SUB * BLOCK) == 0 else 1


def swa_fwd(q, kv, sink_b):
    t, d = q.shape
    sub = _attn_sub(t)
    kvw = N_KV_HEADS * HEAD_DIM

    def body(q_ref, kvc_ref, kvp_ref, sink_ref, o_ref):
        i = pl.program_id(0)
        tri = _tri()
        ext = jnp.concatenate([kvp_ref[...], kvc_ref[...]], axis=0)
        for sb in range(sub):
            r0 = sb * BLOCK
            kbd, vbd = _kv_block_diag(ext[r0:r0 + 2 * BLOCK], kvw)
            p, _ = _all_probs(q_ref, r0, kbd, tri, i * sub + sb, sink_ref)
            for tq in range(N_HEADS // 2):
                pa, pb = _head_rows(p, tq)
                o_ref[r0:r0 + BLOCK, tq * LANES:(tq + 1) * LANES] = _dot(_banded(pa, pb, tri), vbd[tq // GQA]).astype(BF16)

    return pl.pallas_call(
        functools.partial(body), name="swa_fwd", grid=(t // (sub * BLOCK),),
        in_specs=[_rows(sub * BLOCK, d), _rows(sub * BLOCK, 2 * kvw),
                  pl.BlockSpec((BLOCK, 2 * kvw), lambda i: (jnp.maximum(i * sub - 1, 0), 0)), _const(sink_b.shape)],
        out_specs=_rows(sub * BLOCK, d),
        out_shape=jax.ShapeDtypeStruct((t, d), BF16),
        compiler_params=_cparams(("parallel",), VMEM_LIMIT),
    )(q, kv, kv, sink_b)


def swa_bwd(q, kv, do, sink_b):
    t, d = q.shape
    sub = _attn_sub(t)
    nq = t // (sub * BLOCK)
    kvw = N_KV_HEADS * HEAD_DIM

    def body(q_ref, do_ref, kvc_ref, kvp_ref, sink_ref, dq_ref, dkv_ref, dsink_ref, carry):
        i = pl.program_id(0)
        step = nq - 1 - i
        _acc_init(dsink_ref)

        @pl.when(i == 0)
        def _():
            carry[...] = jnp.zeros_like(carry)

        tri = _tri()
        lo = lax.broadcasted_iota(jnp.int32, (2 * BLOCK, LANES), 1) < HEAD_DIM
        ext = jnp.concatenate([kvp_ref[...], kvc_ref[...]], axis=0)
        dkeys = [None] * (sub + 1)
        for sb in reversed(range(sub)):
            r0 = sb * BLOCK
            kbd, vbd = _kv_block_diag(ext[r0:r0 + 2 * BLOCK], kvw)
            p, ps = _all_probs(q_ref, r0, kbd, tri, step * sub + sb, sink_ref)
            dp = []
            for tq in range(N_HEADS // 2):
                dp += list(_dense(_dot_nt(do_ref[r0:r0 + BLOCK, tq * LANES:(tq + 1) * LANES], vbd[tq // GQA]), tri))
            dp = jnp.concatenate(dp, axis=0)
            delta = jnp.sum(p * dp, axis=1, keepdims=True)
            ds = p * (dp - delta) * (HEAD_DIM ** -0.5)
            dsk = ps * delta
            for h in range(N_HEADS):
                dsink_ref[h:h + 1, :] -= jnp.sum(dsk[h * BLOCK:(h + 1) * BLOCK], axis=0, keepdims=True)
            dkb = [jnp.zeros((4 * BLOCK, LANES), F32) for _ in kbd]
            dvb = [jnp.zeros((4 * BLOCK, LANES), F32) for _ in kbd]
            for tq in range(N_HEADS // 2):
                lt = tq // GQA
                cols = slice(tq * LANES, (tq + 1) * LANES)
                dsb = _banded(*_head_rows(ds, tq), tri)
                dq_ref[r0:r0 + BLOCK, cols] = _dot(dsb, kbd[lt]).astype(BF16)
                dkb[lt] = dkb[lt] + _dot_tn(dsb, q_ref[r0:r0 + BLOCK, cols])
                dvb[lt] = dvb[lt] + _dot_tn(_banded(*_head_rows(p, tq), tri), do_ref[r0:r0 + BLOCK, cols])
            dall = jnp.concatenate([jnp.where(lo, x[:2 * BLOCK], x[2 * BLOCK:]) for x in dkb + dvb], axis=1)
            dkeys[sb + 1] = dall[BLOCK:] if dkeys[sb + 1] is None else dkeys[sb + 1] + dall[BLOCK:]
            dkeys[sb] = dall[:BLOCK]
        for sb in range(sub):
            own = dkeys[sb + 1] + carry[...] if sb == sub - 1 else dkeys[sb + 1]
            dkv_ref[sb * BLOCK:(sb + 1) * BLOCK, :] = own
        carry[...] = dkeys[0]

    rev = lambda i: (nq - 1 - i, 0)
    return pl.pallas_call(
        functools.partial(body), name="swa_bwd", grid=(nq,),
        in_specs=[pl.BlockSpec((sub * BLOCK, d), rev), pl.BlockSpec((sub * BLOCK, d), rev),
                  pl.BlockSpec((sub * BLOCK, 2 * kvw), rev),
                  pl.BlockSpec((BLOCK, 2 * kvw), lambda i: (jnp.maximum((nq - 1 - i) * sub - 1, 0), 0)),
                  _const(sink_b.shape)],
        out_specs=[pl.BlockSpec((sub * BLOCK, d), rev), pl.BlockSpec((sub * BLOCK, 2 * kvw), rev),
                   _resident(sink_b.shape)],
        out_shape=[jax.ShapeDtypeStruct((t, d), BF16), jax.ShapeDtypeStruct((t, 2 * kvw), F32),
                   jax.ShapeDtypeStruct(sink_b.shape, F32)],
        scratch_shapes=[pltpu.VMEM((BLOCK, 2 * kvw), F32)],
        compiler_params=_cparams(("arbitrary",), VMEM_LIMIT),
    )(q, do, kv, kv, sink_b)


def oproj_post_fwd(attn, w_o, h0, gpost, gffn):
    t, d = h0.shape
    tm = _tile_rows(t)

    def body(at_ref, w_ref, h0_ref, gpost_ref, gffn_ref, m_ref, h1_ref, a_ref):
        m = _dot(at_ref[...], w_ref[...])
        m_ref[...] = m.astype(BF16)
        h1 = h0_ref[...] + m * _rms_r(m) * gpost_ref[...]
        h1_ref[...] = h1
        a_ref[...] = (h1 * _rms_r(h1) * gffn_ref[...]).astype(BF16)

    return pl.pallas_call(
        functools.partial(body), name="oproj_post_fwd", grid=(t // tm,),
        in_specs=[_rows(tm, d), _const(w_o.shape), _rows(tm, d), _const((1, d)), _const((1, d))],
        out_specs=[_rows(tm, d)] * 3,
        out_shape=[jax.ShapeDtypeStruct((t, d), BF16), jax.ShapeDtypeStruct((t, d), F32),
                   jax.ShapeDtypeStruct((t, d), BF16)],
        compiler_params=_cparams(("parallel",), VMEM_LIMIT),
    )(attn, w_o, h0, gpost, gffn)


def oproj_post_bwd(dh2, da, h1, m, w_o, gpost, gffn):
    t, d = h1.shape
    tm = _tile_rows(t)

    def body(dh2_ref, da_ref, h1_ref, m_ref, w_ref, gpost_ref, gffn_ref, dh1_ref, dm_ref, dat_ref, gacc_ref):
        _acc_init(gacc_ref)
        h1v, mv = h1_ref[...], m_ref[...].astype(F32)
        dh1_n, dgffn = _rms_bwd(h1v, _rms_r(h1v), gffn_ref[...], da_ref[...])
        dh1 = dh2_ref[...] + dh1_n
        dm, dgpost = _rms_bwd(mv, _rms_r(mv), gpost_ref[...], dh1)
        dmb = dm.astype(BF16)
        dh1_ref[...] = dh1
        dm_ref[...] = dmb
        dat_ref[...] = _dot_nt(dmb, w_ref[...]).astype(BF16)
        gacc_ref[0:1, :] += dgpost
        gacc_ref[1:2, :] += dgffn

    return pl.pallas_call(
        functools.partial(body), name="oproj_post_bwd", grid=(t // tm,),
        in_specs=[_rows(tm, d)] * 4 + [_const(w_o.shape), _const((1, d)), _const((1, d))],
        out_specs=[_rows(tm, d)] * 3 + [_resident((8, d))],
        out_shape=[jax.ShapeDtypeStruct((t, d), F32), jax.ShapeDtypeStruct((t, d), BF16),
                   jax.ShapeDtypeStruct((t, d), BF16), jax.ShapeDtypeStruct((8, d), F32)],
        compiler_params=_cparams(("arbitrary",), VMEM_LIMIT),
    )(dh2, da, h1, m, w_o, gpost, gffn)


def _my_place():
    return lax.axis_index("x"), lax.axis_index("y"), lax.axis_index("c")


def _block_index(px, py, pc):
    return 4 * px + 2 * py + pc


def allgather_pieces(shards, name):
    np_ = len(shards)

    def body(*refs):
        in_refs, out_refs = refs[:np_], refs[np_:2 * np_]
        send_sems, recv_sems, local_sems = refs[2 * np_:]
        x, y, c = _my_place()
        me, sibling = (x, y, c), (x, y, 1 - c)
        chips = [(1 - x, y), (x, 1 - y), (1 - x, 1 - y)]

        def rows(p, place):
            r = in_refs[p].shape[0]
            return out_refs[p].at[pl.ds(_block_index(*place) * r, r), :]

        def copy(p, k, block, to, src=None):
            return pltpu.make_async_remote_copy(
                src_ref=rows(p, block) if src is None else src, dst_ref=rows(p, block),
                send_sem=send_sems.at[p, k], recv_sem=recv_sems.at[p, k], device_id=to, device_id_type=MESH)

        mine = [pltpu.make_async_copy(in_refs[p], rows(p, me), local_sems.at[p]) for p in range(np_)]
        first, passed = [], []
        for p in range(np_):
            mine[p].start()
            first.append(copy(p, 0, me, sibling, src=in_refs[p]))
            first += [copy(p, 1 + j, me, (*chip, c), src=in_refs[p]) for j, chip in enumerate(chips)]
        for cp in first:
            cp.start()
        for p in range(np_):
            for j, chip in enumerate(chips):
                copy(p, 1 + j, (*chip, c), me).wait_recv()
                fwd = copy(p, 4 + j, (*chip, c), sibling)
                fwd.start()
                passed.append(fwd)
        for p in range(np_):
            copy(p, 0, sibling, me).wait_recv()
            for j, chip in enumerate(chips):
                copy(p, 4 + j, (*chip, 1 - c), me).wait_recv()
        for cp in first + passed:
            cp.wait_send()
        for cp in mine:
            cp.wait()

    return pl.pallas_call(
        functools.partial(body), name=name,
        in_specs=[ANY] * np_, out_specs=[ANY] * np_,
        out_shape=[jax.ShapeDtypeStruct((N_DEV * s.shape[0], s.shape[1]), s.dtype) for s in shards],
        scratch_shapes=[pltpu.SemaphoreType.DMA((np_, 7)), pltpu.SemaphoreType.DMA((np_, 7)),
                        pltpu.SemaphoreType.DMA((np_,))],
    )(*shards)


def _peers():
    x, y, c = _my_place()
    flips = [(fx, fy, fc) for fx in (0, 1) for fy in (0, 1) for fc in (0, 1)][1:]
    return [(1 - x if fx else x, 1 - y if fy else y, 1 - c if fc else c) for fx, fy, fc in flips]


HBM = pl.BlockSpec(memory_space=pltpu.HBM)
SEM = pl.BlockSpec(memory_space=pltpu.SEMAPHORE)


def _exchange_windows(scatter, src_ref, land_ref, my_block, peer_block):
    if scatter:
        r = land_ref.shape[1]
        return src_ref.at[pl.ds(peer_block * r, r), :], land_ref.at[my_block], land_ref.at[peer_block]
    r = src_ref.shape[0]
    return src_ref, land_ref.at[pl.ds(my_block * r, r), :], land_ref.at[pl.ds(peer_block * r, r), :]


def _own_copy(scatter, src_ref, land_ref, my_block, sem):
    if scatter:
        r = land_ref.shape[1]
        return pltpu.make_async_copy(src_ref.at[pl.ds(my_block * r, r), :], land_ref.at[my_block], sem)
    r = src_ref.shape[0]
    return pltpu.make_async_copy(src_ref, land_ref.at[pl.ds(my_block * r, r), :], sem)


def exchange_start(srcs, lands, after, scatter, name):
    np_ = len(srcs)

    def body(*refs):
        src_refs, land_refs = refs[:np_], refs[np_:2 * np_]
        send_sems, recv_sems, own_sems = refs[2 * np_ + 1:2 * np_ + 4]
        token = refs[-1]
        my_block = _block_index(*_my_place())
        for p in range(np_):
            _own_copy(scatter, src_refs[p], land_refs[p], my_block, own_sems.at[p]).start()
            for k, peer in enumerate(_peers()):
                src, dst, _ = _exchange_windows(scatter, src_refs[p], land_refs[p], my_block, _block_index(*peer))
                pltpu.make_async_remote_copy(src_ref=src, dst_ref=dst, send_sem=send_sems.at[7 * p + k],
                                             recv_sem=recv_sems.at[7 * p + k], device_id=peer, device_id_type=MESH).start()
        token[...] = jnp.zeros_like(token)

    hbm = lambda a: pltpu.with_memory_space_constraint(a, pltpu.HBM)
    outs = pl.pallas_call(
        functools.partial(body), name=name,
        in_specs=[HBM] * (2 * np_) + [ANY],
        out_specs=[SEM, SEM, SEM] + [HBM] * (2 * np_) + [pl.BlockSpec(memory_space=pltpu.VMEM)],
        out_shape=[pltpu.SemaphoreType.DMA((7 * np_,)), pltpu.SemaphoreType.DMA((7 * np_,)), pltpu.SemaphoreType.DMA((np_,))]
        + [pltpu.HBM(a.shape, a.dtype) for a in list(srcs) + list(lands)] + [jax.ShapeDtypeStruct((8, LANES), F32)],
        input_output_aliases={i: 3 + i for i in range(2 * np_)},
        compiler_params=pltpu.CompilerParams(has_side_effects=pltpu.SideEffectType.DATAFLOW_SIDE_EFFECTING),
    )(*[hbm(a) for a in srcs], *[hbm(a) for a in lands], after)
    return dict(sems=outs[:3], srcs=outs[3:3 + np_], lands=outs[3 + np_:3 + 2 * np_], token=outs[-1], scatter=scatter)


def exchange_wait(started, after, name):
    srcs, lands = started["srcs"], started["lands"]
    scatter = started["scatter"]
    np_ = len(srcs)

    def body(*refs):
        src_refs, land_refs = refs[:np_], refs[np_:2 * np_]
        send_sems, recv_sems, own_sems = refs[2 * np_:2 * np_ + 3]
        my_block = _block_index(*_my_place())
        for p in range(np_):
            _own_copy(scatter, src_refs[p], land_refs[p], my_block, own_sems.at[p]).wait()
            for k, peer in enumerate(_peers()):
                src, dst, arrival = _exchange_windows(scatter, src_refs[p], land_refs[p], my_block, _block_index(*peer))
                pltpu.make_async_remote_copy(src_ref=src, dst_ref=dst, send_sem=send_sems.at[7 * p + k],
                                             recv_sem=recv_sems.at[7 * p + k], device_id=peer, device_id_type=MESH).wait_send()
                pltpu.make_async_remote_copy(src_ref=src, dst_ref=arrival, send_sem=send_sems.at[7 * p + k],
                                             recv_sem=recv_sems.at[7 * p + k], device_id=peer, device_id_type=MESH).wait_recv()

    outs = pl.pallas_call(
        functools.partial(body), name=name,
        in_specs=[HBM] * (2 * np_) + [SEM, SEM, SEM, ANY],
        out_specs=[HBM] * (2 * np_),
        out_shape=[pltpu.HBM(a.shape, a.dtype) for a in list(srcs) + list(lands)],
        input_output_aliases={i: i for i in range(2 * np_)},
        compiler_params=pltpu.CompilerParams(has_side_effects=pltpu.SideEffectType.DATAFLOW_SIDE_EFFECTING),
    )(*srcs, *lands, *started["sems"], after)
    return list(outs[np_:])


def _gather_zone(shard):
    return lax.empty((N_DEV * shard.shape[0], shard.shape[1]), shard.dtype)


def _scatter_zone(full):
    return lax.empty((N_DEV, full.shape[0] // N_DEV, full.shape[1]), full.dtype)


def allreduce_small(pack):
    r, c = pack.shape

    def body(pack_ref, out_ref, gathered, send_sems, recv_sems):
        me = _my_place()
        my_block = _block_index(*me)
        peers = _peers()

        def copy(k, slot, to):
            return pltpu.make_async_remote_copy(
                src_ref=pack_ref, dst_ref=gathered.at[slot], send_sem=send_sems.at[k], recv_sem=recv_sems.at[k],
                device_id=to, device_id_type=MESH)

        sends = [copy(k, my_block, peer) for k, peer in enumerate(peers)]
        for cp in sends:
            cp.start()
        gathered[my_block] = pack_ref[...]
        for k, peer in enumerate(peers):
            copy(k, _block_index(*peer), peer).wait_recv()
        for cp in sends:
            cp.wait_send()
        total = gathered[0]
        for j in range(1, N_DEV):
            total = total + gathered[j]
        out_ref[...] = total

    return pl.pallas_call(
        functools.partial(body), name="allreduce_small",
        in_specs=[pl.BlockSpec(memory_space=pltpu.VMEM)], out_specs=pl.BlockSpec(memory_space=pltpu.VMEM),
        out_shape=jax.ShapeDtypeStruct((r, c), F32),
        scratch_shapes=[pltpu.VMEM((N_DEV, r, c), F32), pltpu.SemaphoreType.DMA((7,)), pltpu.SemaphoreType.DMA((7,))],
    )(pack)


def sum_parts(parts):
    n, r, c = parts.shape
    br = 256 if r % 256 == 0 else r

    def body(p_ref, g_ref):
        g = p_ref[0].astype(F32)
        for j in range(1, n):
            g = g + p_ref[j].astype(F32)
        g_ref[...] = g

    return pl.pallas_call(
        functools.partial(body), name="sum_parts", grid=(r // br,),
        in_specs=[pl.BlockSpec((n, br, c), lambda i: (0, i, 0))], out_specs=_rows(br, c),
        out_shape=jax.ShapeDtypeStruct((r, c), F32),
        compiler_params=_cparams(("parallel",)),
    )(parts)


def adamw(w, m, v, parts):
    r, c = w.shape
    n = parts.shape[0]
    br = 256 if r % 256 == 0 else r

    def body(w_ref, m_ref, v_ref, p_ref, g_ref, d_ref, nm_ref, nv_ref):
        g = p_ref[0].astype(F32)
        for j in range(1, n):
            g = g + p_ref[j].astype(F32)
        nm = ADAM_B1 * m_ref[...] + (1.0 - ADAM_B1) * g
        nv = ADAM_B2 * v_ref[...] + (1.0 - ADAM_B2) * (g * g)
        m_hat = nm / (1.0 - ADAM_B1 ** ADAM_STEP)
        v_hat = nv / (1.0 - ADAM_B2 ** ADAM_STEP)
        g_ref[...] = g
        d_ref[...] = -ADAM_LR * (m_hat / (jnp.sqrt(v_hat) + ADAM_EPS) + ADAM_WD * w_ref[...])
        nm_ref[...] = nm
        nv_ref[...] = nv

    return pl.pallas_call(
        functools.partial(body), name="adamw", grid=(r // br,),
        in_specs=[_rows(br, c)] * 3 + [pl.BlockSpec((n, br, c), lambda i: (0, i, 0))],
        out_specs=[_rows(br, c)] * 4, out_shape=[jax.ShapeDtypeStruct((r, c), F32)] * 4,
        compiler_params=_cparams(("parallel",)),
    )(w, m, v, parts)


def _adamw_nd(w, m, v, g):
    shp = w.shape
    c = shp[-1]
    flat = lambda a: a.reshape(-1, c)
    outs = adamw(flat(w), flat(m), flat(v), flat(g)[None])
    return [o.reshape(shp) for o in outs]


def _pair_heads(a, axis, width=HEAD_DIM):
    shp = a.shape
    a = a.reshape(shp[:axis] + (2, 2, GQA, width) + shp[axis + 1:])
    return jnp.swapaxes(a, axis + 1, axis + 2).reshape(shp)


def _unpair_heads(a, axis, width=HEAD_DIM):
    shp = a.shape
    a = a.reshape(shp[:axis] + (2, GQA, 2, width) + shp[axis + 1:])
    return jnp.swapaxes(a, axis + 1, axis + 2).reshape(shp)


def _pad_rows(a, rows=8):
    return jnp.pad(a, ((0, rows - a.shape[0]), (0, 0)))


def kernel(x, p, mix_pre_g, mix_post_g, ffn_pre_g, ffn_post_g, pool_w, pool_scale, kv_norm_g, w_k, w_v, w_q, w_o, sinks, w_ff_gate, w_ff_up, w_ff_down, ple_norm_g, w_ple_gate, w_ple_proj, loss_target, m_mix_pre_g, m_mix_post_g, m_ffn_pre_g, m_ffn_post_g, m_pool_w, m_pool_scale, m_kv_norm_g, m_w_k, m_w_v, m_w_q, m_w_o, m_sinks, m_w_ff_gate, m_w_ff_up, m_w_ff_down, m_ple_norm_g, m_w_ple_gate, m_w_ple_proj, v_mix_pre_g, v_mix_post_g, v_ffn_pre_g, v_ffn_post_g, v_pool_w, v_pool_scale, v_kv_norm_g, v_w_k, v_w_v, v_w_q, v_w_o, v_sinks, v_w_ff_gate, v_w_ff_up, v_w_ff_down, v_ple_norm_g, v_w_ple_gate, v_w_ple_proj):
    depth = w_ff_gate.shape[0]
    n_a = pool_w.shape[0]
    t, d = x.shape[1], x.shape[2]
    h = x[0]
    tgt = loss_target[0]
    p_all = p.reshape(depth * t, p.shape[-1])
    my_block = _block_index(*_my_place())
    row = lambda g, i: g[i][None, :]
    bf = lambda a: a.astype(BF16)

    full, gathers = [None] * depth, {}
    start_tokens = jnp.zeros((), F32)
    for i in range(depth):
        shards = [bf(w_ff_gate[i].T), bf(w_ff_up[i].T), bf(w_ff_down[i]), bf(w_ple_gate[i]), bf(w_ple_proj[i].T)]
        if i == 0:
            pool0, scale_full = allgather_pieces([bf(pool_w[0].reshape(-1, POOL_GROUP)), _pad_rows(pool_scale)],
                                                 "allgather_pool0")
            order = pool0
        elif i < n_a:
            shards.append(bf(pool_w[i].reshape(-1, POOL_GROUP)))
        else:
            shards += [bf(_pair_heads(w_q[i - n_a], 1)), bf(w_o[i - n_a])]
            if i == n_a:
                shards.append(bf(jnp.concatenate([w_k, w_v], axis=1)))
        gathers[i] = exchange_start(shards, [_gather_zone(s) for s in shards], order, False, f"allgather_start_l{i}")
        order = gathers[i]["token"]
        start_tokens = start_tokens + order[0, 0]
    scale_full = scale_full.reshape(N_DEV, 8, -1)[:, :n_a].transpose(1, 0, 2).reshape(n_a, 1, d)

    cos, sin = _rope_tables(t)
    sink_b = [jnp.broadcast_to(_pair_heads(sinks[j][:, None], 0, 1), (N_HEADS, LANES)) for j in range(depth - n_a)]
    pool_full, wo_full = {}, {}

    saved = []
    kv = hk = None
    for i in range(depth):
        if i > 0:
            full[i] = exchange_wait(gathers[i], h, f"allgather_wait_l{i}")
        s = {"h0": h}
        if i < n_a:
            pool_full[i] = ((pool0 if i == 0 else full[i][5]).reshape(N_DEV, len(POOL_WINDOWS), -1, POOL_GROUP)
                            .transpose(1, 0, 2, 3).reshape(len(POOL_WINDOWS), POOL_GROUP, POOL_GROUP))
            gpre = row(mix_pre_g, i) + start_tokens if i == 0 else row(mix_pre_g, i)
            h1, a = pool_mix_fwd(h, gpre, pool_full[i], scale_full[i], row(mix_post_g, i), row(ffn_pre_g, i))
            if i == 0:
                full[0] = exchange_wait(gathers[0], h1, "allgather_wait_l0")
        else:
            j = i - n_a
            wo_full[i] = _pair_heads(full[i][6], 0)
            if i == n_a:
                hk, kv = proj_rope_fwd(h, kv_norm_g[None, :], full[i][7], cos, sin, N_KV_HEADS * HEAD_DIM, "kv_proj_fwd")
            hn, q = proj_rope_fwd(h, row(mix_pre_g, i), full[i][5], cos, sin, d, "q_proj_fwd")
            attn = swa_fwd(q, kv, sink_b[j])
            m, h1, a = oproj_post_fwd(attn, wo_full[i], h, row(mix_post_g, i), row(ffn_pre_g, i))
            s.update(hn=hn, q=q, attn=attn, m=m)
        wg_t, wu_t, wd, wpg, wpp_t = full[i][:5]
        f, gte, up, hdn = ffn_fwd(a, wg_t, wu_t, wd)
        s.update(h1=h1, a=a, f=f, gte=gte, up=up, hdn=hdn)
        if i < depth - 1:
            h = post_ple_fwd(h1, f, p_all, i, row(ffn_post_g, i), row(ple_norm_g, i), wpg, wpp_t)[0]
        else:
            dh, loss_rows = post_ple_fwd(h1, f, p_all, i, row(ffn_post_g, i), row(ple_norm_g, i), wpg, wpp_t, target=tgt)
        saved.append(s)

    g_mix_pre, g_mix_post, g_ffn_pre, g_ffn_post, g_ple = ([None] * depth for _ in range(5))
    g_kv = g_sinks = None
    g_scale = [None] * n_a
    landing, scatters = [None] * depth, {}
    dkv_sum = []
    scatter_token = jnp.zeros((), F32)
    for i in reversed(range(depth)):
        s = saved[i]
        wg_t, wu_t, wd, wpg, wpp_t = full[i][:5]
        dh2, df, ub, dzb, dppb, gacc = post_ple_bwd(dh, s["h1"], s["f"], p_all, i, row(ffn_post_g, i) + scatter_token,
                                                    row(ple_norm_g, i), wpg, wpp_t)
        g_ple[i], g_ffn_post[i] = gacc[0], gacc[1]
        da, dgte, dup = ffn_bwd_act(df, s["gte"], s["up"], wg_t, wu_t, wd)
        grads = [xty(dgte, s["a"]), xty(dup, s["a"]), xty(s["hdn"], df), xty(ub, dzb), xty(dppb, p_all, i)]
        early = exchange_start(grads, [_scatter_zone(g) for g in grads], dh2, True, f"reduce_scatter_start_l{i}a")
        early_token = early["token"][0, 0]
        if i < n_a:
            dh, dpw, gacc = pool_mix_bwd(s["h0"], dh2, da, row(mix_pre_g, i) + early_token, pool_full[i], scale_full[i],
                                         row(mix_post_g, i), row(ffn_pre_g, i))
            g_mix_pre[i], g_mix_post[i], g_ffn_pre[i], g_scale[i] = gacc[0], gacc[1], gacc[2], gacc[3]
            dpw = dpw.reshape(len(POOL_WINDOWS), N_DEV, -1, POOL_GROUP).transpose(1, 0, 2, 3)
            grads = [bf(dpw.reshape(-1, POOL_GROUP))]
        else:
            j = i - n_a
            dh1, dmb, dattn, gacc = oproj_post_bwd(dh2, da, s["h1"], s["m"], wo_full[i], row(mix_post_g, i) + early_token,
                                                   row(ffn_pre_g, i))
            g_mix_post[i], g_ffn_pre[i] = gacc[0], gacc[1]
            dq, dkv, dsink = swa_bwd(s["q"], kv, dattn, sink_b[j])
            dkv_sum.append(dkv)
            g_sinks = [_unpair_heads(dsink[:, 0:1], 0, 1)[:, 0]] + (g_sinks or [])
            branches = [(row(mix_pre_g, i), full[i][5], d, [dq])]
            if i == n_a:
                branches.append((kv_norm_g[None, :], full[i][7], N_KV_HEADS * HEAD_DIM, dkv_sum))
            outs = proj_rope_bwd(dh1, s["h0"], cos, sin, branches, f"proj_bwd_l{i}")
            dh, gacc = outs[0], outs[-1]
            g_mix_pre[i] = gacc[0]
            grads = [xty(s["hn"], outs[1]), _unpair_heads(xty(s["attn"], dmb), 0)]
            if i == n_a:
                g_kv = gacc[1]
                grads.append(xty(hk, outs[2]))
        late = exchange_start(grads, [_scatter_zone(g) for g in grads], dh, True, f"reduce_scatter_start_l{i}b")
        scatter_token = late["token"][0, 0]
        scatters[i] = (early, late)
    grad_x = dh[None]
    after = dh
    for i in reversed(range(depth)):
        landing[i] = (exchange_wait(scatters[i][0], after, f"reduce_scatter_wait_l{i}a")
                      + exchange_wait(scatters[i][1], after, f"reduce_scatter_wait_l{i}b"))
        after = landing[i][0]

    loss_row = jnp.sum(loss_rows, axis=0, keepdims=True)
    sink_row = jnp.pad(jnp.concatenate(g_sinks)[None, :], ((0, 0), (0, d - sinks.size)))
    stack = lambda rows_: _pad_rows(jnp.stack(rows_))
    pack = jnp.concatenate([stack(g_mix_pre), stack(g_mix_post), stack(g_ffn_pre), stack(g_ffn_post), stack(g_ple),
                            _pad_rows(g_kv[None]), stack(g_scale), _pad_rows(sink_row), _pad_rows(loss_row)], axis=0)
    tot = allreduce_small(pack)
    sec = lambda k, n: tot[8 * k:8 * k + n]
    loss = jnp.sum(tot[64])
    small = {
        "mix_pre_g": sec(0, depth), "mix_post_g": sec(1, depth), "ffn_pre_g": sec(2, depth),
        "ffn_post_g": sec(3, depth), "ple_norm_g": sec(4, depth), "kv_norm_g": tot[40],
        "pool_scale": lax.dynamic_slice_in_dim(sec(6, n_a), my_block * pool_scale.shape[1], pool_scale.shape[1], axis=1),
        "sinks": tot[56, :sinks.size].reshape(sinks.shape),
    }

    weights = dict(mix_pre_g=mix_pre_g, mix_post_g=mix_post_g, ffn_pre_g=ffn_pre_g, ffn_post_g=ffn_post_g, pool_w=pool_w, pool_scale=pool_scale, kv_norm_g=kv_norm_g, w_k=w_k, w_v=w_v, w_q=w_q, w_o=w_o, sinks=sinks, w_ff_gate=w_ff_gate, w_ff_up=w_ff_up, w_ff_down=w_ff_down, ple_norm_g=ple_norm_g, w_ple_gate=w_ple_gate, w_ple_proj=w_ple_proj)
    mom1 = dict(mix_pre_g=m_mix_pre_g, mix_post_g=m_mix_post_g, ffn_pre_g=m_ffn_pre_g, ffn_post_g=m_ffn_post_g, pool_w=m_pool_w, pool_scale=m_pool_scale, kv_norm_g=m_kv_norm_g, w_k=m_w_k, w_v=m_w_v, w_q=m_w_q, w_o=m_w_o, sinks=m_sinks, w_ff_gate=m_w_ff_gate, w_ff_up=m_w_ff_up, w_ff_down=m_w_ff_down, ple_norm_g=m_ple_norm_g, w_ple_gate=m_w_ple_gate, w_ple_proj=m_w_ple_proj)
    mom2 = dict(mix_pre_g=v_mix_pre_g, mix_post_g=v_mix_post_g, ffn_pre_g=v_ffn_pre_g, ffn_post_g=v_ffn_post_g, pool_w=v_pool_w, pool_scale=v_pool_scale, kv_norm_g=v_kv_norm_g, w_k=v_w_k, w_v=v_w_v, w_q=v_w_q, w_o=v_w_o, sinks=v_sinks, w_ff_gate=v_w_ff_gate, w_ff_up=v_w_ff_up, w_ff_down=v_w_ff_down, ple_norm_g=v_ple_norm_g, w_ple_gate=v_w_ple_gate, w_ple_proj=v_w_ple_proj)

    def land(i, k):
        return sum_parts(landing[i][k])

    gw = dict(small)
    gw["kv_norm_g"] = small["kv_norm_g"]
    gw["w_ff_gate"] = jnp.stack([land(i, 0).T for i in range(depth)])
    gw["w_ff_up"] = jnp.stack([land(i, 1).T for i in range(depth)])
    gw["w_ff_down"] = jnp.stack([land(i, 2) for i in range(depth)])
    gw["w_ple_gate"] = jnp.stack([land(i, 3) for i in range(depth)])
    gw["w_ple_proj"] = jnp.stack([land(i, 4).T for i in range(depth)])
    gw["pool_w"] = jnp.stack([land(i, 5).reshape(pool_w.shape[1:]) for i in range(n_a)])
    gw["w_q"] = jnp.stack([_unpair_heads(land(i, 5), 1) for i in range(n_a, depth)])
    gw["w_o"] = jnp.stack([land(i, 6) for i in range(n_a, depth)])
    gkv = land(n_a, 7)
    gw["w_k"], gw["w_v"] = gkv[:, :w_k.shape[1]], gkv[:, w_k.shape[1]:]

    order = ["mix_pre_g", "mix_post_g", "ffn_pre_g", "ffn_post_g", "pool_w", "pool_scale", "kv_norm_g", "w_k", "w_v",
             "w_q", "w_o", "sinks", "w_ff_gate", "w_ff_up", "w_ff_down", "ple_norm_g", "w_ple_gate", "w_ple_proj"]
    g_out, d_out, m_out, v_out = [], [], [], []
    for nme in order:
        w = weights[nme]
        as2d = (lambda a: a[None, :]) if w.ndim == 1 else (lambda a: a)
        g, dl, nm, nv = _adamw_nd(as2d(w), as2d(mom1[nme]), as2d(mom2[nme]), as2d(gw[nme]))
        for lst, val in ((g_out, g), (d_out, dl), (m_out, nm), (v_out, nv)):
            lst.append(val.reshape(w.shape))
    return (loss, grad_x, *g_out, *d_out, *m_out, *v_out)
```

```python
import functools

import jax
import jax.numpy as jnp
from jax import lax
from jax.experimental import pallas as pl
from jax.experimental.pallas import tpu as pltpu

F32 = jnp.float32
BF16 = jnp.bfloat16

N_DEV = 8
HEAD_DIM = 64
N_HEADS = 16
N_KV_HEADS = 4
GQA = N_HEADS // N_KV_HEADS
BLOCK = 128
POOL_WINDOWS = (2, 4, 8, 16)
POOL_GROUP = 256
HALO = 16
ROPE_THETA = 10000.0
RMS_EPS = 1e-6
NEG_INF = -1e30
LANES = 128
ATTN_SUB = 8
XTY_ROWS = 2048
FFN_CHUNK = 768
VMEM_LIMIT = 56 * 1024 * 1024

ADAM_LR = 0.001
ADAM_B1 = 0.9
ADAM_B2 = 0.999
ADAM_EPS = 1e-08
ADAM_WD = 0.01
ADAM_STEP = 10

MESH = pl.DeviceIdType.MESH
ANY = pl.BlockSpec(memory_space=pl.ANY)

NT_DIMS = (((1,), (1,)), ((), ()))
TN_DIMS = (((0,), (0,)), ((), ()))


def _cparams(sem=None, vmem=None):
    kw = {}
    if sem is not None:
        kw["dimension_semantics"] = sem
    if vmem is not None:
        kw["vmem_limit_bytes"] = vmem
    return pltpu.CompilerParams(**kw)


def _rows(tm, n, first=0):
    return pl.BlockSpec((tm, n), lambda i: (i + first, 0))


def _rows_rev(tm, n, nt):
    return pl.BlockSpec((tm, n), lambda i: (nt - 1 - i, 0))


def _const(shape):
    nd = len(shape)
    return pl.BlockSpec(shape, lambda *_: (0,) * nd, pipeline_mode=pl.Buffered(1))


def _resident(shape):
    nd = len(shape)
    return pl.BlockSpec(shape, lambda *_: (0,) * nd)


def _tile_rows(t):
    return 512 if t % 512 == 0 else 128


def _dot(a, b):
    return jnp.dot(a, b, preferred_element_type=F32)


def _dot_nt(a, b):
    return lax.dot_general(a, b, NT_DIMS, preferred_element_type=F32)


def _dot_tn(a, b):
    return lax.dot_general(a, b, TN_DIMS, preferred_element_type=F32)


def _rms_r(x):
    return lax.rsqrt(jnp.mean(x * x, axis=-1, keepdims=True) + RMS_EPS)


def _rms_bwd(x, r, g, dy):
    gy = dy * g
    dx = r * gy - x * (r * r * r * jnp.mean(gy * x, axis=-1, keepdims=True))
    dg = jnp.sum(dy * (x * r), axis=0, keepdims=True)
    return dx, dg


def _sigmoid(x):
    return jax.nn.sigmoid(x)


def _rope_tables(t):
    inv = 1.0 / (ROPE_THETA ** (jnp.arange(0, HEAD_DIM, 2, dtype=F32) / HEAD_DIM))
    ang = jnp.arange(t, dtype=F32)[:, None] * jnp.tile(inv, 2 * LANES // HEAD_DIM)[None, :]
    sign = jnp.tile(jnp.repeat(jnp.array([-1.0, 1.0], F32), HEAD_DIM // 2), LANES // HEAD_DIM)
    return jnp.cos(ang), jnp.sin(ang) * sign[None, :]


def _swap_halves(x):
    n = x.shape[1]
    lane = lax.broadcasted_iota(jnp.int32, x.shape, 1)
    first = (lane % HEAD_DIM) < (HEAD_DIM // 2)
    return jnp.where(first, pltpu.roll(x, n - HEAD_DIM // 2, 1), pltpu.roll(x, HEAD_DIM // 2, 1))


def _rope(x, cos, sin):
    reps = x.shape[1] // LANES
    return x * jnp.tile(cos, (1, reps)) + _swap_halves(x) * jnp.tile(sin, (1, reps))


def _unrope(dy, cos, sin):
    reps = dy.shape[1] // LANES
    return dy * jnp.tile(cos, (1, reps)) + _swap_halves(dy * jnp.tile(sin, (1, reps)))


def _acc_init(acc_ref):
    @pl.when(pl.program_id(0) == 0)
    def _():
        acc_ref[...] = jnp.zeros_like(acc_ref)


def _window_sums(ext, tm, forward):
    n = tm + HALO
    out = []
    for g, w in enumerate(POOL_WINDOWS):
        s = ext[:, g * POOL_GROUP:(g + 1) * POOL_GROUP]
        k = 1
        while k < w:
            s = s + pltpu.roll(s, k if forward else n - k, 0)
            k *= 2
        out.append(s[HALO:, :] if forward else s[:tm, :])
    return out


def _pool_inv_counts(tile, tm):
    t = tile * tm + lax.broadcasted_iota(jnp.int32, (tm, 1), 0)
    return [1.0 / jnp.minimum(t + 1, w).astype(F32) for w in POOL_WINDOWS]


def _pool_mix(hn, ext, inv_cnts, pw_ref, scale, tm):
    sums = _window_sums(ext, tm, True)
    pooled, ys = [], []
    for g in range(len(POOL_WINDOWS)):
        pg = (sums[g] * inv_cnts[g] - hn[:, g * POOL_GROUP:(g + 1) * POOL_GROUP]).astype(BF16)
        pooled.append(pg)
        ys.append(_dot(pg, pw_ref[g]))
    y = jnp.concatenate(ys, axis=1)
    return pooled, y, y * scale


def pool_mix_fwd(h0, gpre, pool_w, scale, gpost, gffn):
    t, d = h0.shape
    tm = _tile_rows(t)

    def body(h_ref, gpre_ref, pw_ref, scale_ref, gpost_ref, gffn_ref, h1_ref, a_ref, carry):
        i = pl.program_id(0)

        @pl.when(i == 0)
        def _():
            carry[...] = jnp.zeros_like(carry)

        x = h_ref[...]
        hn = x * _rms_r(x) * gpre_ref[...]
        ext = jnp.concatenate([carry[...], hn], axis=0)
        carry[...] = hn[tm - HALO:, :]
        _, _, m = _pool_mix(hn, ext, _pool_inv_counts(i, tm), pw_ref, scale_ref[...], tm)
        h1 = x + m * _rms_r(m) * gpost_ref[...]
        h1_ref[...] = h1
        a_ref[...] = (h1 * _rms_r(h1) * gffn_ref[...]).astype(BF16)

    return pl.pallas_call(
        functools.partial(body), name="pool_mix_fwd", grid=(t // tm,),
        in_specs=[_rows(tm, d), _const((1, d)), _const(pool_w.shape), _const((1, d)), _const((1, d)), _const((1, d))],
        out_specs=[_rows(tm, d), _rows(tm, d)],
        out_shape=[jax.ShapeDtypeStruct((t, d), F32), jax.ShapeDtypeStruct((t, d), BF16)],
        scratch_shapes=[pltpu.VMEM((HALO, d), F32)],
        compiler_params=_cparams(("arbitrary",), VMEM_LIMIT),
    )(h0, gpre, pool_w, scale, gpost, gffn)


def pool_mix_bwd(h0, dh2, da, gpre, pool_w, scale, gpost, gffn):
    t, d = h0.shape
    tm = _tile_rows(t)
    nt = t // tm
    hb = tm // HALO

    def body(h_ref, halo_ref, dh2_ref, da_ref, gpre_ref, pw_ref, scale_ref, gpost_ref, gffn_ref,
             dh0_ref, dpw_ref, gacc_ref, carry):
        i = pl.program_id(0)
        tile = nt - 1 - i
        _acc_init(gacc_ref)
        _acc_init(dpw_ref)

        @pl.when(i == 0)
        def _():
            carry[...] = jnp.zeros_like(carry)

        x = h_ref[...]
        gpre_v, scale_v, gpost_v, gffn_v = gpre_ref[...], scale_ref[...], gpost_ref[...], gffn_ref[...]
        r0 = _rms_r(x)
        hn = x * r0 * gpre_v
        xh = halo_ref[...]
        hn_halo = jnp.where(tile > 0, xh * _rms_r(xh) * gpre_v, 0.0)
        ext = jnp.concatenate([hn_halo, hn], axis=0)
        inv_cnts = _pool_inv_counts(tile, tm)
        pooled, y, m = _pool_mix(hn, ext, inv_cnts, pw_ref, scale_v, tm)
        rm = _rms_r(m)
        h1 = x + m * rm * gpost_v
        dh1_n, dgffn = _rms_bwd(h1, _rms_r(h1), gffn_v, da_ref[...])
        dh1 = dh2_ref[...] + dh1_n
        dm, dgpost = _rms_bwd(m, rm, gpost_v, dh1)
        dscale = jnp.sum(dm * y, axis=0, keepdims=True)
        dy = (dm * scale_v).astype(BF16)
        dpn = []
        for g in range(len(POOL_WINDOWS)):
            dyg = dy[:, g * POOL_GROUP:(g + 1) * POOL_GROUP]
            dpw_ref[g] += _dot_tn(pooled[g], dyg)
            dpn.append(_dot_nt(dyg, pw_ref[g]))
        dpooled = jnp.concatenate(dpn, axis=1)
        dpc = jnp.concatenate([dpn[g] * inv_cnts[g] for g in range(len(POOL_WINDOWS))], axis=1)
        ext2 = jnp.concatenate([dpc, carry[...]], axis=0)
        carry[...] = dpc[:HALO, :]
        dhn = jnp.concatenate(_window_sums(ext2, tm, False), axis=1) - dpooled
        dh0_n, dgpre = _rms_bwd(x, r0, gpre_v, dhn)
        dh0_ref[...] = dh1 + dh0_n
        gacc_ref[0:1, :] += dgpre
        gacc_ref[1:2, :] += dgpost
        gacc_ref[2:3, :] += dgffn
        gacc_ref[3:4, :] += dscale

    return pl.pallas_call(
        functools.partial(body), name="pool_mix_bwd", grid=(nt,),
        in_specs=[_rows_rev(tm, d, nt),
                  pl.BlockSpec((HALO, d), lambda i: (jnp.maximum((nt - 1 - i) * hb - 1, 0), 0)),
                  _rows_rev(tm, d, nt), _rows_rev(tm, d, nt),
                  _const((1, d)), _const(pool_w.shape), _const((1, d)), _const((1, d)), _const((1, d))],
        out_specs=[_rows_rev(tm, d, nt), _resident(pool_w.shape), _resident((8, d))],
        out_shape=[jax.ShapeDtypeStruct((t, d), F32), jax.ShapeDtypeStruct(pool_w.shape, F32),
                   jax.ShapeDtypeStruct((8, d), F32)],
        scratch_shapes=[pltpu.VMEM((HALO, d), F32)],
        compiler_params=_cparams(("arbitrary",), VMEM_LIMIT),
    )(h0, h0, dh2, da, gpre, pool_w, scale, gpost, gffn)


def _ffn_chunks(f):
    return [(c, min(c + FFN_CHUNK, f)) for c in range(0, f, FFN_CHUNK)]


def ffn_fwd(a, wg_t, wu_t, wd):
    t, d = a.shape
    f = wd.shape[0]
    tm = _tile_rows(t)

    def body(a_ref, wg_ref, wu_ref, wd_ref, f_ref, gte_ref, up_ref, hdn_ref):
        av = a_ref[...]
        acc = jnp.zeros((tm, d), F32)
        for c0, c1 in _ffn_chunks(f):
            gte = _dot_nt(av, wg_ref[c0:c1, :])
            up = _dot_nt(av, wu_ref[c0:c1, :])
            gte_ref[:, c0:c1] = gte.astype(BF16)
            up_ref[:, c0:c1] = up.astype(BF16)
            hdn = (gte * _sigmoid(gte) * up).astype(BF16)
            hdn_ref[:, c0:c1] = hdn
            acc = acc + _dot(hdn, wd_ref[c0:c1, :])
        f_ref[...] = acc.astype(BF16)

    return pl.pallas_call(
        functools.partial(body), name="ffn_fwd", grid=(t // tm,),
        in_specs=[_rows(tm, d), _const((f, d)), _const((f, d)), _const((f, d))],
        out_specs=[_rows(tm, d), _rows(tm, f), _rows(tm, f), _rows(tm, f)],
        out_shape=[jax.ShapeDtypeStruct((t, d), BF16)] + [jax.ShapeDtypeStruct((t, f), BF16)] * 3,
        compiler_params=_cparams(("parallel",), VMEM_LIMIT),
    )(a, wg_t, wu_t, wd)


def ffn_bwd_act(df, gte, up, wg_t, wu_t, wd):
    t, d = df.shape
    f = wd.shape[0]
    tm = _tile_rows(t)

    def body(df_ref, gte_ref, up_ref, wg_ref, wu_ref, wd_ref, da_ref, dgte_ref, dup_ref):
        dfv = df_ref[...]
        chunks = _ffn_chunks(f)
        half = chunks[len(chunks) // 2][0]
        acc = None
        for c0, c1 in chunks:
            g = gte_ref[:, c0:c1].astype(F32)
            u = up_ref[:, c0:c1].astype(F32)
            sg = _sigmoid(g)
            sl = g * sg
            dh = _dot_nt(dfv, wd_ref[c0:c1, :])
            dup_ref[:, c0:c1] = (dh * sl).astype(BF16)
            dgte_ref[:, c0:c1] = (dh * u * (sg * (1.0 + g * (1.0 - sg)))).astype(BF16)
            if c1 == half:
                acc = _dot(dgte_ref[:, :half], wg_ref[:half, :]) + _dot(dup_ref[:, :half], wu_ref[:half, :])
        da_ref[...] = acc + _dot(dgte_ref[:, half:], wg_ref[half:, :]) + _dot(dup_ref[:, half:], wu_ref[half:, :])

    return pl.pallas_call(
        functools.partial(body), name="ffn_bwd_act", grid=(t // tm,),
        in_specs=[_rows(tm, d), _rows(tm, f), _rows(tm, f), _const((f, d)), _const((f, d)), _const((f, d))],
        out_specs=[_rows(tm, d), _rows(tm, f), _rows(tm, f)],
        out_shape=[jax.ShapeDtypeStruct((t, d), F32)] + [jax.ShapeDtypeStruct((t, f), BF16)] * 2,
        compiler_params=_cparams(("parallel",), VMEM_LIMIT),
    )(df, gte, up, wg_t, wu_t, wd)


def xty(x, y, y_part=0):
    t, nx = x.shape
    ny = y.shape[1]
    tk = XTY_ROWS if t % XTY_ROWS == 0 else _tile_rows(t)
    bn = nx // 2 if nx > 1024 else nx
    nk = t // tk

    def body(x_ref, y_ref, o_ref, acc):
        k = pl.program_id(1)

        @pl.when(k == 0)
        def _():
            acc[...] = jnp.zeros_like(acc)

        acc[...] += _dot_tn(x_ref[...].astype(BF16), y_ref[...].astype(BF16))

        @pl.when(k == nk - 1)
        def _():
            o_ref[...] = acc[...].astype(BF16)

    return pl.pallas_call(
        functools.partial(body), name="xty", grid=(nx // bn, nk),
        in_specs=[pl.BlockSpec((tk, bn), lambda j, k: (k, j)),
                  pl.BlockSpec((tk, ny), lambda j, k: (k + y_part * nk, 0))],
        out_specs=pl.BlockSpec((bn, ny), lambda j, k: (j, 0)),
        out_shape=jax.ShapeDtypeStruct((nx, ny), BF16),
        scratch_shapes=[pltpu.VMEM((bn, ny), F32)],
        compiler_params=_cparams(("parallel", "arbitrary"), VMEM_LIMIT),
    )(x, y)


def _ple_fwd_tile(h1, f, p, gpost, gple, wpg_ref, wpp_ref):
    rf = _rms_r(f)
    h2 = h1 + f * rf * gpost
    r2 = _rms_r(h2)
    ub = (h2 * r2 * gple).astype(BF16)
    gate = _sigmoid(_dot(ub, wpg_ref[...]))
    pp = _dot_nt(p.astype(BF16), wpp_ref[...])
    return rf, h2, r2, ub, gate, pp


def post_ple_fwd(h1, f, p, layer, gpost, gple, wpg, wpp_t, target=None):
    t, d = h1.shape
    pd = p.shape[1]
    tm = _tile_rows(t)
    with_loss = target is not None

    def body(*refs):
        if with_loss:
            h1_ref, f_ref, p_ref, gpost_ref, gple_ref, wpg_ref, wpp_ref, tgt_ref, out_ref, loss_ref = refs
        else:
            h1_ref, f_ref, p_ref, gpost_ref, gple_ref, wpg_ref, wpp_ref, out_ref = refs
        _, h2, _, _, gate, pp = _ple_fwd_tile(h1_ref[...], f_ref[...].astype(F32), p_ref[...], gpost_ref[...],
                                              gple_ref[...], wpg_ref, wpp_ref)
        h3 = h2 + pp * gate
        if with_loss:
            err = h3 - tgt_ref[...]
            out_ref[...] = err * (1.0 / d)
            colsum = jnp.sum(err * err, axis=0, keepdims=True) * (0.5 / d)
            loss_ref[...] = jnp.broadcast_to(colsum, (8, d)) * (lax.broadcasted_iota(jnp.int32, (8, d), 0) == 0)
        else:
            out_ref[...] = h3

    in_specs = [_rows(tm, d), _rows(tm, d), _rows(tm, pd, layer * (t // tm)), _const((1, d)), _const((1, d)),
                _const(wpg.shape), _const(wpp_t.shape)]
    out_specs = [_rows(tm, d)]
    out_shape = [jax.ShapeDtypeStruct((t, d), F32)]
    args = [h1, f, p, gpost, gple, wpg, wpp_t]
    if with_loss:
        in_specs.append(_rows(tm, d))
        out_specs.append(_rows(8, d))
        out_shape.append(jax.ShapeDtypeStruct((t // tm * 8, d), F32))
        args.append(target)
    return pl.pallas_call(
        functools.partial(body), name="post_ple_loss" if with_loss else "post_ple_fwd", grid=(t // tm,),
        in_specs=in_specs, out_specs=out_specs, out_shape=out_shape,
        compiler_params=_cparams(("parallel",), VMEM_LIMIT),
    )(*args)


def post_ple_bwd(dh3, h1, f, p, layer, gpost, gple, wpg, wpp_t):
    t, d = h1.shape
    pd = p.shape[1]
    tm = _tile_rows(t)

    def body(dh3_ref, h1_ref, f_ref, p_ref, gpost_ref, gple_ref, wpg_ref, wpp_ref,
             dh2_ref, df_ref, u_ref, dz_ref, dpp_ref, gacc_ref):
        _acc_init(gacc_ref)
        fv = f_ref[...].astype(F32)
        gpost_v, gple_v = gpost_ref[...], gple_ref[...]
        rf, h2, r2, ub, gate, pp = _ple_fwd_tile(h1_ref[...], fv, p_ref[...], gpost_v, gple_v, wpg_ref, wpp_ref)
        dh3v = dh3_ref[...]
        dpp_ref[...] = (dh3v * gate).astype(BF16)
        dz = (dh3v * pp * gate * (1.0 - gate)).astype(BF16)
        dz_ref[...] = dz
        u_ref[...] = ub
        du = _dot_nt(dz, wpg_ref[...])
        dh2_n, dgple = _rms_bwd(h2, r2, gple_v, du)
        dh2 = dh3v + dh2_n
        df, dgpost = _rms_bwd(fv, rf, gpost_v, dh2)
        dh2_ref[...] = dh2
        df_ref[...] = df.astype(BF16)
        gacc_ref[0:1, :] += dgple
        gacc_ref[1:2, :] += dgpost

    return pl.pallas_call(
        functools.partial(body), name="post_ple_bwd", grid=(t // tm,),
        in_specs=[_rows(tm, d), _rows(tm, d), _rows(tm, d), _rows(tm, pd, layer * (t // tm)), _const((1, d)),
                  _const((1, d)), _const(wpg.shape), _const(wpp_t.shape)],
        out_specs=[_rows(tm, d)] * 5 + [_resident((8, d))],
        out_shape=[jax.ShapeDtypeStruct((t, d), F32)] + [jax.ShapeDtypeStruct((t, d), BF16)] * 4
        + [jax.ShapeDtypeStruct((8, d), F32)],
        compiler_params=_cparams(("arbitrary",), VMEM_LIMIT),
    )(dh3, h1, f, p, gpost, gple, wpg, wpp_t)


def proj_rope_fwd(h, gain, w, cos, sin, n_rope, name):
    t, d = h.shape
    n = w.shape[1]
    tm = _tile_rows(t)

    def body(h_ref, g_ref, w_ref, cos_ref, sin_ref, hn_ref, y_ref):
        x = h_ref[...]
        hn = (x * _rms_r(x) * g_ref[...]).astype(BF16)
        hn_ref[...] = hn
        y = _dot(hn, w_ref[...])
        y_ref[:, :n_rope] = _rope(y[:, :n_rope], cos_ref[...], sin_ref[...]).astype(BF16)
        if n_rope < n:
            y_ref[:, n_rope:] = y[:, n_rope:].astype(BF16)

    return pl.pallas_call(
        functools.partial(body), name=name, grid=(t // tm,),
        in_specs=[_rows(tm, d), _const((1, d)), _const(w.shape), _rows(tm, LANES), _rows(tm, LANES)],
        out_specs=[_rows(tm, d), _rows(tm, n)],
        out_shape=[jax.ShapeDtypeStruct((t, d), BF16), jax.ShapeDtypeStruct((t, n), BF16)],
        compiler_params=_cparams(("parallel",), VMEM_LIMIT),
    )(h, gain, w, cos, sin)


def proj_rope_bwd(dh1, h0, cos, sin, branches, name):
    t, d = h0.shape
    tm = _tile_rows(t)
    nb = len(branches)
    n_cot = [len(b[3]) for b in branches]

    def body(*refs):
        dh1_ref, h0_ref, cos_ref, sin_ref = refs[:4]
        pos = 4
        br_refs = []
        for b in range(nb):
            br_refs.append((refs[pos], refs[pos + 1], refs[pos + 2:pos + 2 + n_cot[b]]))
            pos += 2 + n_cot[b]
        dh0_ref = refs[pos]
        dpre_refs = refs[pos + 1:pos + 1 + nb]
        gacc_ref = refs[pos + 1 + nb]
        _acc_init(gacc_ref)
        x = h0_ref[...]
        r0 = _rms_r(x)
        dh = dh1_ref[...]
        for b in range(nb):
            g_ref, w_ref, cot_refs = br_refs[b]
            n_rope = branches[b][2]
            dy = cot_refs[0][...].astype(F32)
            for c_ref in cot_refs[1:]:
                dy = dy + c_ref[...].astype(F32)
            n = dy.shape[1]
            dpre_refs[b][:, :n_rope] = _unrope(dy[:, :n_rope], cos_ref[...], sin_ref[...]).astype(BF16)
            if n_rope < n:
                dpre_refs[b][:, n_rope:] = dy[:, n_rope:].astype(BF16)
            dhn = _dot_nt(dpre_refs[b][...], w_ref[...])
            dx, dg = _rms_bwd(x, r0, g_ref[...], dhn)
            dh = dh + dx
            gacc_ref[b:b + 1, :] += dg
        dh0_ref[...] = dh

    in_specs = [_rows(tm, d), _rows(tm, d), _rows(tm, LANES), _rows(tm, LANES)]
    args = [dh1, h0, cos, sin]
    out_specs = [_rows(tm, d)]
    out_shape = [jax.ShapeDtypeStruct((t, d), F32)]
    for gain, w, _, cots in branches:
        n = w.shape[1]
        in_specs += [_const((1, d)), _const(w.shape)] + [_rows(tm, n)] * len(cots)
        args += [gain, w] + list(cots)
        out_specs.append(_rows(tm, n))
        out_shape.append(jax.ShapeDtypeStruct((t, n), BF16))
    out_specs.append(_resident((8, d)))
    out_shape.append(jax.ShapeDtypeStruct((8, d), F32))
    return pl.pallas_call(
        functools.partial(body), name=name, grid=(t // tm,),
        in_specs=in_specs, out_specs=out_specs, out_shape=out_shape,
        compiler_params=_cparams(("arbitrary",), VMEM_LIMIT),
    )(*args)


def _tri():
    row = lax.broadcasted_iota(jnp.int32, (BLOCK, BLOCK), 0)
    col = lax.broadcasted_iota(jnp.int32, (BLOCK, BLOCK), 1)
    return col <= row


def _block_diag(x):
    lo = lax.broadcasted_iota(jnp.int32, x.shape, 1) < HEAD_DIM
    zero = jnp.zeros_like(x)
    return jnp.concatenate([jnp.where(lo, x, zero), jnp.where(lo, zero, x)], axis=0)


def _dense(x, tri):
    return (jnp.where(tri, x[:, BLOCK:2 * BLOCK], x[:, :BLOCK]),
            jnp.where(tri, x[:, 3 * BLOCK:], x[:, 2 * BLOCK:3 * BLOCK]))


def _banded(xa, xb, tri):
    zero = jnp.zeros_like(xa)
    return jnp.concatenate([jnp.where(tri, zero, xa), jnp.where(tri, xa, zero),
                            jnp.where(tri, zero, xb), jnp.where(tri, xb, zero)], axis=1).astype(BF16)


def _softmax_sink(s, sink):
    mx = jnp.maximum(jnp.max(s, axis=1, keepdims=True), sink)
    e = jnp.exp(s - mx)
    es = jnp.exp(sink - mx)
    inv = 1.0 / (jnp.sum(e, axis=1, keepdims=True) + es)
    return e * inv, es * inv


def _sink_column(sink_ref):
    return jnp.concatenate([jnp.broadcast_to(sink_ref[h:h + 1, 0:1], (BLOCK, 1)) for h in range(N_HEADS)], axis=0)


def _kv_block_diag(band, kvw):
    n_lt = kvw // LANES
    return ([_block_diag(band[:, lt * LANES:(lt + 1) * LANES]) for lt in range(n_lt)],
            [_block_diag(band[:, kvw + lt * LANES:kvw + (lt + 1) * LANES]) for lt in range(n_lt)])


def _all_probs(q_ref, r0, kbd, tri, n, sink_ref):
    dense = []
    for tq in range(N_HEADS // 2):
        s = _dot_nt(q_ref[r0:r0 + BLOCK, tq * LANES:(tq + 1) * LANES], kbd[tq // GQA])
        dense += list(_dense(s, tri))
    bias = jnp.where(jnp.logical_not(tri) & (n == 0), NEG_INF, 0.0)
    s_all = jnp.concatenate(dense, axis=0) * (HEAD_DIM ** -0.5) + jnp.concatenate([bias] * N_HEADS, axis=0)
    return _softmax_sink(s_all, _sink_column(sink_ref))


def _head_rows(x, tq):
    return x[2 * tq * BLOCK:(2 * tq + 1) * BLOCK], x[(2 * tq + 1) * BLOCK:(2 * tq + 2) * BLOCK]


def _attn_sub(t):
    return ATTN_SUB if t % (ATTN_SUB * BLOCK) == 0 else 1


def swa_fwd(q, kv, sink_b):
    t, d = q.shape
    sub = _attn_sub(t)
    kvw = N_KV_HEADS * HEAD_DIM

    def body(q_ref, kvc_ref, kvp_ref, sink_ref, o_ref):
        i = pl.program_id(0)
        tri = _tri()
        ext = jnp.concatenate([kvp_ref[...], kvc_ref[...]], axis=0)
        for sb in range(sub):
            r0 = sb * BLOCK
            kbd, vbd = _kv_block_diag(ext[r0:r0 + 2 * BLOCK], kvw)
            p, _ = _all_probs(q_ref, r0, kbd, tri, i * sub + sb, sink_ref)
            for tq in range(N_HEADS // 2):
                pa, pb = _head_rows(p, tq)
                o_ref[r0:r0 + BLOCK, tq * LANES:(tq + 1) * LANES] = _dot(_banded(pa, pb, tri), vbd[tq // GQA]).astype(BF16)

    return pl.pallas_call(
        functools.partial(body), name="swa_fwd", grid=(t // (sub * BLOCK),),
        in_specs=[_rows(sub * BLOCK, d), _rows(sub * BLOCK, 2 * kvw),
                  pl.BlockSpec((BLOCK, 2 * kvw), lambda i: (jnp.maximum(i * sub - 1, 0), 0)), _const(sink_b.shape)],
        out_specs=_rows(sub * BLOCK, d),
        out_shape=jax.ShapeDtypeStruct((t, d), BF16),
        compiler_params=_cparams(("parallel",), VMEM_LIMIT),
    )(q, kv, kv, sink_b)


def swa_bwd(q, kv, do, sink_b):
    t, d = q.shape
    sub = _attn_sub(t)
    nq = t // (sub * BLOCK)
    kvw = N_KV_HEADS * HEAD_DIM

    def body(q_ref, do_ref, kvc_ref, kvp_ref, sink_ref, dq_ref, dkv_ref, dsink_ref, carry):
        i = pl.program_id(0)
        step = nq - 1 - i
        _acc_init(dsink_ref)

        @pl.when(i == 0)
        def _():
            carry[...] = jnp.zeros_like(carry)

        tri = _tri()
        lo = lax.broadcasted_iota(jnp.int32, (2 * BLOCK, LANES), 1) < HEAD_DIM
        ext = jnp.concatenate([kvp_ref[...], kvc_ref[...]], axis=0)
        dkeys = [None] * (sub + 1)
        for sb in reversed(range(sub)):
            r0 = sb * BLOCK
            kbd, vbd = _kv_block_diag(ext[r0:r0 + 2 * BLOCK], kvw)
            p, ps = _all_probs(q_ref, r0, kbd, tri, step * sub + sb, sink_ref)
            dp = []
            for tq in range(N_HEADS // 2):
                dp += list(_dense(_dot_nt(do_ref[r0:r0 + BLOCK, tq * LANES:(tq + 1) * LANES], vbd[tq // GQA]), tri))
            dp = jnp.concatenate(dp, axis=0)
            delta = jnp.sum(p * dp, axis=1, keepdims=True)
            ds = p * (dp - delta) * (HEAD_DIM ** -0.5)
            dsk = ps * delta
            for h in range(N_HEADS):
                dsink_ref[h:h + 1, :] -= jnp.sum(dsk[h * BLOCK:(h + 1) * BLOCK], axis=0, keepdims=True)
            dkb = [jnp.zeros((4 * BLOCK, LANES), F32) for _ in kbd]
            dvb = [jnp.zeros((4 * BLOCK, LANES), F32) for _ in kbd]
            for tq in range(N_HEADS // 2):
                lt = tq // GQA
                cols = slice(tq * LANES, (tq + 1) * LANES)
                dsb = _banded(*_head_rows(ds, tq), tri)
                dq_ref[r0:r0 + BLOCK, cols] = _dot(dsb, kbd[lt]).astype(BF16)
                dkb[lt] = dkb[lt] + _dot_tn(dsb, q_ref[r0:r0 + BLOCK, cols])
                dvb[lt] = dvb[lt] + _dot_tn(_banded(*_head_rows(p, tq), tri), do_ref[r0:r0 + BLOCK, cols])
            dall = jnp.concatenate([jnp.where(lo, x[:2 * BLOCK], x[2 * BLOCK:]) for x in dkb + dvb], axis=1)
            dkeys[sb + 1] = dall[BLOCK:] if dkeys[sb + 1] is None else dkeys[sb + 1] + dall[BLOCK:]
            dkeys[sb] = dall[:BLOCK]
        for sb in range(sub):
            own = dkeys[sb + 1] + carry[...] if sb == sub - 1 else dkeys[sb + 1]
            dkv_ref[sb * BLOCK:(sb + 1) * BLOCK, :] = own
        carry[...] = dkeys[0]

    rev = lambda i: (nq - 1 - i, 0)
    return pl.pallas_call(
        functools.partial(body), name="swa_bwd", grid=(nq,),
        in_specs=[pl.BlockSpec((sub * BLOCK, d), rev), pl.BlockSpec((sub * BLOCK, d), rev),
                  pl.BlockSpec((sub * BLOCK, 2 * kvw), rev),
                  pl.BlockSpec((BLOCK, 2 * kvw), lambda i: (jnp.maximum((nq - 1 - i) * sub - 1, 0), 0)),
                  _const(sink_b.shape)],
        out_specs=[pl.BlockSpec((sub * BLOCK, d), rev), pl.BlockSpec((sub * BLOCK, 2 * kvw), rev),
                   _resident(sink_b.shape)],
        out_shape=[jax.ShapeDtypeStruct((t, d), BF16), jax.ShapeDtypeStruct((t, 2 * kvw), F32),
                   jax.ShapeDtypeStruct(sink_b.shape, F32)],
        scratch_shapes=[pltpu.VMEM((BLOCK, 2 * kvw), F32)],
        compiler_params=_cparams(("arbitrary",), VMEM_LIMIT),
    )(q, do, kv, kv, sink_b)


def oproj_post_fwd(attn, w_o, h0, gpost, gffn):
    t, d = h0.shape
    tm = _tile_rows(t)

    def body(at_ref, w_ref, h0_ref, gpost_ref, gffn_ref, m_ref, h1_ref, a_ref):
        m = _dot(at_ref[...], w_ref[...])
        m_ref[...] = m.astype(BF16)
        h1 = h0_ref[...] + m * _rms_r(m) * gpost_ref[...]
        h1_ref[...] = h1
        a_ref[...] = (h1 * _rms_r(h1) * gffn_ref[...]).astype(BF16)

    return pl.pallas_call(
        functools.partial(body), name="oproj_post_fwd", grid=(t // tm,),
        in_specs=[_rows(tm, d), _const(w_o.shape), _rows(tm, d), _const((1, d)), _const((1, d))],
        out_specs=[_rows(tm, d)] * 3,
        out_shape=[jax.ShapeDtypeStruct((t, d), BF16), jax.ShapeDtypeStruct((t, d), F32),
                   jax.ShapeDtypeStruct((t, d), BF16)],
        compiler_params=_cparams(("parallel",), VMEM_LIMIT),
    )(attn, w_o, h0, gpost, gffn)


def oproj_post_bwd(dh2, da, h1, m, w_o, gpost, gffn):
    t, d = h1.shape
    tm = _tile_rows(t)

    def body(dh2_ref, da_ref, h1_ref, m_ref, w_ref, gpost_ref, gffn_ref, dh1_ref, dm_ref, dat_ref, gacc_ref):
        _acc_init(gacc_ref)
        h1v, mv = h1_ref[...], m_ref[...].astype(F32)
        dh1_n, dgffn = _rms_bwd(h1v, _rms_r(h1v), gffn_ref[...], da_ref[...])
        dh1 = dh2_ref[...] + dh1_n
        dm, dgpost = _rms_bwd(mv, _rms_r(mv), gpost_ref[...], dh1)
        dmb = dm.astype(BF16)
        dh1_ref[...] = dh1
        dm_ref[...] = dmb
        dat_ref[...] = _dot_nt(dmb, w_ref[...]).astype(BF16)
        gacc_ref[0:1, :] += dgpost
        gacc_ref[1:2, :] += dgffn

    return pl.pallas_call(
        functools.partial(body), name="oproj_post_bwd", grid=(t // tm,),
        in_specs=[_rows(tm, d)] * 4 + [_const(w_o.shape), _const((1, d)), _const((1, d))],
        out_specs=[_rows(tm, d)] * 3 + [_resident((8, d))],
        out_shape=[jax.ShapeDtypeStruct((t, d), F32), jax.ShapeDtypeStruct((t, d), BF16),
                   jax.ShapeDtypeStruct((t, d), BF16), jax.ShapeDtypeStruct((8, d), F32)],
        compiler_params=_cparams(("arbitrary",), VMEM_LIMIT),
    )(dh2, da, h1, m, w_o, gpost, gffn)


def _my_place():
    return lax.axis_index("x"), lax.axis_index("y"), lax.axis_index("c")


def _block_index(px, py, pc):
    return 4 * px + 2 * py + pc


def allgather_pieces(shards, name):
    np_ = len(shards)

    def body(*refs):
        in_refs, out_refs = refs[:np_], refs[np_:2 * np_]
        send_sems, recv_sems, local_sems = refs[2 * np_:]
        x, y, c = _my_place()
        me, sibling = (x, y, c), (x, y, 1 - c)
        chips = [(1 - x, y), (x, 1 - y), (1 - x, 1 - y)]

        def rows(p, place):
            r = in_refs[p].shape[0]
            return out_refs[p].at[pl.ds(_block_index(*place) * r, r), :]

        def copy(p, k, block, to, src=None):
            return pltpu.make_async_remote_copy(
                src_ref=rows(p, block) if src is None else src, dst_ref=rows(p, block),
                send_sem=send_sems.at[p, k], recv_sem=recv_sems.at[p, k], device_id=to, device_id_type=MESH)

        mine = [pltpu.make_async_copy(in_refs[p], rows(p, me), local_sems.at[p]) for p in range(np_)]
        first, passed = [], []
        for p in range(np_):
            mine[p].start()
            first.append(copy(p, 0, me, sibling, src=in_refs[p]))
            first += [copy(p, 1 + j, me, (*chip, c), src=in_refs[p]) for j, chip in enumerate(chips)]
        for cp in first:
            cp.start()
        for p in range(np_):
            for j, chip in enumerate(chips):
                copy(p, 1 + j, (*chip, c), me).wait_recv()
                fwd = copy(p, 4 + j, (*chip, c), sibling)
                fwd.start()
                passed.append(fwd)
        for p in range(np_):
            copy(p, 0, sibling, me).wait_recv()
            for j, chip in enumerate(chips):
                copy(p, 4 + j, (*chip, 1 - c), me).wait_recv()
        for cp in first + passed:
            cp.wait_send()
        for cp in mine:
            cp.wait()

    return pl.pallas_call(
        functools.partial(body), name=name,
        in_specs=[ANY] * np_, out_specs=[ANY] * np_,
        out_shape=[jax.ShapeDtypeStruct((N_DEV * s.shape[0], s.shape[1]), s.dtype) for s in shards],
        scratch_shapes=[pltpu.SemaphoreType.DMA((np_, 7)), pltpu.SemaphoreType.DMA((np_, 7)),
                        pltpu.SemaphoreType.DMA((np_,))],
    )(*shards)


def _peers():
    x, y, c = _my_place()
    flips = [(fx, fy, fc) for fx in (0, 1) for fy in (0, 1) for fc in (0, 1)][1:]
    return [(1 - x if fx else x, 1 - y if fy else y, 1 - c if fc else c) for fx, fy, fc in flips]


HBM = pl.BlockSpec(memory_space=pltpu.HBM)
SEM = pl.BlockSpec(memory_space=pltpu.SEMAPHORE)


def _exchange_windows(scatter, src_ref, land_ref, my_block, peer_block):
    if scatter:
        r = land_ref.shape[1]
        return src_ref.at[pl.ds(peer_block * r, r), :], land_ref.at[my_block], land_ref.at[peer_block]
    r = src_ref.shape[0]
    return src_ref, land_ref.at[pl.ds(my_block * r, r), :], land_ref.at[pl.ds(peer_block * r, r), :]


def _own_copy(scatter, src_ref, land_ref, my_block, sem):
    if scatter:
        r = land_ref.shape[1]
        return pltpu.make_async_copy(src_ref.at[pl.ds(my_block * r, r), :], land_ref.at[my_block], sem)
    r = src_ref.shape[0]
    return pltpu.make_async_copy(src_ref, land_ref.at[pl.ds(my_block * r, r), :], sem)


def exchange_start(srcs, lands, after, scatter, name):
    np_ = len(srcs)

    def body(*refs):
        src_refs, land_refs = refs[:np_], refs[np_:2 * np_]
        send_sems, recv_sems, own_sems = refs[2 * np_ + 1:2 * np_ + 4]
        token = refs[-1]
        my_block = _block_index(*_my_place())
        for p in range(np_):
            _own_copy(scatter, src_refs[p], land_refs[p], my_block, own_sems.at[p]).start()
            for k, peer in enumerate(_peers()):
                src, dst, _ = _exchange_windows(scatter, src_refs[p], land_refs[p], my_block, _block_index(*peer))
                pltpu.make_async_remote_copy(src_ref=src, dst_ref=dst, send_sem=send_sems.at[7 * p + k],
                                             recv_sem=recv_sems.at[7 * p + k], device_id=peer, device_id_type=MESH).start()
        token[...] = jnp.zeros_like(token)

    hbm = lambda a: pltpu.with_memory_space_constraint(a, pltpu.HBM)
    outs = pl.pallas_call(
        functools.partial(body), name=name,
        in_specs=[HBM] * (2 * np_) + [ANY],
        out_specs=[SEM, SEM, SEM] + [HBM] * (2 * np_) + [pl.BlockSpec(memory_space=pltpu.VMEM)],
        out_shape=[pltpu.SemaphoreType.DMA((7 * np_,)), pltpu.SemaphoreType.DMA((7 * np_,)), pltpu.SemaphoreType.DMA((np_,))]
        + [pltpu.HBM(a.shape, a.dtype) for a in list(srcs) + list(lands)] + [jax.ShapeDtypeStruct((8, LANES), F32)],
        input_output_aliases={i: 3 + i for i in range(2 * np_)},
        compiler_params=pltpu.CompilerParams(has_side_effects=pltpu.SideEffectType.DATAFLOW_SIDE_EFFECTING),
    )(*[hbm(a) for a in srcs], *[hbm(a) for a in lands], after)
    return dict(sems=outs[:3], srcs=outs[3:3 + np_], lands=outs[3 + np_:3 + 2 * np_], token=outs[-1], scatter=scatter)


def exchange_wait(started, after, name):
    srcs, lands = started["srcs"], started["lands"]
    scatter = started["scatter"]
    np_ = len(srcs)

    def body(*refs):
        src_refs, land_refs = refs[:np_], refs[np_:2 * np_]
        send_sems, recv_sems, own_sems = refs[2 * np_:2 * np_ + 3]
        my_block = _block_index(*_my_place())
        for p in range(np_):
            _own_copy(scatter, src_refs[p], land_refs[p], my_block, own_sems.at[p]).wait()
            for k, peer in enumerate(_peers()):
                src, dst, arrival = _exchange_windows(scatter, src_refs[p], land_refs[p], my_block, _block_index(*peer))
                pltpu.make_async_remote_copy(src_ref=src, dst_ref=dst, send_sem=send_sems.at[7 * p + k],
                                             recv_sem=recv_sems.at[7 * p + k], device_id=peer, device_id_type=MESH).wait_send()
                pltpu.make_async_remote_copy(src_ref=src, dst_ref=arrival, send_sem=send_sems.at[7 * p + k],
                                             recv_sem=recv_sems.at[7 * p + k], device_id=peer, device_id_type=MESH).wait_recv()

    outs = pl.pallas_call(
        functools.partial(body), name=name,
        in_specs=[HBM] * (2 * np_) + [SEM, SEM, SEM, ANY],
        out_specs=[HBM] * (2 * np_),
        out_shape=[pltpu.HBM(a.shape, a.dtype) for a in list(srcs) + list(lands)],
        input_output_aliases={i: i for i in range(2 * np_)},
        compiler_params=pltpu.CompilerParams(has_side_effects=pltpu.SideEffectType.DATAFLOW_SIDE_EFFECTING),
    )(*srcs, *lands, *started["sems"], after)
    return list(outs[np_:])


def _gather_zone(shard):
    return lax.empty((N_DEV * shard.shape[0], shard.shape[1]), shard.dtype)


def _scatter_zone(full):
    return lax.empty((N_DEV, full.shape[0] // N_DEV, full.shape[1]), full.dtype)


def allreduce_small(pack):
    r, c = pack.shape

    def body(pack_ref, out_ref, gathered, send_sems, recv_sems):
        me = _my_place()
        my_block = _block_index(*me)
        peers = _peers()

        def copy(k, slot, to):
            return pltpu.make_async_remote_copy(
                src_ref=pack_ref, dst_ref=gathered.at[slot], send_sem=send_sems.at[k], recv_sem=recv_sems.at[k],
                device_id=to, device_id_type=MESH)

        sends = [copy(k, my_block, peer) for k, peer in enumerate(peers)]
        for cp in sends:
            cp.start()
        gathered[my_block] = pack_ref[...]
        for k, peer in enumerate(peers):
            copy(k, _block_index(*peer), peer).wait_recv()
        for cp in sends:
            cp.wait_send()
        total = gathered[0]
        for j in range(1, N_DEV):
            total = total + gathered[j]
        out_ref[...] = total

    return pl.pallas_call(
        functools.partial(body), name="allreduce_small",
        in_specs=[pl.BlockSpec(memory_space=pltpu.VMEM)], out_specs=pl.BlockSpec(memory_space=pltpu.VMEM),
        out_shape=jax.ShapeDtypeStruct((r, c), F32),
        scratch_shapes=[pltpu.VMEM((N_DEV, r, c), F32), pltpu.SemaphoreType.DMA((7,)), pltpu.SemaphoreType.DMA((7,))],
    )(pack)


def sum_parts(parts):
    n, r, c = parts.shape
    br = 256 if r % 256 == 0 else r

    def body(p_ref, g_ref):
        g = p_ref[0].astype(F32)
        for j in range(1, n):
            g = g + p_ref[j].astype(F32)
        g_ref[...] = g

    return pl.pallas_call(
        functools.partial(body), name="sum_parts", grid=(r // br,),
        in_specs=[pl.BlockSpec((n, br, c), lambda i: (0, i, 0))], out_specs=_rows(br, c),
        out_shape=jax.ShapeDtypeStruct((r, c), F32),
        compiler_params=_cparams(("parallel",)),
    )(parts)


def adamw(w, m, v, parts):
    r, c = w.shape
    n = parts.shape[0]
    br = 256 if r % 256 == 0 else r

    def body(w_ref, m_ref, v_ref, p_ref, g_ref, d_ref, nm_ref, nv_ref):
        g = p_ref[0].astype(F32)
        for j in range(1, n):
            g = g + p_ref[j].astype(F32)
        nm = ADAM_B1 * m_ref[...] + (1.0 - ADAM_B1) * g
        nv = ADAM_B2 * v_ref[...] + (1.0 - ADAM_B2) * (g * g)
        m_hat = nm / (1.0 - ADAM_B1 ** ADAM_STEP)
        v_hat = nv / (1.0 - ADAM_B2 ** ADAM_STEP)
        g_ref[...] = g
        d_ref[...] = -ADAM_LR * (m_hat / (jnp.sqrt(v_hat) + ADAM_EPS) + ADAM_WD * w_ref[...])
        nm_ref[...] = nm
        nv_ref[...] = nv

    return pl.pallas_call(
        functools.partial(body), name="adamw", grid=(r // br,),
        in_specs=[_rows(br, c)] * 3 + [pl.BlockSpec((n, br, c), lambda i: (0, i, 0))],
        out_specs=[_rows(br, c)] * 4, out_shape=[jax.ShapeDtypeStruct((r, c), F32)] * 4,
        compiler_params=_cparams(("parallel",)),
    )(w, m, v, parts)


def _adamw_nd(w, m, v, g):
    shp = w.shape
    c = shp[-1]
    flat = lambda a: a.reshape(-1, c)
    outs = adamw(flat(w), flat(m), flat(v), flat(g)[None])
    return [o.reshape(shp) for o in outs]


def _pair_heads(a, axis, width=HEAD_DIM):
    shp = a.shape
    a = a.reshape(shp[:axis] + (2, 2, GQA, width) + shp[axis + 1:])
    return jnp.swapaxes(a, axis + 1, axis + 2).reshape(shp)


def _unpair_heads(a, axis, width=HEAD_DIM):
    shp = a.shape
    a = a.reshape(shp[:axis] + (2, GQA, 2, width) + shp[axis + 1:])
    return jnp.swapaxes(a, axis + 1, axis + 2).reshape(shp)


def _pad_rows(a, rows=8):
    return jnp.pad(a, ((0, rows - a.shape[0]), (0, 0)))


def kernel(x, p, mix_pre_g, mix_post_g, ffn_pre_g, ffn_post_g, pool_w, pool_scale, kv_norm_g, w_k, w_v, w_q, w_o, sinks, w_ff_gate, w_ff_up, w_ff_down, ple_norm_g, w_ple_gate, w_ple_proj, loss_target, m_mix_pre_g, m_mix_post_g, m_ffn_pre_g, m_ffn_post_g, m_pool_w, m_pool_scale, m_kv_norm_g, m_w_k, m_w_v, m_w_q, m_w_o, m_sinks, m_w_ff_gate, m_w_ff_up, m_w_ff_down, m_ple_norm_g, m_w_ple_gate, m_w_ple_proj, v_mix_pre_g, v_mix_post_g, v_ffn_pre_g, v_ffn_post_g, v_pool_w, v_pool_scale, v_kv_norm_g, v_w_k, v_w_v, v_w_q, v_w_o, v_sinks, v_w_ff_gate, v_w_ff_up, v_w_ff_down, v_ple_norm_g, v_w_ple_gate, v_w_ple_proj):
    depth = w_ff_gate.shape[0]
    n_a = pool_w.shape[0]
    t, d = x.shape[1], x.shape[2]
    h = x[0]
    tgt = loss_target[0]
    p_all = p.reshape(depth * t, p.shape[-1])
    my_block = _block_index(*_my_place())
    row = lambda g, i: g[i][None, :]
    bf = lambda a: a.astype(BF16)

    full, gathers = [None] * depth, {}
    start_tokens = jnp.zeros((), F32)
    for i in range(depth):
        shards = [bf(w_ff_gate[i].T), bf(w_ff_up[i].T), bf(w_ff_down[i]), bf(w_ple_gate[i]), bf(w_ple_proj[i].T)]
        if i == 0:
            pool0, scale_full = allgather_pieces([bf(pool_w[0].reshape(-1, POOL_GROUP)), _pad_rows(pool_scale)],
                                                 "allgather_pool0")
            order = pool0
        elif i < n_a:
            shards.append(bf(pool_w[i].reshape(-1, POOL_GROUP)))
        else:
            shards += [bf(_pair_heads(w_q[i - n_a], 1)), bf(w_o[i - n_a])]
            if i == n_a:
                shards.append(bf(jnp.concatenate([w_k, w_v], axis=1)))
        gathers[i] = exchange_start(shards, [_gather_zone(s) for s in shards], order, False, f"allgather_start_l{i}")
        order = gathers[i]["token"]
        start_tokens = start_tokens + order[0, 0]
    scale_full = scale_full.reshape(N_DEV, 8, -1)[:, :n_a].transpose(1, 0, 2).reshape(n_a, 1, d)

    cos, sin = _rope_tables(t)
    sink_b = [jnp.broadcast_to(_pair_heads(sinks[j][:, None], 0, 1), (N_HEADS, LANES)) for j in range(depth - n_a)]
    pool_full, wo_full = {}, {}

    saved = []
    kv = hk = None
    for i in range(depth):
        if i > 0:
            full[i] = exchange_wait(gathers[i], h, f"allgather_wait_l{i}")
        s = {"h0": h}
        if i < n_a:
            pool_full[i] = ((pool0 if i == 0 else full[i][5]).reshape(N_DEV, len(POOL_WINDOWS), -1, POOL_GROUP)
                            .transpose(1, 0, 2, 3).reshape(len(POOL_WINDOWS), POOL_GROUP, POOL_GROUP))
            gpre = row(mix_pre_g, i) + start_tokens if i == 0 else row(mix_pre_g, i)
            h1, a = pool_mix_fwd(h, gpre, pool_full[i], scale_full[i], row(mix_post_g, i), row(ffn_pre_g, i))
            if i == 0:
                full[0] = exchange_wait(gathers[0], h1, "allgather_wait_l0")
        else:
            j = i - n_a
            wo_full[i] = _pair_heads(full[i][6], 0)
            if i == n_a:
                hk, kv = proj_rope_fwd(h, kv_norm_g[None, :], full[i][7], cos, sin, N_KV_HEADS * HEAD_DIM, "kv_proj_fwd")
            hn, q = proj_rope_fwd(h, row(mix_pre_g, i), full[i][5], cos, sin, d, "q_proj_fwd")
            attn = swa_fwd(q, kv, sink_b[j])
            m, h1, a = oproj_post_fwd(attn, wo_full[i], h, row(mix_post_g, i), row(ffn_pre_g, i))
            s.update(hn=hn, q=q, attn=attn, m=m)
        wg_t, wu_t, wd, wpg, wpp_t = full[i][:5]
        f, gte, up, hdn = ffn_fwd(a, wg_t, wu_t, wd)
        s.update(h1=h1, a=a, f=f, gte=gte, up=up, hdn=hdn)
        if i < depth - 1:
            h = post_ple_fwd(h1, f, p_all, i, row(ffn_post_g, i), row(ple_norm_g, i), wpg, wpp_t)[0]
        else:
            dh, loss_rows = post_ple_fwd(h1, f, p_all, i, row(ffn_post_g, i), row(ple_norm_g, i), wpg, wpp_t, target=tgt)
        saved.append(s)

    g_mix_pre, g_mix_post, g_ffn_pre, g_ffn_post, g_ple = ([None] * depth for _ in range(5))
    g_kv = g_sinks = None
    g_scale = [None] * n_a
    landing, scatters = [None] * depth, {}
    dkv_sum = []
    scatter_token = jnp.zeros((), F32)
    for i in reversed(range(depth)):
        s = saved[i]
        wg_t, wu_t, wd, wpg, wpp_t = full[i][:5]
        dh2, df, ub, dzb, dppb, gacc = post_ple_bwd(dh, s["h1"], s["f"], p_all, i, row(ffn_post_g, i) + scatter_token,
                                                    row(ple_norm_g, i), wpg, wpp_t)
        g_ple[i], g_ffn_post[i] = gacc[0], gacc[1]
        da, dgte, dup = ffn_bwd_act(df, s["gte"], s["up"], wg_t, wu_t, wd)
        grads = [xty(dgte, s["a"]), xty(dup, s["a"]), xty(s["hdn"], df), xty(ub, dzb), xty(dppb, p_all, i)]
        early = exchange_start(grads, [_scatter_zone(g) for g in grads], dh2, True, f"reduce_scatter_start_l{i}a")
        early_token = early["token"][0, 0]
        if i < n_a:
            dh, dpw, gacc = pool_mix_bwd(s["h0"], dh2, da, row(mix_pre_g, i) + early_token, pool_full[i], scale_full[i],
                                         row(mix_post_g, i), row(ffn_pre_g, i))
            g_mix_pre[i], g_mix_post[i], g_ffn_pre[i], g_scale[i] = gacc[0], gacc[1], gacc[2], gacc[3]
            dpw = dpw.reshape(len(POOL_WINDOWS), N_DEV, -1, POOL_GROUP).transpose(1, 0, 2, 3)
            grads = [bf(dpw.reshape(-1, POOL_GROUP))]
        else:
            j = i - n_a
            dh1, dmb, dattn, gacc = oproj_post_bwd(dh2, da, s["h1"], s["m"], wo_full[i], row(mix_post_g, i) + early_token,
                                                   row(ffn_pre_g, i))
            g_mix_post[i], g_ffn_pre[i] = gacc[0], gacc[1]
            dq, dkv, dsink = swa_bwd(s["q"], kv, dattn, sink_b[j])
            dkv_sum.append(dkv)
            g_sinks = [_unpair_heads(dsink[:, 0:1], 0, 1)[:, 0]] + (g_sinks or [])
            branches = [(row(mix_pre_g, i), full[i][5], d, [dq])]
            if i == n_a:
                branches.append((kv_norm_g[None, :], full[i][7], N_KV_HEADS * HEAD_DIM, dkv_sum))
            outs = proj_rope_bwd(dh1, s["h0"], cos, sin, branches, f"proj_bwd_l{i}")
            dh, gacc = outs[0], outs[-1]
            g_mix_pre[i] = gacc[0]
            grads = [xty(s["hn"], outs[1]), _unpair_heads(xty(s["attn"], dmb), 0)]
            if i == n_a:
                g_kv = gacc[1]
                grads.append(xty(hk, outs[2]))
        late = exchange_start(grads, [_scatter_zone(g) for g in grads], dh, True, f"reduce_scatter_start_l{i}b")
        scatter_token = late["token"][0, 0]
        scatters[i] = (early, late)
    grad_x = dh[None]
    after = dh
    for i in reversed(range(depth)):
        landing[i] = (exchange_wait(scatters[i][0], after, f"reduce_scatter_wait_l{i}a")
                      + exchange_wait(scatters[i][1], after, f"reduce_scatter_wait_l{i}b"))
        after = landing[i][0]

    loss_row = jnp.sum(loss_rows, axis=0, keepdims=True)
    sink_row = jnp.pad(jnp.concatenate(g_sinks)[None, :], ((0, 0), (0, d - sinks.size)))
    stack = lambda rows_: _pad_rows(jnp.stack(rows_))
    pack = jnp.concatenate([stack(g_mix_pre), stack(g_mix_post), stack(g_ffn_pre), stack(g_ffn_post), stack(g_ple),
                            _pad_rows(g_kv[None]), stack(g_scale), _pad_rows(sink_row), _pad_rows(loss_row)], axis=0)
    tot = allreduce_small(pack)
    sec = lambda k, n: tot[8 * k:8 * k + n]
    loss = jnp.sum(tot[64])
    small = {
        "mix_pre_g": sec(0, depth), "mix_post_g": sec(1, depth), "ffn_pre_g": sec(2, depth),
        "ffn_post_g": sec(3, depth), "ple_norm_g": sec(4, depth), "kv_norm_g": tot[40],
        "pool_scale": lax.dynamic_slice_in_dim(sec(6, n_a), my_block * pool_scale.shape[1], pool_scale.shape[1], axis=1),
        "sinks": tot[56, :sinks.size].reshape(sinks.shape),
    }

    weights = dict(mix_pre_g=mix_pre_g, mix_post_g=mix_post_g, ffn_pre_g=ffn_pre_g, ffn_post_g=ffn_post_g, pool_w=pool_w, pool_scale=pool_scale, kv_norm_g=kv_norm_g, w_k=w_k, w_v=w_v, w_q=w_q, w_o=w_o, sinks=sinks, w_ff_gate=w_ff_gate, w_ff_up=w_ff_up, w_ff_down=w_ff_down, ple_norm_g=ple_norm_g, w_ple_gate=w_ple_gate, w_ple_proj=w_ple_proj)
    mom1 = dict(mix_pre_g=m_mix_pre_g, mix_post_g=m_mix_post_g, ffn_pre_g=m_ffn_pre_g, ffn_post_g=m_ffn_post_g, pool_w=m_pool_w, pool_scale=m_pool_scale, kv_norm_g=m_kv_norm_g, w_k=m_w_k, w_v=m_w_v, w_q=m_w_q, w_o=m_w_o, sinks=m_sinks, w_ff_gate=m_w_ff_gate, w_ff_up=m_w_ff_up, w_ff_down=m_w_ff_down, ple_norm_g=m_ple_norm_g, w_ple_gate=m_w_ple_gate, w_ple_proj=m_w_ple_proj)
    mom2 = dict(mix_pre_g=v_mix_pre_g, mix_post_g=v_mix_post_g, ffn_pre_g=v_ffn_pre_g, ffn_post_g=v_ffn_post_g, pool_w=v_pool_w, pool_scale=v_pool_scale, kv_norm_g=v_kv_norm_g, w_k=v_w_k, w_v=v_w_v, w_q=v_w_q, w_o=v_w_o, sinks=v_sinks, w_ff_gate=v_w_ff_gate, w_ff_up=v_w_ff_up, w_ff_down=v_w_ff_down, ple_norm_g=v_ple_norm_g, w_ple_gate=v_w_ple_gate, w_ple_proj=v_w_ple_proj)

    def land(i, k):
        return sum_parts(landing[i][k])

    gw = dict(small)
    gw["kv_norm_g"] = small["kv_norm_g"]
    gw["w_ff_gate"] = jnp.stack([land(i, 0).T for i in range(depth)])
    gw["w_ff_up"] = jnp.stack([land(i, 1).T for i in range(depth)])
    gw["w_ff_down"] = jnp.stack([land(i, 2) for i in range(depth)])
    gw["w_ple_gate"] = jnp.stack([land(i, 3) for i in range(depth)])
    gw["w_ple_proj"] = jnp.stack([land(i, 4).T for i in range(depth)])
    gw["pool_w"] = jnp.stack([land(i, 5).reshape(pool_w.shape[1:]) for i in range(n_a)])
    gw["w_q"] = jnp.stack([_unpair_heads(land(i, 5), 1) for i in range(n_a, depth)])
    gw["w_o"] = jnp.stack([land(i, 6) for i in range(n_a, depth)])
    gkv = land(n_a, 7)
    gw["w_k"], gw["w_v"] = gkv[:, :w_k.shape[1]], gkv[:, w_k.shape[1]:]

    order = ["mix_pre_g", "mix_post_g", "ffn_pre_g", "ffn_post_g", "pool_w", "pool_scale", "kv_norm_g", "w_k", "w_v",
             "w_q", "w_o", "sinks", "w_ff_gate", "w_ff_up", "w_ff_down", "ple_norm_g", "w_ple_gate", "w_ple_proj"]
    g_out, d_out, m_out, v_out = [], [], [], []
    for nme in order:
        w = weights[nme]
        as2d = (lambda a: a[None, :]) if w.ndim == 1 else (lambda a: a)
        g, dl, nm, nv = _adamw_nd(as2d(w), as2d(mom1[nme]), as2d(mom2[nme]), as2d(gw[nme]))
        for lst, val in ((g_out, g), (d_out, dl), (m_out, nm), (v_out, nv)):
            lst.append(val.reshape(w.shape))
    return (loss, grad_x, *g_out, *d_out, *m_out, *v_out)
```

```python
import functools

import jax
import jax.numpy as jnp
from jax import lax
from jax.experimental import pallas as pl
from jax.experimental.pallas import tpu as pltpu

F32 = jnp.float32
BF16 = jnp.bfloat16

N_DEV = 8
HEAD_DIM = 64
N_HEADS = 16
N_KV_HEADS = 4
GQA = N_HEADS // N_KV_HEADS
BLOCK = 128
POOL_WINDOWS = (2, 4, 8, 16)
POOL_GROUP = 256
HALO = 16
ROPE_THETA = 10000.0
RMS_EPS = 1e-6
NEG_INF = -1e30
LANES = 128
ATTN_SUB = 8
XTY_ROWS = 2048
FFN_CHUNK = 768
VMEM_LIMIT = 56 * 1024 * 1024

ADAM_LR = 0.001
ADAM_B1 = 0.9
ADAM_B2 = 0.999
ADAM_EPS = 1e-08
ADAM_WD = 0.01
ADAM_STEP = 10

MESH = pl.DeviceIdType.MESH
ANY = pl.BlockSpec(memory_space=pl.ANY)

NT_DIMS = (((1,), (1,)), ((), ()))
TN_DIMS = (((0,), (0,)), ((), ()))


def _cparams(sem=None, vmem=None):
    kw = {}
    if sem is not None:
        kw["dimension_semantics"] = sem
    if vmem is not None:
        kw["vmem_limit_bytes"] = vmem
    return pltpu.CompilerParams(**kw)


def _rows(tm, n, first=0):
    return pl.BlockSpec((tm, n), lambda i: (i + first, 0))


def _rows_rev(tm, n, nt):
    return pl.BlockSpec((tm, n), lambda i: (nt - 1 - i, 0))


def _const(shape):
    nd = len(shape)
    return pl.BlockSpec(shape, lambda *_: (0,) * nd, pipeline_mode=pl.Buffered(1))


def _resident(shape):
    nd = len(shape)
    return pl.BlockSpec(shape, lambda *_: (0,) * nd)


def _tile_rows(t):
    return 512 if t % 512 == 0 else 128


def _dot(a, b):
    return jnp.dot(a, b, preferred_element_type=F32)


def _dot_nt(a, b):
    return lax.dot_general(a, b, NT_DIMS, preferred_element_type=F32)


def _dot_tn(a, b):
    return lax.dot_general(a, b, TN_DIMS, preferred_element_type=F32)


def _rms_r(x):
    return lax.rsqrt(jnp.mean(x * x, axis=-1, keepdims=True) + RMS_EPS)


def _rms_bwd(x, r, g, dy):
    gy = dy * g
    dx = r * gy - x * (r * r * r * jnp.mean(gy * x, axis=-1, keepdims=True))
    dg = jnp.sum(dy * (x * r), axis=0, keepdims=True)
    return dx, dg


def _sigmoid(x):
    return jax.nn.sigmoid(x)


def _rope_tables(t):
    inv = 1.0 / (ROPE_THETA ** (jnp.arange(0, HEAD_DIM, 2, dtype=F32) / HEAD_DIM))
    ang = jnp.arange(t, dtype=F32)[:, None] * jnp.tile(inv, 2 * LANES // HEAD_DIM)[None, :]
    sign = jnp.tile(jnp.repeat(jnp.array([-1.0, 1.0], F32), HEAD_DIM // 2), LANES // HEAD_DIM)
    return jnp.cos(ang), jnp.sin(ang) * sign[None, :]


def _swap_halves(x):
    n = x.shape[1]
    lane = lax.broadcasted_iota(jnp.int32, x.shape, 1)
    first = (lane % HEAD_DIM) < (HEAD_DIM // 2)
    return jnp.where(first, pltpu.roll(x, n - HEAD_DIM // 2, 1), pltpu.roll(x, HEAD_DIM // 2, 1))


def _rope(x, cos, sin):
    reps = x.shape[1] // LANES
    return x * jnp.tile(cos, (1, reps)) + _swap_halves(x) * jnp.tile(sin, (1, reps))


def _unrope(dy, cos, sin):
    reps = dy.shape[1] // LANES
    return dy * jnp.tile(cos, (1, reps)) + _swap_halves(dy * jnp.tile(sin, (1, reps)))


def _acc_init(acc_ref):
    @pl.when(pl.program_id(0) == 0)
    def _():
        acc_ref[...] = jnp.zeros_like(acc_ref)


def _window_sums(ext, tm, forward):
    n = tm + HALO
    out = []
    for g, w in enumerate(POOL_WINDOWS):
        s = ext[:, g * POOL_GROUP:(g + 1) * POOL_GROUP]
        k = 1
        while k < w:
            s = s + pltpu.roll(s, k if forward else n - k, 0)
            k *= 2
        out.append(s[HALO:, :] if forward else s[:tm, :])
    return out


def _pool_inv_counts(tile, tm):
    t = tile * tm + lax.broadcasted_iota(jnp.int32, (tm, 1), 0)
    return [1.0 / jnp.minimum(t + 1, w).astype(F32) for w in POOL_WINDOWS]


def _pool_mix(hn, ext, inv_cnts, pw_ref, scale, tm):
    sums = _window_sums(ext, tm, True)
    pooled, ys = [], []
    for g in range(len(POOL_WINDOWS)):
        pg = (sums[g] * inv_cnts[g] - hn[:, g * POOL_GROUP:(g + 1) * POOL_GROUP]).astype(BF16)
        pooled.append(pg)
        ys.append(_dot(pg, pw_ref[g]))
    y = jnp.concatenate(ys, axis=1)
    return pooled, y, y * scale


def pool_mix_fwd(h0, gpre, pool_w, scale, gpost, gffn):
    t, d = h0.shape
    tm = _tile_rows(t)

    def body(h_ref, gpre_ref, pw_ref, scale_ref, gpost_ref, gffn_ref, h1_ref, a_ref, carry):
        i = pl.program_id(0)

        @pl.when(i == 0)
        def _():
            carry[...] = jnp.zeros_like(carry)

        x = h_ref[...]
        hn = x * _rms_r(x) * gpre_ref[...]
        ext = jnp.concatenate([carry[...], hn], axis=0)
        carry[...] = hn[tm - HALO:, :]
        _, _, m = _pool_mix(hn, ext, _pool_inv_counts(i, tm), pw_ref, scale_ref[...], tm)
        h1 = x + m * _rms_r(m) * gpost_ref[...]
        h1_ref[...] = h1
        a_ref[...] = (h1 * _rms_r(h1) * gffn_ref[...]).astype(BF16)

    return pl.pallas_call(
        functools.partial(body), name="pool_mix_fwd", grid=(t // tm,),
        in_specs=[_rows(tm, d), _const((1, d)), _const(pool_w.shape), _const((1, d)), _const((1, d)), _const((1, d))],
        out_specs=[_rows(tm, d), _rows(tm, d)],
        out_shape=[jax.ShapeDtypeStruct((t, d), F32), jax.ShapeDtypeStruct((t, d), BF16)],
        scratch_shapes=[pltpu.VMEM((HALO, d), F32)],
        compiler_params=_cparams(("arbitrary",), VMEM_LIMIT),
    )(h0, gpre, pool_w, scale, gpost, gffn)


def pool_mix_bwd(h0, dh2, da, gpre, pool_w, scale, gpost, gffn):
    t, d = h0.shape
    tm = _tile_rows(t)
    nt = t // tm
    hb = tm // HALO

    def body(h_ref, halo_ref, dh2_ref, da_ref, gpre_ref, pw_ref, scale_ref, gpost_ref, gffn_ref,
             dh0_ref, dpw_ref, gacc_ref, carry):
        i = pl.program_id(0)
        tile = nt - 1 - i
        _acc_init(gacc_ref)
        _acc_init(dpw_ref)

        @pl.when(i == 0)
        def _():
            carry[...] = jnp.zeros_like(carry)

        x = h_ref[...]
        gpre_v, scale_v, gpost_v, gffn_v = gpre_ref[...], scale_ref[...], gpost_ref[...], gffn_ref[...]
        r0 = _rms_r(x)
        hn = x * r0 * gpre_v
        xh = halo_ref[...]
        hn_halo = jnp.where(tile > 0, xh * _rms_r(xh) * gpre_v, 0.0)
        ext = jnp.concatenate([hn_halo, hn], axis=0)
        inv_cnts = _pool_inv_counts(tile, tm)
        pooled, y, m = _pool_mix(hn, ext, inv_cnts, pw_ref, scale_v, tm)
        rm = _rms_r(m)
        h1 = x + m * rm * gpost_v
        dh1_n, dgffn = _rms_bwd(h1, _rms_r(h1), gffn_v, da_ref[...])
        dh1 = dh2_ref[...] + dh1_n
        dm, dgpost = _rms_bwd(m, rm, gpost_v, dh1)
        dscale = jnp.sum(dm * y, axis=0, keepdims=True)
        dy = (dm * scale_v).astype(BF16)
        dpn = []
        for g in range(len(POOL_WINDOWS)):
            dyg = dy[:, g * POOL_GROUP:(g + 1) * POOL_GROUP]
            dpw_ref[g] += _dot_tn(pooled[g], dyg)
            dpn.append(_dot_nt(dyg, pw_ref[g]))
        dpooled = jnp.concatenate(dpn, axis=1)
        dpc = jnp.concatenate([dpn[g] * inv_cnts[g] for g in range(len(POOL_WINDOWS))], axis=1)
        ext2 = jnp.concatenate([dpc, carry[...]], axis=0)
        carry[...] = dpc[:HALO, :]
        dhn = jnp.concatenate(_window_sums(ext2, tm, False), axis=1) - dpooled
        dh0_n, dgpre = _rms_bwd(x, r0, gpre_v, dhn)
        dh0_ref[...] = dh1 + dh0_n
        gacc_ref[0:1, :] += dgpre
        gacc_ref[1:2, :] += dgpost
        gacc_ref[2:3, :] += dgffn
        gacc_ref[3:4, :] += dscale

    return pl.pallas_call(
        functools.partial(body), name="pool_mix_bwd", grid=(nt,),
        in_specs=[_rows_rev(tm, d, nt),
                  pl.BlockSpec((HALO, d), lambda i: (jnp.maximum((nt - 1 - i) * hb - 1, 0), 0)),
                  _rows_rev(tm, d, nt), _rows_rev(tm, d, nt),
                  _const((1, d)), _const(pool_w.shape), _const((1, d)), _const((1, d)), _const((1, d))],
        out_specs=[_rows_rev(tm, d, nt), _resident(pool_w.shape), _resident((8, d))],
        out_shape=[jax.ShapeDtypeStruct((t, d), F32), jax.ShapeDtypeStruct(pool_w.shape, F32),
                   jax.ShapeDtypeStruct((8, d), F32)],
        scratch_shapes=[pltpu.VMEM((HALO, d), F32)],
        compiler_params=_cparams(("arbitrary",), VMEM_LIMIT),
    )(h0, h0, dh2, da, gpre, pool_w, scale, gpost, gffn)


def _ffn_chunks(f):
    return [(c, min(c + FFN_CHUNK, f)) for c in range(0, f, FFN_CHUNK)]


def ffn_fwd(a, wg_t, wu_t, wd):
    t, d = a.shape
    f = wd.shape[0]
    tm = _tile_rows(t)

    def body(a_ref, wg_ref, wu_ref, wd_ref, f_ref, gte_ref, up_ref, hdn_ref):
        av = a_ref[...]
        acc = jnp.zeros((tm, d), F32)
        for c0, c1 in _ffn_chunks(f):
            gte = _dot_nt(av, wg_ref[c0:c1, :])
            up = _dot_nt(av, wu_ref[c0:c1, :])
            gte_ref[:, c0:c1] = gte.astype(BF16)
            up_ref[:, c0:c1] = up.astype(BF16)
            hdn = (gte * _sigmoid(gte) * up).astype(BF16)
            hdn_ref[:, c0:c1] = hdn
            acc = acc + _dot(hdn, wd_ref[c0:c1, :])
        f_ref[...] = acc.astype(BF16)

    return pl.pallas_call(
        functools.partial(body), name="ffn_fwd", grid=(t // tm,),
        in_specs=[_rows(tm, d), _const((f, d)), _const((f, d)), _const((f, d))],
        out_specs=[_rows(tm, d), _rows(tm, f), _rows(tm, f), _rows(tm, f)],
        out_shape=[jax.ShapeDtypeStruct((t, d), BF16)] + [jax.ShapeDtypeStruct((t, f), BF16)] * 3,
        compiler_params=_cparams(("parallel",), VMEM_LIMIT),
    )(a, wg_t, wu_t, wd)


def ffn_bwd_act(df, gte, up, wg_t, wu_t, wd):
    t, d = df.shape
    f = wd.shape[0]
    tm = _tile_rows(t)

    def body(df_ref, gte_ref, up_ref, wg_ref, wu_ref, wd_ref, da_ref, dgte_ref, dup_ref):
        dfv = df_ref[...]
        chunks = _ffn_chunks(f)
        half = chunks[len(chunks) // 2][0]
        acc = None
        for c0, c1 in chunks:
            g = gte_ref[:, c0:c1].astype(F32)
            u = up_ref[:, c0:c1].astype(F32)
            sg = _sigmoid(g)
            sl = g * sg
            dh = _dot_nt(dfv, wd_ref[c0:c1, :])
            dup_ref[:, c0:c1] = (dh * sl).astype(BF16)
            dgte_ref[:, c0:c1] = (dh * u * (sg * (1.0 + g * (1.0 - sg)))).astype(BF16)
            if c1 == half:
                acc = _dot(dgte_ref[:, :half], wg_ref[:half, :]) + _dot(dup_ref[:, :half], wu_ref[:half, :])
        da_ref[...] = acc + _dot(dgte_ref[:, half:], wg_ref[half:, :]) + _dot(dup_ref[:, half:], wu_ref[half:, :])

    return pl.pallas_call(
        functools.partial(body), name="ffn_bwd_act", grid=(t // tm,),
        in_specs=[_rows(tm, d), _rows(tm, f), _rows(tm, f), _const((f, d)), _const((f, d)), _const((f, d))],
        out_specs=[_rows(tm, d), _rows(tm, f), _rows(tm, f)],
        out_shape=[jax.ShapeDtypeStruct((t, d), F32)] + [jax.ShapeDtypeStruct((t, f), BF16)] * 2,
        compiler_params=_cparams(("parallel",), VMEM_LIMIT),
    )(df, gte, up, wg_t, wu_t, wd)


def xty(x, y, y_part=0):
    t, nx = x.shape
    ny = y.shape[1]
    tk = XTY_ROWS if t % XTY_ROWS == 0 else _tile_rows(t)
    bn = nx // 2 if nx > 1024 else nx
    nk = t // tk

    def body(x_ref, y_ref, o_ref, acc):
        k = pl.program_id(1)

        @pl.when(k == 0)
        def _():
            acc[...] = jnp.zeros_like(acc)

        acc[...] += _dot_tn(x_ref[...].astype(BF16), y_ref[...].astype(BF16))

        @pl.when(k == nk - 1)
        def _():
            o_ref[...] = acc[...].astype(BF16)

    return pl.pallas_call(
        functools.partial(body), name="xty", grid=(nx // bn, nk),
        in_specs=[pl.BlockSpec((tk, bn), lambda j, k: (k, j)),
                  pl.BlockSpec((tk, ny), lambda j, k: (k + y_part * nk, 0))],
        out_specs=pl.BlockSpec((bn, ny), lambda j, k: (j, 0)),
        out_shape=jax.ShapeDtypeStruct((nx, ny), BF16),
        scratch_shapes=[pltpu.VMEM((bn, ny), F32)],
        compiler_params=_cparams(("parallel", "arbitrary"), VMEM_LIMIT),
    )(x, y)


def _ple_fwd_tile(h1, f, p, gpost, gple, wpg_ref, wpp_ref):
    rf = _rms_r(f)
    h2 = h1 + f * rf * gpost
    r2 = _rms_r(h2)
    ub = (h2 * r2 * gple).astype(BF16)
    gate = _sigmoid(_dot(ub, wpg_ref[...]))
    pp = _dot_nt(p.astype(BF16), wpp_ref[...])
    return rf, h2, r2, ub, gate, pp


def post_ple_fwd(h1, f, p, layer, gpost, gple, wpg, wpp_t):
    t, d = h1.shape
    pd = p.shape[1]
    tm = _tile_rows(t)

    def body(h1_ref, f_ref, p_ref, gpost_ref, gple_ref, wpg_ref, wpp_ref, out_ref):
        _, h2, _, _, gate, pp = _ple_fwd_tile(h1_ref[...], f_ref[...].astype(F32), p_ref[...], gpost_ref[...],
                                              gple_ref[...], wpg_ref, wpp_ref)
        out_ref[...] = h2 + pp * gate

    return pl.pallas_call(
        functools.partial(body), name="post_ple_fwd", grid=(t // tm,),
        in_specs=[_rows(tm, d), _rows(tm, d), _rows(tm, pd, layer * (t // tm)), _const((1, d)), _const((1, d)),
                  _const(wpg.shape), _const(wpp_t.shape)],
        out_specs=_rows(tm, d), out_shape=jax.ShapeDtypeStruct((t, d), F32),
        compiler_params=_cparams(("parallel",), VMEM_LIMIT),
    )(h1, f, p, gpost, gple, wpg, wpp_t)


def post_ple_bwd(dh3, h1, f, p, layer, gpost, gple, wpg, wpp_t, from_target=False):
    t, d = h1.shape
    pd = p.shape[1]
    tm = _tile_rows(t)

    def body(dh3_ref, h1_ref, f_ref, p_ref, gpost_ref, gple_ref, wpg_ref, wpp_ref,
             dh2_ref, df_ref, u_ref, dz_ref, dpp_ref, gacc_ref):
        _acc_init(gacc_ref)
        fv = f_ref[...].astype(F32)
        gpost_v, gple_v = gpost_ref[...], gple_ref[...]
        rf, h2, r2, ub, gate, pp = _ple_fwd_tile(h1_ref[...], fv, p_ref[...], gpost_v, gple_v, wpg_ref, wpp_ref)
        if from_target:
            err = h2 + pp * gate - dh3_ref[...]
            dh3v = err * (1.0 / d)
            gacc_ref[2:3, :] += jnp.sum(err * err, axis=0, keepdims=True) * (0.5 / d)
        else:
            dh3v = dh3_ref[...]
        dpp_ref[...] = (dh3v * gate).astype(BF16)
        dz = (dh3v * pp * gate * (1.0 - gate)).astype(BF16)
        dz_ref[...] = dz
        u_ref[...] = ub
        du = _dot_nt(dz, wpg_ref[...])
        dh2_n, dgple = _rms_bwd(h2, r2, gple_v, du)
        dh2 = dh3v + dh2_n
        df, dgpost = _rms_bwd(fv, rf, gpost_v, dh2)
        dh2_ref[...] = dh2
        df_ref[...] = df.astype(BF16)
        gacc_ref[0:1, :] += dgple
        gacc_ref[1:2, :] += dgpost

    return pl.pallas_call(
        functools.partial(body), name="post_ple_loss_bwd" if from_target else "post_ple_bwd", grid=(t // tm,),
        in_specs=[_rows(tm, d), _rows(tm, d), _rows(tm, d), _rows(tm, pd, layer * (t // tm)), _const((1, d)),
                  _const((1, d)), _const(wpg.shape), _const(wpp_t.shape)],
        out_specs=[_rows(tm, d)] * 5 + [_resident((8, d))],
        out_shape=[jax.ShapeDtypeStruct((t, d), F32)] + [jax.ShapeDtypeStruct((t, d), BF16)] * 4
        + [jax.ShapeDtypeStruct((8, d), F32)],
        compiler_params=_cparams(("arbitrary",), VMEM_LIMIT),
    )(dh3, h1, f, p, gpost, gple, wpg, wpp_t)


def proj_rope_fwd(h, gain, w, cos, sin, n_rope, name):
    t, d = h.shape
    n = w.shape[1]
    tm = _tile_rows(t)

    def body(h_ref, g_ref, w_ref, cos_ref, sin_ref, hn_ref, y_ref):
        x = h_ref[...]
        hn = (x * _rms_r(x) * g_ref[...]).astype(BF16)
        hn_ref[...] = hn
        y = _dot(hn, w_ref[...])
        y_ref[:, :n_rope] = _rope(y[:, :n_rope], cos_ref[...], sin_ref[...]).astype(BF16)
        if n_rope < n:
            y_ref[:, n_rope:] = y[:, n_rope:].astype(BF16)

    return pl.pallas_call(
        functools.partial(body), name=name, grid=(t // tm,),
        in_specs=[_rows(tm, d), _const((1, d)), _const(w.shape), _rows(tm, LANES), _rows(tm, LANES)],
        out_specs=[_rows(tm, d), _rows(tm, n)],
        out_shape=[jax.ShapeDtypeStruct((t, d), BF16), jax.ShapeDtypeStruct((t, n), BF16)],
        compiler_params=_cparams(("parallel",), VMEM_LIMIT),
    )(h, gain, w, cos, sin)


def proj_rope_bwd(dh1, h0, cos, sin, branches, name):
    t, d = h0.shape
    tm = _tile_rows(t)
    nb = len(branches)
    n_cot = [len(b[3]) for b in branches]

    def body(*refs):
        dh1_ref, h0_ref, cos_ref, sin_ref = refs[:4]
        pos = 4
        br_refs = []
        for b in range(nb):
            br_refs.append((refs[pos], refs[pos + 1], refs[pos + 2:pos + 2 + n_cot[b]]))
            pos += 2 + n_cot[b]
        dh0_ref = refs[pos]
        dpre_refs = refs[pos + 1:pos + 1 + nb]
        gacc_ref = refs[pos + 1 + nb]
        _acc_init(gacc_ref)
        x = h0_ref[...]
        r0 = _rms_r(x)
        dh = dh1_ref[...]
        for b in range(nb):
            g_ref, w_ref, cot_refs = br_refs[b]
            n_rope = branches[b][2]
            dy = cot_refs[0][...].astype(F32)
            for c_ref in cot_refs[1:]:
                dy = dy + c_ref[...].astype(F32)
            n = dy.shape[1]
            dpre_refs[b][:, :n_rope] = _unrope(dy[:, :n_rope], cos_ref[...], sin_ref[...]).astype(BF16)
            if n_rope < n:
                dpre_refs[b][:, n_rope:] = dy[:, n_rope:].astype(BF16)
            dhn = _dot_nt(dpre_refs[b][...], w_ref[...])
            dx, dg = _rms_bwd(x, r0, g_ref[...], dhn)
            dh = dh + dx
            gacc_ref[b:b + 1, :] += dg
        dh0_ref[...] = dh

    in_specs = [_rows(tm, d), _rows(tm, d), _rows(tm, LANES), _rows(tm, LANES)]
    args = [dh1, h0, cos, sin]
    out_specs = [_rows(tm, d)]
    out_shape = [jax.ShapeDtypeStruct((t, d), F32)]
    for gain, w, _, cots in branches:
        n = w.shape[1]
        in_specs += [_const((1, d)), _const(w.shape)] + [_rows(tm, n)] * len(cots)
        args += [gain, w] + list(cots)
        out_specs.append(_rows(tm, n))
        out_shape.append(jax.ShapeDtypeStruct((t, n), BF16))
    out_specs.append(_resident((8, d)))
    out_shape.append(jax.ShapeDtypeStruct((8, d), F32))
    return pl.pallas_call(
        functools.partial(body), name=name, grid=(t // tm,),
        in_specs=in_specs, out_specs=out_specs, out_shape=out_shape,
        compiler_params=_cparams(("arbitrary",), VMEM_LIMIT),
    )(*args)


def _tri():
    row = lax.broadcasted_iota(jnp.int32, (BLOCK, BLOCK), 0)
    col = lax.broadcasted_iota(jnp.int32, (BLOCK, BLOCK), 1)
    return col <= row


def _block_diag(x):
    lo = lax.broadcasted_iota(jnp.int32, x.shape, 1) < HEAD_DIM
    zero = jnp.zeros_like(x)
    return jnp.concatenate([jnp.where(lo, x, zero), jnp.where(lo, zero, x)], axis=0)


def _dense(x, tri):
    return (jnp.where(tri, x[:, BLOCK:2 * BLOCK], x[:, :BLOCK]),
            jnp.where(tri, x[:, 3 * BLOCK:], x[:, 2 * BLOCK:3 * BLOCK]))


def _banded(xa, xb, tri):
    zero = jnp.zeros_like(xa)
    return jnp.concatenate([jnp.where(tri, zero, xa), jnp.where(tri, xa, zero),
                            jnp.where(tri, zero, xb), jnp.where(tri, xb, zero)], axis=1).astype(BF16)


def _softmax_sink(s, sink):
    mx = jnp.maximum(jnp.max(s, axis=1, keepdims=True), sink)
    e = jnp.exp(s - mx)
    es = jnp.exp(sink - mx)
    inv = 1.0 / (jnp.sum(e, axis=1, keepdims=True) + es)
    return e * inv, es * inv


def _sink_column(sink_ref):
    return jnp.concatenate([jnp.broadcast_to(sink_ref[h:h + 1, 0:1], (BLOCK, 1)) for h in range(N_HEADS)], axis=0)


def _kv_block_diag(band, kvw):
    n_lt = kvw // LANES
    return ([_block_diag(band[:, lt * LANES:(lt + 1) * LANES]) for lt in range(n_lt)],
            [_block_diag(band[:, kvw + lt * LANES:kvw + (lt + 1) * LANES]) for lt in range(n_lt)])


def _all_probs(q_ref, r0, kbd, tri, n, sink_ref):
    dense = []
    for tq in range(N_HEADS // 2):
        s = _dot_nt(q_ref[r0:r0 + BLOCK, tq * LANES:(tq + 1) * LANES], kbd[tq // GQA])
        dense += list(_dense(s, tri))
    bias = jnp.where(jnp.logical_not(tri) & (n == 0), NEG_INF, 0.0)
    s_all = jnp.concatenate(dense, axis=0) * (HEAD_DIM ** -0.5) + jnp.concatenate([bias] * N_HEADS, axis=0)
    return _softmax_sink(s_all, _sink_column(sink_ref))


def _head_rows(x, tq):
    return x[2 * tq * BLOCK:(2 * tq + 1) * BLOCK], x[(2 * tq + 1) * BLOCK:(2 * tq + 2) * BLOCK]


def _attn_sub(t):
    return ATTN_SUB if t % (ATTN_SUB * BLOCK) == 0 else 1


def swa_fwd(q, kv, sink_b):
    t, d = q.shape
    sub = _attn_sub(t)
    kvw = N_KV_HEADS * HEAD_DIM

    def body(q_ref, kvc_ref, kvp_ref, sink_ref, o_ref):
        i = pl.program_id(0)
        tri = _tri()
        ext = jnp.concatenate([kvp_ref[...], kvc_ref[...]], axis=0)
        for sb in range(sub):
            r0 = sb * BLOCK
            kbd, vbd = _kv_block_diag(ext[r0:r0 + 2 * BLOCK], kvw)
            p, _ = _all_probs(q_ref, r0, kbd, tri, i * sub + sb, sink_ref)
            for tq in range(N_HEADS // 2):
                pa, pb = _head_rows(p, tq)
                o_ref[r0:r0 + BLOCK, tq * LANES:(tq + 1) * LANES] = _dot(_banded(pa, pb, tri), vbd[tq // GQA]).astype(BF16)

    return pl.pallas_call(
        functools.partial(body), name="swa_fwd", grid=(t // (sub * BLOCK),),
        in_specs=[_rows(sub * BLOCK, d), _rows(sub * BLOCK, 2 * kvw),
                  pl.BlockSpec((BLOCK, 2 * kvw), lambda i: (jnp.maximum(i * sub - 1, 0), 0)), _const(sink_b.shape)],
        out_specs=_rows(sub * BLOCK, d),
        out_shape=jax.ShapeDtypeStruct((t, d), BF16),
        compiler_params=_cparams(("parallel",), VMEM_LIMIT),
    )(q, kv, kv, sink_b)


def swa_bwd(q, kv, do, sink_b):
    t, d = q.shape
    sub = _attn_sub(t)
    nq = t // (sub * BLOCK)
    kvw = N_KV_HEADS * HEAD_DIM

    def body(q_ref, do_ref, kvc_ref, kvp_ref, sink_ref, dq_ref, dkv_ref, dsink_ref, carry):
        i = pl.program_id(0)
        step = nq - 1 - i
        _acc_init(dsink_ref)

        @pl.when(i == 0)
        def _():
            carry[...] = jnp.zeros_like(carry)

        tri = _tri()
        lo = lax.broadcasted_iota(jnp.int32, (2 * BLOCK, LANES), 1) < HEAD_DIM
        ext = jnp.concatenate([kvp_ref[...], kvc_ref[...]], axis=0)
        dkeys = [None] * (sub + 1)
        for sb in reversed(range(sub)):
            r0 = sb * BLOCK
            kbd, vbd = _kv_block_diag(ext[r0:r0 + 2 * BLOCK], kvw)
            p, ps = _all_probs(q_ref, r0, kbd, tri, step * sub + sb, sink_ref)
            dp = []
            for tq in range(N_HEADS // 2):
                dp += list(_dense(_dot_nt(do_ref[r0:r0 + BLOCK, tq * LANES:(tq + 1) * LANES], vbd[tq // GQA]), tri))
            dp = jnp.concatenate(dp, axis=0)
            delta = jnp.sum(p * dp, axis=1, keepdims=True)
            ds = p * (dp - delta) * (HEAD_DIM ** -0.5)
            dsk = ps * delta
            for h in range(N_HEADS):
                dsink_ref[h:h + 1, :] -= jnp.sum(dsk[h * BLOCK:(h + 1) * BLOCK], axis=0, keepdims=True)
            dkb = [jnp.zeros((4 * BLOCK, LANES), F32) for _ in kbd]
            dvb = [jnp.zeros((4 * BLOCK, LANES), F32) for _ in kbd]
            for tq in range(N_HEADS // 2):
                lt = tq // GQA
                cols = slice(tq * LANES, (tq + 1) * LANES)
                dsb = _banded(*_head_rows(ds, tq), tri)
                dq_ref[r0:r0 + BLOCK, cols] = _dot(dsb, kbd[lt]).astype(BF16)
                dkb[lt] = dkb[lt] + _dot_tn(dsb, q_ref[r0:r0 + BLOCK, cols])
                dvb[lt] = dvb[lt] + _dot_tn(_banded(*_head_rows(p, tq), tri), do_ref[r0:r0 + BLOCK, cols])
            dall = jnp.concatenate([jnp.where(lo, x[:2 * BLOCK], x[2 * BLOCK:]) for x in dkb + dvb], axis=1)
            dkeys[sb + 1] = dall[BLOCK:] if dkeys[sb + 1] is None else dkeys[sb + 1] + dall[BLOCK:]
            dkeys[sb] = dall[:BLOCK]
        for sb in range(sub):
            own = dkeys[sb + 1] + carry[...] if sb == sub - 1 else dkeys[sb + 1]
            dkv_ref[sb * BLOCK:(sb + 1) * BLOCK, :] = own
        carry[...] = dkeys[0]

    rev = lambda i: (nq - 1 - i, 0)
    return pl.pallas_call(
        functools.partial(body), name="swa_bwd", grid=(nq,),
        in_specs=[pl.BlockSpec((sub * BLOCK, d), rev), pl.BlockSpec((sub * BLOCK, d), rev),
                  pl.BlockSpec((sub * BLOCK, 2 * kvw), rev),
                  pl.BlockSpec((BLOCK, 2 * kvw), lambda i: (jnp.maximum((nq - 1 - i) * sub - 1, 0), 0)),
                  _const(sink_b.shape)],
        out_specs=[pl.BlockSpec((sub * BLOCK, d), rev), pl.BlockSpec((sub * BLOCK, 2 * kvw), rev),
                   _resident(sink_b.shape)],
        out_shape=[jax.ShapeDtypeStruct((t, d), BF16), jax.ShapeDtypeStruct((t, 2 * kvw), F32),
                   jax.ShapeDtypeStruct(sink_b.shape, F32)],
        scratch_shapes=[pltpu.VMEM((BLOCK, 2 * kvw), F32)],
        compiler_params=_cparams(("arbitrary",), VMEM_LIMIT),
    )(q, do, kv, kv, sink_b)


def oproj_post_fwd(attn, w_o, h0, gpost, gffn):
    t, d = h0.shape
    tm = _tile_rows(t)

    def body(at_ref, w_ref, h0_ref, gpost_ref, gffn_ref, m_ref, h1_ref, a_ref):
        m = _dot(at_ref[...], w_ref[...])
        m_ref[...] = m.astype(BF16)
        h1 = h0_ref[...] + m * _rms_r(m) * gpost_ref[...]
        h1_ref[...] = h1
        a_ref[...] = (h1 * _rms_r(h1) * gffn_ref[...]).astype(BF16)

    return pl.pallas_call(
        functools.partial(body), name="oproj_post_fwd", grid=(t // tm,),
        in_specs=[_rows(tm, d), _const(w_o.shape), _rows(tm, d), _const((1, d)), _const((1, d))],
        out_specs=[_rows(tm, d)] * 3,
        out_shape=[jax.ShapeDtypeStruct((t, d), BF16), jax.ShapeDtypeStruct((t, d), F32),
                   jax.ShapeDtypeStruct((t, d), BF16)],
        compiler_params=_cparams(("parallel",), VMEM_LIMIT),
    )(attn, w_o, h0, gpost, gffn)


def oproj_post_bwd(dh2, da, h1, m, w_o, gpost, gffn):
    t, d = h1.shape
    tm = _tile_rows(t)

    def body(dh2_ref, da_ref, h1_ref, m_ref, w_ref, gpost_ref, gffn_ref, dh1_ref, dm_ref, dat_ref, gacc_ref):
        _acc_init(gacc_ref)
        h1v, mv = h1_ref[...], m_ref[...].astype(F32)
        dh1_n, dgffn = _rms_bwd(h1v, _rms_r(h1v), gffn_ref[...], da_ref[...])
        dh1 = dh2_ref[...] + dh1_n
        dm, dgpost = _rms_bwd(mv, _rms_r(mv), gpost_ref[...], dh1)
        dmb = dm.astype(BF16)
        dh1_ref[...] = dh1
        dm_ref[...] = dmb
        dat_ref[...] = _dot_nt(dmb, w_ref[...]).astype(BF16)
        gacc_ref[0:1, :] += dgpost
        gacc_ref[1:2, :] += dgffn

    return pl.pallas_call(
        functools.partial(body), name="oproj_post_bwd", grid=(t // tm,),
        in_specs=[_rows(tm, d)] * 4 + [_const(w_o.shape), _const((1, d)), _const((1, d))],
        out_specs=[_rows(tm, d)] * 3 + [_resident((8, d))],
        out_shape=[jax.ShapeDtypeStruct((t, d), F32), jax.ShapeDtypeStruct((t, d), BF16),
                   jax.ShapeDtypeStruct((t, d), BF16), jax.ShapeDtypeStruct((8, d), F32)],
        compiler_params=_cparams(("arbitrary",), VMEM_LIMIT),
    )(dh2, da, h1, m, w_o, gpost, gffn)


def _my_place():
    return lax.axis_index("x"), lax.axis_index("y"), lax.axis_index("c")


def _block_index(px, py, pc):
    return 4 * px + 2 * py + pc


def allgather_pieces(shards, name):
    np_ = len(shards)

    def body(*refs):
        in_refs, out_refs = refs[:np_], refs[np_:2 * np_]
        send_sems, recv_sems, local_sems = refs[2 * np_:]
        x, y, c = _my_place()
        me, sibling = (x, y, c), (x, y, 1 - c)
        chips = [(1 - x, y), (x, 1 - y), (1 - x, 1 - y)]

        def rows(p, place):
            r = in_refs[p].shape[0]
            return out_refs[p].at[pl.ds(_block_index(*place) * r, r), :]

        def copy(p, k, block, to, src=None):
            return pltpu.make_async_remote_copy(
                src_ref=rows(p, block) if src is None else src, dst_ref=rows(p, block),
                send_sem=send_sems.at[p, k], recv_sem=recv_sems.at[p, k], device_id=to, device_id_type=MESH)

        mine = [pltpu.make_async_copy(in_refs[p], rows(p, me), local_sems.at[p]) for p in range(np_)]
        first, passed = [], []
        for p in range(np_):
            mine[p].start()
            first.append(copy(p, 0, me, sibling, src=in_refs[p]))
            first += [copy(p, 1 + j, me, (*chip, c), src=in_refs[p]) for j, chip in enumerate(chips)]
        for cp in first:
            cp.start()
        for p in range(np_):
            for j, chip in enumerate(chips):
                copy(p, 1 + j, (*chip, c), me).wait_recv()
                fwd = copy(p, 4 + j, (*chip, c), sibling)
                fwd.start()
                passed.append(fwd)
        for p in range(np_):
            copy(p, 0, sibling, me).wait_recv()
            for j, chip in enumerate(chips):
                copy(p, 4 + j, (*chip, 1 - c), me).wait_recv()
        for cp in first + passed:
            cp.wait_send()
        for cp in mine:
            cp.wait()

    return pl.pallas_call(
        functools.partial(body), name=name,
        in_specs=[ANY] * np_, out_specs=[ANY] * np_,
        out_shape=[jax.ShapeDtypeStruct((N_DEV * s.shape[0], s.shape[1]), s.dtype) for s in shards],
        scratch_shapes=[pltpu.SemaphoreType.DMA((np_, 7)), pltpu.SemaphoreType.DMA((np_, 7)),
                        pltpu.SemaphoreType.DMA((np_,))],
    )(*shards)


def _peers():
    x, y, c = _my_place()
    flips = [(fx, fy, fc) for fx in (0, 1) for fy in (0, 1) for fc in (0, 1)][1:]
    return [(1 - x if fx else x, 1 - y if fy else y, 1 - c if fc else c) for fx, fy, fc in flips]


HBM = pl.BlockSpec(memory_space=pltpu.HBM)
SEM = pl.BlockSpec(memory_space=pltpu.SEMAPHORE)


def _exchange_windows(scatter, src_ref, land_ref, my_block, peer_block):
    if scatter:
        r = land_ref.shape[1]
        return src_ref.at[pl.ds(peer_block * r, r), :], land_ref.at[my_block], land_ref.at[peer_block]
    r = src_ref.shape[0]
    return src_ref, land_ref.at[pl.ds(my_block * r, r), :], land_ref.at[pl.ds(peer_block * r, r), :]


def _own_copy(scatter, src_ref, land_ref, my_block, sem):
    if scatter:
        r = land_ref.shape[1]
        return pltpu.make_async_copy(src_ref.at[pl.ds(my_block * r, r), :], land_ref.at[my_block], sem)
    r = src_ref.shape[0]
    return pltpu.make_async_copy(src_ref, land_ref.at[pl.ds(my_block * r, r), :], sem)


def exchange_start(srcs, lands, after, scatter, name):
    np_ = len(srcs)

    def body(*refs):
        src_refs, land_refs = refs[:np_], refs[np_:2 * np_]
        send_sems, recv_sems, own_sems = refs[2 * np_ + 1:2 * np_ + 4]
        token = refs[-1]
        my_block = _block_index(*_my_place())
        for p in range(np_):
            _own_copy(scatter, src_refs[p], land_refs[p], my_block, own_sems.at[p]).start()
            for k, peer in enumerate(_peers()):
                src, dst, _ = _exchange_windows(scatter, src_refs[p], land_refs[p], my_block, _block_index(*peer))
                pltpu.make_async_remote_copy(src_ref=src, dst_ref=dst, send_sem=send_sems.at[7 * p + k],
                                             recv_sem=recv_sems.at[7 * p + k], device_id=peer, device_id_type=MESH).start()
        token[...] = jnp.zeros_like(token)

    hbm = lambda a: pltpu.with_memory_space_constraint(a, pltpu.HBM)
    outs = pl.pallas_call(
        functools.partial(body), name=name,
        in_specs=[HBM] * (2 * np_) + [ANY],
        out_specs=[SEM, SEM, SEM] + [HBM] * (2 * np_) + [pl.BlockSpec(memory_space=pltpu.VMEM)],
        out_shape=[pltpu.SemaphoreType.DMA((7 * np_,)), pltpu.SemaphoreType.DMA((7 * np_,)), pltpu.SemaphoreType.DMA((np_,))]
        + [pltpu.HBM(a.shape, a.dtype) for a in list(srcs) + list(lands)] + [jax.ShapeDtypeStruct((8, LANES), F32)],
        input_output_aliases={i: 3 + i for i in range(2 * np_)},
        compiler_params=pltpu.CompilerParams(has_side_effects=pltpu.SideEffectType.DATAFLOW_SIDE_EFFECTING),
    )(*[hbm(a) for a in srcs], *[hbm(a) for a in lands], after)
    return dict(sems=outs[:3], srcs=outs[3:3 + np_], lands=outs[3 + np_:3 + 2 * np_], token=outs[-1], scatter=scatter)


def exchange_wait(started, after, name):
    srcs, lands = started["srcs"], started["lands"]
    scatter = started["scatter"]
    np_ = len(srcs)

    def body(*refs):
        src_refs, land_refs = refs[:np_], refs[np_:2 * np_]
        send_sems, recv_sems, own_sems = refs[2 * np_:2 * np_ + 3]
        my_block = _block_index(*_my_place())
        for p in range(np_):
            _own_copy(scatter, src_refs[p], land_refs[p], my_block, own_sems.at[p]).wait()
            for k, peer in enumerate(_peers()):
                src, dst, arrival = _exchange_windows(scatter, src_refs[p], land_refs[p], my_block, _block_index(*peer))
                pltpu.make_async_remote_copy(src_ref=src, dst_ref=dst, send_sem=send_sems.at[7 * p + k],
                                             recv_sem=recv_sems.at[7 * p + k], device_id=peer, device_id_type=MESH).wait_send()
                pltpu.make_async_remote_copy(src_ref=src, dst_ref=arrival, send_sem=send_sems.at[7 * p + k],
                                             recv_sem=recv_sems.at[7 * p + k], device_id=peer, device_id_type=MESH).wait_recv()

    outs = pl.pallas_call(
        functools.partial(body), name=name,
        in_specs=[HBM] * (2 * np_) + [SEM, SEM, SEM, ANY],
        out_specs=[HBM] * (2 * np_),
        out_shape=[pltpu.HBM(a.shape, a.dtype) for a in list(srcs) + list(lands)],
        input_output_aliases={i: i for i in range(2 * np_)},
        compiler_params=pltpu.CompilerParams(has_side_effects=pltpu.SideEffectType.DATAFLOW_SIDE_EFFECTING),
    )(*srcs, *lands, *started["sems"], after)
    return list(outs[np_:])


def _gather_zone(shard):
    return lax.empty((N_DEV * shard.shape[0], shard.shape[1]), shard.dtype)


def _scatter_zone(full):
    return lax.empty((N_DEV, full.shape[0] // N_DEV, full.shape[1]), full.dtype)


def allreduce_small(pack):
    r, c = pack.shape

    def body(pack_ref, out_ref, gathered, send_sems, recv_sems):
        me = _my_place()
        my_block = _block_index(*me)
        peers = _peers()

        def copy(k, slot, to):
            return pltpu.make_async_remote_copy(
                src_ref=pack_ref, dst_ref=gathered.at[slot], send_sem=send_sems.at[k], recv_sem=recv_sems.at[k],
                device_id=to, device_id_type=MESH)

        sends = [copy(k, my_block, peer) for k, peer in enumerate(peers)]
        for cp in sends:
            cp.start()
        gathered[my_block] = pack_ref[...]
        for k, peer in enumerate(peers):
            copy(k, _block_index(*peer), peer).wait_recv()
        for cp in sends:
            cp.wait_send()
        total = gathered[0]
        for j in range(1, N_DEV):
            total = total + gathered[j]
        out_ref[...] = total

    return pl.pallas_call(
        functools.partial(body), name="allreduce_small",
        in_specs=[pl.BlockSpec(memory_space=pltpu.VMEM)], out_specs=pl.BlockSpec(memory_space=pltpu.VMEM),
        out_shape=jax.ShapeDtypeStruct((r, c), F32),
        scratch_shapes=[pltpu.VMEM((N_DEV, r, c), F32), pltpu.SemaphoreType.DMA((7,)), pltpu.SemaphoreType.DMA((7,))],
    )(pack)


def sum_parts(parts):
    n, r, c = parts.shape
    br = 256 if r % 256 == 0 else r

    def body(p_ref, g_ref):
        g = p_ref[0].astype(F32)
        for j in range(1, n):
            g = g + p_ref[j].astype(F32)
        g_ref[...] = g

    return pl.pallas_call(
        functools.partial(body), name="sum_parts", grid=(r // br,),
        in_specs=[pl.BlockSpec((n, br, c), lambda i: (0, i, 0))], out_specs=_rows(br, c),
        out_shape=jax.ShapeDtypeStruct((r, c), F32),
        compiler_params=_cparams(("parallel",)),
    )(parts)


def adamw(w, m, v, g):
    nl, r, c = w.shape
    br = 256 if r % 256 == 0 else r
    blk = pl.BlockSpec((None, br, c), lambda l, i: (l, i, 0))

    def body(w_ref, m_ref, v_ref, gin_ref, g_ref, d_ref, nm_ref, nv_ref):
        g = gin_ref[...]
        nm = ADAM_B1 * m_ref[...] + (1.0 - ADAM_B1) * g
        nv = ADAM_B2 * v_ref[...] + (1.0 - ADAM_B2) * (g * g)
        m_hat = nm / (1.0 - ADAM_B1 ** ADAM_STEP)
        v_hat = nv / (1.0 - ADAM_B2 ** ADAM_STEP)
        g_ref[...] = g
        d_ref[...] = -ADAM_LR * (m_hat / (jnp.sqrt(v_hat) + ADAM_EPS) + ADAM_WD * w_ref[...])
        nm_ref[...] = nm
        nv_ref[...] = nv

    return pl.pallas_call(
        functools.partial(body), name="adamw", grid=(nl, r // br),
        in_specs=[blk] * 4, out_specs=[blk] * 4, out_shape=[jax.ShapeDtypeStruct((nl, r, c), F32)] * 4,
        compiler_params=_cparams(("parallel", "parallel")),
    )(w, m, v, g)


def _adamw_nd(w, m, v, g):
    shp = w.shape
    flat = lambda a: a.reshape((-1,) + shp[-2:])
    outs = adamw(flat(w), flat(m), flat(v), flat(g))
    return [o.reshape(shp) for o in outs]


def _pair_heads(a, axis, width=HEAD_DIM):
    shp = a.shape
    a = a.reshape(shp[:axis] + (2, 2, GQA, width) + shp[axis + 1:])
    return jnp.swapaxes(a, axis + 1, axis + 2).reshape(shp)


def _unpair_heads(a, axis, width=HEAD_DIM):
    shp = a.shape
    a = a.reshape(shp[:axis] + (2, GQA, 2, width) + shp[axis + 1:])
    return jnp.swapaxes(a, axis + 1, axis + 2).reshape(shp)


def _pad_rows(a, rows=8):
    return jnp.pad(a, ((0, rows - a.shape[0]), (0, 0)))


def kernel(x, p, mix_pre_g, mix_post_g, ffn_pre_g, ffn_post_g, pool_w, pool_scale, kv_norm_g, w_k, w_v, w_q, w_o, sinks, w_ff_gate, w_ff_up, w_ff_down, ple_norm_g, w_ple_gate, w_ple_proj, loss_target, m_mix_pre_g, m_mix_post_g, m_ffn_pre_g, m_ffn_post_g, m_pool_w, m_pool_scale, m_kv_norm_g, m_w_k, m_w_v, m_w_q, m_w_o, m_sinks, m_w_ff_gate, m_w_ff_up, m_w_ff_down, m_ple_norm_g, m_w_ple_gate, m_w_ple_proj, v_mix_pre_g, v_mix_post_g, v_ffn_pre_g, v_ffn_post_g, v_pool_w, v_pool_scale, v_kv_norm_g, v_w_k, v_w_v, v_w_q, v_w_o, v_sinks, v_w_ff_gate, v_w_ff_up, v_w_ff_down, v_ple_norm_g, v_w_ple_gate, v_w_ple_proj):
    depth = w_ff_gate.shape[0]
    n_a = pool_w.shape[0]
    t, d = x.shape[1], x.shape[2]
    h = x[0]
    tgt = loss_target[0]
    p_all = p.reshape(depth * t, p.shape[-1])
    my_block = _block_index(*_my_place())
    row = lambda g, i: g[i][None, :]
    bf = lambda a: a.astype(BF16)

    full, gathers = [None] * depth, {}
    start_tokens = jnp.zeros((), F32)
    for i in range(depth):
        shards = [bf(w_ff_gate[i].T), bf(w_ff_up[i].T), bf(w_ff_down[i]), bf(w_ple_gate[i]), bf(w_ple_proj[i].T)]
        if i == 0:
            pool0, scale_full = allgather_pieces([bf(pool_w[0].reshape(-1, POOL_GROUP)), _pad_rows(pool_scale)],
                                                 "allgather_pool0")
            order = pool0
        elif i < n_a:
            shards.append(bf(pool_w[i].reshape(-1, POOL_GROUP)))
        else:
            shards += [bf(_pair_heads(w_q[i - n_a], 1)), bf(w_o[i - n_a])]
            if i == n_a:
                shards.append(bf(jnp.concatenate([w_k, w_v], axis=1)))
        gathers[i] = exchange_start(shards, [_gather_zone(s) for s in shards], order, False, f"allgather_start_l{i}")
        order = gathers[i]["token"]
        start_tokens = start_tokens + order[0, 0]
    scale_full = scale_full.reshape(N_DEV, 8, -1)[:, :n_a].transpose(1, 0, 2).reshape(n_a, 1, d)

    cos, sin = _rope_tables(t)
    sink_b = [jnp.broadcast_to(_pair_heads(sinks[j][:, None], 0, 1), (N_HEADS, LANES)) for j in range(depth - n_a)]
    pool_full, wo_full = {}, {}

    saved = []
    kv = hk = None
    for i in range(depth):
        if i > 0:
            full[i] = exchange_wait(gathers[i], h, f"allgather_wait_l{i}")
        s = {"h0": h}
        if i < n_a:
            pool_full[i] = ((pool0 if i == 0 else full[i][5]).reshape(N_DEV, len(POOL_WINDOWS), -1, POOL_GROUP)
                            .transpose(1, 0, 2, 3).reshape(len(POOL_WINDOWS), POOL_GROUP, POOL_GROUP))
            gpre = row(mix_pre_g, i) + start_tokens if i == 0 else row(mix_pre_g, i)
            h1, a = pool_mix_fwd(h, gpre, pool_full[i], scale_full[i], row(mix_post_g, i), row(ffn_pre_g, i))
            if i == 0:
                full[0] = exchange_wait(gathers[0], h1, "allgather_wait_l0")
        else:
            j = i - n_a
            wo_full[i] = _pair_heads(full[i][6], 0)
            if i == n_a:
                hk, kv = proj_rope_fwd(h, kv_norm_g[None, :], full[i][7], cos, sin, N_KV_HEADS * HEAD_DIM, "kv_proj_fwd")
            hn, q = proj_rope_fwd(h, row(mix_pre_g, i), full[i][5], cos, sin, d, "q_proj_fwd")
            attn = swa_fwd(q, kv, sink_b[j])
            m, h1, a = oproj_post_fwd(attn, wo_full[i], h, row(mix_post_g, i), row(ffn_pre_g, i))
            s.update(hn=hn, q=q, attn=attn, m=m)
        wg_t, wu_t, wd, wpg, wpp_t = full[i][:5]
        f, gte, up, hdn = ffn_fwd(a, wg_t, wu_t, wd)
        s.update(h1=h1, a=a, f=f, gte=gte, up=up, hdn=hdn)
        if i < depth - 1:
            h = post_ple_fwd(h1, f, p_all, i, row(ffn_post_g, i), row(ple_norm_g, i), wpg, wpp_t)
        saved.append(s)

    g_mix_pre, g_mix_post, g_ffn_pre, g_ffn_post, g_ple = ([None] * depth for _ in range(5))
    g_kv = g_sinks = None
    g_scale = [None] * n_a
    landing, scatters = [None] * depth, {}
    dkv_sum = []
    scatter_token = jnp.zeros((), F32)
    for i in reversed(range(depth)):
        s = saved[i]
        wg_t, wu_t, wd, wpg, wpp_t = full[i][:5]
        last = i == depth - 1
        dh2, df, ub, dzb, dppb, gacc = post_ple_bwd(tgt if last else dh, s["h1"], s["f"], p_all, i,
                                                    row(ffn_post_g, i) + scatter_token, row(ple_norm_g, i), wpg, wpp_t,
                                                    from_target=last)
        g_ple[i], g_ffn_post[i] = gacc[0], gacc[1]
        if last:
            loss_row = gacc[2][None, :]
        da, dgte, dup = ffn_bwd_act(df, s["gte"], s["up"], wg_t, wu_t, wd)
        grads = [xty(dgte, s["a"]), xty(dup, s["a"]), xty(s["hdn"], df), xty(ub, dzb), xty(dppb, p_all, i)]
        early = exchange_start(grads, [_scatter_zone(g) for g in grads], dh2, True, f"reduce_scatter_start_l{i}a")
        early_token = early["token"][0, 0]
        if i < n_a:
            dh, dpw, gacc = pool_mix_bwd(s["h0"], dh2, da, row(mix_pre_g, i) + early_token, pool_full[i], scale_full[i],
                                         row(mix_post_g, i), row(ffn_pre_g, i))
            g_mix_pre[i], g_mix_post[i], g_ffn_pre[i], g_scale[i] = gacc[0], gacc[1], gacc[2], gacc[3]
            dpw = dpw.reshape(len(POOL_WINDOWS), N_DEV, -1, POOL_GROUP).transpose(1, 0, 2, 3)
            grads = [bf(dpw.reshape(-1, POOL_GROUP))]
        else:
            j = i - n_a
            dh1, dmb, dattn, gacc = oproj_post_bwd(dh2, da, s["h1"], s["m"], wo_full[i], row(mix_post_g, i) + early_token,
                                                   row(ffn_pre_g, i))
            g_mix_post[i], g_ffn_pre[i] = gacc[0], gacc[1]
            dq, dkv, dsink = swa_bwd(s["q"], kv, dattn, sink_b[j])
            dkv_sum.append(dkv)
            g_sinks = [_unpair_heads(dsink[:, 0:1], 0, 1)[:, 0]] + (g_sinks or [])
            branches = [(row(mix_pre_g, i), full[i][5], d, [dq])]
            if i == n_a:
                branches.append((kv_norm_g[None, :], full[i][7], N_KV_HEADS * HEAD_DIM, dkv_sum))
            outs = proj_rope_bwd(dh1, s["h0"], cos, sin, branches, f"proj_bwd_l{i}")
            dh, gacc = outs[0], outs[-1]
            g_mix_pre[i] = gacc[0]
            grads = [xty(s["hn"], outs[1]), _unpair_heads(xty(s["attn"], dmb), 0)]
            if i == n_a:
                g_kv = gacc[1]
                grads.append(xty(hk, outs[2]))
        late = exchange_start(grads, [_scatter_zone(g) for g in grads], dh, True, f"reduce_scatter_start_l{i}b")
        scatter_token = late["token"][0, 0]
        scatters[i] = (early, late)
    grad_x = dh[None]
    after = dh
    for i in reversed(range(depth)):
        landing[i] = (exchange_wait(scatters[i][0], after, f"reduce_scatter_wait_l{i}a")
                      + exchange_wait(scatters[i][1], after, f"reduce_scatter_wait_l{i}b"))
        after = landing[i][0]

    sink_row = jnp.pad(jnp.concatenate(g_sinks)[None, :], ((0, 0), (0, d - sinks.size)))
    stack = lambda rows_: _pad_rows(jnp.stack(rows_))
    pack = jnp.concatenate([stack(g_mix_pre), stack(g_mix_post), stack(g_ffn_pre), stack(g_ffn_post), stack(g_ple),
                            _pad_rows(g_kv[None]), stack(g_scale), _pad_rows(sink_row), _pad_rows(loss_row)], axis=0)
    tot = allreduce_small(pack)
    sec = lambda k, n: tot[8 * k:8 * k + n]
    loss = jnp.sum(tot[64])
    small = {
        "mix_pre_g": sec(0, depth), "mix_post_g": sec(1, depth), "ffn_pre_g": sec(2, depth),
        "ffn_post_g": sec(3, depth), "ple_norm_g": sec(4, depth), "kv_norm_g": tot[40],
        "pool_scale": lax.dynamic_slice_in_dim(sec(6, n_a), my_block * pool_scale.shape[1], pool_scale.shape[1], axis=1),
        "sinks": tot[56, :sinks.size].reshape(sinks.shape),
    }

    weights = dict(mix_pre_g=mix_pre_g, mix_post_g=mix_post_g, ffn_pre_g=ffn_pre_g, ffn_post_g=ffn_post_g, pool_w=pool_w, pool_scale=pool_scale, kv_norm_g=kv_norm_g, w_k=w_k, w_v=w_v, w_q=w_q, w_o=w_o, sinks=sinks, w_ff_gate=w_ff_gate, w_ff_up=w_ff_up, w_ff_down=w_ff_down, ple_norm_g=ple_norm_g, w_ple_gate=w_ple_gate, w_ple_proj=w_ple_proj)
    mom1 = dict(mix_pre_g=m_mix_pre_g, mix_post_g=m_mix_post_g, ffn_pre_g=m_ffn_pre_g, ffn_post_g=m_ffn_post_g, pool_w=m_pool_w, pool_scale=m_pool_scale, kv_norm_g=m_kv_norm_g, w_k=m_w_k, w_v=m_w_v, w_q=m_w_q, w_o=m_w_o, sinks=m_sinks, w_ff_gate=m_w_ff_gate, w_ff_up=m_w_ff_up, w_ff_down=m_w_ff_down, ple_norm_g=m_ple_norm_g, w_ple_gate=m_w_ple_gate, w_ple_proj=m_w_ple_proj)
    mom2 = dict(mix_pre_g=v_mix_pre_g, mix_post_g=v_mix_post_g, ffn_pre_g=v_ffn_pre_g, ffn_post_g=v_ffn_post_g, pool_w=v_pool_w, pool_scale=v_pool_scale, kv_norm_g=v_kv_norm_g, w_k=v_w_k, w_v=v_w_v, w_q=v_w_q, w_o=v_w_o, sinks=v_sinks, w_ff_gate=v_w_ff_gate, w_ff_up=v_w_ff_up, w_ff_down=v_w_ff_down, ple_norm_g=v_ple_norm_g, w_ple_gate=v_w_ple_gate, w_ple_proj=v_w_ple_proj)

    def land(i, k):
        return sum_parts(landing[i][k])

    gw = dict(small)
    gw["kv_norm_g"] = small["kv_norm_g"]
    gw["w_ff_gate"] = jnp.stack([land(i, 0).T for i in range(depth)])
    gw["w_ff_up"] = jnp.stack([land(i, 1).T for i in range(depth)])
    gw["w_ff_down"] = jnp.stack([land(i, 2) for i in range(depth)])
    gw["w_ple_gate"] = jnp.stack([land(i, 3) for i in range(depth)])
    gw["w_ple_proj"] = jnp.stack([land(i, 4).T for i in range(depth)])
    gw["pool_w"] = jnp.stack([land(i, 5).reshape(pool_w.shape[1:]) for i in range(n_a)])
    gw["w_q"] = jnp.stack([_unpair_heads(land(i, 5), 1) for i in range(n_a, depth)])
    gw["w_o"] = jnp.stack([land(i, 6) for i in range(n_a, depth)])
    gkv = land(n_a, 7)
    gw["w_k"], gw["w_v"] = gkv[:, :w_k.shape[1]], gkv[:, w_k.shape[1]:]

    order = ["mix_pre_g", "mix_post_g", "ffn_pre_g", "ffn_post_g", "pool_w", "pool_scale", "kv_norm_g", "w_k", "w_v",
             "w_q", "w_o", "sinks", "w_ff_gate", "w_ff_up", "w_ff_down", "ple_norm_g", "w_ple_gate", "w_ple_proj"]
    g_out, d_out, m_out, v_out = [], [], [], []
    for nme in order:
        w = weights[nme]
        as2d = (lambda a: a[None, :]) if w.ndim == 1 else (lambda a: a)
        g, dl, nm, nv = _adamw_nd(as2d(w), as2d(mom1[nme]), as2d(mom2[nme]), as2d(gw[nme]))
        for lst, val in ((g_out, g), (d_out, dl), (m_out, nm), (v_out, nv)):
            lst.append(val.reshape(w.shape))
    return (loss, grad_x, *g_out, *d_out, *m_out, *v_out)
```

```python
import functools

import jax
import jax.numpy as jnp
from jax import lax
from jax.experimental import pallas as pl
from jax.experimental.pallas import tpu as pltpu

F32 = jnp.float32
BF16 = jnp.bfloat16

N_DEV = 8
HEAD_DIM = 64
N_HEADS = 16
N_KV_HEADS = 4
GQA = N_HEADS // N_KV_HEADS
BLOCK = 128
POOL_WINDOWS = (2, 4, 8, 16)
POOL_GROUP = 256
HALO = 16
ROPE_THETA = 10000.0
RMS_EPS = 1e-6
NEG_INF = -1e30
LANES = 128
ATTN_SUB = 8
XTY_ROWS = 2048
FFN_CHUNK = 768
VMEM_LIMIT = 56 * 1024 * 1024

ADAM_LR = 0.001
ADAM_B1 = 0.9
ADAM_B2 = 0.999
ADAM_EPS = 1e-08
ADAM_WD = 0.01
ADAM_STEP = 10

MESH = pl.DeviceIdType.MESH
ANY = pl.BlockSpec(memory_space=pl.ANY)

NT_DIMS = (((1,), (1,)), ((), ()))
TN_DIMS = (((0,), (0,)), ((), ()))


def _cparams(sem=None, vmem=None):
    kw = {}
    if sem is not None:
        kw["dimension_semantics"] = sem
    if vmem is not None:
        kw["vmem_limit_bytes"] = vmem
    return pltpu.CompilerParams(**kw)


def _rows(tm, n, first=0):
    return pl.BlockSpec((tm, n), lambda i: (i + first, 0))


def _rows_rev(tm, n, nt):
    return pl.BlockSpec((tm, n), lambda i: (nt - 1 - i, 0))


def _const(shape):
    nd = len(shape)
    return pl.BlockSpec(shape, lambda *_: (0,) * nd, pipeline_mode=pl.Buffered(1))


def _resident(shape):
    nd = len(shape)
    return pl.BlockSpec(shape, lambda *_: (0,) * nd)


def _tile_rows(t):
    return 512 if t % 512 == 0 else 128


def _dot(a, b):
    return jnp.dot(a, b, preferred_element_type=F32)


def _dot_nt(a, b):
    return lax.dot_general(a, b, NT_DIMS, preferred_element_type=F32)


def _dot_tn(a, b):
    return lax.dot_general(a, b, TN_DIMS, preferred_element_type=F32)


def _rms_r(x):
    return lax.rsqrt(jnp.mean(x * x, axis=-1, keepdims=True) + RMS_EPS)


def _rms_bwd(x, r, g, dy):
    gy = dy * g
    dx = r * gy - x * (r * r * r * jnp.mean(gy * x, axis=-1, keepdims=True))
    dg = jnp.sum(dy * (x * r), axis=0, keepdims=True)
    return dx, dg


def _sigmoid(x):
    return jax.nn.sigmoid(x)


def _rope_tables(t):
    inv = 1.0 / (ROPE_THETA ** (jnp.arange(0, HEAD_DIM, 2, dtype=F32) / HEAD_DIM))
    ang = jnp.arange(t, dtype=F32)[:, None] * jnp.tile(inv, 2 * LANES // HEAD_DIM)[None, :]
    sign = jnp.tile(jnp.repeat(jnp.array([-1.0, 1.0], F32), HEAD_DIM // 2), LANES // HEAD_DIM)
    return jnp.cos(ang), jnp.sin(ang) * sign[None, :]


def _swap_halves(x):
    n = x.shape[1]
    lane = lax.broadcasted_iota(jnp.int32, x.shape, 1)
    first = (lane % HEAD_DIM) < (HEAD_DIM // 2)
    return jnp.where(first, pltpu.roll(x, n - HEAD_DIM // 2, 1), pltpu.roll(x, HEAD_DIM // 2, 1))


def _rope(x, cos, sin):
    reps = x.shape[1] // LANES
    return x * jnp.tile(cos, (1, reps)) + _swap_halves(x) * jnp.tile(sin, (1, reps))


def _unrope(dy, cos, sin):
    reps = dy.shape[1] // LANES
    return dy * jnp.tile(cos, (1, reps)) + _swap_halves(dy * jnp.tile(sin, (1, reps)))


def _acc_init(acc_ref):
    @pl.when(pl.program_id(0) == 0)
    def _():
        acc_ref[...] = jnp.zeros_like(acc_ref)


def _window_sums(ext, tm, forward):
    n = tm + HALO
    out = []
    for g, w in enumerate(POOL_WINDOWS):
        s = ext[:, g * POOL_GROUP:(g + 1) * POOL_GROUP]
        k = 1
        while k < w:
            s = s + pltpu.roll(s, k if forward else n - k, 0)
            k *= 2
        out.append(s[HALO:, :] if forward else s[:tm, :])
    return out


def _pool_inv_counts(tile, tm):
    t = tile * tm + lax.broadcasted_iota(jnp.int32, (tm, 1), 0)
    return [1.0 / jnp.minimum(t + 1, w).astype(F32) for w in POOL_WINDOWS]


def _pool_mix(hn, ext, inv_cnts, pw_ref, scale, tm):
    sums = _window_sums(ext, tm, True)
    pooled, ys = [], []
    for g in range(len(POOL_WINDOWS)):
        pg = (sums[g] * inv_cnts[g] - hn[:, g * POOL_GROUP:(g + 1) * POOL_GROUP]).astype(BF16)
        pooled.append(pg)
        ys.append(_dot(pg, pw_ref[g]))
    y = jnp.concatenate(ys, axis=1)
    return pooled, y, y * scale


def pool_mix_fwd(h0, gpre, pool_w, scale, gpost, gffn):
    t, d = h0.shape
    tm = _tile_rows(t)

    def body(h_ref, gpre_ref, pw_ref, scale_ref, gpost_ref, gffn_ref, h1_ref, a_ref, carry):
        i = pl.program_id(0)

        @pl.when(i == 0)
        def _():
            carry[...] = jnp.zeros_like(carry)

        x = h_ref[...]
        hn = x * _rms_r(x) * gpre_ref[...]
        ext = jnp.concatenate([carry[...], hn], axis=0)
        carry[...] = hn[tm - HALO:, :]
        _, _, m = _pool_mix(hn, ext, _pool_inv_counts(i, tm), pw_ref, scale_ref[...], tm)
        h1 = x + m * _rms_r(m) * gpost_ref[...]
        h1_ref[...] = h1
        a_ref[...] = (h1 * _rms_r(h1) * gffn_ref[...]).astype(BF16)

    return pl.pallas_call(
        functools.partial(body), name="pool_mix_fwd", grid=(t // tm,),
        in_specs=[_rows(tm, d), _const((1, d)), _const(pool_w.shape), _const((1, d)), _const((1, d)), _const((1, d))],
        out_specs=[_rows(tm, d), _rows(tm, d)],
        out_shape=[jax.ShapeDtypeStruct((t, d), F32), jax.ShapeDtypeStruct((t, d), BF16)],
        scratch_shapes=[pltpu.VMEM((HALO, d), F32)],
        compiler_params=_cparams(("arbitrary",), VMEM_LIMIT),
    )(h0, gpre, pool_w, scale, gpost, gffn)


def pool_mix_bwd(h0, dh2, da, gpre, pool_w, scale, gpost, gffn):
    t, d = h0.shape
    tm = _tile_rows(t)
    nt = t // tm
    hb = tm // HALO

    def body(h_ref, halo_ref, dh2_ref, da_ref, gpre_ref, pw_ref, scale_ref, gpost_ref, gffn_ref,
             dh0_ref, dpw_ref, gacc_ref, carry):
        i = pl.program_id(0)
        tile = nt - 1 - i
        _acc_init(gacc_ref)
        _acc_init(dpw_ref)

        @pl.when(i == 0)
        def _():
            carry[...] = jnp.zeros_like(carry)

        x = h_ref[...]
        gpre_v, scale_v, gpost_v, gffn_v = gpre_ref[...], scale_ref[...], gpost_ref[...], gffn_ref[...]
        r0 = _rms_r(x)
        hn = x * r0 * gpre_v
        xh = halo_ref[...]
        hn_halo = jnp.where(tile > 0, xh * _rms_r(xh) * gpre_v, 0.0)
        ext = jnp.concatenate([hn_halo, hn], axis=0)
        inv_cnts = _pool_inv_counts(tile, tm)
        pooled, y, m = _pool_mix(hn, ext, inv_cnts, pw_ref, scale_v, tm)
        rm = _rms_r(m)
        h1 = x + m * rm * gpost_v
        dh1_n, dgffn = _rms_bwd(h1, _rms_r(h1), gffn_v, da_ref[...])
        dh1 = dh2_ref[...] + dh1_n
        dm, dgpost = _rms_bwd(m, rm, gpost_v, dh1)
        dscale = jnp.sum(dm * y, axis=0, keepdims=True)
        dy = (dm * scale_v).astype(BF16)
        dpn = []
        for g in range(len(POOL_WINDOWS)):
            dyg = dy[:, g * POOL_GROUP:(g + 1) * POOL_GROUP]
            dpw_ref[g] += _dot_tn(pooled[g], dyg)
            dpn.append(_dot_nt(dyg, pw_ref[g]))
        dpooled = jnp.concatenate(dpn, axis=1)
        dpc = jnp.concatenate([dpn[g] * inv_cnts[g] for g in range(len(POOL_WINDOWS))], axis=1)
        ext2 = jnp.concatenate([dpc, carry[...]], axis=0)
        carry[...] = dpc[:HALO, :]
        dhn = jnp.concatenate(_window_sums(ext2, tm, False), axis=1) - dpooled
        dh0_n, dgpre = _rms_bwd(x, r0, gpre_v, dhn)
        dh0_ref[...] = dh1 + dh0_n
        gacc_ref[0:1, :] += dgpre
        gacc_ref[1:2, :] += dgpost
        gacc_ref[2:3, :] += dgffn
        gacc_ref[3:4, :] += dscale

    return pl.pallas_call(
        functools.partial(body), name="pool_mix_bwd", grid=(nt,),
        in_specs=[_rows_rev(tm, d, nt),
                  pl.BlockSpec((HALO, d), lambda i: (jnp.maximum((nt - 1 - i) * hb - 1, 0), 0)),
                  _rows_rev(tm, d, nt), _rows_rev(tm, d, nt),
                  _const((1, d)), _const(pool_w.shape), _const((1, d)), _const((1, d)), _const((1, d))],
        out_specs=[_rows_rev(tm, d, nt), _resident(pool_w.shape), _resident((8, d))],
        out_shape=[jax.ShapeDtypeStruct((t, d), F32), jax.ShapeDtypeStruct(pool_w.shape, F32),
                   jax.ShapeDtypeStruct((8, d), F32)],
        scratch_shapes=[pltpu.VMEM((HALO, d), F32)],
        compiler_params=_cparams(("arbitrary",), VMEM_LIMIT),
    )(h0, h0, dh2, da, gpre, pool_w, scale, gpost, gffn)


def _ffn_chunks(f):
    return [(c, min(c + FFN_CHUNK, f)) for c in range(0, f, FFN_CHUNK)]


def ffn_fwd(a, wg_t, wu_t, wd):
    t, d = a.shape
    f = wd.shape[0]
    tm = _tile_rows(t)

    def body(a_ref, wg_ref, wu_ref, wd_ref, f_ref, gte_ref, up_ref, hdn_ref):
        av = a_ref[...]
        acc = jnp.zeros((tm, d), F32)
        for c0, c1 in _ffn_chunks(f):
            gte = _dot_nt(av, wg_ref[c0:c1, :])
            up = _dot_nt(av, wu_ref[c0:c1, :])
            gte_ref[:, c0:c1] = gte.astype(BF16)
            up_ref[:, c0:c1] = up.astype(BF16)
            hdn = (gte * _sigmoid(gte) * up).astype(BF16)
            hdn_ref[:, c0:c1] = hdn
            acc = acc + _dot(hdn, wd_ref[c0:c1, :])
        f_ref[...] = acc.astype(BF16)

    return pl.pallas_call(
        functools.partial(body), name="ffn_fwd", grid=(t // tm,),
        in_specs=[_rows(tm, d), _const((f, d)), _const((f, d)), _const((f, d))],
        out_specs=[_rows(tm, d), _rows(tm, f), _rows(tm, f), _rows(tm, f)],
        out_shape=[jax.ShapeDtypeStruct((t, d), BF16)] + [jax.ShapeDtypeStruct((t, f), BF16)] * 3,
        compiler_params=_cparams(("parallel",), VMEM_LIMIT),
    )(a, wg_t, wu_t, wd)


def ffn_bwd_act(df, gte, up, wg_t, wu_t, wd):
    t, d = df.shape
    f = wd.shape[0]
    tm = _tile_rows(t)

    def body(df_ref, gte_ref, up_ref, wg_ref, wu_ref, wd_ref, da_ref, dgte_ref, dup_ref):
        dfv = df_ref[...]
        chunks = _ffn_chunks(f)
        half = chunks[len(chunks) // 2][0]
        acc = None
        for c0, c1 in chunks:
            g = gte_ref[:, c0:c1].astype(F32)
            u = up_ref[:, c0:c1].astype(F32)
            sg = _sigmoid(g)
            sl = g * sg
            dh = _dot_nt(dfv, wd_ref[c0:c1, :])
            dup_ref[:, c0:c1] = (dh * sl).astype(BF16)
            dgte_ref[:, c0:c1] = (dh * u * (sg * (1.0 + g * (1.0 - sg)))).astype(BF16)
            if c1 == half:
                acc = _dot(dgte_ref[:, :half], wg_ref[:half, :]) + _dot(dup_ref[:, :half], wu_ref[:half, :])
        da_ref[...] = acc + _dot(dgte_ref[:, half:], wg_ref[half:, :]) + _dot(dup_ref[:, half:], wu_ref[half:, :])

    return pl.pallas_call(
        functools.partial(body), name="ffn_bwd_act", grid=(t // tm,),
        in_specs=[_rows(tm, d), _rows(tm, f), _rows(tm, f), _const((f, d)), _const((f, d)), _const((f, d))],
        out_specs=[_rows(tm, d), _rows(tm, f), _rows(tm, f)],
        out_shape=[jax.ShapeDtypeStruct((t, d), F32)] + [jax.ShapeDtypeStruct((t, f), BF16)] * 2,
        compiler_params=_cparams(("parallel",), VMEM_LIMIT),
    )(df, gte, up, wg_t, wu_t, wd)


def xty(x, y, y_part=0):
    t, nx = x.shape
    ny = y.shape[1]
    tk = XTY_ROWS if t % XTY_ROWS == 0 else _tile_rows(t)
    bn = nx // 2 if nx > 1024 else nx
    nk = t // tk

    def body(x_ref, y_ref, o_ref, acc):
        k = pl.program_id(1)

        @pl.when(k == 0)
        def _():
            acc[...] = jnp.zeros_like(acc)

        acc[...] += _dot_tn(x_ref[...].astype(BF16), y_ref[...].astype(BF16))

        @pl.when(k == nk - 1)
        def _():
            o_ref[...] = acc[...].astype(BF16)

    return pl.pallas_call(
        functools.partial(body), name="xty", grid=(nx // bn, nk),
        in_specs=[pl.BlockSpec((tk, bn), lambda j, k: (k, j)),
                  pl.BlockSpec((tk, ny), lambda j, k: (k + y_part * nk, 0))],
        out_specs=pl.BlockSpec((bn, ny), lambda j, k: (j, 0)),
        out_shape=jax.ShapeDtypeStruct((nx, ny), BF16),
        scratch_shapes=[pltpu.VMEM((bn, ny), F32)],
        compiler_params=_cparams(("parallel", "arbitrary"), VMEM_LIMIT),
    )(x, y)


def _ple_fwd_tile(h1, f, p, gpost, gple, wpg_ref, wpp_ref):
    rf = _rms_r(f)
    h2 = h1 + f * rf * gpost
    r2 = _rms_r(h2)
    ub = (h2 * r2 * gple).astype(BF16)
    gate = _sigmoid(_dot(ub, wpg_ref[...]))
    pp = _dot_nt(p.astype(BF16), wpp_ref[...])
    return rf, h2, r2, ub, gate, pp


def post_ple_fwd(h1, f, p, layer, gpost, gple, wpg, wpp_t):
    t, d = h1.shape
    pd = p.shape[1]
    tm = _tile_rows(t)

    def body(h1_ref, f_ref, p_ref, gpost_ref, gple_ref, wpg_ref, wpp_ref, out_ref):
        _, h2, _, _, gate, pp = _ple_fwd_tile(h1_ref[...], f_ref[...].astype(F32), p_ref[...], gpost_ref[...],
                                              gple_ref[...], wpg_ref, wpp_ref)
        out_ref[...] = h2 + pp * gate

    return pl.pallas_call(
        functools.partial(body), name="post_ple_fwd", grid=(t // tm,),
        in_specs=[_rows(tm, d), _rows(tm, d), _rows(tm, pd, layer * (t // tm)), _const((1, d)), _const((1, d)),
                  _const(wpg.shape), _const(wpp_t.shape)],
        out_specs=_rows(tm, d), out_shape=jax.ShapeDtypeStruct((t, d), F32),
        compiler_params=_cparams(("parallel",), VMEM_LIMIT),
    )(h1, f, p, gpost, gple, wpg, wpp_t)


def post_ple_bwd(dh3, h1, f, p, layer, gpost, gple, wpg, wpp_t, from_target=False):
    t, d = h1.shape
    pd = p.shape[1]
    tm = _tile_rows(t)

    def body(dh3_ref, h1_ref, f_ref, p_ref, gpost_ref, gple_ref, wpg_ref, wpp_ref,
             dh2_ref, df_ref, u_ref, dz_ref, dpp_ref, gacc_ref):
        _acc_init(gacc_ref)
        fv = f_ref[...].astype(F32)
        gpost_v, gple_v = gpost_ref[...], gple_ref[...]
        rf, h2, r2, ub, gate, pp = _ple_fwd_tile(h1_ref[...], fv, p_ref[...], gpost_v, gple_v, wpg_ref, wpp_ref)
        if from_target:
            err = h2 + pp * gate - dh3_ref[...]
            dh3v = err * (1.0 / d)
            gacc_ref[2:3, :] += jnp.sum(err * err, axis=0, keepdims=True) * (0.5 / d)
        else:
            dh3v = dh3_ref[...]
        dpp_ref[...] = (dh3v * gate).astype(BF16)
        dz = (dh3v * pp * gate * (1.0 - gate)).astype(BF16)
        dz_ref[...] = dz
        u_ref[...] = ub
        du = _dot_nt(dz, wpg_ref[...])
        dh2_n, dgple = _rms_bwd(h2, r2, gple_v, du)
        dh2 = dh3v + dh2_n
        df, dgpost = _rms_bwd(fv, rf, gpost_v, dh2)
        dh2_ref[...] = dh2
        df_ref[...] = df.astype(BF16)
        gacc_ref[0:1, :] += dgple
        gacc_ref[1:2, :] += dgpost

    return pl.pallas_call(
        functools.partial(body), name="post_ple_loss_bwd" if from_target else "post_ple_bwd", grid=(t // tm,),
        in_specs=[_rows(tm, d), _rows(tm, d), _rows(tm, d), _rows(tm, pd, layer * (t // tm)), _const((1, d)),
                  _const((1, d)), _const(wpg.shape), _const(wpp_t.shape)],
        out_specs=[_rows(tm, d)] * 5 + [_resident((8, d))],
        out_shape=[jax.ShapeDtypeStruct((t, d), F32)] + [jax.ShapeDtypeStruct((t, d), BF16)] * 4
        + [jax.ShapeDtypeStruct((8, d), F32)],
        compiler_params=_cparams(("arbitrary",), VMEM_LIMIT),
    )(dh3, h1, f, p, gpost, gple, wpg, wpp_t)


def proj_rope_fwd(h, gain, w, cos, sin, n_rope, name):
    t, d = h.shape
    n = w.shape[1]
    tm = _tile_rows(t)

    def body(h_ref, g_ref, w_ref, cos_ref, sin_ref, hn_ref, y_ref):
        x = h_ref[...]
        hn = (x * _rms_r(x) * g_ref[...]).astype(BF16)
        hn_ref[...] = hn
        y = _dot(hn, w_ref[...])
        y_ref[:, :n_rope] = _rope(y[:, :n_rope], cos_ref[...], sin_ref[...]).astype(BF16)
        if n_rope < n:
            y_ref[:, n_rope:] = y[:, n_rope:].astype(BF16)

    return pl.pallas_call(
        functools.partial(body), name=name, grid=(t // tm,),
        in_specs=[_rows(tm, d), _const((1, d)), _const(w.shape), _rows(tm, LANES), _rows(tm, LANES)],
        out_specs=[_rows(tm, d), _rows(tm, n)],
        out_shape=[jax.ShapeDtypeStruct((t, d), BF16), jax.ShapeDtypeStruct((t, n), BF16)],
        compiler_params=_cparams(("parallel",), VMEM_LIMIT),
    )(h, gain, w, cos, sin)


def proj_rope_bwd(dh1, h0, cos, sin, branches, name):
    t, d = h0.shape
    tm = _tile_rows(t)
    nb = len(branches)
    n_cot = [len(b[3]) for b in branches]

    def body(*refs):
        dh1_ref, h0_ref, cos_ref, sin_ref = refs[:4]
        pos = 4
        br_refs = []
        for b in range(nb):
            br_refs.append((refs[pos], refs[pos + 1], refs[pos + 2:pos + 2 + n_cot[b]]))
            pos += 2 + n_cot[b]
        dh0_ref = refs[pos]
        dpre_refs = refs[pos + 1:pos + 1 + nb]
        gacc_ref = refs[pos + 1 + nb]
        _acc_init(gacc_ref)
        x = h0_ref[...]
        r0 = _rms_r(x)
        dh = dh1_ref[...]
        for b in range(nb):
            g_ref, w_ref, cot_refs = br_refs[b]
            n_rope = branches[b][2]
            dy = cot_refs[0][...].astype(F32)
            for c_ref in cot_refs[1:]:
                dy = dy + c_ref[...].astype(F32)
            n = dy.shape[1]
            dpre_refs[b][:, :n_rope] = _unrope(dy[:, :n_rope], cos_ref[...], sin_ref[...]).astype(BF16)
            if n_rope < n:
                dpre_refs[b][:, n_rope:] = dy[:, n_rope:].astype(BF16)
            dhn = _dot_nt(dpre_refs[b][...], w_ref[...])
            dx, dg = _rms_bwd(x, r0, g_ref[...], dhn)
            dh = dh + dx
            gacc_ref[b:b + 1, :] += dg
        dh0_ref[...] = dh

    in_specs = [_rows(tm, d), _rows(tm, d), _rows(tm, LANES), _rows(tm, LANES)]
    args = [dh1, h0, cos, sin]
    out_specs = [_rows(tm, d)]
    out_shape = [jax.ShapeDtypeStruct((t, d), F32)]
    for gain, w, _, cots in branches:
        n = w.shape[1]
        in_specs += [_const((1, d)), _const(w.shape)] + [_rows(tm, n)] * len(cots)
        args += [gain, w] + list(cots)
        out_specs.append(_rows(tm, n))
        out_shape.append(jax.ShapeDtypeStruct((t, n), BF16))
    out_specs.append(_resident((8, d)))
    out_shape.append(jax.ShapeDtypeStruct((8, d), F32))
    return pl.pallas_call(
        functools.partial(body), name=name, grid=(t // tm,),
        in_specs=in_specs, out_specs=out_specs, out_shape=out_shape,
        compiler_params=_cparams(("arbitrary",), VMEM_LIMIT),
    )(*args)


def _tri():
    row = lax.broadcasted_iota(jnp.int32, (BLOCK, BLOCK), 0)
    col = lax.broadcasted_iota(jnp.int32, (BLOCK, BLOCK), 1)
    return col <= row


def _block_diag(x):
    lo = lax.broadcasted_iota(jnp.int32, x.shape, 1) < HEAD_DIM
    zero = jnp.zeros_like(x)
    return jnp.concatenate([jnp.where(lo, x, zero), jnp.where(lo, zero, x)], axis=0)


def _dense(x, tri):
    return (jnp.where(tri, x[:, BLOCK:2 * BLOCK], x[:, :BLOCK]),
            jnp.where(tri, x[:, 3 * BLOCK:], x[:, 2 * BLOCK:3 * BLOCK]))


def _banded(xa, xb, tri):
    zero = jnp.zeros_like(xa)
    return jnp.concatenate([jnp.where(tri, zero, xa), jnp.where(tri, xa, zero),
                            jnp.where(tri, zero, xb), jnp.where(tri, xb, zero)], axis=1).astype(BF16)


def _softmax_sink(s, sink):
    mx = jnp.maximum(jnp.max(s, axis=1, keepdims=True), sink)
    e = jnp.exp(s - mx)
    es = jnp.exp(sink - mx)
    inv = 1.0 / (jnp.sum(e, axis=1, keepdims=True) + es)
    return e * inv, es * inv


def _sink_column(sink_ref):
    return jnp.concatenate([jnp.broadcast_to(sink_ref[h:h + 1, 0:1], (BLOCK, 1)) for h in range(N_HEADS)], axis=0)


def _kv_block_diag(band, kvw):
    n_lt = kvw // LANES
    return ([_block_diag(band[:, lt * LANES:(lt + 1) * LANES]) for lt in range(n_lt)],
            [_block_diag(band[:, kvw + lt * LANES:kvw + (lt + 1) * LANES]) for lt in range(n_lt)])


def _all_probs(q_ref, r0, kbd, tri, n, sink_ref):
    dense = []
    for tq in range(N_HEADS // 2):
        s = _dot_nt(q_ref[r0:r0 + BLOCK, tq * LANES:(tq + 1) * LANES], kbd[tq // GQA])
        dense += list(_dense(s, tri))
    bias = jnp.where(jnp.logical_not(tri) & (n == 0), NEG_INF, 0.0)
    s_all = jnp.concatenate(dense, axis=0) * (HEAD_DIM ** -0.5) + jnp.concatenate([bias] * N_HEADS, axis=0)
    return _softmax_sink(s_all, _sink_column(sink_ref))


def _head_rows(x, tq):
    return x[2 * tq * BLOCK:(2 * tq + 1) * BLOCK], x[(2 * tq + 1) * BLOCK:(2 * tq + 2) * BLOCK]


def _attn_sub(t):
    return ATTN_SUB if t % (ATTN_SUB * BLOCK) == 0 else 1


def swa_fwd(q, kv, sink_b):
    t, d = q.shape
    sub = _attn_sub(t)
    kvw = N_KV_HEADS * HEAD_DIM

    def body(q_ref, kvc_ref, kvp_ref, sink_ref, o_ref):
        i = pl.program_id(0)
        tri = _tri()
        ext = jnp.concatenate([kvp_ref[...], kvc_ref[...]], axis=0)
        for sb in range(sub):
            r0 = sb * BLOCK
            kbd, vbd = _kv_block_diag(ext[r0:r0 + 2 * BLOCK], kvw)
            p, _ = _all_probs(q_ref, r0, kbd, tri, i * sub + sb, sink_ref)
            for tq in range(N_HEADS // 2):
                pa, pb = _head_rows(p, tq)
                o_ref[r0:r0 + BLOCK, tq * LANES:(tq + 1) * LANES] = _dot(_banded(pa, pb, tri), vbd[tq // GQA]).astype(BF16)

    return pl.pallas_call(
        functools.partial(body), name="swa_fwd", grid=(t // (sub * BLOCK),),
        in_specs=[_rows(sub * BLOCK, d), _rows(sub * BLOCK, 2 * kvw),
                  pl.BlockSpec((BLOCK, 2 * kvw), lambda i: (jnp.maximum(i * sub - 1, 0), 0)), _const(sink_b.shape)],
        out_specs=_rows(sub * BLOCK, d),
        out_shape=jax.ShapeDtypeStruct((t, d), BF16),
        compiler_params=_cparams(("parallel",), VMEM_LIMIT),
    )(q, kv, kv, sink_b)


def swa_bwd(q, kv, do, sink_b):
    t, d = q.shape
    sub = _attn_sub(t)
    nq = t // (sub * BLOCK)
    kvw = N_KV_HEADS * HEAD_DIM

    def body(q_ref, do_ref, kvc_ref, kvp_ref, sink_ref, dq_ref, dkv_ref, dsink_ref, carry):
        i = pl.program_id(0)
        step = nq - 1 - i
        _acc_init(dsink_ref)

        @pl.when(i == 0)
        def _():
            carry[...] = jnp.zeros_like(carry)

        tri = _tri()
        lo = lax.broadcasted_iota(jnp.int32, (2 * BLOCK, LANES), 1) < HEAD_DIM
        ext = jnp.concatenate([kvp_ref[...], kvc_ref[...]], axis=0)
        dkeys = [None] * (sub + 1)
        for sb in reversed(range(sub)):
            r0 = sb * BLOCK
            kbd, vbd = _kv_block_diag(ext[r0:r0 + 2 * BLOCK], kvw)
            p, ps = _all_probs(q_ref, r0, kbd, tri, step * sub + sb, sink_ref)
            dp = []
            for tq in range(N_HEADS // 2):
                dp += list(_dense(_dot_nt(do_ref[r0:r0 + BLOCK, tq * LANES:(tq + 1) * LANES], vbd[tq // GQA]), tri))
            dp = jnp.concatenate(dp, axis=0)
            delta = jnp.sum(p * dp, axis=1, keepdims=True)
            ds = p * (dp - delta) * (HEAD_DIM ** -0.5)
            dsk = ps * delta
            for h in range(N_HEADS):
                dsink_ref[h:h + 1, :] -= jnp.sum(dsk[h * BLOCK:(h + 1) * BLOCK], axis=0, keepdims=True)
            dkb = [jnp.zeros((4 * BLOCK, LANES), F32) for _ in kbd]
            dvb = [jnp.zeros((4 * BLOCK, LANES), F32) for _ in kbd]
            for tq in range(N_HEADS // 2):
                lt = tq // GQA
                cols = slice(tq * LANES, (tq + 1) * LANES)
                dsb = _banded(*_head_rows(ds, tq), tri)
                dq_ref[r0:r0 + BLOCK, cols] = _dot(dsb, kbd[lt]).astype(BF16)
                dkb[lt] = dkb[lt] + _dot_tn(dsb, q_ref[r0:r0 + BLOCK, cols])
                dvb[lt] = dvb[lt] + _dot_tn(_banded(*_head_rows(p, tq), tri), do_ref[r0:r0 + BLOCK, cols])
            dall = jnp.concatenate([jnp.where(lo, x[:2 * BLOCK], x[2 * BLOCK:]) for x in dkb + dvb], axis=1)
            dkeys[sb + 1] = dall[BLOCK:] if dkeys[sb + 1] is None else dkeys[sb + 1] + dall[BLOCK:]
            dkeys[sb] = dall[:BLOCK]
        for sb in range(sub):
            own = dkeys[sb + 1] + carry[...] if sb == sub - 1 else dkeys[sb + 1]
            dkv_ref[sb * BLOCK:(sb + 1) * BLOCK, :] = own
        carry[...] = dkeys[0]

    rev = lambda i: (nq - 1 - i, 0)
    return pl.pallas_call(
        functools.partial(body), name="swa_bwd", grid=(nq,),
        in_specs=[pl.BlockSpec((sub * BLOCK, d), rev), pl.BlockSpec((sub * BLOCK, d), rev),
                  pl.BlockSpec((sub * BLOCK, 2 * kvw), rev),
                  pl.BlockSpec((BLOCK, 2 * kvw), lambda i: (jnp.maximum((nq - 1 - i) * sub - 1, 0), 0)),
                  _const(sink_b.shape)],
        out_specs=[pl.BlockSpec((sub * BLOCK, d), rev), pl.BlockSpec((sub * BLOCK, 2 * kvw), rev),
                   _resident(sink_b.shape)],
        out_shape=[jax.ShapeDtypeStruct((t, d), BF16), jax.ShapeDtypeStruct((t, 2 * kvw), F32),
                   jax.ShapeDtypeStruct(sink_b.shape, F32)],
        scratch_shapes=[pltpu.VMEM((BLOCK, 2 * kvw), F32)],
        compiler_params=_cparams(("arbitrary",), VMEM_LIMIT),
    )(q, do, kv, kv, sink_b)


def oproj_post_fwd(attn, w_o, h0, gpost, gffn):
    t, d = h0.shape
    tm = _tile_rows(t)

    def body(at_ref, w_ref, h0_ref, gpost_ref, gffn_ref, m_ref, h1_ref, a_ref):
        m = _dot(at_ref[...], w_ref[...])
        m_ref[...] = m.astype(BF16)
        h1 = h0_ref[...] + m * _rms_r(m) * gpost_ref[...]
        h1_ref[...] = h1
        a_ref[...] = (h1 * _rms_r(h1) * gffn_ref[...]).astype(BF16)

    return pl.pallas_call(
        functools.partial(body), name="oproj_post_fwd", grid=(t // tm,),
        in_specs=[_rows(tm, d), _const(w_o.shape), _rows(tm, d), _const((1, d)), _const((1, d))],
        out_specs=[_rows(tm, d)] * 3,
        out_shape=[jax.ShapeDtypeStruct((t, d), BF16), jax.ShapeDtypeStruct((t, d), F32),
                   jax.ShapeDtypeStruct((t, d), BF16)],
        compiler_params=_cparams(("parallel",), VMEM_LIMIT),
    )(attn, w_o, h0, gpost, gffn)


def oproj_post_bwd(dh2, da, h1, m, w_o, gpost, gffn):
    t, d = h1.shape
    tm = _tile_rows(t)

    def body(dh2_ref, da_ref, h1_ref, m_ref, w_ref, gpost_ref, gffn_ref, dh1_ref, dm_ref, dat_ref, gacc_ref):
        _acc_init(gacc_ref)
        h1v, mv = h1_ref[...], m_ref[...].astype(F32)
        dh1_n, dgffn = _rms_bwd(h1v, _rms_r(h1v), gffn_ref[...], da_ref[...])
        dh1 = dh2_ref[...] + dh1_n
        dm, dgpost = _rms_bwd(mv, _rms_r(mv), gpost_ref[...], dh1)
        dmb = dm.astype(BF16)
        dh1_ref[...] = dh1
        dm_ref[...] = dmb
        dat_ref[...] = _dot_nt(dmb, w_ref[...]).astype(BF16)
        gacc_ref[0:1, :] += dgpost
        gacc_ref[1:2, :] += dgffn

    return pl.pallas_call(
        functools.partial(body), name="oproj_post_bwd", grid=(t // tm,),
        in_specs=[_rows(tm, d)] * 4 + [_const(w_o.shape), _const((1, d)), _const((1, d))],
        out_specs=[_rows(tm, d)] * 3 + [_resident((8, d))],
        out_shape=[jax.ShapeDtypeStruct((t, d), F32), jax.ShapeDtypeStruct((t, d), BF16),
                   jax.ShapeDtypeStruct((t, d), BF16), jax.ShapeDtypeStruct((8, d), F32)],
        compiler_params=_cparams(("arbitrary",), VMEM_LIMIT),
    )(dh2, da, h1, m, w_o, gpost, gffn)


def _my_place():
    return lax.axis_index("x"), lax.axis_index("y"), lax.axis_index("c")


def _block_index(px, py, pc):
    return 4 * px + 2 * py + pc


def allgather_pieces(shards, name):
    np_ = len(shards)

    def body(*refs):
        in_refs, out_refs = refs[:np_], refs[np_:2 * np_]
        send_sems, recv_sems, local_sems = refs[2 * np_:]
        x, y, c = _my_place()
        me, sibling = (x, y, c), (x, y, 1 - c)
        chips = [(1 - x, y), (x, 1 - y), (1 - x, 1 - y)]

        def rows(p, place):
            r = in_refs[p].shape[0]
            return out_refs[p].at[pl.ds(_block_index(*place) * r, r), :]

        def copy(p, k, block, to, src=None):
            return pltpu.make_async_remote_copy(
                src_ref=rows(p, block) if src is None else src, dst_ref=rows(p, block),
                send_sem=send_sems.at[p, k], recv_sem=recv_sems.at[p, k], device_id=to, device_id_type=MESH)

        mine = [pltpu.make_async_copy(in_refs[p], rows(p, me), local_sems.at[p]) for p in range(np_)]
        first, passed = [], []
        for p in range(np_):
            mine[p].start()
            first.append(copy(p, 0, me, sibling, src=in_refs[p]))
            first += [copy(p, 1 + j, me, (*chip, c), src=in_refs[p]) for j, chip in enumerate(chips)]
        for cp in first:
            cp.start()
        for p in range(np_):
            for j, chip in enumerate(chips):
                copy(p, 1 + j, (*chip, c), me).wait_recv()
                fwd = copy(p, 4 + j, (*chip, c), sibling)
                fwd.start()
                passed.append(fwd)
        for p in range(np_):
            copy(p, 0, sibling, me).wait_recv()
            for j, chip in enumerate(chips):
                copy(p, 4 + j, (*chip, 1 - c), me).wait_recv()
        for cp in first + passed:
            cp.wait_send()
        for cp in mine:
            cp.wait()

    return pl.pallas_call(
        functools.partial(body), name=name,
        in_specs=[ANY] * np_, out_specs=[ANY] * np_,
        out_shape=[jax.ShapeDtypeStruct((N_DEV * s.shape[0], s.shape[1]), s.dtype) for s in shards],
        scratch_shapes=[pltpu.SemaphoreType.DMA((np_, 7)), pltpu.SemaphoreType.DMA((np_, 7)),
                        pltpu.SemaphoreType.DMA((np_,))],
    )(*shards)


def _peers():
    x, y, c = _my_place()
    flips = [(fx, fy, fc) for fx in (0, 1) for fy in (0, 1) for fc in (0, 1)][1:]
    return [(1 - x if fx else x, 1 - y if fy else y, 1 - c if fc else c) for fx, fy, fc in flips]


HBM = pl.BlockSpec(memory_space=pltpu.HBM)
SEM = pl.BlockSpec(memory_space=pltpu.SEMAPHORE)


def _exchange_windows(scatter, src_ref, land_ref, my_block, peer_block):
    if scatter:
        r = land_ref.shape[1]
        return src_ref.at[pl.ds(peer_block * r, r), :], land_ref.at[my_block], land_ref.at[peer_block]
    r = src_ref.shape[0]
    return src_ref, land_ref.at[pl.ds(my_block * r, r), :], land_ref.at[pl.ds(peer_block * r, r), :]


def _own_copy(scatter, src_ref, land_ref, my_block, sem):
    if scatter:
        r = land_ref.shape[1]
        return pltpu.make_async_copy(src_ref.at[pl.ds(my_block * r, r), :], land_ref.at[my_block], sem)
    r = src_ref.shape[0]
    return pltpu.make_async_copy(src_ref, land_ref.at[pl.ds(my_block * r, r), :], sem)


def exchange_start(srcs, lands, after, scatter, name):
    np_ = len(srcs)

    def body(*refs):
        src_refs, land_refs = refs[:np_], refs[np_:2 * np_]
        send_sems, recv_sems, own_sems = refs[2 * np_ + 1:2 * np_ + 4]
        token = refs[-1]
        my_block = _block_index(*_my_place())
        for p in range(np_):
            _own_copy(scatter, src_refs[p], land_refs[p], my_block, own_sems.at[p]).start()
            for k, peer in enumerate(_peers()):
                src, dst, _ = _exchange_windows(scatter, src_refs[p], land_refs[p], my_block, _block_index(*peer))
                pltpu.make_async_remote_copy(src_ref=src, dst_ref=dst, send_sem=send_sems.at[7 * p + k],
                                             recv_sem=recv_sems.at[7 * p + k], device_id=peer, device_id_type=MESH).start()
        token[...] = jnp.zeros_like(token)

    hbm = lambda a: pltpu.with_memory_space_constraint(a, pltpu.HBM)
    outs = pl.pallas_call(
        functools.partial(body), name=name,
        in_specs=[HBM] * (2 * np_) + [ANY],
        out_specs=[SEM, SEM, SEM] + [HBM] * (2 * np_) + [pl.BlockSpec(memory_space=pltpu.VMEM)],
        out_shape=[pltpu.SemaphoreType.DMA((7 * np_,)), pltpu.SemaphoreType.DMA((7 * np_,)), pltpu.SemaphoreType.DMA((np_,))]
        + [pltpu.HBM(a.shape, a.dtype) for a in list(srcs) + list(lands)] + [jax.ShapeDtypeStruct((8, LANES), F32)],
        input_output_aliases={i: 3 + i for i in range(2 * np_)},
        compiler_params=pltpu.CompilerParams(has_side_effects=pltpu.SideEffectType.DATAFLOW_SIDE_EFFECTING),
    )(*[hbm(a) for a in srcs], *[hbm(a) for a in lands], after)
    return dict(sems=outs[:3], srcs=outs[3:3 + np_], lands=outs[3 + np_:3 + 2 * np_], token=outs[-1], scatter=scatter)


def exchange_wait(started, after, name):
    srcs, lands = started["srcs"], started["lands"]
    scatter = started["scatter"]
    np_ = len(srcs)

    def body(*refs):
        src_refs, land_refs = refs[:np_], refs[np_:2 * np_]
        send_sems, recv_sems, own_sems = refs[2 * np_:2 * np_ + 3]
        my_block = _block_index(*_my_place())
        for p in range(np_):
            _own_copy(scatter, src_refs[p], land_refs[p], my_block, own_sems.at[p]).wait()
            for k, peer in enumerate(_peers()):
                src, dst, arrival = _exchange_windows(scatter, src_refs[p], land_refs[p], my_block, _block_index(*peer))
                pltpu.make_async_remote_copy(src_ref=src, dst_ref=dst, send_sem=send_sems.at[7 * p + k],
                                             recv_sem=recv_sems.at[7 * p + k], device_id=peer, device_id_type=MESH).wait_send()
                pltpu.make_async_remote_copy(src_ref=src, dst_ref=arrival, send_sem=send_sems.at[7 * p + k],
                                             recv_sem=recv_sems.at[7 * p + k], device_id=peer, device_id_type=MESH).wait_recv()

    outs = pl.pallas_call(
        functools.partial(body), name=name,
        in_specs=[HBM] * (2 * np_) + [SEM, SEM, SEM, ANY],
        out_specs=[HBM] * (2 * np_),
        out_shape=[pltpu.HBM(a.shape, a.dtype) for a in list(srcs) + list(lands)],
        input_output_aliases={i: i for i in range(2 * np_)},
        compiler_params=pltpu.CompilerParams(has_side_effects=pltpu.SideEffectType.DATAFLOW_SIDE_EFFECTING),
    )(*srcs, *lands, *started["sems"], after)
    return list(outs[np_:])


def _gather_zone(shard):
    return lax.empty((N_DEV * shard.shape[0], shard.shape[1]), shard.dtype)


def _scatter_zone(full):
    return lax.empty((N_DEV, full.shape[0] // N_DEV, full.shape[1]), full.dtype)


def allreduce_small(pack):
    r, c = pack.shape

    def body(pack_ref, out_ref, gathered, send_sems, recv_sems):
        me = _my_place()
        my_block = _block_index(*me)
        peers = _peers()

        def copy(k, slot, to):
            return pltpu.make_async_remote_copy(
                src_ref=pack_ref, dst_ref=gathered.at[slot], send_sem=send_sems.at[k], recv_sem=recv_sems.at[k],
                device_id=to, device_id_type=MESH)

        sends = [copy(k, my_block, peer) for k, peer in enumerate(peers)]
        for cp in sends:
            cp.start()
        gathered[my_block] = pack_ref[...]
        for k, peer in enumerate(peers):
            copy(k, _block_index(*peer), peer).wait_recv()
        for cp in sends:
            cp.wait_send()
        total = gathered[0]
        for j in range(1, N_DEV):
            total = total + gathered[j]
        out_ref[...] = total

    return pl.pallas_call(
        functools.partial(body), name="allreduce_small",
        in_specs=[pl.BlockSpec(memory_space=pltpu.VMEM)], out_specs=pl.BlockSpec(memory_space=pltpu.VMEM),
        out_shape=jax.ShapeDtypeStruct((r, c), F32),
        scratch_shapes=[pltpu.VMEM((N_DEV, r, c), F32), pltpu.SemaphoreType.DMA((7,)), pltpu.SemaphoreType.DMA((7,))],
    )(pack)


def sum_parts(parts):
    n, r, c = parts.shape
    br = 256 if r % 256 == 0 else r

    def body(p_ref, g_ref):
        g = p_ref[0].astype(F32)
        for j in range(1, n):
            g = g + p_ref[j].astype(F32)
        g_ref[...] = g

    return pl.pallas_call(
        functools.partial(body), name="sum_parts", grid=(r // br,),
        in_specs=[pl.BlockSpec((n, br, c), lambda i: (0, i, 0))], out_specs=_rows(br, c),
        out_shape=jax.ShapeDtypeStruct((r, c), F32),
        compiler_params=_cparams(("parallel",)),
    )(parts)


def adamw(w, m, v, g):
    nl, r, c = w.shape
    br = 256 if r % 256 == 0 else r
    blk = pl.BlockSpec((None, br, c), lambda l, i: (l, i, 0))

    def body(w_ref, m_ref, v_ref, gin_ref, g_ref, d_ref, nm_ref, nv_ref):
        g = gin_ref[...]
        nm = ADAM_B1 * m_ref[...] + (1.0 - ADAM_B1) * g
        nv = ADAM_B2 * v_ref[...] + (1.0 - ADAM_B2) * (g * g)
        m_hat = nm / (1.0 - ADAM_B1 ** ADAM_STEP)
        v_hat = nv / (1.0 - ADAM_B2 ** ADAM_STEP)
        g_ref[...] = g
        d_ref[...] = -ADAM_LR * (m_hat / (jnp.sqrt(v_hat) + ADAM_EPS) + ADAM_WD * w_ref[...])
        nm_ref[...] = nm
        nv_ref[...] = nv

    return pl.pallas_call(
        functools.partial(body), name="adamw", grid=(nl, r // br),
        in_specs=[blk] * 4, out_specs=[blk] * 4, out_shape=[jax.ShapeDtypeStruct((nl, r, c), F32)] * 4,
        compiler_params=_cparams(("parallel", "parallel")),
    )(w, m, v, g)


def _adamw_nd(w, m, v, g):
    shp = w.shape
    flat = lambda a: a.reshape((-1,) + shp[-2:])
    outs = adamw(flat(w), flat(m), flat(v), flat(g))
    return [o.reshape(shp) for o in outs]


def _pair_heads(a, axis, width=HEAD_DIM):
    shp = a.shape
    a = a.reshape(shp[:axis] + (2, 2, GQA, width) + shp[axis + 1:])
    return jnp.swapaxes(a, axis + 1, axis + 2).reshape(shp)


def _unpair_heads(a, axis, width=HEAD_DIM):
    shp = a.shape
    a = a.reshape(shp[:axis] + (2, GQA, 2, width) + shp[axis + 1:])
    return jnp.swapaxes(a, axis + 1, axis + 2).reshape(shp)


def _pad_rows(a, rows=8):
    return jnp.pad(a, ((0, rows - a.shape[0]), (0, 0)))


def kernel(x, p, mix_pre_g, mix_post_g, ffn_pre_g, ffn_post_g, pool_w, pool_scale, kv_norm_g, w_k, w_v, w_q, w_o, sinks, w_ff_gate, w_ff_up, w_ff_down, ple_norm_g, w_ple_gate, w_ple_proj, loss_target, m_mix_pre_g, m_mix_post_g, m_ffn_pre_g, m_ffn_post_g, m_pool_w, m_pool_scale, m_kv_norm_g, m_w_k, m_w_v, m_w_q, m_w_o, m_sinks, m_w_ff_gate, m_w_ff_up, m_w_ff_down, m_ple_norm_g, m_w_ple_gate, m_w_ple_proj, v_mix_pre_g, v_mix_post_g, v_ffn_pre_g, v_ffn_post_g, v_pool_w, v_pool_scale, v_kv_norm_g, v_w_k, v_w_v, v_w_q, v_w_o, v_sinks, v_w_ff_gate, v_w_ff_up, v_w_ff_down, v_ple_norm_g, v_w_ple_gate, v_w_ple_proj):
    depth = w_ff_gate.shape[0]
    n_a = pool_w.shape[0]
    t, d = x.shape[1], x.shape[2]
    h = x[0]
    tgt = loss_target[0]
    p_all = p.reshape(depth * t, p.shape[-1])
    my_block = _block_index(*_my_place())
    row = lambda g, i: g[i][None, :]
    bf = lambda a: a.astype(BF16)

    full, gathers = [None] * depth, {}
    start_tokens = jnp.zeros((), F32)
    for i in range(depth):
        shards = [bf(w_ff_gate[i].T), bf(w_ff_up[i].T), bf(w_ff_down[i]), bf(w_ple_gate[i]), bf(w_ple_proj[i].T)]
        if i == 0:
            pool0, scale_full = allgather_pieces([bf(pool_w[0].reshape(-1, POOL_GROUP)), _pad_rows(pool_scale)],
                                                 "allgather_pool0")
            order = pool0
        elif i < n_a:
            shards.append(bf(pool_w[i].reshape(-1, POOL_GROUP)))
        else:
            shards += [bf(_pair_heads(w_q[i - n_a], 1)), bf(w_o[i - n_a])]
            if i == n_a:
                shards.append(bf(jnp.concatenate([w_k, w_v], axis=1)))
        gathers[i] = exchange_start(shards, [_gather_zone(s) for s in shards], order, False, f"allgather_start_l{i}")
        order = gathers[i]["token"]
        start_tokens = start_tokens + order[0, 0]
    scale_full = scale_full.reshape(N_DEV, 8, -1)[:, :n_a].transpose(1, 0, 2).reshape(n_a, 1, d)

    cos, sin = _rope_tables(t)
    sink_b = [jnp.broadcast_to(_pair_heads(sinks[j][:, None], 0, 1), (N_HEADS, LANES)) for j in range(depth - n_a)]
    pool_full, wo_full = {}, {}

    saved = []
    kv = hk = None
    for i in range(depth):
        if i > 0:
            full[i] = exchange_wait(gathers[i], h, f"allgather_wait_l{i}")
        s = {"h0": h}
        if i < n_a:
            pool_full[i] = ((pool0 if i == 0 else full[i][5]).reshape(N_DEV, len(POOL_WINDOWS), -1, POOL_GROUP)
                            .transpose(1, 0, 2, 3).reshape(len(POOL_WINDOWS), POOL_GROUP, POOL_GROUP))
            gpre = row(mix_pre_g, i) + start_tokens if i == 0 else row(mix_pre_g, i)
            h1, a = pool_mix_fwd(h, gpre, pool_full[i], scale_full[i], row(mix_post_g, i), row(ffn_pre_g, i))
            if i == 0:
                full[0] = exchange_wait(gathers[0], h1, "allgather_wait_l0")
        else:
            j = i - n_a
            wo_full[i] = _pair_heads(full[i][6], 0)
            if i == n_a:
                hk, kv = proj_rope_fwd(h, kv_norm_g[None, :], full[i][7], cos, sin, N_KV_HEADS * HEAD_DIM, "kv_proj_fwd")
            hn, q = proj_rope_fwd(h, row(mix_pre_g, i), full[i][5], cos, sin, d, "q_proj_fwd")
            attn = swa_fwd(q, kv, sink_b[j])
            m, h1, a = oproj_post_fwd(attn, wo_full[i], h, row(mix_post_g, i), row(ffn_pre_g, i))
            s.update(hn=hn, q=q, attn=attn, m=m)
        wg_t, wu_t, wd, wpg, wpp_t = full[i][:5]
        f, gte, up, hdn = ffn_fwd(a, wg_t, wu_t, wd)
        s.update(h1=h1, a=a, f=f, gte=gte, up=up, hdn=hdn)
        if i < depth - 1:
            h = post_ple_fwd(h1, f, p_all, i, row(ffn_post_g, i), row(ple_norm_g, i), wpg, wpp_t)
        saved.append(s)

    g_mix_pre, g_mix_post, g_ffn_pre, g_ffn_post, g_ple = ([None] * depth for _ in range(5))
    g_kv = g_sinks = None
    g_scale = [None] * n_a
    landing, scatters = [None] * depth, {}
    dkv_sum = []
    scatter_token = jnp.zeros((), F32)
    for i in reversed(range(depth)):
        s = saved[i]
        wg_t, wu_t, wd, wpg, wpp_t = full[i][:5]
        last = i == depth - 1
        dh2, df, ub, dzb, dppb, gacc = post_ple_bwd(tgt if last else dh, s["h1"], s["f"], p_all, i,
                                                    row(ffn_post_g, i) + scatter_token, row(ple_norm_g, i), wpg, wpp_t,
                                                    from_target=last)
        g_ple[i], g_ffn_post[i] = gacc[0], gacc[1]
        if last:
            loss_row = gacc[2][None, :]
        da, dgte, dup = ffn_bwd_act(df, s["gte"], s["up"], wg_t, wu_t, wd)
        grads = [xty(dgte, s["a"]), xty(dup, s["a"]), xty(s["hdn"], df), xty(ub, dzb), xty(dppb, p_all, i)]
        early = exchange_start(grads, [_scatter_zone(g) for g in grads], dh2, True, f"reduce_scatter_start_l{i}a")
        early_token = early["token"][0, 0]
        if i < n_a:
            dh, dpw, gacc = pool_mix_bwd(s["h0"], dh2, da, row(mix_pre_g, i) + early_token, pool_full[i], scale_full[i],
                                         row(mix_post_g, i), row(ffn_pre_g, i))
            g_mix_pre[i], g_mix_post[i], g_ffn_pre[i], g_scale[i] = gacc[0], gacc[1], gacc[2], gacc[3]
            dpw = dpw.reshape(len(POOL_WINDOWS), N_DEV, -1, POOL_GROUP).transpose(1, 0, 2, 3)
            grads = [bf(dpw.reshape(-1, POOL_GROUP))]
        else:
            j = i - n_a
            dh1, dmb, dattn, gacc = oproj_post_bwd(dh2, da, s["h1"], s["m"], wo_full[i], row(mix_post_g, i) + early_token,
                                                   row(ffn_pre_g, i))
            g_mix_post[i], g_ffn_pre[i] = gacc[0], gacc[1]
            dq, dkv, dsink = swa_bwd(s["q"], kv, dattn, sink_b[j])
            dkv_sum.append(dkv)
            g_sinks = [_unpair_heads(dsink[:, 0:1], 0, 1)[:, 0]] + (g_sinks or [])
            branches = [(row(mix_pre_g, i), full[i][5], d, [dq])]
            if i == n_a:
                branches.append((kv_norm_g[None, :], full[i][7], N_KV_HEADS * HEAD_DIM, dkv_sum))
            outs = proj_rope_bwd(dh1, s["h0"], cos, sin, branches, f"proj_bwd_l{i}")
            dh, gacc = outs[0], outs[-1]
            g_mix_pre[i] = gacc[0]
            grads = [xty(s["hn"], outs[1]), _unpair_heads(xty(s["attn"], dmb), 0)]
            if i == n_a:
                g_kv = gacc[1]
                grads.append(xty(hk, outs[2]))
        late = exchange_start(grads, [_scatter_zone(g) for g in grads], dh, True, f"reduce_scatter_start_l{i}b")
        scatter_token = late["token"][0, 0]
        scatters[i] = (early, late)
    grad_x = dh[None]
    after = dh
    for i in reversed(range(depth)):
        landing[i] = (exchange_wait(scatters[i][0], after, f"reduce_scatter_wait_l{i}a")
                      + exchange_wait(scatters[i][1], after, f"reduce_scatter_wait_l{i}b"))
        after = landing[i][0]

    sink_row = jnp.pad(jnp.concatenate(g_sinks)[None, :], ((0, 0), (0, d - sinks.size)))
    stack = lambda rows_: _pad_rows(jnp.stack(rows_))
    pack = jnp.concatenate([stack(g_mix_pre), stack(g_mix_post), stack(g_ffn_pre), stack(g_ffn_post), stack(g_ple),
                            _pad_rows(g_kv[None]), stack(g_scale), _pad_rows(sink_row), _pad_rows(loss_row)], axis=0)
    tot = allreduce_small(pack)
    sec = lambda k, n: tot[8 * k:8 * k + n]
    loss = jnp.sum(tot[64])
    small = {
        "mix_pre_g": sec(0, depth), "mix_post_g": sec(1, depth), "ffn_pre_g": sec(2, depth),
        "ffn_post_g": sec(3, depth), "ple_norm_g": sec(4, depth), "kv_norm_g": tot[40],
        "pool_scale": lax.dynamic_slice_in_dim(sec(6, n_a), my_block * pool_scale.shape[1], pool_scale.shape[1], axis=1),
        "sinks": tot[56, :sinks.size].reshape(sinks.shape),
    }

    weights = dict(mix_pre_g=mix_pre_g, mix_post_g=mix_post_g, ffn_pre_g=ffn_pre_g, ffn_post_g=ffn_post_g, pool_w=pool_w, pool_scale=pool_scale, kv_norm_g=kv_norm_g, w_k=w_k, w_v=w_v, w_q=w_q, w_o=w_o, sinks=sinks, w_ff_gate=w_ff_gate, w_ff_up=w_ff_up, w_ff_down=w_ff_down, ple_norm_g=ple_norm_g, w_ple_gate=w_ple_gate, w_ple_proj=w_ple_proj)
    mom1 = dict(mix_pre_g=m_mix_pre_g, mix_post_g=m_mix_post_g, ffn_pre_g=m_ffn_pre_g, ffn_post_g=m_ffn_post_g, pool_w=m_pool_w, pool_scale=m_pool_scale, kv_norm_g=m_kv_norm_g, w_k=m_w_k, w_v=m_w_v, w_q=m_w_q, w_o=m_w_o, sinks=m_sinks, w_ff_gate=m_w_ff_gate, w_ff_up=m_w_ff_up, w_ff_down=m_w_ff_down, ple_norm_g=m_ple_norm_g, w_ple_gate=m_w_ple_gate, w_ple_proj=m_w_ple_proj)
    mom2 = dict(mix_pre_g=v_mix_pre_g, mix_post_g=v_mix_post_g, ffn_pre_g=v_ffn_pre_g, ffn_post_g=v_ffn_post_g, pool_w=v_pool_w, pool_scale=v_pool_scale, kv_norm_g=v_kv_norm_g, w_k=v_w_k, w_v=v_w_v, w_q=v_w_q, w_o=v_w_o, sinks=v_sinks, w_ff_gate=v_w_ff_gate, w_ff_up=v_w_ff_up, w_ff_down=v_w_ff_down, ple_norm_g=v_ple_norm_g, w_ple_gate=v_w_ple_gate, w_ple_proj=v_w_ple_proj)

    def land(i, k):
        return sum_parts(landing[i][k])

    gw = dict(small)
    gw["kv_norm_g"] = small["kv_norm_g"]
    hidden_major = ("w_ff_gate", "w_ff_up")
    gw["w_ff_gate"] = jnp.stack([land(i, 0) for i in range(depth)])
    gw["w_ff_up"] = jnp.stack([land(i, 1) for i in range(depth)])
    gw["w_ff_down"] = jnp.stack([land(i, 2) for i in range(depth)])
    gw["w_ple_gate"] = jnp.stack([land(i, 3) for i in range(depth)])
    gw["w_ple_proj"] = jnp.stack([land(i, 4).T for i in range(depth)])
    gw["pool_w"] = jnp.stack([land(i, 5).reshape(pool_w.shape[1:]) for i in range(n_a)])
    gw["w_q"] = jnp.stack([_unpair_heads(land(i, 5), 1) for i in range(n_a, depth)])
    gw["w_o"] = jnp.stack([land(i, 6) for i in range(n_a, depth)])
    gkv = land(n_a, 7)
    gw["w_k"], gw["w_v"] = gkv[:, :w_k.shape[1]], gkv[:, w_k.shape[1]:]

    order = ["mix_pre_g", "mix_post_g", "ffn_pre_g", "ffn_post_g", "pool_w", "pool_scale", "kv_norm_g", "w_k", "w_v",
             "w_q", "w_o", "sinks", "w_ff_gate", "w_ff_up", "w_ff_down", "ple_norm_g", "w_ple_gate", "w_ple_proj"]
    g_out, d_out, m_out, v_out = [], [], [], []
    for nme in order:
        w = weights[nme]
        if nme in hidden_major:
            view = unview = lambda a: jnp.swapaxes(a, 1, 2)
        else:
            view = (lambda a: a[None, :]) if w.ndim == 1 else (lambda a: a)
            unview = lambda a: a.reshape(w.shape)
        g = gw[nme] if nme in hidden_major else view(gw[nme])
        outs = _adamw_nd(view(w), view(mom1[nme]), view(mom2[nme]), g)
        for lst, val in zip((g_out, d_out, m_out, v_out), outs):
            lst.append(unview(val))
    return (loss, grad_x, *g_out, *d_out, *m_out, *v_out)
```

```python
import functools

import jax
import jax.numpy as jnp
from jax import lax
from jax.experimental import pallas as pl
from jax.experimental.pallas import tpu as pltpu

F32 = jnp.float32
BF16 = jnp.bfloat16

N_DEV = 8
HEAD_DIM = 64
N_HEADS = 16
N_KV_HEADS = 4
GQA = N_HEADS // N_KV_HEADS
BLOCK = 128
POOL_WINDOWS = (2, 4, 8, 16)
POOL_GROUP = 256
HALO = 16
ROPE_THETA = 10000.0
RMS_EPS = 1e-6
NEG_INF = -1e30
LANES = 128
ATTN_SUB = 8
XTY_ROWS = 2048
FFN_CHUNK = 768
VMEM_LIMIT = 56 * 1024 * 1024

ADAM_LR = 0.001
ADAM_B1 = 0.9
ADAM_B2 = 0.999
ADAM_EPS = 1e-08
ADAM_WD = 0.01
ADAM_STEP = 10

MESH = pl.DeviceIdType.MESH
ANY = pl.BlockSpec(memory_space=pl.ANY)

NT_DIMS = (((1,), (1,)), ((), ()))
TN_DIMS = (((0,), (0,)), ((), ()))


def _cparams(sem=None, vmem=None):
    kw = {}
    if sem is not None:
        kw["dimension_semantics"] = sem
    if vmem is not None:
        kw["vmem_limit_bytes"] = vmem
    return pltpu.CompilerParams(**kw)


def _rows(tm, n, first=0):
    return pl.BlockSpec((tm, n), lambda i: (i + first, 0))


def _rows_rev(tm, n, nt):
    return pl.BlockSpec((tm, n), lambda i: (nt - 1 - i, 0))


def _const(shape):
    nd = len(shape)
    return pl.BlockSpec(shape, lambda *_: (0,) * nd, pipeline_mode=pl.Buffered(1))


def _resident(shape):
    nd = len(shape)
    return pl.BlockSpec(shape, lambda *_: (0,) * nd)


def _tile_rows(t):
    return 512 if t % 512 == 0 else 128


def _dot(a, b):
    return jnp.dot(a, b, preferred_element_type=F32)


def _dot_nt(a, b):
    return lax.dot_general(a, b, NT_DIMS, preferred_element_type=F32)


def _dot_tn(a, b):
    return lax.dot_general(a, b, TN_DIMS, preferred_element_type=F32)


def _rms_r(x):
    return lax.rsqrt(jnp.mean(x * x, axis=-1, keepdims=True) + RMS_EPS)


def _rms_bwd(x, r, g, dy):
    gy = dy * g
    dx = r * gy - x * (r * r * r * jnp.mean(gy * x, axis=-1, keepdims=True))
    dg = jnp.sum(dy * (x * r), axis=0, keepdims=True)
    return dx, dg


def _sigmoid(x):
    return jax.nn.sigmoid(x)


def _rope_tables(t, zero_token):
    inv = 1.0 / (ROPE_THETA ** (jnp.arange(0, HEAD_DIM, 2, dtype=F32) / HEAD_DIM))
    ang = (jnp.arange(t, dtype=F32) + zero_token)[:, None] * jnp.tile(inv, 2 * LANES // HEAD_DIM)[None, :]
    sign = jnp.tile(jnp.repeat(jnp.array([-1.0, 1.0], F32), HEAD_DIM // 2), LANES // HEAD_DIM)
    return jnp.cos(ang), jnp.sin(ang) * sign[None, :]


def _swap_halves(x):
    n = x.shape[1]
    lane = lax.broadcasted_iota(jnp.int32, x.shape, 1)
    first = (lane % HEAD_DIM) < (HEAD_DIM // 2)
    return jnp.where(first, pltpu.roll(x, n - HEAD_DIM // 2, 1), pltpu.roll(x, HEAD_DIM // 2, 1))


def _rope(x, cos, sin):
    reps = x.shape[1] // LANES
    return x * jnp.tile(cos, (1, reps)) + _swap_halves(x) * jnp.tile(sin, (1, reps))


def _unrope(dy, cos, sin):
    reps = dy.shape[1] // LANES
    return dy * jnp.tile(cos, (1, reps)) + _swap_halves(dy * jnp.tile(sin, (1, reps)))


def _acc_init(acc_ref):
    @pl.when(pl.program_id(0) == 0)
    def _():
        acc_ref[...] = jnp.zeros_like(acc_ref)


def _window_sums(ext, tm, forward):
    n = tm + HALO
    out = []
    for g, w in enumerate(POOL_WINDOWS):
        s = ext[:, g * POOL_GROUP:(g + 1) * POOL_GROUP]
        k = 1
        while k < w:
            s = s + pltpu.roll(s, k if forward else n - k, 0)
            k *= 2
        out.append(s[HALO:, :] if forward else s[:tm, :])
    return out


def _pool_inv_counts(tile, tm):
    t = tile * tm + lax.broadcasted_iota(jnp.int32, (tm, 1), 0)
    return [1.0 / jnp.minimum(t + 1, w).astype(F32) for w in POOL_WINDOWS]


def _pool_mix(hn, ext, inv_cnts, pw_ref, scale, tm):
    sums = _window_sums(ext, tm, True)
    pooled, ys = [], []
    for g in range(len(POOL_WINDOWS)):
        pg = (sums[g] * inv_cnts[g] - hn[:, g * POOL_GROUP:(g + 1) * POOL_GROUP]).astype(BF16)
        pooled.append(pg)
        ys.append(_dot(pg, pw_ref[g]))
    y = jnp.concatenate(ys, axis=1)
    return pooled, y, y * scale


def pool_mix_fwd(h0, gpre, pool_w, scale, gpost, gffn):
    t, d = h0.shape
    tm = _tile_rows(t)

    def body(h_ref, gpre_ref, pw_ref, scale_ref, gpost_ref, gffn_ref, h1_ref, a_ref, carry):
        i = pl.program_id(0)

        @pl.when(i == 0)
        def _():
            carry[...] = jnp.zeros_like(carry)

        x = h_ref[...]
        hn = x * _rms_r(x) * gpre_ref[...]
        ext = jnp.concatenate([carry[...], hn], axis=0)
        carry[...] = hn[tm - HALO:, :]
        _, _, m = _pool_mix(hn, ext, _pool_inv_counts(i, tm), pw_ref, scale_ref[...], tm)
        h1 = x + m * _rms_r(m) * gpost_ref[...]
        h1_ref[...] = h1
        a_ref[...] = (h1 * _rms_r(h1) * gffn_ref[...]).astype(BF16)

    return pl.pallas_call(
        functools.partial(body), name="pool_mix_fwd", grid=(t // tm,),
        in_specs=[_rows(tm, d), _const((1, d)), _const(pool_w.shape), _const((1, d)), _const((1, d)), _const((1, d))],
        out_specs=[_rows(tm, d), _rows(tm, d)],
        out_shape=[jax.ShapeDtypeStruct((t, d), F32), jax.ShapeDtypeStruct((t, d), BF16)],
        scratch_shapes=[pltpu.VMEM((HALO, d), F32)],
        compiler_params=_cparams(("arbitrary",), VMEM_LIMIT),
    )(h0, gpre, pool_w, scale, gpost, gffn)


def pool_mix_bwd(h0, dh2, da, gpre, pool_w, scale, gpost, gffn):
    t, d = h0.shape
    tm = _tile_rows(t)
    nt = t // tm
    hb = tm // HALO

    def body(h_ref, halo_ref, dh2_ref, da_ref, gpre_ref, pw_ref, scale_ref, gpost_ref, gffn_ref,
             dh0_ref, dpw_ref, gacc_ref, carry):
        i = pl.program_id(0)
        tile = nt - 1 - i
        _acc_init(gacc_ref)
        _acc_init(dpw_ref)

        @pl.when(i == 0)
        def _():
            carry[...] = jnp.zeros_like(carry)

        x = h_ref[...]
        gpre_v, scale_v, gpost_v, gffn_v = gpre_ref[...], scale_ref[...], gpost_ref[...], gffn_ref[...]
        r0 = _rms_r(x)
        hn = x * r0 * gpre_v
        xh = halo_ref[...]
        hn_halo = jnp.where(tile > 0, xh * _rms_r(xh) * gpre_v, 0.0)
        ext = jnp.concatenate([hn_halo, hn], axis=0)
        inv_cnts = _pool_inv_counts(tile, tm)
        pooled, y, m = _pool_mix(hn, ext, inv_cnts, pw_ref, scale_v, tm)
        rm = _rms_r(m)
        h1 = x + m * rm * gpost_v
        dh1_n, dgffn = _rms_bwd(h1, _rms_r(h1), gffn_v, da_ref[...])
        dh1 = dh2_ref[...] + dh1_n
        dm, dgpost = _rms_bwd(m, rm, gpost_v, dh1)
        dscale = jnp.sum(dm * y, axis=0, keepdims=True)
        dy = (dm * scale_v).astype(BF16)
        dpn = []
        for g in range(len(POOL_WINDOWS)):
            dyg = dy[:, g * POOL_GROUP:(g + 1) * POOL_GROUP]
            dpw_ref[g] += _dot_tn(pooled[g], dyg)
            dpn.append(_dot_nt(dyg, pw_ref[g]))
        dpooled = jnp.concatenate(dpn, axis=1)
        dpc = jnp.concatenate([dpn[g] * inv_cnts[g] for g in range(len(POOL_WINDOWS))], axis=1)
        ext2 = jnp.concatenate([dpc, carry[...]], axis=0)
        carry[...] = dpc[:HALO, :]
        dhn = jnp.concatenate(_window_sums(ext2, tm, False), axis=1) - dpooled
        dh0_n, dgpre = _rms_bwd(x, r0, gpre_v, dhn)
        dh0_ref[...] = dh1 + dh0_n
        gacc_ref[0:1, :] += dgpre
        gacc_ref[1:2, :] += dgpost
        gacc_ref[2:3, :] += dgffn
        gacc_ref[3:4, :] += dscale

    return pl.pallas_call(
        functools.partial(body), name="pool_mix_bwd", grid=(nt,),
        in_specs=[_rows_rev(tm, d, nt),
                  pl.BlockSpec((HALO, d), lambda i: (jnp.maximum((nt - 1 - i) * hb - 1, 0), 0)),
                  _rows_rev(tm, d, nt), _rows_rev(tm, d, nt),
                  _const((1, d)), _const(pool_w.shape), _const((1, d)), _const((1, d)), _const((1, d))],
        out_specs=[_rows_rev(tm, d, nt), _resident(pool_w.shape), _resident((8, d))],
        out_shape=[jax.ShapeDtypeStruct((t, d), F32), jax.ShapeDtypeStruct(pool_w.shape, F32),
                   jax.ShapeDtypeStruct((8, d), F32)],
        scratch_shapes=[pltpu.VMEM((HALO, d), F32)],
        compiler_params=_cparams(("arbitrary",), VMEM_LIMIT),
    )(h0, h0, dh2, da, gpre, pool_w, scale, gpost, gffn)


def _ffn_chunks(f):
    return [(c, min(c + FFN_CHUNK, f)) for c in range(0, f, FFN_CHUNK)]


def ffn_fwd(a, wg_t, wu_t, wd):
    t, d = a.shape
    f = wd.shape[0]
    tm = _tile_rows(t)

    def body(a_ref, wg_ref, wu_ref, wd_ref, f_ref, gte_ref, up_ref, hdn_ref):
        av = a_ref[...]
        acc = jnp.zeros((tm, d), F32)
        for c0, c1 in _ffn_chunks(f):
            gte = _dot_nt(av, wg_ref[c0:c1, :])
            up = _dot_nt(av, wu_ref[c0:c1, :])
            gte_ref[:, c0:c1] = gte.astype(BF16)
            up_ref[:, c0:c1] = up.astype(BF16)
            hdn = (gte * _sigmoid(gte) * up).astype(BF16)
            hdn_ref[:, c0:c1] = hdn
            acc = acc + _dot(hdn, wd_ref[c0:c1, :])
        f_ref[...] = acc.astype(BF16)

    return pl.pallas_call(
        functools.partial(body), name="ffn_fwd", grid=(t // tm,),
        in_specs=[_rows(tm, d), _const((f, d)), _const((f, d)), _const((f, d))],
        out_specs=[_rows(tm, d), _rows(tm, f), _rows(tm, f), _rows(tm, f)],
        out_shape=[jax.ShapeDtypeStruct((t, d), BF16)] + [jax.ShapeDtypeStruct((t, f), BF16)] * 3,
        compiler_params=_cparams(("parallel",), VMEM_LIMIT),
    )(a, wg_t, wu_t, wd)


def ffn_bwd_act(df, gte, up, wg_t, wu_t, wd):
    t, d = df.shape
    f = wd.shape[0]
    tm = _tile_rows(t)

    def body(df_ref, gte_ref, up_ref, wg_ref, wu_ref, wd_ref, da_ref, dgte_ref, dup_ref):
        dfv = df_ref[...]
        chunks = _ffn_chunks(f)
        half = chunks[len(chunks) // 2][0]
        acc = None
        for c0, c1 in chunks:
            g = gte_ref[:, c0:c1].astype(F32)
            u = up_ref[:, c0:c1].astype(F32)
            sg = _sigmoid(g)
            sl = g * sg
            dh = _dot_nt(dfv, wd_ref[c0:c1, :])
            dup_ref[:, c0:c1] = (dh * sl).astype(BF16)
            dgte_ref[:, c0:c1] = (dh * u * (sg * (1.0 + g * (1.0 - sg)))).astype(BF16)
            if c1 == half:
                acc = _dot(dgte_ref[:, :half], wg_ref[:half, :]) + _dot(dup_ref[:, :half], wu_ref[:half, :])
        da_ref[...] = acc + _dot(dgte_ref[:, half:], wg_ref[half:, :]) + _dot(dup_ref[:, half:], wu_ref[half:, :])

    return pl.pallas_call(
        functools.partial(body), name="ffn_bwd_act", grid=(t // tm,),
        in_specs=[_rows(tm, d), _rows(tm, f), _rows(tm, f), _const((f, d)), _const((f, d)), _const((f, d))],
        out_specs=[_rows(tm, d), _rows(tm, f), _rows(tm, f)],
        out_shape=[jax.ShapeDtypeStruct((t, d), F32)] + [jax.ShapeDtypeStruct((t, f), BF16)] * 2,
        compiler_params=_cparams(("parallel",), VMEM_LIMIT),
    )(df, gte, up, wg_t, wu_t, wd)


def xty(x, y, y_part=0):
    t, nx = x.shape
    ny = y.shape[1]
    tk = XTY_ROWS if t % XTY_ROWS == 0 else _tile_rows(t)
    bn = nx // 2 if nx > 1024 else nx
    nk = t // tk

    def body(x_ref, y_ref, o_ref, acc):
        k = pl.program_id(1)

        @pl.when(k == 0)
        def _():
            acc[...] = jnp.zeros_like(acc)

        acc[...] += _dot_tn(x_ref[...].astype(BF16), y_ref[...].astype(BF16))

        @pl.when(k == nk - 1)
        def _():
            o_ref[...] = acc[...].astype(BF16)

    return pl.pallas_call(
        functools.partial(body), name="xty", grid=(nx // bn, nk),
        in_specs=[pl.BlockSpec((tk, bn), lambda j, k: (k, j)),
                  pl.BlockSpec((tk, ny), lambda j, k: (k + y_part * nk, 0))],
        out_specs=pl.BlockSpec((bn, ny), lambda j, k: (j, 0)),
        out_shape=jax.ShapeDtypeStruct((nx, ny), BF16),
        scratch_shapes=[pltpu.VMEM((bn, ny), F32)],
        compiler_params=_cparams(("parallel", "arbitrary"), VMEM_LIMIT),
    )(x, y)


def _ple_fwd_tile(h1, f, p, gpost, gple, wpg_ref, wpp_ref):
    rf = _rms_r(f)
    h2 = h1 + f * rf * gpost
    r2 = _rms_r(h2)
    ub = (h2 * r2 * gple).astype(BF16)
    gate = _sigmoid(_dot(ub, wpg_ref[...]))
    pp = _dot_nt(p.astype(BF16), wpp_ref[...])
    return rf, h2, r2, ub, gate, pp


def post_ple_fwd(h1, f, p, layer, gpost, gple, wpg, wpp_t):
    t, d = h1.shape
    pd = p.shape[1]
    tm = _tile_rows(t)

    def body(h1_ref, f_ref, p_ref, gpost_ref, gple_ref, wpg_ref, wpp_ref, out_ref):
        _, h2, _, _, gate, pp = _ple_fwd_tile(h1_ref[...], f_ref[...].astype(F32), p_ref[...], gpost_ref[...],
                                              gple_ref[...], wpg_ref, wpp_ref)
        out_ref[...] = h2 + pp * gate

    return pl.pallas_call(
        functools.partial(body), name="post_ple_fwd", grid=(t // tm,),
        in_specs=[_rows(tm, d), _rows(tm, d), _rows(tm, pd, layer * (t // tm)), _const((1, d)), _const((1, d)),
                  _const(wpg.shape), _const(wpp_t.shape)],
        out_specs=_rows(tm, d), out_shape=jax.ShapeDtypeStruct((t, d), F32),
        compiler_params=_cparams(("parallel",), VMEM_LIMIT),
    )(h1, f, p, gpost, gple, wpg, wpp_t)


def post_ple_bwd(dh3, h1, f, p, layer, gpost, gple, wpg, wpp_t, from_target=False):
    t, d = h1.shape
    pd = p.shape[1]
    tm = _tile_rows(t)

    def body(dh3_ref, h1_ref, f_ref, p_ref, gpost_ref, gple_ref, wpg_ref, wpp_ref,
             dh2_ref, df_ref, u_ref, dz_ref, dpp_ref, gacc_ref):
        _acc_init(gacc_ref)
        gpost_v, gple_v = gpost_ref[...], gple_ref[...]
        nsub = 2 if tm % 16 == 0 else 1
        for sb in range(nsub):
            rows = slice(sb * (tm // nsub), (sb + 1) * (tm // nsub))
            fv = f_ref[rows, :].astype(F32)
            rf, h2, r2, ub, gate, pp = _ple_fwd_tile(h1_ref[rows, :], fv, p_ref[rows, :], gpost_v, gple_v, wpg_ref,
                                                     wpp_ref)
            if from_target:
                err = h2 + pp * gate - dh3_ref[rows, :]
                dh3v = err * (1.0 / d)
                gacc_ref[2:3, :] += jnp.sum(err * err, axis=0, keepdims=True) * (0.5 / d)
            else:
                dh3v = dh3_ref[rows, :]
            dpp_ref[rows, :] = (dh3v * gate).astype(BF16)
            dz = (dh3v * pp * gate * (1.0 - gate)).astype(BF16)
            dz_ref[rows, :] = dz
            u_ref[rows, :] = ub
            du = _dot_nt(dz, wpg_ref[...])
            dh2_n, dgple = _rms_bwd(h2, r2, gple_v, du)
            dh2 = dh3v + dh2_n
            df, dgpost = _rms_bwd(fv, rf, gpost_v, dh2)
            dh2_ref[rows, :] = dh2
            df_ref[rows, :] = df.astype(BF16)
            gacc_ref[0:1, :] += dgple
            gacc_ref[1:2, :] += dgpost

    return pl.pallas_call(
        functools.partial(body), name="post_ple_loss_bwd" if from_target else "post_ple_bwd", grid=(t // tm,),
        in_specs=[_rows(tm, d), _rows(tm, d), _rows(tm, d), _rows(tm, pd, layer * (t // tm)), _const((1, d)),
                  _const((1, d)), _const(wpg.shape), _const(wpp_t.shape)],
        out_specs=[_rows(tm, d)] * 5 + [_resident((8, d))],
        out_shape=[jax.ShapeDtypeStruct((t, d), F32)] + [jax.ShapeDtypeStruct((t, d), BF16)] * 4
        + [jax.ShapeDtypeStruct((8, d), F32)],
        compiler_params=_cparams(("arbitrary",), VMEM_LIMIT),
    )(dh3, h1, f, p, gpost, gple, wpg, wpp_t)


def proj_rope_fwd(h, gain, w, cos, sin, n_rope, name):
    t, d = h.shape
    n = w.shape[1]
    tm = _tile_rows(t)

    def body(h_ref, g_ref, w_ref, cos_ref, sin_ref, hn_ref, y_ref):
        x = h_ref[...]
        hn = (x * _rms_r(x) * g_ref[...]).astype(BF16)
        hn_ref[...] = hn
        y = _dot(hn, w_ref[...])
        y_ref[:, :n_rope] = _rope(y[:, :n_rope], cos_ref[...], sin_ref[...]).astype(BF16)
        if n_rope < n:
            y_ref[:, n_rope:] = y[:, n_rope:].astype(BF16)

    return pl.pallas_call(
        functools.partial(body), name=name, grid=(t // tm,),
        in_specs=[_rows(tm, d), _const((1, d)), _const(w.shape), _rows(tm, LANES), _rows(tm, LANES)],
        out_specs=[_rows(tm, d), _rows(tm, n)],
        out_shape=[jax.ShapeDtypeStruct((t, d), BF16), jax.ShapeDtypeStruct((t, n), BF16)],
        compiler_params=_cparams(("parallel",), VMEM_LIMIT),
    )(h, gain, w, cos, sin)


def proj_rope_bwd(dh1, h0, cos, sin, branches, name):
    t, d = h0.shape
    tm = _tile_rows(t)
    nb = len(branches)
    n_cot = [len(b[3]) for b in branches]

    def body(*refs):
        dh1_ref, h0_ref, cos_ref, sin_ref = refs[:4]
        pos = 4
        br_refs = []
        for b in range(nb):
            br_refs.append((refs[pos], refs[pos + 1], refs[pos + 2:pos + 2 + n_cot[b]]))
            pos += 2 + n_cot[b]
        dh0_ref = refs[pos]
        dpre_refs = refs[pos + 1:pos + 1 + nb]
        gacc_ref = refs[pos + 1 + nb]
        _acc_init(gacc_ref)
        x = h0_ref[...]
        r0 = _rms_r(x)
        dh = dh1_ref[...]
        for b in range(nb):
            g_ref, w_ref, cot_refs = br_refs[b]
            n_rope = branches[b][2]
            dy = cot_refs[0][...].astype(F32)
            for c_ref in cot_refs[1:]:
                dy = dy + c_ref[...].astype(F32)
            n = dy.shape[1]
            dpre_refs[b][:, :n_rope] = _unrope(dy[:, :n_rope], cos_ref[...], sin_ref[...]).astype(BF16)
            if n_rope < n:
                dpre_refs[b][:, n_rope:] = dy[:, n_rope:].astype(BF16)
            dhn = _dot_nt(dpre_refs[b][...], w_ref[...])
            dx, dg = _rms_bwd(x, r0, g_ref[...], dhn)
            dh = dh + dx
            gacc_ref[b:b + 1, :] += dg
        dh0_ref[...] = dh

    in_specs = [_rows(tm, d), _rows(tm, d), _rows(tm, LANES), _rows(tm, LANES)]
    args = [dh1, h0, cos, sin]
    out_specs = [_rows(tm, d)]
    out_shape = [jax.ShapeDtypeStruct((t, d), F32)]
    for gain, w, _, cots in branches:
        n = w.shape[1]
        in_specs += [_const((1, d)), _const(w.shape)] + [_rows(tm, n)] * len(cots)
        args += [gain, w] + list(cots)
        out_specs.append(_rows(tm, n))
        out_shape.append(jax.ShapeDtypeStruct((t, n), BF16))
    out_specs.append(_resident((8, d)))
    out_shape.append(jax.ShapeDtypeStruct((8, d), F32))
    return pl.pallas_call(
        functools.partial(body), name=name, grid=(t // tm,),
        in_specs=in_specs, out_specs=out_specs, out_shape=out_shape,
        compiler_params=_cparams(("arbitrary",), VMEM_LIMIT),
    )(*args)


def _tri():
    row = lax.broadcasted_iota(jnp.int32, (BLOCK, BLOCK), 0)
    col = lax.broadcasted_iota(jnp.int32, (BLOCK, BLOCK), 1)
    return col <= row


def _block_diag(x):
    lo = lax.broadcasted_iota(jnp.int32, x.shape, 1) < HEAD_DIM
    zero = jnp.zeros_like(x)
    return jnp.concatenate([jnp.where(lo, x, zero), jnp.where(lo, zero, x)], axis=0)


def _dense(x, tri):
    return (jnp.where(tri, x[:, BLOCK:2 * BLOCK], x[:, :BLOCK]),
            jnp.where(tri, x[:, 3 * BLOCK:], x[:, 2 * BLOCK:3 * BLOCK]))


def _banded(xa, xb, tri):
    zero = jnp.zeros_like(xa)
    return jnp.concatenate([jnp.where(tri, zero, xa), jnp.where(tri, xa, zero),
                            jnp.where(tri, zero, xb), jnp.where(tri, xb, zero)], axis=1).astype(BF16)


def _softmax_sink(s, sink):
    mx = jnp.maximum(jnp.max(s, axis=1, keepdims=True), sink)
    e = jnp.exp(s - mx)
    es = jnp.exp(sink - mx)
    inv = 1.0 / (jnp.sum(e, axis=1, keepdims=True) + es)
    return e * inv, es * inv


def _sink_column(sink_ref):
    return jnp.concatenate([jnp.broadcast_to(sink_ref[h:h + 1, 0:1], (BLOCK, 1)) for h in range(N_HEADS)], axis=0)


def _kv_block_diag(band, kvw):
    n_lt = kvw // LANES
    return ([_block_diag(band[:, lt * LANES:(lt + 1) * LANES]) for lt in range(n_lt)],
            [_block_diag(band[:, kvw + lt * LANES:kvw + (lt + 1) * LANES]) for lt in range(n_lt)])


def _all_probs(q_ref, r0, kbd, tri, n, sink_ref):
    dense = []
    for tq in range(N_HEADS // 2):
        s = _dot_nt(q_ref[r0:r0 + BLOCK, tq * LANES:(tq + 1) * LANES], kbd[tq // GQA])
        dense += list(_dense(s, tri))
    bias = jnp.where(jnp.logical_not(tri) & (n == 0), NEG_INF, 0.0)
    s_all = jnp.concatenate(dense, axis=0) * (HEAD_DIM ** -0.5) + jnp.concatenate([bias] * N_HEADS, axis=0)
    return _softmax_sink(s_all, _sink_column(sink_ref))


def _head_rows(x, tq):
    return x[2 * tq * BLOCK:(2 * tq + 1) * BLOCK], x[(2 * tq + 1) * BLOCK:(2 * tq + 2) * BLOCK]


def _attn_sub(t):
    return ATTN_SUB if t % (ATTN_SUB * BLOCK) == 0 else 1


def swa_fwd(q, kv, sink_b):
    t, d = q.shape
    sub = _attn_sub(t)
    kvw = N_KV_HEADS * HEAD_DIM

    def body(q_ref, kvc_ref, kvp_ref, sink_ref, o_ref):
        i = pl.program_id(0)
        tri = _tri()
        ext = jnp.concatenate([kvp_ref[...], kvc_ref[...]], axis=0)
        for sb in range(sub):
            r0 = sb * BLOCK
            kbd, vbd = _kv_block_diag(ext[r0:r0 + 2 * BLOCK], kvw)
            p, _ = _all_probs(q_ref, r0, kbd, tri, i * sub + sb, sink_ref)
            for tq in range(N_HEADS // 2):
                pa, pb = _head_rows(p, tq)
                o_ref[r0:r0 + BLOCK, tq * LANES:(tq + 1) * LANES] = _dot(_banded(pa, pb, tri), vbd[tq // GQA]).astype(BF16)

    return pl.pallas_call(
        functools.partial(body), name="swa_fwd", grid=(t // (sub * BLOCK),),
        in_specs=[_rows(sub * BLOCK, d), _rows(sub * BLOCK, 2 * kvw),
                  pl.BlockSpec((BLOCK, 2 * kvw), lambda i: (jnp.maximum(i * sub - 1, 0), 0)), _const(sink_b.shape)],
        out_specs=_rows(sub * BLOCK, d),
        out_shape=jax.ShapeDtypeStruct((t, d), BF16),
        compiler_params=_cparams(("parallel",), VMEM_LIMIT),
    )(q, kv, kv, sink_b)


def swa_bwd(q, kv, do, sink_b):
    t, d = q.shape
    sub = _attn_sub(t)
    nq = t // (sub * BLOCK)
    kvw = N_KV_HEADS * HEAD_DIM

    def body(q_ref, do_ref, kvc_ref, kvp_ref, sink_ref, dq_ref, dkv_ref, dsink_ref, carry):
        i = pl.program_id(0)
        step = nq - 1 - i
        _acc_init(dsink_ref)

        @pl.when(i == 0)
        def _():
            carry[...] = jnp.zeros_like(carry)

        tri = _tri()
        lo = lax.broadcasted_iota(jnp.int32, (2 * BLOCK, LANES), 1) < HEAD_DIM
        ext = jnp.concatenate([kvp_ref[...], kvc_ref[...]], axis=0)
        dkeys = [None] * (sub + 1)
        for sb in reversed(range(sub)):
            r0 = sb * BLOCK
            kbd, vbd = _kv_block_diag(ext[r0:r0 + 2 * BLOCK], kvw)
            p, ps = _all_probs(q_ref, r0, kbd, tri, step * sub + sb, sink_ref)
            dp = []
            for tq in range(N_HEADS // 2):
                dp += list(_dense(_dot_nt(do_ref[r0:r0 + BLOCK, tq * LANES:(tq + 1) * LANES], vbd[tq // GQA]), tri))
            dp = jnp.concatenate(dp, axis=0)
            delta = jnp.sum(p * dp, axis=1, keepdims=True)
            ds = p * (dp - delta) * (HEAD_DIM ** -0.5)
            dsk = ps * delta
            for h in range(N_HEADS):
                dsink_ref[h:h + 1, :] -= jnp.sum(dsk[h * BLOCK:(h + 1) * BLOCK], axis=0, keepdims=True)
            dkb = [jnp.zeros((4 * BLOCK, LANES), F32) for _ in kbd]
            dvb = [jnp.zeros((4 * BLOCK, LANES), F32) for _ in kbd]
            for tq in range(N_HEADS // 2):
                lt = tq // GQA
                cols = slice(tq * LANES, (tq + 1) * LANES)
                dsb = _banded(*_head_rows(ds, tq), tri)
                dq_ref[r0:r0 + BLOCK, cols] = _dot(dsb, kbd[lt]).astype(BF16)
                dkb[lt] = dkb[lt] + _dot_tn(dsb, q_ref[r0:r0 + BLOCK, cols])
                dvb[lt] = dvb[lt] + _dot_tn(_banded(*_head_rows(p, tq), tri), do_ref[r0:r0 + BLOCK, cols])
            dall = jnp.concatenate([jnp.where(lo, x[:2 * BLOCK], x[2 * BLOCK:]) for x in dkb + dvb], axis=1)
            dkeys[sb + 1] = dall[BLOCK:] if dkeys[sb + 1] is None else dkeys[sb + 1] + dall[BLOCK:]
            dkeys[sb] = dall[:BLOCK]
        for sb in range(sub):
            own = dkeys[sb + 1] + carry[...] if sb == sub - 1 else dkeys[sb + 1]
            dkv_ref[sb * BLOCK:(sb + 1) * BLOCK, :] = own
        carry[...] = dkeys[0]

    rev = lambda i: (nq - 1 - i, 0)
    return pl.pallas_call(
        functools.partial(body), name="swa_bwd", grid=(nq,),
        in_specs=[pl.BlockSpec((sub * BLOCK, d), rev), pl.BlockSpec((sub * BLOCK, d), rev),
                  pl.BlockSpec((sub * BLOCK, 2 * kvw), rev),
                  pl.BlockSpec((BLOCK, 2 * kvw), lambda i: (jnp.maximum((nq - 1 - i) * sub - 1, 0), 0)),
                  _const(sink_b.shape)],
        out_specs=[pl.BlockSpec((sub * BLOCK, d), rev), pl.BlockSpec((sub * BLOCK, 2 * kvw), rev),
                   _resident(sink_b.shape)],
        out_shape=[jax.ShapeDtypeStruct((t, d), BF16), jax.ShapeDtypeStruct((t, 2 * kvw), F32),
                   jax.ShapeDtypeStruct(sink_b.shape, F32)],
        scratch_shapes=[pltpu.VMEM((BLOCK, 2 * kvw), F32)],
        compiler_params=_cparams(("arbitrary",), VMEM_LIMIT),
    )(q, do, kv, kv, sink_b)


def oproj_post_fwd(attn, w_o, h0, gpost, gffn):
    t, d = h0.shape
    tm = _tile_rows(t)

    def body(at_ref, w_ref, h0_ref, gpost_ref, gffn_ref, m_ref, h1_ref, a_ref):
        m = _dot(at_ref[...], w_ref[...])
        m_ref[...] = m.astype(BF16)
        h1 = h0_ref[...] + m * _rms_r(m) * gpost_ref[...]
        h1_ref[...] = h1
        a_ref[...] = (h1 * _rms_r(h1) * gffn_ref[...]).astype(BF16)

    return pl.pallas_call(
        functools.partial(body), name="oproj_post_fwd", grid=(t // tm,),
        in_specs=[_rows(tm, d), _const(w_o.shape), _rows(tm, d), _const((1, d)), _const((1, d))],
        out_specs=[_rows(tm, d)] * 3,
        out_shape=[jax.ShapeDtypeStruct((t, d), BF16), jax.ShapeDtypeStruct((t, d), F32),
                   jax.ShapeDtypeStruct((t, d), BF16)],
        compiler_params=_cparams(("parallel",), VMEM_LIMIT),
    )(attn, w_o, h0, gpost, gffn)


def oproj_post_bwd(dh2, da, h1, m, w_o, gpost, gffn):
    t, d = h1.shape
    tm = _tile_rows(t)

    def body(dh2_ref, da_ref, h1_ref, m_ref, w_ref, gpost_ref, gffn_ref, dh1_ref, dm_ref, dat_ref, gacc_ref):
        _acc_init(gacc_ref)
        h1v, mv = h1_ref[...], m_ref[...].astype(F32)
        dh1_n, dgffn = _rms_bwd(h1v, _rms_r(h1v), gffn_ref[...], da_ref[...])
        dh1 = dh2_ref[...] + dh1_n
        dm, dgpost = _rms_bwd(mv, _rms_r(mv), gpost_ref[...], dh1)
        dmb = dm.astype(BF16)
        dh1_ref[...] = dh1
        dm_ref[...] = dmb
        dat_ref[...] = _dot_nt(dmb, w_ref[...]).astype(BF16)
        gacc_ref[0:1, :] += dgpost
        gacc_ref[1:2, :] += dgffn

    return pl.pallas_call(
        functools.partial(body), name="oproj_post_bwd", grid=(t // tm,),
        in_specs=[_rows(tm, d)] * 4 + [_const(w_o.shape), _const((1, d)), _const((1, d))],
        out_specs=[_rows(tm, d)] * 3 + [_resident((8, d))],
        out_shape=[jax.ShapeDtypeStruct((t, d), F32), jax.ShapeDtypeStruct((t, d), BF16),
                   jax.ShapeDtypeStruct((t, d), BF16), jax.ShapeDtypeStruct((8, d), F32)],
        compiler_params=_cparams(("arbitrary",), VMEM_LIMIT),
    )(dh2, da, h1, m, w_o, gpost, gffn)


def _my_place():
    return lax.axis_index("x"), lax.axis_index("y"), lax.axis_index("c")


def _block_index(px, py, pc):
    return 4 * px + 2 * py + pc


def allgather_pieces(shards, name):
    np_ = len(shards)

    def body(*refs):
        in_refs, out_refs = refs[:np_], refs[np_:2 * np_]
        send_sems, recv_sems, local_sems = refs[2 * np_:]
        x, y, c = _my_place()
        me, sibling = (x, y, c), (x, y, 1 - c)
        chips = [(1 - x, y), (x, 1 - y), (1 - x, 1 - y)]

        def rows(p, place):
            r = in_refs[p].shape[0]
            return out_refs[p].at[pl.ds(_block_index(*place) * r, r), :]

        def copy(p, k, block, to, src=None):
            return pltpu.make_async_remote_copy(
                src_ref=rows(p, block) if src is None else src, dst_ref=rows(p, block),
                send_sem=send_sems.at[p, k], recv_sem=recv_sems.at[p, k], device_id=to, device_id_type=MESH)

        mine = [pltpu.make_async_copy(in_refs[p], rows(p, me), local_sems.at[p]) for p in range(np_)]
        first, passed = [], []
        for p in range(np_):
            mine[p].start()
            first.append(copy(p, 0, me, sibling, src=in_refs[p]))
            first += [copy(p, 1 + j, me, (*chip, c), src=in_refs[p]) for j, chip in enumerate(chips)]
        for cp in first:
            cp.start()
        for p in range(np_):
            for j, chip in enumerate(chips):
                copy(p, 1 + j, (*chip, c), me).wait_recv()
                fwd = copy(p, 4 + j, (*chip, c), sibling)
                fwd.start()
                passed.append(fwd)
        for p in range(np_):
            copy(p, 0, sibling, me).wait_recv()
            for j, chip in enumerate(chips):
                copy(p, 4 + j, (*chip, 1 - c), me).wait_recv()
        for cp in first + passed:
            cp.wait_send()
        for cp in mine:
            cp.wait()

    return pl.pallas_call(
        functools.partial(body), name=name,
        in_specs=[ANY] * np_, out_specs=[ANY] * np_,
        out_shape=[jax.ShapeDtypeStruct((N_DEV * s.shape[0], s.shape[1]), s.dtype) for s in shards],
        scratch_shapes=[pltpu.SemaphoreType.DMA((np_, 7)), pltpu.SemaphoreType.DMA((np_, 7)),
                        pltpu.SemaphoreType.DMA((np_,))],
    )(*shards)


def _peers():
    x, y, c = _my_place()
    flips = [(fx, fy, fc) for fx in (0, 1) for fy in (0, 1) for fc in (0, 1)][1:]
    return [(1 - x if fx else x, 1 - y if fy else y, 1 - c if fc else c) for fx, fy, fc in flips]


HBM = pl.BlockSpec(memory_space=pltpu.HBM)
SEM = pl.BlockSpec(memory_space=pltpu.SEMAPHORE)


def _exchange_windows(scatter, src_ref, land_ref, my_block, peer_block):
    if scatter:
        r = land_ref.shape[1]
        return src_ref.at[pl.ds(peer_block * r, r), :], land_ref.at[my_block], land_ref.at[peer_block]
    r = src_ref.shape[0]
    return src_ref, land_ref.at[pl.ds(my_block * r, r), :], land_ref.at[pl.ds(peer_block * r, r), :]


def _own_copy(scatter, src_ref, land_ref, my_block, sem):
    if scatter:
        r = land_ref.shape[1]
        return pltpu.make_async_copy(src_ref.at[pl.ds(my_block * r, r), :], land_ref.at[my_block], sem)
    r = src_ref.shape[0]
    return pltpu.make_async_copy(src_ref, land_ref.at[pl.ds(my_block * r, r), :], sem)


def exchange_start(srcs, lands, after, scatter, name):
    np_ = len(srcs)

    def body(*refs):
        src_refs, land_refs = refs[:np_], refs[np_:2 * np_]
        send_sems, recv_sems, own_sems = refs[2 * np_ + 1:2 * np_ + 4]
        token = refs[-1]
        my_block = _block_index(*_my_place())
        for p in range(np_):
            _own_copy(scatter, src_refs[p], land_refs[p], my_block, own_sems.at[p]).start()
            for k, peer in enumerate(_peers()):
                src, dst, _ = _exchange_windows(scatter, src_refs[p], land_refs[p], my_block, _block_index(*peer))
                pltpu.make_async_remote_copy(src_ref=src, dst_ref=dst, send_sem=send_sems.at[7 * p + k],
                                             recv_sem=recv_sems.at[7 * p + k], device_id=peer, device_id_type=MESH).start()
        token[...] = jnp.zeros_like(token)

    hbm = lambda a: pltpu.with_memory_space_constraint(a, pltpu.HBM)
    outs = pl.pallas_call(
        functools.partial(body), name=name,
        in_specs=[HBM] * (2 * np_) + [ANY],
        out_specs=[SEM, SEM, SEM] + [HBM] * (2 * np_) + [pl.BlockSpec(memory_space=pltpu.VMEM)],
        out_shape=[pltpu.SemaphoreType.DMA((7 * np_,)), pltpu.SemaphoreType.DMA((7 * np_,)), pltpu.SemaphoreType.DMA((np_,))]
        + [pltpu.HBM(a.shape, a.dtype) for a in list(srcs) + list(lands)] + [jax.ShapeDtypeStruct((8, LANES), F32)],
        input_output_aliases={i: 3 + i for i in range(2 * np_)},
        compiler_params=pltpu.CompilerParams(has_side_effects=pltpu.SideEffectType.DATAFLOW_SIDE_EFFECTING),
    )(*[hbm(a) for a in srcs], *[hbm(a) for a in lands], after)
    return dict(sems=outs[:3], srcs=outs[3:3 + np_], lands=outs[3 + np_:3 + 2 * np_], token=outs[-1], scatter=scatter)


def exchange_wait(started, after, name):
    afters = tuple(after) if isinstance(after, (tuple, list)) else (after,)
    srcs, lands = started["srcs"], started["lands"]
    scatter = started["scatter"]
    np_ = len(srcs)

    def body(*refs):
        src_refs, land_refs = refs[:np_], refs[np_:2 * np_]
        send_sems, recv_sems, own_sems = refs[2 * np_:2 * np_ + 3]
        my_block = _block_index(*_my_place())
        for p in range(np_):
            _own_copy(scatter, src_refs[p], land_refs[p], my_block, own_sems.at[p]).wait()
            for k, peer in enumerate(_peers()):
                src, dst, arrival = _exchange_windows(scatter, src_refs[p], land_refs[p], my_block, _block_index(*peer))
                pltpu.make_async_remote_copy(src_ref=src, dst_ref=dst, send_sem=send_sems.at[7 * p + k],
                                             recv_sem=recv_sems.at[7 * p + k], device_id=peer, device_id_type=MESH).wait_send()
                pltpu.make_async_remote_copy(src_ref=src, dst_ref=arrival, send_sem=send_sems.at[7 * p + k],
                                             recv_sem=recv_sems.at[7 * p + k], device_id=peer, device_id_type=MESH).wait_recv()

    outs = pl.pallas_call(
        functools.partial(body), name=name,
        in_specs=[HBM] * (2 * np_) + [SEM, SEM, SEM] + [ANY] * len(afters),
        out_specs=[HBM] * (2 * np_),
        out_shape=[pltpu.HBM(a.shape, a.dtype) for a in list(srcs) + list(lands)],
        input_output_aliases={i: i for i in range(2 * np_)},
        compiler_params=pltpu.CompilerParams(has_side_effects=pltpu.SideEffectType.DATAFLOW_SIDE_EFFECTING),
    )(*srcs, *lands, *started["sems"], *afters)
    return list(outs[np_:])


def _gather_zone(shard):
    return lax.empty((N_DEV * shard.shape[0], shard.shape[1]), shard.dtype)


def _scatter_zone(full):
    return lax.empty((N_DEV, full.shape[0] // N_DEV, full.shape[1]), full.dtype)


def allreduce_small(pack):
    r, c = pack.shape

    def body(pack_ref, out_ref, gathered, send_sems, recv_sems):
        me = _my_place()
        my_block = _block_index(*me)
        peers = _peers()

        def copy(k, slot, to):
            return pltpu.make_async_remote_copy(
                src_ref=pack_ref, dst_ref=gathered.at[slot], send_sem=send_sems.at[k], recv_sem=recv_sems.at[k],
                device_id=to, device_id_type=MESH)

        sends = [copy(k, my_block, peer) for k, peer in enumerate(peers)]
        for cp in sends:
            cp.start()
        gathered[my_block] = pack_ref[...]
        for k, peer in enumerate(peers):
            copy(k, _block_index(*peer), peer).wait_recv()
        for cp in sends:
            cp.wait_send()
        total = gathered[0]
        for j in range(1, N_DEV):
            total = total + gathered[j]
        out_ref[...] = total

    return pl.pallas_call(
        functools.partial(body), name="allreduce_small",
        in_specs=[pl.BlockSpec(memory_space=pltpu.VMEM)], out_specs=pl.BlockSpec(memory_space=pltpu.VMEM),
        out_shape=jax.ShapeDtypeStruct((r, c), F32),
        scratch_shapes=[pltpu.VMEM((N_DEV, r, c), F32), pltpu.SemaphoreType.DMA((7,)), pltpu.SemaphoreType.DMA((7,))],
    )(pack)


def sum_parts(parts):
    n, r, c = parts.shape
    br = 256 if r % 256 == 0 else r

    def body(p_ref, g_ref):
        g = p_ref[0].astype(F32)
        for j in range(1, n):
            g = g + p_ref[j].astype(F32)
        g_ref[...] = g

    return pl.pallas_call(
        functools.partial(body), name="sum_parts", grid=(r // br,),
        in_specs=[pl.BlockSpec((n, br, c), lambda i: (0, i, 0))], out_specs=_rows(br, c),
        out_shape=jax.ShapeDtypeStruct((r, c), F32),
        compiler_params=_cparams(("parallel",)),
    )(parts)


def adamw(w, m, v, g):
    nl, r, c = w.shape
    br = 256 if r % 256 == 0 else r
    blk = pl.BlockSpec((None, br, c), lambda l, i: (l, i, 0))

    def body(w_ref, m_ref, v_ref, gin_ref, g_ref, d_ref, nm_ref, nv_ref):
        g = gin_ref[...]
        nm = ADAM_B1 * m_ref[...] + (1.0 - ADAM_B1) * g
        nv = ADAM_B2 * v_ref[...] + (1.0 - ADAM_B2) * (g * g)
        m_hat = nm / (1.0 - ADAM_B1 ** ADAM_STEP)
        v_hat = nv / (1.0 - ADAM_B2 ** ADAM_STEP)
        g_ref[...] = g
        d_ref[...] = -ADAM_LR * (m_hat / (jnp.sqrt(v_hat) + ADAM_EPS) + ADAM_WD * w_ref[...])
        nm_ref[...] = nm
        nv_ref[...] = nv

    return pl.pallas_call(
        functools.partial(body), name="adamw", grid=(nl, r // br),
        in_specs=[blk] * 4, out_specs=[blk] * 4, out_shape=[jax.ShapeDtypeStruct((nl, r, c), F32)] * 4,
        compiler_params=_cparams(("parallel", "parallel")),
    )(w, m, v, g)


def _adamw_nd(w, m, v, g):
    shp = w.shape
    flat = lambda a: a.reshape((-1,) + shp[-2:])
    outs = adamw(flat(w), flat(m), flat(v), flat(g))
    return [o.reshape(shp) for o in outs]


def _pair_heads(a, axis, width=HEAD_DIM):
    shp = a.shape
    a = a.reshape(shp[:axis] + (2, 2, GQA, width) + shp[axis + 1:])
    return jnp.swapaxes(a, axis + 1, axis + 2).reshape(shp)


def _unpair_heads(a, axis, width=HEAD_DIM):
    shp = a.shape
    a = a.reshape(shp[:axis] + (2, GQA, 2, width) + shp[axis + 1:])
    return jnp.swapaxes(a, axis + 1, axis + 2).reshape(shp)


def _pad_rows(a, rows=8):
    return jnp.pad(a, ((0, rows - a.shape[0]), (0, 0)))


def kernel(x, p, mix_pre_g, mix_post_g, ffn_pre_g, ffn_post_g, pool_w, pool_scale, kv_norm_g, w_k, w_v, w_q, w_o, sinks, w_ff_gate, w_ff_up, w_ff_down, ple_norm_g, w_ple_gate, w_ple_proj, loss_target, m_mix_pre_g, m_mix_post_g, m_ffn_pre_g, m_ffn_post_g, m_pool_w, m_pool_scale, m_kv_norm_g, m_w_k, m_w_v, m_w_q, m_w_o, m_sinks, m_w_ff_gate, m_w_ff_up, m_w_ff_down, m_ple_norm_g, m_w_ple_gate, m_w_ple_proj, v_mix_pre_g, v_mix_post_g, v_ffn_pre_g, v_ffn_post_g, v_pool_w, v_pool_scale, v_kv_norm_g, v_w_k, v_w_v, v_w_q, v_w_o, v_sinks, v_w_ff_gate, v_w_ff_up, v_w_ff_down, v_ple_norm_g, v_w_ple_gate, v_w_ple_proj):
    depth = w_ff_gate.shape[0]
    n_a = pool_w.shape[0]
    t, d = x.shape[1], x.shape[2]
    h = x[0]
    tgt = loss_target[0]
    p_all = p.reshape(depth * t, p.shape[-1])
    my_block = _block_index(*_my_place())
    row = lambda g, i: g[i][None, :]
    bf = lambda a: a.astype(BF16)

    full, gathers = [None] * depth, {}
    start_tokens = jnp.zeros((), F32)
    for i in range(depth):
        shards = [bf(w_ff_gate[i].T), bf(w_ff_up[i].T), bf(w_ff_down[i]), bf(w_ple_gate[i]), bf(w_ple_proj[i].T)]
        if i == 0:
            pool0, scale_full = allgather_pieces([bf(pool_w[0].reshape(-1, POOL_GROUP)), _pad_rows(pool_scale)],
                                                 "allgather_pool0")
            order = pool0
        elif i < n_a:
            shards.append(bf(pool_w[i].reshape(-1, POOL_GROUP)))
        else:
            shards += [bf(_pair_heads(w_q[i - n_a], 1)), bf(w_o[i - n_a])]
            if i == n_a:
                shards.append(bf(jnp.concatenate([w_k, w_v], axis=1)))
        gathers[i] = exchange_start(shards, [_gather_zone(s) for s in shards], order, False, f"allgather_start_l{i}")
        order = gathers[i]["token"]
        start_tokens = start_tokens + order[0, 0]
    scale_full = scale_full.reshape(N_DEV, 8, -1)[:, :n_a].transpose(1, 0, 2).reshape(n_a, 1, d)

    cos, sin = _rope_tables(t, start_tokens)
    sink_b = [jnp.broadcast_to(_pair_heads(sinks[j][:, None], 0, 1), (N_HEADS, LANES)) for j in range(depth - n_a)]
    pool_full, wo_full = {}, {}

    saved = []
    kv = hk = None
    for i in range(depth):
        if i > 0:
            full[i] = exchange_wait(gathers[i], h, f"allgather_wait_l{i}")
        s = {"h0": h}
        if i < n_a:
            pool_full[i] = ((pool0 if i == 0 else full[i][5]).reshape(N_DEV, len(POOL_WINDOWS), -1, POOL_GROUP)
                            .transpose(1, 0, 2, 3).reshape(len(POOL_WINDOWS), POOL_GROUP, POOL_GROUP))
            gpre = row(mix_pre_g, i) + start_tokens if i == 0 else row(mix_pre_g, i)
            h1, a = pool_mix_fwd(h, gpre, pool_full[i], scale_full[i], row(mix_post_g, i), row(ffn_pre_g, i))
            if i == 0:
                full[0] = exchange_wait(gathers[0], (h1, cos, sin), "allgather_wait_l0")
        else:
            j = i - n_a
            wo_full[i] = _pair_heads(full[i][6], 0)
            if i == n_a:
                hk, kv = proj_rope_fwd(h, kv_norm_g[None, :], full[i][7], cos, sin, N_KV_HEADS * HEAD_DIM, "kv_proj_fwd")
            hn, q = proj_rope_fwd(h, row(mix_pre_g, i), full[i][5], cos, sin, d, "q_proj_fwd")
            attn = swa_fwd(q, kv, sink_b[j])
            m, h1, a = oproj_post_fwd(attn, wo_full[i], h, row(mix_post_g, i), row(ffn_pre_g, i))
            s.update(hn=hn, q=q, attn=attn, m=m)
        wg_t, wu_t, wd, wpg, wpp_t = full[i][:5]
        f, gte, up, hdn = ffn_fwd(a, wg_t, wu_t, wd)
        s.update(h1=h1, a=a, f=f, gte=gte, up=up, hdn=hdn)
        if i < depth - 1:
            h = post_ple_fwd(h1, f, p_all, i, row(ffn_post_g, i), row(ple_norm_g, i), wpg, wpp_t)
        saved.append(s)

    g_mix_pre, g_mix_post, g_ffn_pre, g_ffn_post, g_ple = ([None] * depth for _ in range(5))
    g_kv = g_sinks = None
    g_scale = [None] * n_a
    landing, scatters = [None] * depth, {}
    dkv_sum = []
    scatter_token = jnp.zeros((), F32)
    for i in reversed(range(depth)):
        s = saved[i]
        wg_t, wu_t, wd, wpg, wpp_t = full[i][:5]
        last = i == depth - 1
        dh2, df, ub, dzb, dppb, gacc = post_ple_bwd(tgt if last else dh, s["h1"], s["f"], p_all, i,
                                                    row(ffn_post_g, i) + scatter_token, row(ple_norm_g, i), wpg, wpp_t,
                                                    from_target=last)
        g_ple[i], g_ffn_post[i] = gacc[0], gacc[1]
        if last:
            loss_row = gacc[2][None, :]
        da, dgte, dup = ffn_bwd_act(df, s["gte"], s["up"], wg_t, wu_t, wd)
        grads = [xty(dgte, s["a"]), xty(dup, s["a"]), xty(s["hdn"], df), xty(ub, dzb), xty(dppb, p_all, i)]
        early = exchange_start(grads, [_scatter_zone(g) for g in grads], dh2, True, f"reduce_scatter_start_l{i}a")
        early_token = early["token"][0, 0]
        if i < n_a:
            dh, dpw, gacc = pool_mix_bwd(s["h0"], dh2, da, row(mix_pre_g, i) + early_token, pool_full[i], scale_full[i],
                                         row(mix_post_g, i), row(ffn_pre_g, i))
            g_mix_pre[i], g_mix_post[i], g_ffn_pre[i], g_scale[i] = gacc[0], gacc[1], gacc[2], gacc[3]
            dpw = dpw.reshape(len(POOL_WINDOWS), N_DEV, -1, POOL_GROUP).transpose(1, 0, 2, 3)
            grads = [bf(dpw.reshape(-1, POOL_GROUP))]
        else:
            j = i - n_a
            dh1, dmb, dattn, gacc = oproj_post_bwd(dh2, da, s["h1"], s["m"], wo_full[i], row(mix_post_g, i) + early_token,
                                                   row(ffn_pre_g, i))
            g_mix_post[i], g_ffn_pre[i] = gacc[0], gacc[1]
            dq, dkv, dsink = swa_bwd(s["q"], kv, dattn, sink_b[j])
            dkv_sum.append(dkv)
            g_sinks = [_unpair_heads(dsink[:, 0:1], 0, 1)[:, 0]] + (g_sinks or [])
            branches = [(row(mix_pre_g, i), full[i][5], d, [dq])]
            if i == n_a:
                branches.append((kv_norm_g[None, :], full[i][7], N_KV_HEADS * HEAD_DIM, dkv_sum))
            outs = proj_rope_bwd(dh1, s["h0"], cos, sin, branches, f"proj_bwd_l{i}")
            dh, gacc = outs[0], outs[-1]
            g_mix_pre[i] = gacc[0]
            grads = [xty(s["hn"], outs[1]), _unpair_heads(xty(s["attn"], dmb), 0)]
            if i == n_a:
                g_kv = gacc[1]
                grads.append(xty(hk, outs[2]))
        late = exchange_start(grads, [_scatter_zone(g) for g in grads], dh, True, f"reduce_scatter_start_l{i}b")
        scatter_token = late["token"][0, 0]
        scatters[i] = (early, late)
    grad_x = dh[None]
    after = dh
    for i in reversed(range(depth)):
        landing[i] = (exchange_wait(scatters[i][0], after, f"reduce_scatter_wait_l{i}a")
                      + exchange_wait(scatters[i][1], after, f"reduce_scatter_wait_l{i}b"))
        after = landing[i][0]

    sink_row = jnp.pad(jnp.concatenate(g_sinks)[None, :], ((0, 0), (0, d - sinks.size)))
    stack = lambda rows_: _pad_rows(jnp.stack(rows_))
    pack = jnp.concatenate([stack(g_mix_pre), stack(g_mix_post), stack(g_ffn_pre), stack(g_ffn_post), stack(g_ple),
                            _pad_rows(g_kv[None]), stack(g_scale), _pad_rows(sink_row), _pad_rows(loss_row)], axis=0)
    tot = allreduce_small(pack)
    sec = lambda k, n: tot[8 * k:8 * k + n]
    loss = jnp.sum(tot[64])
    small = {
        "mix_pre_g": sec(0, depth), "mix_post_g": sec(1, depth), "ffn_pre_g": sec(2, depth),
        "ffn_post_g": sec(3, depth), "ple_norm_g": sec(4, depth), "kv_norm_g": tot[40],
        "pool_scale": lax.dynamic_slice_in_dim(sec(6, n_a), my_block * pool_scale.shape[1], pool_scale.shape[1], axis=1),
        "sinks": tot[56, :sinks.size].reshape(sinks.shape),
    }

    weights = dict(mix_pre_g=mix_pre_g, mix_post_g=mix_post_g, ffn_pre_g=ffn_pre_g, ffn_post_g=ffn_post_g, pool_w=pool_w, pool_scale=pool_scale, kv_norm_g=kv_norm_g, w_k=w_k, w_v=w_v, w_q=w_q, w_o=w_o, sinks=sinks, w_ff_gate=w_ff_gate, w_ff_up=w_ff_up, w_ff_down=w_ff_down, ple_norm_g=ple_norm_g, w_ple_gate=w_ple_gate, w_ple_proj=w_ple_proj)
    mom1 = dict(mix_pre_g=m_mix_pre_g, mix_post_g=m_mix_post_g, ffn_pre_g=m_ffn_pre_g, ffn_post_g=m_ffn_post_g, pool_w=m_pool_w, pool_scale=m_pool_scale, kv_norm_g=m_kv_norm_g, w_k=m_w_k, w_v=m_w_v, w_q=m_w_q, w_o=m_w_o, sinks=m_sinks, w_ff_gate=m_w_ff_gate, w_ff_up=m_w_ff_up, w_ff_down=m_w_ff_down, ple_norm_g=m_ple_norm_g, w_ple_gate=m_w_ple_gate, w_ple_proj=m_w_ple_proj)
    mom2 = dict(mix_pre_g=v_mix_pre_g, mix_post_g=v_mix_post_g, ffn_pre_g=v_ffn_pre_g, ffn_post_g=v_ffn_post_g, pool_w=v_pool_w, pool_scale=v_pool_scale, kv_norm_g=v_kv_norm_g, w_k=v_w_k, w_v=v_w_v, w_q=v_w_q, w_o=v_w_o, sinks=v_sinks, w_ff_gate=v_w_ff_gate, w_ff_up=v_w_ff_up, w_ff_down=v_w_ff_down, ple_norm_g=v_ple_norm_g, w_ple_gate=v_w_ple_gate, w_ple_proj=v_w_ple_proj)

    def land(i, k):
        return sum_parts(landing[i][k])

    gw = dict(small)
    gw["kv_norm_g"] = small["kv_norm_g"]
    hidden_major = ("w_ff_gate", "w_ff_up")
    gw["w_ff_gate"] = jnp.stack([land(i, 0) for i in range(depth)])
    gw["w_ff_up"] = jnp.stack([land(i, 1) for i in range(depth)])
    gw["w_ff_down"] = jnp.stack([land(i, 2) for i in range(depth)])
    gw["w_ple_gate"] = jnp.stack([land(i, 3) for i in range(depth)])
    gw["w_ple_proj"] = jnp.stack([land(i, 4).T for i in range(depth)])
    gw["pool_w"] = jnp.stack([land(i, 5).reshape(pool_w.shape[1:]) for i in range(n_a)])
    gw["w_q"] = jnp.stack([_unpair_heads(land(i, 5), 1) for i in range(n_a, depth)])
    gw["w_o"] = jnp.stack([land(i, 6) for i in range(n_a, depth)])
    gkv = land(n_a, 7)
    gw["w_k"], gw["w_v"] = gkv[:, :w_k.shape[1]], gkv[:, w_k.shape[1]:]

    order = ["mix_pre_g", "mix_post_g", "ffn_pre_g", "ffn_post_g", "pool_w", "pool_scale", "kv_norm_g", "w_k", "w_v",
             "w_q", "w_o", "sinks", "w_ff_gate", "w_ff_up", "w_ff_down", "ple_norm_g", "w_ple_gate", "w_ple_proj"]
    g_out, d_out, m_out, v_out = [], [], [], []
    for nme in order:
        w = weights[nme]
        if nme in hidden_major:
            view = unview = lambda a: jnp.swapaxes(a, 1, 2)
        else:
            view = (lambda a: a[None, :]) if w.ndim == 1 else (lambda a: a)
            unview = lambda a: a.reshape(w.shape)
        g = gw[nme] if nme in hidden_major else view(gw[nme])
        outs = _adamw_nd(view(w), view(mom1[nme]), view(mom2[nme]), g)
        for lst, val in zip((g_out, d_out, m_out, v_out), outs):
            lst.append(unview(val))
    return (loss, grad_x, *g_out, *d_out, *m_out, *v_out)
```

```python
import functools

import jax
import jax.numpy as jnp
from jax import lax
from jax.experimental import pallas as pl
from jax.experimental.pallas import tpu as pltpu

F32 = jnp.float32
BF16 = jnp.bfloat16

N_DEV = 8
HEAD_DIM = 64
N_HEADS = 16
N_KV_HEADS = 4
GQA = N_HEADS // N_KV_HEADS
BLOCK = 128
POOL_WINDOWS = (2, 4, 8, 16)
POOL_GROUP = 256
HALO = 16
ROPE_THETA = 10000.0
RMS_EPS = 1e-6
NEG_INF = -1e30
LANES = 128
ATTN_SUB = 8
XTY_ROWS = 2048
FFN_CHUNK = 768
VMEM_LIMIT = 56 * 1024 * 1024

ADAM_LR = 0.001
ADAM_B1 = 0.9
ADAM_B2 = 0.999
ADAM_EPS = 1e-08
ADAM_WD = 0.01
ADAM_STEP = 10

MESH = pl.DeviceIdType.MESH
ANY = pl.BlockSpec(memory_space=pl.ANY)

NT_DIMS = (((1,), (1,)), ((), ()))
TN_DIMS = (((0,), (0,)), ((), ()))


def _cparams(sem=None, vmem=None):
    kw = {}
    if sem is not None:
        kw["dimension_semantics"] = sem
    if vmem is not None:
        kw["vmem_limit_bytes"] = vmem
    return pltpu.CompilerParams(**kw)


def _rows(tm, n, first=0):
    return pl.BlockSpec((tm, n), lambda i: (i + first, 0))


def _rows_rev(tm, n, nt):
    return pl.BlockSpec((tm, n), lambda i: (nt - 1 - i, 0))


def _const(shape):
    nd = len(shape)
    return pl.BlockSpec(shape, lambda *_: (0,) * nd, pipeline_mode=pl.Buffered(1))


def _resident(shape):
    nd = len(shape)
    return pl.BlockSpec(shape, lambda *_: (0,) * nd)


def _tile_rows(t):
    return 512 if t % 512 == 0 else 128


def _dot(a, b):
    return jnp.dot(a, b, preferred_element_type=F32)


def _dot_nt(a, b):
    return lax.dot_general(a, b, NT_DIMS, preferred_element_type=F32)


def _dot_tn(a, b):
    return lax.dot_general(a, b, TN_DIMS, preferred_element_type=F32)


def _rms_r(x):
    return lax.rsqrt(jnp.mean(x * x, axis=-1, keepdims=True) + RMS_EPS)


def _rms_bwd(x, r, g, dy):
    gy = dy * g
    dx = r * gy - x * (r * r * r * jnp.mean(gy * x, axis=-1, keepdims=True))
    dg = jnp.sum(dy * (x * r), axis=0, keepdims=True)
    return dx, dg


def _sigmoid(x):
    return jax.nn.sigmoid(x)


def _rope_tables(t, zero_token):
    inv = 1.0 / (ROPE_THETA ** (jnp.arange(0, HEAD_DIM, 2, dtype=F32) / HEAD_DIM))
    ang = (jnp.arange(t, dtype=F32) + zero_token)[:, None] * jnp.tile(inv, 2 * LANES // HEAD_DIM)[None, :]
    sign = jnp.tile(jnp.repeat(jnp.array([-1.0, 1.0], F32), HEAD_DIM // 2), LANES // HEAD_DIM)
    return jnp.cos(ang), jnp.sin(ang) * sign[None, :]


def _swap_halves(x):
    n = x.shape[1]
    lane = lax.broadcasted_iota(jnp.int32, x.shape, 1)
    first = (lane % HEAD_DIM) < (HEAD_DIM // 2)
    return jnp.where(first, pltpu.roll(x, n - HEAD_DIM // 2, 1), pltpu.roll(x, HEAD_DIM // 2, 1))


def _rope(x, cos, sin):
    reps = x.shape[1] // LANES
    return x * jnp.tile(cos, (1, reps)) + _swap_halves(x) * jnp.tile(sin, (1, reps))


def _unrope(dy, cos, sin):
    reps = dy.shape[1] // LANES
    return dy * jnp.tile(cos, (1, reps)) + _swap_halves(dy * jnp.tile(sin, (1, reps)))


def _acc_init(acc_ref):
    @pl.when(pl.program_id(0) == 0)
    def _():
        acc_ref[...] = jnp.zeros_like(acc_ref)


def _window_sums(ext, tm, forward):
    n = tm + HALO
    out = []
    for g, w in enumerate(POOL_WINDOWS):
        s = ext[:, g * POOL_GROUP:(g + 1) * POOL_GROUP]
        k = 1
        while k < w:
            s = s + pltpu.roll(s, k if forward else n - k, 0)
            k *= 2
        out.append(s[HALO:, :] if forward else s[:tm, :])
    return out


def _pool_inv_counts(tile, tm):
    t = tile * tm + lax.broadcasted_iota(jnp.int32, (tm, 1), 0)
    return [1.0 / jnp.minimum(t + 1, w).astype(F32) for w in POOL_WINDOWS]


def _pool_mix(hn, ext, inv_cnts, pw_ref, scale, tm):
    sums = _window_sums(ext, tm, True)
    pooled, ys = [], []
    for g in range(len(POOL_WINDOWS)):
        pg = (sums[g] * inv_cnts[g] - hn[:, g * POOL_GROUP:(g + 1) * POOL_GROUP]).astype(BF16)
        pooled.append(pg)
        ys.append(_dot(pg, pw_ref[g]))
    y = jnp.concatenate(ys, axis=1)
    return pooled, y, y * scale


def pool_mix_fwd(h0, gpre, pool_w, scale, gpost, gffn):
    t, d = h0.shape
    tm = _tile_rows(t)

    def body(h_ref, gpre_ref, pw_ref, scale_ref, gpost_ref, gffn_ref, h1_ref, a_ref, carry):
        i = pl.program_id(0)

        @pl.when(i == 0)
        def _():
            carry[...] = jnp.zeros_like(carry)

        x = h_ref[...]
        hn = x * _rms_r(x) * gpre_ref[...]
        ext = jnp.concatenate([carry[...], hn], axis=0)
        carry[...] = hn[tm - HALO:, :]
        _, _, m = _pool_mix(hn, ext, _pool_inv_counts(i, tm), pw_ref, scale_ref[...], tm)
        h1 = x + m * _rms_r(m) * gpost_ref[...]
        h1_ref[...] = h1
        a_ref[...] = (h1 * _rms_r(h1) * gffn_ref[...]).astype(BF16)

    return pl.pallas_call(
        functools.partial(body), name="pool_mix_fwd", grid=(t // tm,),
        in_specs=[_rows(tm, d), _const((1, d)), _const(pool_w.shape), _const((1, d)), _const((1, d)), _const((1, d))],
        out_specs=[_rows(tm, d), _rows(tm, d)],
        out_shape=[jax.ShapeDtypeStruct((t, d), F32), jax.ShapeDtypeStruct((t, d), BF16)],
        scratch_shapes=[pltpu.VMEM((HALO, d), F32)],
        compiler_params=_cparams(("arbitrary",), VMEM_LIMIT),
    )(h0, gpre, pool_w, scale, gpost, gffn)


def pool_mix_bwd(h0, dh2, da, gpre, pool_w, scale, gpost, gffn):
    t, d = h0.shape
    tm = _tile_rows(t)
    nt = t // tm
    hb = tm // HALO

    def body(h_ref, halo_ref, dh2_ref, da_ref, gpre_ref, pw_ref, scale_ref, gpost_ref, gffn_ref,
             dh0_ref, dpw_ref, gacc_ref, carry):
        i = pl.program_id(0)
        tile = nt - 1 - i
        _acc_init(gacc_ref)
        _acc_init(dpw_ref)

        @pl.when(i == 0)
        def _():
            carry[...] = jnp.zeros_like(carry)

        x = h_ref[...]
        gpre_v, scale_v, gpost_v, gffn_v = gpre_ref[...], scale_ref[...], gpost_ref[...], gffn_ref[...]
        r0 = _rms_r(x)
        hn = x * r0 * gpre_v
        xh = halo_ref[...]
        hn_halo = jnp.where(tile > 0, xh * _rms_r(xh) * gpre_v, 0.0)
        ext = jnp.concatenate([hn_halo, hn], axis=0)
        inv_cnts = _pool_inv_counts(tile, tm)
        pooled, y, m = _pool_mix(hn, ext, inv_cnts, pw_ref, scale_v, tm)
        rm = _rms_r(m)
        h1 = x + m * rm * gpost_v
        dh1_n, dgffn = _rms_bwd(h1, _rms_r(h1), gffn_v, da_ref[...])
        dh1 = dh2_ref[...] + dh1_n
        dm, dgpost = _rms_bwd(m, rm, gpost_v, dh1)
        dscale = jnp.sum(dm * y, axis=0, keepdims=True)
        dy = (dm * scale_v).astype(BF16)
        dpn = []
        for g in range(len(POOL_WINDOWS)):
            dyg = dy[:, g * POOL_GROUP:(g + 1) * POOL_GROUP]
            dpw_ref[g] += _dot_tn(pooled[g], dyg)
            dpn.append(_dot_nt(dyg, pw_ref[g]))
        dpooled = jnp.concatenate(dpn, axis=1)
        dpc = jnp.concatenate([dpn[g] * inv_cnts[g] for g in range(len(POOL_WINDOWS))], axis=1)
        ext2 = jnp.concatenate([dpc, carry[...]], axis=0)
        carry[...] = dpc[:HALO, :]
        dhn = jnp.concatenate(_window_sums(ext2, tm, False), axis=1) - dpooled
        dh0_n, dgpre = _rms_bwd(x, r0, gpre_v, dhn)
        dh0_ref[...] = dh1 + dh0_n
        gacc_ref[0:1, :] += dgpre
        gacc_ref[1:2, :] += dgpost
        gacc_ref[2:3, :] += dgffn
        gacc_ref[3:4, :] += dscale

    return pl.pallas_call(
        functools.partial(body), name="pool_mix_bwd", grid=(nt,),
        in_specs=[_rows_rev(tm, d, nt),
                  pl.BlockSpec((HALO, d), lambda i: (jnp.maximum((nt - 1 - i) * hb - 1, 0), 0)),
                  _rows_rev(tm, d, nt), _rows_rev(tm, d, nt),
                  _const((1, d)), _const(pool_w.shape), _const((1, d)), _const((1, d)), _const((1, d))],
        out_specs=[_rows_rev(tm, d, nt), _resident(pool_w.shape), _resident((8, d))],
        out_shape=[jax.ShapeDtypeStruct((t, d), F32), jax.ShapeDtypeStruct(pool_w.shape, F32),
                   jax.ShapeDtypeStruct((8, d), F32)],
        scratch_shapes=[pltpu.VMEM((HALO, d), F32)],
        compiler_params=_cparams(("arbitrary",), VMEM_LIMIT),
    )(h0, h0, dh2, da, gpre, pool_w, scale, gpost, gffn)


def _ffn_chunks(f):
    return [(c, min(c + FFN_CHUNK, f)) for c in range(0, f, FFN_CHUNK)]


def ffn_fwd(a, wg_t, wu_t, wd):
    t, d = a.shape
    f = wd.shape[0]
    tm = _tile_rows(t)

    def body(a_ref, wg_ref, wu_ref, wd_ref, f_ref, gte_ref, up_ref, hdn_ref):
        av = a_ref[...]
        acc = jnp.zeros((tm, d), F32)
        for c0, c1 in _ffn_chunks(f):
            gte = _dot_nt(av, wg_ref[c0:c1, :])
            up = _dot_nt(av, wu_ref[c0:c1, :])
            gte_ref[:, c0:c1] = gte.astype(BF16)
            up_ref[:, c0:c1] = up.astype(BF16)
            hdn = (gte * _sigmoid(gte) * up).astype(BF16)
            hdn_ref[:, c0:c1] = hdn
            acc = acc + _dot(hdn, wd_ref[c0:c1, :])
        f_ref[...] = acc.astype(BF16)

    return pl.pallas_call(
        functools.partial(body), name="ffn_fwd", grid=(t // tm,),
        in_specs=[_rows(tm, d), _const((f, d)), _const((f, d)), _const((f, d))],
        out_specs=[_rows(tm, d), _rows(tm, f), _rows(tm, f), _rows(tm, f)],
        out_shape=[jax.ShapeDtypeStruct((t, d), BF16)] + [jax.ShapeDtypeStruct((t, f), BF16)] * 3,
        compiler_params=_cparams(("parallel",), VMEM_LIMIT),
    )(a, wg_t, wu_t, wd)


def ffn_bwd_act(df, gte, up, wg_t, wu_t, wd):
    t, d = df.shape
    f = wd.shape[0]
    tm = _tile_rows(t)

    def body(df_ref, gte_ref, up_ref, wg_ref, wu_ref, wd_ref, da_ref, dgte_ref, dup_ref):
        dfv = df_ref[...]
        chunks = _ffn_chunks(f)
        half = chunks[len(chunks) // 2][0]
        acc = None
        for c0, c1 in chunks:
            g = gte_ref[:, c0:c1].astype(F32)
            u = up_ref[:, c0:c1].astype(F32)
            sg = _sigmoid(g)
            sl = g * sg
            dh = _dot_nt(dfv, wd_ref[c0:c1, :])
            dup_ref[:, c0:c1] = (dh * sl).astype(BF16)
            dgte_ref[:, c0:c1] = (dh * u * (sg * (1.0 + g * (1.0 - sg)))).astype(BF16)
            if c1 == half:
                acc = _dot(dgte_ref[:, :half], wg_ref[:half, :]) + _dot(dup_ref[:, :half], wu_ref[:half, :])
        da_ref[...] = acc + _dot(dgte_ref[:, half:], wg_ref[half:, :]) + _dot(dup_ref[:, half:], wu_ref[half:, :])

    return pl.pallas_call(
        functools.partial(body), name="ffn_bwd_act", grid=(t // tm,),
        in_specs=[_rows(tm, d), _rows(tm, f), _rows(tm, f), _const((f, d)), _const((f, d)), _const((f, d))],
        out_specs=[_rows(tm, d), _rows(tm, f), _rows(tm, f)],
        out_shape=[jax.ShapeDtypeStruct((t, d), F32)] + [jax.ShapeDtypeStruct((t, f), BF16)] * 2,
        compiler_params=_cparams(("parallel",), VMEM_LIMIT),
    )(df, gte, up, wg_t, wu_t, wd)


def xty(x, y, y_part=0):
    t, nx = x.shape
    ny = y.shape[1]
    tk = XTY_ROWS if t % XTY_ROWS == 0 else _tile_rows(t)
    bn = nx // 2 if nx > 1024 else nx
    nk = t // tk

    def body(x_ref, y_ref, o_ref, acc):
        k = pl.program_id(1)

        @pl.when(k == 0)
        def _():
            acc[...] = jnp.zeros_like(acc)

        acc[...] += _dot_tn(x_ref[...].astype(BF16), y_ref[...].astype(BF16))

        @pl.when(k == nk - 1)
        def _():
            o_ref[...] = acc[...].astype(BF16)

    return pl.pallas_call(
        functools.partial(body), name="xty", grid=(nx // bn, nk),
        in_specs=[pl.BlockSpec((tk, bn), lambda j, k: (k, j)),
                  pl.BlockSpec((tk, ny), lambda j, k: (k + y_part * nk, 0))],
        out_specs=pl.BlockSpec((bn, ny), lambda j, k: (j, 0)),
        out_shape=jax.ShapeDtypeStruct((nx, ny), BF16),
        scratch_shapes=[pltpu.VMEM((bn, ny), F32)],
        compiler_params=_cparams(("parallel", "arbitrary"), VMEM_LIMIT),
    )(x, y)


def _ple_fwd_tile(h1, f, p, gpost, gple, wpg_ref, wpp_ref):
    rf = _rms_r(f)
    h2 = h1 + f * rf * gpost
    r2 = _rms_r(h2)
    ub = (h2 * r2 * gple).astype(BF16)
    gate = _sigmoid(_dot(ub, wpg_ref[...]))
    pp = _dot_nt(p.astype(BF16), wpp_ref[...])
    return rf, h2, r2, ub, gate, pp


def post_ple_fwd(h1, f, p, layer, gpost, gple, wpg, wpp_t):
    t, d = h1.shape
    pd = p.shape[1]
    tm = _tile_rows(t)

    def body(h1_ref, f_ref, p_ref, gpost_ref, gple_ref, wpg_ref, wpp_ref, out_ref):
        _, h2, _, _, gate, pp = _ple_fwd_tile(h1_ref[...], f_ref[...].astype(F32), p_ref[...], gpost_ref[...],
                                              gple_ref[...], wpg_ref, wpp_ref)
        out_ref[...] = h2 + pp * gate

    return pl.pallas_call(
        functools.partial(body), name="post_ple_fwd", grid=(t // tm,),
        in_specs=[_rows(tm, d), _rows(tm, d), _rows(tm, pd, layer * (t // tm)), _const((1, d)), _const((1, d)),
                  _const(wpg.shape), _const(wpp_t.shape)],
        out_specs=_rows(tm, d), out_shape=jax.ShapeDtypeStruct((t, d), F32),
        compiler_params=_cparams(("parallel",), VMEM_LIMIT),
    )(h1, f, p, gpost, gple, wpg, wpp_t)


def post_ple_bwd(dh3, h1, f, p, layer, gpost, gple, wpg, wpp_t, from_target=False):
    t, d = h1.shape
    pd = p.shape[1]
    tm = _tile_rows(t)

    def body(dh3_ref, h1_ref, f_ref, p_ref, gpost_ref, gple_ref, wpg_ref, wpp_ref,
             dh2_ref, df_ref, u_ref, dz_ref, dpp_ref, gacc_ref):
        _acc_init(gacc_ref)
        gpost_v, gple_v = gpost_ref[...], gple_ref[...]
        nsub = 2 if tm % 16 == 0 else 1
        for sb in range(nsub):
            rows = slice(sb * (tm // nsub), (sb + 1) * (tm // nsub))
            fv = f_ref[rows, :].astype(F32)
            rf, h2, r2, ub, gate, pp = _ple_fwd_tile(h1_ref[rows, :], fv, p_ref[rows, :], gpost_v, gple_v, wpg_ref,
                                                     wpp_ref)
            if from_target:
                err = h2 + pp * gate - dh3_ref[rows, :]
                dh3v = err * (1.0 / d)
                gacc_ref[2:3, :] += jnp.sum(err * err, axis=0, keepdims=True) * (0.5 / d)
            else:
                dh3v = dh3_ref[rows, :]
            dpp_ref[rows, :] = (dh3v * gate).astype(BF16)
            dz = (dh3v * pp * gate * (1.0 - gate)).astype(BF16)
            dz_ref[rows, :] = dz
            u_ref[rows, :] = ub
            du = _dot_nt(dz, wpg_ref[...])
            dh2_n, dgple = _rms_bwd(h2, r2, gple_v, du)
            dh2 = dh3v + dh2_n
            df, dgpost = _rms_bwd(fv, rf, gpost_v, dh2)
            dh2_ref[rows, :] = dh2
            df_ref[rows, :] = df.astype(BF16)
            gacc_ref[0:1, :] += dgple
            gacc_ref[1:2, :] += dgpost

    return pl.pallas_call(
        functools.partial(body), name="post_ple_loss_bwd" if from_target else "post_ple_bwd", grid=(t // tm,),
        in_specs=[_rows(tm, d), _rows(tm, d), _rows(tm, d), _rows(tm, pd, layer * (t // tm)), _const((1, d)),
                  _const((1, d)), _const(wpg.shape), _const(wpp_t.shape)],
        out_specs=[_rows(tm, d)] * 5 + [_resident((8, d))],
        out_shape=[jax.ShapeDtypeStruct((t, d), F32)] + [jax.ShapeDtypeStruct((t, d), BF16)] * 4
        + [jax.ShapeDtypeStruct((8, d), F32)],
        compiler_params=_cparams(("arbitrary",), VMEM_LIMIT),
    )(dh3, h1, f, p, gpost, gple, wpg, wpp_t)


def proj_rope_fwd(h, gain, w, cos, sin, n_rope, name):
    t, d = h.shape
    n = w.shape[1]
    tm = _tile_rows(t)

    def body(h_ref, g_ref, w_ref, cos_ref, sin_ref, hn_ref, y_ref):
        x = h_ref[...]
        hn = (x * _rms_r(x) * g_ref[...]).astype(BF16)
        hn_ref[...] = hn
        y = _dot(hn, w_ref[...])
        y_ref[:, :n_rope] = _rope(y[:, :n_rope], cos_ref[...], sin_ref[...]).astype(BF16)
        if n_rope < n:
            y_ref[:, n_rope:] = y[:, n_rope:].astype(BF16)

    return pl.pallas_call(
        functools.partial(body), name=name, grid=(t // tm,),
        in_specs=[_rows(tm, d), _const((1, d)), _const(w.shape), _rows(tm, LANES), _rows(tm, LANES)],
        out_specs=[_rows(tm, d), _rows(tm, n)],
        out_shape=[jax.ShapeDtypeStruct((t, d), BF16), jax.ShapeDtypeStruct((t, n), BF16)],
        compiler_params=_cparams(("parallel",), VMEM_LIMIT),
    )(h, gain, w, cos, sin)


def proj_rope_bwd(dh1, h0, cos, sin, branches, name):
    t, d = h0.shape
    tm = _tile_rows(t)
    nb = len(branches)
    n_cot = [len(b[3]) for b in branches]

    def body(*refs):
        dh1_ref, h0_ref, cos_ref, sin_ref = refs[:4]
        pos = 4
        br_refs = []
        for b in range(nb):
            br_refs.append((refs[pos], refs[pos + 1], refs[pos + 2:pos + 2 + n_cot[b]]))
            pos += 2 + n_cot[b]
        dh0_ref = refs[pos]
        dpre_refs = refs[pos + 1:pos + 1 + nb]
        gacc_ref = refs[pos + 1 + nb]
        _acc_init(gacc_ref)
        x = h0_ref[...]
        r0 = _rms_r(x)
        dh = dh1_ref[...]
        for b in range(nb):
            g_ref, w_ref, cot_refs = br_refs[b]
            n_rope = branches[b][2]
            dy = cot_refs[0][...].astype(F32)
            for c_ref in cot_refs[1:]:
                dy = dy + c_ref[...].astype(F32)
            n = dy.shape[1]
            dpre_refs[b][:, :n_rope] = _unrope(dy[:, :n_rope], cos_ref[...], sin_ref[...]).astype(BF16)
            if n_rope < n:
                dpre_refs[b][:, n_rope:] = dy[:, n_rope:].astype(BF16)
            dhn = _dot_nt(dpre_refs[b][...], w_ref[...])
            dx, dg = _rms_bwd(x, r0, g_ref[...], dhn)
            dh = dh + dx
            gacc_ref[b:b + 1, :] += dg
        dh0_ref[...] = dh

    in_specs = [_rows(tm, d), _rows(tm, d), _rows(tm, LANES), _rows(tm, LANES)]
    args = [dh1, h0, cos, sin]
    out_specs = [_rows(tm, d)]
    out_shape = [jax.ShapeDtypeStruct((t, d), F32)]
    for gain, w, _, cots in branches:
        n = w.shape[1]
        in_specs += [_const((1, d)), _const(w.shape)] + [_rows(tm, n)] * len(cots)
        args += [gain, w] + list(cots)
        out_specs.append(_rows(tm, n))
        out_shape.append(jax.ShapeDtypeStruct((t, n), BF16))
    out_specs.append(_resident((8, d)))
    out_shape.append(jax.ShapeDtypeStruct((8, d), F32))
    return pl.pallas_call(
        functools.partial(body), name=name, grid=(t // tm,),
        in_specs=in_specs, out_specs=out_specs, out_shape=out_shape,
        compiler_params=_cparams(("arbitrary",), VMEM_LIMIT),
    )(*args)


def _tri():
    row = lax.broadcasted_iota(jnp.int32, (BLOCK, BLOCK), 0)
    col = lax.broadcasted_iota(jnp.int32, (BLOCK, BLOCK), 1)
    return col <= row


def _block_diag(x):
    lo = lax.broadcasted_iota(jnp.int32, x.shape, 1) < HEAD_DIM
    zero = jnp.zeros_like(x)
    return jnp.concatenate([jnp.where(lo, x, zero), jnp.where(lo, zero, x)], axis=0)


def _dense(x, tri):
    return (jnp.where(tri, x[:, BLOCK:2 * BLOCK], x[:, :BLOCK]),
            jnp.where(tri, x[:, 3 * BLOCK:], x[:, 2 * BLOCK:3 * BLOCK]))


def _banded(xa, xb, tri):
    zero = jnp.zeros_like(xa)
    return jnp.concatenate([jnp.where(tri, zero, xa), jnp.where(tri, xa, zero),
                            jnp.where(tri, zero, xb), jnp.where(tri, xb, zero)], axis=1).astype(BF16)


def _softmax_sink(s, sink):
    mx = jnp.maximum(jnp.max(s, axis=1, keepdims=True), sink)
    e = jnp.exp(s - mx)
    es = jnp.exp(sink - mx)
    inv = 1.0 / (jnp.sum(e, axis=1, keepdims=True) + es)
    return e * inv, es * inv


def _sink_column(sink_ref):
    return jnp.concatenate([jnp.broadcast_to(sink_ref[h:h + 1, 0:1], (BLOCK, 1)) for h in range(N_HEADS)], axis=0)


def _kv_block_diag(band, kvw):
    n_lt = kvw // LANES
    return ([_block_diag(band[:, lt * LANES:(lt + 1) * LANES]) for lt in range(n_lt)],
            [_block_diag(band[:, kvw + lt * LANES:kvw + (lt + 1) * LANES]) for lt in range(n_lt)])


def _all_probs(q_ref, r0, kbd, tri, n, sink_ref):
    dense = []
    for tq in range(N_HEADS // 2):
        s = _dot_nt(q_ref[r0:r0 + BLOCK, tq * LANES:(tq + 1) * LANES], kbd[tq // GQA])
        dense += list(_dense(s, tri))
    bias = jnp.where(jnp.logical_not(tri) & (n == 0), NEG_INF, 0.0)
    s_all = jnp.concatenate(dense, axis=0) * (HEAD_DIM ** -0.5) + jnp.concatenate([bias] * N_HEADS, axis=0)
    return _softmax_sink(s_all, _sink_column(sink_ref))


def _head_rows(x, tq):
    return x[2 * tq * BLOCK:(2 * tq + 1) * BLOCK], x[(2 * tq + 1) * BLOCK:(2 * tq + 2) * BLOCK]


def _attn_sub(t):
    return ATTN_SUB if t % (ATTN_SUB * BLOCK) == 0 else 1


def swa_fwd(q, kv, sink_b):
    t, d = q.shape
    sub = _attn_sub(t)
    kvw = N_KV_HEADS * HEAD_DIM

    def body(q_ref, kvc_ref, kvp_ref, sink_ref, o_ref):
        i = pl.program_id(0)
        tri = _tri()
        ext = jnp.concatenate([kvp_ref[...], kvc_ref[...]], axis=0)
        for sb in range(sub):
            r0 = sb * BLOCK
            kbd, vbd = _kv_block_diag(ext[r0:r0 + 2 * BLOCK], kvw)
            p, _ = _all_probs(q_ref, r0, kbd, tri, i * sub + sb, sink_ref)
            for tq in range(N_HEADS // 2):
                pa, pb = _head_rows(p, tq)
                o_ref[r0:r0 + BLOCK, tq * LANES:(tq + 1) * LANES] = _dot(_banded(pa, pb, tri), vbd[tq // GQA]).astype(BF16)

    return pl.pallas_call(
        functools.partial(body), name="swa_fwd", grid=(t // (sub * BLOCK),),
        in_specs=[_rows(sub * BLOCK, d), _rows(sub * BLOCK, 2 * kvw),
                  pl.BlockSpec((BLOCK, 2 * kvw), lambda i: (jnp.maximum(i * sub - 1, 0), 0)), _const(sink_b.shape)],
        out_specs=_rows(sub * BLOCK, d),
        out_shape=jax.ShapeDtypeStruct((t, d), BF16),
        compiler_params=_cparams(("parallel",), VMEM_LIMIT),
    )(q, kv, kv, sink_b)


def swa_bwd(q, kv, do, sink_b):
    t, d = q.shape
    sub = _attn_sub(t)
    nq = t // (sub * BLOCK)
    kvw = N_KV_HEADS * HEAD_DIM

    def body(q_ref, do_ref, kvc_ref, kvp_ref, sink_ref, dq_ref, dkv_ref, dsink_ref, carry):
        i = pl.program_id(0)
        step = nq - 1 - i
        _acc_init(dsink_ref)

        @pl.when(i == 0)
        def _():
            carry[...] = jnp.zeros_like(carry)

        tri = _tri()
        lo = lax.broadcasted_iota(jnp.int32, (2 * BLOCK, LANES), 1) < HEAD_DIM
        ext = jnp.concatenate([kvp_ref[...], kvc_ref[...]], axis=0)
        dkeys = [None] * (sub + 1)
        for sb in reversed(range(sub)):
            r0 = sb * BLOCK
            kbd, vbd = _kv_block_diag(ext[r0:r0 + 2 * BLOCK], kvw)
            p, ps = _all_probs(q_ref, r0, kbd, tri, step * sub + sb, sink_ref)
            dp = []
            for tq in range(N_HEADS // 2):
                dp += list(_dense(_dot_nt(do_ref[r0:r0 + BLOCK, tq * LANES:(tq + 1) * LANES], vbd[tq // GQA]), tri))
            dp = jnp.concatenate(dp, axis=0)
            delta = jnp.sum(p * dp, axis=1, keepdims=True)
            ds = p * (dp - delta) * (HEAD_DIM ** -0.5)
            dsk = ps * delta
            for h in range(N_HEADS):
                dsink_ref[h:h + 1, :] -= jnp.sum(dsk[h * BLOCK:(h + 1) * BLOCK], axis=0, keepdims=True)
            dkb = [jnp.zeros((4 * BLOCK, LANES), F32) for _ in kbd]
            dvb = [jnp.zeros((4 * BLOCK, LANES), F32) for _ in kbd]
            for tq in range(N_HEADS // 2):
                lt = tq // GQA
                cols = slice(tq * LANES, (tq + 1) * LANES)
                dsb = _banded(*_head_rows(ds, tq), tri)
                dq_ref[r0:r0 + BLOCK, cols] = _dot(dsb, kbd[lt]).astype(BF16)
                dkb[lt] = dkb[lt] + _dot_tn(dsb, q_ref[r0:r0 + BLOCK, cols])
                dvb[lt] = dvb[lt] + _dot_tn(_banded(*_head_rows(p, tq), tri), do_ref[r0:r0 + BLOCK, cols])
            dall = jnp.concatenate([jnp.where(lo, x[:2 * BLOCK], x[2 * BLOCK:]) for x in dkb + dvb], axis=1)
            dkeys[sb + 1] = dall[BLOCK:] if dkeys[sb + 1] is None else dkeys[sb + 1] + dall[BLOCK:]
            dkeys[sb] = dall[:BLOCK]
        for sb in range(sub):
            own = dkeys[sb + 1] + carry[...] if sb == sub - 1 else dkeys[sb + 1]
            dkv_ref[sb * BLOCK:(sb + 1) * BLOCK, :] = own
        carry[...] = dkeys[0]

    rev = lambda i: (nq - 1 - i, 0)
    return pl.pallas_call(
        functools.partial(body), name="swa_bwd", grid=(nq,),
        in_specs=[pl.BlockSpec((sub * BLOCK, d), rev), pl.BlockSpec((sub * BLOCK, d), rev),
                  pl.BlockSpec((sub * BLOCK, 2 * kvw), rev),
                  pl.BlockSpec((BLOCK, 2 * kvw), lambda i: (jnp.maximum((nq - 1 - i) * sub - 1, 0), 0)),
                  _const(sink_b.shape)],
        out_specs=[pl.BlockSpec((sub * BLOCK, d), rev), pl.BlockSpec((sub * BLOCK, 2 * kvw), rev),
                   _resident(sink_b.shape)],
        out_shape=[jax.ShapeDtypeStruct((t, d), BF16), jax.ShapeDtypeStruct((t, 2 * kvw), F32),
                   jax.ShapeDtypeStruct(sink_b.shape, F32)],
        scratch_shapes=[pltpu.VMEM((BLOCK, 2 * kvw), F32)],
        compiler_params=_cparams(("arbitrary",), VMEM_LIMIT),
    )(q, do, kv, kv, sink_b)


def oproj_post_fwd(attn, w_o, h0, gpost, gffn):
    t, d = h0.shape
    tm = _tile_rows(t)

    def body(at_ref, w_ref, h0_ref, gpost_ref, gffn_ref, m_ref, h1_ref, a_ref):
        m = _dot(at_ref[...], w_ref[...])
        m_ref[...] = m.astype(BF16)
        h1 = h0_ref[...] + m * _rms_r(m) * gpost_ref[...]
        h1_ref[...] = h1
        a_ref[...] = (h1 * _rms_r(h1) * gffn_ref[...]).astype(BF16)

    return pl.pallas_call(
        functools.partial(body), name="oproj_post_fwd", grid=(t // tm,),
        in_specs=[_rows(tm, d), _const(w_o.shape), _rows(tm, d), _const((1, d)), _const((1, d))],
        out_specs=[_rows(tm, d)] * 3,
        out_shape=[jax.ShapeDtypeStruct((t, d), BF16), jax.ShapeDtypeStruct((t, d), F32),
                   jax.ShapeDtypeStruct((t, d), BF16)],
        compiler_params=_cparams(("parallel",), VMEM_LIMIT),
    )(attn, w_o, h0, gpost, gffn)


def oproj_post_bwd(dh2, da, h1, m, w_o, gpost, gffn):
    t, d = h1.shape
    tm = _tile_rows(t)

    def body(dh2_ref, da_ref, h1_ref, m_ref, w_ref, gpost_ref, gffn_ref, dh1_ref, dm_ref, dat_ref, gacc_ref):
        _acc_init(gacc_ref)
        h1v, mv = h1_ref[...], m_ref[...].astype(F32)
        dh1_n, dgffn = _rms_bwd(h1v, _rms_r(h1v), gffn_ref[...], da_ref[...])
        dh1 = dh2_ref[...] + dh1_n
        dm, dgpost = _rms_bwd(mv, _rms_r(mv), gpost_ref[...], dh1)
        dmb = dm.astype(BF16)
        dh1_ref[...] = dh1
        dm_ref[...] = dmb
        dat_ref[...] = _dot_nt(dmb, w_ref[...]).astype(BF16)
        gacc_ref[0:1, :] += dgpost
        gacc_ref[1:2, :] += dgffn

    return pl.pallas_call(
        functools.partial(body), name="oproj_post_bwd", grid=(t // tm,),
        in_specs=[_rows(tm, d)] * 4 + [_const(w_o.shape), _const((1, d)), _const((1, d))],
        out_specs=[_rows(tm, d)] * 3 + [_resident((8, d))],
        out_shape=[jax.ShapeDtypeStruct((t, d), F32), jax.ShapeDtypeStruct((t, d), BF16),
                   jax.ShapeDtypeStruct((t, d), BF16), jax.ShapeDtypeStruct((8, d), F32)],
        compiler_params=_cparams(("arbitrary",), VMEM_LIMIT),
    )(dh2, da, h1, m, w_o, gpost, gffn)


def _my_place():
    return lax.axis_index("x"), lax.axis_index("y"), lax.axis_index("c")


def _block_index(px, py, pc):
    return 4 * px + 2 * py + pc


def allgather_pieces(shards, name):
    np_ = len(shards)

    def body(*refs):
        in_refs, out_refs = refs[:np_], refs[np_:2 * np_]
        send_sems, recv_sems, local_sems = refs[2 * np_:]
        x, y, c = _my_place()
        me, sibling = (x, y, c), (x, y, 1 - c)
        chips = [(1 - x, y), (x, 1 - y), (1 - x, 1 - y)]

        def rows(p, place):
            r = in_refs[p].shape[0]
            return out_refs[p].at[pl.ds(_block_index(*place) * r, r), :]

        def copy(p, k, block, to, src=None):
            return pltpu.make_async_remote_copy(
                src_ref=rows(p, block) if src is None else src, dst_ref=rows(p, block),
                send_sem=send_sems.at[p, k], recv_sem=recv_sems.at[p, k], device_id=to, device_id_type=MESH)

        mine = [pltpu.make_async_copy(in_refs[p], rows(p, me), local_sems.at[p]) for p in range(np_)]
        first, passed = [], []
        for p in range(np_):
            mine[p].start()
            first.append(copy(p, 0, me, sibling, src=in_refs[p]))
            first += [copy(p, 1 + j, me, (*chip, c), src=in_refs[p]) for j, chip in enumerate(chips)]
        for cp in first:
            cp.start()
        for p in range(np_):
            for j, chip in enumerate(chips):
                copy(p, 1 + j, (*chip, c), me).wait_recv()
                fwd = copy(p, 4 + j, (*chip, c), sibling)
                fwd.start()
                passed.append(fwd)
        for p in range(np_):
            copy(p, 0, sibling, me).wait_recv()
            for j, chip in enumerate(chips):
                copy(p, 4 + j, (*chip, 1 - c), me).wait_recv()
        for cp in first + passed:
            cp.wait_send()
        for cp in mine:
            cp.wait()

    return pl.pallas_call(
        functools.partial(body), name=name,
        in_specs=[ANY] * np_, out_specs=[ANY] * np_,
        out_shape=[jax.ShapeDtypeStruct((N_DEV * s.shape[0], s.shape[1]), s.dtype) for s in shards],
        scratch_shapes=[pltpu.SemaphoreType.DMA((np_, 7)), pltpu.SemaphoreType.DMA((np_, 7)),
                        pltpu.SemaphoreType.DMA((np_,))],
    )(*shards)


def _peers():
    x, y, c = _my_place()
    flips = [(fx, fy, fc) for fx in (0, 1) for fy in (0, 1) for fc in (0, 1)][1:]
    return [(1 - x if fx else x, 1 - y if fy else y, 1 - c if fc else c) for fx, fy, fc in flips]


HBM = pl.BlockSpec(memory_space=pltpu.HBM)
SEM = pl.BlockSpec(memory_space=pltpu.SEMAPHORE)


def _exchange_windows(scatter, src_ref, land_ref, my_block, peer_block):
    if scatter:
        r = land_ref.shape[1]
        return src_ref.at[pl.ds(peer_block * r, r), :], land_ref.at[my_block], land_ref.at[peer_block]
    r = src_ref.shape[0]
    return src_ref, land_ref.at[pl.ds(my_block * r, r), :], land_ref.at[pl.ds(peer_block * r, r), :]


def _own_copy(scatter, src_ref, land_ref, my_block, sem):
    if scatter:
        r = land_ref.shape[1]
        return pltpu.make_async_copy(src_ref.at[pl.ds(my_block * r, r), :], land_ref.at[my_block], sem)
    r = src_ref.shape[0]
    return pltpu.make_async_copy(src_ref, land_ref.at[pl.ds(my_block * r, r), :], sem)


def exchange_start(srcs, lands, after, scatter, name):
    np_ = len(srcs)

    def body(*refs):
        src_refs, land_refs = refs[:np_], refs[np_:2 * np_]
        send_sems, recv_sems, own_sems = refs[2 * np_ + 1:2 * np_ + 4]
        token = refs[-1]
        my_block = _block_index(*_my_place())
        for p in range(np_):
            _own_copy(scatter, src_refs[p], land_refs[p], my_block, own_sems.at[p]).start()
            for k, peer in enumerate(_peers()):
                src, dst, _ = _exchange_windows(scatter, src_refs[p], land_refs[p], my_block, _block_index(*peer))
                pltpu.make_async_remote_copy(src_ref=src, dst_ref=dst, send_sem=send_sems.at[7 * p + k],
                                             recv_sem=recv_sems.at[7 * p + k], device_id=peer, device_id_type=MESH).start()
        token[...] = jnp.zeros_like(token)

    hbm = lambda a: pltpu.with_memory_space_constraint(a, pltpu.HBM)
    outs = pl.pallas_call(
        functools.partial(body), name=name,
        in_specs=[HBM] * (2 * np_) + [ANY],
        out_specs=[SEM, SEM, SEM] + [HBM] * (2 * np_) + [pl.BlockSpec(memory_space=pltpu.VMEM)],
        out_shape=[pltpu.SemaphoreType.DMA((7 * np_,)), pltpu.SemaphoreType.DMA((7 * np_,)), pltpu.SemaphoreType.DMA((np_,))]
        + [pltpu.HBM(a.shape, a.dtype) for a in list(srcs) + list(lands)] + [jax.ShapeDtypeStruct((8, LANES), F32)],
        input_output_aliases={i: 3 + i for i in range(2 * np_)},
        compiler_params=pltpu.CompilerParams(has_side_effects=pltpu.SideEffectType.DATAFLOW_SIDE_EFFECTING),
    )(*[hbm(a) for a in srcs], *[hbm(a) for a in lands], after)
    return dict(sems=outs[:3], srcs=outs[3:3 + np_], lands=outs[3 + np_:3 + 2 * np_], token=outs[-1], scatter=scatter)


def exchange_wait(started, after, name):
    afters = tuple(after) if isinstance(after, (tuple, list)) else (after,)
    srcs, lands = started["srcs"], started["lands"]
    scatter = started["scatter"]
    np_ = len(srcs)

    def body(*refs):
        src_refs, land_refs = refs[:np_], refs[np_:2 * np_]
        send_sems, recv_sems, own_sems = refs[2 * np_:2 * np_ + 3]
        my_block = _block_index(*_my_place())
        for p in range(np_):
            _own_copy(scatter, src_refs[p], land_refs[p], my_block, own_sems.at[p]).wait()
            for k, peer in enumerate(_peers()):
                src, dst, arrival = _exchange_windows(scatter, src_refs[p], land_refs[p], my_block, _block_index(*peer))
                pltpu.make_async_remote_copy(src_ref=src, dst_ref=dst, send_sem=send_sems.at[7 * p + k],
                                             recv_sem=recv_sems.at[7 * p + k], device_id=peer, device_id_type=MESH).wait_send()
                pltpu.make_async_remote_copy(src_ref=src, dst_ref=arrival, send_sem=send_sems.at[7 * p + k],
                                             recv_sem=recv_sems.at[7 * p + k], device_id=peer, device_id_type=MESH).wait_recv()

    outs = pl.pallas_call(
        functools.partial(body), name=name,
        in_specs=[HBM] * (2 * np_) + [SEM, SEM, SEM] + [ANY] * len(afters),
        out_specs=[HBM] * (2 * np_),
        out_shape=[pltpu.HBM(a.shape, a.dtype) for a in list(srcs) + list(lands)],
        input_output_aliases={i: i for i in range(2 * np_)},
        compiler_params=pltpu.CompilerParams(has_side_effects=pltpu.SideEffectType.DATAFLOW_SIDE_EFFECTING),
    )(*srcs, *lands, *started["sems"], *afters)
    return list(outs[np_:])


def _gather_zone(shard):
    return lax.empty((N_DEV * shard.shape[0], shard.shape[1]), shard.dtype)


def _scatter_zone(full):
    return lax.empty((N_DEV, full.shape[0] // N_DEV, full.shape[1]), full.dtype)


def allreduce_small(pack):
    r, c = pack.shape

    def body(pack_ref, out_ref, gathered, send_sems, recv_sems):
        me = _my_place()
        my_block = _block_index(*me)
        peers = _peers()

        def copy(k, slot, to):
            return pltpu.make_async_remote_copy(
                src_ref=pack_ref, dst_ref=gathered.at[slot], send_sem=send_sems.at[k], recv_sem=recv_sems.at[k],
                device_id=to, device_id_type=MESH)

        sends = [copy(k, my_block, peer) for k, peer in enumerate(peers)]
        for cp in sends:
            cp.start()
        gathered[my_block] = pack_ref[...]
        for k, peer in enumerate(peers):
            copy(k, _block_index(*peer), peer).wait_recv()
        for cp in sends:
            cp.wait_send()
        total = gathered[0]
        for j in range(1, N_DEV):
            total = total + gathered[j]
        out_ref[...] = total

    return pl.pallas_call(
        functools.partial(body), name="allreduce_small",
        in_specs=[pl.BlockSpec(memory_space=pltpu.VMEM)], out_specs=pl.BlockSpec(memory_space=pltpu.VMEM),
        out_shape=jax.ShapeDtypeStruct((r, c), F32),
        scratch_shapes=[pltpu.VMEM((N_DEV, r, c), F32), pltpu.SemaphoreType.DMA((7,)), pltpu.SemaphoreType.DMA((7,))],
    )(pack)


def adamw(w, m, v, parts, layer=0, prev=None):
    nl, r, c = w.shape
    n = parts.shape[0]
    br = 256 if r % 256 == 0 else r
    blk = pl.BlockSpec((None, br, c), lambda i: (layer, i, 0))
    n_prev = 0 if prev is None else 4

    def body(w_ref, m_ref, v_ref, p_ref, *rest):
        g_ref, d_ref, nm_ref, nv_ref = rest[n_prev:]
        g = p_ref[0].astype(F32)
        for j in range(1, n):
            g = g + p_ref[j].astype(F32)
        nm = ADAM_B1 * m_ref[...] + (1.0 - ADAM_B1) * g
        nv = ADAM_B2 * v_ref[...] + (1.0 - ADAM_B2) * (g * g)
        m_hat = nm / (1.0 - ADAM_B1 ** ADAM_STEP)
        v_hat = nv / (1.0 - ADAM_B2 ** ADAM_STEP)
        g_ref[...] = g
        d_ref[...] = -ADAM_LR * (m_hat / (jnp.sqrt(v_hat) + ADAM_EPS) + ADAM_WD * w_ref[...])
        nm_ref[...] = nm
        nv_ref[...] = nv

    return pl.pallas_call(
        functools.partial(body), name="adamw", grid=(r // br,),
        in_specs=[blk] * 3 + [pl.BlockSpec((n, br, c), lambda i: (0, i, 0))] + [ANY] * n_prev,
        out_specs=[blk] * 4, out_shape=[jax.ShapeDtypeStruct((nl, r, c), F32)] * 4,
        input_output_aliases={4 + k: k for k in range(n_prev)},
        compiler_params=_cparams(("parallel",)),
    )(w, m, v, parts, *(prev or ()))


def adamw_layers(w, m, v, layer_parts):
    outs = None
    for layer, parts in enumerate(layer_parts):
        outs = adamw(w, m, v, parts, layer, outs)
    return outs


def _pair_heads(a, axis, width=HEAD_DIM):
    shp = a.shape
    a = a.reshape(shp[:axis] + (2, 2, GQA, width) + shp[axis + 1:])
    return jnp.swapaxes(a, axis + 1, axis + 2).reshape(shp)


def _unpair_heads(a, axis, width=HEAD_DIM):
    shp = a.shape
    a = a.reshape(shp[:axis] + (2, GQA, 2, width) + shp[axis + 1:])
    return jnp.swapaxes(a, axis + 1, axis + 2).reshape(shp)


def _pad_rows(a, rows=8):
    return jnp.pad(a, ((0, rows - a.shape[0]), (0, 0)))


def kernel(x, p, mix_pre_g, mix_post_g, ffn_pre_g, ffn_post_g, pool_w, pool_scale, kv_norm_g, w_k, w_v, w_q, w_o, sinks, w_ff_gate, w_ff_up, w_ff_down, ple_norm_g, w_ple_gate, w_ple_proj, loss_target, m_mix_pre_g, m_mix_post_g, m_ffn_pre_g, m_ffn_post_g, m_pool_w, m_pool_scale, m_kv_norm_g, m_w_k, m_w_v, m_w_q, m_w_o, m_sinks, m_w_ff_gate, m_w_ff_up, m_w_ff_down, m_ple_norm_g, m_w_ple_gate, m_w_ple_proj, v_mix_pre_g, v_mix_post_g, v_ffn_pre_g, v_ffn_post_g, v_pool_w, v_pool_scale, v_kv_norm_g, v_w_k, v_w_v, v_w_q, v_w_o, v_sinks, v_w_ff_gate, v_w_ff_up, v_w_ff_down, v_ple_norm_g, v_w_ple_gate, v_w_ple_proj):
    depth = w_ff_gate.shape[0]
    n_a = pool_w.shape[0]
    t, d = x.shape[1], x.shape[2]
    h = x[0]
    tgt = loss_target[0]
    p_all = p.reshape(depth * t, p.shape[-1])
    my_block = _block_index(*_my_place())
    row = lambda g, i: g[i][None, :]
    bf = lambda a: a.astype(BF16)

    full, gathers = [None] * depth, {}
    start_tokens = jnp.zeros((), F32)
    for i in range(depth):
        shards = [bf(w_ff_gate[i].T), bf(w_ff_up[i].T), bf(w_ff_down[i]), bf(w_ple_gate[i]), bf(w_ple_proj[i].T)]
        if i == 0:
            pool0, scale_full = allgather_pieces([bf(pool_w[0].reshape(-1, POOL_GROUP)), _pad_rows(pool_scale)],
                                                 "allgather_pool0")
            order = pool0
        elif i < n_a:
            shards.append(bf(pool_w[i].reshape(-1, POOL_GROUP)))
        else:
            shards += [bf(_pair_heads(w_q[i - n_a], 1)), bf(w_o[i - n_a])]
            if i == n_a:
                shards.append(bf(jnp.concatenate([w_k, w_v], axis=1)))
        gathers[i] = exchange_start(shards, [_gather_zone(s) for s in shards], order, False, f"allgather_start_l{i}")
        order = gathers[i]["token"]
        start_tokens = start_tokens + order[0, 0]
    scale_full = scale_full.reshape(N_DEV, 8, -1)[:, :n_a].transpose(1, 0, 2).reshape(n_a, 1, d)

    cos, sin = _rope_tables(t, start_tokens)
    sink_b = [jnp.broadcast_to(_pair_heads(sinks[j][:, None], 0, 1), (N_HEADS, LANES)) for j in range(depth - n_a)]
    pool_full, wo_full = {}, {}

    saved = []
    kv = hk = None
    for i in range(depth):
        if i > 0:
            full[i] = exchange_wait(gathers[i], h, f"allgather_wait_l{i}")
        s = {"h0": h}
        if i < n_a:
            pool_full[i] = ((pool0 if i == 0 else full[i][5]).reshape(N_DEV, len(POOL_WINDOWS), -1, POOL_GROUP)
                            .transpose(1, 0, 2, 3).reshape(len(POOL_WINDOWS), POOL_GROUP, POOL_GROUP))
            gpre = row(mix_pre_g, i) + start_tokens if i == 0 else row(mix_pre_g, i)
            h1, a = pool_mix_fwd(h, gpre, pool_full[i], scale_full[i], row(mix_post_g, i), row(ffn_pre_g, i))
            if i == 0:
                full[0] = exchange_wait(gathers[0], (h1, cos, sin), "allgather_wait_l0")
        else:
            j = i - n_a
            wo_full[i] = _pair_heads(full[i][6], 0)
            if i == n_a:
                hk, kv = proj_rope_fwd(h, kv_norm_g[None, :], full[i][7], cos, sin, N_KV_HEADS * HEAD_DIM, "kv_proj_fwd")
            hn, q = proj_rope_fwd(h, row(mix_pre_g, i), full[i][5], cos, sin, d, "q_proj_fwd")
            attn = swa_fwd(q, kv, sink_b[j])
            m, h1, a = oproj_post_fwd(attn, wo_full[i], h, row(mix_post_g, i), row(ffn_pre_g, i))
            s.update(hn=hn, q=q, attn=attn, m=m)
        wg_t, wu_t, wd, wpg, wpp_t = full[i][:5]
        f, gte, up, hdn = ffn_fwd(a, wg_t, wu_t, wd)
        s.update(h1=h1, a=a, f=f, gte=gte, up=up, hdn=hdn)
        if i < depth - 1:
            h = post_ple_fwd(h1, f, p_all, i, row(ffn_post_g, i), row(ple_norm_g, i), wpg, wpp_t)
        saved.append(s)

    g_mix_pre, g_mix_post, g_ffn_pre, g_ffn_post, g_ple = ([None] * depth for _ in range(5))
    g_kv = g_sinks = None
    g_scale = [None] * n_a
    landing, scatters = [None] * depth, {}
    dkv_sum = []
    scatter_token = jnp.zeros((), F32)
    for i in reversed(range(depth)):
        s = saved[i]
        wg_t, wu_t, wd, wpg, wpp_t = full[i][:5]
        last = i == depth - 1
        dh2, df, ub, dzb, dppb, gacc = post_ple_bwd(tgt if last else dh, s["h1"], s["f"], p_all, i,
                                                    row(ffn_post_g, i) + scatter_token, row(ple_norm_g, i), wpg, wpp_t,
                                                    from_target=last)
        g_ple[i], g_ffn_post[i] = gacc[0], gacc[1]
        if last:
            loss_row = gacc[2][None, :]
        da, dgte, dup = ffn_bwd_act(df, s["gte"], s["up"], wg_t, wu_t, wd)
        grads = [xty(dgte, s["a"]), xty(dup, s["a"]), xty(s["hdn"], df), xty(ub, dzb), xty(dppb, p_all, i)]
        early = exchange_start(grads, [_scatter_zone(g) for g in grads], dh2, True, f"reduce_scatter_start_l{i}a")
        early_token = early["token"][0, 0]
        if i < n_a:
            dh, dpw, gacc = pool_mix_bwd(s["h0"], dh2, da, row(mix_pre_g, i) + early_token, pool_full[i], scale_full[i],
                                         row(mix_post_g, i), row(ffn_pre_g, i))
            g_mix_pre[i], g_mix_post[i], g_ffn_pre[i], g_scale[i] = gacc[0], gacc[1], gacc[2], gacc[3]
            dpw = dpw.reshape(len(POOL_WINDOWS), N_DEV, -1, POOL_GROUP).transpose(1, 0, 2, 3)
            grads = [bf(dpw.reshape(-1, POOL_GROUP))]
        else:
            j = i - n_a
            dh1, dmb, dattn, gacc = oproj_post_bwd(dh2, da, s["h1"], s["m"], wo_full[i], row(mix_post_g, i) + early_token,
                                                   row(ffn_pre_g, i))
            g_mix_post[i], g_ffn_pre[i] = gacc[0], gacc[1]
            dq, dkv, dsink = swa_bwd(s["q"], kv, dattn, sink_b[j])
            dkv_sum.append(dkv)
            g_sinks = [_unpair_heads(dsink[:, 0:1], 0, 1)[:, 0]] + (g_sinks or [])
            branches = [(row(mix_pre_g, i), full[i][5], d, [dq])]
            if i == n_a:
                branches.append((kv_norm_g[None, :], full[i][7], N_KV_HEADS * HEAD_DIM, dkv_sum))
            outs = proj_rope_bwd(dh1, s["h0"], cos, sin, branches, f"proj_bwd_l{i}")
            dh, gacc = outs[0], outs[-1]
            g_mix_pre[i] = gacc[0]
            grads = [xty(s["hn"], outs[1]), _unpair_heads(xty(s["attn"], dmb), 0)]
            if i == n_a:
                g_kv = gacc[1]
                grads.append(xty(hk, outs[2]))
        late = exchange_start(grads, [_scatter_zone(g) for g in grads], dh, True, f"reduce_scatter_start_l{i}b")
        scatter_token = late["token"][0, 0]
        scatters[i] = (early, late)
    grad_x = dh[None]
    after = dh
    for i in reversed(range(depth)):
        landing[i] = (exchange_wait(scatters[i][0], after, f"reduce_scatter_wait_l{i}a")
                      + exchange_wait(scatters[i][1], after, f"reduce_scatter_wait_l{i}b"))
        after = landing[i][0]

    sink_row = jnp.pad(jnp.concatenate(g_sinks)[None, :], ((0, 0), (0, d - sinks.size)))
    stack = lambda rows_: _pad_rows(jnp.stack(rows_))
    pack = jnp.concatenate([stack(g_mix_pre), stack(g_mix_post), stack(g_ffn_pre), stack(g_ffn_post), stack(g_ple),
                            _pad_rows(g_kv[None]), stack(g_scale), _pad_rows(sink_row), _pad_rows(loss_row)], axis=0)
    tot = allreduce_small(pack)
    sec = lambda k, n: tot[8 * k:8 * k + n]
    loss = jnp.sum(tot[64])
    small = {
        "mix_pre_g": sec(0, depth), "mix_post_g": sec(1, depth), "ffn_pre_g": sec(2, depth),
        "ffn_post_g": sec(3, depth), "ple_norm_g": sec(4, depth), "kv_norm_g": tot[40],
        "pool_scale": lax.dynamic_slice_in_dim(sec(6, n_a), my_block * pool_scale.shape[1], pool_scale.shape[1], axis=1),
        "sinks": tot[56, :sinks.size].reshape(sinks.shape),
    }

    weights = dict(mix_pre_g=mix_pre_g, mix_post_g=mix_post_g, ffn_pre_g=ffn_pre_g, ffn_post_g=ffn_post_g, pool_w=pool_w, pool_scale=pool_scale, kv_norm_g=kv_norm_g, w_k=w_k, w_v=w_v, w_q=w_q, w_o=w_o, sinks=sinks, w_ff_gate=w_ff_gate, w_ff_up=w_ff_up, w_ff_down=w_ff_down, ple_norm_g=ple_norm_g, w_ple_gate=w_ple_gate, w_ple_proj=w_ple_proj)
    mom1 = dict(mix_pre_g=m_mix_pre_g, mix_post_g=m_mix_post_g, ffn_pre_g=m_ffn_pre_g, ffn_post_g=m_ffn_post_g, pool_w=m_pool_w, pool_scale=m_pool_scale, kv_norm_g=m_kv_norm_g, w_k=m_w_k, w_v=m_w_v, w_q=m_w_q, w_o=m_w_o, sinks=m_sinks, w_ff_gate=m_w_ff_gate, w_ff_up=m_w_ff_up, w_ff_down=m_w_ff_down, ple_norm_g=m_ple_norm_g, w_ple_gate=m_w_ple_gate, w_ple_proj=m_w_ple_proj)
    mom2 = dict(mix_pre_g=v_mix_pre_g, mix_post_g=v_mix_post_g, ffn_pre_g=v_ffn_pre_g, ffn_post_g=v_ffn_post_g, pool_w=v_pool_w, pool_scale=v_pool_scale, kv_norm_g=v_kv_norm_g, w_k=v_w_k, w_v=v_w_v, w_q=v_w_q, w_o=v_w_o, sinks=v_sinks, w_ff_gate=v_w_ff_gate, w_ff_up=v_w_ff_up, w_ff_down=v_w_ff_down, ple_norm_g=v_ple_norm_g, w_ple_gate=v_w_ple_gate, w_ple_proj=v_w_ple_proj)

    swap = lambda a: jnp.swapaxes(a, 1, 2)
    same = lambda a: a
    att = range(n_a, depth)
    plan = {
        "w_ff_gate": (swap, swap, [landing[i][0] for i in range(depth)]),
        "w_ff_up": (swap, swap, [landing[i][1] for i in range(depth)]),
        "w_ff_down": (same, same, [landing[i][2] for i in range(depth)]),
        "w_ple_gate": (same, same, [landing[i][3] for i in range(depth)]),
        "w_ple_proj": (swap, swap, [landing[i][4] for i in range(depth)]),
        "pool_w": (lambda a: a.reshape(n_a, -1, POOL_GROUP), lambda a: a.reshape(pool_w.shape),
                   [landing[i][5] for i in range(n_a)]),
        "w_q": (lambda a: _pair_heads(a, 2), lambda a: _unpair_heads(a, 2), [landing[i][5] for i in att]),
        "w_o": (same, same, [landing[i][6] for i in att]),
    }
    for nme, g in small.items():
        w = weights[nme]
        plan[nme] = ((lambda a: a.reshape((1, -1, a.shape[-1]))), (lambda a, shp=w.shape: a.reshape(shp)),
                     [g.reshape((1, -1, w.shape[-1]))])
    results = {}
    for nme, (view, unview, layer_parts) in plan.items():
        outs = adamw_layers(view(weights[nme]), view(mom1[nme]), view(mom2[nme]), layer_parts)
        results[nme] = [unview(o) for o in outs]
    kv_cat = lambda ws: jnp.concatenate([ws["w_k"], ws["w_v"]], axis=1)[None]
    outs = adamw_layers(kv_cat(weights), kv_cat(mom1), kv_cat(mom2), [landing[n_a][7]])
    results["w_k"] = [o[0, :, :w_k.shape[1]] for o in outs]
    results["w_v"] = [o[0, :, w_k.shape[1]:] for o in outs]

    order = ["mix_pre_g", "mix_post_g", "ffn_pre_g", "ffn_post_g", "pool_w", "pool_scale", "kv_norm_g", "w_k", "w_v",
             "w_q", "w_o", "sinks", "w_ff_gate", "w_ff_up", "w_ff_down", "ple_norm_g", "w_ple_gate", "w_ple_proj"]
    g_out, d_out, m_out, v_out = ([results[nme][k] for nme in order] for k in range(4))
    return (loss, grad_x, *g_out, *d_out, *m_out, *v_out)
```

```python
import functools

import jax
import jax.numpy as jnp
from jax import lax
from jax.experimental import pallas as pl
from jax.experimental.pallas import tpu as pltpu

F32 = jnp.float32
BF16 = jnp.bfloat16

N_DEV = 8
HEAD_DIM = 64
N_HEADS = 16
N_KV_HEADS = 4
GQA = N_HEADS // N_KV_HEADS
BLOCK = 128
POOL_WINDOWS = (2, 4, 8, 16)
POOL_GROUP = 256
HALO = 16
ROPE_THETA = 10000.0
RMS_EPS = 1e-6
NEG_INF = -1e30
LANES = 128
ATTN_SUB = 8
XTY_ROWS = 2048
FFN_CHUNK = 768
VMEM_LIMIT = 56 * 1024 * 1024

ADAM_LR = 0.001
ADAM_B1 = 0.9
ADAM_B2 = 0.999
ADAM_EPS = 1e-08
ADAM_WD = 0.01
ADAM_STEP = 10

MESH = pl.DeviceIdType.MESH
ANY = pl.BlockSpec(memory_space=pl.ANY)

NT_DIMS = (((1,), (1,)), ((), ()))
TN_DIMS = (((0,), (0,)), ((), ()))


def _cparams(sem=None, vmem=None):
    kw = {}
    if sem is not None:
        kw["dimension_semantics"] = sem
    if vmem is not None:
        kw["vmem_limit_bytes"] = vmem
    return pltpu.CompilerParams(**kw)


def _rows(tm, n, first=0):
    return pl.BlockSpec((tm, n), lambda i: (i + first, 0))


def _rows_rev(tm, n, nt):
    return pl.BlockSpec((tm, n), lambda i: (nt - 1 - i, 0))


def _const(shape):
    nd = len(shape)
    return pl.BlockSpec(shape, lambda *_: (0,) * nd, pipeline_mode=pl.Buffered(1))


def _resident(shape):
    nd = len(shape)
    return pl.BlockSpec(shape, lambda *_: (0,) * nd)


def _tile_rows(t):
    return 512 if t % 512 == 0 else 128


def _dot(a, b):
    return jnp.dot(a, b, preferred_element_type=F32)


def _dot_nt(a, b):
    return lax.dot_general(a, b, NT_DIMS, preferred_element_type=F32)


def _dot_tn(a, b):
    return lax.dot_general(a, b, TN_DIMS, preferred_element_type=F32)


def _rms_r(x):
    return lax.rsqrt(jnp.mean(x * x, axis=-1, keepdims=True) + RMS_EPS)


def _rms_bwd(x, r, g, dy):
    gy = dy * g
    dx = r * gy - x * (r * r * r * jnp.mean(gy * x, axis=-1, keepdims=True))
    dg = jnp.sum(dy * (x * r), axis=0, keepdims=True)
    return dx, dg


def _sigmoid(x):
    return jax.nn.sigmoid(x)


def _rope_tables(t, zero_token):
    inv = 1.0 / (ROPE_THETA ** (jnp.arange(0, HEAD_DIM, 2, dtype=F32) / HEAD_DIM))
    ang = (jnp.arange(t, dtype=F32) + zero_token)[:, None] * jnp.tile(inv, 2 * LANES // HEAD_DIM)[None, :]
    sign = jnp.tile(jnp.repeat(jnp.array([-1.0, 1.0], F32), HEAD_DIM // 2), LANES // HEAD_DIM)
    return jnp.cos(ang), jnp.sin(ang) * sign[None, :]


def _swap_halves(x):
    n = x.shape[1]
    lane = lax.broadcasted_iota(jnp.int32, x.shape, 1)
    first = (lane % HEAD_DIM) < (HEAD_DIM // 2)
    return jnp.where(first, pltpu.roll(x, n - HEAD_DIM // 2, 1), pltpu.roll(x, HEAD_DIM // 2, 1))


def _rope(x, cos, sin):
    reps = x.shape[1] // LANES
    return x * jnp.tile(cos, (1, reps)) + _swap_halves(x) * jnp.tile(sin, (1, reps))


def _unrope(dy, cos, sin):
    reps = dy.shape[1] // LANES
    return dy * jnp.tile(cos, (1, reps)) + _swap_halves(dy * jnp.tile(sin, (1, reps)))


def _acc_init(acc_ref):
    @pl.when(pl.program_id(0) == 0)
    def _():
        acc_ref[...] = jnp.zeros_like(acc_ref)


def _window_sums(ext, tm, forward):
    n = tm + HALO
    out = []
    for g, w in enumerate(POOL_WINDOWS):
        s = ext[:, g * POOL_GROUP:(g + 1) * POOL_GROUP]
        k = 1
        while k < w:
            s = s + pltpu.roll(s, k if forward else n - k, 0)
            k *= 2
        out.append(s[HALO:, :] if forward else s[:tm, :])
    return out


def _pool_inv_counts(tile, tm):
    t = tile * tm + lax.broadcasted_iota(jnp.int32, (tm, 1), 0)
    return [1.0 / jnp.minimum(t + 1, w).astype(F32) for w in POOL_WINDOWS]


def _pool_mix(hn, ext, inv_cnts, pw_ref, scale, tm):
    sums = _window_sums(ext, tm, True)
    pooled, ys = [], []
    for g in range(len(POOL_WINDOWS)):
        pg = (sums[g] * inv_cnts[g] - hn[:, g * POOL_GROUP:(g + 1) * POOL_GROUP]).astype(BF16)
        pooled.append(pg)
        ys.append(_dot(pg, pw_ref[g]))
    y = jnp.concatenate(ys, axis=1)
    return pooled, y, y * scale


def pool_mix_fwd(h0, gpre, pool_w, scale, gpost, gffn):
    t, d = h0.shape
    tm = _tile_rows(t)

    def body(h_ref, gpre_ref, pw_ref, scale_ref, gpost_ref, gffn_ref, h1_ref, a_ref, carry):
        i = pl.program_id(0)

        @pl.when(i == 0)
        def _():
            carry[...] = jnp.zeros_like(carry)

        x = h_ref[...]
        hn = x * _rms_r(x) * gpre_ref[...]
        ext = jnp.concatenate([carry[...], hn], axis=0)
        carry[...] = hn[tm - HALO:, :]
        _, _, m = _pool_mix(hn, ext, _pool_inv_counts(i, tm), pw_ref, scale_ref[...], tm)
        h1 = x + m * _rms_r(m) * gpost_ref[...]
        h1_ref[...] = h1
        a_ref[...] = (h1 * _rms_r(h1) * gffn_ref[...]).astype(BF16)

    return pl.pallas_call(
        functools.partial(body), name="pool_mix_fwd", grid=(t // tm,),
        in_specs=[_rows(tm, d), _const((1, d)), _const(pool_w.shape), _const((1, d)), _const((1, d)), _const((1, d))],
        out_specs=[_rows(tm, d), _rows(tm, d)],
        out_shape=[jax.ShapeDtypeStruct((t, d), F32), jax.ShapeDtypeStruct((t, d), BF16)],
        scratch_shapes=[pltpu.VMEM((HALO, d), F32)],
        compiler_params=_cparams(("arbitrary",), VMEM_LIMIT),
    )(h0, gpre, pool_w, scale, gpost, gffn)


def pool_mix_bwd(h0, dh2, da, gpre, pool_w, scale, gpost, gffn):
    t, d = h0.shape
    tm = _tile_rows(t)
    nt = t // tm
    hb = tm // HALO

    def body(h_ref, halo_ref, dh2_ref, da_ref, gpre_ref, pw_ref, scale_ref, gpost_ref, gffn_ref,
             dh0_ref, dpw_ref, gacc_ref, carry):
        i = pl.program_id(0)
        tile = nt - 1 - i
        _acc_init(gacc_ref)
        _acc_init(dpw_ref)

        @pl.when(i == 0)
        def _():
            carry[...] = jnp.zeros_like(carry)

        x = h_ref[...]
        gpre_v, scale_v, gpost_v, gffn_v = gpre_ref[...], scale_ref[...], gpost_ref[...], gffn_ref[...]
        r0 = _rms_r(x)
        hn = x * r0 * gpre_v
        xh = halo_ref[...]
        hn_halo = jnp.where(tile > 0, xh * _rms_r(xh) * gpre_v, 0.0)
        ext = jnp.concatenate([hn_halo, hn], axis=0)
        inv_cnts = _pool_inv_counts(tile, tm)
        pooled, y, m = _pool_mix(hn, ext, inv_cnts, pw_ref, scale_v, tm)
        rm = _rms_r(m)
        h1 = x + m * rm * gpost_v
        dh1_n, dgffn = _rms_bwd(h1, _rms_r(h1), gffn_v, da_ref[...].astype(F32))
        dh1 = dh2_ref[...] + dh1_n
        dm, dgpost = _rms_bwd(m, rm, gpost_v, dh1)
        dscale = jnp.sum(dm * y, axis=0, keepdims=True)
        dy = (dm * scale_v).astype(BF16)
        dpn = []
        for g in range(len(POOL_WINDOWS)):
            dyg = dy[:, g * POOL_GROUP:(g + 1) * POOL_GROUP]
            dpw_ref[g] += _dot_tn(pooled[g], dyg)
            dpn.append(_dot_nt(dyg, pw_ref[g]))
        dpooled = jnp.concatenate(dpn, axis=1)
        dpc = jnp.concatenate([dpn[g] * inv_cnts[g] for g in range(len(POOL_WINDOWS))], axis=1)
        ext2 = jnp.concatenate([dpc, carry[...]], axis=0)
        carry[...] = dpc[:HALO, :]
        dhn = jnp.concatenate(_window_sums(ext2, tm, False), axis=1) - dpooled
        dh0_n, dgpre = _rms_bwd(x, r0, gpre_v, dhn)
        dh0_ref[...] = dh1 + dh0_n
        gacc_ref[0:1, :] += dgpre
        gacc_ref[1:2, :] += dgpost
        gacc_ref[2:3, :] += dgffn
        gacc_ref[3:4, :] += dscale

    return pl.pallas_call(
        functools.partial(body), name="pool_mix_bwd", grid=(nt,),
        in_specs=[_rows_rev(tm, d, nt),
                  pl.BlockSpec((HALO, d), lambda i: (jnp.maximum((nt - 1 - i) * hb - 1, 0), 0)),
                  _rows_rev(tm, d, nt), _rows_rev(tm, d, nt),
                  _const((1, d)), _const(pool_w.shape), _const((1, d)), _const((1, d)), _const((1, d))],
        out_specs=[_rows_rev(tm, d, nt), _resident(pool_w.shape), _resident((8, d))],
        out_shape=[jax.ShapeDtypeStruct((t, d), F32), jax.ShapeDtypeStruct(pool_w.shape, F32),
                   jax.ShapeDtypeStruct((8, d), F32)],
        scratch_shapes=[pltpu.VMEM((HALO, d), F32)],
        compiler_params=_cparams(("arbitrary",), VMEM_LIMIT),
    )(h0, h0, dh2, da, gpre, pool_w, scale, gpost, gffn)


def _ffn_chunks(f):
    return [(c, min(c + FFN_CHUNK, f)) for c in range(0, f, FFN_CHUNK)]


def ffn_fwd(a, wg_t, wu_t, wd):
    t, d = a.shape
    f = wd.shape[0]
    tm = _tile_rows(t)

    def body(a_ref, wg_ref, wu_ref, wd_ref, f_ref, gte_ref, up_ref, hdn_ref):
        av = a_ref[...]
        acc = jnp.zeros((tm, d), F32)
        for c0, c1 in _ffn_chunks(f):
            gte = _dot_nt(av, wg_ref[c0:c1, :])
            up = _dot_nt(av, wu_ref[c0:c1, :])
            gte_ref[:, c0:c1] = gte.astype(BF16)
            up_ref[:, c0:c1] = up.astype(BF16)
            hdn = (gte * _sigmoid(gte) * up).astype(BF16)
            hdn_ref[:, c0:c1] = hdn
            acc = acc + _dot(hdn, wd_ref[c0:c1, :])
        f_ref[...] = acc.astype(BF16)

    return pl.pallas_call(
        functools.partial(body), name="ffn_fwd", grid=(t // tm,),
        in_specs=[_rows(tm, d), _const((f, d)), _const((f, d)), _const((f, d))],
        out_specs=[_rows(tm, d), _rows(tm, f), _rows(tm, f), _rows(tm, f)],
        out_shape=[jax.ShapeDtypeStruct((t, d), BF16)] + [jax.ShapeDtypeStruct((t, f), BF16)] * 3,
        compiler_params=_cparams(("parallel",), VMEM_LIMIT),
    )(a, wg_t, wu_t, wd)


def ffn_bwd_act(df, gte, up, wg_t, wu_t, wd):
    t, d = df.shape
    f = wd.shape[0]
    tm = _tile_rows(t)

    def body(df_ref, gte_ref, up_ref, wg_ref, wu_ref, wd_ref, da_ref, dgte_ref, dup_ref):
        dfv = df_ref[...]
        chunks = _ffn_chunks(f)
        half = chunks[len(chunks) // 2][0]
        acc = None
        for c0, c1 in chunks:
            g = gte_ref[:, c0:c1].astype(F32)
            u = up_ref[:, c0:c1].astype(F32)
            sg = _sigmoid(g)
            sl = g * sg
            dh = _dot_nt(dfv, wd_ref[c0:c1, :])
            dup_ref[:, c0:c1] = (dh * sl).astype(BF16)
            dgte_ref[:, c0:c1] = (dh * u * (sg * (1.0 + g * (1.0 - sg)))).astype(BF16)
            if c1 == half:
                acc = _dot(dgte_ref[:, :half], wg_ref[:half, :]) + _dot(dup_ref[:, :half], wu_ref[:half, :])
        da = acc + _dot(dgte_ref[:, half:], wg_ref[half:, :]) + _dot(dup_ref[:, half:], wu_ref[half:, :])
        da_ref[...] = da.astype(BF16)

    return pl.pallas_call(
        functools.partial(body), name="ffn_bwd_act", grid=(t // tm,),
        in_specs=[_rows(tm, d), _rows(tm, f), _rows(tm, f), _const((f, d)), _const((f, d)), _const((f, d))],
        out_specs=[_rows(tm, d), _rows(tm, f), _rows(tm, f)],
        out_shape=[jax.ShapeDtypeStruct((t, d), BF16)] + [jax.ShapeDtypeStruct((t, f), BF16)] * 2,
        compiler_params=_cparams(("parallel",), VMEM_LIMIT),
    )(df, gte, up, wg_t, wu_t, wd)


def xty(x, y, y_part=0):
    t, nx = x.shape
    ny = y.shape[1]
    tk = XTY_ROWS if t % XTY_ROWS == 0 else _tile_rows(t)
    bn = nx // 2 if nx > 1024 else nx
    nk = t // tk

    def body(x_ref, y_ref, o_ref, acc):
        k = pl.program_id(1)

        @pl.when(k == 0)
        def _():
            acc[...] = jnp.zeros_like(acc)

        acc[...] += _dot_tn(x_ref[...].astype(BF16), y_ref[...].astype(BF16))

        @pl.when(k == nk - 1)
        def _():
            o_ref[...] = acc[...].astype(BF16)

    return pl.pallas_call(
        functools.partial(body), name="xty", grid=(nx // bn, nk),
        in_specs=[pl.BlockSpec((tk, bn), lambda j, k: (k, j)),
                  pl.BlockSpec((tk, ny), lambda j, k: (k + y_part * nk, 0))],
        out_specs=pl.BlockSpec((bn, ny), lambda j, k: (j, 0)),
        out_shape=jax.ShapeDtypeStruct((nx, ny), BF16),
        scratch_shapes=[pltpu.VMEM((bn, ny), F32)],
        compiler_params=_cparams(("parallel", "arbitrary"), VMEM_LIMIT),
    )(x, y)


def _ple_fwd_tile(h1, f, p, gpost, gple, wpg_ref, wpp_ref):
    rf = _rms_r(f)
    h2 = h1 + f * rf * gpost
    r2 = _rms_r(h2)
    ub = (h2 * r2 * gple).astype(BF16)
    gate = _sigmoid(_dot(ub, wpg_ref[...]))
    pp = _dot_nt(p.astype(BF16), wpp_ref[...])
    return rf, h2, r2, ub, gate, pp


def post_ple_fwd(h1, f, p, layer, gpost, gple, wpg, wpp_t):
    t, d = h1.shape
    pd = p.shape[1]
    tm = _tile_rows(t)

    def body(h1_ref, f_ref, p_ref, gpost_ref, gple_ref, wpg_ref, wpp_ref, out_ref):
        _, h2, _, _, gate, pp = _ple_fwd_tile(h1_ref[...], f_ref[...].astype(F32), p_ref[...], gpost_ref[...],
                                              gple_ref[...], wpg_ref, wpp_ref)
        out_ref[...] = h2 + pp * gate

    return pl.pallas_call(
        functools.partial(body), name="post_ple_fwd", grid=(t // tm,),
        in_specs=[_rows(tm, d), _rows(tm, d), _rows(tm, pd, layer * (t // tm)), _const((1, d)), _const((1, d)),
                  _const(wpg.shape), _const(wpp_t.shape)],
        out_specs=_rows(tm, d), out_shape=jax.ShapeDtypeStruct((t, d), F32),
        compiler_params=_cparams(("parallel",), VMEM_LIMIT),
    )(h1, f, p, gpost, gple, wpg, wpp_t)


def post_ple_bwd(dh3, h1, f, p, layer, gpost, gple, wpg, wpp_t, from_target=False):
    t, d = h1.shape
    pd = p.shape[1]
    tm = _tile_rows(t)

    def body(dh3_ref, h1_ref, f_ref, p_ref, gpost_ref, gple_ref, wpg_ref, wpp_ref,
             dh2_ref, df_ref, u_ref, dz_ref, dpp_ref, gacc_ref):
        _acc_init(gacc_ref)
        gpost_v, gple_v = gpost_ref[...], gple_ref[...]
        nsub = 2 if tm % 16 == 0 else 1
        for sb in range(nsub):
            rows = slice(sb * (tm // nsub), (sb + 1) * (tm // nsub))
            fv = f_ref[rows, :].astype(F32)
            rf, h2, r2, ub, gate, pp = _ple_fwd_tile(h1_ref[rows, :], fv, p_ref[rows, :], gpost_v, gple_v, wpg_ref,
                                                     wpp_ref)
            if from_target:
                err = h2 + pp * gate - dh3_ref[rows, :]
                dh3v = err * (1.0 / d)
                gacc_ref[2:3, :] += jnp.sum(err * err, axis=0, keepdims=True) * (0.5 / d)
            else:
                dh3v = dh3_ref[rows, :]
            dpp_ref[rows, :] = (dh3v * gate).astype(BF16)
            dz = (dh3v * pp * gate * (1.0 - gate)).astype(BF16)
            dz_ref[rows, :] = dz
            u_ref[rows, :] = ub
            du = _dot_nt(dz, wpg_ref[...])
            dh2_n, dgple = _rms_bwd(h2, r2, gple_v, du)
            dh2 = dh3v + dh2_n
            df, dgpost = _rms_bwd(fv, rf, gpost_v, dh2)
            dh2_ref[rows, :] = dh2
            df_ref[rows, :] = df.astype(BF16)
            gacc_ref[0:1, :] += dgple
            gacc_ref[1:2, :] += dgpost

    return pl.pallas_call(
        functools.partial(body), name="post_ple_loss_bwd" if from_target else "post_ple_bwd", grid=(t // tm,),
        in_specs=[_rows(tm, d), _rows(tm, d), _rows(tm, d), _rows(tm, pd, layer * (t // tm)), _const((1, d)),
                  _const((1, d)), _const(wpg.shape), _const(wpp_t.shape)],
        out_specs=[_rows(tm, d)] * 5 + [_resident((8, d))],
        out_shape=[jax.ShapeDtypeStruct((t, d), F32)] + [jax.ShapeDtypeStruct((t, d), BF16)] * 4
        + [jax.ShapeDtypeStruct((8, d), F32)],
        compiler_params=_cparams(("arbitrary",), VMEM_LIMIT),
    )(dh3, h1, f, p, gpost, gple, wpg, wpp_t)


def proj_rope_fwd(h, gain, w, cos, sin, n_rope, name):
    t, d = h.shape
    n = w.shape[1]
    tm = _tile_rows(t)

    def body(h_ref, g_ref, w_ref, cos_ref, sin_ref, hn_ref, y_ref):
        x = h_ref[...]
        hn = (x * _rms_r(x) * g_ref[...]).astype(BF16)
        hn_ref[...] = hn
        y = _dot(hn, w_ref[...])
        y_ref[:, :n_rope] = _rope(y[:, :n_rope], cos_ref[...], sin_ref[...]).astype(BF16)
        if n_rope < n:
            y_ref[:, n_rope:] = y[:, n_rope:].astype(BF16)

    return pl.pallas_call(
        functools.partial(body), name=name, grid=(t // tm,),
        in_specs=[_rows(tm, d), _const((1, d)), _const(w.shape), _rows(tm, LANES), _rows(tm, LANES)],
        out_specs=[_rows(tm, d), _rows(tm, n)],
        out_shape=[jax.ShapeDtypeStruct((t, d), BF16), jax.ShapeDtypeStruct((t, n), BF16)],
        compiler_params=_cparams(("parallel",), VMEM_LIMIT),
    )(h, gain, w, cos, sin)


def proj_rope_bwd(dh1, h0, cos, sin, branches, name):
    t, d = h0.shape
    tm = _tile_rows(t)
    nb = len(branches)
    n_cot = [len(b[3]) for b in branches]

    def body(*refs):
        dh1_ref, h0_ref, cos_ref, sin_ref = refs[:4]
        pos = 4
        br_refs = []
        for b in range(nb):
            br_refs.append((refs[pos], refs[pos + 1], refs[pos + 2:pos + 2 + n_cot[b]]))
            pos += 2 + n_cot[b]
        dh0_ref = refs[pos]
        dpre_refs = refs[pos + 1:pos + 1 + nb]
        gacc_ref = refs[pos + 1 + nb]
        _acc_init(gacc_ref)
        x = h0_ref[...]
        r0 = _rms_r(x)
        dh = dh1_ref[...]
        for b in range(nb):
            g_ref, w_ref, cot_refs = br_refs[b]
            n_rope = branches[b][2]
            dy = cot_refs[0][...].astype(F32)
            for c_ref in cot_refs[1:]:
                dy = dy + c_ref[...].astype(F32)
            n = dy.shape[1]
            dpre_refs[b][:, :n_rope] = _unrope(dy[:, :n_rope], cos_ref[...], sin_ref[...]).astype(BF16)
            if n_rope < n:
                dpre_refs[b][:, n_rope:] = dy[:, n_rope:].astype(BF16)
            dhn = _dot_nt(dpre_refs[b][...], w_ref[...])
            dx, dg = _rms_bwd(x, r0, g_ref[...], dhn)
            dh = dh + dx
            gacc_ref[b:b + 1, :] += dg
        dh0_ref[...] = dh

    in_specs = [_rows(tm, d), _rows(tm, d), _rows(tm, LANES), _rows(tm, LANES)]
    args = [dh1, h0, cos, sin]
    out_specs = [_rows(tm, d)]
    out_shape = [jax.ShapeDtypeStruct((t, d), F32)]
    for gain, w, _, cots in branches:
        n = w.shape[1]
        in_specs += [_const((1, d)), _const(w.shape)] + [_rows(tm, n)] * len(cots)
        args += [gain, w] + list(cots)
        out_specs.append(_rows(tm, n))
        out_shape.append(jax.ShapeDtypeStruct((t, n), BF16))
    out_specs.append(_resident((8, d)))
    out_shape.append(jax.ShapeDtypeStruct((8, d), F32))
    return pl.pallas_call(
        functools.partial(body), name=name, grid=(t // tm,),
        in_specs=in_specs, out_specs=out_specs, out_shape=out_shape,
        compiler_params=_cparams(("arbitrary",), VMEM_LIMIT),
    )(*args)


def _tri():
    row = lax.broadcasted_iota(jnp.int32, (BLOCK, BLOCK), 0)
    col = lax.broadcasted_iota(jnp.int32, (BLOCK, BLOCK), 1)
    return col <= row


def _block_diag(x):
    lo = lax.broadcasted_iota(jnp.int32, x.shape, 1) < HEAD_DIM
    zero = jnp.zeros_like(x)
    return jnp.concatenate([jnp.where(lo, x, zero), jnp.where(lo, zero, x)], axis=0)


def _dense(x, tri):
    return (jnp.where(tri, x[:, BLOCK:2 * BLOCK], x[:, :BLOCK]),
            jnp.where(tri, x[:, 3 * BLOCK:], x[:, 2 * BLOCK:3 * BLOCK]))


def _banded(xa, xb, tri):
    zero = jnp.zeros_like(xa)
    return jnp.concatenate([jnp.where(tri, zero, xa), jnp.where(tri, xa, zero),
                            jnp.where(tri, zero, xb), jnp.where(tri, xb, zero)], axis=1).astype(BF16)


def _softmax_sink(s, sink):
    mx = jnp.maximum(jnp.max(s, axis=1, keepdims=True), sink)
    e = jnp.exp(s - mx)
    es = jnp.exp(sink - mx)
    inv = 1.0 / (jnp.sum(e, axis=1, keepdims=True) + es)
    return e * inv, es * inv


def _sink_column(sink_ref):
    return jnp.concatenate([jnp.broadcast_to(sink_ref[h:h + 1, 0:1], (BLOCK, 1)) for h in range(N_HEADS)], axis=0)


def _kv_block_diag(band, kvw):
    n_lt = kvw // LANES
    return ([_block_diag(band[:, lt * LANES:(lt + 1) * LANES]) for lt in range(n_lt)],
            [_block_diag(band[:, kvw + lt * LANES:kvw + (lt + 1) * LANES]) for lt in range(n_lt)])


def _all_probs(q_ref, r0, kbd, tri, n, sink_ref):
    dense = []
    for tq in range(N_HEADS // 2):
        s = _dot_nt(q_ref[r0:r0 + BLOCK, tq * LANES:(tq + 1) * LANES], kbd[tq // GQA])
        dense += list(_dense(s, tri))
    bias = jnp.where(jnp.logical_not(tri) & (n == 0), NEG_INF, 0.0)
    s_all = jnp.concatenate(dense, axis=0) * (HEAD_DIM ** -0.5) + jnp.concatenate([bias] * N_HEADS, axis=0)
    return _softmax_sink(s_all, _sink_column(sink_ref))


def _head_rows(x, tq):
    return x[2 * tq * BLOCK:(2 * tq + 1) * BLOCK], x[(2 * tq + 1) * BLOCK:(2 * tq + 2) * BLOCK]


def _attn_sub(t):
    return ATTN_SUB if t % (ATTN_SUB * BLOCK) == 0 else 1


def swa_fwd(q, kv, sink_b):
    t, d = q.shape
    sub = _attn_sub(t)
    kvw = N_KV_HEADS * HEAD_DIM

    def body(q_ref, kvc_ref, kvp_ref, sink_ref, o_ref):
        i = pl.program_id(0)
        tri = _tri()
        ext = jnp.concatenate([kvp_ref[...], kvc_ref[...]], axis=0)
        for sb in range(sub):
            r0 = sb * BLOCK
            kbd, vbd = _kv_block_diag(ext[r0:r0 + 2 * BLOCK], kvw)
            p, _ = _all_probs(q_ref, r0, kbd, tri, i * sub + sb, sink_ref)
            for tq in range(N_HEADS // 2):
                pa, pb = _head_rows(p, tq)
                o_ref[r0:r0 + BLOCK, tq * LANES:(tq + 1) * LANES] = _dot(_banded(pa, pb, tri), vbd[tq // GQA]).astype(BF16)

    return pl.pallas_call(
        functools.partial(body), name="swa_fwd", grid=(t // (sub * BLOCK),),
        in_specs=[_rows(sub * BLOCK, d), _rows(sub * BLOCK, 2 * kvw),
                  pl.BlockSpec((BLOCK, 2 * kvw), lambda i: (jnp.maximum(i * sub - 1, 0), 0)), _const(sink_b.shape)],
        out_specs=_rows(sub * BLOCK, d),
        out_shape=jax.ShapeDtypeStruct((t, d), BF16),
        compiler_params=_cparams(("parallel",), VMEM_LIMIT),
    )(q, kv, kv, sink_b)


def swa_bwd(q, kv, do, sink_b):
    t, d = q.shape
    sub = _attn_sub(t)
    nq = t // (sub * BLOCK)
    kvw = N_KV_HEADS * HEAD_DIM

    def body(q_ref, do_ref, kvc_ref, kvp_ref, sink_ref, dq_ref, dkv_ref, dsink_ref, carry):
        i = pl.program_id(0)
        step = nq - 1 - i
        _acc_init(dsink_ref)

        @pl.when(i == 0)
        def _():
            carry[...] = jnp.zeros_like(carry)

        tri = _tri()
        lo = lax.broadcasted_iota(jnp.int32, (2 * BLOCK, LANES), 1) < HEAD_DIM
        ext = jnp.concatenate([kvp_ref[...], kvc_ref[...]], axis=0)
        dkeys = [None] * (sub + 1)
        for sb in reversed(range(sub)):
            r0 = sb * BLOCK
            kbd, vbd = _kv_block_diag(ext[r0:r0 + 2 * BLOCK], kvw)
            p, ps = _all_probs(q_ref, r0, kbd, tri, step * sub + sb, sink_ref)
            dp = []
            for tq in range(N_HEADS // 2):
                dp += list(_dense(_dot_nt(do_ref[r0:r0 + BLOCK, tq * LANES:(tq + 1) * LANES], vbd[tq // GQA]), tri))
            dp = jnp.concatenate(dp, axis=0)
            delta = jnp.sum(p * dp, axis=1, keepdims=True)
            ds = p * (dp - delta) * (HEAD_DIM ** -0.5)
            dsk = ps * delta
            for h in range(N_HEADS):
                dsink_ref[h:h + 1, :] -= jnp.sum(dsk[h * BLOCK:(h + 1) * BLOCK], axis=0, keepdims=True)
            dkb = [jnp.zeros((4 * BLOCK, LANES), F32) for _ in kbd]
            dvb = [jnp.zeros((4 * BLOCK, LANES), F32) for _ in kbd]
            for tq in range(N_HEADS // 2):
                lt = tq // GQA
                cols = slice(tq * LANES, (tq + 1) * LANES)
                dsb = _banded(*_head_rows(ds, tq), tri)
                dq_ref[r0:r0 + BLOCK, cols] = _dot(dsb, kbd[lt]).astype(BF16)
                dkb[lt] = dkb[lt] + _dot_tn(dsb, q_ref[r0:r0 + BLOCK, cols])
                dvb[lt] = dvb[lt] + _dot_tn(_banded(*_head_rows(p, tq), tri), do_ref[r0:r0 + BLOCK, cols])
            dall = jnp.concatenate([jnp.where(lo, x[:2 * BLOCK], x[2 * BLOCK:]) for x in dkb + dvb], axis=1)
            dkeys[sb + 1] = dall[BLOCK:] if dkeys[sb + 1] is None else dkeys[sb + 1] + dall[BLOCK:]
            dkeys[sb] = dall[:BLOCK]
        for sb in range(sub):
            own = dkeys[sb + 1] + carry[...] if sb == sub - 1 else dkeys[sb + 1]
            dkv_ref[sb * BLOCK:(sb + 1) * BLOCK, :] = own
        carry[...] = dkeys[0]

    rev = lambda i: (nq - 1 - i, 0)
    return pl.pallas_call(
        functools.partial(body), name="swa_bwd", grid=(nq,),
        in_specs=[pl.BlockSpec((sub * BLOCK, d), rev), pl.BlockSpec((sub * BLOCK, d), rev),
                  pl.BlockSpec((sub * BLOCK, 2 * kvw), rev),
                  pl.BlockSpec((BLOCK, 2 * kvw), lambda i: (jnp.maximum((nq - 1 - i) * sub - 1, 0), 0)),
                  _const(sink_b.shape)],
        out_specs=[pl.BlockSpec((sub * BLOCK, d), rev), pl.BlockSpec((sub * BLOCK, 2 * kvw), rev),
                   _resident(sink_b.shape)],
        out_shape=[jax.ShapeDtypeStruct((t, d), BF16), jax.ShapeDtypeStruct((t, 2 * kvw), F32),
                   jax.ShapeDtypeStruct(sink_b.shape, F32)],
        scratch_shapes=[pltpu.VMEM((BLOCK, 2 * kvw), F32)],
        compiler_params=_cparams(("arbitrary",), VMEM_LIMIT),
    )(q, do, kv, kv, sink_b)


def oproj_post_fwd(attn, w_o, h0, gpost, gffn):
    t, d = h0.shape
    tm = _tile_rows(t)

    def body(at_ref, w_ref, h0_ref, gpost_ref, gffn_ref, m_ref, h1_ref, a_ref):
        m = _dot(at_ref[...], w_ref[...])
        m_ref[...] = m.astype(BF16)
        h1 = h0_ref[...] + m * _rms_r(m) * gpost_ref[...]
        h1_ref[...] = h1
        a_ref[...] = (h1 * _rms_r(h1) * gffn_ref[...]).astype(BF16)

    return pl.pallas_call(
        functools.partial(body), name="oproj_post_fwd", grid=(t // tm,),
        in_specs=[_rows(tm, d), _const(w_o.shape), _rows(tm, d), _const((1, d)), _const((1, d))],
        out_specs=[_rows(tm, d)] * 3,
        out_shape=[jax.ShapeDtypeStruct((t, d), BF16), jax.ShapeDtypeStruct((t, d), F32),
                   jax.ShapeDtypeStruct((t, d), BF16)],
        compiler_params=_cparams(("parallel",), VMEM_LIMIT),
    )(attn, w_o, h0, gpost, gffn)


def oproj_post_bwd(dh2, da, h1, m, w_o, gpost, gffn):
    t, d = h1.shape
    tm = _tile_rows(t)

    def body(dh2_ref, da_ref, h1_ref, m_ref, w_ref, gpost_ref, gffn_ref, dh1_ref, dm_ref, dat_ref, gacc_ref):
        _acc_init(gacc_ref)
        h1v, mv = h1_ref[...], m_ref[...].astype(F32)
        dh1_n, dgffn = _rms_bwd(h1v, _rms_r(h1v), gffn_ref[...], da_ref[...].astype(F32))
        dh1 = dh2_ref[...] + dh1_n
        dm, dgpost = _rms_bwd(mv, _rms_r(mv), gpost_ref[...], dh1)
        dmb = dm.astype(BF16)
        dh1_ref[...] = dh1
        dm_ref[...] = dmb
        dat_ref[...] = _dot_nt(dmb, w_ref[...]).astype(BF16)
        gacc_ref[0:1, :] += dgpost
        gacc_ref[1:2, :] += dgffn

    return pl.pallas_call(
        functools.partial(body), name="oproj_post_bwd", grid=(t // tm,),
        in_specs=[_rows(tm, d)] * 4 + [_const(w_o.shape), _const((1, d)), _const((1, d))],
        out_specs=[_rows(tm, d)] * 3 + [_resident((8, d))],
        out_shape=[jax.ShapeDtypeStruct((t, d), F32), jax.ShapeDtypeStruct((t, d), BF16),
                   jax.ShapeDtypeStruct((t, d), BF16), jax.ShapeDtypeStruct((8, d), F32)],
        compiler_params=_cparams(("arbitrary",), VMEM_LIMIT),
    )(dh2, da, h1, m, w_o, gpost, gffn)


def _my_place():
    return lax.axis_index("x"), lax.axis_index("y"), lax.axis_index("c")


def _block_index(px, py, pc):
    return 4 * px + 2 * py + pc


def allgather_pieces(shards, name):
    np_ = len(shards)

    def body(*refs):
        in_refs, out_refs = refs[:np_], refs[np_:2 * np_]
        send_sems, recv_sems, local_sems = refs[2 * np_:]
        x, y, c = _my_place()
        me, sibling = (x, y, c), (x, y, 1 - c)
        chips = [(1 - x, y), (x, 1 - y), (1 - x, 1 - y)]

        def rows(p, place):
            r = in_refs[p].shape[0]
            return out_refs[p].at[pl.ds(_block_index(*place) * r, r), :]

        def copy(p, k, block, to, src=None):
            return pltpu.make_async_remote_copy(
                src_ref=rows(p, block) if src is None else src, dst_ref=rows(p, block),
                send_sem=send_sems.at[p, k], recv_sem=recv_sems.at[p, k], device_id=to, device_id_type=MESH)

        mine = [pltpu.make_async_copy(in_refs[p], rows(p, me), local_sems.at[p]) for p in range(np_)]
        first, passed = [], []
        for p in range(np_):
            mine[p].start()
            first.append(copy(p, 0, me, sibling, src=in_refs[p]))
            first += [copy(p, 1 + j, me, (*chip, c), src=in_refs[p]) for j, chip in enumerate(chips)]
        for cp in first:
            cp.start()
        for p in range(np_):
            for j, chip in enumerate(chips):
                copy(p, 1 + j, (*chip, c), me).wait_recv()
                fwd = copy(p, 4 + j, (*chip, c), sibling)
                fwd.start()
                passed.append(fwd)
        for p in range(np_):
            copy(p, 0, sibling, me).wait_recv()
            for j, chip in enumerate(chips):
                copy(p, 4 + j, (*chip, 1 - c), me).wait_recv()
        for cp in first + passed:
            cp.wait_send()
        for cp in mine:
            cp.wait()

    return pl.pallas_call(
        functools.partial(body), name=name,
        in_specs=[ANY] * np_, out_specs=[ANY] * np_,
        out_shape=[jax.ShapeDtypeStruct((N_DEV * s.shape[0], s.shape[1]), s.dtype) for s in shards],
        scratch_shapes=[pltpu.SemaphoreType.DMA((np_, 7)), pltpu.SemaphoreType.DMA((np_, 7)),
                        pltpu.SemaphoreType.DMA((np_,))],
    )(*shards)


def _peers():
    x, y, c = _my_place()
    flips = [(fx, fy, fc) for fx in (0, 1) for fy in (0, 1) for fc in (0, 1)][1:]
    return [(1 - x if fx else x, 1 - y if fy else y, 1 - c if fc else c) for fx, fy, fc in flips]


HBM = pl.BlockSpec(memory_space=pltpu.HBM)
SEM = pl.BlockSpec(memory_space=pltpu.SEMAPHORE)


def _exchange_windows(scatter, src_ref, land_ref, my_block, peer_block):
    if scatter:
        r = land_ref.shape[1]
        return src_ref.at[pl.ds(peer_block * r, r), :], land_ref.at[my_block], land_ref.at[peer_block]
    r = src_ref.shape[0]
    return src_ref, land_ref.at[pl.ds(my_block * r, r), :], land_ref.at[pl.ds(peer_block * r, r), :]


def _own_copy(scatter, src_ref, land_ref, my_block, sem):
    if scatter:
        r = land_ref.shape[1]
        return pltpu.make_async_copy(src_ref.at[pl.ds(my_block * r, r), :], land_ref.at[my_block], sem)
    r = src_ref.shape[0]
    return pltpu.make_async_copy(src_ref, land_ref.at[pl.ds(my_block * r, r), :], sem)


def exchange_start(srcs, lands, after, scatter, name):
    np_ = len(srcs)

    def body(*refs):
        src_refs, land_refs = refs[:np_], refs[np_:2 * np_]
        send_sems, recv_sems, own_sems = refs[2 * np_ + 1:2 * np_ + 4]
        token = refs[-1]
        my_block = _block_index(*_my_place())
        for p in range(np_):
            _own_copy(scatter, src_refs[p], land_refs[p], my_block, own_sems.at[p]).start()
            for k, peer in enumerate(_peers()):
                src, dst, _ = _exchange_windows(scatter, src_refs[p], land_refs[p], my_block, _block_index(*peer))
                pltpu.make_async_remote_copy(src_ref=src, dst_ref=dst, send_sem=send_sems.at[7 * p + k],
                                             recv_sem=recv_sems.at[7 * p + k], device_id=peer, device_id_type=MESH).start()
        token[...] = jnp.zeros_like(token)

    hbm = lambda a: pltpu.with_memory_space_constraint(a, pltpu.HBM)
    outs = pl.pallas_call(
        functools.partial(body), name=name,
        in_specs=[HBM] * (2 * np_) + [ANY],
        out_specs=[SEM, SEM, SEM] + [HBM] * (2 * np_) + [pl.BlockSpec(memory_space=pltpu.VMEM)],
        out_shape=[pltpu.SemaphoreType.DMA((7 * np_,)), pltpu.SemaphoreType.DMA((7 * np_,)), pltpu.SemaphoreType.DMA((np_,))]
        + [pltpu.HBM(a.shape, a.dtype) for a in list(srcs) + list(lands)] + [jax.ShapeDtypeStruct((8, LANES), F32)],
        input_output_aliases={i: 3 + i for i in range(2 * np_)},
        compiler_params=pltpu.CompilerParams(has_side_effects=pltpu.SideEffectType.DATAFLOW_SIDE_EFFECTING),
    )(*[hbm(a) for a in srcs], *[hbm(a) for a in lands], after)
    return dict(sems=outs[:3], srcs=outs[3:3 + np_], lands=outs[3 + np_:3 + 2 * np_], token=outs[-1], scatter=scatter)


def exchange_wait(started, after, name):
    afters = tuple(after) if isinstance(after, (tuple, list)) else (after,)
    srcs, lands = started["srcs"], started["lands"]
    scatter = started["scatter"]
    np_ = len(srcs)

    def body(*refs):
        src_refs, land_refs = refs[:np_], refs[np_:2 * np_]
        send_sems, recv_sems, own_sems = refs[2 * np_:2 * np_ + 3]
        my_block = _block_index(*_my_place())
        for p in range(np_):
            _own_copy(scatter, src_refs[p], land_refs[p], my_block, own_sems.at[p]).wait()
            for k, peer in enumerate(_peers()):
                src, dst, arrival = _exchange_windows(scatter, src_refs[p], land_refs[p], my_block, _block_index(*peer))
                pltpu.make_async_remote_copy(src_ref=src, dst_ref=dst, send_sem=send_sems.at[7 * p + k],
                                             recv_sem=recv_sems.at[7 * p + k], device_id=peer, device_id_type=MESH).wait_send()
                pltpu.make_async_remote_copy(src_ref=src, dst_ref=arrival, send_sem=send_sems.at[7 * p + k],
                                             recv_sem=recv_sems.at[7 * p + k], device_id=peer, device_id_type=MESH).wait_recv()

    outs = pl.pallas_call(
        functools.partial(body), name=name,
        in_specs=[HBM] * (2 * np_) + [SEM, SEM, SEM] + [ANY] * len(afters),
        out_specs=[HBM] * (2 * np_),
        out_shape=[pltpu.HBM(a.shape, a.dtype) for a in list(srcs) + list(lands)],
        input_output_aliases={i: i for i in range(2 * np_)},
        compiler_params=pltpu.CompilerParams(has_side_effects=pltpu.SideEffectType.DATAFLOW_SIDE_EFFECTING),
    )(*srcs, *lands, *started["sems"], *afters)
    return list(outs[np_:])


def _gather_zone(shard):
    return lax.empty((N_DEV * shard.shape[0], shard.shape[1]), shard.dtype)


def _scatter_zone(full):
    return lax.empty((N_DEV, full.shape[0] // N_DEV, full.shape[1]), full.dtype)


def allreduce_small(pack):
    r, c = pack.shape

    def body(pack_ref, out_ref, gathered, send_sems, recv_sems):
        me = _my_place()
        my_block = _block_index(*me)
        peers = _peers()

        def copy(k, slot, to):
            return pltpu.make_async_remote_copy(
                src_ref=pack_ref, dst_ref=gathered.at[slot], send_sem=send_sems.at[k], recv_sem=recv_sems.at[k],
                device_id=to, device_id_type=MESH)

        sends = [copy(k, my_block, peer) for k, peer in enumerate(peers)]
        for cp in sends:
            cp.start()
        gathered[my_block] = pack_ref[...]
        for k, peer in enumerate(peers):
            copy(k, _block_index(*peer), peer).wait_recv()
        for cp in sends:
            cp.wait_send()
        total = gathered[0]
        for j in range(1, N_DEV):
            total = total + gathered[j]
        out_ref[...] = total

    return pl.pallas_call(
        functools.partial(body), name="allreduce_small",
        in_specs=[pl.BlockSpec(memory_space=pltpu.VMEM)], out_specs=pl.BlockSpec(memory_space=pltpu.VMEM),
        out_shape=jax.ShapeDtypeStruct((r, c), F32),
        scratch_shapes=[pltpu.VMEM((N_DEV, r, c), F32), pltpu.SemaphoreType.DMA((7,)), pltpu.SemaphoreType.DMA((7,))],
    )(pack)


def adamw(w, m, v, parts, layer=0, prev=None):
    nl, r, c = w.shape
    n = parts.shape[0]
    br = 256 if r % 256 == 0 else r
    blk = pl.BlockSpec((None, br, c), lambda i: (layer, i, 0))
    n_prev = 0 if prev is None else 4

    def body(w_ref, m_ref, v_ref, p_ref, *rest):
        g_ref, d_ref, nm_ref, nv_ref = rest[n_prev:]
        g = p_ref[0].astype(F32)
        for j in range(1, n):
            g = g + p_ref[j].astype(F32)
        nm = ADAM_B1 * m_ref[...] + (1.0 - ADAM_B1) * g
        nv = ADAM_B2 * v_ref[...] + (1.0 - ADAM_B2) * (g * g)
        m_hat = nm / (1.0 - ADAM_B1 ** ADAM_STEP)
        v_hat = nv / (1.0 - ADAM_B2 ** ADAM_STEP)
        g_ref[...] = g
        d_ref[...] = -ADAM_LR * (m_hat / (jnp.sqrt(v_hat) + ADAM_EPS) + ADAM_WD * w_ref[...])
        nm_ref[...] = nm
        nv_ref[...] = nv

    return pl.pallas_call(
        functools.partial(body), name="adamw", grid=(r // br,),
        in_specs=[blk] * 3 + [pl.BlockSpec((n, br, c), lambda i: (0, i, 0))] + [ANY] * n_prev,
        out_specs=[blk] * 4, out_shape=[jax.ShapeDtypeStruct((nl, r, c), F32)] * 4,
        input_output_aliases={4 + k: k for k in range(n_prev)},
        compiler_params=_cparams(("parallel",)),
    )(w, m, v, parts, *(prev or ()))


def adamw_layers(w, m, v, layer_parts):
    outs = None
    for layer, parts in enumerate(layer_parts):
        outs = adamw(w, m, v, parts, layer, outs)
    return outs


def _pair_heads(a, axis, width=HEAD_DIM):
    shp = a.shape
    a = a.reshape(shp[:axis] + (2, 2, GQA, width) + shp[axis + 1:])
    return jnp.swapaxes(a, axis + 1, axis + 2).reshape(shp)


def _unpair_heads(a, axis, width=HEAD_DIM):
    shp = a.shape
    a = a.reshape(shp[:axis] + (2, GQA, 2, width) + shp[axis + 1:])
    return jnp.swapaxes(a, axis + 1, axis + 2).reshape(shp)


def _pad_rows(a, rows=8):
    return jnp.pad(a, ((0, rows - a.shape[0]), (0, 0)))


def kernel(x, p, mix_pre_g, mix_post_g, ffn_pre_g, ffn_post_g, pool_w, pool_scale, kv_norm_g, w_k, w_v, w_q, w_o, sinks, w_ff_gate, w_ff_up, w_ff_down, ple_norm_g, w_ple_gate, w_ple_proj, loss_target, m_mix_pre_g, m_mix_post_g, m_ffn_pre_g, m_ffn_post_g, m_pool_w, m_pool_scale, m_kv_norm_g, m_w_k, m_w_v, m_w_q, m_w_o, m_sinks, m_w_ff_gate, m_w_ff_up, m_w_ff_down, m_ple_norm_g, m_w_ple_gate, m_w_ple_proj, v_mix_pre_g, v_mix_post_g, v_ffn_pre_g, v_ffn_post_g, v_pool_w, v_pool_scale, v_kv_norm_g, v_w_k, v_w_v, v_w_q, v_w_o, v_sinks, v_w_ff_gate, v_w_ff_up, v_w_ff_down, v_ple_norm_g, v_w_ple_gate, v_w_ple_proj):
    depth = w_ff_gate.shape[0]
    n_a = pool_w.shape[0]
    t, d = x.shape[1], x.shape[2]
    h = x[0]
    tgt = loss_target[0]
    p_all = p.reshape(depth * t, p.shape[-1])
    my_block = _block_index(*_my_place())
    row = lambda g, i: g[i][None, :]
    bf = lambda a: a.astype(BF16)

    full, gathers = [None] * depth, {}
    start_tokens = jnp.zeros((), F32)
    for i in range(depth):
        shards = [bf(w_ff_gate[i].T), bf(w_ff_up[i].T), bf(w_ff_down[i]), bf(w_ple_gate[i]), bf(w_ple_proj[i].T)]
        if i == 0:
            pool0, scale_full = allgather_pieces([bf(pool_w[0].reshape(-1, POOL_GROUP)), _pad_rows(pool_scale)],
                                                 "allgather_pool0")
            order = pool0
        elif i < n_a:
            shards.append(bf(pool_w[i].reshape(-1, POOL_GROUP)))
        else:
            shards += [bf(_pair_heads(w_q[i - n_a], 1)), bf(w_o[i - n_a])]
            if i == n_a:
                shards.append(bf(jnp.concatenate([w_k, w_v], axis=1)))
        gathers[i] = exchange_start(shards, [_gather_zone(s) for s in shards], order, False, f"allgather_start_l{i}")
        order = gathers[i]["token"]
        start_tokens = start_tokens + order[0, 0]
    scale_full = scale_full.reshape(N_DEV, 8, -1)[:, :n_a].transpose(1, 0, 2).reshape(n_a, 1, d)

    cos, sin = _rope_tables(t, start_tokens)
    sink_b = [jnp.broadcast_to(_pair_heads(sinks[j][:, None], 0, 1), (N_HEADS, LANES)) for j in range(depth - n_a)]
    pool_full, wo_full = {}, {}

    saved = []
    kv = hk = None
    for i in range(depth):
        if i > 0:
            full[i] = exchange_wait(gathers[i], h, f"allgather_wait_l{i}")
        s = {"h0": h}
        if i < n_a:
            pool_full[i] = ((pool0 if i == 0 else full[i][5]).reshape(N_DEV, len(POOL_WINDOWS), -1, POOL_GROUP)
                            .transpose(1, 0, 2, 3).reshape(len(POOL_WINDOWS), POOL_GROUP, POOL_GROUP))
            gpre = row(mix_pre_g, i) + start_tokens if i == 0 else row(mix_pre_g, i)
            h1, a = pool_mix_fwd(h, gpre, pool_full[i], scale_full[i], row(mix_post_g, i), row(ffn_pre_g, i))
            if i == 0:
                full[0] = exchange_wait(gathers[0], (h1, cos, sin), "allgather_wait_l0")
        else:
            j = i - n_a
            wo_full[i] = _pair_heads(full[i][6], 0)
            if i == n_a:
                hk, kv = proj_rope_fwd(h, kv_norm_g[None, :], full[i][7], cos, sin, N_KV_HEADS * HEAD_DIM, "kv_proj_fwd")
            hn, q = proj_rope_fwd(h, row(mix_pre_g, i), full[i][5], cos, sin, d, "q_proj_fwd")
            attn = swa_fwd(q, kv, sink_b[j])
            m, h1, a = oproj_post_fwd(attn, wo_full[i], h, row(mix_post_g, i), row(ffn_pre_g, i))
            s.update(hn=hn, q=q, attn=attn, m=m)
        wg_t, wu_t, wd, wpg, wpp_t = full[i][:5]
        f, gte, up, hdn = ffn_fwd(a, wg_t, wu_t, wd)
        s.update(h1=h1, a=a, f=f, gte=gte, up=up, hdn=hdn)
        if i < depth - 1:
            h = post_ple_fwd(h1, f, p_all, i, row(ffn_post_g, i), row(ple_norm_g, i), wpg, wpp_t)
        saved.append(s)

    g_mix_pre, g_mix_post, g_ffn_pre, g_ffn_post, g_ple = ([None] * depth for _ in range(5))
    g_kv = g_sinks = None
    g_scale = [None] * n_a
    landing, scatters = [None] * depth, {}
    dkv_sum = []
    scatter_token = jnp.zeros((), F32)
    for i in reversed(range(depth)):
        s = saved[i]
        wg_t, wu_t, wd, wpg, wpp_t = full[i][:5]
        last = i == depth - 1
        dh2, df, ub, dzb, dppb, gacc = post_ple_bwd(tgt if last else dh, s["h1"], s["f"], p_all, i,
                                                    row(ffn_post_g, i) + scatter_token, row(ple_norm_g, i), wpg, wpp_t,
                                                    from_target=last)
        g_ple[i], g_ffn_post[i] = gacc[0], gacc[1]
        if last:
            loss_row = gacc[2][None, :]
        da, dgte, dup = ffn_bwd_act(df, s["gte"], s["up"], wg_t, wu_t, wd)
        grads = [xty(dgte, s["a"]), xty(dup, s["a"]), xty(s["hdn"], df), xty(ub, dzb), xty(dppb, p_all, i)]
        early = exchange_start(grads, [_scatter_zone(g) for g in grads], dh2, True, f"reduce_scatter_start_l{i}a")
        early_token = early["token"][0, 0]
        if i < n_a:
            dh, dpw, gacc = pool_mix_bwd(s["h0"], dh2, da, row(mix_pre_g, i) + early_token, pool_full[i], scale_full[i],
                                         row(mix_post_g, i), row(ffn_pre_g, i))
            g_mix_pre[i], g_mix_post[i], g_ffn_pre[i], g_scale[i] = gacc[0], gacc[1], gacc[2], gacc[3]
            dpw = dpw.reshape(len(POOL_WINDOWS), N_DEV, -1, POOL_GROUP).transpose(1, 0, 2, 3)
            grads = [bf(dpw.reshape(-1, POOL_GROUP))]
        else:
            j = i - n_a
            dh1, dmb, dattn, gacc = oproj_post_bwd(dh2, da, s["h1"], s["m"], wo_full[i], row(mix_post_g, i) + early_token,
                                                   row(ffn_pre_g, i))
            g_mix_post[i], g_ffn_pre[i] = gacc[0], gacc[1]
            dq, dkv, dsink = swa_bwd(s["q"], kv, dattn, sink_b[j])
            dkv_sum.append(dkv)
            g_sinks = [_unpair_heads(dsink[:, 0:1], 0, 1)[:, 0]] + (g_sinks or [])
            branches = [(row(mix_pre_g, i), full[i][5], d, [dq])]
            if i == n_a:
                branches.append((kv_norm_g[None, :], full[i][7], N_KV_HEADS * HEAD_DIM, dkv_sum))
            outs = proj_rope_bwd(dh1, s["h0"], cos, sin, branches, f"proj_bwd_l{i}")
            dh, gacc = outs[0], outs[-1]
            g_mix_pre[i] = gacc[0]
            grads = [xty(s["hn"], outs[1]), _unpair_heads(xty(s["attn"], dmb), 0)]
            if i == n_a:
                g_kv = gacc[1]
                grads.append(xty(hk, outs[2]))
        late = exchange_start(grads, [_scatter_zone(g) for g in grads], dh, True, f"reduce_scatter_start_l{i}b")
        scatter_token = late["token"][0, 0]
        scatters[i] = (early, late)
    grad_x = dh[None]
    after = dh
    for i in reversed(range(depth)):
        landing[i] = (exchange_wait(scatters[i][0], after, f"reduce_scatter_wait_l{i}a")
                      + exchange_wait(scatters[i][1], after, f"reduce_scatter_wait_l{i}b"))
        after = landing[i][0]

    sink_row = jnp.pad(jnp.concatenate(g_sinks)[None, :], ((0, 0), (0, d - sinks.size)))
    stack = lambda rows_: _pad_rows(jnp.stack(rows_))
    pack = jnp.concatenate([stack(g_mix_pre), stack(g_mix_post), stack(g_ffn_pre), stack(g_ffn_post), stack(g_ple),
                            _pad_rows(g_kv[None]), stack(g_scale), _pad_rows(sink_row), _pad_rows(loss_row)], axis=0)
    tot = allreduce_small(pack)
    sec = lambda k, n: tot[8 * k:8 * k + n]
    loss = jnp.sum(tot[64])
    small = {
        "mix_pre_g": sec(0, depth), "mix_post_g": sec(1, depth), "ffn_pre_g": sec(2, depth),
        "ffn_post_g": sec(3, depth), "ple_norm_g": sec(4, depth), "kv_norm_g": tot[40],
        "pool_scale": lax.dynamic_slice_in_dim(sec(6, n_a), my_block * pool_scale.shape[1], pool_scale.shape[1], axis=1),
        "sinks": tot[56, :sinks.size].reshape(sinks.shape),
    }

    weights = dict(mix_pre_g=mix_pre_g, mix_post_g=mix_post_g, ffn_pre_g=ffn_pre_g, ffn_post_g=ffn_post_g, pool_w=pool_w, pool_scale=pool_scale, kv_norm_g=kv_norm_g, w_k=w_k, w_v=w_v, w_q=w_q, w_o=w_o, sinks=sinks, w_ff_gate=w_ff_gate, w_ff_up=w_ff_up, w_ff_down=w_ff_down, ple_norm_g=ple_norm_g, w_ple_gate=w_ple_gate, w_ple_proj=w_ple_proj)
    mom1 = dict(mix_pre_g=m_mix_pre_g, mix_post_g=m_mix_post_g, ffn_pre_g=m_ffn_pre_g, ffn_post_g=m_ffn_post_g, pool_w=m_pool_w, pool_scale=m_pool_scale, kv_norm_g=m_kv_norm_g, w_k=m_w_k, w_v=m_w_v, w_q=m_w_q, w_o=m_w_o, sinks=m_sinks, w_ff_gate=m_w_ff_gate, w_ff_up=m_w_ff_up, w_ff_down=m_w_ff_down, ple_norm_g=m_ple_norm_g, w_ple_gate=m_w_ple_gate, w_ple_proj=m_w_ple_proj)
    mom2 = dict(mix_pre_g=v_mix_pre_g, mix_post_g=v_mix_post_g, ffn_pre_g=v_ffn_pre_g, ffn_post_g=v_ffn_post_g, pool_w=v_pool_w, pool_scale=v_pool_scale, kv_norm_g=v_kv_norm_g, w_k=v_w_k, w_v=v_w_v, w_q=v_w_q, w_o=v_w_o, sinks=v_sinks, w_ff_gate=v_w_ff_gate, w_ff_up=v_w_ff_up, w_ff_down=v_w_ff_down, ple_norm_g=v_ple_norm_g, w_ple_gate=v_w_ple_gate, w_ple_proj=v_w_ple_proj)

    swap = lambda a: jnp.swapaxes(a, 1, 2)
    same = lambda a: a
    att = range(n_a, depth)
    plan = {
        "w_ff_gate": (swap, swap, [landing[i][0] for i in range(depth)]),
        "w_ff_up": (swap, swap, [landing[i][1] for i in range(depth)]),
        "w_ff_down": (same, same, [landing[i][2] for i in range(depth)]),
        "w_ple_gate": (same, same, [landing[i][3] for i in range(depth)]),
        "w_ple_proj": (swap, swap, [landing[i][4] for i in range(depth)]),
        "pool_w": (lambda a: a.reshape(n_a, -1, POOL_GROUP), lambda a: a.reshape(pool_w.shape),
                   [landing[i][5] for i in range(n_a)]),
        "w_q": (lambda a: _pair_heads(a, 2), lambda a: _unpair_heads(a, 2), [landing[i][5] for i in att]),
        "w_o": (same, same, [landing[i][6] for i in att]),
    }
    for nme, g in small.items():
        w = weights[nme]
        plan[nme] = ((lambda a: a.reshape((1, -1, a.shape[-1]))), (lambda a, shp=w.shape: a.reshape(shp)),
                     [g.reshape((1, -1, w.shape[-1]))])
    results = {}
    for nme, (view, unview, layer_parts) in plan.items():
        outs = adamw_layers(view(weights[nme]), view(mom1[nme]), view(mom2[nme]), layer_parts)
        results[nme] = [unview(o) for o in outs]
    kv_cat = lambda ws: jnp.concatenate([ws["w_k"], ws["w_v"]], axis=1)[None]
    outs = adamw_layers(kv_cat(weights), kv_cat(mom1), kv_cat(mom2), [landing[n_a][7]])
    results["w_k"] = [o[0, :, :w_k.shape[1]] for o in outs]
    results["w_v"] = [o[0, :, w_k.shape[1]:] for o in outs]

    order = ["mix_pre_g", "mix_post_g", "ffn_pre_g", "ffn_post_g", "pool_w", "pool_scale", "kv_norm_g", "w_k", "w_v",
             "w_q", "w_o", "sinks", "w_ff_gate", "w_ff_up", "w_ff_down", "ple_norm_g", "w_ple_gate", "w_ple_proj"]
    g_out, d_out, m_out, v_out = ([results[nme][k] for nme in order] for k in range(4))
    return (loss, grad_x, *g_out, *d_out, *m_out, *v_out)
```

```python
import functools

import jax
import jax.numpy as jnp
from jax import lax
from jax.experimental import pallas as pl
from jax.experimental.pallas import tpu as pltpu

F32 = jnp.float32
BF16 = jnp.bfloat16

N_DEV = 8
HEAD_DIM = 64
N_HEADS = 16
N_KV_HEADS = 4
GQA = N_HEADS // N_KV_HEADS
BLOCK = 128
POOL_WINDOWS = (2, 4, 8, 16)
POOL_GROUP = 256
HALO = 16
ROPE_THETA = 10000.0
RMS_EPS = 1e-6
NEG_INF = -1e30
LANES = 128
ATTN_SUB = 8
XTY_ROWS = 2048
FFN_CHUNK = 768
VMEM_LIMIT = 56 * 1024 * 1024

ADAM_LR = 0.001
ADAM_B1 = 0.9
ADAM_B2 = 0.999
ADAM_EPS = 1e-08
ADAM_WD = 0.01
ADAM_STEP = 10

MESH = pl.DeviceIdType.MESH
ANY = pl.BlockSpec(memory_space=pl.ANY)

NT_DIMS = (((1,), (1,)), ((), ()))
TN_DIMS = (((0,), (0,)), ((), ()))


def _cparams(sem=None, vmem=None):
    kw = {}
    if sem is not None:
        kw["dimension_semantics"] = sem
    if vmem is not None:
        kw["vmem_limit_bytes"] = vmem
    return pltpu.CompilerParams(**kw)


def _rows(tm, n, first=0):
    return pl.BlockSpec((tm, n), lambda i: (i + first, 0))


def _rows_rev(tm, n, nt):
    return pl.BlockSpec((tm, n), lambda i: (nt - 1 - i, 0))


def _const(shape):
    nd = len(shape)
    return pl.BlockSpec(shape, lambda *_: (0,) * nd, pipeline_mode=pl.Buffered(1))


def _resident(shape):
    nd = len(shape)
    return pl.BlockSpec(shape, lambda *_: (0,) * nd)


def _tile_rows(t):
    return 512 if t % 512 == 0 else 128


def _dot(a, b):
    return jnp.dot(a, b, preferred_element_type=F32)


def _dot_nt(a, b):
    return lax.dot_general(a, b, NT_DIMS, preferred_element_type=F32)


def _dot_tn(a, b):
    return lax.dot_general(a, b, TN_DIMS, preferred_element_type=F32)


def _rms_r(x):
    return lax.rsqrt(jnp.mean(x * x, axis=-1, keepdims=True) + RMS_EPS)


def _rms_bwd(x, r, g, dy):
    gy = dy * g
    dx = r * gy - x * (r * r * r * jnp.mean(gy * x, axis=-1, keepdims=True))
    dg = jnp.sum(dy * (x * r), axis=0, keepdims=True)
    return dx, dg


def _sigmoid(x):
    return jax.nn.sigmoid(x)


def _rope_tables(t, zero_token):
    inv = 1.0 / (ROPE_THETA ** (jnp.arange(0, HEAD_DIM, 2, dtype=F32) / HEAD_DIM))
    ang = (jnp.arange(t, dtype=F32) + zero_token)[:, None] * jnp.tile(inv, 2 * LANES // HEAD_DIM)[None, :]
    sign = jnp.tile(jnp.repeat(jnp.array([-1.0, 1.0], F32), HEAD_DIM // 2), LANES // HEAD_DIM)
    return jnp.cos(ang), jnp.sin(ang) * sign[None, :]


def _swap_halves(x):
    n = x.shape[1]
    lane = lax.broadcasted_iota(jnp.int32, x.shape, 1)
    first = (lane % HEAD_DIM) < (HEAD_DIM // 2)
    return jnp.where(first, pltpu.roll(x, n - HEAD_DIM // 2, 1), pltpu.roll(x, HEAD_DIM // 2, 1))


def _rope(x, cos, sin):
    reps = x.shape[1] // LANES
    return x * jnp.tile(cos, (1, reps)) + _swap_halves(x) * jnp.tile(sin, (1, reps))


def _unrope(dy, cos, sin):
    reps = dy.shape[1] // LANES
    return dy * jnp.tile(cos, (1, reps)) + _swap_halves(dy * jnp.tile(sin, (1, reps)))


def _acc_init(acc_ref):
    @pl.when(pl.program_id(0) == 0)
    def _():
        acc_ref[...] = jnp.zeros_like(acc_ref)


def _window_sums(ext, tm, forward):
    n = tm + HALO
    out = []
    for g, w in enumerate(POOL_WINDOWS):
        s = ext[:, g * POOL_GROUP:(g + 1) * POOL_GROUP]
        k = 1
        while k < w:
            s = s + pltpu.roll(s, k if forward else n - k, 0)
            k *= 2
        out.append(s[HALO:, :] if forward else s[:tm, :])
    return out


def _pool_inv_counts(tile, tm):
    t = tile * tm + lax.broadcasted_iota(jnp.int32, (tm, 1), 0)
    return [1.0 / jnp.minimum(t + 1, w).astype(F32) for w in POOL_WINDOWS]


def _pool_mix(hn, ext, inv_cnts, pw_ref, scale, tm):
    sums = _window_sums(ext, tm, True)
    pooled, ys = [], []
    for g in range(len(POOL_WINDOWS)):
        pg = (sums[g] * inv_cnts[g] - hn[:, g * POOL_GROUP:(g + 1) * POOL_GROUP]).astype(BF16)
        pooled.append(pg)
        ys.append(_dot(pg, pw_ref[g]))
    y = jnp.concatenate(ys, axis=1)
    return pooled, y, y * scale


def pool_mix_fwd(h0, gpre, pool_w, scale, gpost, gffn):
    t, d = h0.shape
    tm = _tile_rows(t)

    def body(h_ref, gpre_ref, pw_ref, scale_ref, gpost_ref, gffn_ref, h1_ref, a_ref, carry):
        i = pl.program_id(0)

        @pl.when(i == 0)
        def _():
            carry[...] = jnp.zeros_like(carry)

        x = h_ref[...]
        hn = x * _rms_r(x) * gpre_ref[...]
        ext = jnp.concatenate([carry[...], hn], axis=0)
        carry[...] = hn[tm - HALO:, :]
        _, _, m = _pool_mix(hn, ext, _pool_inv_counts(i, tm), pw_ref, scale_ref[...], tm)
        h1 = x + m * _rms_r(m) * gpost_ref[...]
        h1_ref[...] = h1
        a_ref[...] = (h1 * _rms_r(h1) * gffn_ref[...]).astype(BF16)

    return pl.pallas_call(
        functools.partial(body), name="pool_mix_fwd", grid=(t // tm,),
        in_specs=[_rows(tm, d), _const((1, d)), _const(pool_w.shape), _const((1, d)), _const((1, d)), _const((1, d))],
        out_specs=[_rows(tm, d), _rows(tm, d)],
        out_shape=[jax.ShapeDtypeStruct((t, d), F32), jax.ShapeDtypeStruct((t, d), BF16)],
        scratch_shapes=[pltpu.VMEM((HALO, d), F32)],
        compiler_params=_cparams(("arbitrary",), VMEM_LIMIT),
    )(h0, gpre, pool_w, scale, gpost, gffn)


def pool_mix_bwd(h0, dh2, da, gpre, pool_w, scale, gpost, gffn):
    t, d = h0.shape
    tm = _tile_rows(t)
    nt = t // tm
    hb = tm // HALO

    def body(h_ref, halo_ref, dh2_ref, da_ref, gpre_ref, pw_ref, scale_ref, gpost_ref, gffn_ref,
             dh0_ref, dpw_ref, gacc_ref, carry):
        i = pl.program_id(0)
        tile = nt - 1 - i
        _acc_init(gacc_ref)
        _acc_init(dpw_ref)

        @pl.when(i == 0)
        def _():
            carry[...] = jnp.zeros_like(carry)

        x = h_ref[...]
        gpre_v, scale_v, gpost_v, gffn_v = gpre_ref[...], scale_ref[...], gpost_ref[...], gffn_ref[...]
        r0 = _rms_r(x)
        hn = x * r0 * gpre_v
        xh = halo_ref[...]
        hn_halo = jnp.where(tile > 0, xh * _rms_r(xh) * gpre_v, 0.0)
        ext = jnp.concatenate([hn_halo, hn], axis=0)
        inv_cnts = _pool_inv_counts(tile, tm)
        pooled, y, m = _pool_mix(hn, ext, inv_cnts, pw_ref, scale_v, tm)
        rm = _rms_r(m)
        h1 = x + m * rm * gpost_v
        dh1_n, dgffn = _rms_bwd(h1, _rms_r(h1), gffn_v, da_ref[...].astype(F32))
        dh1 = dh2_ref[...] + dh1_n
        dm, dgpost = _rms_bwd(m, rm, gpost_v, dh1)
        dscale = jnp.sum(dm * y, axis=0, keepdims=True)
        dy = (dm * scale_v).astype(BF16)
        dpn = []
        for g in range(len(POOL_WINDOWS)):
            dyg = dy[:, g * POOL_GROUP:(g + 1) * POOL_GROUP]
            dpw_ref[g] += _dot_tn(pooled[g], dyg)
            dpn.append(_dot_nt(dyg, pw_ref[g]))
        dpooled = jnp.concatenate(dpn, axis=1)
        dpc = jnp.concatenate([dpn[g] * inv_cnts[g] for g in range(len(POOL_WINDOWS))], axis=1)
        ext2 = jnp.concatenate([dpc, carry[...]], axis=0)
        carry[...] = dpc[:HALO, :]
        dhn = jnp.concatenate(_window_sums(ext2, tm, False), axis=1) - dpooled
        dh0_n, dgpre = _rms_bwd(x, r0, gpre_v, dhn)
        dh0_ref[...] = dh1 + dh0_n
        gacc_ref[0:1, :] += dgpre
        gacc_ref[1:2, :] += dgpost
        gacc_ref[2:3, :] += dgffn
        gacc_ref[3:4, :] += dscale

    return pl.pallas_call(
        functools.partial(body), name="pool_mix_bwd", grid=(nt,),
        in_specs=[_rows_rev(tm, d, nt),
                  pl.BlockSpec((HALO, d), lambda i: (jnp.maximum((nt - 1 - i) * hb - 1, 0), 0)),
                  _rows_rev(tm, d, nt), _rows_rev(tm, d, nt),
                  _const((1, d)), _const(pool_w.shape), _const((1, d)), _const((1, d)), _const((1, d))],
        out_specs=[_rows_rev(tm, d, nt), _resident(pool_w.shape), _resident((8, d))],
        out_shape=[jax.ShapeDtypeStruct((t, d), F32), jax.ShapeDtypeStruct(pool_w.shape, F32),
                   jax.ShapeDtypeStruct((8, d), F32)],
        scratch_shapes=[pltpu.VMEM((HALO, d), F32)],
        compiler_params=_cparams(("arbitrary",), VMEM_LIMIT),
    )(h0, h0, dh2, da, gpre, pool_w, scale, gpost, gffn)


def _ffn_chunks(f):
    return [(c, min(c + FFN_CHUNK, f)) for c in range(0, f, FFN_CHUNK)]


def ffn_fwd(a, wg_t, wu_t, wd):
    t, d = a.shape
    f = wd.shape[0]
    tm = _tile_rows(t)

    def body(a_ref, wg_ref, wu_ref, wd_ref, f_ref, gte_ref, up_ref, hdn_ref):
        av = a_ref[...]
        acc = jnp.zeros((tm, d), F32)
        for c0, c1 in _ffn_chunks(f):
            gte = _dot_nt(av, wg_ref[c0:c1, :])
            up = _dot_nt(av, wu_ref[c0:c1, :])
            gte_ref[:, c0:c1] = gte.astype(BF16)
            up_ref[:, c0:c1] = up.astype(BF16)
            hdn = (gte * _sigmoid(gte) * up).astype(BF16)
            hdn_ref[:, c0:c1] = hdn
            acc = acc + _dot(hdn, wd_ref[c0:c1, :])
        f_ref[...] = acc.astype(BF16)

    return pl.pallas_call(
        functools.partial(body), name="ffn_fwd", grid=(t // tm,),
        in_specs=[_rows(tm, d), _const((f, d)), _const((f, d)), _const((f, d))],
        out_specs=[_rows(tm, d), _rows(tm, f), _rows(tm, f), _rows(tm, f)],
        out_shape=[jax.ShapeDtypeStruct((t, d), BF16)] + [jax.ShapeDtypeStruct((t, f), BF16)] * 3,
        compiler_params=_cparams(("parallel",), VMEM_LIMIT),
    )(a, wg_t, wu_t, wd)


def ffn_bwd_act(df, gte, up, wg_t, wu_t, wd):
    t, d = df.shape
    f = wd.shape[0]
    tm = _tile_rows(t)

    def body(df_ref, gte_ref, up_ref, wg_ref, wu_ref, wd_ref, da_ref, dgte_ref, dup_ref):
        dfv = df_ref[...]
        chunks = _ffn_chunks(f)
        half = chunks[len(chunks) // 2][0]
        acc = None
        for c0, c1 in chunks:
            g = gte_ref[:, c0:c1].astype(F32)
            u = up_ref[:, c0:c1].astype(F32)
            sg = _sigmoid(g)
            sl = g * sg
            dh = _dot_nt(dfv, wd_ref[c0:c1, :])
            dup_ref[:, c0:c1] = (dh * sl).astype(BF16)
            dgte_ref[:, c0:c1] = (dh * u * (sg * (1.0 + g * (1.0 - sg)))).astype(BF16)
            if c1 == half:
                acc = _dot(dgte_ref[:, :half], wg_ref[:half, :]) + _dot(dup_ref[:, :half], wu_ref[:half, :])
        da = acc + _dot(dgte_ref[:, half:], wg_ref[half:, :]) + _dot(dup_ref[:, half:], wu_ref[half:, :])
        da_ref[...] = da.astype(BF16)

    return pl.pallas_call(
        functools.partial(body), name="ffn_bwd_act", grid=(t // tm,),
        in_specs=[_rows(tm, d), _rows(tm, f), _rows(tm, f), _const((f, d)), _const((f, d)), _const((f, d))],
        out_specs=[_rows(tm, d), _rows(tm, f), _rows(tm, f)],
        out_shape=[jax.ShapeDtypeStruct((t, d), BF16)] + [jax.ShapeDtypeStruct((t, f), BF16)] * 2,
        compiler_params=_cparams(("parallel",), VMEM_LIMIT),
    )(df, gte, up, wg_t, wu_t, wd)


def xty(x, y, y_part=0):
    t, nx = x.shape
    ny = y.shape[1]
    tk = XTY_ROWS if t % XTY_ROWS == 0 else _tile_rows(t)
    bn = nx // 2 if nx > 1024 else nx
    nk = t // tk

    def body(x_ref, y_ref, o_ref, acc):
        k = pl.program_id(1)

        @pl.when(k == 0)
        def _():
            acc[...] = jnp.zeros_like(acc)

        acc[...] += _dot_tn(x_ref[...].astype(BF16), y_ref[...].astype(BF16))

        @pl.when(k == nk - 1)
        def _():
            o_ref[...] = acc[...].astype(BF16)

    return pl.pallas_call(
        functools.partial(body), name="xty", grid=(nx // bn, nk),
        in_specs=[pl.BlockSpec((tk, bn), lambda j, k: (k, j)),
                  pl.BlockSpec((tk, ny), lambda j, k: (k + y_part * nk, 0))],
        out_specs=pl.BlockSpec((bn, ny), lambda j, k: (j, 0)),
        out_shape=jax.ShapeDtypeStruct((nx, ny), BF16),
        scratch_shapes=[pltpu.VMEM((bn, ny), F32)],
        compiler_params=_cparams(("parallel", "arbitrary"), VMEM_LIMIT),
    )(x, y)


def _ple_fwd_tile(h1, f, p, gpost, gple, wpg_ref, wpp_ref):
    rf = _rms_r(f)
    h2 = h1 + f * rf * gpost
    r2 = _rms_r(h2)
    ub = (h2 * r2 * gple).astype(BF16)
    gate = _sigmoid(_dot(ub, wpg_ref[...]))
    pp = _dot_nt(p.astype(BF16), wpp_ref[...])
    return rf, h2, r2, ub, gate, pp


def post_ple_fwd(h1, f, p, layer, gpost, gple, wpg, wpp_t):
    t, d = h1.shape
    pd = p.shape[1]
    tm = _tile_rows(t)

    def body(h1_ref, f_ref, p_ref, gpost_ref, gple_ref, wpg_ref, wpp_ref, out_ref):
        _, h2, _, _, gate, pp = _ple_fwd_tile(h1_ref[...], f_ref[...].astype(F32), p_ref[...], gpost_ref[...],
                                              gple_ref[...], wpg_ref, wpp_ref)
        out_ref[...] = h2 + pp * gate

    return pl.pallas_call(
        functools.partial(body), name="post_ple_fwd", grid=(t // tm,),
        in_specs=[_rows(tm, d), _rows(tm, d), _rows(tm, pd, layer * (t // tm)), _const((1, d)), _const((1, d)),
                  _const(wpg.shape), _const(wpp_t.shape)],
        out_specs=_rows(tm, d), out_shape=jax.ShapeDtypeStruct((t, d), F32),
        compiler_params=_cparams(("parallel",), VMEM_LIMIT),
    )(h1, f, p, gpost, gple, wpg, wpp_t)


def post_ple_bwd(dh3, h1, f, p, layer, gpost, gple, wpg, wpp_t, from_target=False):
    t, d = h1.shape
    pd = p.shape[1]
    tm = _tile_rows(t)

    def body(dh3_ref, h1_ref, f_ref, p_ref, gpost_ref, gple_ref, wpg_ref, wpp_ref,
             dh2_ref, df_ref, u_ref, dz_ref, dpp_ref, gacc_ref):
        _acc_init(gacc_ref)
        gpost_v, gple_v = gpost_ref[...], gple_ref[...]
        nsub = 2 if tm % 16 == 0 else 1
        for sb in range(nsub):
            rows = slice(sb * (tm // nsub), (sb + 1) * (tm // nsub))
            fv = f_ref[rows, :].astype(F32)
            rf, h2, r2, ub, gate, pp = _ple_fwd_tile(h1_ref[rows, :], fv, p_ref[rows, :], gpost_v, gple_v, wpg_ref,
                                                     wpp_ref)
            if from_target:
                err = h2 + pp * gate - dh3_ref[rows, :]
                dh3v = err * (1.0 / d)
                gacc_ref[2:3, :] += jnp.sum(err * err, axis=0, keepdims=True) * (0.5 / d)
            else:
                dh3v = dh3_ref[rows, :]
            dpp_ref[rows, :] = (dh3v * gate).astype(BF16)
            dz = (dh3v * pp * gate * (1.0 - gate)).astype(BF16)
            dz_ref[rows, :] = dz
            u_ref[rows, :] = ub
            du = _dot_nt(dz, wpg_ref[...])
            dh2_n, dgple = _rms_bwd(h2, r2, gple_v, du)
            dh2 = dh3v + dh2_n
            df, dgpost = _rms_bwd(fv, rf, gpost_v, dh2)
            dh2_ref[rows, :] = dh2
            df_ref[rows, :] = df.astype(BF16)
            gacc_ref[0:1, :] += dgple
            gacc_ref[1:2, :] += dgpost

    return pl.pallas_call(
        functools.partial(body), name="post_ple_loss_bwd" if from_target else "post_ple_bwd", grid=(t // tm,),
        in_specs=[_rows(tm, d), _rows(tm, d), _rows(tm, d), _rows(tm, pd, layer * (t // tm)), _const((1, d)),
                  _const((1, d)), _const(wpg.shape), _const(wpp_t.shape)],
        out_specs=[_rows(tm, d)] * 5 + [_resident((8, d))],
        out_shape=[jax.ShapeDtypeStruct((t, d), F32)] + [jax.ShapeDtypeStruct((t, d), BF16)] * 4
        + [jax.ShapeDtypeStruct((8, d), F32)],
        compiler_params=_cparams(("arbitrary",), VMEM_LIMIT),
    )(dh3, h1, f, p, gpost, gple, wpg, wpp_t)


def proj_rope_fwd(h, gain, w, cos, sin, n_rope, name):
    t, d = h.shape
    n = w.shape[1]
    tm = _tile_rows(t)

    def body(h_ref, g_ref, w_ref, cos_ref, sin_ref, hn_ref, y_ref):
        x = h_ref[...]
        hn = (x * _rms_r(x) * g_ref[...]).astype(BF16)
        hn_ref[...] = hn
        y = _dot(hn, w_ref[...])
        y_ref[:, :n_rope] = _rope(y[:, :n_rope], cos_ref[...], sin_ref[...]).astype(BF16)
        if n_rope < n:
            y_ref[:, n_rope:] = y[:, n_rope:].astype(BF16)

    return pl.pallas_call(
        functools.partial(body), name=name, grid=(t // tm,),
        in_specs=[_rows(tm, d), _const((1, d)), _const(w.shape), _rows(tm, LANES), _rows(tm, LANES)],
        out_specs=[_rows(tm, d), _rows(tm, n)],
        out_shape=[jax.ShapeDtypeStruct((t, d), BF16), jax.ShapeDtypeStruct((t, n), BF16)],
        compiler_params=_cparams(("parallel",), VMEM_LIMIT),
    )(h, gain, w, cos, sin)


def proj_rope_bwd(dh1, h0, cos, sin, branches, name):
    t, d = h0.shape
    tm = _tile_rows(t)
    nb = len(branches)
    n_cot = [len(b[3]) for b in branches]

    def body(*refs):
        dh1_ref, h0_ref, cos_ref, sin_ref = refs[:4]
        pos = 4
        br_refs = []
        for b in range(nb):
            br_refs.append((refs[pos], refs[pos + 1], refs[pos + 2:pos + 2 + n_cot[b]]))
            pos += 2 + n_cot[b]
        dh0_ref = refs[pos]
        dpre_refs = refs[pos + 1:pos + 1 + nb]
        gacc_ref = refs[pos + 1 + nb]
        _acc_init(gacc_ref)
        x = h0_ref[...]
        r0 = _rms_r(x)
        dh = dh1_ref[...]
        for b in range(nb):
            g_ref, w_ref, cot_refs = br_refs[b]
            n_rope = branches[b][2]
            dy = cot_refs[0][...].astype(F32)
            for c_ref in cot_refs[1:]:
                dy = dy + c_ref[...].astype(F32)
            n = dy.shape[1]
            dpre_refs[b][:, :n_rope] = _unrope(dy[:, :n_rope], cos_ref[...], sin_ref[...]).astype(BF16)
            if n_rope < n:
                dpre_refs[b][:, n_rope:] = dy[:, n_rope:].astype(BF16)
            dhn = _dot_nt(dpre_refs[b][...], w_ref[...])
            dx, dg = _rms_bwd(x, r0, g_ref[...], dhn)
            dh = dh + dx
            gacc_ref[b:b + 1, :] += dg
        dh0_ref[...] = dh

    in_specs = [_rows(tm, d), _rows(tm, d), _rows(tm, LANES), _rows(tm, LANES)]
    args = [dh1, h0, cos, sin]
    out_specs = [_rows(tm, d)]
    out_shape = [jax.ShapeDtypeStruct((t, d), F32)]
    for gain, w, _, cots in branches:
        n = w.shape[1]
        in_specs += [_const((1, d)), _const(w.shape)] + [_rows(tm, n)] * len(cots)
        args += [gain, w] + list(cots)
        out_specs.append(_rows(tm, n))
        out_shape.append(jax.ShapeDtypeStruct((t, n), BF16))
    out_specs.append(_resident((8, d)))
    out_shape.append(jax.ShapeDtypeStruct((8, d), F32))
    return pl.pallas_call(
        functools.partial(body), name=name, grid=(t // tm,),
        in_specs=in_specs, out_specs=out_specs, out_shape=out_shape,
        compiler_params=_cparams(("arbitrary",), VMEM_LIMIT),
    )(*args)


def _tri():
    row = lax.broadcasted_iota(jnp.int32, (BLOCK, BLOCK), 0)
    col = lax.broadcasted_iota(jnp.int32, (BLOCK, BLOCK), 1)
    return col <= row


def _block_diag(x):
    lo = lax.broadcasted_iota(jnp.int32, x.shape, 1) < HEAD_DIM
    zero = jnp.zeros_like(x)
    return jnp.concatenate([jnp.where(lo, x, zero), jnp.where(lo, zero, x)], axis=0)


def _dense(x, tri):
    return (jnp.where(tri, x[:, BLOCK:2 * BLOCK], x[:, :BLOCK]),
            jnp.where(tri, x[:, 3 * BLOCK:], x[:, 2 * BLOCK:3 * BLOCK]))


def _banded(xa, xb, tri):
    zero = jnp.zeros_like(xa)
    return jnp.concatenate([jnp.where(tri, zero, xa), jnp.where(tri, xa, zero),
                            jnp.where(tri, zero, xb), jnp.where(tri, xb, zero)], axis=1).astype(BF16)


def _softmax_sink(s, sink):
    mx = jnp.maximum(jnp.max(s, axis=1, keepdims=True), sink)
    e = jnp.exp(s - mx)
    es = jnp.exp(sink - mx)
    inv = 1.0 / (jnp.sum(e, axis=1, keepdims=True) + es)
    return e * inv, es * inv


def _sink_column(sink_ref):
    return jnp.concatenate([jnp.broadcast_to(sink_ref[h:h + 1, 0:1], (BLOCK, 1)) for h in range(N_HEADS)], axis=0)


def _kv_block_diag(band, kvw):
    n_lt = kvw // LANES
    return ([_block_diag(band[:, lt * LANES:(lt + 1) * LANES]) for lt in range(n_lt)],
            [_block_diag(band[:, kvw + lt * LANES:kvw + (lt + 1) * LANES]) for lt in range(n_lt)])


def _all_probs(q_ref, r0, kbd, tri, n, sink_ref):
    dense = []
    for tq in range(N_HEADS // 2):
        s = _dot_nt(q_ref[r0:r0 + BLOCK, tq * LANES:(tq + 1) * LANES], kbd[tq // GQA])
        dense += list(_dense(s, tri))
    bias = jnp.where(jnp.logical_not(tri) & (n == 0), NEG_INF, 0.0)
    s_all = jnp.concatenate(dense, axis=0) * (HEAD_DIM ** -0.5) + jnp.concatenate([bias] * N_HEADS, axis=0)
    return _softmax_sink(s_all, _sink_column(sink_ref))


def _head_rows(x, tq):
    return x[2 * tq * BLOCK:(2 * tq + 1) * BLOCK], x[(2 * tq + 1) * BLOCK:(2 * tq + 2) * BLOCK]


def _attn_sub(t):
    return ATTN_SUB if t % (ATTN_SUB * BLOCK) == 0 else 1


def swa_fwd(q, kv, sink_b, w_o, h0, gpost, gffn):
    t, d = q.shape
    sub = _attn_sub(t)
    rows = sub * BLOCK
    kvw = N_KV_HEADS * HEAD_DIM

    def body(q_ref, kvc_ref, kvp_ref, sink_ref, w_ref, h0_ref, gpost_ref, gffn_ref, o_ref, m_ref, h1_ref, a_ref):
        i = pl.program_id(0)
        tri = _tri()
        ext = jnp.concatenate([kvp_ref[...], kvc_ref[...]], axis=0)
        for sb in range(sub):
            r0 = sb * BLOCK
            kbd, vbd = _kv_block_diag(ext[r0:r0 + 2 * BLOCK], kvw)
            p, _ = _all_probs(q_ref, r0, kbd, tri, i * sub + sb, sink_ref)
            for tq in range(N_HEADS // 2):
                pa, pb = _head_rows(p, tq)
                o_ref[r0:r0 + BLOCK, tq * LANES:(tq + 1) * LANES] = _dot(_banded(pa, pb, tri), vbd[tq // GQA]).astype(BF16)
            blk = slice(r0, r0 + BLOCK)
            m = _dot(o_ref[blk, :], w_ref[...])
            m_ref[blk, :] = m.astype(BF16)
            h1 = h0_ref[blk, :] + m * _rms_r(m) * gpost_ref[...]
            h1_ref[blk, :] = h1
            a_ref[blk, :] = (h1 * _rms_r(h1) * gffn_ref[...]).astype(BF16)

    return pl.pallas_call(
        functools.partial(body), name="swa_fwd", grid=(t // rows,),
        in_specs=[_rows(rows, d), _rows(rows, 2 * kvw),
                  pl.BlockSpec((BLOCK, 2 * kvw), lambda i: (jnp.maximum(i * sub - 1, 0), 0)), _const(sink_b.shape),
                  _const(w_o.shape), _rows(rows, d), _const((1, d)), _const((1, d))],
        out_specs=[_rows(rows, d)] * 4,
        out_shape=[jax.ShapeDtypeStruct((t, d), BF16), jax.ShapeDtypeStruct((t, d), BF16),
                   jax.ShapeDtypeStruct((t, d), F32), jax.ShapeDtypeStruct((t, d), BF16)],
        compiler_params=_cparams(("parallel",), VMEM_LIMIT),
    )(q, kv, kv, sink_b, w_o, h0, gpost, gffn)


def swa_bwd(q, kv, do, sink_b):
    t, d = q.shape
    sub = _attn_sub(t)
    nq = t // (sub * BLOCK)
    kvw = N_KV_HEADS * HEAD_DIM

    def body(q_ref, do_ref, kvc_ref, kvp_ref, sink_ref, dq_ref, dkv_ref, dsink_ref, carry):
        i = pl.program_id(0)
        step = nq - 1 - i
        _acc_init(dsink_ref)

        @pl.when(i == 0)
        def _():
            carry[...] = jnp.zeros_like(carry)

        tri = _tri()
        lo = lax.broadcasted_iota(jnp.int32, (2 * BLOCK, LANES), 1) < HEAD_DIM
        ext = jnp.concatenate([kvp_ref[...], kvc_ref[...]], axis=0)
        dkeys = [None] * (sub + 1)
        for sb in reversed(range(sub)):
            r0 = sb * BLOCK
            kbd, vbd = _kv_block_diag(ext[r0:r0 + 2 * BLOCK], kvw)
            p, ps = _all_probs(q_ref, r0, kbd, tri, step * sub + sb, sink_ref)
            dp = []
            for tq in range(N_HEADS // 2):
                dp += list(_dense(_dot_nt(do_ref[r0:r0 + BLOCK, tq * LANES:(tq + 1) * LANES], vbd[tq // GQA]), tri))
            dp = jnp.concatenate(dp, axis=0)
            delta = jnp.sum(p * dp, axis=1, keepdims=True)
            ds = p * (dp - delta) * (HEAD_DIM ** -0.5)
            dsk = ps * delta
            for h in range(N_HEADS):
                dsink_ref[h:h + 1, :] -= jnp.sum(dsk[h * BLOCK:(h + 1) * BLOCK], axis=0, keepdims=True)
            dkb = [jnp.zeros((4 * BLOCK, LANES), F32) for _ in kbd]
            dvb = [jnp.zeros((4 * BLOCK, LANES), F32) for _ in kbd]
            for tq in range(N_HEADS // 2):
                lt = tq // GQA
                cols = slice(tq * LANES, (tq + 1) * LANES)
                dsb = _banded(*_head_rows(ds, tq), tri)
                dq_ref[r0:r0 + BLOCK, cols] = _dot(dsb, kbd[lt]).astype(BF16)
                dkb[lt] = dkb[lt] + _dot_tn(dsb, q_ref[r0:r0 + BLOCK, cols])
                dvb[lt] = dvb[lt] + _dot_tn(_banded(*_head_rows(p, tq), tri), do_ref[r0:r0 + BLOCK, cols])
            dall = jnp.concatenate([jnp.where(lo, x[:2 * BLOCK], x[2 * BLOCK:]) for x in dkb + dvb], axis=1)
            dkeys[sb + 1] = dall[BLOCK:] if dkeys[sb + 1] is None else dkeys[sb + 1] + dall[BLOCK:]
            dkeys[sb] = dall[:BLOCK]
        for sb in range(sub):
            own = dkeys[sb + 1] + carry[...] if sb == sub - 1 else dkeys[sb + 1]
            dkv_ref[sb * BLOCK:(sb + 1) * BLOCK, :] = own
        carry[...] = dkeys[0]

    rev = lambda i: (nq - 1 - i, 0)
    return pl.pallas_call(
        functools.partial(body), name="swa_bwd", grid=(nq,),
        in_specs=[pl.BlockSpec((sub * BLOCK, d), rev), pl.BlockSpec((sub * BLOCK, d), rev),
                  pl.BlockSpec((sub * BLOCK, 2 * kvw), rev),
                  pl.BlockSpec((BLOCK, 2 * kvw), lambda i: (jnp.maximum((nq - 1 - i) * sub - 1, 0), 0)),
                  _const(sink_b.shape)],
        out_specs=[pl.BlockSpec((sub * BLOCK, d), rev), pl.BlockSpec((sub * BLOCK, 2 * kvw), rev),
                   _resident(sink_b.shape)],
        out_shape=[jax.ShapeDtypeStruct((t, d), BF16), jax.ShapeDtypeStruct((t, 2 * kvw), F32),
                   jax.ShapeDtypeStruct(sink_b.shape, F32)],
        scratch_shapes=[pltpu.VMEM((BLOCK, 2 * kvw), F32)],
        compiler_params=_cparams(("arbitrary",), VMEM_LIMIT),
    )(q, do, kv, kv, sink_b)


def oproj_post_bwd(dh2, da, h1, m, w_o, gpost, gffn):
    t, d = h1.shape
    tm = _tile_rows(t)

    def body(dh2_ref, da_ref, h1_ref, m_ref, w_ref, gpost_ref, gffn_ref, dh1_ref, dm_ref, dat_ref, gacc_ref):
        _acc_init(gacc_ref)
        h1v, mv = h1_ref[...], m_ref[...].astype(F32)
        dh1_n, dgffn = _rms_bwd(h1v, _rms_r(h1v), gffn_ref[...], da_ref[...].astype(F32))
        dh1 = dh2_ref[...] + dh1_n
        dm, dgpost = _rms_bwd(mv, _rms_r(mv), gpost_ref[...], dh1)
        dmb = dm.astype(BF16)
        dh1_ref[...] = dh1
        dm_ref[...] = dmb
        dat_ref[...] = _dot_nt(dmb, w_ref[...]).astype(BF16)
        gacc_ref[0:1, :] += dgpost
        gacc_ref[1:2, :] += dgffn

    return pl.pallas_call(
        functools.partial(body), name="oproj_post_bwd", grid=(t // tm,),
        in_specs=[_rows(tm, d)] * 4 + [_const(w_o.shape), _const((1, d)), _const((1, d))],
        out_specs=[_rows(tm, d)] * 3 + [_resident((8, d))],
        out_shape=[jax.ShapeDtypeStruct((t, d), F32), jax.ShapeDtypeStruct((t, d), BF16),
                   jax.ShapeDtypeStruct((t, d), BF16), jax.ShapeDtypeStruct((8, d), F32)],
        compiler_params=_cparams(("arbitrary",), VMEM_LIMIT),
    )(dh2, da, h1, m, w_o, gpost, gffn)


def _my_place():
    return lax.axis_index("x"), lax.axis_index("y"), lax.axis_index("c")


def _block_index(px, py, pc):
    return 4 * px + 2 * py + pc


def allgather_pieces(shards, name):
    np_ = len(shards)

    def body(*refs):
        in_refs, out_refs = refs[:np_], refs[np_:2 * np_]
        send_sems, recv_sems, local_sems = refs[2 * np_:]
        x, y, c = _my_place()
        me, sibling = (x, y, c), (x, y, 1 - c)
        chips = [(1 - x, y), (x, 1 - y), (1 - x, 1 - y)]

        def rows(p, place):
            r = in_refs[p].shape[0]
            return out_refs[p].at[pl.ds(_block_index(*place) * r, r), :]

        def copy(p, k, block, to, src=None):
            return pltpu.make_async_remote_copy(
                src_ref=rows(p, block) if src is None else src, dst_ref=rows(p, block),
                send_sem=send_sems.at[p, k], recv_sem=recv_sems.at[p, k], device_id=to, device_id_type=MESH)

        mine = [pltpu.make_async_copy(in_refs[p], rows(p, me), local_sems.at[p]) for p in range(np_)]
        first, passed = [], []
        for p in range(np_):
            mine[p].start()
            first.append(copy(p, 0, me, sibling, src=in_refs[p]))
            first += [copy(p, 1 + j, me, (*chip, c), src=in_refs[p]) for j, chip in enumerate(chips)]
        for cp in first:
            cp.start()
        for p in range(np_):
            for j, chip in enumerate(chips):
                copy(p, 1 + j, (*chip, c), me).wait_recv()
                fwd = copy(p, 4 + j, (*chip, c), sibling)
                fwd.start()
                passed.append(fwd)
        for p in range(np_):
            copy(p, 0, sibling, me).wait_recv()
            for j, chip in enumerate(chips):
                copy(p, 4 + j, (*chip, 1 - c), me).wait_recv()
        for cp in first + passed:
            cp.wait_send()
        for cp in mine:
            cp.wait()

    return pl.pallas_call(
        functools.partial(body), name=name,
        in_specs=[ANY] * np_, out_specs=[ANY] * np_,
        out_shape=[jax.ShapeDtypeStruct((N_DEV * s.shape[0], s.shape[1]), s.dtype) for s in shards],
        scratch_shapes=[pltpu.SemaphoreType.DMA((np_, 7)), pltpu.SemaphoreType.DMA((np_, 7)),
                        pltpu.SemaphoreType.DMA((np_,))],
    )(*shards)


def _peers():
    x, y, c = _my_place()
    flips = [(fx, fy, fc) for fx in (0, 1) for fy in (0, 1) for fc in (0, 1)][1:]
    return [(1 - x if fx else x, 1 - y if fy else y, 1 - c if fc else c) for fx, fy, fc in flips]


HBM = pl.BlockSpec(memory_space=pltpu.HBM)
SEM = pl.BlockSpec(memory_space=pltpu.SEMAPHORE)


def _exchange_windows(scatter, src_ref, land_ref, my_block, peer_block):
    if scatter:
        r = land_ref.shape[1]
        return src_ref.at[pl.ds(peer_block * r, r), :], land_ref.at[my_block], land_ref.at[peer_block]
    r = src_ref.shape[0]
    return src_ref, land_ref.at[pl.ds(my_block * r, r), :], land_ref.at[pl.ds(peer_block * r, r), :]


def _own_copy(scatter, src_ref, land_ref, my_block, sem):
    if scatter:
        r = land_ref.shape[1]
        return pltpu.make_async_copy(src_ref.at[pl.ds(my_block * r, r), :], land_ref.at[my_block], sem)
    r = src_ref.shape[0]
    return pltpu.make_async_copy(src_ref, land_ref.at[pl.ds(my_block * r, r), :], sem)


def exchange_start(srcs, lands, after, scatter, name):
    np_ = len(srcs)

    def body(*refs):
        src_refs, land_refs = refs[:np_], refs[np_:2 * np_]
        send_sems, recv_sems, own_sems = refs[2 * np_ + 1:2 * np_ + 4]
        token = refs[-1]
        my_block = _block_index(*_my_place())
        for p in range(np_):
            _own_copy(scatter, src_refs[p], land_refs[p], my_block, own_sems.at[p]).start()
            for k, peer in enumerate(_peers()):
                src, dst, _ = _exchange_windows(scatter, src_refs[p], land_refs[p], my_block, _block_index(*peer))
                pltpu.make_async_remote_copy(src_ref=src, dst_ref=dst, send_sem=send_sems.at[7 * p + k],
                                             recv_sem=recv_sems.at[7 * p + k], device_id=peer, device_id_type=MESH).start()
        token[...] = jnp.zeros_like(token)

    hbm = lambda a: pltpu.with_memory_space_constraint(a, pltpu.HBM)
    outs = pl.pallas_call(
        functools.partial(body), name=name,
        in_specs=[HBM] * (2 * np_) + [ANY],
        out_specs=[SEM, SEM, SEM] + [HBM] * (2 * np_) + [pl.BlockSpec(memory_space=pltpu.VMEM)],
        out_shape=[pltpu.SemaphoreType.DMA((7 * np_,)), pltpu.SemaphoreType.DMA((7 * np_,)), pltpu.SemaphoreType.DMA((np_,))]
        + [pltpu.HBM(a.shape, a.dtype) for a in list(srcs) + list(lands)] + [jax.ShapeDtypeStruct((8, LANES), F32)],
        input_output_aliases={i: 3 + i for i in range(2 * np_)},
        compiler_params=pltpu.CompilerParams(has_side_effects=pltpu.SideEffectType.DATAFLOW_SIDE_EFFECTING),
    )(*[hbm(a) for a in srcs], *[hbm(a) for a in lands], after)
    return dict(sems=outs[:3], srcs=outs[3:3 + np_], lands=outs[3 + np_:3 + 2 * np_], token=outs[-1], scatter=scatter)


def exchange_wait(started, after, name):
    afters = tuple(after) if isinstance(after, (tuple, list)) else (after,)
    srcs, lands = started["srcs"], started["lands"]
    scatter = started["scatter"]
    np_ = len(srcs)

    def body(*refs):
        src_refs, land_refs = refs[:np_], refs[np_:2 * np_]
        send_sems, recv_sems, own_sems = refs[2 * np_:2 * np_ + 3]
        my_block = _block_index(*_my_place())
        for p in range(np_):
            _own_copy(scatter, src_refs[p], land_refs[p], my_block, own_sems.at[p]).wait()
            for k, peer in enumerate(_peers()):
                src, dst, arrival = _exchange_windows(scatter, src_refs[p], land_refs[p], my_block, _block_index(*peer))
                pltpu.make_async_remote_copy(src_ref=src, dst_ref=dst, send_sem=send_sems.at[7 * p + k],
                                             recv_sem=recv_sems.at[7 * p + k], device_id=peer, device_id_type=MESH).wait_send()
                pltpu.make_async_remote_copy(src_ref=src, dst_ref=arrival, send_sem=send_sems.at[7 * p + k],
                                             recv_sem=recv_sems.at[7 * p + k], device_id=peer, device_id_type=MESH).wait_recv()

    outs = pl.pallas_call(
        functools.partial(body), name=name,
        in_specs=[HBM] * (2 * np_) + [SEM, SEM, SEM] + [ANY] * len(afters),
        out_specs=[HBM] * (2 * np_),
        out_shape=[pltpu.HBM(a.shape, a.dtype) for a in list(srcs) + list(lands)],
        input_output_aliases={i: i for i in range(2 * np_)},
        compiler_params=pltpu.CompilerParams(has_side_effects=pltpu.SideEffectType.DATAFLOW_SIDE_EFFECTING),
    )(*srcs, *lands, *started["sems"], *afters)
    return list(outs[np_:])


def _gather_zone(shard):
    return lax.empty((N_DEV * shard.shape[0], shard.shape[1]), shard.dtype)


def _scatter_zone(full):
    return lax.empty((N_DEV, full.shape[0] // N_DEV, full.shape[1]), full.dtype)


def allreduce_small(pack):
    r, c = pack.shape

    def body(pack_ref, out_ref, gathered, send_sems, recv_sems):
        me = _my_place()
        my_block = _block_index(*me)
        peers = _peers()

        def copy(k, slot, to):
            return pltpu.make_async_remote_copy(
                src_ref=pack_ref, dst_ref=gathered.at[slot], send_sem=send_sems.at[k], recv_sem=recv_sems.at[k],
                device_id=to, device_id_type=MESH)

        sends = [copy(k, my_block, peer) for k, peer in enumerate(peers)]
        for cp in sends:
            cp.start()
        gathered[my_block] = pack_ref[...]
        for k, peer in enumerate(peers):
            copy(k, _block_index(*peer), peer).wait_recv()
        for cp in sends:
            cp.wait_send()
        total = gathered[0]
        for j in range(1, N_DEV):
            total = total + gathered[j]
        out_ref[...] = total

    return pl.pallas_call(
        functools.partial(body), name="allreduce_small",
        in_specs=[pl.BlockSpec(memory_space=pltpu.VMEM)], out_specs=pl.BlockSpec(memory_space=pltpu.VMEM),
        out_shape=jax.ShapeDtypeStruct((r, c), F32),
        scratch_shapes=[pltpu.VMEM((N_DEV, r, c), F32), pltpu.SemaphoreType.DMA((7,)), pltpu.SemaphoreType.DMA((7,))],
    )(pack)


def adamw(w, m, v, parts, layer=0, prev=None):
    nl, r, c = w.shape
    n = parts.shape[0]
    br = 256 if r % 256 == 0 else r
    blk = pl.BlockSpec((None, br, c), lambda i: (layer, i, 0))
    n_prev = 0 if prev is None else 4

    def body(w_ref, m_ref, v_ref, p_ref, *rest):
        g_ref, d_ref, nm_ref, nv_ref = rest[n_prev:]
        g = p_ref[0].astype(F32)
        for j in range(1, n):
            g = g + p_ref[j].astype(F32)
        nm = ADAM_B1 * m_ref[...] + (1.0 - ADAM_B1) * g
        nv = ADAM_B2 * v_ref[...] + (1.0 - ADAM_B2) * (g * g)
        m_hat = nm / (1.0 - ADAM_B1 ** ADAM_STEP)
        v_hat = nv / (1.0 - ADAM_B2 ** ADAM_STEP)
        g_ref[...] = g
        d_ref[...] = -ADAM_LR * (m_hat / (jnp.sqrt(v_hat) + ADAM_EPS) + ADAM_WD * w_ref[...])
        nm_ref[...] = nm
        nv_ref[...] = nv

    return pl.pallas_call(
        functools.partial(body), name="adamw", grid=(r // br,),
        in_specs=[blk] * 3 + [pl.BlockSpec((n, br, c), lambda i: (0, i, 0))] + [ANY] * n_prev,
        out_specs=[blk] * 4, out_shape=[jax.ShapeDtypeStruct((nl, r, c), F32)] * 4,
        input_output_aliases={4 + k: k for k in range(n_prev)},
        compiler_params=_cparams(("parallel",)),
    )(w, m, v, parts, *(prev or ()))


def adamw_layers(w, m, v, layer_parts):
    outs = None
    for layer, parts in enumerate(layer_parts):
        outs = adamw(w, m, v, parts, layer, outs)
    return outs


def _pair_heads(a, axis, width=HEAD_DIM):
    shp = a.shape
    a = a.reshape(shp[:axis] + (2, 2, GQA, width) + shp[axis + 1:])
    return jnp.swapaxes(a, axis + 1, axis + 2).reshape(shp)


def _unpair_heads(a, axis, width=HEAD_DIM):
    shp = a.shape
    a = a.reshape(shp[:axis] + (2, GQA, 2, width) + shp[axis + 1:])
    return jnp.swapaxes(a, axis + 1, axis + 2).reshape(shp)


def _pad_rows(a, rows=8):
    return jnp.pad(a, ((0, rows - a.shape[0]), (0, 0)))


def kernel(x, p, mix_pre_g, mix_post_g, ffn_pre_g, ffn_post_g, pool_w, pool_scale, kv_norm_g, w_k, w_v, w_q, w_o, sinks, w_ff_gate, w_ff_up, w_ff_down, ple_norm_g, w_ple_gate, w_ple_proj, loss_target, m_mix_pre_g, m_mix_post_g, m_ffn_pre_g, m_ffn_post_g, m_pool_w, m_pool_scale, m_kv_norm_g, m_w_k, m_w_v, m_w_q, m_w_o, m_sinks, m_w_ff_gate, m_w_ff_up, m_w_ff_down, m_ple_norm_g, m_w_ple_gate, m_w_ple_proj, v_mix_pre_g, v_mix_post_g, v_ffn_pre_g, v_ffn_post_g, v_pool_w, v_pool_scale, v_kv_norm_g, v_w_k, v_w_v, v_w_q, v_w_o, v_sinks, v_w_ff_gate, v_w_ff_up, v_w_ff_down, v_ple_norm_g, v_w_ple_gate, v_w_ple_proj):
    depth = w_ff_gate.shape[0]
    n_a = pool_w.shape[0]
    t, d = x.shape[1], x.shape[2]
    h = x[0]
    tgt = loss_target[0]
    p_all = p.reshape(depth * t, p.shape[-1])
    my_block = _block_index(*_my_place())
    row = lambda g, i: g[i][None, :]
    bf = lambda a: a.astype(BF16)

    full, gathers = [None] * depth, {}
    start_tokens = jnp.zeros((), F32)
    for i in range(depth):
        shards = [bf(w_ff_gate[i].T), bf(w_ff_up[i].T), bf(w_ff_down[i]), bf(w_ple_gate[i]), bf(w_ple_proj[i].T)]
        if i == 0:
            pool0, scale_full = allgather_pieces([bf(pool_w[0].reshape(-1, POOL_GROUP)), _pad_rows(pool_scale)],
                                                 "allgather_pool0")
            order = pool0
        elif i < n_a:
            shards.append(bf(pool_w[i].reshape(-1, POOL_GROUP)))
        else:
            shards += [bf(_pair_heads(w_q[i - n_a], 1)), bf(w_o[i - n_a])]
            if i == n_a:
                shards.append(bf(jnp.concatenate([w_k, w_v], axis=1)))
        gathers[i] = exchange_start(shards, [_gather_zone(s) for s in shards], order, False, f"allgather_start_l{i}")
        order = gathers[i]["token"]
        start_tokens = start_tokens + order[0, 0]
    scale_full = scale_full.reshape(N_DEV, 8, -1)[:, :n_a].transpose(1, 0, 2).reshape(n_a, 1, d)

    cos, sin = _rope_tables(t, start_tokens)
    sink_b = [jnp.broadcast_to(_pair_heads(sinks[j][:, None], 0, 1), (N_HEADS, LANES)) for j in range(depth - n_a)]
    pool_full, wo_full = {}, {}

    saved = []
    kv = hk = None
    for i in range(depth):
        if i > 0:
            full[i] = exchange_wait(gathers[i], h, f"allgather_wait_l{i}")
        s = {"h0": h}
        if i < n_a:
            pool_full[i] = ((pool0 if i == 0 else full[i][5]).reshape(N_DEV, len(POOL_WINDOWS), -1, POOL_GROUP)
                            .transpose(1, 0, 2, 3).reshape(len(POOL_WINDOWS), POOL_GROUP, POOL_GROUP))
            gpre = row(mix_pre_g, i) + start_tokens if i == 0 else row(mix_pre_g, i)
            h1, a = pool_mix_fwd(h, gpre, pool_full[i], scale_full[i], row(mix_post_g, i), row(ffn_pre_g, i))
            if i == 0:
                full[0] = exchange_wait(gathers[0], (h1, cos, sin), "allgather_wait_l0")
        else:
            j = i - n_a
            wo_full[i] = _pair_heads(full[i][6], 0)
            if i == n_a:
                hk, kv = proj_rope_fwd(h, kv_norm_g[None, :], full[i][7], cos, sin, N_KV_HEADS * HEAD_DIM, "kv_proj_fwd")
            hn, q = proj_rope_fwd(h, row(mix_pre_g, i), full[i][5], cos, sin, d, "q_proj_fwd")
            attn, m, h1, a = swa_fwd(q, kv, sink_b[j], wo_full[i], h, row(mix_post_g, i), row(ffn_pre_g, i))
            s.update(hn=hn, q=q, attn=attn, m=m)
        wg_t, wu_t, wd, wpg, wpp_t = full[i][:5]
        f, gte, up, hdn = ffn_fwd(a, wg_t, wu_t, wd)
        s.update(h1=h1, a=a, f=f, gte=gte, up=up, hdn=hdn)
        if i < depth - 1:
            h = post_ple_fwd(h1, f, p_all, i, row(ffn_post_g, i), row(ple_norm_g, i), wpg, wpp_t)
        saved.append(s)

    g_mix_pre, g_mix_post, g_ffn_pre, g_ffn_post, g_ple = ([None] * depth for _ in range(5))
    g_kv = g_sinks = None
    g_scale = [None] * n_a
    landing, scatters = [None] * depth, {}
    dkv_sum = []
    scatter_token = jnp.zeros((), F32)
    for i in reversed(range(depth)):
        s = saved[i]
        wg_t, wu_t, wd, wpg, wpp_t = full[i][:5]
        last = i == depth - 1
        dh2, df, ub, dzb, dppb, gacc = post_ple_bwd(tgt if last else dh, s["h1"], s["f"], p_all, i,
                                                    row(ffn_post_g, i) + scatter_token, row(ple_norm_g, i), wpg, wpp_t,
                                                    from_target=last)
        g_ple[i], g_ffn_post[i] = gacc[0], gacc[1]
        if last:
            loss_row = gacc[2][None, :]
        da, dgte, dup = ffn_bwd_act(df, s["gte"], s["up"], wg_t, wu_t, wd)
        grads = [xty(dgte, s["a"]), xty(dup, s["a"]), xty(s["hdn"], df), xty(ub, dzb), xty(dppb, p_all, i)]
        early = exchange_start(grads, [_scatter_zone(g) for g in grads], dh2, True, f"reduce_scatter_start_l{i}a")
        early_token = early["token"][0, 0]
        if i < n_a:
            dh, dpw, gacc = pool_mix_bwd(s["h0"], dh2, da, row(mix_pre_g, i) + early_token, pool_full[i], scale_full[i],
                                         row(mix_post_g, i), row(ffn_pre_g, i))
            g_mix_pre[i], g_mix_post[i], g_ffn_pre[i], g_scale[i] = gacc[0], gacc[1], gacc[2], gacc[3]
            dpw = dpw.reshape(len(POOL_WINDOWS), N_DEV, -1, POOL_GROUP).transpose(1, 0, 2, 3)
            grads = [bf(dpw.reshape(-1, POOL_GROUP))]
        else:
            j = i - n_a
            dh1, dmb, dattn, gacc = oproj_post_bwd(dh2, da, s["h1"], s["m"], wo_full[i], row(mix_post_g, i) + early_token,
                                                   row(ffn_pre_g, i))
            g_mix_post[i], g_ffn_pre[i] = gacc[0], gacc[1]
            dq, dkv, dsink = swa_bwd(s["q"], kv, dattn, sink_b[j])
            dkv_sum.append(dkv)
            g_sinks = [_unpair_heads(dsink[:, 0:1], 0, 1)[:, 0]] + (g_sinks or [])
            branches = [(row(mix_pre_g, i), full[i][5], d, [dq])]
            if i == n_a:
                branches.append((kv_norm_g[None, :], full[i][7], N_KV_HEADS * HEAD_DIM, dkv_sum))
            outs = proj_rope_bwd(dh1, s["h0"], cos, sin, branches, f"proj_bwd_l{i}")
            dh, gacc = outs[0], outs[-1]
            g_mix_pre[i] = gacc[0]
            grads = [xty(s["hn"], outs[1]), _unpair_heads(xty(s["attn"], dmb), 0)]
            if i == n_a:
                g_kv = gacc[1]
                grads.append(xty(hk, outs[2]))
        late = exchange_start(grads, [_scatter_zone(g) for g in grads], dh, True, f"reduce_scatter_start_l{i}b")
        scatter_token = late["token"][0, 0]
        scatters[i] = (early, late)
    grad_x = dh[None]
    after = dh
    for i in reversed(range(depth)):
        landing[i] = (exchange_wait(scatters[i][0], after, f"reduce_scatter_wait_l{i}a")
                      + exchange_wait(scatters[i][1], after, f"reduce_scatter_wait_l{i}b"))
        after = landing[i][0]

    sink_row = jnp.pad(jnp.concatenate(g_sinks)[None, :], ((0, 0), (0, d - sinks.size)))
    stack = lambda rows_: _pad_rows(jnp.stack(rows_))
    pack = jnp.concatenate([stack(g_mix_pre), stack(g_mix_post), stack(g_ffn_pre), stack(g_ffn_post), stack(g_ple),
                            _pad_rows(g_kv[None]), stack(g_scale), _pad_rows(sink_row), _pad_rows(loss_row)], axis=0)
    tot = allreduce_small(pack)
    sec = lambda k, n: tot[8 * k:8 * k + n]
    loss = jnp.sum(tot[64])
    small = {
        "mix_pre_g": sec(0, depth), "mix_post_g": sec(1, depth), "ffn_pre_g": sec(2, depth),
        "ffn_post_g": sec(3, depth), "ple_norm_g": sec(4, depth), "kv_norm_g": tot[40],
        "pool_scale": lax.dynamic_slice_in_dim(sec(6, n_a), my_block * pool_scale.shape[1], pool_scale.shape[1], axis=1),
        "sinks": tot[56, :sinks.size].reshape(sinks.shape),
    }

    weights = dict(mix_pre_g=mix_pre_g, mix_post_g=mix_post_g, ffn_pre_g=ffn_pre_g, ffn_post_g=ffn_post_g, pool_w=pool_w, pool_scale=pool_scale, kv_norm_g=kv_norm_g, w_k=w_k, w_v=w_v, w_q=w_q, w_o=w_o, sinks=sinks, w_ff_gate=w_ff_gate, w_ff_up=w_ff_up, w_ff_down=w_ff_down, ple_norm_g=ple_norm_g, w_ple_gate=w_ple_gate, w_ple_proj=w_ple_proj)
    mom1 = dict(mix_pre_g=m_mix_pre_g, mix_post_g=m_mix_post_g, ffn_pre_g=m_ffn_pre_g, ffn_post_g=m_ffn_post_g, pool_w=m_pool_w, pool_scale=m_pool_scale, kv_norm_g=m_kv_norm_g, w_k=m_w_k, w_v=m_w_v, w_q=m_w_q, w_o=m_w_o, sinks=m_sinks, w_ff_gate=m_w_ff_gate, w_ff_up=m_w_ff_up, w_ff_down=m_w_ff_down, ple_norm_g=m_ple_norm_g, w_ple_gate=m_w_ple_gate, w_ple_proj=m_w_ple_proj)
    mom2 = dict(mix_pre_g=v_mix_pre_g, mix_post_g=v_mix_post_g, ffn_pre_g=v_ffn_pre_g, ffn_post_g=v_ffn_post_g, pool_w=v_pool_w, pool_scale=v_pool_scale, kv_norm_g=v_kv_norm_g, w_k=v_w_k, w_v=v_w_v, w_q=v_w_q, w_o=v_w_o, sinks=v_sinks, w_ff_gate=v_w_ff_gate, w_ff_up=v_w_ff_up, w_ff_down=v_w_ff_down, ple_norm_g=v_ple_norm_g, w_ple_gate=v_w_ple_gate, w_ple_proj=v_w_ple_proj)

    swap = lambda a: jnp.swapaxes(a, 1, 2)
    same = lambda a: a
    att = range(n_a, depth)
    plan = {
        "w_ff_gate": (swap, swap, [landing[i][0] for i in range(depth)]),
        "w_ff_up": (swap, swap, [landing[i][1] for i in range(depth)]),
        "w_ff_down": (same, same, [landing[i][2] for i in range(depth)]),
        "w_ple_gate": (same, same, [landing[i][3] for i in range(depth)]),
        "w_ple_proj": (swap, swap, [landing[i][4] for i in range(depth)]),
        "pool_w": (lambda a: a.reshape(n_a, -1, POOL_GROUP), lambda a: a.reshape(pool_w.shape),
                   [landing[i][5] for i in range(n_a)]),
        "w_q": (lambda a: _pair_heads(a, 2), lambda a: _unpair_heads(a, 2), [landing[i][5] for i in att]),
        "w_o": (same, same, [landing[i][6] for i in att]),
    }
    for nme, g in small.items():
        w = weights[nme]
        plan[nme] = ((lambda a: a.reshape((1, -1, a.shape[-1]))), (lambda a, shp=w.shape: a.reshape(shp)),
                     [g.reshape((1, -1, w.shape[-1]))])
    results = {}
    for nme, (view, unview, layer_parts) in plan.items():
        outs = adamw_layers(view(weights[nme]), view(mom1[nme]), view(mom2[nme]), layer_parts)
        results[nme] = [unview(o) for o in outs]
    kv_cat = lambda ws: jnp.concatenate([ws["w_k"], ws["w_v"]], axis=1)[None]
    outs = adamw_layers(kv_cat(weights), kv_cat(mom1), kv_cat(mom2), [landing[n_a][7]])
    results["w_k"] = [o[0, :, :w_k.shape[1]] for o in outs]
    results["w_v"] = [o[0, :, w_k.shape[1]:] for o in outs]

    order = ["mix_pre_g", "mix_post_g", "ffn_pre_g", "ffn_post_g", "pool_w", "pool_scale", "kv_norm_g", "w_k", "w_v",
             "w_q", "w_o", "sinks", "w_ff_gate", "w_ff_up", "w_ff_down", "ple_norm_g", "w_ple_gate", "w_ple_proj"]
    g_out, d_out, m_out, v_out = ([results[nme][k] for nme in order] for k in range(4))
    return (loss, grad_x, *g_out, *d_out, *m_out, *v_out)
```

```python
import functools

import jax
import jax.numpy as jnp
from jax import lax
from jax.experimental import pallas as pl
from jax.experimental.pallas import tpu as pltpu

F32 = jnp.float32
BF16 = jnp.bfloat16

N_DEV = 8
HEAD_DIM = 64
N_HEADS = 16
N_KV_HEADS = 4
GQA = N_HEADS // N_KV_HEADS
BLOCK = 128
POOL_WINDOWS = (2, 4, 8, 16)
POOL_GROUP = 256
HALO = 16
ROPE_THETA = 10000.0
RMS_EPS = 1e-6
NEG_INF = -1e30
LANES = 128
ATTN_SUB = 8
XTY_ROWS = 2048
FFN_CHUNK = 768
VMEM_LIMIT = 56 * 1024 * 1024

ADAM_LR = 0.001
ADAM_B1 = 0.9
ADAM_B2 = 0.999
ADAM_EPS = 1e-08
ADAM_WD = 0.01
ADAM_STEP = 10

MESH = pl.DeviceIdType.MESH
ANY = pl.BlockSpec(memory_space=pl.ANY)

NT_DIMS = (((1,), (1,)), ((), ()))
TN_DIMS = (((0,), (0,)), ((), ()))


def _cparams(sem=None, vmem=None):
    kw = {}
    if sem is not None:
        kw["dimension_semantics"] = sem
    if vmem is not None:
        kw["vmem_limit_bytes"] = vmem
    return pltpu.CompilerParams(**kw)


def _rows(tm, n, first=0):
    return pl.BlockSpec((tm, n), lambda i: (i + first, 0))


def _rows_rev(tm, n, nt):
    return pl.BlockSpec((tm, n), lambda i: (nt - 1 - i, 0))


def _const(shape):
    nd = len(shape)
    return pl.BlockSpec(shape, lambda *_: (0,) * nd, pipeline_mode=pl.Buffered(1))


def _resident(shape):
    nd = len(shape)
    return pl.BlockSpec(shape, lambda *_: (0,) * nd)


def _tile_rows(t):
    return 512 if t % 512 == 0 else 128


def _dot(a, b):
    return jnp.dot(a, b, preferred_element_type=F32)


def _dot_nt(a, b):
    return lax.dot_general(a, b, NT_DIMS, preferred_element_type=F32)


def _dot_tn(a, b):
    return lax.dot_general(a, b, TN_DIMS, preferred_element_type=F32)


def _rms_r(x):
    return lax.rsqrt(jnp.mean(x * x, axis=-1, keepdims=True) + RMS_EPS)


def _rms_bwd(x, r, g, dy):
    gy = dy * g
    dx = r * gy - x * (r * r * r * jnp.mean(gy * x, axis=-1, keepdims=True))
    dg = jnp.sum(dy * (x * r), axis=0, keepdims=True)
    return dx, dg


def _sigmoid(x):
    return jax.nn.sigmoid(x)


def _rope_tables(t, zero_token):
    inv = 1.0 / (ROPE_THETA ** (jnp.arange(0, HEAD_DIM, 2, dtype=F32) / HEAD_DIM))
    ang = (jnp.arange(t, dtype=F32) + zero_token)[:, None] * jnp.tile(inv, 2 * LANES // HEAD_DIM)[None, :]
    sign = jnp.tile(jnp.repeat(jnp.array([-1.0, 1.0], F32), HEAD_DIM // 2), LANES // HEAD_DIM)
    return jnp.cos(ang), jnp.sin(ang) * sign[None, :]


def _swap_halves(x):
    n = x.shape[1]
    lane = lax.broadcasted_iota(jnp.int32, x.shape, 1)
    first = (lane % HEAD_DIM) < (HEAD_DIM // 2)
    return jnp.where(first, pltpu.roll(x, n - HEAD_DIM // 2, 1), pltpu.roll(x, HEAD_DIM // 2, 1))


def _rope(x, cos, sin):
    reps = x.shape[1] // LANES
    return x * jnp.tile(cos, (1, reps)) + _swap_halves(x) * jnp.tile(sin, (1, reps))


def _unrope(dy, cos, sin):
    reps = dy.shape[1] // LANES
    return dy * jnp.tile(cos, (1, reps)) + _swap_halves(dy * jnp.tile(sin, (1, reps)))


def _acc_init(acc_ref):
    @pl.when(pl.program_id(0) == 0)
    def _():
        acc_ref[...] = jnp.zeros_like(acc_ref)


def _window_sums(ext, tm, forward):
    n = tm + HALO
    out = []
    for g, w in enumerate(POOL_WINDOWS):
        s = ext[:, g * POOL_GROUP:(g + 1) * POOL_GROUP]
        k = 1
        while k < w:
            s = s + pltpu.roll(s, k if forward else n - k, 0)
            k *= 2
        out.append(s[HALO:, :] if forward else s[:tm, :])
    return out


def _pool_inv_counts(tile, tm):
    t = tile * tm + lax.broadcasted_iota(jnp.int32, (tm, 1), 0)
    return [1.0 / jnp.minimum(t + 1, w).astype(F32) for w in POOL_WINDOWS]


def _pool_mix(hn, ext, inv_cnts, pw_ref, scale, tm):
    sums = _window_sums(ext, tm, True)
    pooled, ys = [], []
    for g in range(len(POOL_WINDOWS)):
        pg = (sums[g] * inv_cnts[g] - hn[:, g * POOL_GROUP:(g + 1) * POOL_GROUP]).astype(BF16)
        pooled.append(pg)
        ys.append(_dot(pg, pw_ref[g]))
    y = jnp.concatenate(ys, axis=1)
    return pooled, y, y * scale


def pool_mix_fwd(h0, gpre, pool_w, scale, gpost, gffn):
    t, d = h0.shape
    tm = _tile_rows(t)

    def body(h_ref, gpre_ref, pw_ref, scale_ref, gpost_ref, gffn_ref, h1_ref, a_ref, carry):
        i = pl.program_id(0)

        @pl.when(i == 0)
        def _():
            carry[...] = jnp.zeros_like(carry)

        x = h_ref[...]
        hn = x * _rms_r(x) * gpre_ref[...]
        ext = jnp.concatenate([carry[...], hn], axis=0)
        carry[...] = hn[tm - HALO:, :]
        _, _, m = _pool_mix(hn, ext, _pool_inv_counts(i, tm), pw_ref, scale_ref[...], tm)
        h1 = x + m * _rms_r(m) * gpost_ref[...]
        h1_ref[...] = h1
        a_ref[...] = (h1 * _rms_r(h1) * gffn_ref[...]).astype(BF16)

    return pl.pallas_call(
        functools.partial(body), name="pool_mix_fwd", grid=(t // tm,),
        in_specs=[_rows(tm, d), _const((1, d)), _const(pool_w.shape), _const((1, d)), _const((1, d)), _const((1, d))],
        out_specs=[_rows(tm, d), _rows(tm, d)],
        out_shape=[jax.ShapeDtypeStruct((t, d), F32), jax.ShapeDtypeStruct((t, d), BF16)],
        scratch_shapes=[pltpu.VMEM((HALO, d), F32)],
        compiler_params=_cparams(("arbitrary",), VMEM_LIMIT),
    )(h0, gpre, pool_w, scale, gpost, gffn)


def pool_mix_bwd(h0, dh2, da, gpre, pool_w, scale, gpost, gffn):
    t, d = h0.shape
    tm = _tile_rows(t)
    nt = t // tm
    hb = tm // HALO

    def body(h_ref, halo_ref, dh2_ref, da_ref, gpre_ref, pw_ref, scale_ref, gpost_ref, gffn_ref,
             dh0_ref, dpw_ref, gacc_ref, carry):
        i = pl.program_id(0)
        tile = nt - 1 - i
        _acc_init(gacc_ref)
        _acc_init(dpw_ref)

        @pl.when(i == 0)
        def _():
            carry[...] = jnp.zeros_like(carry)

        x = h_ref[...]
        gpre_v, scale_v, gpost_v, gffn_v = gpre_ref[...], scale_ref[...], gpost_ref[...], gffn_ref[...]
        r0 = _rms_r(x)
        hn = x * r0 * gpre_v
        xh = halo_ref[...]
        hn_halo = jnp.where(tile > 0, xh * _rms_r(xh) * gpre_v, 0.0)
        ext = jnp.concatenate([hn_halo, hn], axis=0)
        inv_cnts = _pool_inv_counts(tile, tm)
        pooled, y, m = _pool_mix(hn, ext, inv_cnts, pw_ref, scale_v, tm)
        rm = _rms_r(m)
        h1 = x + m * rm * gpost_v
        dh1_n, dgffn = _rms_bwd(h1, _rms_r(h1), gffn_v, da_ref[...].astype(F32))
        dh1 = dh2_ref[...] + dh1_n
        dm, dgpost = _rms_bwd(m, rm, gpost_v, dh1)
        dscale = jnp.sum(dm * y, axis=0, keepdims=True)
        dy = (dm * scale_v).astype(BF16)
        dpn = []
        for g in range(len(POOL_WINDOWS)):
            dyg = dy[:, g * POOL_GROUP:(g + 1) * POOL_GROUP]
            dpw_ref[g] += _dot_tn(pooled[g], dyg)
            dpn.append(_dot_nt(dyg, pw_ref[g]))
        dpooled = jnp.concatenate(dpn, axis=1)
        dpc = jnp.concatenate([dpn[g] * inv_cnts[g] for g in range(len(POOL_WINDOWS))], axis=1)
        ext2 = jnp.concatenate([dpc, carry[...]], axis=0)
        carry[...] = dpc[:HALO, :]
        dhn = jnp.concatenate(_window_sums(ext2, tm, False), axis=1) - dpooled
        dh0_n, dgpre = _rms_bwd(x, r0, gpre_v, dhn)
        dh0_ref[...] = dh1 + dh0_n
        gacc_ref[0:1, :] += dgpre
        gacc_ref[1:2, :] += dgpost
        gacc_ref[2:3, :] += dgffn
        gacc_ref[3:4, :] += dscale

    return pl.pallas_call(
        functools.partial(body), name="pool_mix_bwd", grid=(nt,),
        in_specs=[_rows_rev(tm, d, nt),
                  pl.BlockSpec((HALO, d), lambda i: (jnp.maximum((nt - 1 - i) * hb - 1, 0), 0)),
                  _rows_rev(tm, d, nt), _rows_rev(tm, d, nt),
                  _const((1, d)), _const(pool_w.shape), _const((1, d)), _const((1, d)), _const((1, d))],
        out_specs=[_rows_rev(tm, d, nt), _resident(pool_w.shape), _resident((8, d))],
        out_shape=[jax.ShapeDtypeStruct((t, d), F32), jax.ShapeDtypeStruct(pool_w.shape, F32),
                   jax.ShapeDtypeStruct((8, d), F32)],
        scratch_shapes=[pltpu.VMEM((HALO, d), F32)],
        compiler_params=_cparams(("arbitrary",), VMEM_LIMIT),
    )(h0, h0, dh2, da, gpre, pool_w, scale, gpost, gffn)


def _ffn_chunks(f):
    return [(c, min(c + FFN_CHUNK, f)) for c in range(0, f, FFN_CHUNK)]


def ffn_fwd(a, wg_t, wu_t, wd):
    t, d = a.shape
    f = wd.shape[0]
    tm = _tile_rows(t)

    def body(a_ref, wg_ref, wu_ref, wd_ref, f_ref, gte_ref, up_ref, hdn_ref):
        av = a_ref[...]
        acc = jnp.zeros((tm, d), F32)
        for c0, c1 in _ffn_chunks(f):
            gte = _dot_nt(av, wg_ref[c0:c1, :])
            up = _dot_nt(av, wu_ref[c0:c1, :])
            gte_ref[:, c0:c1] = gte.astype(BF16)
            up_ref[:, c0:c1] = up.astype(BF16)
            hdn = (gte * _sigmoid(gte) * up).astype(BF16)
            hdn_ref[:, c0:c1] = hdn
            acc = acc + _dot(hdn, wd_ref[c0:c1, :])
        f_ref[...] = acc.astype(BF16)

    return pl.pallas_call(
        functools.partial(body), name="ffn_fwd", grid=(t // tm,),
        in_specs=[_rows(tm, d), _const((f, d)), _const((f, d)), _const((f, d))],
        out_specs=[_rows(tm, d), _rows(tm, f), _rows(tm, f), _rows(tm, f)],
        out_shape=[jax.ShapeDtypeStruct((t, d), BF16)] + [jax.ShapeDtypeStruct((t, f), BF16)] * 3,
        compiler_params=_cparams(("parallel",), VMEM_LIMIT),
    )(a, wg_t, wu_t, wd)


def ffn_bwd_act(df, gte, up, wg_t, wu_t, wd):
    t, d = df.shape
    f = wd.shape[0]
    tm = _tile_rows(t)

    def body(df_ref, gte_ref, up_ref, wg_ref, wu_ref, wd_ref, da_ref, dgte_ref, dup_ref):
        dfv = df_ref[...]
        chunks = _ffn_chunks(f)
        half = chunks[len(chunks) // 2][0]
        acc = None
        for c0, c1 in chunks:
            g = gte_ref[:, c0:c1].astype(F32)
            u = up_ref[:, c0:c1].astype(F32)
            sg = _sigmoid(g)
            sl = g * sg
            dh = _dot_nt(dfv, wd_ref[c0:c1, :])
            dup_ref[:, c0:c1] = (dh * sl).astype(BF16)
            dgte_ref[:, c0:c1] = (dh * u * (sg * (1.0 + g * (1.0 - sg)))).astype(BF16)
            if c1 == half:
                acc = _dot(dgte_ref[:, :half], wg_ref[:half, :]) + _dot(dup_ref[:, :half], wu_ref[:half, :])
        da = acc + _dot(dgte_ref[:, half:], wg_ref[half:, :]) + _dot(dup_ref[:, half:], wu_ref[half:, :])
        da_ref[...] = da.astype(BF16)

    return pl.pallas_call(
        functools.partial(body), name="ffn_bwd_act", grid=(t // tm,),
        in_specs=[_rows(tm, d), _rows(tm, f), _rows(tm, f), _const((f, d)), _const((f, d)), _const((f, d))],
        out_specs=[_rows(tm, d), _rows(tm, f), _rows(tm, f)],
        out_shape=[jax.ShapeDtypeStruct((t, d), BF16)] + [jax.ShapeDtypeStruct((t, f), BF16)] * 2,
        compiler_params=_cparams(("parallel",), VMEM_LIMIT),
    )(df, gte, up, wg_t, wu_t, wd)


def xty(x, y, y_part=0):
    t, nx = x.shape
    ny = y.shape[1]
    tk = XTY_ROWS if t % XTY_ROWS == 0 else _tile_rows(t)
    bn = nx // 2 if nx > 1024 else nx
    nk = t // tk

    def body(x_ref, y_ref, o_ref, acc):
        k = pl.program_id(1)

        @pl.when(k == 0)
        def _():
            acc[...] = jnp.zeros_like(acc)

        acc[...] += _dot_tn(x_ref[...].astype(BF16), y_ref[...].astype(BF16))

        @pl.when(k == nk - 1)
        def _():
            o_ref[...] = acc[...].astype(BF16)

    return pl.pallas_call(
        functools.partial(body), name="xty", grid=(nx // bn, nk),
        in_specs=[pl.BlockSpec((tk, bn), lambda j, k: (k, j)),
                  pl.BlockSpec((tk, ny), lambda j, k: (k + y_part * nk, 0))],
        out_specs=pl.BlockSpec((bn, ny), lambda j, k: (j, 0)),
        out_shape=jax.ShapeDtypeStruct((nx, ny), BF16),
        scratch_shapes=[pltpu.VMEM((bn, ny), F32)],
        compiler_params=_cparams(("parallel", "arbitrary"), VMEM_LIMIT),
    )(x, y)


def _ple_fwd_tile(h1, f, p, gpost, gple, wpg_ref, wpp_ref):
    rf = _rms_r(f)
    h2 = h1 + f * rf * gpost
    r2 = _rms_r(h2)
    ub = (h2 * r2 * gple).astype(BF16)
    gate = _sigmoid(_dot(ub, wpg_ref[...]))
    pp = _dot_nt(p.astype(BF16), wpp_ref[...])
    return rf, h2, r2, ub, gate, pp


def post_ple_fwd(h1, f, p, layer, gpost, gple, wpg, wpp_t, cos=None, sin=None, proj=()):
    t, d = h1.shape
    pd = p.shape[1]
    tm = _tile_rows(t)
    nb = len(proj)

    def body(*refs):
        h1_ref, f_ref, p_ref, gpost_ref, gple_ref, wpg_ref, wpp_ref = refs[:7]
        pos = 9 if nb else 7
        out_ref = refs[pos + 2 * nb]
        _, h2, _, _, gate, pp = _ple_fwd_tile(h1_ref[...], f_ref[...].astype(F32), p_ref[...], gpost_ref[...],
                                              gple_ref[...], wpg_ref, wpp_ref)
        h3 = h2 + pp * gate
        out_ref[...] = h3
        if nb:
            r3 = _rms_r(h3)
        for b in range(nb):
            g_ref, w_ref = refs[pos + 2 * b], refs[pos + 2 * b + 1]
            hn_ref, y_ref = refs[pos + 2 * nb + 1 + 2 * b], refs[pos + 2 * nb + 2 + 2 * b]
            n_rope = proj[b][2]
            hn = (h3 * r3 * g_ref[...]).astype(BF16)
            hn_ref[...] = hn
            y = _dot(hn, w_ref[...])
            y_ref[:, :n_rope] = _rope(y[:, :n_rope], refs[7][...], refs[8][...]).astype(BF16)
            if n_rope < y.shape[1]:
                y_ref[:, n_rope:] = y[:, n_rope:].astype(BF16)

    in_specs = [_rows(tm, d), _rows(tm, d), _rows(tm, pd, layer * (t // tm)), _const((1, d)), _const((1, d)),
                _const(wpg.shape), _const(wpp_t.shape)]
    args = [h1, f, p, gpost, gple, wpg, wpp_t]
    out_specs, out_shape = [_rows(tm, d)], [jax.ShapeDtypeStruct((t, d), F32)]
    if nb:
        in_specs += [_rows(tm, LANES), _rows(tm, LANES)]
        args += [cos, sin]
    for gain, w, _ in proj:
        in_specs += [_const((1, d)), _const(w.shape)]
        args += [gain, w]
        out_specs += [_rows(tm, d), _rows(tm, w.shape[1])]
        out_shape += [jax.ShapeDtypeStruct((t, d), BF16), jax.ShapeDtypeStruct((t, w.shape[1]), BF16)]
    return pl.pallas_call(
        functools.partial(body), name="post_ple_proj_fwd" if nb else "post_ple_fwd", grid=(t // tm,),
        in_specs=in_specs, out_specs=out_specs, out_shape=out_shape,
        compiler_params=_cparams(("parallel",), VMEM_LIMIT),
    )(*args)


def post_ple_bwd(dh3, h1, f, p, layer, gpost, gple, wpg, wpp_t, from_target=False):
    t, d = h1.shape
    pd = p.shape[1]
    tm = _tile_rows(t)

    def body(dh3_ref, h1_ref, f_ref, p_ref, gpost_ref, gple_ref, wpg_ref, wpp_ref,
             dh2_ref, df_ref, u_ref, dz_ref, dpp_ref, gacc_ref):
        _acc_init(gacc_ref)
        gpost_v, gple_v = gpost_ref[...], gple_ref[...]
        nsub = 2 if tm % 16 == 0 else 1
        for sb in range(nsub):
            rows = slice(sb * (tm // nsub), (sb + 1) * (tm // nsub))
            fv = f_ref[rows, :].astype(F32)
            rf, h2, r2, ub, gate, pp = _ple_fwd_tile(h1_ref[rows, :], fv, p_ref[rows, :], gpost_v, gple_v, wpg_ref,
                                                     wpp_ref)
            if from_target:
                err = h2 + pp * gate - dh3_ref[rows, :]
                dh3v = err * (1.0 / d)
                gacc_ref[2:3, :] += jnp.sum(err * err, axis=0, keepdims=True) * (0.5 / d)
            else:
                dh3v = dh3_ref[rows, :]
            dpp_ref[rows, :] = (dh3v * gate).astype(BF16)
            dz = (dh3v * pp * gate * (1.0 - gate)).astype(BF16)
            dz_ref[rows, :] = dz
            u_ref[rows, :] = ub
            du = _dot_nt(dz, wpg_ref[...])
            dh2_n, dgple = _rms_bwd(h2, r2, gple_v, du)
            dh2 = dh3v + dh2_n
            df, dgpost = _rms_bwd(fv, rf, gpost_v, dh2)
            dh2_ref[rows, :] = dh2
            df_ref[rows, :] = df.astype(BF16)
            gacc_ref[0:1, :] += dgple
            gacc_ref[1:2, :] += dgpost

    return pl.pallas_call(
        functools.partial(body), name="post_ple_loss_bwd" if from_target else "post_ple_bwd", grid=(t // tm,),
        in_specs=[_rows(tm, d), _rows(tm, d), _rows(tm, d), _rows(tm, pd, layer * (t // tm)), _const((1, d)),
                  _const((1, d)), _const(wpg.shape), _const(wpp_t.shape)],
        out_specs=[_rows(tm, d)] * 5 + [_resident((8, d))],
        out_shape=[jax.ShapeDtypeStruct((t, d), F32)] + [jax.ShapeDtypeStruct((t, d), BF16)] * 4
        + [jax.ShapeDtypeStruct((8, d), F32)],
        compiler_params=_cparams(("arbitrary",), VMEM_LIMIT),
    )(dh3, h1, f, p, gpost, gple, wpg, wpp_t)


def proj_rope_bwd(dh1, h0, cos, sin, branches, name):
    t, d = h0.shape
    tm = _tile_rows(t)
    nb = len(branches)
    n_cot = [len(b[3]) for b in branches]

    def body(*refs):
        dh1_ref, h0_ref, cos_ref, sin_ref = refs[:4]
        pos = 4
        br_refs = []
        for b in range(nb):
            br_refs.append((refs[pos], refs[pos + 1], refs[pos + 2:pos + 2 + n_cot[b]]))
            pos += 2 + n_cot[b]
        dh0_ref = refs[pos]
        dpre_refs = refs[pos + 1:pos + 1 + nb]
        gacc_ref = refs[pos + 1 + nb]
        _acc_init(gacc_ref)
        x = h0_ref[...]
        r0 = _rms_r(x)
        dh = dh1_ref[...]
        for b in range(nb):
            g_ref, w_ref, cot_refs = br_refs[b]
            n_rope = branches[b][2]
            dy = cot_refs[0][...].astype(F32)
            for c_ref in cot_refs[1:]:
                dy = dy + c_ref[...].astype(F32)
            n = dy.shape[1]
            dpre_refs[b][:, :n_rope] = _unrope(dy[:, :n_rope], cos_ref[...], sin_ref[...]).astype(BF16)
            if n_rope < n:
                dpre_refs[b][:, n_rope:] = dy[:, n_rope:].astype(BF16)
            dhn = _dot_nt(dpre_refs[b][...], w_ref[...])
            dx, dg = _rms_bwd(x, r0, g_ref[...], dhn)
            dh = dh + dx
            gacc_ref[b:b + 1, :] += dg
        dh0_ref[...] = dh

    in_specs = [_rows(tm, d), _rows(tm, d), _rows(tm, LANES), _rows(tm, LANES)]
    args = [dh1, h0, cos, sin]
    out_specs = [_rows(tm, d)]
    out_shape = [jax.ShapeDtypeStruct((t, d), F32)]
    for gain, w, _, cots in branches:
        n = w.shape[1]
        in_specs += [_const((1, d)), _const(w.shape)] + [_rows(tm, n)] * len(cots)
        args += [gain, w] + list(cots)
        out_specs.append(_rows(tm, n))
        out_shape.append(jax.ShapeDtypeStruct((t, n), BF16))
    out_specs.append(_resident((8, d)))
    out_shape.append(jax.ShapeDtypeStruct((8, d), F32))
    return pl.pallas_call(
        functools.partial(body), name=name, grid=(t // tm,),
        in_specs=in_specs, out_specs=out_specs, out_shape=out_shape,
        compiler_params=_cparams(("arbitrary",), VMEM_LIMIT),
    )(*args)


def _tri():
    row = lax.broadcasted_iota(jnp.int32, (BLOCK, BLOCK), 0)
    col = lax.broadcasted_iota(jnp.int32, (BLOCK, BLOCK), 1)
    return col <= row


def _block_diag(x):
    lo = lax.broadcasted_iota(jnp.int32, x.shape, 1) < HEAD_DIM
    zero = jnp.zeros_like(x)
    return jnp.concatenate([jnp.where(lo, x, zero), jnp.where(lo, zero, x)], axis=0)


def _dense(x, tri):
    return (jnp.where(tri, x[:, BLOCK:2 * BLOCK], x[:, :BLOCK]),
            jnp.where(tri, x[:, 3 * BLOCK:], x[:, 2 * BLOCK:3 * BLOCK]))


def _banded(xa, xb, tri):
    zero = jnp.zeros_like(xa)
    return jnp.concatenate([jnp.where(tri, zero, xa), jnp.where(tri, xa, zero),
                            jnp.where(tri, zero, xb), jnp.where(tri, xb, zero)], axis=1).astype(BF16)


def _softmax_sink(s, sink):
    mx = jnp.maximum(jnp.max(s, axis=1, keepdims=True), sink)
    e = jnp.exp(s - mx)
    es = jnp.exp(sink - mx)
    inv = 1.0 / (jnp.sum(e, axis=1, keepdims=True) + es)
    return e * inv, es * inv


def _sink_column(sink_ref):
    return jnp.concatenate([jnp.broadcast_to(sink_ref[h:h + 1, 0:1], (BLOCK, 1)) for h in range(N_HEADS)], axis=0)


def _kv_block_diag(band, kvw):
    n_lt = kvw // LANES
    return ([_block_diag(band[:, lt * LANES:(lt + 1) * LANES]) for lt in range(n_lt)],
            [_block_diag(band[:, kvw + lt * LANES:kvw + (lt + 1) * LANES]) for lt in range(n_lt)])


def _all_probs(q_ref, r0, kbd, tri, n, sink_ref):
    dense = []
    for tq in range(N_HEADS // 2):
        s = _dot_nt(q_ref[r0:r0 + BLOCK, tq * LANES:(tq + 1) * LANES], kbd[tq // GQA])
        dense += list(_dense(s, tri))
    bias = jnp.where(jnp.logical_not(tri) & (n == 0), NEG_INF, 0.0)
    s_all = jnp.concatenate(dense, axis=0) * (HEAD_DIM ** -0.5) + jnp.concatenate([bias] * N_HEADS, axis=0)
    return _softmax_sink(s_all, _sink_column(sink_ref))


def _head_rows(x, tq):
    return x[2 * tq * BLOCK:(2 * tq + 1) * BLOCK], x[(2 * tq + 1) * BLOCK:(2 * tq + 2) * BLOCK]


def _attn_sub(t):
    return ATTN_SUB if t % (ATTN_SUB * BLOCK) == 0 else 1


def swa_fwd(q, kv, sink_b, w_o, h0, gpost, gffn):
    t, d = q.shape
    sub = _attn_sub(t)
    rows = sub * BLOCK
    kvw = N_KV_HEADS * HEAD_DIM

    def body(q_ref, kvc_ref, kvp_ref, sink_ref, w_ref, h0_ref, gpost_ref, gffn_ref, o_ref, m_ref, h1_ref, a_ref):
        i = pl.program_id(0)
        tri = _tri()
        ext = jnp.concatenate([kvp_ref[...], kvc_ref[...]], axis=0)
        for sb in range(sub):
            r0 = sb * BLOCK
            kbd, vbd = _kv_block_diag(ext[r0:r0 + 2 * BLOCK], kvw)
            p, _ = _all_probs(q_ref, r0, kbd, tri, i * sub + sb, sink_ref)
            for tq in range(N_HEADS // 2):
                pa, pb = _head_rows(p, tq)
                o_ref[r0:r0 + BLOCK, tq * LANES:(tq + 1) * LANES] = _dot(_banded(pa, pb, tri), vbd[tq // GQA]).astype(BF16)
            blk = slice(r0, r0 + BLOCK)
            m = _dot(o_ref[blk, :], w_ref[...])
            m_ref[blk, :] = m.astype(BF16)
            h1 = h0_ref[blk, :] + m * _rms_r(m) * gpost_ref[...]
            h1_ref[blk, :] = h1
            a_ref[blk, :] = (h1 * _rms_r(h1) * gffn_ref[...]).astype(BF16)

    return pl.pallas_call(
        functools.partial(body), name="swa_fwd", grid=(t // rows,),
        in_specs=[_rows(rows, d), _rows(rows, 2 * kvw),
                  pl.BlockSpec((BLOCK, 2 * kvw), lambda i: (jnp.maximum(i * sub - 1, 0), 0)), _const(sink_b.shape),
                  _const(w_o.shape), _rows(rows, d), _const((1, d)), _const((1, d))],
        out_specs=[_rows(rows, d)] * 4,
        out_shape=[jax.ShapeDtypeStruct((t, d), BF16), jax.ShapeDtypeStruct((t, d), BF16),
                   jax.ShapeDtypeStruct((t, d), F32), jax.ShapeDtypeStruct((t, d), BF16)],
        compiler_params=_cparams(("parallel",), VMEM_LIMIT),
    )(q, kv, kv, sink_b, w_o, h0, gpost, gffn)


def swa_bwd(q, kv, do, sink_b):
    t, d = q.shape
    sub = _attn_sub(t)
    nq = t // (sub * BLOCK)
    kvw = N_KV_HEADS * HEAD_DIM

    def body(q_ref, do_ref, kvc_ref, kvp_ref, sink_ref, dq_ref, dkv_ref, dsink_ref, carry):
        i = pl.program_id(0)
        step = nq - 1 - i
        _acc_init(dsink_ref)

        @pl.when(i == 0)
        def _():
            carry[...] = jnp.zeros_like(carry)

        tri = _tri()
        lo = lax.broadcasted_iota(jnp.int32, (2 * BLOCK, LANES), 1) < HEAD_DIM
        ext = jnp.concatenate([kvp_ref[...], kvc_ref[...]], axis=0)
        dkeys = [None] * (sub + 1)
        for sb in reversed(range(sub)):
            r0 = sb * BLOCK
            kbd, vbd = _kv_block_diag(ext[r0:r0 + 2 * BLOCK], kvw)
            p, ps = _all_probs(q_ref, r0, kbd, tri, step * sub + sb, sink_ref)
            dp = []
            for tq in range(N_HEADS // 2):
                dp += list(_dense(_dot_nt(do_ref[r0:r0 + BLOCK, tq * LANES:(tq + 1) * LANES], vbd[tq // GQA]), tri))
            dp = jnp.concatenate(dp, axis=0)
            delta = jnp.sum(p * dp, axis=1, keepdims=True)
            ds = p * (dp - delta) * (HEAD_DIM ** -0.5)
            dsk = ps * delta
            for h in range(N_HEADS):
                dsink_ref[h:h + 1, :] -= jnp.sum(dsk[h * BLOCK:(h + 1) * BLOCK], axis=0, keepdims=True)
            dkb = [jnp.zeros((4 * BLOCK, LANES), F32) for _ in kbd]
            dvb = [jnp.zeros((4 * BLOCK, LANES), F32) for _ in kbd]
            for tq in range(N_HEADS // 2):
                lt = tq // GQA
                cols = slice(tq * LANES, (tq + 1) * LANES)
                dsb = _banded(*_head_rows(ds, tq), tri)
                dq_ref[r0:r0 + BLOCK, cols] = _dot(dsb, kbd[lt]).astype(BF16)
                dkb[lt] = dkb[lt] + _dot_tn(dsb, q_ref[r0:r0 + BLOCK, cols])
                dvb[lt] = dvb[lt] + _dot_tn(_banded(*_head_rows(p, tq), tri), do_ref[r0:r0 + BLOCK, cols])
            dall = jnp.concatenate([jnp.where(lo, x[:2 * BLOCK], x[2 * BLOCK:]) for x in dkb + dvb], axis=1)
            dkeys[sb + 1] = dall[BLOCK:] if dkeys[sb + 1] is None else dkeys[sb + 1] + dall[BLOCK:]
            dkeys[sb] = dall[:BLOCK]
        for sb in range(sub):
            own = dkeys[sb + 1] + carry[...] if sb == sub - 1 else dkeys[sb + 1]
            dkv_ref[sb * BLOCK:(sb + 1) * BLOCK, :] = own
        carry[...] = dkeys[0]

    rev = lambda i: (nq - 1 - i, 0)
    return pl.pallas_call(
        functools.partial(body), name="swa_bwd", grid=(nq,),
        in_specs=[pl.BlockSpec((sub * BLOCK, d), rev), pl.BlockSpec((sub * BLOCK, d), rev),
                  pl.BlockSpec((sub * BLOCK, 2 * kvw), rev),
                  pl.BlockSpec((BLOCK, 2 * kvw), lambda i: (jnp.maximum((nq - 1 - i) * sub - 1, 0), 0)),
                  _const(sink_b.shape)],
        out_specs=[pl.BlockSpec((sub * BLOCK, d), rev), pl.BlockSpec((sub * BLOCK, 2 * kvw), rev),
                   _resident(sink_b.shape)],
        out_shape=[jax.ShapeDtypeStruct((t, d), BF16), jax.ShapeDtypeStruct((t, 2 * kvw), F32),
                   jax.ShapeDtypeStruct(sink_b.shape, F32)],
        scratch_shapes=[pltpu.VMEM((BLOCK, 2 * kvw), F32)],
        compiler_params=_cparams(("arbitrary",), VMEM_LIMIT),
    )(q, do, kv, kv, sink_b)


def oproj_post_bwd(dh2, da, h1, m, w_o, gpost, gffn):
    t, d = h1.shape
    tm = _tile_rows(t)

    def body(dh2_ref, da_ref, h1_ref, m_ref, w_ref, gpost_ref, gffn_ref, dh1_ref, dm_ref, dat_ref, gacc_ref):
        _acc_init(gacc_ref)
        h1v, mv = h1_ref[...], m_ref[...].astype(F32)
        dh1_n, dgffn = _rms_bwd(h1v, _rms_r(h1v), gffn_ref[...], da_ref[...].astype(F32))
        dh1 = dh2_ref[...] + dh1_n
        dm, dgpost = _rms_bwd(mv, _rms_r(mv), gpost_ref[...], dh1)
        dmb = dm.astype(BF16)
        dh1_ref[...] = dh1
        dm_ref[...] = dmb
        dat_ref[...] = _dot_nt(dmb, w_ref[...]).astype(BF16)
        gacc_ref[0:1, :] += dgpost
        gacc_ref[1:2, :] += dgffn

    return pl.pallas_call(
        functools.partial(body), name="oproj_post_bwd", grid=(t // tm,),
        in_specs=[_rows(tm, d)] * 4 + [_const(w_o.shape), _const((1, d)), _const((1, d))],
        out_specs=[_rows(tm, d)] * 3 + [_resident((8, d))],
        out_shape=[jax.ShapeDtypeStruct((t, d), F32), jax.ShapeDtypeStruct((t, d), BF16),
                   jax.ShapeDtypeStruct((t, d), BF16), jax.ShapeDtypeStruct((8, d), F32)],
        compiler_params=_cparams(("arbitrary",), VMEM_LIMIT),
    )(dh2, da, h1, m, w_o, gpost, gffn)


def _my_place():
    return lax.axis_index("x"), lax.axis_index("y"), lax.axis_index("c")


def _block_index(px, py, pc):
    return 4 * px + 2 * py + pc


def allgather_pieces(shards, name):
    np_ = len(shards)

    def body(*refs):
        in_refs, out_refs = refs[:np_], refs[np_:2 * np_]
        send_sems, recv_sems, local_sems = refs[2 * np_:]
        x, y, c = _my_place()
        me, sibling = (x, y, c), (x, y, 1 - c)
        chips = [(1 - x, y), (x, 1 - y), (1 - x, 1 - y)]

        def rows(p, place):
            r = in_refs[p].shape[0]
            return out_refs[p].at[pl.ds(_block_index(*place) * r, r), :]

        def copy(p, k, block, to, src=None):
            return pltpu.make_async_remote_copy(
                src_ref=rows(p, block) if src is None else src, dst_ref=rows(p, block),
                send_sem=send_sems.at[p, k], recv_sem=recv_sems.at[p, k], device_id=to, device_id_type=MESH)

        mine = [pltpu.make_async_copy(in_refs[p], rows(p, me), local_sems.at[p]) for p in range(np_)]
        first, passed = [], []
        for p in range(np_):
            mine[p].start()
            first.append(copy(p, 0, me, sibling, src=in_refs[p]))
            first += [copy(p, 1 + j, me, (*chip, c), src=in_refs[p]) for j, chip in enumerate(chips)]
        for cp in first:
            cp.start()
        for p in range(np_):
            for j, chip in enumerate(chips):
                copy(p, 1 + j, (*chip, c), me).wait_recv()
                fwd = copy(p, 4 + j, (*chip, c), sibling)
                fwd.start()
                passed.append(fwd)
        for p in range(np_):
            copy(p, 0, sibling, me).wait_recv()
            for j, chip in enumerate(chips):
                copy(p, 4 + j, (*chip, 1 - c), me).wait_recv()
        for cp in first + passed:
            cp.wait_send()
        for cp in mine:
            cp.wait()

    return pl.pallas_call(
        functools.partial(body), name=name,
        in_specs=[ANY] * np_, out_specs=[ANY] * np_,
        out_shape=[jax.ShapeDtypeStruct((N_DEV * s.shape[0], s.shape[1]), s.dtype) for s in shards],
        scratch_shapes=[pltpu.SemaphoreType.DMA((np_, 7)), pltpu.SemaphoreType.DMA((np_, 7)),
                        pltpu.SemaphoreType.DMA((np_,))],
    )(*shards)


def _peers():
    x, y, c = _my_place()
    flips = [(fx, fy, fc) for fx in (0, 1) for fy in (0, 1) for fc in (0, 1)][1:]
    return [(1 - x if fx else x, 1 - y if fy else y, 1 - c if fc else c) for fx, fy, fc in flips]


HBM = pl.BlockSpec(memory_space=pltpu.HBM)
SEM = pl.BlockSpec(memory_space=pltpu.SEMAPHORE)


def _exchange_windows(scatter, src_ref, land_ref, my_block, peer_block):
    if scatter:
        r = land_ref.shape[1]
        return src_ref.at[pl.ds(peer_block * r, r), :], land_ref.at[my_block], land_ref.at[peer_block]
    r = src_ref.shape[0]
    return src_ref, land_ref.at[pl.ds(my_block * r, r), :], land_ref.at[pl.ds(peer_block * r, r), :]


def _own_copy(scatter, src_ref, land_ref, my_block, sem):
    if scatter:
        r = land_ref.shape[1]
        return pltpu.make_async_copy(src_ref.at[pl.ds(my_block * r, r), :], land_ref.at[my_block], sem)
    r = src_ref.shape[0]
    return pltpu.make_async_copy(src_ref, land_ref.at[pl.ds(my_block * r, r), :], sem)


def exchange_start(srcs, lands, after, scatter, name):
    np_ = len(srcs)

    def body(*refs):
        src_refs, land_refs = refs[:np_], refs[np_:2 * np_]
        send_sems, recv_sems, own_sems = refs[2 * np_ + 1:2 * np_ + 4]
        token = refs[-1]
        my_block = _block_index(*_my_place())
        for p in range(np_):
            _own_copy(scatter, src_refs[p], land_refs[p], my_block, own_sems.at[p]).start()
            for k, peer in enumerate(_peers()):
                src, dst, _ = _exchange_windows(scatter, src_refs[p], land_refs[p], my_block, _block_index(*peer))
                pltpu.make_async_remote_copy(src_ref=src, dst_ref=dst, send_sem=send_sems.at[7 * p + k],
                                             recv_sem=recv_sems.at[7 * p + k], device_id=peer, device_id_type=MESH).start()
        token[...] = jnp.zeros_like(token)

    hbm = lambda a: pltpu.with_memory_space_constraint(a, pltpu.HBM)
    outs = pl.pallas_call(
        functools.partial(body), name=name,
        in_specs=[HBM] * (2 * np_) + [ANY],
        out_specs=[SEM, SEM, SEM] + [HBM] * (2 * np_) + [pl.BlockSpec(memory_space=pltpu.VMEM)],
        out_shape=[pltpu.SemaphoreType.DMA((7 * np_,)), pltpu.SemaphoreType.DMA((7 * np_,)), pltpu.SemaphoreType.DMA((np_,))]
        + [pltpu.HBM(a.shape, a.dtype) for a in list(srcs) + list(lands)] + [jax.ShapeDtypeStruct((8, LANES), F32)],
        input_output_aliases={i: 3 + i for i in range(2 * np_)},
        compiler_params=pltpu.CompilerParams(has_side_effects=pltpu.SideEffectType.DATAFLOW_SIDE_EFFECTING),
    )(*[hbm(a) for a in srcs], *[hbm(a) for a in lands], after)
    return dict(sems=outs[:3], srcs=outs[3:3 + np_], lands=outs[3 + np_:3 + 2 * np_], token=outs[-1], scatter=scatter)


def exchange_wait(started, after, name):
    afters = tuple(after) if isinstance(after, (tuple, list)) else (after,)
    srcs, lands = started["srcs"], started["lands"]
    scatter = started["scatter"]
    np_ = len(srcs)

    def body(*refs):
        src_refs, land_refs = refs[:np_], refs[np_:2 * np_]
        send_sems, recv_sems, own_sems = refs[2 * np_:2 * np_ + 3]
        my_block = _block_index(*_my_place())
        for p in range(np_):
            _own_copy(scatter, src_refs[p], land_refs[p], my_block, own_sems.at[p]).wait()
            for k, peer in enumerate(_peers()):
                src, dst, arrival = _exchange_windows(scatter, src_refs[p], land_refs[p], my_block, _block_index(*peer))
                pltpu.make_async_remote_copy(src_ref=src, dst_ref=dst, send_sem=send_sems.at[7 * p + k],
                                             recv_sem=recv_sems.at[7 * p + k], device_id=peer, device_id_type=MESH).wait_send()
                pltpu.make_async_remote_copy(src_ref=src, dst_ref=arrival, send_sem=send_sems.at[7 * p + k],
                                             recv_sem=recv_sems.at[7 * p + k], device_id=peer, device_id_type=MESH).wait_recv()

    outs = pl.pallas_call(
        functools.partial(body), name=name,
        in_specs=[HBM] * (2 * np_) + [SEM, SEM, SEM] + [ANY] * len(afters),
        out_specs=[HBM] * (2 * np_),
        out_shape=[pltpu.HBM(a.shape, a.dtype) for a in list(srcs) + list(lands)],
        input_output_aliases={i: i for i in range(2 * np_)},
        compiler_params=pltpu.CompilerParams(has_side_effects=pltpu.SideEffectType.DATAFLOW_SIDE_EFFECTING),
    )(*srcs, *lands, *started["sems"], *afters)
    return list(outs[np_:])


def _gather_zone(shard):
    return lax.empty((N_DEV * shard.shape[0], shard.shape[1]), shard.dtype)


def _scatter_zone(full):
    return lax.empty((N_DEV, full.shape[0] // N_DEV, full.shape[1]), full.dtype)


def allreduce_small(pack):
    r, c = pack.shape

    def body(pack_ref, out_ref, gathered, send_sems, recv_sems):
        me = _my_place()
        my_block = _block_index(*me)
        peers = _peers()

        def copy(k, slot, to):
            return pltpu.make_async_remote_copy(
                src_ref=pack_ref, dst_ref=gathered.at[slot], send_sem=send_sems.at[k], recv_sem=recv_sems.at[k],
                device_id=to, device_id_type=MESH)

        sends = [copy(k, my_block, peer) for k, peer in enumerate(peers)]
        for cp in sends:
            cp.start()
        gathered[my_block] = pack_ref[...]
        for k, peer in enumerate(peers):
            copy(k, _block_index(*peer), peer).wait_recv()
        for cp in sends:
            cp.wait_send()
        total = gathered[0]
        for j in range(1, N_DEV):
            total = total + gathered[j]
        out_ref[...] = total

    return pl.pallas_call(
        functools.partial(body), name="allreduce_small",
        in_specs=[pl.BlockSpec(memory_space=pltpu.VMEM)], out_specs=pl.BlockSpec(memory_space=pltpu.VMEM),
        out_shape=jax.ShapeDtypeStruct((r, c), F32),
        scratch_shapes=[pltpu.VMEM((N_DEV, r, c), F32), pltpu.SemaphoreType.DMA((7,)), pltpu.SemaphoreType.DMA((7,))],
    )(pack)


def adamw(w, m, v, parts, layer=0, prev=None):
    nl, r, c = w.shape
    n = parts.shape[0]
    br = 256 if r % 256 == 0 else r
    blk = pl.BlockSpec((None, br, c), lambda i: (layer, i, 0))
    n_prev = 0 if prev is None else 4

    def body(w_ref, m_ref, v_ref, p_ref, *rest):
        g_ref, d_ref, nm_ref, nv_ref = rest[n_prev:]
        g = p_ref[0].astype(F32)
        for j in range(1, n):
            g = g + p_ref[j].astype(F32)
        nm = ADAM_B1 * m_ref[...] + (1.0 - ADAM_B1) * g
        nv = ADAM_B2 * v_ref[...] + (1.0 - ADAM_B2) * (g * g)
        m_hat = nm / (1.0 - ADAM_B1 ** ADAM_STEP)
        v_hat = nv / (1.0 - ADAM_B2 ** ADAM_STEP)
        g_ref[...] = g
        d_ref[...] = -ADAM_LR * (m_hat / (jnp.sqrt(v_hat) + ADAM_EPS) + ADAM_WD * w_ref[...])
        nm_ref[...] = nm
        nv_ref[...] = nv

    return pl.pallas_call(
        functools.partial(body), name="adamw", grid=(r // br,),
        in_specs=[blk] * 3 + [pl.BlockSpec((n, br, c), lambda i: (0, i, 0))] + [ANY] * n_prev,
        out_specs=[blk] * 4, out_shape=[jax.ShapeDtypeStruct((nl, r, c), F32)] * 4,
        input_output_aliases={4 + k: k for k in range(n_prev)},
        compiler_params=_cparams(("parallel",)),
    )(w, m, v, parts, *(prev or ()))


def adamw_layers(w, m, v, layer_parts):
    outs = None
    for layer, parts in enumerate(layer_parts):
        outs = adamw(w, m, v, parts, layer, outs)
    return outs


def _pair_heads(a, axis, width=HEAD_DIM):
    shp = a.shape
    a = a.reshape(shp[:axis] + (2, 2, GQA, width) + shp[axis + 1:])
    return jnp.swapaxes(a, axis + 1, axis + 2).reshape(shp)


def _unpair_heads(a, axis, width=HEAD_DIM):
    shp = a.shape
    a = a.reshape(shp[:axis] + (2, GQA, 2, width) + shp[axis + 1:])
    return jnp.swapaxes(a, axis + 1, axis + 2).reshape(shp)


def _pad_rows(a, rows=8):
    return jnp.pad(a, ((0, rows - a.shape[0]), (0, 0)))


def kernel(x, p, mix_pre_g, mix_post_g, ffn_pre_g, ffn_post_g, pool_w, pool_scale, kv_norm_g, w_k, w_v, w_q, w_o, sinks, w_ff_gate, w_ff_up, w_ff_down, ple_norm_g, w_ple_gate, w_ple_proj, loss_target, m_mix_pre_g, m_mix_post_g, m_ffn_pre_g, m_ffn_post_g, m_pool_w, m_pool_scale, m_kv_norm_g, m_w_k, m_w_v, m_w_q, m_w_o, m_sinks, m_w_ff_gate, m_w_ff_up, m_w_ff_down, m_ple_norm_g, m_w_ple_gate, m_w_ple_proj, v_mix_pre_g, v_mix_post_g, v_ffn_pre_g, v_ffn_post_g, v_pool_w, v_pool_scale, v_kv_norm_g, v_w_k, v_w_v, v_w_q, v_w_o, v_sinks, v_w_ff_gate, v_w_ff_up, v_w_ff_down, v_ple_norm_g, v_w_ple_gate, v_w_ple_proj):
    depth = w_ff_gate.shape[0]
    n_a = pool_w.shape[0]
    t, d = x.shape[1], x.shape[2]
    h = x[0]
    tgt = loss_target[0]
    p_all = p.reshape(depth * t, p.shape[-1])
    my_block = _block_index(*_my_place())
    row = lambda g, i: g[i][None, :]
    bf = lambda a: a.astype(BF16)

    full, gathers = [None] * depth, {}
    start_tokens = jnp.zeros((), F32)
    for i in range(depth):
        shards = [bf(w_ff_gate[i].T), bf(w_ff_up[i].T), bf(w_ff_down[i]), bf(w_ple_gate[i]), bf(w_ple_proj[i].T)]
        if i == 0:
            pool0, scale_full = allgather_pieces([bf(pool_w[0].reshape(-1, POOL_GROUP)), _pad_rows(pool_scale)],
                                                 "allgather_pool0")
            order = pool0
        elif i < n_a:
            shards.append(bf(pool_w[i].reshape(-1, POOL_GROUP)))
        else:
            shards += [bf(_pair_heads(w_q[i - n_a], 1)), bf(w_o[i - n_a])]
            if i == n_a:
                shards.append(bf(jnp.concatenate([w_k, w_v], axis=1)))
        gathers[i] = exchange_start(shards, [_gather_zone(s) for s in shards], order, False, f"allgather_start_l{i}")
        order = gathers[i]["token"]
        start_tokens = start_tokens + order[0, 0]
    scale_full = scale_full.reshape(N_DEV, 8, -1)[:, :n_a].transpose(1, 0, 2).reshape(n_a, 1, d)

    cos, sin = _rope_tables(t, start_tokens)
    sink_b = [jnp.broadcast_to(_pair_heads(sinks[j][:, None], 0, 1), (N_HEADS, LANES)) for j in range(depth - n_a)]
    pool_full, wo_full = {}, {}

    saved = []
    kv = hk = None
    pre = {}
    for i in range(depth):
        if i > 0 and full[i] is None:
            full[i] = exchange_wait(gathers[i], h, f"allgather_wait_l{i}")
        s = {"h0": h}
        if i < n_a:
            pool_full[i] = ((pool0 if i == 0 else full[i][5]).reshape(N_DEV, len(POOL_WINDOWS), -1, POOL_GROUP)
                            .transpose(1, 0, 2, 3).reshape(len(POOL_WINDOWS), POOL_GROUP, POOL_GROUP))
            gpre = row(mix_pre_g, i) + start_tokens if i == 0 else row(mix_pre_g, i)
            h1, a = pool_mix_fwd(h, gpre, pool_full[i], scale_full[i], row(mix_post_g, i), row(ffn_pre_g, i))
            if i == 0:
                full[0] = exchange_wait(gathers[0], (h1, cos, sin), "allgather_wait_l0")
        else:
            j = i - n_a
            wo_full[i] = _pair_heads(full[i][6], 0)
            if i == n_a:
                hk, kv = pre[i][2:]
            hn, q = pre[i][:2]
            attn, m, h1, a = swa_fwd(q, kv, sink_b[j], wo_full[i], h, row(mix_post_g, i), row(ffn_pre_g, i))
            s.update(hn=hn, q=q, attn=attn, m=m)
        wg_t, wu_t, wd, wpg, wpp_t = full[i][:5]
        f, gte, up, hdn = ffn_fwd(a, wg_t, wu_t, wd)
        s.update(h1=h1, a=a, f=f, gte=gte, up=up, hdn=hdn)
        if i < depth - 1:
            proj = []
            if i + 1 >= n_a:
                full[i + 1] = exchange_wait(gathers[i + 1], f, f"allgather_wait_l{i + 1}")
                proj = [(row(mix_pre_g, i + 1), full[i + 1][5], d)]
                if i + 1 == n_a:
                    proj.append((kv_norm_g[None, :], full[i + 1][7], N_KV_HEADS * HEAD_DIM))
            h, *pre[i + 1] = post_ple_fwd(h1, f, p_all, i, row(ffn_post_g, i), row(ple_norm_g, i), wpg, wpp_t,
                                          cos, sin, proj)
        saved.append(s)

    g_mix_pre, g_mix_post, g_ffn_pre, g_ffn_post, g_ple = ([None] * depth for _ in range(5))
    g_kv = g_sinks = None
    g_scale = [None] * n_a
    landing, scatters = [None] * depth, {}
    dkv_sum = []
    scatter_token = jnp.zeros((), F32)
    for i in reversed(range(depth)):
        s = saved[i]
        wg_t, wu_t, wd, wpg, wpp_t = full[i][:5]
        last = i == depth - 1
        dh2, df, ub, dzb, dppb, gacc = post_ple_bwd(tgt if last else dh, s["h1"], s["f"], p_all, i,
                                                    row(ffn_post_g, i) + scatter_token, row(ple_norm_g, i), wpg, wpp_t,
                                                    from_target=last)
        g_ple[i], g_ffn_post[i] = gacc[0], gacc[1]
        if last:
            loss_row = gacc[2][None, :]
        da, dgte, dup = ffn_bwd_act(df, s["gte"], s["up"], wg_t, wu_t, wd)
        grads = [xty(dgte, s["a"]), xty(dup, s["a"]), xty(s["hdn"], df), xty(ub, dzb), xty(dppb, p_all, i)]
        early = exchange_start(grads, [_scatter_zone(g) for g in grads], dh2, True, f"reduce_scatter_start_l{i}a")
        early_token = early["token"][0, 0]
        if i < n_a:
            dh, dpw, gacc = pool_mix_bwd(s["h0"], dh2, da, row(mix_pre_g, i) + early_token, pool_full[i], scale_full[i],
                                         row(mix_post_g, i), row(ffn_pre_g, i))
            g_mix_pre[i], g_mix_post[i], g_ffn_pre[i], g_scale[i] = gacc[0], gacc[1], gacc[2], gacc[3]
            dpw = dpw.reshape(len(POOL_WINDOWS), N_DEV, -1, POOL_GROUP).transpose(1, 0, 2, 3)
            grads = [bf(dpw.reshape(-1, POOL_GROUP))]
        else:
            j = i - n_a
            dh1, dmb, dattn, gacc = oproj_post_bwd(dh2, da, s["h1"], s["m"], wo_full[i], row(mix_post_g, i) + early_token,
                                                   row(ffn_pre_g, i))
            g_mix_post[i], g_ffn_pre[i] = gacc[0], gacc[1]
            dq, dkv, dsink = swa_bwd(s["q"], kv, dattn, sink_b[j])
            dkv_sum.append(dkv)
            g_sinks = [_unpair_heads(dsink[:, 0:1], 0, 1)[:, 0]] + (g_sinks or [])
            branches = [(row(mix_pre_g, i), full[i][5], d, [dq])]
            if i == n_a:
                branches.append((kv_norm_g[None, :], full[i][7], N_KV_HEADS * HEAD_DIM, dkv_sum))
            outs = proj_rope_bwd(dh1, s["h0"], cos, sin, branches, f"proj_bwd_l{i}")
            dh, gacc = outs[0], outs[-1]
            g_mix_pre[i] = gacc[0]
            grads = [xty(s["hn"], outs[1]), _unpair_heads(xty(s["attn"], dmb), 0)]
            if i == n_a:
                g_kv = gacc[1]
                grads.append(xty(hk, outs[2]))
        late = exchange_start(grads, [_scatter_zone(g) for g in grads], dh, True, f"reduce_scatter_start_l{i}b")
        scatter_token = late["token"][0, 0]
        scatters[i] = (early, late)
    grad_x = dh[None]
    after = dh
    for i in reversed(range(depth)):
        landing[i] = (exchange_wait(scatters[i][0], after, f"reduce_scatter_wait_l{i}a")
                      + exchange_wait(scatters[i][1], after, f"reduce_scatter_wait_l{i}b"))
        after = landing[i][0]

    sink_row = jnp.pad(jnp.concatenate(g_sinks)[None, :], ((0, 0), (0, d - sinks.size)))
    stack = lambda rows_: _pad_rows(jnp.stack(rows_))
    pack = jnp.concatenate([stack(g_mix_pre), stack(g_mix_post), stack(g_ffn_pre), stack(g_ffn_post), stack(g_ple),
                            _pad_rows(g_kv[None]), stack(g_scale), _pad_rows(sink_row), _pad_rows(loss_row)], axis=0)
    tot = allreduce_small(pack)
    sec = lambda k, n: tot[8 * k:8 * k + n]
    loss = jnp.sum(tot[64])
    small = {
        "mix_pre_g": sec(0, depth), "mix_post_g": sec(1, depth), "ffn_pre_g": sec(2, depth),
        "ffn_post_g": sec(3, depth), "ple_norm_g": sec(4, depth), "kv_norm_g": tot[40],
        "pool_scale": lax.dynamic_slice_in_dim(sec(6, n_a), my_block * pool_scale.shape[1], pool_scale.shape[1], axis=1),
        "sinks": tot[56, :sinks.size].reshape(sinks.shape),
    }

    weights = dict(mix_pre_g=mix_pre_g, mix_post_g=mix_post_g, ffn_pre_g=ffn_pre_g, ffn_post_g=ffn_post_g, pool_w=pool_w, pool_scale=pool_scale, kv_norm_g=kv_norm_g, w_k=w_k, w_v=w_v, w_q=w_q, w_o=w_o, sinks=sinks, w_ff_gate=w_ff_gate, w_ff_up=w_ff_up, w_ff_down=w_ff_down, ple_norm_g=ple_norm_g, w_ple_gate=w_ple_gate, w_ple_proj=w_ple_proj)
    mom1 = dict(mix_pre_g=m_mix_pre_g, mix_post_g=m_mix_post_g, ffn_pre_g=m_ffn_pre_g, ffn_post_g=m_ffn_post_g, pool_w=m_pool_w, pool_scale=m_pool_scale, kv_norm_g=m_kv_norm_g, w_k=m_w_k, w_v=m_w_v, w_q=m_w_q, w_o=m_w_o, sinks=m_sinks, w_ff_gate=m_w_ff_gate, w_ff_up=m_w_ff_up, w_ff_down=m_w_ff_down, ple_norm_g=m_ple_norm_g, w_ple_gate=m_w_ple_gate, w_ple_proj=m_w_ple_proj)
    mom2 = dict(mix_pre_g=v_mix_pre_g, mix_post_g=v_mix_post_g, ffn_pre_g=v_ffn_pre_g, ffn_post_g=v_ffn_post_g, pool_w=v_pool_w, pool_scale=v_pool_scale, kv_norm_g=v_kv_norm_g, w_k=v_w_k, w_v=v_w_v, w_q=v_w_q, w_o=v_w_o, sinks=v_sinks, w_ff_gate=v_w_ff_gate, w_ff_up=v_w_ff_up, w_ff_down=v_w_ff_down, ple_norm_g=v_ple_norm_g, w_ple_gate=v_w_ple_gate, w_ple_proj=v_w_ple_proj)

    swap = lambda a: jnp.swapaxes(a, 1, 2)
    same = lambda a: a
    att = range(n_a, depth)
    plan = {
        "w_ff_gate": (swap, swap, [landing[i][0] for i in range(depth)]),
        "w_ff_up": (swap, swap, [landing[i][1] for i in range(depth)]),
        "w_ff_down": (same, same, [landing[i][2] for i in range(depth)]),
        "w_ple_gate": (same, same, [landing[i][3] for i in range(depth)]),
        "w_ple_proj": (swap, swap, [landing[i][4] for i in range(depth)]),
        "pool_w": (lambda a: a.reshape(n_a, -1, POOL_GROUP), lambda a: a.reshape(pool_w.shape),
                   [landing[i][5] for i in range(n_a)]),
        "w_q": (lambda a: _pair_heads(a, 2), lambda a: _unpair_heads(a, 2), [landing[i][5] for i in att]),
        "w_o": (same, same, [landing[i][6] for i in att]),
    }
    for nme, g in small.items():
        w = weights[nme]
        plan[nme] = ((lambda a: a.reshape((1, -1, a.shape[-1]))), (lambda a, shp=w.shape: a.reshape(shp)),
                     [g.reshape((1, -1, w.shape[-1]))])
    results = {}
    for nme, (view, unview, layer_parts) in plan.items():
        outs = adamw_layers(view(weights[nme]), view(mom1[nme]), view(mom2[nme]), layer_parts)
        results[nme] = [unview(o) for o in outs]
    kv_cat = lambda ws: jnp.concatenate([ws["w_k"], ws["w_v"]], axis=1)[None]
    outs = adamw_layers(kv_cat(weights), kv_cat(mom1), kv_cat(mom2), [landing[n_a][7]])
    results["w_k"] = [o[0, :, :w_k.shape[1]] for o in outs]
    results["w_v"] = [o[0, :, w_k.shape[1]:] for o in outs]

    order = ["mix_pre_g", "mix_post_g", "ffn_pre_g", "ffn_post_g", "pool_w", "pool_scale", "kv_norm_g", "w_k", "w_v",
             "w_q", "w_o", "sinks", "w_ff_gate", "w_ff_up", "w_ff_down", "ple_norm_g", "w_ple_gate", "w_ple_proj"]
    g_out, d_out, m_out, v_out = ([results[nme][k] for nme in order] for k in range(4))
    return (loss, grad_x, *g_out, *d_out, *m_out, *v_out)
```

```python
import functools

import jax
import jax.numpy as jnp
from jax import lax
from jax.experimental import pallas as pl
from jax.experimental.pallas import tpu as pltpu

F32 = jnp.float32
BF16 = jnp.bfloat16

N_DEV = 8
HEAD_DIM = 64
N_HEADS = 16
N_KV_HEADS = 4
GQA = N_HEADS // N_KV_HEADS
BLOCK = 128
POOL_WINDOWS = (2, 4, 8, 16)
POOL_GROUP = 256
HALO = 16
ROPE_THETA = 10000.0
RMS_EPS = 1e-6
NEG_INF = -1e30
LANES = 128
ATTN_SUB = 8
XTY_ROWS = 2048
FFN_CHUNK = 768
VMEM_LIMIT = 56 * 1024 * 1024

ADAM_LR = 0.001
ADAM_B1 = 0.9
ADAM_B2 = 0.999
ADAM_EPS = 1e-08
ADAM_WD = 0.01
ADAM_STEP = 10

MESH = pl.DeviceIdType.MESH
ANY = pl.BlockSpec(memory_space=pl.ANY)

NT_DIMS = (((1,), (1,)), ((), ()))
TN_DIMS = (((0,), (0,)), ((), ()))


def _cparams(sem=None, vmem=None):
    kw = {}
    if sem is not None:
        kw["dimension_semantics"] = sem
    if vmem is not None:
        kw["vmem_limit_bytes"] = vmem
    return pltpu.CompilerParams(**kw)


def _rows(tm, n, first=0):
    return pl.BlockSpec((tm, n), lambda i: (i + first, 0))


def _rows_rev(tm, n, nt):
    return pl.BlockSpec((tm, n), lambda i: (nt - 1 - i, 0))


def _const(shape):
    nd = len(shape)
    return pl.BlockSpec(shape, lambda *_: (0,) * nd, pipeline_mode=pl.Buffered(1))


def _resident(shape):
    nd = len(shape)
    return pl.BlockSpec(shape, lambda *_: (0,) * nd)


def _tile_rows(t):
    return 512 if t % 512 == 0 else 128


def _dot(a, b):
    return jnp.dot(a, b, preferred_element_type=F32)


def _dot_nt(a, b):
    return lax.dot_general(a, b, NT_DIMS, preferred_element_type=F32)


def _dot_tn(a, b):
    return lax.dot_general(a, b, TN_DIMS, preferred_element_type=F32)


def _rms_r(x):
    return lax.rsqrt(jnp.mean(x * x, axis=-1, keepdims=True) + RMS_EPS)


def _rms_bwd(x, r, g, dy):
    gy = dy * g
    dx = r * gy - x * (r * r * r * jnp.mean(gy * x, axis=-1, keepdims=True))
    dg = jnp.sum(dy * (x * r), axis=0, keepdims=True)
    return dx, dg


def _sigmoid(x):
    return jax.nn.sigmoid(x)


def _rope_tables(t, zero_token):
    inv = 1.0 / (ROPE_THETA ** (jnp.arange(0, HEAD_DIM, 2, dtype=F32) / HEAD_DIM))
    ang = (jnp.arange(t, dtype=F32) + zero_token)[:, None] * jnp.tile(inv, 2 * LANES // HEAD_DIM)[None, :]
    sign = jnp.tile(jnp.repeat(jnp.array([-1.0, 1.0], F32), HEAD_DIM // 2), LANES // HEAD_DIM)
    return jnp.cos(ang), jnp.sin(ang) * sign[None, :]


def _swap_halves(x):
    n = x.shape[1]
    lane = lax.broadcasted_iota(jnp.int32, x.shape, 1)
    first = (lane % HEAD_DIM) < (HEAD_DIM // 2)
    return jnp.where(first, pltpu.roll(x, n - HEAD_DIM // 2, 1), pltpu.roll(x, HEAD_DIM // 2, 1))


def _rope(x, cos, sin):
    reps = x.shape[1] // LANES
    return x * jnp.tile(cos, (1, reps)) + _swap_halves(x) * jnp.tile(sin, (1, reps))


def _unrope(dy, cos, sin):
    reps = dy.shape[1] // LANES
    return dy * jnp.tile(cos, (1, reps)) + _swap_halves(dy * jnp.tile(sin, (1, reps)))


def _acc_init(acc_ref):
    @pl.when(pl.program_id(0) == 0)
    def _():
        acc_ref[...] = jnp.zeros_like(acc_ref)


def _window_sums(ext, tm, forward):
    n = tm + HALO
    out = []
    for g, w in enumerate(POOL_WINDOWS):
        s = ext[:, g * POOL_GROUP:(g + 1) * POOL_GROUP]
        k = 1
        while k < w:
            s = s + pltpu.roll(s, k if forward else n - k, 0)
            k *= 2
        out.append(s[HALO:, :] if forward else s[:tm, :])
    return out


def _pool_inv_counts(tile, tm):
    t = tile * tm + lax.broadcasted_iota(jnp.int32, (tm, 1), 0)
    return [1.0 / jnp.minimum(t + 1, w).astype(F32) for w in POOL_WINDOWS]


def _pool_mix(hn, ext, inv_cnts, pw_ref, scale, tm):
    sums = _window_sums(ext, tm, True)
    pooled, ys = [], []
    for g in range(len(POOL_WINDOWS)):
        pg = (sums[g] * inv_cnts[g] - hn[:, g * POOL_GROUP:(g + 1) * POOL_GROUP]).astype(BF16)
        pooled.append(pg)
        ys.append(_dot(pg, pw_ref[g]))
    y = jnp.concatenate(ys, axis=1)
    return pooled, y, y * scale


def pool_mix_fwd(h0, gpre, pool_w, scale, gpost, gffn):
    t, d = h0.shape
    tm = _tile_rows(t)

    def body(h_ref, gpre_ref, pw_ref, scale_ref, gpost_ref, gffn_ref, h1_ref, a_ref, carry):
        i = pl.program_id(0)

        @pl.when(i == 0)
        def _():
            carry[...] = jnp.zeros_like(carry)

        x = h_ref[...]
        hn = x * _rms_r(x) * gpre_ref[...]
        ext = jnp.concatenate([carry[...], hn], axis=0)
        carry[...] = hn[tm - HALO:, :]
        _, _, m = _pool_mix(hn, ext, _pool_inv_counts(i, tm), pw_ref, scale_ref[...], tm)
        h1 = x + m * _rms_r(m) * gpost_ref[...]
        h1_ref[...] = h1
        a_ref[...] = (h1 * _rms_r(h1) * gffn_ref[...]).astype(BF16)

    return pl.pallas_call(
        functools.partial(body), name="pool_mix_fwd", grid=(t // tm,),
        in_specs=[_rows(tm, d), _const((1, d)), _const(pool_w.shape), _const((1, d)), _const((1, d)), _const((1, d))],
        out_specs=[_rows(tm, d), _rows(tm, d)],
        out_shape=[jax.ShapeDtypeStruct((t, d), F32), jax.ShapeDtypeStruct((t, d), BF16)],
        scratch_shapes=[pltpu.VMEM((HALO, d), F32)],
        compiler_params=_cparams(("arbitrary",), VMEM_LIMIT),
    )(h0, gpre, pool_w, scale, gpost, gffn)


def pool_mix_bwd(h0, dh2, da, gpre, pool_w, scale, gpost, gffn):
    t, d = h0.shape
    tm = _tile_rows(t)
    nt = t // tm
    hb = tm // HALO

    def body(h_ref, halo_ref, dh2_ref, da_ref, gpre_ref, pw_ref, scale_ref, gpost_ref, gffn_ref,
             dh0_ref, dpw_ref, gacc_ref, carry):
        i = pl.program_id(0)
        tile = nt - 1 - i
        _acc_init(gacc_ref)
        _acc_init(dpw_ref)

        @pl.when(i == 0)
        def _():
            carry[...] = jnp.zeros_like(carry)

        x = h_ref[...]
        gpre_v, scale_v, gpost_v, gffn_v = gpre_ref[...], scale_ref[...], gpost_ref[...], gffn_ref[...]
        r0 = _rms_r(x)
        hn = x * r0 * gpre_v
        xh = halo_ref[...]
        hn_halo = jnp.where(tile > 0, xh * _rms_r(xh) * gpre_v, 0.0)
        ext = jnp.concatenate([hn_halo, hn], axis=0)
        inv_cnts = _pool_inv_counts(tile, tm)
        pooled, y, m = _pool_mix(hn, ext, inv_cnts, pw_ref, scale_v, tm)
        rm = _rms_r(m)
        h1 = x + m * rm * gpost_v
        dh1_n, dgffn = _rms_bwd(h1, _rms_r(h1), gffn_v, da_ref[...].astype(F32))
        dh1 = dh2_ref[...] + dh1_n
        dm, dgpost = _rms_bwd(m, rm, gpost_v, dh1)
        dscale = jnp.sum(dm * y, axis=0, keepdims=True)
        dy = (dm * scale_v).astype(BF16)
        dpn = []
        for g in range(len(POOL_WINDOWS)):
            dyg = dy[:, g * POOL_GROUP:(g + 1) * POOL_GROUP]
            dpw_ref[g] += _dot_tn(pooled[g], dyg)
            dpn.append(_dot_nt(dyg, pw_ref[g]))
        dpooled = jnp.concatenate(dpn, axis=1)
        dpc = jnp.concatenate([dpn[g] * inv_cnts[g] for g in range(len(POOL_WINDOWS))], axis=1)
        ext2 = jnp.concatenate([dpc, carry[...]], axis=0)
        carry[...] = dpc[:HALO, :]
        dhn = jnp.concatenate(_window_sums(ext2, tm, False), axis=1) - dpooled
        dh0_n, dgpre = _rms_bwd(x, r0, gpre_v, dhn)
        dh0_ref[...] = dh1 + dh0_n
        gacc_ref[0:1, :] += dgpre
        gacc_ref[1:2, :] += dgpost
        gacc_ref[2:3, :] += dgffn
        gacc_ref[3:4, :] += dscale

    return pl.pallas_call(
        functools.partial(body), name="pool_mix_bwd", grid=(nt,),
        in_specs=[_rows_rev(tm, d, nt),
                  pl.BlockSpec((HALO, d), lambda i: (jnp.maximum((nt - 1 - i) * hb - 1, 0), 0)),
                  _rows_rev(tm, d, nt), _rows_rev(tm, d, nt),
                  _const((1, d)), _const(pool_w.shape), _const((1, d)), _const((1, d)), _const((1, d))],
        out_specs=[_rows_rev(tm, d, nt), _resident(pool_w.shape), _resident((8, d))],
        out_shape=[jax.ShapeDtypeStruct((t, d), F32), jax.ShapeDtypeStruct(pool_w.shape, F32),
                   jax.ShapeDtypeStruct((8, d), F32)],
        scratch_shapes=[pltpu.VMEM((HALO, d), F32)],
        compiler_params=_cparams(("arbitrary",), VMEM_LIMIT),
    )(h0, h0, dh2, da, gpre, pool_w, scale, gpost, gffn)


def _ffn_chunks(f):
    return [(c, min(c + FFN_CHUNK, f)) for c in range(0, f, FFN_CHUNK)]


def ffn_fwd(a, wg_t, wu_t, wd):
    t, d = a.shape
    f = wd.shape[0]
    tm = _tile_rows(t)

    def body(a_ref, wg_ref, wu_ref, wd_ref, f_ref, gte_ref, up_ref, hdn_ref):
        av = a_ref[...]
        acc = jnp.zeros((tm, d), F32)
        for c0, c1 in _ffn_chunks(f):
            gte = _dot_nt(av, wg_ref[c0:c1, :])
            up = _dot_nt(av, wu_ref[c0:c1, :])
            gte_ref[:, c0:c1] = gte.astype(BF16)
            up_ref[:, c0:c1] = up.astype(BF16)
            hdn = (gte * _sigmoid(gte) * up).astype(BF16)
            hdn_ref[:, c0:c1] = hdn
            acc = acc + _dot(hdn, wd_ref[c0:c1, :])
        f_ref[...] = acc.astype(BF16)

    return pl.pallas_call(
        functools.partial(body), name="ffn_fwd", grid=(t // tm,),
        in_specs=[_rows(tm, d), _const((f, d)), _const((f, d)), _const((f, d))],
        out_specs=[_rows(tm, d), _rows(tm, f), _rows(tm, f), _rows(tm, f)],
        out_shape=[jax.ShapeDtypeStruct((t, d), BF16)] + [jax.ShapeDtypeStruct((t, f), BF16)] * 3,
        compiler_params=_cparams(("parallel",), VMEM_LIMIT),
    )(a, wg_t, wu_t, wd)


def ffn_bwd_act(df, gte, up, wg_t, wu_t, wd):
    t, d = df.shape
    f = wd.shape[0]
    tm = _tile_rows(t)

    def body(df_ref, gte_ref, up_ref, wg_ref, wu_ref, wd_ref, da_ref, dgte_ref, dup_ref):
        dfv = df_ref[...]
        chunks = _ffn_chunks(f)
        half = chunks[len(chunks) // 2][0]
        acc = None
        for c0, c1 in chunks:
            g = gte_ref[:, c0:c1].astype(F32)
            u = up_ref[:, c0:c1].astype(F32)
            sg = _sigmoid(g)
            sl = g * sg
            dh = _dot_nt(dfv, wd_ref[c0:c1, :])
            dup_ref[:, c0:c1] = (dh * sl).astype(BF16)
            dgte_ref[:, c0:c1] = (dh * u * (sg * (1.0 + g * (1.0 - sg)))).astype(BF16)
            if c1 == half:
                acc = _dot(dgte_ref[:, :half], wg_ref[:half, :]) + _dot(dup_ref[:, :half], wu_ref[:half, :])
        da = acc + _dot(dgte_ref[:, half:], wg_ref[half:, :]) + _dot(dup_ref[:, half:], wu_ref[half:, :])
        da_ref[...] = da.astype(BF16)

    return pl.pallas_call(
        functools.partial(body), name="ffn_bwd_act", grid=(t // tm,),
        in_specs=[_rows(tm, d), _rows(tm, f), _rows(tm, f), _const((f, d)), _const((f, d)), _const((f, d))],
        out_specs=[_rows(tm, d), _rows(tm, f), _rows(tm, f)],
        out_shape=[jax.ShapeDtypeStruct((t, d), BF16)] + [jax.ShapeDtypeStruct((t, f), BF16)] * 2,
        compiler_params=_cparams(("parallel",), VMEM_LIMIT),
    )(df, gte, up, wg_t, wu_t, wd)


def xty(x, y, y_part=0):
    t, nx = x.shape
    ny = y.shape[1]
    tk = XTY_ROWS if t % XTY_ROWS == 0 else _tile_rows(t)
    bn = nx // 2 if nx > 1024 else nx
    nk = t // tk

    def body(x_ref, y_ref, o_ref, acc):
        k = pl.program_id(1)

        @pl.when(k == 0)
        def _():
            acc[...] = jnp.zeros_like(acc)

        acc[...] += _dot_tn(x_ref[...].astype(BF16), y_ref[...].astype(BF16))

        @pl.when(k == nk - 1)
        def _():
            o_ref[...] = acc[...].astype(BF16)

    return pl.pallas_call(
        functools.partial(body), name="xty", grid=(nx // bn, nk),
        in_specs=[pl.BlockSpec((tk, bn), lambda j, k: (k, j)),
                  pl.BlockSpec((tk, ny), lambda j, k: (k + y_part * nk, 0))],
        out_specs=pl.BlockSpec((bn, ny), lambda j, k: (j, 0)),
        out_shape=jax.ShapeDtypeStruct((nx, ny), BF16),
        scratch_shapes=[pltpu.VMEM((bn, ny), F32)],
        compiler_params=_cparams(("parallel", "arbitrary"), VMEM_LIMIT),
    )(x, y)


def _ple_fwd_tile(h1, f, p, gpost, gple, wpg_ref, wpp_ref):
    rf = _rms_r(f)
    h2 = h1 + f * rf * gpost
    r2 = _rms_r(h2)
    ub = (h2 * r2 * gple).astype(BF16)
    gate = _sigmoid(_dot(ub, wpg_ref[...]))
    pp = _dot_nt(p.astype(BF16), wpp_ref[...])
    return rf, h2, r2, ub, gate, pp


def post_ple_fwd(h1, f, p, layer, gpost, gple, wpg, wpp_t, cos=None, sin=None, proj=()):
    t, d = h1.shape
    pd = p.shape[1]
    tm = _tile_rows(t)
    nb = len(proj)

    def body(*refs):
        h1_ref, f_ref, p_ref, gpost_ref, gple_ref, wpg_ref, wpp_ref = refs[:7]
        pos = 9 if nb else 7
        out_ref = refs[pos + 2 * nb]
        _, h2, _, _, gate, pp = _ple_fwd_tile(h1_ref[...], f_ref[...].astype(F32), p_ref[...], gpost_ref[...],
                                              gple_ref[...], wpg_ref, wpp_ref)
        h3 = h2 + pp * gate
        out_ref[...] = h3
        if nb:
            r3 = _rms_r(h3)
        for b in range(nb):
            g_ref, w_ref = refs[pos + 2 * b], refs[pos + 2 * b + 1]
            hn_ref, y_ref = refs[pos + 2 * nb + 1 + 2 * b], refs[pos + 2 * nb + 2 + 2 * b]
            n_rope = proj[b][2]
            hn = (h3 * r3 * g_ref[...]).astype(BF16)
            hn_ref[...] = hn
            y = _dot(hn, w_ref[...])
            y_ref[:, :n_rope] = _rope(y[:, :n_rope], refs[7][...], refs[8][...]).astype(BF16)
            if n_rope < y.shape[1]:
                y_ref[:, n_rope:] = y[:, n_rope:].astype(BF16)

    in_specs = [_rows(tm, d), _rows(tm, d), _rows(tm, pd, layer * (t // tm)), _const((1, d)), _const((1, d)),
                _const(wpg.shape), _const(wpp_t.shape)]
    args = [h1, f, p, gpost, gple, wpg, wpp_t]
    out_specs, out_shape = [_rows(tm, d)], [jax.ShapeDtypeStruct((t, d), F32)]
    if nb:
        in_specs += [_rows(tm, LANES), _rows(tm, LANES)]
        args += [cos, sin]
    for gain, w, _ in proj:
        in_specs += [_const((1, d)), _const(w.shape)]
        args += [gain, w]
        out_specs += [_rows(tm, d), _rows(tm, w.shape[1])]
        out_shape += [jax.ShapeDtypeStruct((t, d), BF16), jax.ShapeDtypeStruct((t, w.shape[1]), BF16)]
    return pl.pallas_call(
        functools.partial(body), name="post_ple_proj_fwd" if nb else "post_ple_fwd", grid=(t // tm,),
        in_specs=in_specs, out_specs=out_specs, out_shape=out_shape,
        compiler_params=_cparams(("parallel",), VMEM_LIMIT),
    )(*args)


def post_ple_bwd(dh3, h1, f, p, layer, gpost, gple, wpg, wpp_t, from_target=False):
    t, d = h1.shape
    pd = p.shape[1]
    tm = _tile_rows(t)

    def body(dh3_ref, h1_ref, f_ref, p_ref, gpost_ref, gple_ref, wpg_ref, wpp_ref,
             dh2_ref, df_ref, u_ref, dz_ref, dpp_ref, gacc_ref):
        _acc_init(gacc_ref)
        gpost_v, gple_v = gpost_ref[...], gple_ref[...]
        nsub = 2 if tm % 16 == 0 else 1
        for sb in range(nsub):
            rows = slice(sb * (tm // nsub), (sb + 1) * (tm // nsub))
            fv = f_ref[rows, :].astype(F32)
            rf, h2, r2, ub, gate, pp = _ple_fwd_tile(h1_ref[rows, :], fv, p_ref[rows, :], gpost_v, gple_v, wpg_ref,
                                                     wpp_ref)
            if from_target:
                err = h2 + pp * gate - dh3_ref[rows, :]
                dh3v = err * (1.0 / d)
                gacc_ref[2:3, :] += jnp.sum(err * err, axis=0, keepdims=True) * (0.5 / d)
            else:
                dh3v = dh3_ref[rows, :]
            dpp_ref[rows, :] = (dh3v * gate).astype(BF16)
            dz = (dh3v * pp * gate * (1.0 - gate)).astype(BF16)
            dz_ref[rows, :] = dz
            u_ref[rows, :] = ub
            du = _dot_nt(dz, wpg_ref[...])
            dh2_n, dgple = _rms_bwd(h2, r2, gple_v, du)
            dh2 = dh3v + dh2_n
            df, dgpost = _rms_bwd(fv, rf, gpost_v, dh2)
            dh2_ref[rows, :] = dh2
            df_ref[rows, :] = df.astype(BF16)
            gacc_ref[0:1, :] += dgple
            gacc_ref[1:2, :] += dgpost

    return pl.pallas_call(
        functools.partial(body), name="post_ple_loss_bwd" if from_target else "post_ple_bwd", grid=(t // tm,),
        in_specs=[_rows(tm, d), _rows(tm, d), _rows(tm, d), _rows(tm, pd, layer * (t // tm)), _const((1, d)),
                  _const((1, d)), _const(wpg.shape), _const(wpp_t.shape)],
        out_specs=[_rows(tm, d)] * 5 + [_resident((8, d))],
        out_shape=[jax.ShapeDtypeStruct((t, d), F32)] + [jax.ShapeDtypeStruct((t, d), BF16)] * 4
        + [jax.ShapeDtypeStruct((8, d), F32)],
        compiler_params=_cparams(("arbitrary",), VMEM_LIMIT),
    )(dh3, h1, f, p, gpost, gple, wpg, wpp_t)


def proj_rope_bwd(dh1, h0, cos, sin, branches, name):
    t, d = h0.shape
    tm = _tile_rows(t)
    nb = len(branches)
    n_cot = [len(b[3]) for b in branches]

    def body(*refs):
        dh1_ref, h0_ref, cos_ref, sin_ref = refs[:4]
        pos = 4
        br_refs = []
        for b in range(nb):
            br_refs.append((refs[pos], refs[pos + 1], refs[pos + 2:pos + 2 + n_cot[b]]))
            pos += 2 + n_cot[b]
        dh0_ref = refs[pos]
        dpre_refs = refs[pos + 1:pos + 1 + nb]
        gacc_ref = refs[pos + 1 + nb]
        _acc_init(gacc_ref)
        x = h0_ref[...]
        r0 = _rms_r(x)
        dh = dh1_ref[...]
        for b in range(nb):
            g_ref, w_ref, cot_refs = br_refs[b]
            n_rope = branches[b][2]
            dy = cot_refs[0][...].astype(F32)
            for c_ref in cot_refs[1:]:
                dy = dy + c_ref[...].astype(F32)
            n = dy.shape[1]
            dpre_refs[b][:, :n_rope] = _unrope(dy[:, :n_rope], cos_ref[...], sin_ref[...]).astype(BF16)
            if n_rope < n:
                dpre_refs[b][:, n_rope:] = dy[:, n_rope:].astype(BF16)
            dhn = _dot_nt(dpre_refs[b][...], w_ref[...])
            dx, dg = _rms_bwd(x, r0, g_ref[...], dhn)
            dh = dh + dx
            gacc_ref[b:b + 1, :] += dg
        dh0_ref[...] = dh

    in_specs = [_rows(tm, d), _rows(tm, d), _rows(tm, LANES), _rows(tm, LANES)]
    args = [dh1, h0, cos, sin]
    out_specs = [_rows(tm, d)]
    out_shape = [jax.ShapeDtypeStruct((t, d), F32)]
    for gain, w, _, cots in branches:
        n = w.shape[1]
        in_specs += [_const((1, d)), _const(w.shape)] + [_rows(tm, n)] * len(cots)
        args += [gain, w] + list(cots)
        out_specs.append(_rows(tm, n))
        out_shape.append(jax.ShapeDtypeStruct((t, n), BF16))
    out_specs.append(_resident((8, d)))
    out_shape.append(jax.ShapeDtypeStruct((8, d), F32))
    return pl.pallas_call(
        functools.partial(body), name=name, grid=(t // tm,),
        in_specs=in_specs, out_specs=out_specs, out_shape=out_shape,
        compiler_params=_cparams(("arbitrary",), VMEM_LIMIT),
    )(*args)


def _tri():
    row = lax.broadcasted_iota(jnp.int32, (BLOCK, BLOCK), 0)
    col = lax.broadcasted_iota(jnp.int32, (BLOCK, BLOCK), 1)
    return col <= row


def _block_diag(x):
    lo = lax.broadcasted_iota(jnp.int32, x.shape, 1) < HEAD_DIM
    zero = jnp.zeros_like(x)
    return jnp.concatenate([jnp.where(lo, x, zero), jnp.where(lo, zero, x)], axis=0)


def _dense(x, tri):
    return (jnp.where(tri, x[:, BLOCK:2 * BLOCK], x[:, :BLOCK]),
            jnp.where(tri, x[:, 3 * BLOCK:], x[:, 2 * BLOCK:3 * BLOCK]))


def _banded(xa, xb, tri):
    zero = jnp.zeros_like(xa)
    return jnp.concatenate([jnp.where(tri, zero, xa), jnp.where(tri, xa, zero),
                            jnp.where(tri, zero, xb), jnp.where(tri, xb, zero)], axis=1).astype(BF16)


def _softmax_sink(s, sink):
    mx = jnp.maximum(jnp.max(s, axis=1, keepdims=True), sink)
    e = jnp.exp(s - mx)
    es = jnp.exp(sink - mx)
    inv = 1.0 / (jnp.sum(e, axis=1, keepdims=True) + es)
    return e * inv, es * inv


def _sink_column(sink_ref):
    return jnp.concatenate([jnp.broadcast_to(sink_ref[h:h + 1, 0:1], (BLOCK, 1)) for h in range(N_HEADS)], axis=0)


def _kv_block_diag(band, kvw):
    n_lt = kvw // LANES
    return ([_block_diag(band[:, lt * LANES:(lt + 1) * LANES]) for lt in range(n_lt)],
            [_block_diag(band[:, kvw + lt * LANES:kvw + (lt + 1) * LANES]) for lt in range(n_lt)])


def _all_probs(q_ref, r0, kbd, tri, n, sink_ref):
    dense = []
    for tq in range(N_HEADS // 2):
        s = _dot_nt(q_ref[r0:r0 + BLOCK, tq * LANES:(tq + 1) * LANES], kbd[tq // GQA])
        dense += list(_dense(s, tri))
    bias = jnp.where(jnp.logical_not(tri) & (n == 0), NEG_INF, 0.0)
    s_all = jnp.concatenate(dense, axis=0) * (HEAD_DIM ** -0.5) + jnp.concatenate([bias] * N_HEADS, axis=0)
    return _softmax_sink(s_all, _sink_column(sink_ref))


def _head_rows(x, tq):
    return x[2 * tq * BLOCK:(2 * tq + 1) * BLOCK], x[(2 * tq + 1) * BLOCK:(2 * tq + 2) * BLOCK]


def _attn_sub(t):
    return ATTN_SUB if t % (ATTN_SUB * BLOCK) == 0 else 1


def swa_fwd(q, kv, sink_b, w_o, h0, gpost, gffn):
    t, d = q.shape
    sub = _attn_sub(t)
    rows = sub * BLOCK
    kvw = N_KV_HEADS * HEAD_DIM

    def body(q_ref, kvc_ref, kvp_ref, sink_ref, w_ref, h0_ref, gpost_ref, gffn_ref, o_ref, m_ref, h1_ref, a_ref):
        i = pl.program_id(0)
        tri = _tri()
        ext = jnp.concatenate([kvp_ref[...], kvc_ref[...]], axis=0)
        for sb in range(sub):
            r0 = sb * BLOCK
            kbd, vbd = _kv_block_diag(ext[r0:r0 + 2 * BLOCK], kvw)
            p, _ = _all_probs(q_ref, r0, kbd, tri, i * sub + sb, sink_ref)
            for tq in range(N_HEADS // 2):
                pa, pb = _head_rows(p, tq)
                o_ref[r0:r0 + BLOCK, tq * LANES:(tq + 1) * LANES] = _dot(_banded(pa, pb, tri), vbd[tq // GQA]).astype(BF16)
            blk = slice(r0, r0 + BLOCK)
            m = _dot(o_ref[blk, :], w_ref[...])
            m_ref[blk, :] = m.astype(BF16)
            h1 = h0_ref[blk, :] + m * _rms_r(m) * gpost_ref[...]
            h1_ref[blk, :] = h1
            a_ref[blk, :] = (h1 * _rms_r(h1) * gffn_ref[...]).astype(BF16)

    return pl.pallas_call(
        functools.partial(body), name="swa_fwd", grid=(t // rows,),
        in_specs=[_rows(rows, d), _rows(rows, 2 * kvw),
                  pl.BlockSpec((BLOCK, 2 * kvw), lambda i: (jnp.maximum(i * sub - 1, 0), 0)), _const(sink_b.shape),
                  _const(w_o.shape), _rows(rows, d), _const((1, d)), _const((1, d))],
        out_specs=[_rows(rows, d)] * 4,
        out_shape=[jax.ShapeDtypeStruct((t, d), BF16), jax.ShapeDtypeStruct((t, d), BF16),
                   jax.ShapeDtypeStruct((t, d), F32), jax.ShapeDtypeStruct((t, d), BF16)],
        compiler_params=_cparams(("parallel",), VMEM_LIMIT),
    )(q, kv, kv, sink_b, w_o, h0, gpost, gffn)


def swa_bwd(q, kv, do, sink_b):
    t, d = q.shape
    sub = _attn_sub(t)
    nq = t // (sub * BLOCK)
    kvw = N_KV_HEADS * HEAD_DIM

    def body(q_ref, do_ref, kvc_ref, kvp_ref, sink_ref, dq_ref, dkv_ref, dsink_ref, carry):
        i = pl.program_id(0)
        step = nq - 1 - i
        _acc_init(dsink_ref)

        @pl.when(i == 0)
        def _():
            carry[...] = jnp.zeros_like(carry)

        tri = _tri()
        lo = lax.broadcasted_iota(jnp.int32, (2 * BLOCK, LANES), 1) < HEAD_DIM
        ext = jnp.concatenate([kvp_ref[...], kvc_ref[...]], axis=0)
        dkeys = [None] * (sub + 1)
        for sb in reversed(range(sub)):
            r0 = sb * BLOCK
            kbd, vbd = _kv_block_diag(ext[r0:r0 + 2 * BLOCK], kvw)
            p, ps = _all_probs(q_ref, r0, kbd, tri, step * sub + sb, sink_ref)
            dp = []
            for tq in range(N_HEADS // 2):
                dp += list(_dense(_dot_nt(do_ref[r0:r0 + BLOCK, tq * LANES:(tq + 1) * LANES], vbd[tq // GQA]), tri))
            dp = jnp.concatenate(dp, axis=0)
            delta = jnp.sum(p * dp, axis=1, keepdims=True)
            ds = p * (dp - delta) * (HEAD_DIM ** -0.5)
            dsk = ps * delta
            for h in range(N_HEADS):
                dsink_ref[h:h + 1, :] -= jnp.sum(dsk[h * BLOCK:(h + 1) * BLOCK], axis=0, keepdims=True)
            dkb = [jnp.zeros((4 * BLOCK, LANES), F32) for _ in kbd]
            dvb = [jnp.zeros((4 * BLOCK, LANES), F32) for _ in kbd]
            for tq in range(N_HEADS // 2):
                lt = tq // GQA
                cols = slice(tq * LANES, (tq + 1) * LANES)
                dsb = _banded(*_head_rows(ds, tq), tri)
                dq_ref[r0:r0 + BLOCK, cols] = _dot(dsb, kbd[lt]).astype(BF16)
                dkb[lt] = dkb[lt] + _dot_tn(dsb, q_ref[r0:r0 + BLOCK, cols])
                dvb[lt] = dvb[lt] + _dot_tn(_banded(*_head_rows(p, tq), tri), do_ref[r0:r0 + BLOCK, cols])
            dall = jnp.concatenate([jnp.where(lo, x[:2 * BLOCK], x[2 * BLOCK:]) for x in dkb + dvb], axis=1)
            dkeys[sb + 1] = dall[BLOCK:] if dkeys[sb + 1] is None else dkeys[sb + 1] + dall[BLOCK:]
            dkeys[sb] = dall[:BLOCK]
        for sb in range(sub):
            own = dkeys[sb + 1] + carry[...] if sb == sub - 1 else dkeys[sb + 1]
            dkv_ref[sb * BLOCK:(sb + 1) * BLOCK, :] = own
        carry[...] = dkeys[0]

    rev = lambda i: (nq - 1 - i, 0)
    return pl.pallas_call(
        functools.partial(body), name="swa_bwd", grid=(nq,),
        in_specs=[pl.BlockSpec((sub * BLOCK, d), rev), pl.BlockSpec((sub * BLOCK, d), rev),
                  pl.BlockSpec((sub * BLOCK, 2 * kvw), rev),
                  pl.BlockSpec((BLOCK, 2 * kvw), lambda i: (jnp.maximum((nq - 1 - i) * sub - 1, 0), 0)),
                  _const(sink_b.shape)],
        out_specs=[pl.BlockSpec((sub * BLOCK, d), rev), pl.BlockSpec((sub * BLOCK, 2 * kvw), rev),
                   _resident(sink_b.shape)],
        out_shape=[jax.ShapeDtypeStruct((t, d), BF16), jax.ShapeDtypeStruct((t, 2 * kvw), F32),
                   jax.ShapeDtypeStruct(sink_b.shape, F32)],
        scratch_shapes=[pltpu.VMEM((BLOCK, 2 * kvw), F32)],
        compiler_params=_cparams(("arbitrary",), VMEM_LIMIT),
    )(q, do, kv, kv, sink_b)


def oproj_post_bwd(dh2, da, h1, m, w_o, gpost, gffn):
    t, d = h1.shape
    tm = _tile_rows(t)

    def body(dh2_ref, da_ref, h1_ref, m_ref, w_ref, gpost_ref, gffn_ref, dh1_ref, dm_ref, dat_ref, gacc_ref):
        _acc_init(gacc_ref)
        h1v, mv = h1_ref[...], m_ref[...].astype(F32)
        dh1_n, dgffn = _rms_bwd(h1v, _rms_r(h1v), gffn_ref[...], da_ref[...].astype(F32))
        dh1 = dh2_ref[...] + dh1_n
        dm, dgpost = _rms_bwd(mv, _rms_r(mv), gpost_ref[...], dh1)
        dmb = dm.astype(BF16)
        dh1_ref[...] = dh1
        dm_ref[...] = dmb
        dat_ref[...] = _dot_nt(dmb, w_ref[...]).astype(BF16)
        gacc_ref[0:1, :] += dgpost
        gacc_ref[1:2, :] += dgffn

    return pl.pallas_call(
        functools.partial(body), name="oproj_post_bwd", grid=(t // tm,),
        in_specs=[_rows(tm, d)] * 4 + [_const(w_o.shape), _const((1, d)), _const((1, d))],
        out_specs=[_rows(tm, d)] * 3 + [_resident((8, d))],
        out_shape=[jax.ShapeDtypeStruct((t, d), F32), jax.ShapeDtypeStruct((t, d), BF16),
                   jax.ShapeDtypeStruct((t, d), BF16), jax.ShapeDtypeStruct((8, d), F32)],
        compiler_params=_cparams(("arbitrary",), VMEM_LIMIT),
    )(dh2, da, h1, m, w_o, gpost, gffn)


def _my_place():
    return lax.axis_index("x"), lax.axis_index("y"), lax.axis_index("c")


def _block_index(px, py, pc):
    return 4 * px + 2 * py + pc


def allgather_pieces(shards, name):
    np_ = len(shards)

    def body(*refs):
        in_refs, out_refs = refs[:np_], refs[np_:2 * np_]
        send_sems, recv_sems, local_sems = refs[2 * np_:]
        x, y, c = _my_place()
        me, sibling = (x, y, c), (x, y, 1 - c)
        chips = [(1 - x, y), (x, 1 - y), (1 - x, 1 - y)]

        def rows(p, place):
            r = in_refs[p].shape[0]
            return out_refs[p].at[pl.ds(_block_index(*place) * r, r), :]

        def copy(p, k, block, to, src=None):
            return pltpu.make_async_remote_copy(
                src_ref=rows(p, block) if src is None else src, dst_ref=rows(p, block),
                send_sem=send_sems.at[p, k], recv_sem=recv_sems.at[p, k], device_id=to, device_id_type=MESH)

        mine = [pltpu.make_async_copy(in_refs[p], rows(p, me), local_sems.at[p]) for p in range(np_)]
        first, passed = [], []
        for p in range(np_):
            mine[p].start()
            first.append(copy(p, 0, me, sibling, src=in_refs[p]))
            first += [copy(p, 1 + j, me, (*chip, c), src=in_refs[p]) for j, chip in enumerate(chips)]
        for cp in first:
            cp.start()
        for p in range(np_):
            for j, chip in enumerate(chips):
                copy(p, 1 + j, (*chip, c), me).wait_recv()
                fwd = copy(p, 4 + j, (*chip, c), sibling)
                fwd.start()
                passed.append(fwd)
        for p in range(np_):
            copy(p, 0, sibling, me).wait_recv()
            for j, chip in enumerate(chips):
                copy(p, 4 + j, (*chip, 1 - c), me).wait_recv()
        for cp in first + passed:
            cp.wait_send()
        for cp in mine:
            cp.wait()

    return pl.pallas_call(
        functools.partial(body), name=name,
        in_specs=[ANY] * np_, out_specs=[ANY] * np_,
        out_shape=[jax.ShapeDtypeStruct((N_DEV * s.shape[0], s.shape[1]), s.dtype) for s in shards],
        scratch_shapes=[pltpu.SemaphoreType.DMA((np_, 7)), pltpu.SemaphoreType.DMA((np_, 7)),
                        pltpu.SemaphoreType.DMA((np_,))],
    )(*shards)


def _peers():
    x, y, c = _my_place()
    flips = [(fx, fy, fc) for fx in (0, 1) for fy in (0, 1) for fc in (0, 1)][1:]
    return [(1 - x if fx else x, 1 - y if fy else y, 1 - c if fc else c) for fx, fy, fc in flips]


HBM = pl.BlockSpec(memory_space=pltpu.HBM)
SEM = pl.BlockSpec(memory_space=pltpu.SEMAPHORE)


def _exchange_windows(scatter, src_ref, land_ref, my_block, peer_block):
    if scatter:
        r = land_ref.shape[1]
        return src_ref.at[pl.ds(peer_block * r, r), :], land_ref.at[my_block], land_ref.at[peer_block]
    r = src_ref.shape[0]
    return src_ref, land_ref.at[pl.ds(my_block * r, r), :], land_ref.at[pl.ds(peer_block * r, r), :]


def _own_copy(scatter, src_ref, land_ref, my_block, sem):
    if scatter:
        r = land_ref.shape[1]
        return pltpu.make_async_copy(src_ref.at[pl.ds(my_block * r, r), :], land_ref.at[my_block], sem)
    r = src_ref.shape[0]
    return pltpu.make_async_copy(src_ref, land_ref.at[pl.ds(my_block * r, r), :], sem)


def exchange_start(srcs, lands, after, scatter, name):
    np_ = len(srcs)

    def body(*refs):
        src_refs, land_refs = refs[:np_], refs[np_:2 * np_]
        send_sems, recv_sems, own_sems = refs[2 * np_ + 1:2 * np_ + 4]
        token = refs[-1]
        my_block = _block_index(*_my_place())
        for p in range(np_):
            _own_copy(scatter, src_refs[p], land_refs[p], my_block, own_sems.at[p]).start()
            for k, peer in enumerate(_peers()):
                src, dst, _ = _exchange_windows(scatter, src_refs[p], land_refs[p], my_block, _block_index(*peer))
                pltpu.make_async_remote_copy(src_ref=src, dst_ref=dst, send_sem=send_sems.at[7 * p + k],
                                             recv_sem=recv_sems.at[7 * p + k], device_id=peer, device_id_type=MESH).start()
        token[...] = jnp.zeros_like(token)

    hbm = lambda a: pltpu.with_memory_space_constraint(a, pltpu.HBM)
    outs = pl.pallas_call(
        functools.partial(body), name=name,
        in_specs=[HBM] * (2 * np_) + [ANY],
        out_specs=[SEM, SEM, SEM] + [HBM] * (2 * np_) + [pl.BlockSpec(memory_space=pltpu.VMEM)],
        out_shape=[pltpu.SemaphoreType.DMA((7 * np_,)), pltpu.SemaphoreType.DMA((7 * np_,)), pltpu.SemaphoreType.DMA((np_,))]
        + [pltpu.HBM(a.shape, a.dtype) for a in list(srcs) + list(lands)] + [jax.ShapeDtypeStruct((8, LANES), F32)],
        input_output_aliases={i: 3 + i for i in range(2 * np_)},
        compiler_params=pltpu.CompilerParams(has_side_effects=pltpu.SideEffectType.DATAFLOW_SIDE_EFFECTING),
    )(*[hbm(a) for a in srcs], *[hbm(a) for a in lands], after)
    return dict(sems=outs[:3], srcs=outs[3:3 + np_], lands=outs[3 + np_:3 + 2 * np_], token=outs[-1], scatter=scatter)


def exchange_wait(started, after, name):
    afters = tuple(after) if isinstance(after, (tuple, list)) else (after,)
    srcs, lands = started["srcs"], started["lands"]
    scatter = started["scatter"]
    np_ = len(srcs)

    def body(*refs):
        src_refs, land_refs = refs[:np_], refs[np_:2 * np_]
        send_sems, recv_sems, own_sems = refs[2 * np_:2 * np_ + 3]
        my_block = _block_index(*_my_place())
        for p in range(np_):
            _own_copy(scatter, src_refs[p], land_refs[p], my_block, own_sems.at[p]).wait()
            for k, peer in enumerate(_peers()):
                src, dst, arrival = _exchange_windows(scatter, src_refs[p], land_refs[p], my_block, _block_index(*peer))
                pltpu.make_async_remote_copy(src_ref=src, dst_ref=dst, send_sem=send_sems.at[7 * p + k],
                                             recv_sem=recv_sems.at[7 * p + k], device_id=peer, device_id_type=MESH).wait_send()
                pltpu.make_async_remote_copy(src_ref=src, dst_ref=arrival, send_sem=send_sems.at[7 * p + k],
                                             recv_sem=recv_sems.at[7 * p + k], device_id=peer, device_id_type=MESH).wait_recv()

    outs = pl.pallas_call(
        functools.partial(body), name=name,
        in_specs=[HBM] * (2 * np_) + [SEM, SEM, SEM] + [ANY] * len(afters),
        out_specs=[HBM] * (2 * np_),
        out_shape=[pltpu.HBM(a.shape, a.dtype) for a in list(srcs) + list(lands)],
        input_output_aliases={i: i for i in range(2 * np_)},
        compiler_params=pltpu.CompilerParams(has_side_effects=pltpu.SideEffectType.DATAFLOW_SIDE_EFFECTING),
    )(*srcs, *lands, *started["sems"], *afters)
    return list(outs[np_:])


def _gather_zone(shard):
    return lax.empty((N_DEV * shard.shape[0], shard.shape[1]), shard.dtype)


def _scatter_zone(full):
    return lax.empty((N_DEV, full.shape[0] // N_DEV, full.shape[1]), full.dtype)


def allreduce_small(pack):
    r, c = pack.shape

    def body(pack_ref, out_ref, gathered, send_sems, recv_sems):
        me = _my_place()
        my_block = _block_index(*me)
        peers = _peers()

        def copy(k, slot, to):
            return pltpu.make_async_remote_copy(
                src_ref=pack_ref, dst_ref=gathered.at[slot], send_sem=send_sems.at[k], recv_sem=recv_sems.at[k],
                device_id=to, device_id_type=MESH)

        sends = [copy(k, my_block, peer) for k, peer in enumerate(peers)]
        for cp in sends:
            cp.start()
        gathered[my_block] = pack_ref[...]
        for k, peer in enumerate(peers):
            copy(k, _block_index(*peer), peer).wait_recv()
        for cp in sends:
            cp.wait_send()
        total = gathered[0]
        for j in range(1, N_DEV):
            total = total + gathered[j]
        out_ref[...] = total

    return pl.pallas_call(
        functools.partial(body), name="allreduce_small",
        in_specs=[pl.BlockSpec(memory_space=pltpu.VMEM)], out_specs=pl.BlockSpec(memory_space=pltpu.VMEM),
        out_shape=jax.ShapeDtypeStruct((r, c), F32),
        scratch_shapes=[pltpu.VMEM((N_DEV, r, c), F32), pltpu.SemaphoreType.DMA((7,)), pltpu.SemaphoreType.DMA((7,))],
    )(pack)


def adamw(w, m, v, parts, layer=0, prev=None):
    nl, r, c = w.shape
    n = parts.shape[0]
    br = 256 if r % 256 == 0 else r
    blk = pl.BlockSpec((None, br, c), lambda i: (layer, i, 0))
    n_prev = 0 if prev is None else 4

    def body(w_ref, m_ref, v_ref, p_ref, *rest):
        g_ref, d_ref, nm_ref, nv_ref = rest[n_prev:]
        g = p_ref[0].astype(F32)
        for j in range(1, n):
            g = g + p_ref[j].astype(F32)
        nm = ADAM_B1 * m_ref[...] + (1.0 - ADAM_B1) * g
        nv = ADAM_B2 * v_ref[...] + (1.0 - ADAM_B2) * (g * g)
        m_hat = nm / (1.0 - ADAM_B1 ** ADAM_STEP)
        v_hat = nv / (1.0 - ADAM_B2 ** ADAM_STEP)
        g_ref[...] = g
        d_ref[...] = -ADAM_LR * (m_hat / (jnp.sqrt(v_hat) + ADAM_EPS) + ADAM_WD * w_ref[...])
        nm_ref[...] = nm
        nv_ref[...] = nv

    return pl.pallas_call(
        functools.partial(body), name="adamw", grid=(r // br,),
        in_specs=[blk] * 3 + [pl.BlockSpec((n, br, c), lambda i: (0, i, 0))] + [ANY] * n_prev,
        out_specs=[blk] * 4, out_shape=[jax.ShapeDtypeStruct((nl, r, c), F32)] * 4,
        input_output_aliases={4 + k: k for k in range(n_prev)},
        compiler_params=_cparams(("parallel",)),
    )(w, m, v, parts, *(prev or ()))


def _pair_heads(a, axis, width=HEAD_DIM):
    shp = a.shape
    a = a.reshape(shp[:axis] + (2, 2, GQA, width) + shp[axis + 1:])
    return jnp.swapaxes(a, axis + 1, axis + 2).reshape(shp)


def _unpair_heads(a, axis, width=HEAD_DIM):
    shp = a.shape
    a = a.reshape(shp[:axis] + (2, GQA, 2, width) + shp[axis + 1:])
    return jnp.swapaxes(a, axis + 1, axis + 2).reshape(shp)


def _pad_rows(a, rows=8):
    return jnp.pad(a, ((0, rows - a.shape[0]), (0, 0)))


def kernel(x, p, mix_pre_g, mix_post_g, ffn_pre_g, ffn_post_g, pool_w, pool_scale, kv_norm_g, w_k, w_v, w_q, w_o, sinks, w_ff_gate, w_ff_up, w_ff_down, ple_norm_g, w_ple_gate, w_ple_proj, loss_target, m_mix_pre_g, m_mix_post_g, m_ffn_pre_g, m_ffn_post_g, m_pool_w, m_pool_scale, m_kv_norm_g, m_w_k, m_w_v, m_w_q, m_w_o, m_sinks, m_w_ff_gate, m_w_ff_up, m_w_ff_down, m_ple_norm_g, m_w_ple_gate, m_w_ple_proj, v_mix_pre_g, v_mix_post_g, v_ffn_pre_g, v_ffn_post_g, v_pool_w, v_pool_scale, v_kv_norm_g, v_w_k, v_w_v, v_w_q, v_w_o, v_sinks, v_w_ff_gate, v_w_ff_up, v_w_ff_down, v_ple_norm_g, v_w_ple_gate, v_w_ple_proj):
    depth = w_ff_gate.shape[0]
    n_a = pool_w.shape[0]
    t, d = x.shape[1], x.shape[2]
    h = x[0]
    tgt = loss_target[0]
    p_all = p.reshape(depth * t, p.shape[-1])
    my_block = _block_index(*_my_place())
    row = lambda g, i: g[i][None, :]
    bf = lambda a: a.astype(BF16)

    full, gathers = [None] * depth, {}
    start_tokens = jnp.zeros((), F32)
    for i in range(depth):
        shards = [bf(w_ff_gate[i].T), bf(w_ff_up[i].T), bf(w_ff_down[i]), bf(w_ple_gate[i]), bf(w_ple_proj[i].T)]
        if i == 0:
            pool0, scale_full = allgather_pieces([bf(pool_w[0].reshape(-1, POOL_GROUP)), _pad_rows(pool_scale)],
                                                 "allgather_pool0")
            order = pool0
        elif i < n_a:
            shards.append(bf(pool_w[i].reshape(-1, POOL_GROUP)))
        else:
            shards += [bf(_pair_heads(w_q[i - n_a], 1)), bf(w_o[i - n_a])]
            if i == n_a:
                shards.append(bf(jnp.concatenate([w_k, w_v], axis=1)))
        gathers[i] = exchange_start(shards, [_gather_zone(s) for s in shards], order, False, f"allgather_start_l{i}")
        order = gathers[i]["token"]
        start_tokens = start_tokens + order[0, 0]
    scale_full = scale_full.reshape(N_DEV, 8, -1)[:, :n_a].transpose(1, 0, 2).reshape(n_a, 1, d)

    cos, sin = _rope_tables(t, start_tokens)
    sink_b = [jnp.broadcast_to(_pair_heads(sinks[j][:, None], 0, 1), (N_HEADS, LANES)) for j in range(depth - n_a)]
    pool_full, wo_full = {}, {}

    saved = []
    kv = hk = None
    pre = {}
    for i in range(depth):
        if i > 0 and full[i] is None:
            full[i] = exchange_wait(gathers[i], h, f"allgather_wait_l{i}")
        s = {"h0": h}
        if i < n_a:
            pool_full[i] = ((pool0 if i == 0 else full[i][5]).reshape(N_DEV, len(POOL_WINDOWS), -1, POOL_GROUP)
                            .transpose(1, 0, 2, 3).reshape(len(POOL_WINDOWS), POOL_GROUP, POOL_GROUP))
            gpre = row(mix_pre_g, i) + start_tokens if i == 0 else row(mix_pre_g, i)
            h1, a = pool_mix_fwd(h, gpre, pool_full[i], scale_full[i], row(mix_post_g, i), row(ffn_pre_g, i))
            if i == 0:
                full[0] = exchange_wait(gathers[0], (h1, cos, sin), "allgather_wait_l0")
        else:
            j = i - n_a
            wo_full[i] = _pair_heads(full[i][6], 0)
            if i == n_a:
                hk, kv = pre[i][2:]
            hn, q = pre[i][:2]
            attn, m, h1, a = swa_fwd(q, kv, sink_b[j], wo_full[i], h, row(mix_post_g, i), row(ffn_pre_g, i))
            s.update(hn=hn, q=q, attn=attn, m=m)
        wg_t, wu_t, wd, wpg, wpp_t = full[i][:5]
        f, gte, up, hdn = ffn_fwd(a, wg_t, wu_t, wd)
        s.update(h1=h1, a=a, f=f, gte=gte, up=up, hdn=hdn)
        if i < depth - 1:
            proj = []
            if i + 1 >= n_a:
                full[i + 1] = exchange_wait(gathers[i + 1], f, f"allgather_wait_l{i + 1}")
                proj = [(row(mix_pre_g, i + 1), full[i + 1][5], d)]
                if i + 1 == n_a:
                    proj.append((kv_norm_g[None, :], full[i + 1][7], N_KV_HEADS * HEAD_DIM))
            h, *pre[i + 1] = post_ple_fwd(h1, f, p_all, i, row(ffn_post_g, i), row(ple_norm_g, i), wpg, wpp_t,
                                          cos, sin, proj)
        saved.append(s)

    g_mix_pre, g_mix_post, g_ffn_pre, g_ffn_post, g_ple = ([None] * depth for _ in range(5))
    g_kv = g_sinks = None
    g_scale = [None] * n_a
    landing, scatters = [None] * depth, {}
    dkv_sum = []
    scatter_token = jnp.zeros((), F32)
    for i in reversed(range(depth)):
        s = saved[i]
        wg_t, wu_t, wd, wpg, wpp_t = full[i][:5]
        last = i == depth - 1
        dh2, df, ub, dzb, dppb, gacc = post_ple_bwd(tgt if last else dh, s["h1"], s["f"], p_all, i,
                                                    row(ffn_post_g, i) + scatter_token, row(ple_norm_g, i), wpg, wpp_t,
                                                    from_target=last)
        g_ple[i], g_ffn_post[i] = gacc[0], gacc[1]
        if last:
            loss_row = gacc[2][None, :]
        da, dgte, dup = ffn_bwd_act(df, s["gte"], s["up"], wg_t, wu_t, wd)
        grads = [xty(dgte, s["a"]), xty(dup, s["a"]), xty(s["hdn"], df), xty(ub, dzb), xty(dppb, p_all, i)]
        early = exchange_start(grads, [_scatter_zone(g) for g in grads], dh2, True, f"reduce_scatter_start_l{i}a")
        early_token = early["token"][0, 0]
        if i < n_a:
            dh, dpw, gacc = pool_mix_bwd(s["h0"], dh2, da, row(mix_pre_g, i) + early_token, pool_full[i], scale_full[i],
                                         row(mix_post_g, i), row(ffn_pre_g, i))
            g_mix_pre[i], g_mix_post[i], g_ffn_pre[i], g_scale[i] = gacc[0], gacc[1], gacc[2], gacc[3]
            dpw = dpw.reshape(len(POOL_WINDOWS), N_DEV, -1, POOL_GROUP).transpose(1, 0, 2, 3)
            grads = [bf(dpw.reshape(-1, POOL_GROUP))]
        else:
            j = i - n_a
            dh1, dmb, dattn, gacc = oproj_post_bwd(dh2, da, s["h1"], s["m"], wo_full[i], row(mix_post_g, i) + early_token,
                                                   row(ffn_pre_g, i))
            g_mix_post[i], g_ffn_pre[i] = gacc[0], gacc[1]
            dq, dkv, dsink = swa_bwd(s["q"], kv, dattn, sink_b[j])
            dkv_sum.append(dkv)
            g_sinks = [_unpair_heads(dsink[:, 0:1], 0, 1)[:, 0]] + (g_sinks or [])
            branches = [(row(mix_pre_g, i), full[i][5], d, [dq])]
            if i == n_a:
                branches.append((kv_norm_g[None, :], full[i][7], N_KV_HEADS * HEAD_DIM, dkv_sum))
            outs = proj_rope_bwd(dh1, s["h0"], cos, sin, branches, f"proj_bwd_l{i}")
            dh, gacc = outs[0], outs[-1]
            g_mix_pre[i] = gacc[0]
            grads = [xty(s["hn"], outs[1]), _unpair_heads(xty(s["attn"], dmb), 0)]
            if i == n_a:
                g_kv = gacc[1]
                grads.append(xty(hk, outs[2]))
        late = exchange_start(grads, [_scatter_zone(g) for g in grads], dh, True, f"reduce_scatter_start_l{i}b")
        scatter_token = late["token"][0, 0]
        scatters[i] = (early, late)
    grad_x = dh[None]

    def scatter_wait(i, after):
        return (exchange_wait(scatters[i][0], after, f"reduce_scatter_wait_l{i}a")
                + exchange_wait(scatters[i][1], after, f"reduce_scatter_wait_l{i}b"))

    after = dh
    for i in reversed(range(1, depth)):
        landing[i] = scatter_wait(i, after)
        after = landing[i][0]

    sink_row = jnp.pad(jnp.concatenate(g_sinks)[None, :], ((0, 0), (0, d - sinks.size)))
    stack = lambda rows_: _pad_rows(jnp.stack(rows_))
    pack = jnp.concatenate([stack(g_mix_pre), stack(g_mix_post), stack(g_ffn_pre), stack(g_ffn_post), stack(g_ple),
                            _pad_rows(g_kv[None]), stack(g_scale), _pad_rows(sink_row), _pad_rows(loss_row)], axis=0)
    tot = allreduce_small(pack)
    sec = lambda k, n: tot[8 * k:8 * k + n]
    loss = jnp.sum(tot[64])
    small = {
        "mix_pre_g": sec(0, depth), "mix_post_g": sec(1, depth), "ffn_pre_g": sec(2, depth),
        "ffn_post_g": sec(3, depth), "ple_norm_g": sec(4, depth), "kv_norm_g": tot[40],
        "pool_scale": lax.dynamic_slice_in_dim(sec(6, n_a), my_block * pool_scale.shape[1], pool_scale.shape[1], axis=1),
        "sinks": tot[56, :sinks.size].reshape(sinks.shape),
    }

    weights = dict(mix_pre_g=mix_pre_g, mix_post_g=mix_post_g, ffn_pre_g=ffn_pre_g, ffn_post_g=ffn_post_g, pool_w=pool_w, pool_scale=pool_scale, kv_norm_g=kv_norm_g, w_k=w_k, w_v=w_v, w_q=w_q, w_o=w_o, sinks=sinks, w_ff_gate=w_ff_gate, w_ff_up=w_ff_up, w_ff_down=w_ff_down, ple_norm_g=ple_norm_g, w_ple_gate=w_ple_gate, w_ple_proj=w_ple_proj)
    mom1 = dict(mix_pre_g=m_mix_pre_g, mix_post_g=m_mix_post_g, ffn_pre_g=m_ffn_pre_g, ffn_post_g=m_ffn_post_g, pool_w=m_pool_w, pool_scale=m_pool_scale, kv_norm_g=m_kv_norm_g, w_k=m_w_k, w_v=m_w_v, w_q=m_w_q, w_o=m_w_o, sinks=m_sinks, w_ff_gate=m_w_ff_gate, w_ff_up=m_w_ff_up, w_ff_down=m_w_ff_down, ple_norm_g=m_ple_norm_g, w_ple_gate=m_w_ple_gate, w_ple_proj=m_w_ple_proj)
    mom2 = dict(mix_pre_g=v_mix_pre_g, mix_post_g=v_mix_post_g, ffn_pre_g=v_ffn_pre_g, ffn_post_g=v_ffn_post_g, pool_w=v_pool_w, pool_scale=v_pool_scale, kv_norm_g=v_kv_norm_g, w_k=v_w_k, w_v=v_w_v, w_q=v_w_q, w_o=v_w_o, sinks=v_sinks, w_ff_gate=v_w_ff_gate, w_ff_up=v_w_ff_up, w_ff_down=v_w_ff_down, ple_norm_g=v_ple_norm_g, w_ple_gate=v_w_ple_gate, w_ple_proj=v_w_ple_proj)

    swap = lambda a: jnp.swapaxes(a, 1, 2)
    same = lambda a: a
    att = range(n_a, depth)
    for ws in (weights, mom1, mom2):
        ws["w_kv"] = jnp.concatenate([ws["w_k"], ws["w_v"]], axis=1)[None]
    plan = {
        "w_ff_gate": (swap, swap, [(i, i, 0) for i in range(depth)]),
        "w_ff_up": (swap, swap, [(i, i, 1) for i in range(depth)]),
        "w_ff_down": (same, same, [(i, i, 2) for i in range(depth)]),
        "w_ple_gate": (same, same, [(i, i, 3) for i in range(depth)]),
        "w_ple_proj": (swap, swap, [(i, i, 4) for i in range(depth)]),
        "pool_w": (lambda a: a.reshape(n_a, -1, POOL_GROUP), lambda a: a.reshape(pool_w.shape),
                   [(i, i, 5) for i in range(n_a)]),
        "w_q": (lambda a: _pair_heads(a, 2), lambda a: _unpair_heads(a, 2), [(i - n_a, i, 5) for i in att]),
        "w_o": (same, same, [(i - n_a, i, 6) for i in att]),
        "w_kv": (same, same, [(0, n_a, 7)]),
    }
    chains = {}

    def update_layer(layer):
        for nme, (view, _, where) in plan.items():
            for idx, li, k in where:
                if li == layer:
                    chains[nme] = adamw(view(weights[nme]), view(mom1[nme]), view(mom2[nme]), landing[li][k], idx,
                                        chains.get(nme))

    for layer in reversed(range(1, depth)):
        update_layer(layer)
    landing[0] = scatter_wait(0, chains["w_ff_down"][0] if depth > 1 else after)
    update_layer(0)
    results = {nme: [plan[nme][1](o) for o in outs] for nme, outs in chains.items()}
    results["w_k"] = [o[0, :, :w_k.shape[1]] for o in results["w_kv"]]
    results["w_v"] = [o[0, :, w_k.shape[1]:] for o in results["w_kv"]]
    for nme, g in small.items():
        w = weights[nme]
        flat = lambda a: a.reshape((1, -1, a.shape[-1]))
        results[nme] = [o.reshape(w.shape) for o in adamw(flat(w), flat(mom1[nme]), flat(mom2[nme]), flat(g))]

    order = ["mix_pre_g", "mix_post_g", "ffn_pre_g", "ffn_post_g", "pool_w", "pool_scale", "kv_norm_g", "w_k", "w_v",
             "w_q", "w_o", "sinks", "w_ff_gate", "w_ff_up", "w_ff_down", "ple_norm_g", "w_ple_gate", "w_ple_proj"]
    g_out, d_out, m_out, v_out = ([results[nme][k] for nme in order] for k in range(4))
    return (loss, grad_x, *g_out, *d_out, *m_out, *v_out)
```

```python
import functools

import jax
import jax.numpy as jnp
from jax import lax
from jax.experimental import pallas as pl
from jax.experimental.pallas import tpu as pltpu

F32 = jnp.float32
BF16 = jnp.bfloat16

N_DEV = 8
HEAD_DIM = 64
N_HEADS = 16
N_KV_HEADS = 4
GQA = N_HEADS // N_KV_HEADS
BLOCK = 128
POOL_WINDOWS = (2, 4, 8, 16)
POOL_GROUP = 256
HALO = 16
ROPE_THETA = 10000.0
RMS_EPS = 1e-6
NEG_INF = -1e30
LANES = 128
ATTN_SUB = 8
XTY_ROWS = 2048
FFN_CHUNK = 768
VMEM_LIMIT = 56 * 1024 * 1024

ADAM_LR = 0.001
ADAM_B1 = 0.9
ADAM_B2 = 0.999
ADAM_EPS = 1e-08
ADAM_WD = 0.01
ADAM_STEP = 10

MESH = pl.DeviceIdType.MESH
ANY = pl.BlockSpec(memory_space=pl.ANY)

NT_DIMS = (((1,), (1,)), ((), ()))
TN_DIMS = (((0,), (0,)), ((), ()))


def _cparams(sem=None, vmem=None):
    kw = {}
    if sem is not None:
        kw["dimension_semantics"] = sem
    if vmem is not None:
        kw["vmem_limit_bytes"] = vmem
    return pltpu.CompilerParams(**kw)


def _rows(tm, n, first=0):
    return pl.BlockSpec((tm, n), lambda i: (i + first, 0))


def _rows_rev(tm, n, nt):
    return pl.BlockSpec((tm, n), lambda i: (nt - 1 - i, 0))


def _const(shape):
    nd = len(shape)
    return pl.BlockSpec(shape, lambda *_: (0,) * nd, pipeline_mode=pl.Buffered(1))


def _resident(shape):
    nd = len(shape)
    return pl.BlockSpec(shape, lambda *_: (0,) * nd)


def _tile_rows(t):
    return 512 if t % 512 == 0 else 128


def _dot(a, b):
    return jnp.dot(a, b, preferred_element_type=F32)


def _dot_nt(a, b):
    return lax.dot_general(a, b, NT_DIMS, preferred_element_type=F32)


def _dot_tn(a, b):
    return lax.dot_general(a, b, TN_DIMS, preferred_element_type=F32)


def _rms_r(x):
    return lax.rsqrt(jnp.mean(x * x, axis=-1, keepdims=True) + RMS_EPS)


def _rms_bwd(x, r, g, dy):
    gy = dy * g
    dx = r * gy - x * (r * r * r * jnp.mean(gy * x, axis=-1, keepdims=True))
    dg = jnp.sum(dy * (x * r), axis=0, keepdims=True)
    return dx, dg


def _sigmoid(x):
    return jax.nn.sigmoid(x)


def _rope_tables(t, zero_token):
    inv = 1.0 / (ROPE_THETA ** (jnp.arange(0, HEAD_DIM, 2, dtype=F32) / HEAD_DIM))
    ang = (jnp.arange(t, dtype=F32) + zero_token)[:, None] * jnp.tile(inv, 2 * LANES // HEAD_DIM)[None, :]
    sign = jnp.tile(jnp.repeat(jnp.array([-1.0, 1.0], F32), HEAD_DIM // 2), LANES // HEAD_DIM)
    return jnp.cos(ang), jnp.sin(ang) * sign[None, :]


def _swap_halves(x):
    n = x.shape[1]
    lane = lax.broadcasted_iota(jnp.int32, x.shape, 1)
    first = (lane % HEAD_DIM) < (HEAD_DIM // 2)
    return jnp.where(first, pltpu.roll(x, n - HEAD_DIM // 2, 1), pltpu.roll(x, HEAD_DIM // 2, 1))


def _rope(x, cos, sin):
    reps = x.shape[1] // LANES
    return x * jnp.tile(cos, (1, reps)) + _swap_halves(x) * jnp.tile(sin, (1, reps))


def _unrope(dy, cos, sin):
    reps = dy.shape[1] // LANES
    return dy * jnp.tile(cos, (1, reps)) + _swap_halves(dy * jnp.tile(sin, (1, reps)))


def _acc_init(acc_ref):
    @pl.when(pl.program_id(0) == 0)
    def _():
        acc_ref[...] = jnp.zeros_like(acc_ref)


def _window_sums(ext, tm, forward):
    n = tm + HALO
    out = []
    for g, w in enumerate(POOL_WINDOWS):
        s = ext[:, g * POOL_GROUP:(g + 1) * POOL_GROUP]
        k = 1
        while k < w:
            s = s + pltpu.roll(s, k if forward else n - k, 0)
            k *= 2
        out.append(s[HALO:, :] if forward else s[:tm, :])
    return out


def _pool_inv_counts(tile, tm):
    t = tile * tm + lax.broadcasted_iota(jnp.int32, (tm, 1), 0)
    return [1.0 / jnp.minimum(t + 1, w).astype(F32) for w in POOL_WINDOWS]


def _pool_mix(hn, ext, inv_cnts, pw_ref, scale, tm):
    sums = _window_sums(ext, tm, True)
    pooled, ys = [], []
    for g in range(len(POOL_WINDOWS)):
        pg = (sums[g] * inv_cnts[g] - hn[:, g * POOL_GROUP:(g + 1) * POOL_GROUP]).astype(BF16)
        pooled.append(pg)
        ys.append(_dot(pg, pw_ref[g]))
    y = jnp.concatenate(ys, axis=1)
    return pooled, y, y * scale


def pool_mix_fwd(h0, gpre, pool_w, scale, gpost, gffn):
    t, d = h0.shape
    tm = _tile_rows(t)

    def body(h_ref, gpre_ref, pw_ref, scale_ref, gpost_ref, gffn_ref, h1_ref, a_ref, carry):
        i = pl.program_id(0)

        @pl.when(i == 0)
        def _():
            carry[...] = jnp.zeros_like(carry)

        x = h_ref[...]
        hn = x * _rms_r(x) * gpre_ref[...]
        ext = jnp.concatenate([carry[...], hn], axis=0)
        carry[...] = hn[tm - HALO:, :]
        _, _, m = _pool_mix(hn, ext, _pool_inv_counts(i, tm), pw_ref, scale_ref[...], tm)
        h1 = x + m * _rms_r(m) * gpost_ref[...]
        h1_ref[...] = h1
        a_ref[...] = (h1 * _rms_r(h1) * gffn_ref[...]).astype(BF16)

    return pl.pallas_call(
        functools.partial(body), name="pool_mix_fwd", grid=(t // tm,),
        in_specs=[_rows(tm, d), _const((1, d)), _const(pool_w.shape), _const((1, d)), _const((1, d)), _const((1, d))],
        out_specs=[_rows(tm, d), _rows(tm, d)],
        out_shape=[jax.ShapeDtypeStruct((t, d), F32), jax.ShapeDtypeStruct((t, d), BF16)],
        scratch_shapes=[pltpu.VMEM((HALO, d), F32)],
        compiler_params=_cparams(("arbitrary",), VMEM_LIMIT),
    )(h0, gpre, pool_w, scale, gpost, gffn)


def pool_mix_bwd(h0, dh2, da, gpre, pool_w, scale, gpost, gffn):
    t, d = h0.shape
    tm = _tile_rows(t)
    nt = t // tm
    hb = tm // HALO

    def body(h_ref, halo_ref, dh2_ref, da_ref, gpre_ref, pw_ref, scale_ref, gpost_ref, gffn_ref,
             dh0_ref, dpw_ref, gacc_ref, carry):
        i = pl.program_id(0)
        tile = nt - 1 - i
        _acc_init(gacc_ref)
        _acc_init(dpw_ref)

        @pl.when(i == 0)
        def _():
            carry[...] = jnp.zeros_like(carry)

        x = h_ref[...]
        gpre_v, scale_v, gpost_v, gffn_v = gpre_ref[...], scale_ref[...], gpost_ref[...], gffn_ref[...]
        r0 = _rms_r(x)
        hn = x * r0 * gpre_v
        xh = halo_ref[...]
        hn_halo = jnp.where(tile > 0, xh * _rms_r(xh) * gpre_v, 0.0)
        ext = jnp.concatenate([hn_halo, hn], axis=0)
        inv_cnts = _pool_inv_counts(tile, tm)
        pooled, y, m = _pool_mix(hn, ext, inv_cnts, pw_ref, scale_v, tm)
        rm = _rms_r(m)
        h1 = x + m * rm * gpost_v
        dh1_n, dgffn = _rms_bwd(h1, _rms_r(h1), gffn_v, da_ref[...].astype(F32))
        dh1 = dh2_ref[...] + dh1_n
        dm, dgpost = _rms_bwd(m, rm, gpost_v, dh1)
        dscale = jnp.sum(dm * y, axis=0, keepdims=True)
        dy = (dm * scale_v).astype(BF16)
        dpn = []
        for g in range(len(POOL_WINDOWS)):
            dyg = dy[:, g * POOL_GROUP:(g + 1) * POOL_GROUP]
            dpw_ref[g] += _dot_tn(pooled[g], dyg)
            dpn.append(_dot_nt(dyg, pw_ref[g]))
        dpooled = jnp.concatenate(dpn, axis=1)
        dpc = jnp.concatenate([dpn[g] * inv_cnts[g] for g in range(len(POOL_WINDOWS))], axis=1)
        ext2 = jnp.concatenate([dpc, carry[...]], axis=0)
        carry[...] = dpc[:HALO, :]
        dhn = jnp.concatenate(_window_sums(ext2, tm, False), axis=1) - dpooled
        dh0_n, dgpre = _rms_bwd(x, r0, gpre_v, dhn)
        dh0_ref[...] = dh1 + dh0_n
        gacc_ref[0:1, :] += dgpre
        gacc_ref[1:2, :] += dgpost
        gacc_ref[2:3, :] += dgffn
        gacc_ref[3:4, :] += dscale

    return pl.pallas_call(
        functools.partial(body), name="pool_mix_bwd", grid=(nt,),
        in_specs=[_rows_rev(tm, d, nt),
                  pl.BlockSpec((HALO, d), lambda i: (jnp.maximum((nt - 1 - i) * hb - 1, 0), 0)),
                  _rows_rev(tm, d, nt), _rows_rev(tm, d, nt),
                  _const((1, d)), _const(pool_w.shape), _const((1, d)), _const((1, d)), _const((1, d))],
        out_specs=[_rows_rev(tm, d, nt), _resident(pool_w.shape), _resident((8, d))],
        out_shape=[jax.ShapeDtypeStruct((t, d), F32), jax.ShapeDtypeStruct(pool_w.shape, F32),
                   jax.ShapeDtypeStruct((8, d), F32)],
        scratch_shapes=[pltpu.VMEM((HALO, d), F32)],
        compiler_params=_cparams(("arbitrary",), VMEM_LIMIT),
    )(h0, h0, dh2, da, gpre, pool_w, scale, gpost, gffn)


def _ffn_chunks(f):
    return [(c, min(c + FFN_CHUNK, f)) for c in range(0, f, FFN_CHUNK)]


def ffn_fwd(a, wg_t, wu_t, wd):
    t, d = a.shape
    f = wd.shape[0]
    tm = _tile_rows(t)

    def body(a_ref, wg_ref, wu_ref, wd_ref, f_ref, gte_ref, up_ref, hdn_ref):
        av = a_ref[...]
        acc = jnp.zeros((tm, d), F32)
        for c0, c1 in _ffn_chunks(f):
            gte = _dot_nt(av, wg_ref[c0:c1, :])
            up = _dot_nt(av, wu_ref[c0:c1, :])
            gte_ref[:, c0:c1] = gte.astype(BF16)
            up_ref[:, c0:c1] = up.astype(BF16)
            hdn = (gte * _sigmoid(gte) * up).astype(BF16)
            hdn_ref[:, c0:c1] = hdn
            acc = acc + _dot(hdn, wd_ref[c0:c1, :])
        f_ref[...] = acc.astype(BF16)

    return pl.pallas_call(
        functools.partial(body), name="ffn_fwd", grid=(t // tm,),
        in_specs=[_rows(tm, d), _const((f, d)), _const((f, d)), _const((f, d))],
        out_specs=[_rows(tm, d), _rows(tm, f), _rows(tm, f), _rows(tm, f)],
        out_shape=[jax.ShapeDtypeStruct((t, d), BF16)] + [jax.ShapeDtypeStruct((t, f), BF16)] * 3,
        compiler_params=_cparams(("parallel",), VMEM_LIMIT),
    )(a, wg_t, wu_t, wd)


def ffn_bwd_act(df, gte, up, wg_t, wu_t, wd):
    t, d = df.shape
    f = wd.shape[0]
    tm = _tile_rows(t)

    def body(df_ref, gte_ref, up_ref, wg_ref, wu_ref, wd_ref, da_ref, dgte_ref, dup_ref):
        dfv = df_ref[...]
        chunks = _ffn_chunks(f)
        half = chunks[len(chunks) // 2][0]
        acc = None
        for c0, c1 in chunks:
            g = gte_ref[:, c0:c1].astype(F32)
            u = up_ref[:, c0:c1].astype(F32)
            sg = _sigmoid(g)
            sl = g * sg
            dh = _dot_nt(dfv, wd_ref[c0:c1, :])
            dup_ref[:, c0:c1] = (dh * sl).astype(BF16)
            dgte_ref[:, c0:c1] = (dh * u * (sg * (1.0 + g * (1.0 - sg)))).astype(BF16)
            if c1 == half:
                acc = _dot(dgte_ref[:, :half], wg_ref[:half, :]) + _dot(dup_ref[:, :half], wu_ref[:half, :])
        da = acc + _dot(dgte_ref[:, half:], wg_ref[half:, :]) + _dot(dup_ref[:, half:], wu_ref[half:, :])
        da_ref[...] = da.astype(BF16)

    return pl.pallas_call(
        functools.partial(body), name="ffn_bwd_act", grid=(t // tm,),
        in_specs=[_rows(tm, d), _rows(tm, f), _rows(tm, f), _const((f, d)), _const((f, d)), _const((f, d))],
        out_specs=[_rows(tm, d), _rows(tm, f), _rows(tm, f)],
        out_shape=[jax.ShapeDtypeStruct((t, d), BF16)] + [jax.ShapeDtypeStruct((t, f), BF16)] * 2,
        compiler_params=_cparams(("parallel",), VMEM_LIMIT),
    )(df, gte, up, wg_t, wu_t, wd)


def xty(x, y, y_part=0):
    t, nx = x.shape
    ny = y.shape[1]
    tk = XTY_ROWS if t % XTY_ROWS == 0 else _tile_rows(t)
    bn = nx // 2 if nx > 1024 else nx
    nk = t // tk

    def body(x_ref, y_ref, o_ref, acc):
        k = pl.program_id(1)

        @pl.when(k == 0)
        def _():
            acc[...] = jnp.zeros_like(acc)

        acc[...] += _dot_tn(x_ref[...].astype(BF16), y_ref[...].astype(BF16))

        @pl.when(k == nk - 1)
        def _():
            o_ref[...] = acc[...].astype(BF16)

    return pl.pallas_call(
        functools.partial(body), name="xty", grid=(nx // bn, nk),
        in_specs=[pl.BlockSpec((tk, bn), lambda j, k: (k, j)),
                  pl.BlockSpec((tk, ny), lambda j, k: (k + y_part * nk, 0))],
        out_specs=pl.BlockSpec((bn, ny), lambda j, k: (j, 0)),
        out_shape=jax.ShapeDtypeStruct((nx, ny), BF16),
        scratch_shapes=[pltpu.VMEM((bn, ny), F32)],
        compiler_params=_cparams(("parallel", "arbitrary"), VMEM_LIMIT),
    )(x, y)


def _ple_fwd_tile(h1, f, p, gpost, gple, wpg_ref, wpp_ref):
    rf = _rms_r(f)
    h2 = h1 + f * rf * gpost
    r2 = _rms_r(h2)
    ub = (h2 * r2 * gple).astype(BF16)
    gate = _sigmoid(_dot(ub, wpg_ref[...]))
    pp = _dot_nt(p.astype(BF16), wpp_ref[...])
    return rf, h2, r2, ub, gate, pp


def post_ple_fwd(h1, f, p, layer, gpost, gple, wpg, wpp_t, cos=None, sin=None, proj=()):
    t, d = h1.shape
    pd = p.shape[1]
    tm = _tile_rows(t)
    nb = len(proj)

    def body(*refs):
        h1_ref, f_ref, p_ref, gpost_ref, gple_ref, wpg_ref, wpp_ref = refs[:7]
        pos = 9 if nb else 7
        out_ref = refs[pos + 2 * nb]
        _, h2, _, _, gate, pp = _ple_fwd_tile(h1_ref[...], f_ref[...].astype(F32), p_ref[...], gpost_ref[...],
                                              gple_ref[...], wpg_ref, wpp_ref)
        h3 = h2 + pp * gate
        out_ref[...] = h3
        if nb:
            r3 = _rms_r(h3)
        for b in range(nb):
            g_ref, w_ref = refs[pos + 2 * b], refs[pos + 2 * b + 1]
            hn_ref, y_ref = refs[pos + 2 * nb + 1 + 2 * b], refs[pos + 2 * nb + 2 + 2 * b]
            n_rope = proj[b][2]
            hn = (h3 * r3 * g_ref[...]).astype(BF16)
            hn_ref[...] = hn
            y = _dot(hn, w_ref[...])
            y_ref[:, :n_rope] = _rope(y[:, :n_rope], refs[7][...], refs[8][...]).astype(BF16)
            if n_rope < y.shape[1]:
                y_ref[:, n_rope:] = y[:, n_rope:].astype(BF16)

    in_specs = [_rows(tm, d), _rows(tm, d), _rows(tm, pd, layer * (t // tm)), _const((1, d)), _const((1, d)),
                _const(wpg.shape), _const(wpp_t.shape)]
    args = [h1, f, p, gpost, gple, wpg, wpp_t]
    out_specs, out_shape = [_rows(tm, d)], [jax.ShapeDtypeStruct((t, d), F32)]
    if nb:
        in_specs += [_rows(tm, LANES), _rows(tm, LANES)]
        args += [cos, sin]
    for gain, w, _ in proj:
        in_specs += [_const((1, d)), _const(w.shape)]
        args += [gain, w]
        out_specs += [_rows(tm, d), _rows(tm, w.shape[1])]
        out_shape += [jax.ShapeDtypeStruct((t, d), BF16), jax.ShapeDtypeStruct((t, w.shape[1]), BF16)]
    return pl.pallas_call(
        functools.partial(body), name="post_ple_proj_fwd" if nb else "post_ple_fwd", grid=(t // tm,),
        in_specs=in_specs, out_specs=out_specs, out_shape=out_shape,
        compiler_params=_cparams(("parallel",), VMEM_LIMIT),
    )(*args)


def post_ple_bwd(dh3, h1, f, p, layer, gpost, gple, wpg, wpp_t, from_target=False):
    t, d = h1.shape
    pd = p.shape[1]
    tm = _tile_rows(t)

    def body(dh3_hbm, h1_ref, f_ref, p_ref, gpost_ref, gple_ref, wpg_ref, wpp_ref,
             dh2_ref, df_ref, u_ref, dz_ref, dpp_ref, gacc_ref, ring, ring_sem):
        _acc_init(gacc_ref)
        step, nsteps = pl.program_id(0), t // tm

        def fetch(k):
            return pltpu.make_async_copy(dh3_hbm.at[pl.ds(k * tm, tm), :], ring.at[k % 3], ring_sem.at[k % 3])

        @pl.when(step == 0)
        def _():
            fetch(0).start()
            if nsteps > 1:
                fetch(1).start()

        @pl.when(step + 2 < nsteps)
        def _():
            fetch(step + 2).start()

        fetch(step).wait()
        dh3_ref = ring.at[step % 3]
        gpost_v, gple_v = gpost_ref[...], gple_ref[...]
        nsub = 2 if tm % 16 == 0 else 1
        for sb in range(nsub):
            rows = slice(sb * (tm // nsub), (sb + 1) * (tm // nsub))
            fv = f_ref[rows, :].astype(F32)
            rf, h2, r2, ub, gate, pp = _ple_fwd_tile(h1_ref[rows, :], fv, p_ref[rows, :], gpost_v, gple_v, wpg_ref,
                                                     wpp_ref)
            if from_target:
                err = h2 + pp * gate - dh3_ref[rows, :]
                dh3v = err * (1.0 / d)
                gacc_ref[2:3, :] += jnp.sum(err * err, axis=0, keepdims=True) * (0.5 / d)
            else:
                dh3v = dh3_ref[rows, :]
            dpp_ref[rows, :] = (dh3v * gate).astype(BF16)
            dz = (dh3v * pp * gate * (1.0 - gate)).astype(BF16)
            dz_ref[rows, :] = dz
            u_ref[rows, :] = ub
            du = _dot_nt(dz, wpg_ref[...])
            dh2_n, dgple = _rms_bwd(h2, r2, gple_v, du)
            dh2 = dh3v + dh2_n
            df, dgpost = _rms_bwd(fv, rf, gpost_v, dh2)
            dh2_ref[rows, :] = dh2
            df_ref[rows, :] = df.astype(BF16)
            gacc_ref[0:1, :] += dgple
            gacc_ref[1:2, :] += dgpost

    return pl.pallas_call(
        functools.partial(body), name="post_ple_loss_bwd" if from_target else "post_ple_bwd", grid=(t // tm,),
        in_specs=[ANY, _rows(tm, d), _rows(tm, d), _rows(tm, pd, layer * (t // tm)), _const((1, d)),
                  _const((1, d)), _const(wpg.shape), _const(wpp_t.shape)],
        out_specs=[_rows(tm, d)] * 5 + [_resident((8, d))],
        out_shape=[jax.ShapeDtypeStruct((t, d), F32)] + [jax.ShapeDtypeStruct((t, d), BF16)] * 4
        + [jax.ShapeDtypeStruct((8, d), F32)],
        scratch_shapes=[pltpu.VMEM((3, tm, d), F32), pltpu.SemaphoreType.DMA((3,))],
        compiler_params=_cparams(("arbitrary",), VMEM_LIMIT),
    )(dh3, h1, f, p, gpost, gple, wpg, wpp_t)


def proj_rope_bwd(dh1, h0, cos, sin, branches, name):
    t, d = h0.shape
    tm = _tile_rows(t)
    nb = len(branches)
    n_cot = [len(b[3]) for b in branches]

    def body(*refs):
        dh1_ref, h0_ref, cos_ref, sin_ref = refs[:4]
        pos = 4
        br_refs = []
        for b in range(nb):
            br_refs.append((refs[pos], refs[pos + 1], refs[pos + 2:pos + 2 + n_cot[b]]))
            pos += 2 + n_cot[b]
        dh0_ref = refs[pos]
        dpre_refs = refs[pos + 1:pos + 1 + nb]
        gacc_ref = refs[pos + 1 + nb]
        _acc_init(gacc_ref)
        x = h0_ref[...]
        r0 = _rms_r(x)
        dh = dh1_ref[...]
        for b in range(nb):
            g_ref, w_ref, cot_refs = br_refs[b]
            n_rope = branches[b][2]
            dy = cot_refs[0][...].astype(F32)
            for c_ref in cot_refs[1:]:
                dy = dy + c_ref[...].astype(F32)
            n = dy.shape[1]
            dpre_refs[b][:, :n_rope] = _unrope(dy[:, :n_rope], cos_ref[...], sin_ref[...]).astype(BF16)
            if n_rope < n:
                dpre_refs[b][:, n_rope:] = dy[:, n_rope:].astype(BF16)
            dhn = _dot_nt(dpre_refs[b][...], w_ref[...])
            dx, dg = _rms_bwd(x, r0, g_ref[...], dhn)
            dh = dh + dx
            gacc_ref[b:b + 1, :] += dg
        dh0_ref[...] = dh

    in_specs = [_rows(tm, d), _rows(tm, d), _rows(tm, LANES), _rows(tm, LANES)]
    args = [dh1, h0, cos, sin]
    out_specs = [_rows(tm, d)]
    out_shape = [jax.ShapeDtypeStruct((t, d), F32)]
    for gain, w, _, cots in branches:
        n = w.shape[1]
        in_specs += [_const((1, d)), _const(w.shape)] + [_rows(tm, n)] * len(cots)
        args += [gain, w] + list(cots)
        out_specs.append(_rows(tm, n))
        out_shape.append(jax.ShapeDtypeStruct((t, n), BF16))
    out_specs.append(_resident((8, d)))
    out_shape.append(jax.ShapeDtypeStruct((8, d), F32))
    return pl.pallas_call(
        functools.partial(body), name=name, grid=(t // tm,),
        in_specs=in_specs, out_specs=out_specs, out_shape=out_shape,
        compiler_params=_cparams(("arbitrary",), VMEM_LIMIT),
    )(*args)


def _tri():
    row = lax.broadcasted_iota(jnp.int32, (BLOCK, BLOCK), 0)
    col = lax.broadcasted_iota(jnp.int32, (BLOCK, BLOCK), 1)
    return col <= row


def _block_diag(x):
    lo = lax.broadcasted_iota(jnp.int32, x.shape, 1) < HEAD_DIM
    zero = jnp.zeros_like(x)
    return jnp.concatenate([jnp.where(lo, x, zero), jnp.where(lo, zero, x)], axis=0)


def _dense(x, tri):
    return (jnp.where(tri, x[:, BLOCK:2 * BLOCK], x[:, :BLOCK]),
            jnp.where(tri, x[:, 3 * BLOCK:], x[:, 2 * BLOCK:3 * BLOCK]))


def _banded(xa, xb, tri):
    zero = jnp.zeros_like(xa)
    return jnp.concatenate([jnp.where(tri, zero, xa), jnp.where(tri, xa, zero),
                            jnp.where(tri, zero, xb), jnp.where(tri, xb, zero)], axis=1).astype(BF16)


def _softmax_sink(s, sink):
    mx = jnp.maximum(jnp.max(s, axis=1, keepdims=True), sink)
    e = jnp.exp(s - mx)
    es = jnp.exp(sink - mx)
    inv = 1.0 / (jnp.sum(e, axis=1, keepdims=True) + es)
    return e * inv, es * inv


def _sink_column(sink_ref):
    return jnp.concatenate([jnp.broadcast_to(sink_ref[h:h + 1, 0:1], (BLOCK, 1)) for h in range(N_HEADS)], axis=0)


def _kv_block_diag(band, kvw):
    n_lt = kvw // LANES
    return ([_block_diag(band[:, lt * LANES:(lt + 1) * LANES]) for lt in range(n_lt)],
            [_block_diag(band[:, kvw + lt * LANES:kvw + (lt + 1) * LANES]) for lt in range(n_lt)])


def _all_probs(q_ref, r0, kbd, tri, n, sink_ref):
    dense = []
    for tq in range(N_HEADS // 2):
        s = _dot_nt(q_ref[r0:r0 + BLOCK, tq * LANES:(tq + 1) * LANES], kbd[tq // GQA])
        dense += list(_dense(s, tri))
    bias = jnp.where(jnp.logical_not(tri) & (n == 0), NEG_INF, 0.0)
    s_all = jnp.concatenate(dense, axis=0) * (HEAD_DIM ** -0.5) + jnp.concatenate([bias] * N_HEADS, axis=0)
    return _softmax_sink(s_all, _sink_column(sink_ref))


def _head_rows(x, tq):
    return x[2 * tq * BLOCK:(2 * tq + 1) * BLOCK], x[(2 * tq + 1) * BLOCK:(2 * tq + 2) * BLOCK]


def _attn_sub(t):
    return ATTN_SUB if t % (ATTN_SUB * BLOCK) == 0 else 1


def swa_fwd(q, kv, sink_b, w_o, h0, gpost, gffn):
    t, d = q.shape
    sub = _attn_sub(t)
    rows = sub * BLOCK
    kvw = N_KV_HEADS * HEAD_DIM

    def body(q_ref, kvc_ref, kvp_ref, sink_ref, w_ref, h0_ref, gpost_ref, gffn_ref, o_ref, m_ref, h1_ref, a_ref):
        i = pl.program_id(0)
        tri = _tri()
        ext = jnp.concatenate([kvp_ref[...], kvc_ref[...]], axis=0)
        for sb in range(sub):
            r0 = sb * BLOCK
            kbd, vbd = _kv_block_diag(ext[r0:r0 + 2 * BLOCK], kvw)
            p, _ = _all_probs(q_ref, r0, kbd, tri, i * sub + sb, sink_ref)
            for tq in range(N_HEADS // 2):
                pa, pb = _head_rows(p, tq)
                o_ref[r0:r0 + BLOCK, tq * LANES:(tq + 1) * LANES] = _dot(_banded(pa, pb, tri), vbd[tq // GQA]).astype(BF16)
            blk = slice(r0, r0 + BLOCK)
            m = _dot(o_ref[blk, :], w_ref[...])
            m_ref[blk, :] = m.astype(BF16)
            h1 = h0_ref[blk, :] + m * _rms_r(m) * gpost_ref[...]
            h1_ref[blk, :] = h1
            a_ref[blk, :] = (h1 * _rms_r(h1) * gffn_ref[...]).astype(BF16)

    return pl.pallas_call(
        functools.partial(body), name="swa_fwd", grid=(t // rows,),
        in_specs=[_rows(rows, d), _rows(rows, 2 * kvw),
                  pl.BlockSpec((BLOCK, 2 * kvw), lambda i: (jnp.maximum(i * sub - 1, 0), 0)), _const(sink_b.shape),
                  _const(w_o.shape), _rows(rows, d), _const((1, d)), _const((1, d))],
        out_specs=[_rows(rows, d)] * 4,
        out_shape=[jax.ShapeDtypeStruct((t, d), BF16), jax.ShapeDtypeStruct((t, d), BF16),
                   jax.ShapeDtypeStruct((t, d), F32), jax.ShapeDtypeStruct((t, d), BF16)],
        compiler_params=_cparams(("parallel",), VMEM_LIMIT),
    )(q, kv, kv, sink_b, w_o, h0, gpost, gffn)


def swa_bwd(q, kv, do, sink_b):
    t, d = q.shape
    sub = _attn_sub(t)
    nq = t // (sub * BLOCK)
    kvw = N_KV_HEADS * HEAD_DIM

    def body(q_ref, do_ref, kvc_ref, kvp_ref, sink_ref, dq_ref, dkv_ref, dsink_ref, carry):
        i = pl.program_id(0)
        step = nq - 1 - i
        _acc_init(dsink_ref)

        @pl.when(i == 0)
        def _():
            carry[...] = jnp.zeros_like(carry)

        tri = _tri()
        lo = lax.broadcasted_iota(jnp.int32, (2 * BLOCK, LANES), 1) < HEAD_DIM
        ext = jnp.concatenate([kvp_ref[...], kvc_ref[...]], axis=0)
        dkeys = [None] * (sub + 1)
        for sb in reversed(range(sub)):
            r0 = sb * BLOCK
            kbd, vbd = _kv_block_diag(ext[r0:r0 + 2 * BLOCK], kvw)
            p, ps = _all_probs(q_ref, r0, kbd, tri, step * sub + sb, sink_ref)
            dp = []
            for tq in range(N_HEADS // 2):
                dp += list(_dense(_dot_nt(do_ref[r0:r0 + BLOCK, tq * LANES:(tq + 1) * LANES], vbd[tq // GQA]), tri))
            dp = jnp.concatenate(dp, axis=0)
            delta = jnp.sum(p * dp, axis=1, keepdims=True)
            ds = p * (dp - delta) * (HEAD_DIM ** -0.5)
            dsk = ps * delta
            for h in range(N_HEADS):
                dsink_ref[h:h + 1, :] -= jnp.sum(dsk[h * BLOCK:(h + 1) * BLOCK], axis=0, keepdims=True)
            dkb = [jnp.zeros((4 * BLOCK, LANES), F32) for _ in kbd]
            dvb = [jnp.zeros((4 * BLOCK, LANES), F32) for _ in kbd]
            for tq in range(N_HEADS // 2):
                lt = tq // GQA
                cols = slice(tq * LANES, (tq + 1) * LANES)
                dsb = _banded(*_head_rows(ds, tq), tri)
                dq_ref[r0:r0 + BLOCK, cols] = _dot(dsb, kbd[lt]).astype(BF16)
                dkb[lt] = dkb[lt] + _dot_tn(dsb, q_ref[r0:r0 + BLOCK, cols])
                dvb[lt] = dvb[lt] + _dot_tn(_banded(*_head_rows(p, tq), tri), do_ref[r0:r0 + BLOCK, cols])
            dall = jnp.concatenate([jnp.where(lo, x[:2 * BLOCK], x[2 * BLOCK:]) for x in dkb + dvb], axis=1)
            dkeys[sb + 1] = dall[BLOCK:] if dkeys[sb + 1] is None else dkeys[sb + 1] + dall[BLOCK:]
            dkeys[sb] = dall[:BLOCK]
        for sb in range(sub):
            own = dkeys[sb + 1] + carry[...] if sb == sub - 1 else dkeys[sb + 1]
            dkv_ref[sb * BLOCK:(sb + 1) * BLOCK, :] = own
        carry[...] = dkeys[0]

    rev = lambda i: (nq - 1 - i, 0)
    return pl.pallas_call(
        functools.partial(body), name="swa_bwd", grid=(nq,),
        in_specs=[pl.BlockSpec((sub * BLOCK, d), rev), pl.BlockSpec((sub * BLOCK, d), rev),
                  pl.BlockSpec((sub * BLOCK, 2 * kvw), rev),
                  pl.BlockSpec((BLOCK, 2 * kvw), lambda i: (jnp.maximum((nq - 1 - i) * sub - 1, 0), 0)),
                  _const(sink_b.shape)],
        out_specs=[pl.BlockSpec((sub * BLOCK, d), rev), pl.BlockSpec((sub * BLOCK, 2 * kvw), rev),
                   _resident(sink_b.shape)],
        out_shape=[jax.ShapeDtypeStruct((t, d), BF16), jax.ShapeDtypeStruct((t, 2 * kvw), F32),
                   jax.ShapeDtypeStruct(sink_b.shape, F32)],
        scratch_shapes=[pltpu.VMEM((BLOCK, 2 * kvw), F32)],
        compiler_params=_cparams(("arbitrary",), VMEM_LIMIT),
    )(q, do, kv, kv, sink_b)


def oproj_post_bwd(dh2, da, h1, m, w_o, gpost, gffn):
    t, d = h1.shape
    tm = _tile_rows(t)

    def body(dh2_ref, da_ref, h1_ref, m_ref, w_ref, gpost_ref, gffn_ref, dh1_ref, dm_ref, dat_ref, gacc_ref):
        _acc_init(gacc_ref)
        h1v, mv = h1_ref[...], m_ref[...].astype(F32)
        dh1_n, dgffn = _rms_bwd(h1v, _rms_r(h1v), gffn_ref[...], da_ref[...].astype(F32))
        dh1 = dh2_ref[...] + dh1_n
        dm, dgpost = _rms_bwd(mv, _rms_r(mv), gpost_ref[...], dh1)
        dmb = dm.astype(BF16)
        dh1_ref[...] = dh1
        dm_ref[...] = dmb
        dat_ref[...] = _dot_nt(dmb, w_ref[...]).astype(BF16)
        gacc_ref[0:1, :] += dgpost
        gacc_ref[1:2, :] += dgffn

    return pl.pallas_call(
        functools.partial(body), name="oproj_post_bwd", grid=(t // tm,),
        in_specs=[_rows(tm, d)] * 4 + [_const(w_o.shape), _const((1, d)), _const((1, d))],
        out_specs=[_rows(tm, d)] * 3 + [_resident((8, d))],
        out_shape=[jax.ShapeDtypeStruct((t, d), F32), jax.ShapeDtypeStruct((t, d), BF16),
                   jax.ShapeDtypeStruct((t, d), BF16), jax.ShapeDtypeStruct((8, d), F32)],
        compiler_params=_cparams(("arbitrary",), VMEM_LIMIT),
    )(dh2, da, h1, m, w_o, gpost, gffn)


def _my_place():
    return lax.axis_index("x"), lax.axis_index("y"), lax.axis_index("c")


def _block_index(px, py, pc):
    return 4 * px + 2 * py + pc


def allgather_pieces(shards, name):
    np_ = len(shards)

    def body(*refs):
        in_refs, out_refs = refs[:np_], refs[np_:2 * np_]
        send_sems, recv_sems, local_sems = refs[2 * np_:]
        x, y, c = _my_place()
        me, sibling = (x, y, c), (x, y, 1 - c)
        chips = [(1 - x, y), (x, 1 - y), (1 - x, 1 - y)]

        def rows(p, place):
            r = in_refs[p].shape[0]
            return out_refs[p].at[pl.ds(_block_index(*place) * r, r), :]

        def copy(p, k, block, to, src=None):
            return pltpu.make_async_remote_copy(
                src_ref=rows(p, block) if src is None else src, dst_ref=rows(p, block),
                send_sem=send_sems.at[p, k], recv_sem=recv_sems.at[p, k], device_id=to, device_id_type=MESH)

        mine = [pltpu.make_async_copy(in_refs[p], rows(p, me), local_sems.at[p]) for p in range(np_)]
        first, passed = [], []
        for p in range(np_):
            mine[p].start()
            first.append(copy(p, 0, me, sibling, src=in_refs[p]))
            first += [copy(p, 1 + j, me, (*chip, c), src=in_refs[p]) for j, chip in enumerate(chips)]
        for cp in first:
            cp.start()
        for p in range(np_):
            for j, chip in enumerate(chips):
                copy(p, 1 + j, (*chip, c), me).wait_recv()
                fwd = copy(p, 4 + j, (*chip, c), sibling)
                fwd.start()
                passed.append(fwd)
        for p in range(np_):
            copy(p, 0, sibling, me).wait_recv()
            for j, chip in enumerate(chips):
                copy(p, 4 + j, (*chip, 1 - c), me).wait_recv()
        for cp in first + passed:
            cp.wait_send()
        for cp in mine:
            cp.wait()

    return pl.pallas_call(
        functools.partial(body), name=name,
        in_specs=[ANY] * np_, out_specs=[ANY] * np_,
        out_shape=[jax.ShapeDtypeStruct((N_DEV * s.shape[0], s.shape[1]), s.dtype) for s in shards],
        scratch_shapes=[pltpu.SemaphoreType.DMA((np_, 7)), pltpu.SemaphoreType.DMA((np_, 7)),
                        pltpu.SemaphoreType.DMA((np_,))],
    )(*shards)


def _peers():
    x, y, c = _my_place()
    flips = [(fx, fy, fc) for fx in (0, 1) for fy in (0, 1) for fc in (0, 1)][1:]
    return [(1 - x if fx else x, 1 - y if fy else y, 1 - c if fc else c) for fx, fy, fc in flips]


HBM = pl.BlockSpec(memory_space=pltpu.HBM)
SEM = pl.BlockSpec(memory_space=pltpu.SEMAPHORE)


def _exchange_windows(scatter, src_ref, land_ref, my_block, peer_block):
    if scatter:
        r = land_ref.shape[1]
        return src_ref.at[pl.ds(peer_block * r, r), :], land_ref.at[my_block], land_ref.at[peer_block]
    r = src_ref.shape[0]
    return src_ref, land_ref.at[pl.ds(my_block * r, r), :], land_ref.at[pl.ds(peer_block * r, r), :]


def _own_copy(scatter, src_ref, land_ref, my_block, sem):
    if scatter:
        r = land_ref.shape[1]
        return pltpu.make_async_copy(src_ref.at[pl.ds(my_block * r, r), :], land_ref.at[my_block], sem)
    r = src_ref.shape[0]
    return pltpu.make_async_copy(src_ref, land_ref.at[pl.ds(my_block * r, r), :], sem)


def exchange_start(srcs, lands, after, scatter, name):
    np_ = len(srcs)

    def body(*refs):
        src_refs, land_refs = refs[:np_], refs[np_:2 * np_]
        send_sems, recv_sems, own_sems = refs[2 * np_ + 1:2 * np_ + 4]
        token = refs[-1]
        my_block = _block_index(*_my_place())
        for p in range(np_):
            _own_copy(scatter, src_refs[p], land_refs[p], my_block, own_sems.at[p]).start()
            for k, peer in enumerate(_peers()):
                src, dst, _ = _exchange_windows(scatter, src_refs[p], land_refs[p], my_block, _block_index(*peer))
                pltpu.make_async_remote_copy(src_ref=src, dst_ref=dst, send_sem=send_sems.at[7 * p + k],
                                             recv_sem=recv_sems.at[7 * p + k], device_id=peer, device_id_type=MESH).start()
        token[...] = jnp.zeros_like(token)

    hbm = lambda a: pltpu.with_memory_space_constraint(a, pltpu.HBM)
    outs = pl.pallas_call(
        functools.partial(body), name=name,
        in_specs=[HBM] * (2 * np_) + [ANY],
        out_specs=[SEM, SEM, SEM] + [HBM] * (2 * np_) + [pl.BlockSpec(memory_space=pltpu.VMEM)],
        out_shape=[pltpu.SemaphoreType.DMA((7 * np_,)), pltpu.SemaphoreType.DMA((7 * np_,)), pltpu.SemaphoreType.DMA((np_,))]
        + [pltpu.HBM(a.shape, a.dtype) for a in list(srcs) + list(lands)] + [jax.ShapeDtypeStruct((8, LANES), F32)],
        input_output_aliases={i: 3 + i for i in range(2 * np_)},
        compiler_params=pltpu.CompilerParams(has_side_effects=pltpu.SideEffectType.DATAFLOW_SIDE_EFFECTING),
    )(*[hbm(a) for a in srcs], *[hbm(a) for a in lands], after)
    return dict(sems=outs[:3], srcs=outs[3:3 + np_], lands=outs[3 + np_:3 + 2 * np_], token=outs[-1], scatter=scatter)


def exchange_wait(started, after, name):
    afters = tuple(after) if isinstance(after, (tuple, list)) else (after,)
    srcs, lands = started["srcs"], started["lands"]
    scatter = started["scatter"]
    np_ = len(srcs)

    def body(*refs):
        src_refs, land_refs = refs[:np_], refs[np_:2 * np_]
        send_sems, recv_sems, own_sems = refs[2 * np_:2 * np_ + 3]
        my_block = _block_index(*_my_place())
        for p in range(np_):
            _own_copy(scatter, src_refs[p], land_refs[p], my_block, own_sems.at[p]).wait()
            for k, peer in enumerate(_peers()):
                src, dst, arrival = _exchange_windows(scatter, src_refs[p], land_refs[p], my_block, _block_index(*peer))
                pltpu.make_async_remote_copy(src_ref=src, dst_ref=dst, send_sem=send_sems.at[7 * p + k],
                                             recv_sem=recv_sems.at[7 * p + k], device_id=peer, device_id_type=MESH).wait_send()
                pltpu.make_async_remote_copy(src_ref=src, dst_ref=arrival, send_sem=send_sems.at[7 * p + k],
                                             recv_sem=recv_sems.at[7 * p + k], device_id=peer, device_id_type=MESH).wait_recv()

    outs = pl.pallas_call(
        functools.partial(body), name=name,
        in_specs=[HBM] * (2 * np_) + [SEM, SEM, SEM] + [ANY] * len(afters),
        out_specs=[HBM] * (2 * np_),
        out_shape=[pltpu.HBM(a.shape, a.dtype) for a in list(srcs) + list(lands)],
        input_output_aliases={i: i for i in range(2 * np_)},
        compiler_params=pltpu.CompilerParams(has_side_effects=pltpu.SideEffectType.DATAFLOW_SIDE_EFFECTING),
    )(*srcs, *lands, *started["sems"], *afters)
    return list(outs[np_:])


def _gather_zone(shard):
    return lax.empty((N_DEV * shard.shape[0], shard.shape[1]), shard.dtype)


def _scatter_zone(full):
    return lax.empty((N_DEV, full.shape[0] // N_DEV, full.shape[1]), full.dtype)


def allreduce_small(pack):
    r, c = pack.shape

    def body(pack_ref, out_ref, gathered, send_sems, recv_sems):
        me = _my_place()
        my_block = _block_index(*me)
        peers = _peers()

        def copy(k, slot, to):
            return pltpu.make_async_remote_copy(
                src_ref=pack_ref, dst_ref=gathered.at[slot], send_sem=send_sems.at[k], recv_sem=recv_sems.at[k],
                device_id=to, device_id_type=MESH)

        sends = [copy(k, my_block, peer) for k, peer in enumerate(peers)]
        for cp in sends:
            cp.start()
        gathered[my_block] = pack_ref[...]
        for k, peer in enumerate(peers):
            copy(k, _block_index(*peer), peer).wait_recv()
        for cp in sends:
            cp.wait_send()
        total = gathered[0]
        for j in range(1, N_DEV):
            total = total + gathered[j]
        out_ref[...] = total

    return pl.pallas_call(
        functools.partial(body), name="allreduce_small",
        in_specs=[pl.BlockSpec(memory_space=pltpu.VMEM)], out_specs=pl.BlockSpec(memory_space=pltpu.VMEM),
        out_shape=jax.ShapeDtypeStruct((r, c), F32),
        scratch_shapes=[pltpu.VMEM((N_DEV, r, c), F32), pltpu.SemaphoreType.DMA((7,)), pltpu.SemaphoreType.DMA((7,))],
    )(pack)


def adamw(w, m, v, parts, layer=0, prev=None):
    nl, r, c = w.shape
    n = parts.shape[0]
    br = 256 if r % 256 == 0 else r
    blk = pl.BlockSpec((None, br, c), lambda i: (layer, i, 0))
    n_prev = 0 if prev is None else 4

    def body(w_ref, m_ref, v_ref, p_ref, *rest):
        g_ref, d_ref, nm_ref, nv_ref = rest[n_prev:]
        g = p_ref[0].astype(F32)
        for j in range(1, n):
            g = g + p_ref[j].astype(F32)
        nm = ADAM_B1 * m_ref[...] + (1.0 - ADAM_B1) * g
        nv = ADAM_B2 * v_ref[...] + (1.0 - ADAM_B2) * (g * g)
        m_hat = nm / (1.0 - ADAM_B1 ** ADAM_STEP)
        v_hat = nv / (1.0 - ADAM_B2 ** ADAM_STEP)
        g_ref[...] = g
        d_ref[...] = -ADAM_LR * (m_hat / (jnp.sqrt(v_hat) + ADAM_EPS) + ADAM_WD * w_ref[...])
        nm_ref[...] = nm
        nv_ref[...] = nv

    return pl.pallas_call(
        functools.partial(body), name="adamw", grid=(r // br,),
        in_specs=[blk] * 3 + [pl.BlockSpec((n, br, c), lambda i: (0, i, 0))] + [ANY] * n_prev,
        out_specs=[blk] * 4, out_shape=[jax.ShapeDtypeStruct((nl, r, c), F32)] * 4,
        input_output_aliases={4 + k: k for k in range(n_prev)},
        compiler_params=_cparams(("parallel",)),
    )(w, m, v, parts, *(prev or ()))


def _pair_heads(a, axis, width=HEAD_DIM):
    shp = a.shape
    a = a.reshape(shp[:axis] + (2, 2, GQA, width) + shp[axis + 1:])
    return jnp.swapaxes(a, axis + 1, axis + 2).reshape(shp)


def _unpair_heads(a, axis, width=HEAD_DIM):
    shp = a.shape
    a = a.reshape(shp[:axis] + (2, GQA, 2, width) + shp[axis + 1:])
    return jnp.swapaxes(a, axis + 1, axis + 2).reshape(shp)


def _pad_rows(a, rows=8):
    return jnp.pad(a, ((0, rows - a.shape[0]), (0, 0)))


def kernel(x, p, mix_pre_g, mix_post_g, ffn_pre_g, ffn_post_g, pool_w, pool_scale, kv_norm_g, w_k, w_v, w_q, w_o, sinks, w_ff_gate, w_ff_up, w_ff_down, ple_norm_g, w_ple_gate, w_ple_proj, loss_target, m_mix_pre_g, m_mix_post_g, m_ffn_pre_g, m_ffn_post_g, m_pool_w, m_pool_scale, m_kv_norm_g, m_w_k, m_w_v, m_w_q, m_w_o, m_sinks, m_w_ff_gate, m_w_ff_up, m_w_ff_down, m_ple_norm_g, m_w_ple_gate, m_w_ple_proj, v_mix_pre_g, v_mix_post_g, v_ffn_pre_g, v_ffn_post_g, v_pool_w, v_pool_scale, v_kv_norm_g, v_w_k, v_w_v, v_w_q, v_w_o, v_sinks, v_w_ff_gate, v_w_ff_up, v_w_ff_down, v_ple_norm_g, v_w_ple_gate, v_w_ple_proj):
    depth = w_ff_gate.shape[0]
    n_a = pool_w.shape[0]
    t, d = x.shape[1], x.shape[2]
    h = x[0]
    tgt = loss_target[0]
    p_all = p.reshape(depth * t, p.shape[-1])
    my_block = _block_index(*_my_place())
    row = lambda g, i: g[i][None, :]
    bf = lambda a: a.astype(BF16)

    full, gathers = [None] * depth, {}
    start_tokens = jnp.zeros((), F32)
    for i in range(depth):
        shards = [bf(w_ff_gate[i].T), bf(w_ff_up[i].T), bf(w_ff_down[i]), bf(w_ple_gate[i]), bf(w_ple_proj[i].T)]
        if i == 0:
            pool0, scale_full = allgather_pieces([bf(pool_w[0].reshape(-1, POOL_GROUP)), _pad_rows(pool_scale)],
                                                 "allgather_pool0")
            order = pool0
        elif i < n_a:
            shards.append(bf(pool_w[i].reshape(-1, POOL_GROUP)))
        else:
            shards += [bf(_pair_heads(w_q[i - n_a], 1)), bf(w_o[i - n_a])]
            if i == n_a:
                shards.append(bf(jnp.concatenate([w_k, w_v], axis=1)))
        gathers[i] = exchange_start(shards, [_gather_zone(s) for s in shards], order, False, f"allgather_start_l{i}")
        order = gathers[i]["token"]
        start_tokens = start_tokens + order[0, 0]
    scale_full = scale_full.reshape(N_DEV, 8, -1)[:, :n_a].transpose(1, 0, 2).reshape(n_a, 1, d)

    cos, sin = _rope_tables(t, start_tokens)
    sink_b = [jnp.broadcast_to(_pair_heads(sinks[j][:, None], 0, 1), (N_HEADS, LANES)) for j in range(depth - n_a)]
    pool_full, wo_full = {}, {}

    saved = []
    kv = hk = None
    pre = {}
    for i in range(depth):
        if i > 0 and full[i] is None:
            full[i] = exchange_wait(gathers[i], h, f"allgather_wait_l{i}")
        s = {"h0": h}
        if i < n_a:
            pool_full[i] = ((pool0 if i == 0 else full[i][5]).reshape(N_DEV, len(POOL_WINDOWS), -1, POOL_GROUP)
                            .transpose(1, 0, 2, 3).reshape(len(POOL_WINDOWS), POOL_GROUP, POOL_GROUP))
            gpre = row(mix_pre_g, i) + start_tokens if i == 0 else row(mix_pre_g, i)
            h1, a = pool_mix_fwd(h, gpre, pool_full[i], scale_full[i], row(mix_post_g, i), row(ffn_pre_g, i))
            if i == 0:
                full[0] = exchange_wait(gathers[0], (h1, cos, sin), "allgather_wait_l0")
        else:
            j = i - n_a
            wo_full[i] = _pair_heads(full[i][6], 0)
            if i == n_a:
                hk, kv = pre[i][2:]
            hn, q = pre[i][:2]
            attn, m, h1, a = swa_fwd(q, kv, sink_b[j], wo_full[i], h, row(mix_post_g, i), row(ffn_pre_g, i))
            s.update(hn=hn, q=q, attn=attn, m=m)
        wg_t, wu_t, wd, wpg, wpp_t = full[i][:5]
        f, gte, up, hdn = ffn_fwd(a, wg_t, wu_t, wd)
        s.update(h1=h1, a=a, f=f, gte=gte, up=up, hdn=hdn)
        if i < depth - 1:
            proj = []
            if i + 1 >= n_a:
                full[i + 1] = exchange_wait(gathers[i + 1], f, f"allgather_wait_l{i + 1}")
                proj = [(row(mix_pre_g, i + 1), full[i + 1][5], d)]
                if i + 1 == n_a:
                    proj.append((kv_norm_g[None, :], full[i + 1][7], N_KV_HEADS * HEAD_DIM))
            h, *pre[i + 1] = post_ple_fwd(h1, f, p_all, i, row(ffn_post_g, i), row(ple_norm_g, i), wpg, wpp_t,
                                          cos, sin, proj)
        saved.append(s)

    g_mix_pre, g_mix_post, g_ffn_pre, g_ffn_post, g_ple = ([None] * depth for _ in range(5))
    g_kv = g_sinks = None
    g_scale = [None] * n_a
    landing, scatters = [None] * depth, {}
    dkv_sum = []
    scatter_token = jnp.zeros((), F32)
    for i in reversed(range(depth)):
        s = saved[i]
        wg_t, wu_t, wd, wpg, wpp_t = full[i][:5]
        last = i == depth - 1
        dh2, df, ub, dzb, dppb, gacc = post_ple_bwd(tgt if last else dh, s["h1"], s["f"], p_all, i,
                                                    row(ffn_post_g, i) + scatter_token, row(ple_norm_g, i), wpg, wpp_t,
                                                    from_target=last)
        g_ple[i], g_ffn_post[i] = gacc[0], gacc[1]
        if last:
            loss_row = gacc[2][None, :]
        da, dgte, dup = ffn_bwd_act(df, s["gte"], s["up"], wg_t, wu_t, wd)
        grads = [xty(dgte, s["a"]), xty(dup, s["a"]), xty(s["hdn"], df), xty(ub, dzb), xty(dppb, p_all, i)]
        early = exchange_start(grads, [_scatter_zone(g) for g in grads], dh2, True, f"reduce_scatter_start_l{i}a")
        early_token = early["token"][0, 0]
        if i < n_a:
            dh, dpw, gacc = pool_mix_bwd(s["h0"], dh2, da, row(mix_pre_g, i) + early_token, pool_full[i], scale_full[i],
                                         row(mix_post_g, i), row(ffn_pre_g, i))
            g_mix_pre[i], g_mix_post[i], g_ffn_pre[i], g_scale[i] = gacc[0], gacc[1], gacc[2], gacc[3]
            dpw = dpw.reshape(len(POOL_WINDOWS), N_DEV, -1, POOL_GROUP).transpose(1, 0, 2, 3)
            grads = [bf(dpw.reshape(-1, POOL_GROUP))]
        else:
            j = i - n_a
            dh1, dmb, dattn, gacc = oproj_post_bwd(dh2, da, s["h1"], s["m"], wo_full[i], row(mix_post_g, i) + early_token,
                                                   row(ffn_pre_g, i))
            g_mix_post[i], g_ffn_pre[i] = gacc[0], gacc[1]
            dq, dkv, dsink = swa_bwd(s["q"], kv, dattn, sink_b[j])
            dkv_sum.append(dkv)
            g_sinks = [_unpair_heads(dsink[:, 0:1], 0, 1)[:, 0]] + (g_sinks or [])
            branches = [(row(mix_pre_g, i), full[i][5], d, [dq])]
            if i == n_a:
                branches.append((kv_norm_g[None, :], full[i][7], N_KV_HEADS * HEAD_DIM, dkv_sum))
            outs = proj_rope_bwd(dh1, s["h0"], cos, sin, branches, f"proj_bwd_l{i}")
            dh, gacc = outs[0], outs[-1]
            g_mix_pre[i] = gacc[0]
            grads = [xty(s["hn"], outs[1]), _unpair_heads(xty(s["attn"], dmb), 0)]
            if i == n_a:
                g_kv = gacc[1]
                grads.append(xty(hk, outs[2]))
        late = exchange_start(grads, [_scatter_zone(g) for g in grads], dh, True, f"reduce_scatter_start_l{i}b")
        scatter_token = late["token"][0, 0]
        scatters[i] = (early, late)
    grad_x = dh[None]

    def scatter_wait(i, after):
        return (exchange_wait(scatters[i][0], after, f"reduce_scatter_wait_l{i}a")
                + exchange_wait(scatters[i][1], after, f"reduce_scatter_wait_l{i}b"))

    after = dh
    for i in reversed(range(1, depth)):
        landing[i] = scatter_wait(i, after)
        after = landing[i][0]

    sink_row = jnp.pad(jnp.concatenate(g_sinks)[None, :], ((0, 0), (0, d - sinks.size)))
    stack = lambda rows_: _pad_rows(jnp.stack(rows_))
    pack = jnp.concatenate([stack(g_mix_pre), stack(g_mix_post), stack(g_ffn_pre), stack(g_ffn_post), stack(g_ple),
                            _pad_rows(g_kv[None]), stack(g_scale), _pad_rows(sink_row), _pad_rows(loss_row)], axis=0)
    tot = allreduce_small(pack)
    sec = lambda k, n: tot[8 * k:8 * k + n]
    loss = jnp.sum(tot[64])
    small = {
        "mix_pre_g": sec(0, depth), "mix_post_g": sec(1, depth), "ffn_pre_g": sec(2, depth),
        "ffn_post_g": sec(3, depth), "ple_norm_g": sec(4, depth), "kv_norm_g": tot[40],
        "pool_scale": lax.dynamic_slice_in_dim(sec(6, n_a), my_block * pool_scale.shape[1], pool_scale.shape[1], axis=1),
        "sinks": tot[56, :sinks.size].reshape(sinks.shape),
    }

    weights = dict(mix_pre_g=mix_pre_g, mix_post_g=mix_post_g, ffn_pre_g=ffn_pre_g, ffn_post_g=ffn_post_g, pool_w=pool_w, pool_scale=pool_scale, kv_norm_g=kv_norm_g, w_k=w_k, w_v=w_v, w_q=w_q, w_o=w_o, sinks=sinks, w_ff_gate=w_ff_gate, w_ff_up=w_ff_up, w_ff_down=w_ff_down, ple_norm_g=ple_norm_g, w_ple_gate=w_ple_gate, w_ple_proj=w_ple_proj)
    mom1 = dict(mix_pre_g=m_mix_pre_g, mix_post_g=m_mix_post_g, ffn_pre_g=m_ffn_pre_g, ffn_post_g=m_ffn_post_g, pool_w=m_pool_w, pool_scale=m_pool_scale, kv_norm_g=m_kv_norm_g, w_k=m_w_k, w_v=m_w_v, w_q=m_w_q, w_o=m_w_o, sinks=m_sinks, w_ff_gate=m_w_ff_gate, w_ff_up=m_w_ff_up, w_ff_down=m_w_ff_down, ple_norm_g=m_ple_norm_g, w_ple_gate=m_w_ple_gate, w_ple_proj=m_w_ple_proj)
    mom2 = dict(mix_pre_g=v_mix_pre_g, mix_post_g=v_mix_post_g, ffn_pre_g=v_ffn_pre_g, ffn_post_g=v_ffn_post_g, pool_w=v_pool_w, pool_scale=v_pool_scale, kv_norm_g=v_kv_norm_g, w_k=v_w_k, w_v=v_w_v, w_q=v_w_q, w_o=v_w_o, sinks=v_sinks, w_ff_gate=v_w_ff_gate, w_ff_up=v_w_ff_up, w_ff_down=v_w_ff_down, ple_norm_g=v_ple_norm_g, w_ple_gate=v_w_ple_gate, w_ple_proj=v_w_ple_proj)

    swap = lambda a: jnp.swapaxes(a, 1, 2)
    same = lambda a: a
    att = range(n_a, depth)
    for ws in (weights, mom1, mom2):
        ws["w_kv"] = jnp.concatenate([ws["w_k"], ws["w_v"]], axis=1)[None]
    plan = {
        "w_ff_gate": (swap, swap, [(i, i, 0) for i in range(depth)]),
        "w_ff_up": (swap, swap, [(i, i, 1) for i in range(depth)]),
        "w_ff_down": (same, same, [(i, i, 2) for i in range(depth)]),
        "w_ple_gate": (same, same, [(i, i, 3) for i in range(depth)]),
        "w_ple_proj": (swap, swap, [(i, i, 4) for i in range(depth)]),
        "pool_w": (lambda a: a.reshape(n_a, -1, POOL_GROUP), lambda a: a.reshape(pool_w.shape),
                   [(i, i, 5) for i in range(n_a)]),
        "w_q": (lambda a: _pair_heads(a, 2), lambda a: _unpair_heads(a, 2), [(i - n_a, i, 5) for i in att]),
        "w_o": (same, same, [(i - n_a, i, 6) for i in att]),
        "w_kv": (same, same, [(0, n_a, 7)]),
    }
    chains = {}

    def update_layer(layer):
        for nme, (view, _, where) in plan.items():
            for idx, li, k in where:
                if li == layer:
                    chains[nme] = adamw(view(weights[nme]), view(mom1[nme]), view(mom2[nme]), landing[li][k], idx,
                                        chains.get(nme))

    for layer in reversed(range(1, depth)):
        update_layer(layer)
    landing[0] = scatter_wait(0, chains["w_ff_down"][0] if depth > 1 else after)
    update_layer(0)
    results = {nme: [plan[nme][1](o) for o in outs] for nme, outs in chains.items()}
    results["w_k"] = [o[0, :, :w_k.shape[1]] for o in results["w_kv"]]
    results["w_v"] = [o[0, :, w_k.shape[1]:] for o in results["w_kv"]]
    for nme, g in small.items():
        w = weights[nme]
        flat = lambda a: a.reshape((1, -1, a.shape[-1]))
        results[nme] = [o.reshape(w.shape) for o in adamw(flat(w), flat(mom1[nme]), flat(mom2[nme]), flat(g))]

    order = ["mix_pre_g", "mix_post_g", "ffn_pre_g", "ffn_post_g", "pool_w", "pool_scale", "kv_norm_g", "w_k", "w_v",
             "w_q", "w_o", "sinks", "w_ff_gate", "w_ff_up", "w_ff_down", "ple_norm_g", "w_ple_gate", "w_ple_proj"]
    g_out, d_out, m_out, v_out = ([results[nme][k] for nme in order] for k in range(4))
    return (loss, grad_x, *g_out, *d_out, *m_out, *v_out)
```
